```python
import jax, jax.numpy as jnp
from jax import lax
import numpy as np

D_MODEL = 2048
BATCH = 8
SEQ = 8192
DEPTH = 1

HEAD_DIM = 128
ATTN_WIDTH = D_MODEL // 2
ATTN_HEADS = ATTN_WIDTH // HEAD_DIM
POOL_WIDTH = D_MODEL // 2
POOL_WINDOWS = (2, 4, 8, 16)
POOL_GROUPS = len(POOL_WINDOWS)
POOL_GROUP_WIDTH = POOL_WIDTH // POOL_GROUPS
DILATED_PATTERNS = ((128, 1), (512, 4), (2048, 16))
SUB_BLOCK = 128
D_FF = 4 * D_MODEL
ROPE_THETA = 10000.0
LN_EPS = 1e-5
DEEPNORM_ALPHA = (2.0 * DEPTH) ** 0.25
DEEPNORM_BETA = (8.0 * DEPTH) ** -0.25
IN_SPLITS = (ATTN_WIDTH, 2 * ATTN_WIDTH, 3 * ATTN_WIDTH, 3 * ATTN_WIDTH + POOL_WIDTH,
             3 * ATTN_WIDTH + POOL_WIDTH + D_MODEL)
IN_WIDTH = 3 * ATTN_WIDTH + POOL_WIDTH + 2 * D_MODEL

kernel_name = "dilated_attn_pool_gated_hybrid_deepnorm"


def layer_norm(x, g, b):
    xf = x.astype(jnp.float32)
    mu = jnp.mean(xf, axis=-1, keepdims=True)
    var = jnp.mean(jnp.square(xf - mu), axis=-1, keepdims=True)
    return ((xf - mu) * lax.rsqrt(var + LN_EPS) * g.astype(jnp.float32) + b.astype(jnp.float32)).astype(x.dtype)


def rope(t, positions):
    half = HEAD_DIM // 2
    inv_freq = ROPE_THETA ** (-jnp.arange(half, dtype=jnp.float32) / half)
    ang = positions.astype(jnp.float32)[..., None] * inv_freq
    cos = jnp.cos(ang)[:, :, None, :]
    sin = jnp.sin(ang)[:, :, None, :]
    t1 = t[..., :half].astype(jnp.float32)
    t2 = t[..., half:].astype(jnp.float32)
    return jnp.concatenate([t1 * cos - t2 * sin, t2 * cos + t1 * sin], axis=-1).astype(t.dtype)


def dilated_window_attention(q, k, v, window, dilation):
    B, S, H, Dh = q.shape
    span = window // dilation
    blk = SUB_BLOCK
    assert span <= blk
    unit = dilation * blk
    s_pad = -(-S // unit) * unit
    m_len = s_pad // dilation
    nb = m_len // blk

    def to_blocks(t):
        t = jnp.pad(t, ((0, 0), (0, s_pad - S), (0, 0), (0, 0)))
        t = t.reshape(B, m_len, dilation, H, Dh).transpose(0, 2, 1, 3, 4)
        return t.reshape(B, dilation, nb, blk, H, Dh)

    def with_prev(t):
        prev = jnp.pad(t, ((0, 0), (0, 0), (1, 0), (0, 0), (0, 0), (0, 0)))[:, :, :-1]
        return jnp.concatenate([prev, t], axis=3)

    qb = to_blocks(q)
    kw = with_prev(to_blocks(k))
    vw = with_prev(to_blocks(v))
    s = jnp.einsum('brnqhd,brnkhd->brnhqk', qb, kw,
                   preferred_element_type=jnp.float32) * (HEAD_DIM ** -0.5)
    qi = jnp.arange(blk)[:, None]
    kj = jnp.arange(2 * blk)[None, :]
    dist = qi + blk - kj
    band = (dist >= 0) & (dist <= span)
    valid = band[None] & ((jnp.arange(nb)[:, None, None] > 0) | (kj >= blk)[None])
    s = jnp.where(valid[None, None, :, None], s, -jnp.inf)
    mx = jnp.max(s, axis=-1, keepdims=True)
    p = jnp.exp(s - mx)
    l = jnp.sum(p, axis=-1)
    o = jnp.einsum('brnhqk,brnkhd->brnqhd', p, vw.astype(jnp.float32))
    o = o / jnp.swapaxes(l, 3, 4)[..., None]
    lse = jnp.swapaxes(mx[..., 0] + jnp.log(l), 3, 4)
    o = o.reshape(B, dilation, m_len, H, Dh).transpose(0, 2, 1, 3, 4).reshape(B, s_pad, H, Dh)[:, :S]
    lse = lse.reshape(B, dilation, m_len, H).transpose(0, 2, 1, 3).reshape(B, s_pad, H)[:, :S]
    return o, lse


def pool_mixer(u, w_pool, pool_scale):
    B, S, _ = u.shape
    ug = u.reshape(B, S, POOL_GROUPS, POOL_GROUP_WIDTH)
    pooled = []
    for g, w in enumerate(POOL_WINDOWS):
        xg = ug[:, :, g].astype(jnp.float32)
        c = jnp.cumsum(xg, axis=1)
        c_lag = jnp.pad(c, ((0, 0), (w, 0), (0, 0)))[:, :S]
        count = jnp.minimum(jnp.arange(1, S + 1), w).astype(jnp.float32)[None, :, None]
        pooled.append((c - c_lag) / count - xg)
    p = jnp.stack(pooled, axis=2).astype(u.dtype)
    y = jnp.einsum('bsgc,gcd->bsgd', p, w_pool).reshape(B, S, POOL_WIDTH)
    return y * pool_scale


def _fwd_setup_inputs(seed: int = 0) -> dict:
    key = jax.random.key(seed)
    ks = jax.random.split(key, 20)
    f32 = jnp.float32

    def nrm(k, shape, fan_in, gain=1.0):
        return jax.random.normal(k, shape, f32) * (gain * fan_in ** -0.5)

    x = jax.random.normal(ks[0], (BATCH, SEQ, D_MODEL), f32)
    positions = (jnp.arange(SEQ, dtype=jnp.int32)[None, :]
                 + jax.random.randint(ks[1], (BATCH, 1), 0, 1024, dtype=jnp.int32))
    w_in = jnp.concatenate([
        nrm(ks[2], (DEPTH, D_MODEL, 2 * ATTN_WIDTH), D_MODEL),
        nrm(ks[3], (DEPTH, D_MODEL, ATTN_WIDTH), D_MODEL, DEEPNORM_BETA),
        nrm(ks[4], (DEPTH, D_MODEL, POOL_WIDTH), D_MODEL),
        nrm(ks[5], (DEPTH, D_MODEL, 2 * D_MODEL), D_MODEL),
    ], axis=-1)
    w_pool = nrm(ks[6], (DEPTH, POOL_GROUPS, POOL_GROUP_WIDTH, POOL_GROUP_WIDTH), POOL_GROUP_WIDTH)
    pool_scale = 1.0 + 0.1 * jax.random.normal(ks[7], (DEPTH, POOL_WIDTH), f32)
    w_branch_attn = nrm(ks[8], (DEPTH, ATTN_WIDTH, D_MODEL), ATTN_WIDTH, DEEPNORM_BETA)
    w_branch_pool = nrm(ks[9], (DEPTH, POOL_WIDTH, D_MODEL), POOL_WIDTH, DEEPNORM_BETA)
    w_out = nrm(ks[10], (DEPTH, D_MODEL, D_MODEL), D_MODEL, DEEPNORM_BETA)
    ln_mix_g = 1.0 + 0.05 * jax.random.normal(ks[11], (DEPTH, D_MODEL), f32)
    ln_mix_b = 0.02 * jax.random.normal(ks[12], (DEPTH, D_MODEL), f32)
    w_ff1 = nrm(ks[13], (DEPTH, D_MODEL, D_FF), D_MODEL, DEEPNORM_BETA)
    w_ff2 = nrm(ks[14], (DEPTH, D_FF, D_MODEL), D_FF, DEEPNORM_BETA)
    ln_ff_g = 1.0 + 0.05 * jax.random.normal(ks[15], (DEPTH, D_MODEL), f32)
    ln_ff_b = 0.02 * jax.random.normal(ks[16], (DEPTH, D_MODEL), f32)
    return {"x": x, "positions": positions, "w_in": w_in, "w_pool": w_pool,
            "pool_scale": pool_scale, "w_branch_attn": w_branch_attn,
            "w_branch_pool": w_branch_pool, "w_out": w_out, "ln_mix_g": ln_mix_g,
            "ln_mix_b": ln_mix_b, "w_ff1": w_ff1, "w_ff2": w_ff2,
            "ln_ff_g": ln_ff_g, "ln_ff_b": ln_ff_b}


def _fwd_reference(x, positions, w_in, w_pool, pool_scale, w_branch_attn, w_branch_pool, w_out,
              ln_mix_g, ln_mix_b, w_ff1, w_ff2, ln_ff_g, ln_ff_b):
    B, S, _ = x.shape
    for layer in range(DEPTH):
        h = x @ w_in[layer]
        q, k, v, u, gate_attn, gate_pool = jnp.split(h, IN_SPLITS, axis=-1)
        q = rope(q.reshape(B, S, ATTN_HEADS, HEAD_DIM), positions)
        k = rope(k.reshape(B, S, ATTN_HEADS, HEAD_DIM), positions)
        v = v.reshape(B, S, ATTN_HEADS, HEAD_DIM)
        outs, lses = [], []
        for window, dilation in DILATED_PATTERNS:
            o_g, lse_g = dilated_window_attention(q, k, v, window, dilation)
            outs.append(o_g)
            lses.append(lse_g)
        mix_w = jax.nn.softmax(jnp.stack(lses, axis=0), axis=0)
        o_attn = jnp.einsum('pbsh,pbshd->bshd', mix_w, jnp.stack(outs, axis=0))
        y_attn = o_attn.reshape(B, S, ATTN_WIDTH).astype(x.dtype) @ w_branch_attn[layer]
        y_pool = pool_mixer(u, w_pool[layer], pool_scale[layer]) @ w_branch_pool[layer]
        merged = jax.nn.sigmoid(gate_attn) * y_attn + jax.nn.sigmoid(gate_pool) * y_pool
        mix = merged @ w_out[layer]
        x = layer_norm(DEEPNORM_ALPHA * x + mix, ln_mix_g[layer], ln_mix_b[layer])
        f = jnp.square(jax.nn.relu(x @ w_ff1[layer])) @ w_ff2[layer]
        x = layer_norm(DEEPNORM_ALPHA * x + f, ln_ff_g[layer], ln_ff_b[layer])
    return x


import jax as _jax
import jax.numpy as _jnp

TWIN_FORMAT = 'train_step'
FWD_PARAMS = ['x', 'positions', 'w_in', 'w_pool', 'pool_scale', 'w_branch_attn', 'w_branch_pool', 'w_out', 'ln_mix_g', 'ln_mix_b', 'w_ff1', 'w_ff2', 'ln_ff_g', 'ln_ff_b']
TWIN_WEIGHTS = ['w_in', 'w_pool', 'pool_scale', 'w_branch_attn', 'w_branch_pool', 'w_out', 'ln_mix_g', 'ln_mix_b', 'w_ff1', 'w_ff2', 'ln_ff_g', 'ln_ff_b']
TWIN_DIFF_INPUT = 'x'
TWIN_INPUTS = ['x', 'positions', 'w_in', 'w_pool', 'pool_scale', 'w_branch_attn', 'w_branch_pool', 'w_out', 'ln_mix_g', 'ln_mix_b', 'w_ff1', 'w_ff2', 'ln_ff_g', 'ln_ff_b', 'loss_target', 'm_w_in', 'm_w_pool', 'm_pool_scale', 'm_w_branch_attn', 'm_w_branch_pool', 'm_w_out', 'm_ln_mix_g', 'm_ln_mix_b', 'm_w_ff1', 'm_w_ff2', 'm_ln_ff_g', 'm_ln_ff_b', 'v_w_in', 'v_w_pool', 'v_pool_scale', 'v_w_branch_attn', 'v_w_branch_pool', 'v_w_out', 'v_ln_mix_g', 'v_ln_mix_b', 'v_w_ff1', 'v_w_ff2', 'v_ln_ff_g', 'v_ln_ff_b']
TWIN_OUTPUTS = ['loss', 'grad_x', 'grad_w_in', 'grad_w_pool', 'grad_pool_scale', 'grad_w_branch_attn', 'grad_w_branch_pool', 'grad_w_out', 'grad_ln_mix_g', 'grad_ln_mix_b', 'grad_w_ff1', 'grad_w_ff2', 'grad_ln_ff_g', 'grad_ln_ff_b', 'delta_w_in', 'delta_w_pool', 'delta_pool_scale', 'delta_w_branch_attn', 'delta_w_branch_pool', 'delta_w_out', 'delta_ln_mix_g', 'delta_ln_mix_b', 'delta_w_ff1', 'delta_w_ff2', 'delta_ln_ff_g', 'delta_ln_ff_b', 'new_m_w_in', 'new_m_w_pool', 'new_m_pool_scale', 'new_m_w_branch_attn', 'new_m_w_branch_pool', 'new_m_w_out', 'new_m_ln_mix_g', 'new_m_ln_mix_b', 'new_m_w_ff1', 'new_m_w_ff2', 'new_m_ln_ff_g', 'new_m_ln_ff_b', 'new_v_w_in', 'new_v_w_pool', 'new_v_pool_scale', 'new_v_w_branch_attn', 'new_v_w_branch_pool', 'new_v_w_out', 'new_v_ln_mix_g', 'new_v_ln_mix_b', 'new_v_w_ff1', 'new_v_w_ff2', 'new_v_ln_ff_g', 'new_v_ln_ff_b']
TWIN_LEAF_KINDS = {'loss': 'loss', 'grad_x': 'grad_x', 'grad_w_in': 'grad_w', 'grad_w_pool': 'grad_w', 'grad_pool_scale': 'grad_w', 'grad_w_branch_attn': 'grad_w', 'grad_w_branch_pool': 'grad_w', 'grad_w_out': 'grad_w', 'grad_ln_mix_g': 'grad_w', 'grad_ln_mix_b': 'grad_w', 'grad_w_ff1': 'grad_w', 'grad_w_ff2': 'grad_w', 'grad_ln_ff_g': 'grad_w', 'grad_ln_ff_b': 'grad_w', 'delta_w_in': 'delta_w', 'delta_w_pool': 'delta_w', 'delta_pool_scale': 'delta_w', 'delta_w_branch_attn': 'delta_w', 'delta_w_branch_pool': 'delta_w', 'delta_w_out': 'delta_w', 'delta_ln_mix_g': 'delta_w', 'delta_ln_mix_b': 'delta_w', 'delta_w_ff1': 'delta_w', 'delta_w_ff2': 'delta_w', 'delta_ln_ff_g': 'delta_w', 'delta_ln_ff_b': 'delta_w', 'new_m_w_in': 'new_m', 'new_m_w_pool': 'new_m', 'new_m_pool_scale': 'new_m', 'new_m_w_branch_attn': 'new_m', 'new_m_w_branch_pool': 'new_m', 'new_m_w_out': 'new_m', 'new_m_ln_mix_g': 'new_m', 'new_m_ln_mix_b': 'new_m', 'new_m_w_ff1': 'new_m', 'new_m_w_ff2': 'new_m', 'new_m_ln_ff_g': 'new_m', 'new_m_ln_ff_b': 'new_m', 'new_v_w_in': 'new_v', 'new_v_w_pool': 'new_v', 'new_v_pool_scale': 'new_v', 'new_v_w_branch_attn': 'new_v', 'new_v_w_branch_pool': 'new_v', 'new_v_w_out': 'new_v', 'new_v_ln_mix_g': 'new_v', 'new_v_ln_mix_b': 'new_v', 'new_v_w_ff1': 'new_v', 'new_v_w_ff2': 'new_v', 'new_v_ln_ff_g': 'new_v', 'new_v_ln_ff_b': 'new_v'}


def _forward(args):
    return _fwd_reference(*[args[k] for k in FWD_PARAMS])


def _output_shape():
    def fwd():
        inp = _fwd_setup_inputs(0)
        return _fwd_reference(*[inp[k] for k in FWD_PARAMS])
    out = _jax.eval_shape(fwd)
    return out.shape, out.dtype

N_MICROBATCH = 1
ADAM_LR = 0.001
ADAM_B1 = 0.9
ADAM_B2 = 0.999
ADAM_EPS = 1e-08
ADAM_WD = 0.01
ADAM_STEP = 10
PER_EXAMPLE_BATCH_AXIS = {'x': 0, 'positions': 0, 'loss_target': 0}
SHARED_INPUTS = []
_WEIGHT_DTYPES = {'w_in': _jnp.float32, 'w_pool': _jnp.float32, 'pool_scale': _jnp.float32, 'w_branch_attn': _jnp.float32, 'w_branch_pool': _jnp.float32, 'w_out': _jnp.float32, 'ln_mix_g': _jnp.float32, 'ln_mix_b': _jnp.float32, 'w_ff1': _jnp.float32, 'w_ff2': _jnp.float32, 'ln_ff_g': _jnp.float32, 'ln_ff_b': _jnp.float32}
MOMENT_SCALE = {'w_in': 9.911667e-03, 'w_pool': 2.598732e-02, 'pool_scale': 2.735046e-02, 'w_branch_attn': 4.696953e-03, 'w_branch_pool': 3.082945e-02, 'w_out': 3.053631e-02, 'ln_mix_g': 3.014311e+00, 'ln_mix_b': 5.519813e-01, 'w_ff1': 2.622404e-02, 'w_ff2': 7.609299e-02, 'ln_ff_g': 3.209357e+01, 'ln_ff_b': 2.744715e+00}


def _to_microbatches(a, axis):
    t = _jnp.moveaxis(a, axis, 0)
    t = t.reshape((N_MICROBATCH, t.shape[0] // N_MICROBATCH) + t.shape[1:])
    return _jnp.moveaxis(t, 1, axis + 1)


def setup_inputs(seed: int = 0) -> dict:
    inp = _fwd_setup_inputs(seed)
    key = _jax.random.fold_in(_jax.random.key(seed), 7919)
    shape, _ = _output_shape()
    out = dict(inp)
    out["loss_target"] = _jax.random.normal(_jax.random.fold_in(key, 0), shape, _jnp.float32)
    for i, name in enumerate(TWIN_WEIGHTS):
        w = inp[name].astype(_jnp.float32)
        if MOMENT_SCALE is None:
            s = _jnp.sqrt(_jnp.mean(_jnp.square(w)) + 1e-30)
        else:
            s = MOMENT_SCALE[name]
        km, kv = _jax.random.split(_jax.random.fold_in(key, i + 1))
        out[name] = w
        out["m_" + name] = s * _jax.random.normal(km, w.shape, _jnp.float32)
        out["v_" + name] = (s * s) * _jax.random.uniform(kv, w.shape, _jnp.float32, 0.5, 1.5)
    if N_MICROBATCH > 1:
        for name, axis in PER_EXAMPLE_BATCH_AXIS.items():
            out[name] = _to_microbatches(out[name], axis)
    return {'x': out['x'], 'positions': out['positions'], 'w_in': out['w_in'], 'w_pool': out['w_pool'], 'pool_scale': out['pool_scale'], 'w_branch_attn': out['w_branch_attn'], 'w_branch_pool': out['w_branch_pool'], 'w_out': out['w_out'], 'ln_mix_g': out['ln_mix_g'], 'ln_mix_b': out['ln_mix_b'], 'w_ff1': out['w_ff1'], 'w_ff2': out['w_ff2'], 'ln_ff_g': out['ln_ff_g'], 'ln_ff_b': out['ln_ff_b'], 'loss_target': out['loss_target'], 'm_w_in': out['m_w_in'], 'm_w_pool': out['m_w_pool'], 'm_pool_scale': out['m_pool_scale'], 'm_w_branch_attn': out['m_w_branch_attn'], 'm_w_branch_pool': out['m_w_branch_pool'], 'm_w_out': out['m_w_out'], 'm_ln_mix_g': out['m_ln_mix_g'], 'm_ln_mix_b': out['m_ln_mix_b'], 'm_w_ff1': out['m_w_ff1'], 'm_w_ff2': out['m_w_ff2'], 'm_ln_ff_g': out['m_ln_ff_g'], 'm_ln_ff_b': out['m_ln_ff_b'], 'v_w_in': out['v_w_in'], 'v_w_pool': out['v_w_pool'], 'v_pool_scale': out['v_pool_scale'], 'v_w_branch_attn': out['v_w_branch_attn'], 'v_w_branch_pool': out['v_w_branch_pool'], 'v_w_out': out['v_w_out'], 'v_ln_mix_g': out['v_ln_mix_g'], 'v_ln_mix_b': out['v_ln_mix_b'], 'v_w_ff1': out['v_w_ff1'], 'v_w_ff2': out['v_w_ff2'], 'v_ln_ff_g': out['v_ln_ff_g'], 'v_ln_ff_b': out['v_ln_ff_b']}


def _loss(weights, diff, rest, loss_target):
    with _jax.named_scope("forward"):
        args = {**rest, TWIN_DIFF_INPUT: diff, **{k: w.astype(_WEIGHT_DTYPES[k]) for k, w in weights.items()}}
        y = _forward(args)
    with _jax.named_scope("loss_head"):
        err = _jnp.square(y.astype(_jnp.float32) - loss_target)
        return 0.5 * _jnp.sum(_jnp.mean(err, axis=-1)) if err.ndim else 0.5 * err


def _adamw(w, g, m, v):
    m = ADAM_B1 * m + (1.0 - ADAM_B1) * g
    v = ADAM_B2 * v + (1.0 - ADAM_B2) * _jnp.square(g)
    m_hat = m / (1.0 - ADAM_B1 ** ADAM_STEP)
    v_hat = v / (1.0 - ADAM_B2 ** ADAM_STEP)
    delta = -ADAM_LR * (m_hat / (_jnp.sqrt(v_hat) + ADAM_EPS) + ADAM_WD * w)
    return delta, m, v


def reference(x, positions, w_in, w_pool, pool_scale, w_branch_attn, w_branch_pool, w_out, ln_mix_g, ln_mix_b, w_ff1, w_ff2, ln_ff_g, ln_ff_b, loss_target, m_w_in, m_w_pool, m_pool_scale, m_w_branch_attn, m_w_branch_pool, m_w_out, m_ln_mix_g, m_ln_mix_b, m_w_ff1, m_w_ff2, m_ln_ff_g, m_ln_ff_b, v_w_in, v_w_pool, v_pool_scale, v_w_branch_attn, v_w_branch_pool, v_w_out, v_ln_mix_g, v_ln_mix_b, v_w_ff1, v_w_ff2, v_ln_ff_g, v_ln_ff_b):
    given = dict(x=x, positions=positions, w_in=w_in, w_pool=w_pool, pool_scale=pool_scale, w_branch_attn=w_branch_attn, w_branch_pool=w_branch_pool, w_out=w_out, ln_mix_g=ln_mix_g, ln_mix_b=ln_mix_b, w_ff1=w_ff1, w_ff2=w_ff2, ln_ff_g=ln_ff_g, ln_ff_b=ln_ff_b, loss_target=loss_target, m_w_in=m_w_in, m_w_pool=m_w_pool, m_pool_scale=m_pool_scale, m_w_branch_attn=m_w_branch_attn, m_w_branch_pool=m_w_branch_pool, m_w_out=m_w_out, m_ln_mix_g=m_ln_mix_g, m_ln_mix_b=m_ln_mix_b, m_w_ff1=m_w_ff1, m_w_ff2=m_w_ff2, m_ln_ff_g=m_ln_ff_g, m_ln_ff_b=m_ln_ff_b, v_w_in=v_w_in, v_w_pool=v_w_pool, v_pool_scale=v_pool_scale, v_w_branch_attn=v_w_branch_attn, v_w_branch_pool=v_w_branch_pool, v_w_out=v_w_out, v_ln_mix_g=v_ln_mix_g, v_ln_mix_b=v_ln_mix_b, v_w_ff1=v_w_ff1, v_w_ff2=v_w_ff2, v_ln_ff_g=v_ln_ff_g, v_ln_ff_b=v_ln_ff_b)
    weights = {n: given[n] for n in TWIN_WEIGHTS}
    shared = {n: given[n] for n in SHARED_INPUTS}
    per_example = {n: given[n] for n in ['x', 'positions']}
    grad_fn = _jax.value_and_grad(_loss, argnums=(0, 1))

    def one_microbatch(ex, loss_target):
        ex = dict(ex)
        diff = ex.pop(TWIN_DIFF_INPUT)
        return grad_fn(weights, diff, {**shared, **ex}, loss_target)

    if N_MICROBATCH == 1:
        loss, (grad_w, grad_x) = one_microbatch(per_example, given["loss_target"])
    else:
        def body(carry, xs):
            loss_sum, grad_sum = carry
            l_k, (gw_k, gx_k) = one_microbatch(xs[0], xs[1])
            with _jax.named_scope("update"):
                return (loss_sum + l_k, _jax.tree.map(_jnp.add, grad_sum, gw_k)), gx_k

        init = (_jnp.zeros((), _jnp.float32), _jax.tree.map(_jnp.zeros_like, weights))
        (loss, grad_w), grad_x = _jax.lax.scan(body, init, (per_example, given["loss_target"]))
    with _jax.named_scope("update"):
        delta_w, new_m, new_v = {}, {}, {}
        for n in TWIN_WEIGHTS:
            delta_w[n], new_m[n], new_v[n] = _adamw(weights[n], grad_w[n], given["m_" + n], given["v_" + n])
    return (loss, grad_x, *[grad_w[n] for n in TWIN_WEIGHTS], *[delta_w[n] for n in TWIN_WEIGHTS],
            *[new_m[n] for n in TWIN_WEIGHTS], *[new_v[n] for n in TWIN_WEIGHTS])
```

```python
import jax
import jax.numpy as jnp
from jax import lax
from jax.experimental import pallas as pl
from jax.experimental.pallas import tpu as pltpu

F32 = jnp.float32
BF16 = jnp.bfloat16
MESH = pl.DeviceIdType.MESH

HEAD_DIM = 128
SUB_BLOCK = 128
DILATIONS = (1, 4, 16)
POOL_WINDOWS = (2, 4, 8, 16)
POOL_HALO = 16
ROPE_THETA = 10000.0
LN_EPS = 1e-5
ALPHA = 2.0 ** 0.25
ADAM_LR, ADAM_B1, ADAM_B2, ADAM_EPS, ADAM_WD, ADAM_STEP = 0.001, 0.9, 0.999, 1e-08, 0.01, 10
NEG = -1e30
N_CHIPS = 4
VMEM_LIMIT = 56 * 1024 * 1024


def _params(sem=None, vmem=VMEM_LIMIT):
    kw = {"vmem_limit_bytes": vmem}
    if sem is not None:
        kw["dimension_semantics"] = sem
    return pltpu.CompilerParams(**kw)


def _dot(a, b, contract):
    return lax.dot_general(a, b, (contract, ((), ())), preferred_element_type=F32)


ANY_SPEC = pl.BlockSpec(memory_space=pl.ANY)
NN = ((1,), (0,))
NT = ((1,), (1,))
TN = ((0,), (0,))


def _mm(name, grid, a, a_spec, b, b_spec, contract, extras, extra_specs, out_shape, out_specs,
        epilogue, acc_shape):
    nk = grid[2]
    n_ex = len(extras)
    n_out = len(out_shape)

    def body(*refs):
        a_ref, b_ref = refs[0], refs[1]
        ex = refs[2:2 + n_ex]
        outs = refs[2 + n_ex:2 + n_ex + n_out]
        part = _dot(a_ref[...], b_ref[...], contract)
        if nk == 1:
            epilogue(part, ex, outs)
        else:
            acc = refs[2 + n_ex + n_out]
            k = pl.program_id(2)

            @pl.when(k == 0)
            def _():
                acc[...] = part

            @pl.when(k > 0)
            def _():
                acc[...] += part

            @pl.when(k == nk - 1)
            def _():
                epilogue(acc[...], ex, outs)

    scratch = [pltpu.VMEM(acc_shape, F32)] if nk > 1 else []
    return pl.pallas_call(
        body, name=name, grid=grid, in_specs=[a_spec, b_spec, *extra_specs], out_specs=out_specs,
        out_shape=out_shape, scratch_shapes=scratch,
        compiler_params=_params(("parallel", "parallel", "arbitrary")),
    )(a, b, *extras)


def _stats_rows(rows, width):
    idx = lax.broadcasted_iota(jnp.int32, (8, width), 0)
    out = jnp.zeros((8, width), F32)
    for r, v in enumerate(rows):
        out = jnp.where(idx == r, jnp.broadcast_to(v, (8, width)), out)
    return out


def _layer_norm_fwd(z):
    mu = jnp.mean(z, axis=-1, keepdims=True)
    zc = z - mu
    var = jnp.mean(zc * zc, axis=-1, keepdims=True)
    rstd = lax.rsqrt(var + LN_EPS)
    return zc * rstd, rstd


def _layer_norm_bwd(dy, xhat, rstd, g):
    dxh = dy * g
    m1 = jnp.mean(dxh, axis=-1, keepdims=True)
    m2 = jnp.mean(dxh * xhat, axis=-1, keepdims=True)
    return rstd * (dxh - m1 - xhat * m2)


def _in_proj(xb, w_in_st, cos_t, sin_t, aw):
    s, d = xb.shape
    n_sh = w_in_st.shape[2]
    tm, tn = min(s, 1024), aw
    per = n_sh // tn
    grid = (s // tm, (N_CHIPS * n_sh) // tn, 1)

    def epilogue(acc, ex, outs):
        cos_ref, sin_ref = ex
        (h_ref,) = outs
        seg = pl.program_id(1)

        @pl.when(seg < 2)
        def _():
            cos, sin = cos_ref[...], sin_ref[...]
            for hh in range(tn // HEAD_DIM):
                sl = slice(hh * HEAD_DIM, (hh + 1) * HEAD_DIM)
                t = acc[:, sl]
                h_ref[:, sl] = (t * cos + pltpu.roll(t, HEAD_DIM // 2, 1) * sin).astype(BF16)

        @pl.when((seg >= 2) & (seg < 4))
        def _():
            h_ref[...] = acc.astype(BF16)

        @pl.when(seg >= 4)
        def _():
            h_ref[...] = jax.nn.sigmoid(acc).astype(BF16)

    return _mm(
        "in_proj", grid, xb, pl.BlockSpec((tm, d), lambda i, j, k: (i, 0)),
        w_in_st, pl.BlockSpec((None, d, tn), lambda i, j, k: (j // per, 0, j % per)), NN,
        [cos_t, sin_t], [pl.BlockSpec((tm, HEAD_DIM), lambda i, j, k: (i, 0))] * 2,
        [jax.ShapeDtypeStruct((s, N_CHIPS * n_sh), BF16)],
        [pl.BlockSpec((tm, tn), lambda i, j, k: (i, j))], epilogue, None)[0]


def _band_masks(block_idx):
    qi = lax.broadcasted_iota(jnp.int32, (SUB_BLOCK, 2 * SUB_BLOCK), 0)
    kj = lax.broadcasted_iota(jnp.int32, (SUB_BLOCK, 2 * SUB_BLOCK), 1)
    first_key = jnp.where(block_idx > 0, 0, SUB_BLOCK)
    return (kj >= qi) & (kj <= qi + SUB_BLOCK) & (kj >= first_key)


def _attn_fwd(name, q, k, v, offs, cw, width):
    m = q.shape[0]
    nh = cw // HEAD_DIM
    grid = (m // SUB_BLOCK, width // cw)
    scale = HEAD_DIM ** -0.5

    def body(q_ref, kp_ref, kc_ref, vp_ref, vc_ref, o_ref, lse_ref):
        valid = _band_masks(pl.program_id(0))
        for hh in range(nh):
            sl = slice(hh * HEAD_DIM, (hh + 1) * HEAD_DIM)
            kcat = jnp.concatenate([kp_ref[:, sl], kc_ref[:, sl]], axis=0)
            vcat = jnp.concatenate([vp_ref[:, sl], vc_ref[:, sl]], axis=0)
            sc = _dot(q_ref[:, sl], kcat, NT) * scale
            sc = jnp.where(valid, sc, NEG)
            mx = jnp.max(sc, axis=-1, keepdims=True)
            p = jnp.exp(sc - mx)
            l = jnp.sum(p, axis=-1, keepdims=True)
            o = _dot(p.astype(BF16), vcat, NN) / l
            o_ref[:, sl] = o.astype(BF16)
            lse_ref[:, sl] = jnp.broadcast_to(mx + jnp.log(l), (SUB_BLOCK, HEAD_DIM))

    def cur(off):
        return pl.BlockSpec((SUB_BLOCK, cw), lambda n, j: (n, j + off))

    def prev(off):
        return pl.BlockSpec((SUB_BLOCK, cw), lambda n, j: (jnp.maximum(n - 1, 0), j + off))

    return pl.pallas_call(
        body, name=name, grid=grid,
        in_specs=[cur(offs[0]), prev(offs[1]), cur(offs[1]), prev(offs[2]), cur(offs[2])],
        out_specs=[cur(0), cur(0)],
        out_shape=[jax.ShapeDtypeStruct((m, width), BF16), jax.ShapeDtypeStruct((m, width), F32)],
        compiler_params=_params(("parallel", "parallel")),
    )(q, k, k, v, v)


def _attn_combine(os_, lses):
    s, aw = os_[0].shape
    tm = min(s, 512)

    def body(o1, o2, o3, l1, l2, l3, o_ref, lse_ref):
        a, b, c = l1[...], l2[...], l3[...]
        mx = jnp.maximum(jnp.maximum(a, b), c)
        ea, eb, ec = jnp.exp(a - mx), jnp.exp(b - mx), jnp.exp(c - mx)
        tot = ea + eb + ec
        o = (ea * o1[...].astype(F32) + eb * o2[...].astype(F32) + ec * o3[...].astype(F32)) / tot
        o_ref[...] = o.astype(BF16)
        lse_ref[...] = mx + jnp.log(tot)

    spec = pl.BlockSpec((tm, aw), lambda i: (i, 0))
    return pl.pallas_call(
        body, name="attn_combine", grid=(s // tm,), in_specs=[spec] * 6, out_specs=[spec, spec],
        out_shape=[jax.ShapeDtypeStruct((s, aw), BF16), jax.ShapeDtypeStruct((s, aw), F32)],
        compiler_params=_params(("parallel",)),
    )(*os_, *lses)


def _pool_counts(tm, rows, pgw, row0):
    t = lax.broadcasted_iota(jnp.int32, (rows, len(POOL_WINDOWS) * pgw), 0) + row0
    col = lax.broadcasted_iota(jnp.int32, (rows, len(POOL_WINDOWS) * pgw), 1)
    w = jnp.full((rows, len(POOL_WINDOWS) * pgw), POOL_WINDOWS[0], jnp.int32)
    for g in range(1, len(POOL_WINDOWS)):
        w = jnp.where(col >= g * pgw, POOL_WINDOWS[g], w)
    return jnp.minimum(t + 1, w).astype(F32)


def _window_sums(xs, direction, pgw):
    rows = xs.shape[0]
    acc = xs
    out = None
    col = lax.broadcasted_iota(jnp.int32, xs.shape, 1)
    for g, w in enumerate(POOL_WINDOWS):
        sh = w // 2
        acc = acc + pltpu.roll(acc, sh if direction > 0 else rows - sh, 0)
        out = acc if out is None else jnp.where(col >= g * pgw, acc, out)
    return out


def _pool_fwd(h, wp, scale, aw):
    s = h.shape[0]
    pw_ = aw
    pgw = pw_ // len(POOL_WINDOWS)
    tm = min(s, 512)
    hb = tm // POOL_HALO

    def body(u_ref, halo_ref, wp_ref, sc_ref, p_ref, pw_ref, y_ref):
        i = pl.program_id(0)
        u = u_ref[...].astype(F32)
        halo = halo_ref[...].astype(F32) * jnp.where(i > 0, 1.0, 0.0)
        xs = jnp.concatenate([halo, u], axis=0)
        sums = _window_sums(xs, +1, pgw)[POOL_HALO:]
        p = (sums / _pool_counts(tm, tm, pgw, i * tm) - u).astype(BF16)
        p_ref[...] = p
        sc = sc_ref[...]
        for g in range(len(POOL_WINDOWS)):
            sl = slice(g * pgw, (g + 1) * pgw)
            pw = _dot(p[:, sl], wp_ref[g], NN)
            pw_ref[:, sl] = pw.astype(BF16)
            y_ref[:, sl] = (pw * sc[:, sl]).astype(BF16)

    out = jax.ShapeDtypeStruct((s, pw_), BF16)
    row = pl.BlockSpec((tm, pw_), lambda i: (i, 0))
    return pl.pallas_call(
        body, name="pool_fwd", grid=(s // tm,),
        in_specs=[pl.BlockSpec((tm, pw_), lambda i: (i, 3)),
                  pl.BlockSpec((POOL_HALO, pw_), lambda i: (jnp.maximum(i * hb - 1, 0), 3)),
                  pl.BlockSpec(wp.shape, lambda i: (0, 0, 0)),
                  pl.BlockSpec((1, pw_), lambda i: (0, 0))],
        out_specs=[row, row, row], out_shape=[out, out, out],
        compiler_params=_params(("parallel",)),
    )(h, h, wp, scale)


def _branch_merge(o_attn, y, wba_st, wbp_st, h, aw, d):
    s = o_attn.shape[0]
    tn = wba_st.shape[2]
    tm = min(s, 1024)
    ga0 = 4 * aw // tn
    gp0 = (4 * aw + d) // tn

    def body(o_ref, y_ref, wa_ref, wp_ref, sga_ref, sgp_ref, ya_ref, yp_ref, mg_ref):
        ya = _dot(o_ref[...], wa_ref[...], NN)
        yp = _dot(y_ref[...], wp_ref[...], NN)
        ya_ref[...] = ya.astype(BF16)
        yp_ref[...] = yp.astype(BF16)
        mg_ref[...] = (sga_ref[...].astype(F32) * ya + sgp_ref[...].astype(F32) * yp).astype(BF16)

    out = jax.ShapeDtypeStruct((s, d), BF16)
    blk = pl.BlockSpec((tm, tn), lambda i, j: (i, j))
    return pl.pallas_call(
        body, name="branch_merge", grid=(s // tm, N_CHIPS),
        in_specs=[pl.BlockSpec((tm, aw), lambda i, j: (i, 0)),
                  pl.BlockSpec((tm, aw), lambda i, j: (i, 0)),
                  pl.BlockSpec((None, aw, tn), lambda i, j: (j, 0, 0)),
                  pl.BlockSpec((None, aw, tn), lambda i, j: (j, 0, 0)),
                  pl.BlockSpec((tm, tn), lambda i, j: (i, j + ga0)),
                  pl.BlockSpec((tm, tn), lambda i, j: (i, j + gp0))],
        out_specs=[blk, blk, blk], out_shape=[out, out, out],
        compiler_params=_params(("parallel", "parallel")),
    )(o_attn, y, wba_st, wbp_st, h, h)


def _mix_norm(merged, w_out, x, g1, b1):
    s, d = x.shape
    tm = min(s, 256)

    def epilogue(acc, ex, outs):
        x_ref, g_ref, b_ref = ex
        xh_ref, rs_ref, xb_ref = outs
        xhat, rstd = _layer_norm_fwd(ALPHA * x_ref[...] + acc)
        xh_ref[...] = xhat
        rs_ref[...] = rstd
        xb_ref[...] = (xhat * g_ref[...] + b_ref[...]).astype(BF16)

    row = pl.BlockSpec((tm, d), lambda i, j, k: (i, 0))
    vec = pl.BlockSpec((1, d), lambda i, j, k: (0, 0))
    return _mm(
        "mix_norm", (s // tm, 1, 1), merged, row, w_out, pl.BlockSpec((d, d), lambda i, j, k: (0, 0)),
        NN, [x, g1, b1], [row, vec, vec],
        [jax.ShapeDtypeStruct((s, d), F32), jax.ShapeDtypeStruct((s, 1), F32),
         jax.ShapeDtypeStruct((s, d), BF16)],
        [row, pl.BlockSpec((tm, 1), lambda i, j, k: (i, 0)), row], epilogue, None)


def _ff_up(x1b, w1_st):
    s, d = x1b.shape
    n_sh = w1_st.shape[2]
    tm, tn = min(s, 1024), min(n_sh, 1024)
    per = n_sh // tn

    def epilogue(acc, ex, outs):
        r = jnp.maximum(acc, 0.0)
        outs[0][...] = (r * r).astype(BF16)

    return _mm(
        "ff_up", (s // tm, N_CHIPS * per, 1), x1b, pl.BlockSpec((tm, d), lambda i, j, k: (i, 0)),
        w1_st, pl.BlockSpec((None, d, tn), lambda i, j, k: (j // per, 0, j % per)), NN, [], [],
        [jax.ShapeDtypeStruct((s, N_CHIPS * n_sh), BF16)],
        [pl.BlockSpec((tm, tn), lambda i, j, k: (i, j))], epilogue, None)[0]


def _ff_down_loss(r, w2, xhat1, g1, b1, g2, b2, target):
    s, d = xhat1.shape
    dff = r.shape[1]
    tm, tk = min(s, 256), min(dff, 1024)

    def epilogue(acc, ex, outs):
        xh1_ref, g1_ref, b1_ref, g2_ref, b2_ref, t_ref = ex
        dz_ref, dzb_ref, st_ref = outs
        x1 = xh1_ref[...] * g1_ref[...] + b1_ref[...]
        xhat2, rstd2 = _layer_norm_fwd(ALPHA * x1 + acc)
        g2v = g2_ref[...]
        err = xhat2 * g2v + b2_ref[...] - t_ref[...]
        loss = (0.5 / d) * jnp.sum(jnp.sum(err * err, axis=-1, keepdims=True), axis=0, keepdims=True)
        dy = err * (1.0 / d)
        dz = _layer_norm_bwd(dy, xhat2, rstd2, g2v)
        dz_ref[...] = dz
        dzb_ref[...] = dz.astype(BF16)
        st_ref[...] = _stats_rows([jnp.sum(dy * xhat2, axis=0, keepdims=True),
                                   jnp.sum(dy, axis=0, keepdims=True),
                                   jnp.broadcast_to(loss, (1, d))], d)

    row = pl.BlockSpec((tm, d), lambda i, j, k: (i, 0))
    vec = pl.BlockSpec((1, d), lambda i, j, k: (0, 0))
    return _mm(
        "ff_down_loss", (s // tm, 1, dff // tk), r, pl.BlockSpec((tm, tk), lambda i, j, k: (i, k)),
        w2, pl.BlockSpec((tk, d), lambda i, j, k: (k, 0)), NN,
        [xhat1, g1, b1, g2, b2, target], [row, vec, vec, vec, vec, row],
        [jax.ShapeDtypeStruct((s, d), F32), jax.ShapeDtypeStruct((s, d), BF16),
         jax.ShapeDtypeStruct((s // tm, 8, d), F32)],
        [row, row, pl.BlockSpec((None, 8, d), lambda i, j, k: (i, 0, 0))], epilogue, (tm, d))


def _ff_down_bwd(dz2b, w2, r):
    s, d = dz2b.shape
    dff = r.shape[1]
    tm, tn = min(s, 1024), min(dff, 1024)

    def epilogue(acc, ex, outs):
        outs[0][...] = (acc * (2.0 * jnp.sqrt(ex[0][...].astype(F32)))).astype(BF16)

    blk = pl.BlockSpec((tm, tn), lambda i, j, k: (i, j))
    return _mm(
        "ff_down_bwd", (s // tm, dff // tn, 1), dz2b, pl.BlockSpec((tm, d), lambda i, j, k: (i, 0)),
        w2, pl.BlockSpec((tn, d), lambda i, j, k: (j, 0)), NT, [r], [blk],
        [jax.ShapeDtypeStruct((s, dff), BF16)], [blk], epilogue, None)[0]


def _wgrad(name, a, g, n_sh):
    s, rows = a.shape
    cols = g.shape[1]
    tm, tn, tk = min(rows, 1024), min(n_sh, 1024), min(s, 1024)
    per = n_sh // tn

    def epilogue(acc, ex, outs):
        outs[0][...] = acc.astype(BF16)

    return _mm(
        name, (rows // tm, cols // tn, s // tk), a, pl.BlockSpec((tk, tm), lambda i, j, k: (k, i)),
        g, pl.BlockSpec((tk, tn), lambda i, j, k: (k, j)), TN, [], [],
        [jax.ShapeDtypeStruct((cols // n_sh, rows, n_sh), BF16)],
        [pl.BlockSpec((None, tm, tn), lambda i, j, k: (j // per, i, j % per))], epilogue,
        (tm, tn))[0]


def _ff_up_bwd(da, w1_st, dz2, xhat1, rstd1, g1):
    s, d = dz2.shape
    n_sh = w1_st.shape[2]
    tm, tk = min(s, 256), min(n_sh, 1024)
    per = n_sh // tk

    def epilogue(acc, ex, outs):
        dz2_ref, xh_ref, rs_ref, g_ref = ex
        dz_ref, dzb_ref, st_ref = outs
        dx1 = ALPHA * dz2_ref[...] + acc
        xhat = xh_ref[...]
        dz = _layer_norm_bwd(dx1, xhat, rs_ref[...], g_ref[...])
        dz_ref[...] = dz
        dzb_ref[...] = dz.astype(BF16)
        st_ref[...] = _stats_rows([jnp.sum(dx1 * xhat, axis=0, keepdims=True),
                                   jnp.sum(dx1, axis=0, keepdims=True)], d)

    row = pl.BlockSpec((tm, d), lambda i, j, k: (i, 0))
    return _mm(
        "ff_up_bwd", (s // tm, 1, N_CHIPS * per), da, pl.BlockSpec((tm, tk), lambda i, j, k: (i, k)),
        w1_st, pl.BlockSpec((None, d, tk), lambda i, j, k: (k // per, 0, k % per)), NT,
        [dz2, xhat1, rstd1, g1],
        [row, row, pl.BlockSpec((tm, 1), lambda i, j, k: (i, 0)), pl.BlockSpec((1, d), lambda i, j, k: (0, 0))],
        [jax.ShapeDtypeStruct((s, d), F32), jax.ShapeDtypeStruct((s, d), BF16),
         jax.ShapeDtypeStruct((s // tm, 8, d), F32)],
        [row, row, pl.BlockSpec((None, 8, d), lambda i, j, k: (i, 0, 0))], epilogue, (tm, d))


def _mix_bwd(dz1b, w_out, h, ya, yp, aw):
    s, d = dz1b.shape
    tm = min(s, 256)
    gblk = 4 * aw // d

    def epilogue(acc, ex, outs):
        sga_ref, sgp_ref, ya_ref, yp_ref = ex
        dya_ref, dyp_ref, dg_ref = outs
        sga, sgp = sga_ref[...].astype(F32), sgp_ref[...].astype(F32)
        dya_ref[...] = (acc * sga).astype(BF16)
        dyp_ref[...] = (acc * sgp).astype(BF16)
        dg_ref[:, :d] = (acc * ya_ref[...].astype(F32) * (sga * (1.0 - sga))).astype(BF16)
        dg_ref[:, d:] = (acc * yp_ref[...].astype(F32) * (sgp * (1.0 - sgp))).astype(BF16)

    row = pl.BlockSpec((tm, d), lambda i, j, k: (i, 0))
    return _mm(
        "mix_bwd", (s // tm, 1, 1), dz1b, row, w_out, pl.BlockSpec((d, d), lambda i, j, k: (0, 0)), NT,
        [h, h, ya, yp],
        [pl.BlockSpec((tm, d), lambda i, j, k: (i, gblk)),
         pl.BlockSpec((tm, d), lambda i, j, k: (i, gblk + 1)), row, row],
        [jax.ShapeDtypeStruct((s, d), BF16), jax.ShapeDtypeStruct((s, d), BF16),
         jax.ShapeDtypeStruct((s, 2 * d), BF16)],
        [row, row, pl.BlockSpec((tm, 2 * d), lambda i, j, k: (i, 0))], epilogue, None)


def _branch_in_bwd(name, dyb, wb_st, epilogue, extras, extra_specs, out_shape, out_specs):
    s = dyb.shape[0]
    aw, tk = wb_st.shape[1], wb_st.shape[2]
    tm = min(s, 1024)
    return _mm(
        name, (s // tm, 1, N_CHIPS), dyb, pl.BlockSpec((tm, tk), lambda i, j, k: (i, k)),
        wb_st, pl.BlockSpec((None, aw, tk), lambda i, j, k: (k, 0, 0)), NT,
        extras, extra_specs, out_shape, out_specs, epilogue, (tm, aw)), tm


def _attn_out_bwd(dya, wba_st, o_attn):
    s, aw = o_attn.shape

    def epilogue(acc, ex, outs):
        o = ex[0][...].astype(F32)
        do_ref, dl_ref = outs
        do_ref[...] = acc.astype(BF16)
        for hh in range(aw // HEAD_DIM):
            sl = slice(hh * HEAD_DIM, (hh + 1) * HEAD_DIM)
            dl = jnp.sum(acc[:, sl] * o[:, sl], axis=-1, keepdims=True)
            dl_ref[:, sl] = jnp.broadcast_to(dl, (acc.shape[0], HEAD_DIM))

    tm = min(s, 1024)
    row = pl.BlockSpec((tm, aw), lambda i, j, k: (i, 0))
    return _branch_in_bwd(
        "attn_out_bwd", dya, wba_st, epilogue, [o_attn], [row],
        [jax.ShapeDtypeStruct((s, aw), BF16), jax.ShapeDtypeStruct((s, aw), F32)], [row, row])[0]


def _pool_out_bwd(dyp, wbp_st, pw, scale):
    s, pw_ = pw.shape
    tm = min(s, 1024)

    def epilogue(acc, ex, outs):
        pw_ref, sc_ref = ex
        dpw_ref, st_ref = outs
        dpw_ref[...] = (acc * sc_ref[...]).astype(BF16)
        st_ref[...] = _stats_rows([jnp.sum(acc * pw_ref[...].astype(F32), axis=0, keepdims=True)], pw_)

    row = pl.BlockSpec((tm, pw_), lambda i, j, k: (i, 0))
    return _branch_in_bwd(
        "pool_out_bwd", dyp, wbp_st, epilogue, [pw, scale],
        [row, pl.BlockSpec((1, pw_), lambda i, j, k: (0, 0))],
        [jax.ShapeDtypeStruct((s, pw_), BF16), jax.ShapeDtypeStruct((s // tm, 8, pw_), F32)],
        [row, pl.BlockSpec((None, 8, pw_), lambda i, j, k: (i, 0, 0))])[0]


def _pool_bwd(dpw, p, wp):
    s, pw_ = p.shape
    ng = len(POOL_WINDOWS)
    pgw = pw_ // ng
    tm = min(s, 512)
    hb = tm // POOL_HALO
    nblk = s // tm

    def body(dpw_ref, nxt_ref, p_ref, wp_ref, dwp_ref, du_ref):
        i = pl.program_id(0)
        nxt = (nxt_ref[...].astype(F32) * jnp.where(i < nblk - 1, 1.0, 0.0)).astype(BF16)
        dpw_all = jnp.concatenate([dpw_ref[...], nxt], axis=0)

        @pl.when(i == 0)
        def _():
            dwp_ref[...] = jnp.zeros_like(dwp_ref)

        dps = []
        for g in range(ng):
            sl = slice(g * pgw, (g + 1) * pgw)
            dwp_ref[g] += _dot(p_ref[:, sl], dpw_ref[:, sl], TN)
            dps.append(_dot(dpw_all[:, sl], wp_ref[g], NT))
        dp = jnp.concatenate(dps, axis=1)
        dpn = dp / _pool_counts(tm, tm + POOL_HALO, pgw, i * tm)
        du_ref[...] = (_window_sums(dpn, -1, pgw)[:tm] - dp[:tm]).astype(BF16)

    row = pl.BlockSpec((tm, pw_), lambda i: (i, 0))
    full = pl.BlockSpec((ng, pgw, pgw), lambda i: (0, 0, 0))
    return pl.pallas_call(
        body, name="pool_bwd", grid=(nblk,),
        in_specs=[row, pl.BlockSpec((POOL_HALO, pw_), lambda i: (jnp.minimum((i + 1) * hb, s // POOL_HALO - 1), 0)),
                  row, full],
        out_specs=[full, row],
        out_shape=[jax.ShapeDtypeStruct((ng, pgw, pgw), F32), jax.ShapeDtypeStruct((s, pw_), BF16)],
        compiler_params=_params(("arbitrary",)),
    )(dpw, dpw, p, wp)


def _attn_bwd_q(name, q, k, v, do, lse, delta, offs, cw, width):
    m = do.shape[0]
    nh = cw // HEAD_DIM
    grid = (m // SUB_BLOCK, width // cw)
    scale = HEAD_DIM ** -0.5

    def body(q_ref, kp_ref, kc_ref, vp_ref, vc_ref, do_ref, lse_ref, dl_ref, dq_ref):
        valid = _band_masks(pl.program_id(0))
        for hh in range(nh):
            sl = slice(hh * HEAD_DIM, (hh + 1) * HEAD_DIM)
            kcat = jnp.concatenate([kp_ref[:, sl], kc_ref[:, sl]], axis=0)
            vcat = jnp.concatenate([vp_ref[:, sl], vc_ref[:, sl]], axis=0)
            lse2 = jnp.concatenate([lse_ref[:, sl]] * 2, axis=1)
            dl2 = jnp.concatenate([dl_ref[:, sl]] * 2, axis=1)
            sc = _dot(q_ref[:, sl], kcat, NT) * scale
            p = jnp.where(valid, jnp.exp(jnp.where(valid, sc, NEG) - lse2), 0.0)
            dp = _dot(do_ref[:, sl], vcat, NT)
            ds = (p * (dp - dl2)).astype(BF16)
            dq_ref[:, sl] = (_dot(ds, kcat, NN) * scale).astype(BF16)

    def cur(off):
        return pl.BlockSpec((SUB_BLOCK, cw), lambda n, j: (n, j + off))

    def prev(off):
        return pl.BlockSpec((SUB_BLOCK, cw), lambda n, j: (jnp.maximum(n - 1, 0), j + off))

    return pl.pallas_call(
        body, name=name, grid=grid,
        in_specs=[cur(offs[0]), prev(offs[1]), cur(offs[1]), prev(offs[2]), cur(offs[2]),
                  cur(0), cur(0), cur(0)],
        out_specs=cur(0), out_shape=jax.ShapeDtypeStruct((m, width), BF16),
        compiler_params=_params(("parallel", "parallel")),
    )(q, k, k, v, v, do, lse, delta)


def _attn_bwd_kv(name, q, k, v, do, lse, delta, offs, cw, width):
    m = do.shape[0]
    nh = cw // HEAD_DIM
    nblk = m // SUB_BLOCK
    grid = (nblk, width // cw)
    scale = HEAD_DIM ** -0.5

    def body(qc_ref, qn_ref, k_ref, v_ref, doc_ref, don_ref, lsec_ref, lsen_ref, dlc_ref, dln_ref,
             dk_ref, dv_ref):
        no_next = jnp.where(pl.program_id(0) < nblk - 1, 0, 4 * SUB_BLOCK)
        r = lax.broadcasted_iota(jnp.int32, (2 * SUB_BLOCK, SUB_BLOCK), 0)
        kj = lax.broadcasted_iota(jnp.int32, (2 * SUB_BLOCK, SUB_BLOCK), 1)
        valid = ((r < SUB_BLOCK) & (kj <= r)) | ((r >= SUB_BLOCK) & (kj >= r - SUB_BLOCK + no_next))
        for hh in range(nh):
            sl = slice(hh * HEAD_DIM, (hh + 1) * HEAD_DIM)
            qcat = jnp.concatenate([qc_ref[:, sl], qn_ref[:, sl]], axis=0)
            docat = jnp.concatenate([doc_ref[:, sl], don_ref[:, sl]], axis=0)
            lse2 = jnp.concatenate([lsec_ref[:, sl], lsen_ref[:, sl]], axis=0)
            dl2 = jnp.concatenate([dlc_ref[:, sl], dln_ref[:, sl]], axis=0)
            sc = _dot(qcat, k_ref[:, sl], NT) * scale
            p = jnp.where(valid, jnp.exp(jnp.where(valid, sc, NEG) - lse2), 0.0)
            dp = _dot(docat, v_ref[:, sl], NT)
            ds = (p * (dp - dl2)).astype(BF16)
            dv_ref[:, sl] = _dot(p.astype(BF16), docat, TN).astype(BF16)
            dk_ref[:, sl] = (_dot(ds, qcat, TN) * scale).astype(BF16)

    def cur(off):
        return pl.BlockSpec((SUB_BLOCK, cw), lambda n, j: (n, j + off))

    def nxt(off):
        return pl.BlockSpec((SUB_BLOCK, cw), lambda n, j: (jnp.minimum(n + 1, nblk - 1), j + off))

    out = jax.ShapeDtypeStruct((m, width), BF16)
    return pl.pallas_call(
        body, name=name, grid=grid,
        in_specs=[cur(offs[0]), nxt(offs[0]), cur(offs[1]), cur(offs[2]),
                  cur(0), nxt(0), cur(0), nxt(0), cur(0), nxt(0)],
        out_specs=[cur(0), cur(0)], out_shape=[out, out],
        compiler_params=_params(("parallel", "parallel")),
    )(q, q, k, v, do, do, lse, lse, delta, delta)


def _qkvu_grad(dqs, dks, dvs, du, cos_t, sin_t):
    s, aw = du.shape
    tm = min(s, 512)

    def body(*refs):
        ins, (cos_ref, sin_ref, du_ref, out_ref) = refs[:9], refs[9:]
        cos, sin = cos_ref[...], sin_ref[...]
        for part in range(2):
            tot = sum(r[...].astype(F32) for r in ins[3 * part:3 * part + 3])
            for hh in range(aw // HEAD_DIM):
                sl = slice(hh * HEAD_DIM, (hh + 1) * HEAD_DIM)
                t = tot[:, sl]
                out_ref[:, part * aw + hh * HEAD_DIM:part * aw + (hh + 1) * HEAD_DIM] = (
                    t * cos - pltpu.roll(t, HEAD_DIM // 2, 1) * sin).astype(BF16)
        out_ref[:, 2 * aw:3 * aw] = sum(r[...].astype(F32) for r in ins[6:9]).astype(BF16)
        out_ref[:, 3 * aw:] = du_ref[...]

    row = pl.BlockSpec((tm, aw), lambda i: (i, 0))
    tab = pl.BlockSpec((tm, HEAD_DIM), lambda i: (i, 0))
    return pl.pallas_call(
        body, name="qkvu_grad", grid=(s // tm,), in_specs=[row] * 9 + [tab, tab, row],
        out_specs=pl.BlockSpec((tm, 4 * aw), lambda i: (i, 0)),
        out_shape=jax.ShapeDtypeStruct((s, 4 * aw), BF16),
        compiler_params=_params(("parallel",)),
    )(*dqs, *dks, *dvs, cos_t, sin_t, du)


def _in_proj_bwd_x(name, dh, w_in_st, shard0, base, scale_base):
    s, kdim = dh.shape
    d, n_sh = w_in_st.shape[1], w_in_st.shape[2]
    tm, tk = min(s, 512), min(n_sh, 1024)
    per = n_sh // tk

    def epilogue(acc, ex, outs):
        outs[0][...] = scale_base * ex[0][...] + acc

    row = pl.BlockSpec((tm, d), lambda i, j, k: (i, 0))
    return _mm(
        name, (s // tm, 1, kdim // tk), dh, pl.BlockSpec((tm, tk), lambda i, j, k: (i, k)),
        w_in_st, pl.BlockSpec((None, d, tk), lambda i, j, k: (shard0 + k // per, 0, k % per)), NT,
        [base], [row], [jax.ShapeDtypeStruct((s, d), F32)], [row], epilogue, (tm, d))[0]


def _chip_peers():
    x, y, c = lax.axis_index("x"), lax.axis_index("y"), lax.axis_index("c")
    return x, y, c, [(1 - x, y), (x, 1 - y), (1 - x, 1 - y)]


def _gather_weights(shards):
    n = len(shards)

    def body(*refs):
        ins, outs = refs[:n], refs[n:2 * n]
        send, recv, loc = refs[2 * n:]
        x, y, c, peers = _chip_peers()
        me = 2 * x + y
        local, remote = [], []
        for w in range(n):
            own = pltpu.make_async_copy(ins[w], outs[w].at[me], loc.at[w])
            own.start()
            local.append(own)
            for p, (px, py) in enumerate(peers):
                cp = pltpu.make_async_remote_copy(
                    src_ref=ins[w], dst_ref=outs[w].at[me], send_sem=send.at[w, p],
                    recv_sem=recv.at[w, p], device_id=(px, py, c), device_id_type=MESH)
                cp.start()
                remote.append(cp)
        for w in range(n):
            for p, (px, py) in enumerate(peers):
                pltpu.make_async_remote_copy(
                    src_ref=ins[w], dst_ref=outs[w].at[2 * px + py], send_sem=send.at[w, p],
                    recv_sem=recv.at[w, p], device_id=(px, py, c), device_id_type=MESH).wait_recv()
        for cp in remote:
            cp.wait_send()
        for cp in local:
            cp.wait()

    return pl.pallas_call(
        body, name="gather_weights", in_specs=[ANY_SPEC] * n, out_specs=[ANY_SPEC] * n,
        out_shape=[jax.ShapeDtypeStruct((N_CHIPS, *sh.shape), sh.dtype) for sh in shards],
        scratch_shapes=[pltpu.SemaphoreType.DMA((n, 3)), pltpu.SemaphoreType.DMA((n, 3)),
                        pltpu.SemaphoreType.DMA((n,))],
    )(*shards)


def _scatter_grads(grads):
    n = len(grads)

    def body(*refs):
        ins, lands, owns = refs[:n], refs[n:2 * n], refs[2 * n:3 * n]
        send, recv, loc = refs[3 * n:]
        x, y, c, peers = _chip_peers()
        me = 2 * x + y
        local, remote = [], []
        for w in range(n):
            own = pltpu.make_async_copy(ins[w].at[me], owns[w], loc.at[w])
            own.start()
            local.append(own)
            for p, (px, py) in enumerate(peers):
                cp = pltpu.make_async_remote_copy(
                    src_ref=ins[w].at[2 * px + py], dst_ref=lands[w].at[p], send_sem=send.at[w, p],
                    recv_sem=recv.at[w, p], device_id=(px, py, c), device_id_type=MESH)
                cp.start()
                remote.append(cp)
        for cp in remote:
            cp.wait_recv()
        for cp in remote:
            cp.wait_send()
        for cp in local:
            cp.wait()

    return pl.pallas_call(
        body, name="scatter_grads", in_specs=[ANY_SPEC] * n, out_specs=[ANY_SPEC] * (2 * n),
        out_shape=[jax.ShapeDtypeStruct((3, *g.shape[1:]), g.dtype) for g in grads]
        + [jax.ShapeDtypeStruct(g.shape[1:], g.dtype) for g in grads],
        scratch_shapes=[pltpu.SemaphoreType.DMA((n, 3)), pltpu.SemaphoreType.DMA((n, 3)),
                        pltpu.SemaphoreType.DMA((n,))],
    )(*grads)


def _sum_slabs(name, own, land):
    r, c = own.shape
    tm = min(r, 256)

    def body(own_ref, land_ref, out_ref):
        acc = own_ref[...].astype(F32)
        for p in range(3):
            acc = acc + land_ref[p].astype(F32)
        out_ref[...] = acc

    return pl.pallas_call(
        body, name=name, grid=(r // tm,),
        in_specs=[pl.BlockSpec((tm, c), lambda i: (i, 0)), pl.BlockSpec((3, tm, c), lambda i: (0, i, 0))],
        out_specs=pl.BlockSpec((tm, c), lambda i: (i, 0)), out_shape=jax.ShapeDtypeStruct((r, c), F32),
        compiler_params=_params(("parallel",)),
    )(own, land)


def _exchange_cores(parts):
    n = len(parts)

    def body(*refs):
        ins, outs = refs[:n], refs[n:2 * n]
        send, recv = refs[2 * n:]
        x, y, c = lax.axis_index("x"), lax.axis_index("y"), lax.axis_index("c")
        copies = [pltpu.make_async_remote_copy(
            src_ref=ins[w], dst_ref=outs[w], send_sem=send.at[w], recv_sem=recv.at[w],
            device_id=(x, y, 1 - c), device_id_type=MESH) for w in range(n)]
        for cp in copies:
            cp.start()
        for cp in copies:
            cp.wait()

    return pl.pallas_call(
        body, name="exchange_cores", in_specs=[ANY_SPEC] * n, out_specs=[ANY_SPEC] * n,
        out_shape=[jax.ShapeDtypeStruct(p.shape, p.dtype) for p in parts],
        scratch_shapes=[pltpu.SemaphoreType.DMA((n,)), pltpu.SemaphoreType.DMA((n,))],
    )(*parts)


def _allreduce_stats(stats):
    n = len(stats)

    def body(*refs):
        ins, outs = refs[:n], refs[n:2 * n]
        mine, gath = refs[2 * n:3 * n], refs[3 * n:4 * n]
        send, recv = refs[4 * n:]
        x, y, c = lax.axis_index("x"), lax.axis_index("y"), lax.axis_index("c")
        me = 4 * x + 2 * y + c
        flips = [(bx, by, bc) for bx in (0, 1) for by in (0, 1) for bc in (0, 1)][1:]

        def peer(f):
            return (x + f[0] * (1 - 2 * x), y + f[1] * (1 - 2 * y), c + f[2] * (1 - 2 * c))

        copies = []
        for t in range(n):
            tot = ins[t][0]
            for b in range(1, ins[t].shape[0]):
                tot = tot + ins[t][b]
            mine[t][...] = tot
            gath[t][me] = tot
            for k, f in enumerate(flips):
                cp = pltpu.make_async_remote_copy(
                    src_ref=mine[t], dst_ref=gath[t].at[me], send_sem=send.at[t, k],
                    recv_sem=recv.at[t, k], device_id=peer(f), device_id_type=MESH)
                cp.start()
                copies.append(cp)
        for t in range(n):
            for k, f in enumerate(flips):
                px, py, pc = peer(f)
                pltpu.make_async_remote_copy(
                    src_ref=mine[t], dst_ref=gath[t].at[4 * px + 2 * py + pc], send_sem=send.at[t, k],
                    recv_sem=recv.at[t, k], device_id=(px, py, pc), device_id_type=MESH).wait_recv()
        for cp in copies:
            cp.wait_send()
        for t in range(n):
            tot = gath[t][0]
            for dev in range(1, 8):
                tot = tot + gath[t][dev]
            outs[t][...] = tot

    vm = pl.BlockSpec(memory_space=pltpu.VMEM)
    return pl.pallas_call(
        body, name="allreduce_stats", in_specs=[vm] * n, out_specs=[vm] * n,
        out_shape=[jax.ShapeDtypeStruct(s.shape[1:], F32) for s in stats],
        scratch_shapes=[pltpu.VMEM(s.shape[1:], F32) for s in stats]
        + [pltpu.VMEM((8, *s.shape[1:]), F32) for s in stats]
        + [pltpu.SemaphoreType.DMA((n, 7)), pltpu.SemaphoreType.DMA((n, 7))],
    )(*stats)


def _adamw(name, w, m, v, g_parts):
    r, c = w.shape
    tm = min(r, 128)
    n_g = len(g_parts)

    def body(*refs):
        w_ref, m_ref, v_ref = refs[:3]
        g_refs = refs[3:3 + n_g]
        g_out, d_out, m_out, v_out = refs[3 + n_g:]
        g = g_refs[0][...]
        for gr in g_refs[1:]:
            g = g + gr[...]
        m_new = ADAM_B1 * m_ref[...] + (1.0 - ADAM_B1) * g
        v_new = ADAM_B2 * v_ref[...] + (1.0 - ADAM_B2) * (g * g)
        m_hat = m_new / (1.0 - ADAM_B1 ** ADAM_STEP)
        v_hat = v_new / (1.0 - ADAM_B2 ** ADAM_STEP)
        g_out[...] = g
        d_out[...] = -ADAM_LR * (m_hat / (jnp.sqrt(v_hat) + ADAM_EPS) + ADAM_WD * w_ref[...])
        m_out[...] = m_new
        v_out[...] = v_new

    blk = pl.BlockSpec((tm, c), lambda i: (i, 0))
    out = jax.ShapeDtypeStruct((r, c), F32)
    return pl.pallas_call(
        body, name=name, grid=(r // tm,), in_specs=[blk] * (3 + n_g), out_specs=[blk] * 4,
        out_shape=[out] * 4, compiler_params=_params(("parallel",)),
    )(w, m, v, *g_parts)


def _rope_tables(positions):
    half = HEAD_DIM // 2
    inv_freq = ROPE_THETA ** (-jnp.arange(half, dtype=F32) / half)
    ang = positions.astype(F32)[0, :, None] * inv_freq
    cos, sin = jnp.cos(ang), jnp.sin(ang)
    return jnp.concatenate([cos, cos], axis=-1), jnp.concatenate([-sin, sin], axis=-1)


def kernel(x, positions, w_in, w_pool, pool_scale, w_branch_attn, w_branch_pool, w_out, ln_mix_g, ln_mix_b, w_ff1, w_ff2, ln_ff_g, ln_ff_b, loss_target, m_w_in, m_w_pool, m_pool_scale, m_w_branch_attn, m_w_branch_pool, m_w_out, m_ln_mix_g, m_ln_mix_b, m_w_ff1, m_w_ff2, m_ln_ff_g, m_ln_ff_b, v_w_in, v_w_pool, v_pool_scale, v_w_branch_attn, v_w_branch_pool, v_w_out, v_ln_mix_g, v_ln_mix_b, v_w_ff1, v_w_ff2, v_ln_ff_g, v_ln_ff_b):
    s, d = x.shape[1], x.shape[2]
    aw = d // 2
    ng = len(POOL_WINDOWS)
    pgw = aw // ng
    x2d, target = x[0], loss_target[0]
    xb = x2d.astype(BF16)
    cos_t, sin_t = _rope_tables(positions)

    big = {"w_in": w_in[0], "w_pool": w_pool[0].reshape(-1, pgw), "w_branch_attn": w_branch_attn[0],
           "w_branch_pool": w_branch_pool[0], "w_out": w_out[0], "w_ff1": w_ff1[0], "w_ff2": w_ff2[0]}
    names = list(big)
    w_in_st, wp_st, wba_st, wbp_st, w_out_st, w1_st, w2_st = _gather_weights(
        [big[k].astype(BF16) for k in names])
    rows_sh = pgw // N_CHIPS
    wp = wp_st.reshape(N_CHIPS, ng, rows_sh, pgw).transpose(1, 0, 2, 3).reshape(ng, pgw, pgw)
    w_out_full = w_out_st.reshape(d, d)
    w2_full = w2_st.reshape(-1, d)
    dff = w2_full.shape[0]

    h = _in_proj(xb, w_in_st, cos_t, sin_t, aw)

    def to_view(a, dil):
        return a.reshape(s // dil, dil * a.shape[1])

    qkv = {1: (h, h, h)}
    offs = {1: (0, 1, 2)}
    for dil in DILATIONS[1:]:
        qkv[dil] = tuple(to_view(h[:, i * aw:(i + 1) * aw], dil) for i in range(3))
        offs[dil] = (0, 0, 0)
    o_parts, lse_parts = [], []
    for dil in DILATIONS:
        o_p, lse_p = _attn_fwd(f"attn_fwd_d{dil}", *qkv[dil], offs[dil], aw, dil * aw)
        o_parts.append(o_p.reshape(s, aw))
        lse_parts.append(lse_p.reshape(s, aw))
    o_attn, lse = _attn_combine(o_parts, lse_parts)
    p, pw, y = _pool_fwd(h, wp, pool_scale, aw)
    ya, yp, merged = _branch_merge(o_attn, y, wba_st, wbp_st, h, aw, d)
    xhat1, rstd1, x1b = _mix_norm(merged, w_out_full, x2d, ln_mix_g, ln_mix_b)
    r = _ff_up(x1b, w1_st)
    dz2, dz2b, st2 = _ff_down_loss(r, w2_full, xhat1, ln_mix_g, ln_mix_b, ln_ff_g, ln_ff_b, target)

    da = _ff_down_bwd(dz2b, w2_full, r)
    g_w2 = _wgrad("wgrad_ff2", r, dz2b, d).reshape(N_CHIPS, dff // N_CHIPS, d)
    g_w1 = _wgrad("wgrad_ff1", x1b, da, dff // N_CHIPS)
    dz1, dz1b, st1 = _ff_up_bwd(da, w1_st, dz2, xhat1, rstd1, ln_mix_g)
    dya, dyp, dgate = _mix_bwd(dz1b, w_out_full, h, ya, yp, aw)
    g_wout = _wgrad("wgrad_out", merged, dz1b, d).reshape(N_CHIPS, d // N_CHIPS, d)
    g_wba = _wgrad("wgrad_branch_attn", o_attn, dya, d // N_CHIPS)
    g_wbp = _wgrad("wgrad_branch_pool", y, dyp, d // N_CHIPS)
    do, delta = _attn_out_bwd(dya, wba_st, o_attn)
    dpw, stp = _pool_out_bwd(dyp, wbp_st, pw, pool_scale)
    dwp, du = _pool_bwd(dpw, p, wp)

    dqs, dks, dvs = [], [], []
    for dil in DILATIONS:
        dov, lsev, dlv = (to_view(a, dil) for a in (do, lse, delta))
        args = (*qkv[dil], dov, lsev, dlv, offs[dil], aw, dil * aw)
        dqs.append(_attn_bwd_q(f"attn_bwd_q_d{dil}", *args).reshape(s, aw))
        dk_p, dv_p = _attn_bwd_kv(f"attn_bwd_kv_d{dil}", *args)
        dks.append(dk_p.reshape(s, aw))
        dvs.append(dv_p.reshape(s, aw))
    dqkvu = _qkvu_grad(dqs, dks, dvs, du, cos_t, sin_t)
    g_win_a = _wgrad("wgrad_in_qkvu", xb, dqkvu, d)
    g_win_b = _wgrad("wgrad_in_gates", xb, dgate, d)
    g_win = jnp.concatenate([g_win_a, g_win_b], axis=0)
    dx_a = _in_proj_bwd_x("in_proj_bwd_qkvu", dqkvu, w_in_st, 0, dz1, ALPHA)
    grad_x = _in_proj_bwd_x("in_proj_bwd_gates", dgate, w_in_st, 2, dx_a, 1.0)

    g_wp = dwp.reshape(ng, N_CHIPS, rows_sh, pgw).transpose(1, 0, 2, 3).reshape(
        N_CHIPS, ng * rows_sh, pgw).astype(BF16)
    grads = {"w_in": g_win, "w_pool": g_wp, "w_branch_attn": g_wba, "w_branch_pool": g_wbp,
             "w_out": g_wout, "w_ff1": g_w1, "w_ff2": g_w2}
    scattered = _scatter_grads([grads[k] for k in names])
    lands, owns = scattered[:len(names)], scattered[len(names):]
    parts = [_sum_slabs(f"sum_slabs_{k}", owns[i], lands[i]) for i, k in enumerate(names)]
    other = _exchange_cores(parts)
    tot2, tot1, totp = _allreduce_stats([st2, st1, stp])

    moments = {"w_in": (m_w_in, v_w_in), "w_pool": (m_w_pool, v_w_pool),
               "w_branch_attn": (m_w_branch_attn, v_w_branch_attn),
               "w_branch_pool": (m_w_branch_pool, v_w_branch_pool), "w_out": (m_w_out, v_w_out),
               "w_ff1": (m_w_ff1, v_w_ff1), "w_ff2": (m_w_ff2, v_w_ff2)}
    originals = {"w_in": w_in, "w_pool": w_pool, "w_branch_attn": w_branch_attn,
                 "w_branch_pool": w_branch_pool, "w_out": w_out, "w_ff1": w_ff1, "w_ff2": w_ff2}
    res = {}
    for i, k in enumerate(names):
        shape2d = big[k].shape
        mk, vk = (a.reshape(shape2d) for a in moments[k])
        outs = _adamw(f"adamw_{k}", big[k], mk, vk, [parts[i], other[i]])
        res[k] = [o.reshape(originals[k].shape) for o in outs]

    def pad_d(a):
        return jnp.pad(a, ((0, 0), (0, d - a.shape[1])))

    small = ["ln_mix_g", "ln_mix_b", "ln_ff_g", "ln_ff_b", "pool_scale"]
    small_w = {"ln_mix_g": ln_mix_g, "ln_mix_b": ln_mix_b, "ln_ff_g": ln_ff_g, "ln_ff_b": ln_ff_b,
               "pool_scale": pool_scale}
    small_m = {"ln_mix_g": m_ln_mix_g, "ln_mix_b": m_ln_mix_b, "ln_ff_g": m_ln_ff_g,
               "ln_ff_b": m_ln_ff_b, "pool_scale": m_pool_scale}
    small_v = {"ln_mix_g": v_ln_mix_g, "ln_mix_b": v_ln_mix_b, "ln_ff_g": v_ln_ff_g,
               "ln_ff_b": v_ln_ff_b, "pool_scale": v_pool_scale}
    small_g = [tot1[0:1], tot1[1:2], tot2[0:1], tot2[1:2], pad_d(totp[0:1])]

    def pack(rows):
        return jnp.concatenate([pad_d(a) for a in rows] + [jnp.zeros((8 - len(rows), d), F32)], axis=0)

    outs = _adamw("adamw_small", pack([small_w[k] for k in small]), pack([small_m[k] for k in small]),
                  pack([small_v[k] for k in small]), [pack(small_g)])
    for i, k in enumerate(small):
        res[k] = [o[i:i + 1, :small_w[k].shape[1]] for o in outs]
    loss = tot2[2, 0]

    order = ["w_in", "w_pool", "pool_scale", "w_branch_attn", "w_branch_pool", "w_out", "ln_mix_g",
             "ln_mix_b", "w_ff1", "w_ff2", "ln_ff_g", "ln_ff_b"]
    result = [loss, grad_x[None]]
    for idx in range(4):
        result += [res[k][idx] for k in order]
    return tuple(result)
```

```python
import jax
import jax.numpy as jnp
from jax import lax
from jax.experimental import pallas as pl
from jax.experimental.pallas import tpu as pltpu

F32 = jnp.float32
BF16 = jnp.bfloat16
MESH = pl.DeviceIdType.MESH

HEAD_DIM = 128
SUB_BLOCK = 128
DILATIONS = (1, 4, 16)
POOL_WINDOWS = (2, 4, 8, 16)
POOL_HALO = 16
ROPE_THETA = 10000.0
LN_EPS = 1e-5
ALPHA = 2.0 ** 0.25
ADAM_LR, ADAM_B1, ADAM_B2, ADAM_EPS, ADAM_WD, ADAM_STEP = 0.001, 0.9, 0.999, 1e-08, 0.01, 10
NEG = -1e30
N_CHIPS = 4
VMEM_LIMIT = 56 * 1024 * 1024
EPILOGUE_ROWS = 128


def _params(sem=None, vmem=VMEM_LIMIT):
    kw = {"vmem_limit_bytes": vmem}
    if sem is not None:
        kw["dimension_semantics"] = sem
    return pltpu.CompilerParams(**kw)


def _dot(a, b, contract):
    return lax.dot_general(a, b, (contract, ((), ())), preferred_element_type=F32)


ANY_SPEC = pl.BlockSpec(memory_space=pl.ANY)
NN = ((1,), (0,))
NT = ((1,), (1,))
TN = ((0,), (0,))


def _mm(name, grid, a, a_spec, b, b_spec, contract, extras, extra_specs, out_shape, out_specs,
        epilogue, acc_shape, acc_as_ref=False):
    nk = grid[2]
    n_ex = len(extras)
    n_out = len(out_shape)

    def body(*refs):
        a_ref, b_ref = refs[0], refs[1]
        ex = refs[2:2 + n_ex]
        outs = refs[2 + n_ex:2 + n_ex + n_out]
        part = _dot(a_ref[...], b_ref[...], contract)
        if nk == 1:
            epilogue(part, ex, outs)
        else:
            acc = refs[2 + n_ex + n_out]
            k = pl.program_id(2)

            @pl.when(k == 0)
            def _():
                acc[...] = part

            @pl.when(k > 0)
            def _():
                acc[...] += part

            @pl.when(k == nk - 1)
            def _():
                epilogue(acc if acc_as_ref else acc[...], ex, outs)

    scratch = [pltpu.VMEM(acc_shape, F32)] if nk > 1 else []
    return pl.pallas_call(
        body, name=name, grid=grid, in_specs=[a_spec, b_spec, *extra_specs], out_specs=out_specs,
        out_shape=out_shape, scratch_shapes=scratch,
        compiler_params=_params(("parallel", "parallel", "arbitrary")),
    )(a, b, *extras)


def _stats_rows(rows, width):
    idx = lax.broadcasted_iota(jnp.int32, (8, width), 0)
    out = jnp.zeros((8, width), F32)
    for r, v in enumerate(rows):
        out = jnp.where(idx == r, jnp.broadcast_to(v, (8, width)), out)
    return out


def _layer_norm_fwd(z):
    mu = jnp.mean(z, axis=-1, keepdims=True)
    zc = z - mu
    var = jnp.mean(zc * zc, axis=-1, keepdims=True)
    rstd = lax.rsqrt(var + LN_EPS)
    return zc * rstd, rstd


def _layer_norm_bwd(dy, xhat, rstd, g):
    dxh = dy * g
    m1 = jnp.mean(dxh, axis=-1, keepdims=True)
    m2 = jnp.mean(dxh * xhat, axis=-1, keepdims=True)
    return rstd * (dxh - m1 - xhat * m2)


def _in_proj(xb, w_in_st, cos_t, sin_t, aw):
    s, d = xb.shape
    n_sh = w_in_st.shape[2]
    tm, tn = min(s, 1024), aw
    per = n_sh // tn
    grid = (s // tm, (N_CHIPS * n_sh) // tn, 1)

    def epilogue(acc, ex, outs):
        cos_ref, sin_ref = ex
        (h_ref,) = outs
        seg = pl.program_id(1)

        @pl.when(seg < 2)
        def _():
            cos, sin = cos_ref[...], sin_ref[...]
            for hh in range(tn // HEAD_DIM):
                sl = slice(hh * HEAD_DIM, (hh + 1) * HEAD_DIM)
                t = acc[:, sl]
                h_ref[:, sl] = (t * cos + pltpu.roll(t, HEAD_DIM // 2, 1) * sin).astype(BF16)

        @pl.when((seg >= 2) & (seg < 4))
        def _():
            h_ref[...] = acc.astype(BF16)

        @pl.when(seg >= 4)
        def _():
            h_ref[...] = jax.nn.sigmoid(acc).astype(BF16)

    return _mm(
        "in_proj", grid, xb, pl.BlockSpec((tm, d), lambda i, j, k: (i, 0)),
        w_in_st, pl.BlockSpec((None, d, tn), lambda i, j, k: (j // per, 0, j % per)), NN,
        [cos_t, sin_t], [pl.BlockSpec((tm, HEAD_DIM), lambda i, j, k: (i, 0))] * 2,
        [jax.ShapeDtypeStruct((s, N_CHIPS * n_sh), BF16)],
        [pl.BlockSpec((tm, tn), lambda i, j, k: (i, j))], epilogue, None)[0]


def _band_masks(block_idx):
    qi = lax.broadcasted_iota(jnp.int32, (SUB_BLOCK, 2 * SUB_BLOCK), 0)
    kj = lax.broadcasted_iota(jnp.int32, (SUB_BLOCK, 2 * SUB_BLOCK), 1)
    first_key = jnp.where(block_idx > 0, 0, SUB_BLOCK)
    return (kj >= qi) & (kj <= qi + SUB_BLOCK) & (kj >= first_key)


def _attn_fwd(name, q, k, v, offs, cw, width):
    m = q.shape[0]
    nh = cw // HEAD_DIM
    grid = (m // SUB_BLOCK, width // cw)
    scale = HEAD_DIM ** -0.5

    def body(q_ref, kp_ref, kc_ref, vp_ref, vc_ref, o_ref, lse_ref):
        valid = _band_masks(pl.program_id(0))
        for hh in range(nh):
            sl = slice(hh * HEAD_DIM, (hh + 1) * HEAD_DIM)
            kcat = jnp.concatenate([kp_ref[:, sl], kc_ref[:, sl]], axis=0)
            vcat = jnp.concatenate([vp_ref[:, sl], vc_ref[:, sl]], axis=0)
            sc = _dot(q_ref[:, sl], kcat, NT) * scale
            sc = jnp.where(valid, sc, NEG)
            mx = jnp.max(sc, axis=-1, keepdims=True)
            p = jnp.exp(sc - mx)
            l = jnp.sum(p, axis=-1, keepdims=True)
            o = _dot(p.astype(BF16), vcat, NN) / l
            o_ref[:, sl] = o.astype(BF16)
            lse_ref[:, sl] = jnp.broadcast_to(mx + jnp.log(l), (SUB_BLOCK, HEAD_DIM))

    def cur(off):
        return pl.BlockSpec((SUB_BLOCK, cw), lambda n, j: (n, j + off))

    def prev(off):
        return pl.BlockSpec((SUB_BLOCK, cw), lambda n, j: (jnp.maximum(n - 1, 0), j + off))

    return pl.pallas_call(
        body, name=name, grid=grid,
        in_specs=[cur(offs[0]), prev(offs[1]), cur(offs[1]), prev(offs[2]), cur(offs[2])],
        out_specs=[cur(0), cur(0)],
        out_shape=[jax.ShapeDtypeStruct((m, width), BF16), jax.ShapeDtypeStruct((m, width), F32)],
        compiler_params=_params(("parallel", "parallel")),
    )(q, k, k, v, v)


def _attn_combine(os_, lses):
    s, aw = os_[0].shape
    tm = min(s, 512)

    def body(o1, o2, o3, l1, l2, l3, o_ref, lse_ref):
        a, b, c = l1[...], l2[...], l3[...]
        mx = jnp.maximum(jnp.maximum(a, b), c)
        ea, eb, ec = jnp.exp(a - mx), jnp.exp(b - mx), jnp.exp(c - mx)
        tot = ea + eb + ec
        o = (ea * o1[...].astype(F32) + eb * o2[...].astype(F32) + ec * o3[...].astype(F32)) / tot
        o_ref[...] = o.astype(BF16)
        lse_ref[...] = mx + jnp.log(tot)

    spec = pl.BlockSpec((tm, aw), lambda i: (i, 0))
    return pl.pallas_call(
        body, name="attn_combine", grid=(s // tm,), in_specs=[spec] * 6, out_specs=[spec, spec],
        out_shape=[jax.ShapeDtypeStruct((s, aw), BF16), jax.ShapeDtypeStruct((s, aw), F32)],
        compiler_params=_params(("parallel",)),
    )(*os_, *lses)


def _pool_counts(tm, rows, pgw, row0):
    t = lax.broadcasted_iota(jnp.int32, (rows, len(POOL_WINDOWS) * pgw), 0) + row0
    col = lax.broadcasted_iota(jnp.int32, (rows, len(POOL_WINDOWS) * pgw), 1)
    w = jnp.full((rows, len(POOL_WINDOWS) * pgw), POOL_WINDOWS[0], jnp.int32)
    for g in range(1, len(POOL_WINDOWS)):
        w = jnp.where(col >= g * pgw, POOL_WINDOWS[g], w)
    return jnp.minimum(t + 1, w).astype(F32)


def _window_sums(xs, direction, pgw):
    rows = xs.shape[0]
    acc = xs
    out = None
    col = lax.broadcasted_iota(jnp.int32, xs.shape, 1)
    for g, w in enumerate(POOL_WINDOWS):
        sh = w // 2
        acc = acc + pltpu.roll(acc, sh if direction > 0 else rows - sh, 0)
        out = acc if out is None else jnp.where(col >= g * pgw, acc, out)
    return out


def _pool_fwd(h, wp, scale, aw):
    s = h.shape[0]
    pw_ = aw
    pgw = pw_ // len(POOL_WINDOWS)
    tm = min(s, 512)
    hb = tm // POOL_HALO

    def body(u_ref, halo_ref, wp_ref, sc_ref, p_ref, pw_ref, y_ref):
        i = pl.program_id(0)
        u = u_ref[...].astype(F32)
        halo = halo_ref[...].astype(F32) * jnp.where(i > 0, 1.0, 0.0)
        xs = jnp.concatenate([halo, u], axis=0)
        sums = _window_sums(xs, +1, pgw)[POOL_HALO:]
        p = (sums / _pool_counts(tm, tm, pgw, i * tm) - u).astype(BF16)
        p_ref[...] = p
        sc = sc_ref[...]
        for g in range(len(POOL_WINDOWS)):
            sl = slice(g * pgw, (g + 1) * pgw)
            pw = _dot(p[:, sl], wp_ref[g], NN)
            pw_ref[:, sl] = pw.astype(BF16)
            y_ref[:, sl] = (pw * sc[:, sl]).astype(BF16)

    out = jax.ShapeDtypeStruct((s, pw_), BF16)
    row = pl.BlockSpec((tm, pw_), lambda i: (i, 0))
    return pl.pallas_call(
        body, name="pool_fwd", grid=(s // tm,),
        in_specs=[pl.BlockSpec((tm, pw_), lambda i: (i, 3)),
                  pl.BlockSpec((POOL_HALO, pw_), lambda i: (jnp.maximum(i * hb - 1, 0), 3)),
                  pl.BlockSpec(wp.shape, lambda i: (0, 0, 0)),
                  pl.BlockSpec((1, pw_), lambda i: (0, 0))],
        out_specs=[row, row, row], out_shape=[out, out, out],
        compiler_params=_params(("parallel",)),
    )(h, h, wp, scale)


def _branch_merge(o_attn, y, wba_st, wbp_st, h, aw, d):
    s = o_attn.shape[0]
    tn = wba_st.shape[2]
    tm = min(s, 1024)
    ga0 = 4 * aw // tn
    gp0 = (4 * aw + d) // tn

    def body(o_ref, y_ref, wa_ref, wp_ref, sga_ref, sgp_ref, ya_ref, yp_ref, mg_ref):
        ya = _dot(o_ref[...], wa_ref[...], NN)
        yp = _dot(y_ref[...], wp_ref[...], NN)
        ya_ref[...] = ya.astype(BF16)
        yp_ref[...] = yp.astype(BF16)
        mg_ref[...] = (sga_ref[...].astype(F32) * ya + sgp_ref[...].astype(F32) * yp).astype(BF16)

    out = jax.ShapeDtypeStruct((s, d), BF16)
    blk = pl.BlockSpec((tm, tn), lambda i, j: (i, j))
    return pl.pallas_call(
        body, name="branch_merge", grid=(s // tm, N_CHIPS),
        in_specs=[pl.BlockSpec((tm, aw), lambda i, j: (i, 0)),
                  pl.BlockSpec((tm, aw), lambda i, j: (i, 0)),
                  pl.BlockSpec((None, aw, tn), lambda i, j: (j, 0, 0)),
                  pl.BlockSpec((None, aw, tn), lambda i, j: (j, 0, 0)),
                  pl.BlockSpec((tm, tn), lambda i, j: (i, j + ga0)),
                  pl.BlockSpec((tm, tn), lambda i, j: (i, j + gp0))],
        out_specs=[blk, blk, blk], out_shape=[out, out, out],
        compiler_params=_params(("parallel", "parallel")),
    )(o_attn, y, wba_st, wbp_st, h, h)


def _mix_norm(merged, w_out, x, g1, b1):
    s, d = x.shape
    tm = min(s, 256)

    def epilogue(acc, ex, outs):
        x_ref, g_ref, b_ref = ex
        xh_ref, rs_ref, xb_ref = outs
        xhat, rstd = _layer_norm_fwd(ALPHA * x_ref[...] + acc)
        xh_ref[...] = xhat
        rs_ref[...] = rstd
        xb_ref[...] = (xhat * g_ref[...] + b_ref[...]).astype(BF16)

    row = pl.BlockSpec((tm, d), lambda i, j, k: (i, 0))
    vec = pl.BlockSpec((1, d), lambda i, j, k: (0, 0))
    return _mm(
        "mix_norm", (s // tm, 1, 1), merged, row, w_out, pl.BlockSpec((d, d), lambda i, j, k: (0, 0)),
        NN, [x, g1, b1], [row, vec, vec],
        [jax.ShapeDtypeStruct((s, d), F32), jax.ShapeDtypeStruct((s, 1), F32),
         jax.ShapeDtypeStruct((s, d), BF16)],
        [row, pl.BlockSpec((tm, 1), lambda i, j, k: (i, 0)), row], epilogue, None)


def _ff_up(x1b, w1_st):
    s, d = x1b.shape
    n_sh = w1_st.shape[2]
    tm, tn = min(s, 1024), min(n_sh, 1024)
    per = n_sh // tn

    def epilogue(acc, ex, outs):
        r = jnp.maximum(acc, 0.0)
        outs[0][...] = (r * r).astype(BF16)

    return _mm(
        "ff_up", (s // tm, N_CHIPS * per, 1), x1b, pl.BlockSpec((tm, d), lambda i, j, k: (i, 0)),
        w1_st, pl.BlockSpec((None, d, tn), lambda i, j, k: (j // per, 0, j % per)), NN, [], [],
        [jax.ShapeDtypeStruct((s, N_CHIPS * n_sh), BF16)],
        [pl.BlockSpec((tm, tn), lambda i, j, k: (i, j))], epilogue, None)[0]


def _ff_down_loss(r, w2, xhat1, g1, b1, g2, b2, target):
    s, d = xhat1.shape
    dff = r.shape[1]
    tm, tk = min(s, 512), min(dff, 1024)
    ch = min(tm, EPILOGUE_ROWS)

    def epilogue(acc_ref, ex, outs):
        xh1_ref, g1_ref, b1_ref, g2_ref, b2_ref, t_ref = ex
        dz_ref, dzb_ref, st_ref = outs
        g1v, b1v, g2v, b2v = g1_ref[...], b1_ref[...], g2_ref[...], b2_ref[...]
        dg = db = loss = None
        for c in range(tm // ch):
            rows = slice(c * ch, (c + 1) * ch)
            x1 = xh1_ref[rows, :] * g1v + b1v
            xhat2, rstd2 = _layer_norm_fwd(ALPHA * x1 + acc_ref[rows, :])
            err = xhat2 * g2v + b2v - t_ref[rows, :]
            dy = err * (1.0 / d)
            dz = _layer_norm_bwd(dy, xhat2, rstd2, g2v)
            dz_ref[rows, :] = dz
            dzb_ref[rows, :] = dz.astype(BF16)
            parts = (jnp.sum(dy * xhat2, axis=0, keepdims=True), jnp.sum(dy, axis=0, keepdims=True),
                     jnp.sum(jnp.sum(err * err, axis=-1, keepdims=True), axis=0, keepdims=True))
            dg, db, loss = parts if c == 0 else (dg + parts[0], db + parts[1], loss + parts[2])
        st_ref[...] = _stats_rows([dg, db, jnp.broadcast_to((0.5 / d) * loss, (1, d))], d)

    row = pl.BlockSpec((tm, d), lambda i, j, k: (i, 0))
    vec = pl.BlockSpec((1, d), lambda i, j, k: (0, 0))
    return _mm(
        "ff_down_loss", (s // tm, 1, dff // tk), r, pl.BlockSpec((tm, tk), lambda i, j, k: (i, k)),
        w2, pl.BlockSpec((tk, d), lambda i, j, k: (k, 0)), NN,
        [xhat1, g1, b1, g2, b2, target], [row, vec, vec, vec, vec, row],
        [jax.ShapeDtypeStruct((s, d), F32), jax.ShapeDtypeStruct((s, d), BF16),
         jax.ShapeDtypeStruct((s // tm, 8, d), F32)],
        [row, row, pl.BlockSpec((None, 8, d), lambda i, j, k: (i, 0, 0))], epilogue, (tm, d),
        acc_as_ref=True)


def _ff_down_bwd(dz2b, w2, r):
    s, d = dz2b.shape
    dff = r.shape[1]
    tm, tn = min(s, 1024), min(dff, 1024)

    def epilogue(acc, ex, outs):
        outs[0][...] = (acc * (2.0 * jnp.sqrt(ex[0][...].astype(F32)))).astype(BF16)

    blk = pl.BlockSpec((tm, tn), lambda i, j, k: (i, j))
    return _mm(
        "ff_down_bwd", (s // tm, dff // tn, 1), dz2b, pl.BlockSpec((tm, d), lambda i, j, k: (i, 0)),
        w2, pl.BlockSpec((tn, d), lambda i, j, k: (j, 0)), NT, [r], [blk],
        [jax.ShapeDtypeStruct((s, dff), BF16)], [blk], epilogue, None)[0]


def _wgrad(name, a, g, n_sh):
    s, rows = a.shape
    cols = g.shape[1]
    tm, tn, tk = min(rows, 2048), min(cols, 1024), min(s, 1024)
    if tn >= n_sh:
        span = tn // n_sh
        out_spec = pl.BlockSpec((span, tm, n_sh), lambda i, j, k: (j, i, 0))

        def epilogue(acc_ref, ex, outs):
            for sh in range(span):
                outs[0][sh] = acc_ref[:, sh * n_sh:(sh + 1) * n_sh].astype(BF16)
    else:
        per = n_sh // tn
        out_spec = pl.BlockSpec((None, tm, tn), lambda i, j, k: (j // per, i, j % per))

        def epilogue(acc_ref, ex, outs):
            outs[0][...] = acc_ref[...].astype(BF16)

    return _mm(
        name, (rows // tm, cols // tn, s // tk), a, pl.BlockSpec((tk, tm), lambda i, j, k: (k, i)),
        g, pl.BlockSpec((tk, tn), lambda i, j, k: (k, j)), TN, [], [],
        [jax.ShapeDtypeStruct((cols // n_sh, rows, n_sh), BF16)], [out_spec], epilogue,
        (tm, tn), acc_as_ref=True)[0]


def _ff_up_bwd(da, w1_st, dz2, xhat1, rstd1, g1):
    s, d = dz2.shape
    n_sh = w1_st.shape[2]
    tm, tk = min(s, 512), min(n_sh, 1024)
    per = n_sh // tk
    ch = min(tm, EPILOGUE_ROWS)

    def epilogue(acc_ref, ex, outs):
        dz2_ref, xh_ref, rs_ref, g_ref = ex
        dz_ref, dzb_ref, st_ref = outs
        gv = g_ref[...]
        dg = db = None
        for c in range(tm // ch):
            rows = slice(c * ch, (c + 1) * ch)
            dx1 = ALPHA * dz2_ref[rows, :] + acc_ref[rows, :]
            xhat = xh_ref[rows, :]
            dz = _layer_norm_bwd(dx1, xhat, rs_ref[rows, :], gv)
            dz_ref[rows, :] = dz
            dzb_ref[rows, :] = dz.astype(BF16)
            parts = (jnp.sum(dx1 * xhat, axis=0, keepdims=True), jnp.sum(dx1, axis=0, keepdims=True))
            dg, db = parts if c == 0 else (dg + parts[0], db + parts[1])
        st_ref[...] = _stats_rows([dg, db], d)

    row = pl.BlockSpec((tm, d), lambda i, j, k: (i, 0))
    return _mm(
        "ff_up_bwd", (s // tm, 1, N_CHIPS * per), da, pl.BlockSpec((tm, tk), lambda i, j, k: (i, k)),
        w1_st, pl.BlockSpec((None, d, tk), lambda i, j, k: (k // per, 0, k % per)), NT,
        [dz2, xhat1, rstd1, g1],
        [row, row, pl.BlockSpec((tm, 1), lambda i, j, k: (i, 0)), pl.BlockSpec((1, d), lambda i, j, k: (0, 0))],
        [jax.ShapeDtypeStruct((s, d), F32), jax.ShapeDtypeStruct((s, d), BF16),
         jax.ShapeDtypeStruct((s // tm, 8, d), F32)],
        [row, row, pl.BlockSpec((None, 8, d), lambda i, j, k: (i, 0, 0))], epilogue, (tm, d),
        acc_as_ref=True)


def _mix_bwd(dz1b, w_out, h, ya, yp, aw):
    s, d = dz1b.shape
    tm = min(s, 256)
    gblk = 4 * aw // d

    def epilogue(acc, ex, outs):
        sga_ref, sgp_ref, ya_ref, yp_ref = ex
        dya_ref, dyp_ref, dg_ref = outs
        sga, sgp = sga_ref[...].astype(F32), sgp_ref[...].astype(F32)
        dya_ref[...] = (acc * sga).astype(BF16)
        dyp_ref[...] = (acc * sgp).astype(BF16)
        dg_ref[:, :d] = (acc * ya_ref[...].astype(F32) * (sga * (1.0 - sga))).astype(BF16)
        dg_ref[:, d:] = (acc * yp_ref[...].astype(F32) * (sgp * (1.0 - sgp))).astype(BF16)

    row = pl.BlockSpec((tm, d), lambda i, j, k: (i, 0))
    return _mm(
        "mix_bwd", (s // tm, 1, 1), dz1b, row, w_out, pl.BlockSpec((d, d), lambda i, j, k: (0, 0)), NT,
        [h, h, ya, yp],
        [pl.BlockSpec((tm, d), lambda i, j, k: (i, gblk)),
         pl.BlockSpec((tm, d), lambda i, j, k: (i, gblk + 1)), row, row],
        [jax.ShapeDtypeStruct((s, d), BF16), jax.ShapeDtypeStruct((s, d), BF16),
         jax.ShapeDtypeStruct((s, 2 * d), BF16)],
        [row, row, pl.BlockSpec((tm, 2 * d), lambda i, j, k: (i, 0))], epilogue, None)


def _branch_in_bwd(name, dyb, wb_st, epilogue, extras, extra_specs, out_shape, out_specs):
    s = dyb.shape[0]
    aw, tk = wb_st.shape[1], wb_st.shape[2]
    tm = min(s, 1024)
    return _mm(
        name, (s // tm, 1, N_CHIPS), dyb, pl.BlockSpec((tm, tk), lambda i, j, k: (i, k)),
        wb_st, pl.BlockSpec((None, aw, tk), lambda i, j, k: (k, 0, 0)), NT,
        extras, extra_specs, out_shape, out_specs, epilogue, (tm, aw)), tm


def _attn_out_bwd(dya, wba_st, o_attn):
    s, aw = o_attn.shape

    def epilogue(acc, ex, outs):
        o = ex[0][...].astype(F32)
        do_ref, dl_ref = outs
        do_ref[...] = acc.astype(BF16)
        for hh in range(aw // HEAD_DIM):
            sl = slice(hh * HEAD_DIM, (hh + 1) * HEAD_DIM)
            dl = jnp.sum(acc[:, sl] * o[:, sl], axis=-1, keepdims=True)
            dl_ref[:, sl] = jnp.broadcast_to(dl, (acc.shape[0], HEAD_DIM))

    tm = min(s, 1024)
    row = pl.BlockSpec((tm, aw), lambda i, j, k: (i, 0))
    return _branch_in_bwd(
        "attn_out_bwd", dya, wba_st, epilogue, [o_attn], [row],
        [jax.ShapeDtypeStruct((s, aw), BF16), jax.ShapeDtypeStruct((s, aw), F32)], [row, row])[0]


def _pool_out_bwd(dyp, wbp_st, pw, scale):
    s, pw_ = pw.shape
    tm = min(s, 1024)

    def epilogue(acc, ex, outs):
        pw_ref, sc_ref = ex
        dpw_ref, st_ref = outs
        dpw_ref[...] = (acc * sc_ref[...]).astype(BF16)
        st_ref[...] = _stats_rows([jnp.sum(acc * pw_ref[...].astype(F32), axis=0, keepdims=True)], pw_)

    row = pl.BlockSpec((tm, pw_), lambda i, j, k: (i, 0))
    return _branch_in_bwd(
        "pool_out_bwd", dyp, wbp_st, epilogue, [pw, scale],
        [row, pl.BlockSpec((1, pw_), lambda i, j, k: (0, 0))],
        [jax.ShapeDtypeStruct((s, pw_), BF16), jax.ShapeDtypeStruct((s // tm, 8, pw_), F32)],
        [row, pl.BlockSpec((None, 8, pw_), lambda i, j, k: (i, 0, 0))])[0]


def _pool_bwd(dpw, p, wp):
    s, pw_ = p.shape
    ng = len(POOL_WINDOWS)
    pgw = pw_ // ng
    tm = min(s, 512)
    hb = tm // POOL_HALO
    nblk = s // tm

    def body(dpw_ref, nxt_ref, p_ref, wp_ref, dwp_ref, du_ref):
        i = pl.program_id(0)
        nxt = (nxt_ref[...].astype(F32) * jnp.where(i < nblk - 1, 1.0, 0.0)).astype(BF16)
        dpw_all = jnp.concatenate([dpw_ref[...], nxt], axis=0)

        @pl.when(i == 0)
        def _():
            dwp_ref[...] = jnp.zeros_like(dwp_ref)

        dps = []
        for g in range(ng):
            sl = slice(g * pgw, (g + 1) * pgw)
            dwp_ref[g] += _dot(p_ref[:, sl], dpw_ref[:, sl], TN)
            dps.append(_dot(dpw_all[:, sl], wp_ref[g], NT))
        dp = jnp.concatenate(dps, axis=1)
        dpn = dp / _pool_counts(tm, tm + POOL_HALO, pgw, i * tm)
        du_ref[...] = (_window_sums(dpn, -1, pgw)[:tm] - dp[:tm]).astype(BF16)

    row = pl.BlockSpec((tm, pw_), lambda i: (i, 0))
    full = pl.BlockSpec((ng, pgw, pgw), lambda i: (0, 0, 0))
    return pl.pallas_call(
        body, name="pool_bwd", grid=(nblk,),
        in_specs=[row, pl.BlockSpec((POOL_HALO, pw_), lambda i: (jnp.minimum((i + 1) * hb, s // POOL_HALO - 1), 0)),
                  row, full],
        out_specs=[full, row],
        out_shape=[jax.ShapeDtypeStruct((ng, pgw, pgw), F32), jax.ShapeDtypeStruct((s, pw_), BF16)],
        compiler_params=_params(("arbitrary",)),
    )(dpw, dpw, p, wp)


def _attn_bwd_q(name, q, k, v, do, lse, delta, offs, cw, width):
    m = do.shape[0]
    nh = cw // HEAD_DIM
    grid = (m // SUB_BLOCK, width // cw)
    scale = HEAD_DIM ** -0.5

    def body(q_ref, kp_ref, kc_ref, vp_ref, vc_ref, do_ref, lse_ref, dl_ref, dq_ref):
        valid = _band_masks(pl.program_id(0))
        for hh in range(nh):
            sl = slice(hh * HEAD_DIM, (hh + 1) * HEAD_DIM)
            kcat = jnp.concatenate([kp_ref[:, sl], kc_ref[:, sl]], axis=0)
            vcat = jnp.concatenate([vp_ref[:, sl], vc_ref[:, sl]], axis=0)
            lse2 = jnp.concatenate([lse_ref[:, sl]] * 2, axis=1)
            dl2 = jnp.concatenate([dl_ref[:, sl]] * 2, axis=1)
            sc = _dot(q_ref[:, sl], kcat, NT) * scale
            p = jnp.where(valid, jnp.exp(jnp.where(valid, sc, NEG) - lse2), 0.0)
            dp = _dot(do_ref[:, sl], vcat, NT)
            ds = (p * (dp - dl2)).astype(BF16)
            dq_ref[:, sl] = (_dot(ds, kcat, NN) * scale).astype(BF16)

    def cur(off):
        return pl.BlockSpec((SUB_BLOCK, cw), lambda n, j: (n, j + off))

    def prev(off):
        return pl.BlockSpec((SUB_BLOCK, cw), lambda n, j: (jnp.maximum(n - 1, 0), j + off))

    return pl.pallas_call(
        body, name=name, grid=grid,
        in_specs=[cur(offs[0]), prev(offs[1]), cur(offs[1]), prev(offs[2]), cur(offs[2]),
                  cur(0), cur(0), cur(0)],
        out_specs=cur(0), out_shape=jax.ShapeDtypeStruct((m, width), BF16),
        compiler_params=_params(("parallel", "parallel")),
    )(q, k, k, v, v, do, lse, delta)


def _attn_bwd_kv(name, q, k, v, do, lse, delta, offs, cw, width):
    m = do.shape[0]
    nh = cw // HEAD_DIM
    nblk = m // SUB_BLOCK
    grid = (nblk, width // cw)
    scale = HEAD_DIM ** -0.5

    def body(qc_ref, qn_ref, k_ref, v_ref, doc_ref, don_ref, lsec_ref, lsen_ref, dlc_ref, dln_ref,
             dk_ref, dv_ref):
        no_next = jnp.where(pl.program_id(0) < nblk - 1, 0, 4 * SUB_BLOCK)
        r = lax.broadcasted_iota(jnp.int32, (2 * SUB_BLOCK, SUB_BLOCK), 0)
        kj = lax.broadcasted_iota(jnp.int32, (2 * SUB_BLOCK, SUB_BLOCK), 1)
        valid = ((r < SUB_BLOCK) & (kj <= r)) | ((r >= SUB_BLOCK) & (kj >= r - SUB_BLOCK + no_next))
        for hh in range(nh):
            sl = slice(hh * HEAD_DIM, (hh + 1) * HEAD_DIM)
            qcat = jnp.concatenate([qc_ref[:, sl], qn_ref[:, sl]], axis=0)
            docat = jnp.concatenate([doc_ref[:, sl], don_ref[:, sl]], axis=0)
            lse2 = jnp.concatenate([lsec_ref[:, sl], lsen_ref[:, sl]], axis=0)
            dl2 = jnp.concatenate([dlc_ref[:, sl], dln_ref[:, sl]], axis=0)
            sc = _dot(qcat, k_ref[:, sl], NT) * scale
            p = jnp.where(valid, jnp.exp(jnp.where(valid, sc, NEG) - lse2), 0.0)
            dp = _dot(docat, v_ref[:, sl], NT)
            ds = (p * (dp - dl2)).astype(BF16)
            dv_ref[:, sl] = _dot(p.astype(BF16), docat, TN).astype(BF16)
            dk_ref[:, sl] = (_dot(ds, qcat, TN) * scale).astype(BF16)

    def cur(off):
        return pl.BlockSpec((SUB_BLOCK, cw), lambda n, j: (n, j + off))

    def nxt(off):
        return pl.BlockSpec((SUB_BLOCK, cw), lambda n, j: (jnp.minimum(n + 1, nblk - 1), j + off))

    out = jax.ShapeDtypeStruct((m, width), BF16)
    return pl.pallas_call(
        body, name=name, grid=grid,
        in_specs=[cur(offs[0]), nxt(offs[0]), cur(offs[1]), cur(offs[2]),
                  cur(0), nxt(0), cur(0), nxt(0), cur(0), nxt(0)],
        out_specs=[cur(0), cur(0)], out_shape=[out, out],
        compiler_params=_params(("parallel", "parallel")),
    )(q, q, k, v, do, do, lse, lse, delta, delta)


def _qkvu_grad(dqs, dks, dvs, du, cos_t, sin_t):
    s, aw = du.shape
    tm = min(s, 512)

    def body(*refs):
        ins, (cos_ref, sin_ref, du_ref, out_ref) = refs[:9], refs[9:]
        cos, sin = cos_ref[...], sin_ref[...]
        for part in range(2):
            tot = sum(r[...].astype(F32) for r in ins[3 * part:3 * part + 3])
            for hh in range(aw // HEAD_DIM):
                sl = slice(hh * HEAD_DIM, (hh + 1) * HEAD_DIM)
                t = tot[:, sl]
                out_ref[:, part * aw + hh * HEAD_DIM:part * aw + (hh + 1) * HEAD_DIM] = (
                    t * cos - pltpu.roll(t, HEAD_DIM // 2, 1) * sin).astype(BF16)
        out_ref[:, 2 * aw:3 * aw] = sum(r[...].astype(F32) for r in ins[6:9]).astype(BF16)
        out_ref[:, 3 * aw:] = du_ref[...]

    row = pl.BlockSpec((tm, aw), lambda i: (i, 0))
    tab = pl.BlockSpec((tm, HEAD_DIM), lambda i: (i, 0))
    return pl.pallas_call(
        body, name="qkvu_grad", grid=(s // tm,), in_specs=[row] * 9 + [tab, tab, row],
        out_specs=pl.BlockSpec((tm, 4 * aw), lambda i: (i, 0)),
        out_shape=jax.ShapeDtypeStruct((s, 4 * aw), BF16),
        compiler_params=_params(("parallel",)),
    )(*dqs, *dks, *dvs, cos_t, sin_t, du)


def _in_proj_bwd_x(name, dh, w_in_st, shard0, base, scale_base):
    s, kdim = dh.shape
    d, n_sh = w_in_st.shape[1], w_in_st.shape[2]
    tm, tk = min(s, 1024), min(n_sh, 512)
    per = n_sh // tk

    ch = min(tm, 2 * EPILOGUE_ROWS)

    def epilogue(acc_ref, ex, outs):
        for c in range(tm // ch):
            rows = slice(c * ch, (c + 1) * ch)
            outs[0][rows, :] = scale_base * ex[0][rows, :] + acc_ref[rows, :]

    row = pl.BlockSpec((tm, d), lambda i, j, k: (i, 0))
    return _mm(
        name, (s // tm, 1, kdim // tk), dh, pl.BlockSpec((tm, tk), lambda i, j, k: (i, k)),
        w_in_st, pl.BlockSpec((None, d, tk), lambda i, j, k: (shard0 + k // per, 0, k % per)), NT,
        [base], [row], [jax.ShapeDtypeStruct((s, d), F32)], [row], epilogue, (tm, d),
        acc_as_ref=True)[0]


def _chip_peers():
    x, y, c = lax.axis_index("x"), lax.axis_index("y"), lax.axis_index("c")
    return x, y, c, [(1 - x, y), (x, 1 - y), (1 - x, 1 - y)]


def _gather_weights(shards):
    n = len(shards)

    def body(*refs):
        ins, outs = refs[:n], refs[n:2 * n]
        send, recv, loc = refs[2 * n:]
        x, y, c, peers = _chip_peers()
        me = 2 * x + y
        local, remote = [], []
        for w in range(n):
            own = pltpu.make_async_copy(ins[w], outs[w].at[me], loc.at[w])
            own.start()
            local.append(own)
            for p, (px, py) in enumerate(peers):
                cp = pltpu.make_async_remote_copy(
                    src_ref=ins[w], dst_ref=outs[w].at[me], send_sem=send.at[w, p],
                    recv_sem=recv.at[w, p], device_id=(px, py, c), device_id_type=MESH)
                cp.start()
                remote.append(cp)
        for w in range(n):
            for p, (px, py) in enumerate(peers):
                pltpu.make_async_remote_copy(
                    src_ref=ins[w], dst_ref=outs[w].at[2 * px + py], send_sem=send.at[w, p],
                    recv_sem=recv.at[w, p], device_id=(px, py, c), device_id_type=MESH).wait_recv()
        for cp in remote:
            cp.wait_send()
        for cp in local:
            cp.wait()

    return pl.pallas_call(
        body, name="gather_weights", in_specs=[ANY_SPEC] * n, out_specs=[ANY_SPEC] * n,
        out_shape=[jax.ShapeDtypeStruct((N_CHIPS, *sh.shape), sh.dtype) for sh in shards],
        scratch_shapes=[pltpu.SemaphoreType.DMA((n, 3)), pltpu.SemaphoreType.DMA((n, 3)),
                        pltpu.SemaphoreType.DMA((n,))],
    )(*shards)


def _scatter_grads(grads):
    n = len(grads)

    def body(*refs):
        ins, lands, owns = refs[:n], refs[n:2 * n], refs[2 * n:3 * n]
        send, recv, loc = refs[3 * n:]
        x, y, c, peers = _chip_peers()
        me = 2 * x + y
        local, remote = [], []
        for w in range(n):
            own = pltpu.make_async_copy(ins[w].at[me], owns[w], loc.at[w])
            own.start()
            local.append(own)
            for p, (px, py) in enumerate(peers):
                cp = pltpu.make_async_remote_copy(
                    src_ref=ins[w].at[2 * px + py], dst_ref=lands[w].at[p], send_sem=send.at[w, p],
                    recv_sem=recv.at[w, p], device_id=(px, py, c), device_id_type=MESH)
                cp.start()
                remote.append(cp)
        for cp in remote:
            cp.wait_recv()
        for cp in remote:
            cp.wait_send()
        for cp in local:
            cp.wait()

    return pl.pallas_call(
        body, name="scatter_grads", in_specs=[ANY_SPEC] * n, out_specs=[ANY_SPEC] * (2 * n),
        out_shape=[jax.ShapeDtypeStruct((3, *g.shape[1:]), g.dtype) for g in grads]
        + [jax.ShapeDtypeStruct(g.shape[1:], g.dtype) for g in grads],
        scratch_shapes=[pltpu.SemaphoreType.DMA((n, 3)), pltpu.SemaphoreType.DMA((n, 3)),
                        pltpu.SemaphoreType.DMA((n,))],
    )(*grads)


def _sum_slabs(name, own, land):
    r, c = own.shape
    tm = min(r, 256)

    def body(own_ref, land_ref, out_ref):
        acc = own_ref[...].astype(F32)
        for p in range(3):
            acc = acc + land_ref[p].astype(F32)
        out_ref[...] = acc

    return pl.pallas_call(
        body, name=name, grid=(r // tm,),
        in_specs=[pl.BlockSpec((tm, c), lambda i: (i, 0)), pl.BlockSpec((3, tm, c), lambda i: (0, i, 0))],
        out_specs=pl.BlockSpec((tm, c), lambda i: (i, 0)), out_shape=jax.ShapeDtypeStruct((r, c), F32),
        compiler_params=_params(("parallel",)),
    )(own, land)


def _exchange_cores(parts):
    n = len(parts)

    def body(*refs):
        ins, outs = refs[:n], refs[n:2 * n]
        send, recv = refs[2 * n:]
        x, y, c = lax.axis_index("x"), lax.axis_index("y"), lax.axis_index("c")
        copies = [pltpu.make_async_remote_copy(
            src_ref=ins[w], dst_ref=outs[w], send_sem=send.at[w], recv_sem=recv.at[w],
            device_id=(x, y, 1 - c), device_id_type=MESH) for w in range(n)]
        for cp in copies:
            cp.start()
        for cp in copies:
            cp.wait()

    return pl.pallas_call(
        body, name="exchange_cores", in_specs=[ANY_SPEC] * n, out_specs=[ANY_SPEC] * n,
        out_shape=[jax.ShapeDtypeStruct(p.shape, p.dtype) for p in parts],
        scratch_shapes=[pltpu.SemaphoreType.DMA((n,)), pltpu.SemaphoreType.DMA((n,))],
    )(*parts)


def _allreduce_stats(stats):
    n = len(stats)

    def body(*refs):
        ins, outs = refs[:n], refs[n:2 * n]
        mine, gath = refs[2 * n:3 * n], refs[3 * n:4 * n]
        send, recv = refs[4 * n:]
        x, y, c = lax.axis_index("x"), lax.axis_index("y"), lax.axis_index("c")
        me = 4 * x + 2 * y + c
        flips = [(bx, by, bc) for bx in (0, 1) for by in (0, 1) for bc in (0, 1)][1:]

        def peer(f):
            return (x + f[0] * (1 - 2 * x), y + f[1] * (1 - 2 * y), c + f[2] * (1 - 2 * c))

        copies = []
        for t in range(n):
            tot = ins[t][0]
            for b in range(1, ins[t].shape[0]):
                tot = tot + ins[t][b]
            mine[t][...] = tot
            gath[t][me] = tot
            for k, f in enumerate(flips):
                cp = pltpu.make_async_remote_copy(
                    src_ref=mine[t], dst_ref=gath[t].at[me], send_sem=send.at[t, k],
                    recv_sem=recv.at[t, k], device_id=peer(f), device_id_type=MESH)
                cp.start()
                copies.append(cp)
        for t in range(n):
            for k, f in enumerate(flips):
                px, py, pc = peer(f)
                pltpu.make_async_remote_copy(
                    src_ref=mine[t], dst_ref=gath[t].at[4 * px + 2 * py + pc], send_sem=send.at[t, k],
                    recv_sem=recv.at[t, k], device_id=(px, py, pc), device_id_type=MESH).wait_recv()
        for cp in copies:
            cp.wait_send()
        for t in range(n):
            tot = gath[t][0]
            for dev in range(1, 8):
                tot = tot + gath[t][dev]
            outs[t][...] = tot

    vm = pl.BlockSpec(memory_space=pltpu.VMEM)
    return pl.pallas_call(
        body, name="allreduce_stats", in_specs=[vm] * n, out_specs=[vm] * n,
        out_shape=[jax.ShapeDtypeStruct(s.shape[1:], F32) for s in stats],
        scratch_shapes=[pltpu.VMEM(s.shape[1:], F32) for s in stats]
        + [pltpu.VMEM((8, *s.shape[1:]), F32) for s in stats]
        + [pltpu.SemaphoreType.DMA((n, 7)), pltpu.SemaphoreType.DMA((n, 7))],
    )(*stats)


def _adamw(name, w, m, v, g_parts):
    r, c = w.shape
    tm = min(r, 128)
    n_g = len(g_parts)

    def body(*refs):
        w_ref, m_ref, v_ref = refs[:3]
        g_refs = refs[3:3 + n_g]
        g_out, d_out, m_out, v_out = refs[3 + n_g:]
        g = g_refs[0][...]
        for gr in g_refs[1:]:
            g = g + gr[...]
        m_new = ADAM_B1 * m_ref[...] + (1.0 - ADAM_B1) * g
        v_new = ADAM_B2 * v_ref[...] + (1.0 - ADAM_B2) * (g * g)
        m_hat = m_new / (1.0 - ADAM_B1 ** ADAM_STEP)
        v_hat = v_new / (1.0 - ADAM_B2 ** ADAM_STEP)
        g_out[...] = g
        d_out[...] = -ADAM_LR * (m_hat / (jnp.sqrt(v_hat) + ADAM_EPS) + ADAM_WD * w_ref[...])
        m_out[...] = m_new
        v_out[...] = v_new

    blk = pl.BlockSpec((tm, c), lambda i: (i, 0))
    out = jax.ShapeDtypeStruct((r, c), F32)
    return pl.pallas_call(
        body, name=name, grid=(r // tm,), in_specs=[blk] * (3 + n_g), out_specs=[blk] * 4,
        out_shape=[out] * 4, compiler_params=_params(("parallel",)),
    )(w, m, v, *g_parts)


def _rope_tables(positions):
    half = HEAD_DIM // 2
    inv_freq = ROPE_THETA ** (-jnp.arange(half, dtype=F32) / half)
    ang = positions.astype(F32)[0, :, None] * inv_freq
    cos, sin = jnp.cos(ang), jnp.sin(ang)
    return jnp.concatenate([cos, cos], axis=-1), jnp.concatenate([-sin, sin], axis=-1)


def kernel(x, positions, w_in, w_pool, pool_scale, w_branch_attn, w_branch_pool, w_out, ln_mix_g, ln_mix_b, w_ff1, w_ff2, ln_ff_g, ln_ff_b, loss_target, m_w_in, m_w_pool, m_pool_scale, m_w_branch_attn, m_w_branch_pool, m_w_out, m_ln_mix_g, m_ln_mix_b, m_w_ff1, m_w_ff2, m_ln_ff_g, m_ln_ff_b, v_w_in, v_w_pool, v_pool_scale, v_w_branch_attn, v_w_branch_pool, v_w_out, v_ln_mix_g, v_ln_mix_b, v_w_ff1, v_w_ff2, v_ln_ff_g, v_ln_ff_b):
    s, d = x.shape[1], x.shape[2]
    aw = d // 2
    ng = len(POOL_WINDOWS)
    pgw = aw // ng
    x2d, target = x[0], loss_target[0]
    xb = x2d.astype(BF16)
    cos_t, sin_t = _rope_tables(positions)

    big = {"w_in": w_in[0], "w_pool": w_pool[0].reshape(-1, pgw), "w_branch_attn": w_branch_attn[0],
           "w_branch_pool": w_branch_pool[0], "w_out": w_out[0], "w_ff1": w_ff1[0], "w_ff2": w_ff2[0]}
    names = list(big)
    w_in_st, wp_st, wba_st, wbp_st, w_out_st, w1_st, w2_st = _gather_weights(
        [big[k].astype(BF16) for k in names])
    rows_sh = pgw // N_CHIPS
    wp = wp_st.reshape(N_CHIPS, ng, rows_sh, pgw).transpose(1, 0, 2, 3).reshape(ng, pgw, pgw)
    w_out_full = w_out_st.reshape(d, d)
    w2_full = w2_st.reshape(-1, d)
    dff = w2_full.shape[0]

    h = _in_proj(xb, w_in_st, cos_t, sin_t, aw)

    def to_view(a, dil):
        return a.reshape(s // dil, dil * a.shape[1])

    qkv = {1: (h, h, h)}
    offs = {1: (0, 1, 2)}
    for dil in DILATIONS[1:]:
        qkv[dil] = tuple(to_view(h[:, i * aw:(i + 1) * aw], dil) for i in range(3))
        offs[dil] = (0, 0, 0)
    o_parts, lse_parts = [], []
    for dil in DILATIONS:
        o_p, lse_p = _attn_fwd(f"attn_fwd_d{dil}", *qkv[dil], offs[dil], aw, dil * aw)
        o_parts.append(o_p.reshape(s, aw))
        lse_parts.append(lse_p.reshape(s, aw))
    o_attn, lse = _attn_combine(o_parts, lse_parts)
    p, pw, y = _pool_fwd(h, wp, pool_scale, aw)
    ya, yp, merged = _branch_merge(o_attn, y, wba_st, wbp_st, h, aw, d)
    xhat1, rstd1, x1b = _mix_norm(merged, w_out_full, x2d, ln_mix_g, ln_mix_b)
    r = _ff_up(x1b, w1_st)
    dz2, dz2b, st2 = _ff_down_loss(r, w2_full, xhat1, ln_mix_g, ln_mix_b, ln_ff_g, ln_ff_b, target)

    da = _ff_down_bwd(dz2b, w2_full, r)
    g_w2 = _wgrad("wgrad_ff2", r, dz2b, d).reshape(N_CHIPS, dff // N_CHIPS, d)
    g_w1 = _wgrad("wgrad_ff1", x1b, da, dff // N_CHIPS)
    dz1, dz1b, st1 = _ff_up_bwd(da, w1_st, dz2, xhat1, rstd1, ln_mix_g)
    dya, dyp, dgate = _mix_bwd(dz1b, w_out_full, h, ya, yp, aw)
    g_wout = _wgrad("wgrad_out", merged, dz1b, d).reshape(N_CHIPS, d // N_CHIPS, d)
    g_wba = _wgrad("wgrad_branch_attn", o_attn, dya, d // N_CHIPS)
    g_wbp = _wgrad("wgrad_branch_pool", y, dyp, d // N_CHIPS)
    do, delta = _attn_out_bwd(dya, wba_st, o_attn)
    dpw, stp = _pool_out_bwd(dyp, wbp_st, pw, pool_scale)
    dwp, du = _pool_bwd(dpw, p, wp)

    dqs, dks, dvs = [], [], []
    for dil in DILATIONS:
        dov, lsev, dlv = (to_view(a, dil) for a in (do, lse, delta))
        args = (*qkv[dil], dov, lsev, dlv, offs[dil], aw, dil * aw)
        dqs.append(_attn_bwd_q(f"attn_bwd_q_d{dil}", *args).reshape(s, aw))
        dk_p, dv_p = _attn_bwd_kv(f"attn_bwd_kv_d{dil}", *args)
        dks.append(dk_p.reshape(s, aw))
        dvs.append(dv_p.reshape(s, aw))
    dqkvu = _qkvu_grad(dqs, dks, dvs, du, cos_t, sin_t)
    g_win_a = _wgrad("wgrad_in_qkvu", xb, dqkvu, d)
    g_win_b = _wgrad("wgrad_in_gates", xb, dgate, d)
    g_win = jnp.concatenate([g_win_a, g_win_b], axis=0)
    dx_a = _in_proj_bwd_x("in_proj_bwd_qkvu", dqkvu, w_in_st, 0, dz1, ALPHA)
    grad_x = _in_proj_bwd_x("in_proj_bwd_gates", dgate, w_in_st, 2, dx_a, 1.0)

    g_wp = dwp.reshape(ng, N_CHIPS, rows_sh, pgw).transpose(1, 0, 2, 3).reshape(
        N_CHIPS, ng * rows_sh, pgw).astype(BF16)
    grads = {"w_in": g_win, "w_pool": g_wp, "w_branch_attn": g_wba, "w_branch_pool": g_wbp,
             "w_out": g_wout, "w_ff1": g_w1, "w_ff2": g_w2}
    scattered = _scatter_grads([grads[k] for k in names])
    lands, owns = scattered[:len(names)], scattered[len(names):]
    parts = [_sum_slabs(f"sum_slabs_{k}", owns[i], lands[i]) for i, k in enumerate(names)]
    other = _exchange_cores(parts)
    tot2, tot1, totp = _allreduce_stats([st2, st1, stp])

    moments = {"w_in": (m_w_in, v_w_in), "w_pool": (m_w_pool, v_w_pool),
               "w_branch_attn": (m_w_branch_attn, v_w_branch_attn),
               "w_branch_pool": (m_w_branch_pool, v_w_branch_pool), "w_out": (m_w_out, v_w_out),
               "w_ff1": (m_w_ff1, v_w_ff1), "w_ff2": (m_w_ff2, v_w_ff2)}
    originals = {"w_in": w_in, "w_pool": w_pool, "w_branch_attn": w_branch_attn,
                 "w_branch_pool": w_branch_pool, "w_out": w_out, "w_ff1": w_ff1, "w_ff2": w_ff2}
    res = {}
    for i, k in enumerate(names):
        shape2d = big[k].shape
        mk, vk = (a.reshape(shape2d) for a in moments[k])
        outs = _adamw(f"adamw_{k}", big[k], mk, vk, [parts[i], other[i]])
        res[k] = [o.reshape(originals[k].shape) for o in outs]

    def pad_d(a):
        return jnp.pad(a, ((0, 0), (0, d - a.shape[1])))

    small = ["ln_mix_g", "ln_mix_b", "ln_ff_g", "ln_ff_b", "pool_scale"]
    small_w = {"ln_mix_g": ln_mix_g, "ln_mix_b": ln_mix_b, "ln_ff_g": ln_ff_g, "ln_ff_b": ln_ff_b,
               "pool_scale": pool_scale}
    small_m = {"ln_mix_g": m_ln_mix_g, "ln_mix_b": m_ln_mix_b, "ln_ff_g": m_ln_ff_g,
               "ln_ff_b": m_ln_ff_b, "pool_scale": m_pool_scale}
    small_v = {"ln_mix_g": v_ln_mix_g, "ln_mix_b": v_ln_mix_b, "ln_ff_g": v_ln_ff_g,
               "ln_ff_b": v_ln_ff_b, "pool_scale": v_pool_scale}
    small_g = [tot1[0:1], tot1[1:2], tot2[0:1], tot2[1:2], pad_d(totp[0:1])]

    def pack(rows):
        return jnp.concatenate([pad_d(a) for a in rows] + [jnp.zeros((8 - len(rows), d), F32)], axis=0)

    outs = _adamw("adamw_small", pack([small_w[k] for k in small]), pack([small_m[k] for k in small]),
                  pack([small_v[k] for k in small]), [pack(small_g)])
    for i, k in enumerate(small):
        res[k] = [o[i:i + 1, :small_w[k].shape[1]] for o in outs]
    loss = tot2[2, 0]

    order = ["w_in", "w_pool", "pool_scale", "w_branch_attn", "w_branch_pool", "w_out", "ln_mix_g",
             "ln_mix_b", "w_ff1", "w_ff2", "ln_ff_g", "ln_ff_b"]
    result = [loss, grad_x[None]]
    for idx in range(4):
        result += [res[k][idx] for k in order]
    return tuple(result)
```

```python
import jax
import jax.numpy as jnp
from jax import lax
from jax.experimental import pallas as pl
from jax.experimental.pallas import tpu as pltpu

F32 = jnp.float32
BF16 = jnp.bfloat16
MESH = pl.DeviceIdType.MESH

HEAD_DIM = 128
SUB_BLOCK = 128
DILATIONS = (1, 4, 16)
POOL_WINDOWS = (2, 4, 8, 16)
POOL_HALO = 16
ROPE_THETA = 10000.0
LN_EPS = 1e-5
ALPHA = 2.0 ** 0.25
ADAM_LR, ADAM_B1, ADAM_B2, ADAM_EPS, ADAM_WD, ADAM_STEP = 0.001, 0.9, 0.999, 1e-08, 0.01, 10
NEG = -1e30
N_CHIPS = 4
VMEM_LIMIT = 56 * 1024 * 1024
EPILOGUE_ROWS = 128


def _params(sem=None, vmem=VMEM_LIMIT):
    kw = {"vmem_limit_bytes": vmem}
    if sem is not None:
        kw["dimension_semantics"] = sem
    return pltpu.CompilerParams(**kw)


def _dot(a, b, contract):
    return lax.dot_general(a, b, (contract, ((), ())), preferred_element_type=F32)


ANY_SPEC = pl.BlockSpec(memory_space=pl.ANY)
HBM_SPEC = pl.BlockSpec(memory_space=pltpu.HBM)
SEM_SPEC = pl.BlockSpec(memory_space=pltpu.SEMAPHORE)
DATAFLOW = pltpu.SideEffectType.DATAFLOW_SIDE_EFFECTING


def _hbm(a):
    return pltpu.with_memory_space_constraint(a, pltpu.HBM)


def _after(token, a):
    return lax.optimization_barrier((token, a))[1]
NN = ((1,), (0,))
NT = ((1,), (1,))
TN = ((0,), (0,))


def _mm(name, grid, a, a_spec, b, b_spec, contract, extras, extra_specs, out_shape, out_specs,
        epilogue, acc_shape, acc_as_ref=False):
    nk = grid[2]
    n_ex = len(extras)
    n_out = len(out_shape)

    def body(*refs):
        a_ref, b_ref = refs[0], refs[1]
        ex = refs[2:2 + n_ex]
        outs = refs[2 + n_ex:2 + n_ex + n_out]
        if nk == 1:
            epilogue(_dot(a_ref[...], b_ref[...], contract), ex, outs)
        else:
            acc = refs[2 + n_ex + n_out]
            k = pl.program_id(2)

            @pl.when(k == 0)
            def _():
                acc[...] = jnp.zeros_like(acc)

            acc[...] += _dot(a_ref[...], b_ref[...], contract)

            @pl.when(k == nk - 1)
            def _():
                epilogue(acc if acc_as_ref else acc[...], ex, outs)

    scratch = [pltpu.VMEM(acc_shape, F32)] if nk > 1 else []
    return pl.pallas_call(
        body, name=name, grid=grid, in_specs=[a_spec, b_spec, *extra_specs], out_specs=out_specs,
        out_shape=out_shape, scratch_shapes=scratch,
        compiler_params=_params(("parallel", "parallel", "arbitrary")),
    )(a, b, *extras)


def _stats_rows(rows, width):
    idx = lax.broadcasted_iota(jnp.int32, (8, width), 0)
    out = jnp.zeros((8, width), F32)
    for r, v in enumerate(rows):
        out = jnp.where(idx == r, jnp.broadcast_to(v, (8, width)), out)
    return out


def _layer_norm_fwd(z):
    mu = jnp.mean(z, axis=-1, keepdims=True)
    zc = z - mu
    var = jnp.mean(zc * zc, axis=-1, keepdims=True)
    rstd = lax.rsqrt(var + LN_EPS)
    return zc * rstd, rstd


def _layer_norm_bwd(dy, xhat, rstd, g):
    dxh = dy * g
    m1 = jnp.mean(dxh, axis=-1, keepdims=True)
    m2 = jnp.mean(dxh * xhat, axis=-1, keepdims=True)
    return rstd * (dxh - m1 - xhat * m2)


def _in_proj(xb, w_in_st, cos_t, sin_t, aw):
    s, d = xb.shape
    n_sh = w_in_st.shape[2]
    tm, tn = min(s, 1024), aw
    per = n_sh // tn
    grid = (s // tm, (N_CHIPS * n_sh) // tn, 1)

    def epilogue(acc, ex, outs):
        cos_ref, sin_ref = ex
        (h_ref,) = outs
        seg = pl.program_id(1)

        @pl.when(seg < 2)
        def _():
            cos, sin = cos_ref[...], sin_ref[...]
            for hh in range(tn // HEAD_DIM):
                sl = slice(hh * HEAD_DIM, (hh + 1) * HEAD_DIM)
                t = acc[:, sl]
                h_ref[:, sl] = (t * cos + pltpu.roll(t, HEAD_DIM // 2, 1) * sin).astype(BF16)

        @pl.when((seg >= 2) & (seg < 4))
        def _():
            h_ref[...] = acc.astype(BF16)

        @pl.when(seg >= 4)
        def _():
            h_ref[...] = jax.nn.sigmoid(acc).astype(BF16)

    return _mm(
        "in_proj", grid, xb, pl.BlockSpec((tm, d), lambda i, j, k: (i, 0)),
        w_in_st, pl.BlockSpec((None, d, tn), lambda i, j, k: (j // per, 0, j % per)), NN,
        [cos_t, sin_t], [pl.BlockSpec((tm, HEAD_DIM), lambda i, j, k: (i, 0))] * 2,
        [jax.ShapeDtypeStruct((s, N_CHIPS * n_sh), BF16)],
        [pl.BlockSpec((tm, tn), lambda i, j, k: (i, j))], epilogue, None)[0]


def _band_masks(block_idx):
    qi = lax.broadcasted_iota(jnp.int32, (SUB_BLOCK, 2 * SUB_BLOCK), 0)
    kj = lax.broadcasted_iota(jnp.int32, (SUB_BLOCK, 2 * SUB_BLOCK), 1)
    first_key = jnp.where(block_idx > 0, 0, SUB_BLOCK)
    return (kj >= qi) & (kj <= qi + SUB_BLOCK) & (kj >= first_key)


def _attn_fwd(name, q, k, v, offs, cw, width):
    m = q.shape[0]
    nh = cw // HEAD_DIM
    grid = (m // SUB_BLOCK, width // cw)
    scale = HEAD_DIM ** -0.5

    def body(q_ref, kp_ref, kc_ref, vp_ref, vc_ref, o_ref, lse_ref):
        valid = _band_masks(pl.program_id(0))
        for hh in range(nh):
            sl = slice(hh * HEAD_DIM, (hh + 1) * HEAD_DIM)
            kcat = jnp.concatenate([kp_ref[:, sl], kc_ref[:, sl]], axis=0)
            vcat = jnp.concatenate([vp_ref[:, sl], vc_ref[:, sl]], axis=0)
            sc = _dot(q_ref[:, sl], kcat, NT) * scale
            sc = jnp.where(valid, sc, NEG)
            mx = jnp.max(sc, axis=-1, keepdims=True)
            p = jnp.exp(sc - mx)
            l = jnp.sum(p, axis=-1, keepdims=True)
            o = _dot(p.astype(BF16), vcat, NN) / l
            o_ref[:, sl] = o.astype(BF16)
            lse_ref[:, sl] = jnp.broadcast_to(mx + jnp.log(l), (SUB_BLOCK, HEAD_DIM))

    def cur(off):
        return pl.BlockSpec((SUB_BLOCK, cw), lambda n, j: (n, j + off))

    def prev(off):
        return pl.BlockSpec((SUB_BLOCK, cw), lambda n, j: (jnp.maximum(n - 1, 0), j + off))

    return pl.pallas_call(
        body, name=name, grid=grid,
        in_specs=[cur(offs[0]), prev(offs[1]), cur(offs[1]), prev(offs[2]), cur(offs[2])],
        out_specs=[cur(0), cur(0)],
        out_shape=[jax.ShapeDtypeStruct((m, width), BF16), jax.ShapeDtypeStruct((m, width), F32)],
        compiler_params=_params(("parallel", "parallel")),
    )(q, k, k, v, v)


def _attn_combine(os_, lses):
    s, aw = os_[0].shape
    tm = min(s, 512)

    def body(o1, o2, o3, l1, l2, l3, o_ref, lse_ref):
        a, b, c = l1[...], l2[...], l3[...]
        mx = jnp.maximum(jnp.maximum(a, b), c)
        ea, eb, ec = jnp.exp(a - mx), jnp.exp(b - mx), jnp.exp(c - mx)
        tot = ea + eb + ec
        o = (ea * o1[...].astype(F32) + eb * o2[...].astype(F32) + ec * o3[...].astype(F32)) / tot
        o_ref[...] = o.astype(BF16)
        lse_ref[...] = mx + jnp.log(tot)

    spec = pl.BlockSpec((tm, aw), lambda i: (i, 0))
    return pl.pallas_call(
        body, name="attn_combine", grid=(s // tm,), in_specs=[spec] * 6, out_specs=[spec, spec],
        out_shape=[jax.ShapeDtypeStruct((s, aw), BF16), jax.ShapeDtypeStruct((s, aw), F32)],
        compiler_params=_params(("parallel",)),
    )(*os_, *lses)


def _pool_counts(tm, rows, pgw, row0):
    t = lax.broadcasted_iota(jnp.int32, (rows, len(POOL_WINDOWS) * pgw), 0) + row0
    col = lax.broadcasted_iota(jnp.int32, (rows, len(POOL_WINDOWS) * pgw), 1)
    w = jnp.full((rows, len(POOL_WINDOWS) * pgw), POOL_WINDOWS[0], jnp.int32)
    for g in range(1, len(POOL_WINDOWS)):
        w = jnp.where(col >= g * pgw, POOL_WINDOWS[g], w)
    return jnp.minimum(t + 1, w).astype(F32)


def _window_sums(xs, direction, pgw):
    rows = xs.shape[0]
    acc = xs
    out = None
    col = lax.broadcasted_iota(jnp.int32, xs.shape, 1)
    for g, w in enumerate(POOL_WINDOWS):
        sh = w // 2
        acc = acc + pltpu.roll(acc, sh if direction > 0 else rows - sh, 0)
        out = acc if out is None else jnp.where(col >= g * pgw, acc, out)
    return out


def _pool_fwd(h, wp, scale, aw):
    s = h.shape[0]
    pw_ = aw
    pgw = pw_ // len(POOL_WINDOWS)
    tm = min(s, 512)
    hb = tm // POOL_HALO

    def body(u_ref, halo_ref, wp_ref, sc_ref, p_ref, pw_ref, y_ref):
        i = pl.program_id(0)
        u = u_ref[...].astype(F32)
        halo = halo_ref[...].astype(F32) * jnp.where(i > 0, 1.0, 0.0)
        xs = jnp.concatenate([halo, u], axis=0)
        sums = _window_sums(xs, +1, pgw)[POOL_HALO:]
        p = (sums / _pool_counts(tm, tm, pgw, i * tm) - u).astype(BF16)
        p_ref[...] = p
        sc = sc_ref[...]
        for g in range(len(POOL_WINDOWS)):
            sl = slice(g * pgw, (g + 1) * pgw)
            pw = _dot(p[:, sl], wp_ref[g], NN)
            pw_ref[:, sl] = pw.astype(BF16)
            y_ref[:, sl] = (pw * sc[:, sl]).astype(BF16)

    out = jax.ShapeDtypeStruct((s, pw_), BF16)
    row = pl.BlockSpec((tm, pw_), lambda i: (i, 0))
    return pl.pallas_call(
        body, name="pool_fwd", grid=(s // tm,),
        in_specs=[pl.BlockSpec((tm, pw_), lambda i: (i, 3)),
                  pl.BlockSpec((POOL_HALO, pw_), lambda i: (jnp.maximum(i * hb - 1, 0), 3)),
                  pl.BlockSpec(wp.shape, lambda i: (0, 0, 0)),
                  pl.BlockSpec((1, pw_), lambda i: (0, 0))],
        out_specs=[row, row, row], out_shape=[out, out, out],
        compiler_params=_params(("parallel",)),
    )(h, h, wp, scale)


def _branch_merge(o_attn, y, wba_st, wbp_st, h, aw, d):
    s = o_attn.shape[0]
    tn = wba_st.shape[2]
    tm = min(s, 1024)
    ga0 = 4 * aw // tn
    gp0 = (4 * aw + d) // tn

    def body(o_ref, y_ref, wa_ref, wp_ref, sga_ref, sgp_ref, ya_ref, yp_ref, mg_ref):
        ya = _dot(o_ref[...], wa_ref[...], NN)
        yp = _dot(y_ref[...], wp_ref[...], NN)
        ya_ref[...] = ya.astype(BF16)
        yp_ref[...] = yp.astype(BF16)
        mg_ref[...] = (sga_ref[...].astype(F32) * ya + sgp_ref[...].astype(F32) * yp).astype(BF16)

    out = jax.ShapeDtypeStruct((s, d), BF16)
    blk = pl.BlockSpec((tm, tn), lambda i, j: (i, j))
    return pl.pallas_call(
        body, name="branch_merge", grid=(s // tm, N_CHIPS),
        in_specs=[pl.BlockSpec((tm, aw), lambda i, j: (i, 0)),
                  pl.BlockSpec((tm, aw), lambda i, j: (i, 0)),
                  pl.BlockSpec((None, aw, tn), lambda i, j: (j, 0, 0)),
                  pl.BlockSpec((None, aw, tn), lambda i, j: (j, 0, 0)),
                  pl.BlockSpec((tm, tn), lambda i, j: (i, j + ga0)),
                  pl.BlockSpec((tm, tn), lambda i, j: (i, j + gp0))],
        out_specs=[blk, blk, blk], out_shape=[out, out, out],
        compiler_params=_params(("parallel", "parallel")),
    )(o_attn, y, wba_st, wbp_st, h, h)


def _mix_norm(merged, w_out, x, g1, b1):
    s, d = x.shape
    tm = min(s, 256)

    def epilogue(acc, ex, outs):
        x_ref, g_ref, b_ref = ex
        xh_ref, rs_ref, xb_ref = outs
        xhat, rstd = _layer_norm_fwd(ALPHA * x_ref[...] + acc)
        xh_ref[...] = xhat
        rs_ref[...] = rstd
        xb_ref[...] = (xhat * g_ref[...] + b_ref[...]).astype(BF16)

    row = pl.BlockSpec((tm, d), lambda i, j, k: (i, 0))
    vec = pl.BlockSpec((1, d), lambda i, j, k: (0, 0))
    return _mm(
        "mix_norm", (s // tm, 1, 1), merged, row, w_out, pl.BlockSpec((d, d), lambda i, j, k: (0, 0)),
        NN, [x, g1, b1], [row, vec, vec],
        [jax.ShapeDtypeStruct((s, d), F32), jax.ShapeDtypeStruct((s, 1), F32),
         jax.ShapeDtypeStruct((s, d), BF16)],
        [row, pl.BlockSpec((tm, 1), lambda i, j, k: (i, 0)), row], epilogue, None)


def _ff_up(x1b, w1_st):
    s, d = x1b.shape
    n_sh = w1_st.shape[2]
    tm, tn = min(s, 1024), min(n_sh, 1024)
    per = n_sh // tn

    def epilogue(acc, ex, outs):
        r = jnp.maximum(acc, 0.0)
        outs[0][...] = (r * r).astype(BF16)

    return _mm(
        "ff_up", (s // tm, N_CHIPS * per, 1), x1b, pl.BlockSpec((tm, d), lambda i, j, k: (i, 0)),
        w1_st, pl.BlockSpec((None, d, tn), lambda i, j, k: (j // per, 0, j % per)), NN, [], [],
        [jax.ShapeDtypeStruct((s, N_CHIPS * n_sh), BF16)],
        [pl.BlockSpec((tm, tn), lambda i, j, k: (i, j))], epilogue, None)[0]


def _ff_down_loss(r, w2, xhat1, g1, b1, g2, b2, target):
    s, d = xhat1.shape
    dff = r.shape[1]
    tm, tk = min(s, 512), min(dff, 1024)
    ch = min(tm, EPILOGUE_ROWS)

    def epilogue(acc_ref, ex, outs):
        xh1_ref, g1_ref, b1_ref, g2_ref, b2_ref, t_ref = ex
        dz_ref, dzb_ref, st_ref = outs
        g1v, b1v, g2v, b2v = g1_ref[...], b1_ref[...], g2_ref[...], b2_ref[...]
        dg = db = loss = None
        for c in range(tm // ch):
            rows = slice(c * ch, (c + 1) * ch)
            x1 = xh1_ref[rows, :] * g1v + b1v
            xhat2, rstd2 = _layer_norm_fwd(ALPHA * x1 + acc_ref[rows, :])
            err = xhat2 * g2v + b2v - t_ref[rows, :]
            dy = err * (1.0 / d)
            dz = _layer_norm_bwd(dy, xhat2, rstd2, g2v)
            dz_ref[rows, :] = dz
            dzb_ref[rows, :] = dz.astype(BF16)
            parts = (jnp.sum(dy * xhat2, axis=0, keepdims=True), jnp.sum(dy, axis=0, keepdims=True),
                     jnp.sum(jnp.sum(err * err, axis=-1, keepdims=True), axis=0, keepdims=True))
            dg, db, loss = parts if c == 0 else (dg + parts[0], db + parts[1], loss + parts[2])
        st_ref[...] = _stats_rows([dg, db, jnp.broadcast_to((0.5 / d) * loss, (1, d))], d)

    row = pl.BlockSpec((tm, d), lambda i, j, k: (i, 0))
    vec = pl.BlockSpec((1, d), lambda i, j, k: (0, 0))
    return _mm(
        "ff_down_loss", (s // tm, 1, dff // tk), r, pl.BlockSpec((tm, tk), lambda i, j, k: (i, k)),
        w2, pl.BlockSpec((tk, d), lambda i, j, k: (k, 0)), NN,
        [xhat1, g1, b1, g2, b2, target], [row, vec, vec, vec, vec, row],
        [jax.ShapeDtypeStruct((s, d), F32), jax.ShapeDtypeStruct((s, d), BF16),
         jax.ShapeDtypeStruct((s // tm, 8, d), F32)],
        [row, row, pl.BlockSpec((None, 8, d), lambda i, j, k: (i, 0, 0))], epilogue, (tm, d),
        acc_as_ref=True)


def _ff_down_bwd(dz2b, w2, r):
    s, d = dz2b.shape
    dff = r.shape[1]
    tm, tn = min(s, 1024), min(dff, 1024)

    def epilogue(acc, ex, outs):
        outs[0][...] = (acc * (2.0 * jnp.sqrt(ex[0][...].astype(F32)))).astype(BF16)

    blk = pl.BlockSpec((tm, tn), lambda i, j, k: (i, j))
    return _mm(
        "ff_down_bwd", (s // tm, dff // tn, 1), dz2b, pl.BlockSpec((tm, d), lambda i, j, k: (i, 0)),
        w2, pl.BlockSpec((tn, d), lambda i, j, k: (j, 0)), NT, [r], [blk],
        [jax.ShapeDtypeStruct((s, dff), BF16)], [blk], epilogue, None)[0]


def _wgrad(name, a, g, n_sh):
    s, rows = a.shape
    cols = g.shape[1]
    tm, tn, tk = min(rows, 2048), min(cols, 1024), min(s, 1024)
    if tn >= n_sh:
        span = tn // n_sh
        out_spec = pl.BlockSpec((span, tm, n_sh), lambda i, j, k: (j, i, 0))

        def epilogue(acc_ref, ex, outs):
            for sh in range(span):
                outs[0][sh] = acc_ref[:, sh * n_sh:(sh + 1) * n_sh].astype(BF16)
    else:
        per = n_sh // tn
        out_spec = pl.BlockSpec((None, tm, tn), lambda i, j, k: (j // per, i, j % per))

        def epilogue(acc_ref, ex, outs):
            outs[0][...] = acc_ref[...].astype(BF16)

    return _mm(
        name, (rows // tm, cols // tn, s // tk), a, pl.BlockSpec((tk, tm), lambda i, j, k: (k, i)),
        g, pl.BlockSpec((tk, tn), lambda i, j, k: (k, j)), TN, [], [],
        [jax.ShapeDtypeStruct((cols // n_sh, rows, n_sh), BF16)], [out_spec], epilogue,
        (tm, tn), acc_as_ref=True)[0]


def _ff_up_bwd(da, w1_st, dz2, xhat1, rstd1, g1):
    s, d = dz2.shape
    n_sh = w1_st.shape[2]
    tm, tk = min(s, 512), min(n_sh, 1024)
    per = n_sh // tk
    ch = min(tm, EPILOGUE_ROWS)

    def epilogue(acc_ref, ex, outs):
        dz2_ref, xh_ref, rs_ref, g_ref = ex
        dz_ref, dzb_ref, st_ref = outs
        gv = g_ref[...]
        dg = db = None
        for c in range(tm // ch):
            rows = slice(c * ch, (c + 1) * ch)
            dx1 = ALPHA * dz2_ref[rows, :] + acc_ref[rows, :]
            xhat = xh_ref[rows, :]
            dz = _layer_norm_bwd(dx1, xhat, rs_ref[rows, :], gv)
            dz_ref[rows, :] = dz
            dzb_ref[rows, :] = dz.astype(BF16)
            parts = (jnp.sum(dx1 * xhat, axis=0, keepdims=True), jnp.sum(dx1, axis=0, keepdims=True))
            dg, db = parts if c == 0 else (dg + parts[0], db + parts[1])
        st_ref[...] = _stats_rows([dg, db], d)

    row = pl.BlockSpec((tm, d), lambda i, j, k: (i, 0))
    return _mm(
        "ff_up_bwd", (s // tm, 1, N_CHIPS * per), da, pl.BlockSpec((tm, tk), lambda i, j, k: (i, k)),
        w1_st, pl.BlockSpec((None, d, tk), lambda i, j, k: (k // per, 0, k % per)), NT,
        [dz2, xhat1, rstd1, g1],
        [row, row, pl.BlockSpec((tm, 1), lambda i, j, k: (i, 0)), pl.BlockSpec((1, d), lambda i, j, k: (0, 0))],
        [jax.ShapeDtypeStruct((s, d), F32), jax.ShapeDtypeStruct((s, d), BF16),
         jax.ShapeDtypeStruct((s // tm, 8, d), F32)],
        [row, row, pl.BlockSpec((None, 8, d), lambda i, j, k: (i, 0, 0))], epilogue, (tm, d),
        acc_as_ref=True)


def _mix_bwd(dz1b, w_out, h, ya, yp, aw):
    s, d = dz1b.shape
    tm = min(s, 256)
    gblk = 4 * aw // d

    def epilogue(acc, ex, outs):
        sga_ref, sgp_ref, ya_ref, yp_ref = ex
        dya_ref, dyp_ref, dg_ref = outs
        sga, sgp = sga_ref[...].astype(F32), sgp_ref[...].astype(F32)
        dya_ref[...] = (acc * sga).astype(BF16)
        dyp_ref[...] = (acc * sgp).astype(BF16)
        dg_ref[:, :d] = (acc * ya_ref[...].astype(F32) * (sga * (1.0 - sga))).astype(BF16)
        dg_ref[:, d:] = (acc * yp_ref[...].astype(F32) * (sgp * (1.0 - sgp))).astype(BF16)

    row = pl.BlockSpec((tm, d), lambda i, j, k: (i, 0))
    return _mm(
        "mix_bwd", (s // tm, 1, 1), dz1b, row, w_out, pl.BlockSpec((d, d), lambda i, j, k: (0, 0)), NT,
        [h, h, ya, yp],
        [pl.BlockSpec((tm, d), lambda i, j, k: (i, gblk)),
         pl.BlockSpec((tm, d), lambda i, j, k: (i, gblk + 1)), row, row],
        [jax.ShapeDtypeStruct((s, d), BF16), jax.ShapeDtypeStruct((s, d), BF16),
         jax.ShapeDtypeStruct((s, 2 * d), BF16)],
        [row, row, pl.BlockSpec((tm, 2 * d), lambda i, j, k: (i, 0))], epilogue, None)


def _branch_in_bwd(name, dyb, wb_st, epilogue, extras, extra_specs, out_shape, out_specs):
    s = dyb.shape[0]
    aw, tk = wb_st.shape[1], wb_st.shape[2]
    tm = min(s, 1024)
    return _mm(
        name, (s // tm, 1, N_CHIPS), dyb, pl.BlockSpec((tm, tk), lambda i, j, k: (i, k)),
        wb_st, pl.BlockSpec((None, aw, tk), lambda i, j, k: (k, 0, 0)), NT,
        extras, extra_specs, out_shape, out_specs, epilogue, (tm, aw)), tm


def _attn_out_bwd(dya, wba_st, o_attn):
    s, aw = o_attn.shape

    def epilogue(acc, ex, outs):
        o = ex[0][...].astype(F32)
        do_ref, dl_ref = outs
        do_ref[...] = acc.astype(BF16)
        for hh in range(aw // HEAD_DIM):
            sl = slice(hh * HEAD_DIM, (hh + 1) * HEAD_DIM)
            dl = jnp.sum(acc[:, sl] * o[:, sl], axis=-1, keepdims=True)
            dl_ref[:, sl] = jnp.broadcast_to(dl, (acc.shape[0], HEAD_DIM))

    tm = min(s, 1024)
    row = pl.BlockSpec((tm, aw), lambda i, j, k: (i, 0))
    return _branch_in_bwd(
        "attn_out_bwd", dya, wba_st, epilogue, [o_attn], [row],
        [jax.ShapeDtypeStruct((s, aw), BF16), jax.ShapeDtypeStruct((s, aw), F32)], [row, row])[0]


def _pool_out_bwd(dyp, wbp_st, pw, scale):
    s, pw_ = pw.shape
    tm = min(s, 1024)

    def epilogue(acc, ex, outs):
        pw_ref, sc_ref = ex
        dpw_ref, st_ref = outs
        dpw_ref[...] = (acc * sc_ref[...]).astype(BF16)
        st_ref[...] = _stats_rows([jnp.sum(acc * pw_ref[...].astype(F32), axis=0, keepdims=True)], pw_)

    row = pl.BlockSpec((tm, pw_), lambda i, j, k: (i, 0))
    return _branch_in_bwd(
        "pool_out_bwd", dyp, wbp_st, epilogue, [pw, scale],
        [row, pl.BlockSpec((1, pw_), lambda i, j, k: (0, 0))],
        [jax.ShapeDtypeStruct((s, pw_), BF16), jax.ShapeDtypeStruct((s // tm, 8, pw_), F32)],
        [row, pl.BlockSpec((None, 8, pw_), lambda i, j, k: (i, 0, 0))])[0]


def _pool_bwd(dpw, p, wp):
    s, pw_ = p.shape
    ng = len(POOL_WINDOWS)
    pgw = pw_ // ng
    tm = min(s, 512)
    hb = tm // POOL_HALO
    nblk = s // tm

    def body(dpw_ref, nxt_ref, p_ref, wp_ref, dwp_ref, du_ref):
        i = pl.program_id(0)
        nxt = (nxt_ref[...].astype(F32) * jnp.where(i < nblk - 1, 1.0, 0.0)).astype(BF16)
        dpw_all = jnp.concatenate([dpw_ref[...], nxt], axis=0)

        @pl.when(i == 0)
        def _():
            dwp_ref[...] = jnp.zeros_like(dwp_ref)

        dps = []
        for g in range(ng):
            sl = slice(g * pgw, (g + 1) * pgw)
            dwp_ref[g] += _dot(p_ref[:, sl], dpw_ref[:, sl], TN)
            dps.append(_dot(dpw_all[:, sl], wp_ref[g], NT))
        dp = jnp.concatenate(dps, axis=1)
        dpn = dp / _pool_counts(tm, tm + POOL_HALO, pgw, i * tm)
        du_ref[...] = (_window_sums(dpn, -1, pgw)[:tm] - dp[:tm]).astype(BF16)

    row = pl.BlockSpec((tm, pw_), lambda i: (i, 0))
    full = pl.BlockSpec((ng, pgw, pgw), lambda i: (0, 0, 0))
    return pl.pallas_call(
        body, name="pool_bwd", grid=(nblk,),
        in_specs=[row, pl.BlockSpec((POOL_HALO, pw_), lambda i: (jnp.minimum((i + 1) * hb, s // POOL_HALO - 1), 0)),
                  row, full],
        out_specs=[full, row],
        out_shape=[jax.ShapeDtypeStruct((ng, pgw, pgw), F32), jax.ShapeDtypeStruct((s, pw_), BF16)],
        compiler_params=_params(("arbitrary",)),
    )(dpw, dpw, p, wp)


def _attn_bwd_q(name, q, k, v, do, lse, delta, offs, cw, width):
    m = do.shape[0]
    nh = cw // HEAD_DIM
    grid = (m // SUB_BLOCK, width // cw)
    scale = HEAD_DIM ** -0.5

    def body(q_ref, kp_ref, kc_ref, vp_ref, vc_ref, do_ref, lse_ref, dl_ref, dq_ref):
        valid = _band_masks(pl.program_id(0))
        for hh in range(nh):
            sl = slice(hh * HEAD_DIM, (hh + 1) * HEAD_DIM)
            kcat = jnp.concatenate([kp_ref[:, sl], kc_ref[:, sl]], axis=0)
            vcat = jnp.concatenate([vp_ref[:, sl], vc_ref[:, sl]], axis=0)
            lse2 = jnp.concatenate([lse_ref[:, sl]] * 2, axis=1)
            dl2 = jnp.concatenate([dl_ref[:, sl]] * 2, axis=1)
            sc = _dot(q_ref[:, sl], kcat, NT) * scale
            p = jnp.where(valid, jnp.exp(jnp.where(valid, sc, NEG) - lse2), 0.0)
            dp = _dot(do_ref[:, sl], vcat, NT)
            ds = (p * (dp - dl2)).astype(BF16)
            dq_ref[:, sl] = (_dot(ds, kcat, NN) * scale).astype(BF16)

    def cur(off):
        return pl.BlockSpec((SUB_BLOCK, cw), lambda n, j: (n, j + off))

    def prev(off):
        return pl.BlockSpec((SUB_BLOCK, cw), lambda n, j: (jnp.maximum(n - 1, 0), j + off))

    return pl.pallas_call(
        body, name=name, grid=grid,
        in_specs=[cur(offs[0]), prev(offs[1]), cur(offs[1]), prev(offs[2]), cur(offs[2]),
                  cur(0), cur(0), cur(0)],
        out_specs=cur(0), out_shape=jax.ShapeDtypeStruct((m, width), BF16),
        compiler_params=_params(("parallel", "parallel")),
    )(q, k, k, v, v, do, lse, delta)


def _attn_bwd_kv(name, q, k, v, do, lse, delta, offs, cw, width):
    m = do.shape[0]
    nh = cw // HEAD_DIM
    nblk = m // SUB_BLOCK
    grid = (nblk, width // cw)
    scale = HEAD_DIM ** -0.5

    def body(qc_ref, qn_ref, k_ref, v_ref, doc_ref, don_ref, lsec_ref, lsen_ref, dlc_ref, dln_ref,
             dk_ref, dv_ref):
        no_next = jnp.where(pl.program_id(0) < nblk - 1, 0, 4 * SUB_BLOCK)
        r = lax.broadcasted_iota(jnp.int32, (2 * SUB_BLOCK, SUB_BLOCK), 0)
        kj = lax.broadcasted_iota(jnp.int32, (2 * SUB_BLOCK, SUB_BLOCK), 1)
        valid = ((r < SUB_BLOCK) & (kj <= r)) | ((r >= SUB_BLOCK) & (kj >= r - SUB_BLOCK + no_next))
        for hh in range(nh):
            sl = slice(hh * HEAD_DIM, (hh + 1) * HEAD_DIM)
            qcat = jnp.concatenate([qc_ref[:, sl], qn_ref[:, sl]], axis=0)
            docat = jnp.concatenate([doc_ref[:, sl], don_ref[:, sl]], axis=0)
            lse2 = jnp.concatenate([lsec_ref[:, sl], lsen_ref[:, sl]], axis=0)
            dl2 = jnp.concatenate([dlc_ref[:, sl], dln_ref[:, sl]], axis=0)
            sc = _dot(qcat, k_ref[:, sl], NT) * scale
            p = jnp.where(valid, jnp.exp(jnp.where(valid, sc, NEG) - lse2), 0.0)
            dp = _dot(docat, v_ref[:, sl], NT)
            ds = (p * (dp - dl2)).astype(BF16)
            dv_ref[:, sl] = _dot(p.astype(BF16), docat, TN).astype(BF16)
            dk_ref[:, sl] = (_dot(ds, qcat, TN) * scale).astype(BF16)

    def cur(off):
        return pl.BlockSpec((SUB_BLOCK, cw), lambda n, j: (n, j + off))

    def nxt(off):
        return pl.BlockSpec((SUB_BLOCK, cw), lambda n, j: (jnp.minimum(n + 1, nblk - 1), j + off))

    out = jax.ShapeDtypeStruct((m, width), BF16)
    return pl.pallas_call(
        body, name=name, grid=grid,
        in_specs=[cur(offs[0]), nxt(offs[0]), cur(offs[1]), cur(offs[2]),
                  cur(0), nxt(0), cur(0), nxt(0), cur(0), nxt(0)],
        out_specs=[cur(0), cur(0)], out_shape=[out, out],
        compiler_params=_params(("parallel", "parallel")),
    )(q, q, k, v, do, do, lse, lse, delta, delta)


def _qkvu_grad(dqs, dks, dvs, du, cos_t, sin_t):
    s, aw = du.shape
    tm = min(s, 512)

    def body(*refs):
        ins, (cos_ref, sin_ref, du_ref, out_ref) = refs[:9], refs[9:]
        cos, sin = cos_ref[...], sin_ref[...]
        for part in range(2):
            tot = sum(r[...].astype(F32) for r in ins[3 * part:3 * part + 3])
            for hh in range(aw // HEAD_DIM):
                sl = slice(hh * HEAD_DIM, (hh + 1) * HEAD_DIM)
                t = tot[:, sl]
                out_ref[:, part * aw + hh * HEAD_DIM:part * aw + (hh + 1) * HEAD_DIM] = (
                    t * cos - pltpu.roll(t, HEAD_DIM // 2, 1) * sin).astype(BF16)
        out_ref[:, 2 * aw:3 * aw] = sum(r[...].astype(F32) for r in ins[6:9]).astype(BF16)
        out_ref[:, 3 * aw:] = du_ref[...]

    row = pl.BlockSpec((tm, aw), lambda i: (i, 0))
    tab = pl.BlockSpec((tm, HEAD_DIM), lambda i: (i, 0))
    return pl.pallas_call(
        body, name="qkvu_grad", grid=(s // tm,), in_specs=[row] * 9 + [tab, tab, row],
        out_specs=pl.BlockSpec((tm, 4 * aw), lambda i: (i, 0)),
        out_shape=jax.ShapeDtypeStruct((s, 4 * aw), BF16),
        compiler_params=_params(("parallel",)),
    )(*dqs, *dks, *dvs, cos_t, sin_t, du)


def _in_proj_bwd_x(name, dh, w_in_st, shard0, base, scale_base):
    s, kdim = dh.shape
    d, n_sh = w_in_st.shape[1], w_in_st.shape[2]
    tm, tk = min(s, 512), min(n_sh, 2048)
    per = n_sh // tk

    ch = min(tm, 2 * EPILOGUE_ROWS)

    def epilogue(acc_ref, ex, outs):
        for c in range(tm // ch):
            rows = slice(c * ch, (c + 1) * ch)
            outs[0][rows, :] = scale_base * ex[0][rows, :] + acc_ref[rows, :]

    row = pl.BlockSpec((tm, d), lambda i, j, k: (i, 0))
    return _mm(
        name, (s // tm, 1, kdim // tk), dh, pl.BlockSpec((tm, tk), lambda i, j, k: (i, k)),
        w_in_st, pl.BlockSpec((None, d, tk), lambda i, j, k: (shard0 + k // per, 0, k % per)), NT,
        [base], [row], [jax.ShapeDtypeStruct((s, d), F32)], [row], epilogue, (tm, d),
        acc_as_ref=True)[0]


def _chip_peers():
    x, y, c = lax.axis_index("x"), lax.axis_index("y"), lax.axis_index("c")
    return x, y, c, [(1 - x, y), (x, 1 - y), (1 - x, 1 - y)]


def _exchange_descriptor(gather, src, land, send, recv, p, peer, me, c, arriving):
    px, py = peer
    pid = 2 * px + py
    if gather:
        src_ref, dst_ref = src, land.at[pid if arriving else me]
    else:
        src_ref, dst_ref = src.at[pid], land.at[p]
    return pltpu.make_async_remote_copy(
        src_ref=src_ref, dst_ref=dst_ref, send_sem=send.at[p], recv_sem=recv.at[p],
        device_id=(px, py, c), device_id_type=MESH)


def _exchange_start(name, gather, srcs, land_shapes):
    n = len(srcs)
    lands = [_hbm(lax.empty(shape, src.dtype)) for shape, src in zip(land_shapes, srcs)]

    def body(*refs):
        src_refs, land_refs = refs[:n], refs[n:2 * n]
        sends, recvs = refs[2 * n:3 * n], refs[3 * n:4 * n]
        token = refs[6 * n]
        x, y, c, peers = _chip_peers()
        me = 2 * x + y
        for w in range(n):
            for p, peer in enumerate(peers):
                _exchange_descriptor(gather, src_refs[w], land_refs[w], sends[w], recvs[w], p, peer,
                                     me, c, arriving=False).start()
        token[...] = jnp.zeros_like(token)

    sem = pltpu.SemaphoreType.DMA((3,))
    outs = pl.pallas_call(
        body, name=name, in_specs=[HBM_SPEC] * (2 * n),
        out_specs=[SEM_SPEC] * (2 * n) + [HBM_SPEC] * (2 * n) + [pl.BlockSpec(memory_space=pltpu.VMEM)],
        out_shape=[sem] * (2 * n) + [pltpu.HBM(a.shape, a.dtype) for a in (*srcs, *lands)]
        + [jax.ShapeDtypeStruct((8, 128), F32)],
        input_output_aliases={i: 2 * n + i for i in range(2 * n)},
        compiler_params=pltpu.CompilerParams(has_side_effects=DATAFLOW),
    )(*[_hbm(a) for a in srcs], *lands)
    return {"send": outs[:n], "recv": outs[n:2 * n], "src": outs[2 * n:3 * n],
            "land": outs[3 * n:4 * n], "token": outs[4 * n]}


def _exchange_wait(name, gather, started, which, after):
    m = len(which)

    def body(*refs):
        src_refs, land_refs = refs[:m], refs[m:2 * m]
        sends, recvs = refs[2 * m:3 * m], refs[3 * m:4 * m]
        x, y, c, peers = _chip_peers()
        me = 2 * x + y
        for w in range(m):
            for p, peer in enumerate(peers):
                _exchange_descriptor(gather, src_refs[w], land_refs[w], sends[w], recvs[w], p, peer,
                                     me, c, arriving=False).wait_send()
                _exchange_descriptor(gather, src_refs[w], land_refs[w], sends[w], recvs[w], p, peer,
                                     me, c, arriving=True).wait_recv()

    pick = lambda key: [started[key][w] for w in which]
    bufs = pick("src") + pick("land")
    outs = pl.pallas_call(
        body, name=name, in_specs=[HBM_SPEC] * (2 * m) + [SEM_SPEC] * (2 * m) + [ANY_SPEC],
        out_specs=[HBM_SPEC] * (2 * m), out_shape=[pltpu.HBM(a.shape, a.dtype) for a in bufs],
        input_output_aliases={i: i for i in range(2 * m)},
        compiler_params=pltpu.CompilerParams(has_side_effects=DATAFLOW),
    )(*bufs, *pick("send"), *pick("recv"), after)
    return outs[:m], outs[m:]


def _place_own(name, shards, lands):
    n = len(shards)

    def body(*refs):
        shard_refs, out_refs, sem = refs[:n], refs[2 * n:3 * n], refs[3 * n]
        me = 2 * lax.axis_index("x") + lax.axis_index("y")
        copies = [pltpu.make_async_copy(shard_refs[w], out_refs[w].at[me], sem.at[w]) for w in range(n)]
        for cp in copies:
            cp.start()
        for cp in copies:
            cp.wait()

    return pl.pallas_call(
        body, name=name, in_specs=[ANY_SPEC] * (2 * n), out_specs=[ANY_SPEC] * n,
        out_shape=[jax.ShapeDtypeStruct(a.shape, a.dtype) for a in lands],
        input_output_aliases={n + i: i for i in range(n)},
        scratch_shapes=[pltpu.SemaphoreType.DMA((n,))],
    )(*shards, *lands)


def _sum_slabs(name, grads, land, me):
    _, r, c = grads.shape
    tm = min(r, 256)

    def body(me_ref, own_ref, land_ref, out_ref):
        acc = own_ref[...].astype(F32)
        for p in range(3):
            acc = acc + land_ref[p].astype(F32)
        out_ref[...] = acc

    return pl.pallas_call(
        body, name=name,
        grid_spec=pltpu.PrefetchScalarGridSpec(
            num_scalar_prefetch=1, grid=(r // tm,),
            in_specs=[pl.BlockSpec((None, tm, c), lambda i, me_ref: (me_ref[0], i, 0)),
                      pl.BlockSpec((3, tm, c), lambda i, me_ref: (0, i, 0))],
            out_specs=pl.BlockSpec((tm, c), lambda i, me_ref: (i, 0))),
        out_shape=jax.ShapeDtypeStruct((r, c), F32), compiler_params=_params(("parallel",)),
    )(me, grads, land)


def _exchange_cores(name, parts):
    n = len(parts)

    def body(*refs):
        ins, outs = refs[:n], refs[n:2 * n]
        send, recv = refs[2 * n:]
        x, y, c = lax.axis_index("x"), lax.axis_index("y"), lax.axis_index("c")
        copies = [pltpu.make_async_remote_copy(
            src_ref=ins[w], dst_ref=outs[w], send_sem=send.at[w], recv_sem=recv.at[w],
            device_id=(x, y, 1 - c), device_id_type=MESH) for w in range(n)]
        for cp in copies:
            cp.start()
        for cp in copies:
            cp.wait()

    return pl.pallas_call(
        body, name=name, in_specs=[ANY_SPEC] * n, out_specs=[ANY_SPEC] * n,
        out_shape=[jax.ShapeDtypeStruct(p.shape, p.dtype) for p in parts],
        scratch_shapes=[pltpu.SemaphoreType.DMA((n,)), pltpu.SemaphoreType.DMA((n,))],
    )(*parts)


def _allreduce_stats(stats):
    n = len(stats)

    def body(*refs):
        ins, outs = refs[:n], refs[n:2 * n]
        mine, gath = refs[2 * n:3 * n], refs[3 * n:4 * n]
        send, recv = refs[4 * n:]
        x, y, c = lax.axis_index("x"), lax.axis_index("y"), lax.axis_index("c")
        me = 4 * x + 2 * y + c
        flips = [(bx, by, bc) for bx in (0, 1) for by in (0, 1) for bc in (0, 1)][1:]

        def peer(f):
            return (x + f[0] * (1 - 2 * x), y + f[1] * (1 - 2 * y), c + f[2] * (1 - 2 * c))

        copies = []
        for t in range(n):
            tot = ins[t][0]
            for b in range(1, ins[t].shape[0]):
                tot = tot + ins[t][b]
            mine[t][...] = tot
            gath[t][me] = tot
            for k, f in enumerate(flips):
                cp = pltpu.make_async_remote_copy(
                    src_ref=mine[t], dst_ref=gath[t].at[me], send_sem=send.at[t, k],
                    recv_sem=recv.at[t, k], device_id=peer(f), device_id_type=MESH)
                cp.start()
                copies.append(cp)
        for t in range(n):
            for k, f in enumerate(flips):
                px, py, pc = peer(f)
                pltpu.make_async_remote_copy(
                    src_ref=mine[t], dst_ref=gath[t].at[4 * px + 2 * py + pc], send_sem=send.at[t, k],
                    recv_sem=recv.at[t, k], device_id=(px, py, pc), device_id_type=MESH).wait_recv()
        for cp in copies:
            cp.wait_send()
        for t in range(n):
            tot = gath[t][0]
            for dev in range(1, 8):
                tot = tot + gath[t][dev]
            outs[t][...] = tot

    vm = pl.BlockSpec(memory_space=pltpu.VMEM)
    return pl.pallas_call(
        body, name="allreduce_stats", in_specs=[vm] * n, out_specs=[vm] * n,
        out_shape=[jax.ShapeDtypeStruct(s.shape[1:], F32) for s in stats],
        scratch_shapes=[pltpu.VMEM(s.shape[1:], F32) for s in stats]
        + [pltpu.VMEM((8, *s.shape[1:]), F32) for s in stats]
        + [pltpu.SemaphoreType.DMA((n, 7)), pltpu.SemaphoreType.DMA((n, 7))],
    )(*stats)


def _adamw(name, w, m, v, g_parts):
    r, c = w.shape
    tm = min(r, 128)
    n_g = len(g_parts)

    def body(*refs):
        w_ref, m_ref, v_ref = refs[:3]
        g_refs = refs[3:3 + n_g]
        g_out, d_out, m_out, v_out = refs[3 + n_g:]
        g = g_refs[0][...]
        for gr in g_refs[1:]:
            g = g + gr[...]
        m_new = ADAM_B1 * m_ref[...] + (1.0 - ADAM_B1) * g
        v_new = ADAM_B2 * v_ref[...] + (1.0 - ADAM_B2) * (g * g)
        m_hat = m_new / (1.0 - ADAM_B1 ** ADAM_STEP)
        v_hat = v_new / (1.0 - ADAM_B2 ** ADAM_STEP)
        g_out[...] = g
        d_out[...] = -ADAM_LR * (m_hat / (jnp.sqrt(v_hat) + ADAM_EPS) + ADAM_WD * w_ref[...])
        m_out[...] = m_new
        v_out[...] = v_new

    blk = pl.BlockSpec((tm, c), lambda i: (i, 0))
    out = jax.ShapeDtypeStruct((r, c), F32)
    return pl.pallas_call(
        body, name=name, grid=(r // tm,), in_specs=[blk] * (3 + n_g), out_specs=[blk] * 4,
        out_shape=[out] * 4, compiler_params=_params(("parallel",)),
    )(w, m, v, *g_parts)


def _rope_tables(positions):
    half = HEAD_DIM // 2
    inv_freq = ROPE_THETA ** (-jnp.arange(half, dtype=F32) / half)
    ang = positions.astype(F32)[0, :, None] * inv_freq
    cos, sin = jnp.cos(ang), jnp.sin(ang)
    return jnp.concatenate([cos, cos], axis=-1), jnp.concatenate([-sin, sin], axis=-1)


def kernel(x, positions, w_in, w_pool, pool_scale, w_branch_attn, w_branch_pool, w_out, ln_mix_g, ln_mix_b, w_ff1, w_ff2, ln_ff_g, ln_ff_b, loss_target, m_w_in, m_w_pool, m_pool_scale, m_w_branch_attn, m_w_branch_pool, m_w_out, m_ln_mix_g, m_ln_mix_b, m_w_ff1, m_w_ff2, m_ln_ff_g, m_ln_ff_b, v_w_in, v_w_pool, v_pool_scale, v_w_branch_attn, v_w_branch_pool, v_w_out, v_ln_mix_g, v_ln_mix_b, v_w_ff1, v_w_ff2, v_ln_ff_g, v_ln_ff_b):
    s, d = x.shape[1], x.shape[2]
    aw = d // 2
    ng = len(POOL_WINDOWS)
    pgw = aw // ng
    x2d, target = x[0], loss_target[0]
    xb = x2d.astype(BF16)
    cos_t, sin_t = _rope_tables(positions)

    big = {"w_in": w_in[0], "w_pool": w_pool[0].reshape(-1, pgw), "w_branch_attn": w_branch_attn[0],
           "w_branch_pool": w_branch_pool[0], "w_out": w_out[0], "w_ff1": w_ff1[0], "w_ff2": w_ff2[0]}
    names = list(big)
    me_chip = (2 * lax.axis_index("x") + lax.axis_index("y")).astype(jnp.int32).reshape(1)
    shards = [big[k].astype(BF16) for k in names]
    gathering = _exchange_start("gather_start", True, shards, [(N_CHIPS, *a.shape) for a in shards])

    def gathered(name, which, after):
        srcs, lands = _exchange_wait(f"gather_wait_{name}", True, gathering, which, after)
        return _place_own(f"place_own_{name}", srcs, lands)

    rows_sh = pgw // N_CHIPS
    dff = N_CHIPS * big["w_ff2"].shape[0]

    w_in_st, wp_st = gathered("in", [0, 1], gathering["token"])
    wp = wp_st.reshape(N_CHIPS, ng, rows_sh, pgw).transpose(1, 0, 2, 3).reshape(ng, pgw, pgw)
    h = _in_proj(xb, w_in_st, cos_t, sin_t, aw)

    def to_view(a, dil):
        return a.reshape(s // dil, dil * a.shape[1])

    qkv = {1: (h, h, h)}
    offs = {1: (0, 1, 2)}
    for dil in DILATIONS[1:]:
        qkv[dil] = tuple(to_view(h[:, i * aw:(i + 1) * aw], dil) for i in range(3))
        offs[dil] = (0, 0, 0)
    o_parts, lse_parts = [], []
    for dil in DILATIONS:
        o_p, lse_p = _attn_fwd(f"attn_fwd_d{dil}", *qkv[dil], offs[dil], aw, dil * aw)
        o_parts.append(o_p.reshape(s, aw))
        lse_parts.append(lse_p.reshape(s, aw))
    o_attn, lse = _attn_combine(o_parts, lse_parts)
    p, pw, y = _pool_fwd(h, wp, pool_scale, aw)
    wba_st, wbp_st, w_out_st = gathered("mix", [2, 3, 4], y)
    w_out_full = w_out_st.reshape(d, d)
    ya, yp, merged = _branch_merge(o_attn, y, wba_st, wbp_st, h, aw, d)
    xhat1, rstd1, x1b = _mix_norm(merged, w_out_full, x2d, ln_mix_g, ln_mix_b)
    w1_st, w2_st = gathered("ff", [5, 6], x1b)
    w2_full = w2_st.reshape(dff, d)
    r = _ff_up(x1b, w1_st)
    dz2, dz2b, st2 = _ff_down_loss(r, w2_full, xhat1, ln_mix_g, ln_mix_b, ln_ff_g, ln_ff_b, target)

    def scatter_start(name, grads):
        return _exchange_start(f"scatter_start_{name}", False, grads, [(3, *g.shape[1:]) for g in grads])

    da = _ff_down_bwd(dz2b, w2_full, r)
    g_w2 = _wgrad("wgrad_ff2", r, dz2b, d).reshape(N_CHIPS, dff // N_CHIPS, d)
    g_w1 = _wgrad("wgrad_ff1", x1b, da, dff // N_CHIPS)
    sent_ff = scatter_start("ff", [g_w1, g_w2])
    dz1, dz1b, st1 = _ff_up_bwd(da, w1_st, dz2, xhat1, _after(sent_ff["token"], rstd1), ln_mix_g)
    dya, dyp, dgate = _mix_bwd(dz1b, w_out_full, h, ya, yp, aw)
    g_wout = _wgrad("wgrad_out", merged, dz1b, d).reshape(N_CHIPS, d // N_CHIPS, d)
    g_wba = _wgrad("wgrad_branch_attn", o_attn, dya, d // N_CHIPS)
    g_wbp = _wgrad("wgrad_branch_pool", y, dyp, d // N_CHIPS)
    do, delta = _attn_out_bwd(dya, wba_st, o_attn)
    dpw, stp = _pool_out_bwd(dyp, wbp_st, pw, pool_scale)
    dwp, du = _pool_bwd(dpw, p, wp)
    g_wp = dwp.reshape(ng, N_CHIPS, rows_sh, pgw).transpose(1, 0, 2, 3).reshape(
        N_CHIPS, ng * rows_sh, pgw).astype(BF16)
    sent_mix = scatter_start("mix", [g_wp, g_wba, g_wbp, g_wout])
    do = _after(sent_mix["token"], do)

    dqs, dks, dvs = [], [], []
    for dil in DILATIONS:
        dov, lsev, dlv = (to_view(a, dil) for a in (do, lse, delta))
        args = (*qkv[dil], dov, lsev, dlv, offs[dil], aw, dil * aw)
        dqs.append(_attn_bwd_q(f"attn_bwd_q_d{dil}", *args).reshape(s, aw))
        dk_p, dv_p = _attn_bwd_kv(f"attn_bwd_kv_d{dil}", *args)
        dks.append(dk_p.reshape(s, aw))
        dvs.append(dv_p.reshape(s, aw))
    dqkvu = _qkvu_grad(dqs, dks, dvs, du, cos_t, sin_t)
    g_win_a = _wgrad("wgrad_in_qkvu", xb, dqkvu, d)
    g_win_b = _wgrad("wgrad_in_gates", xb, dgate, d)
    g_win = jnp.concatenate([g_win_a, g_win_b], axis=0)
    sent_in = scatter_start("in", [g_win])
    dx_a = _in_proj_bwd_x("in_proj_bwd_qkvu", dqkvu, w_in_st, 0, _after(sent_in["token"], dz1), ALPHA)
    grad_x = _in_proj_bwd_x("in_proj_bwd_gates", dgate, w_in_st, 2, dx_a, 1.0)

    def reduced(name, sent, keys, after):
        srcs, lands = _exchange_wait(f"scatter_wait_{name}", False, sent, list(range(len(keys))), after)
        parts = [_sum_slabs(f"sum_slabs_{k}", srcs[i], lands[i], me_chip) for i, k in enumerate(keys)]
        other = _exchange_cores(f"exchange_cores_{name}", parts)
        return {k: [parts[i], other[i]] for i, k in enumerate(keys)}

    g_parts = {}
    g_parts.update(reduced("ff", sent_ff, ["w_ff1", "w_ff2"], dx_a))
    g_parts.update(reduced("mix", sent_mix, ["w_pool", "w_branch_attn", "w_branch_pool", "w_out"], dx_a))
    g_parts.update(reduced("in", sent_in, ["w_in"], grad_x))
    tot2, tot1, totp = _allreduce_stats([st2, st1, stp])

    moments = {"w_in": (m_w_in, v_w_in), "w_pool": (m_w_pool, v_w_pool),
               "w_branch_attn": (m_w_branch_attn, v_w_branch_attn),
               "w_branch_pool": (m_w_branch_pool, v_w_branch_pool), "w_out": (m_w_out, v_w_out),
               "w_ff1": (m_w_ff1, v_w_ff1), "w_ff2": (m_w_ff2, v_w_ff2)}
    originals = {"w_in": w_in, "w_pool": w_pool, "w_branch_attn": w_branch_attn,
                 "w_branch_pool": w_branch_pool, "w_out": w_out, "w_ff1": w_ff1, "w_ff2": w_ff2}
    res = {}
    for i, k in enumerate(names):
        shape2d = big[k].shape
        mk, vk = (a.reshape(shape2d) for a in moments[k])
        outs = _adamw(f"adamw_{k}", big[k], mk, vk, g_parts[k])
        res[k] = [o.reshape(originals[k].shape) for o in outs]

    def pad_d(a):
        return jnp.pad(a, ((0, 0), (0, d - a.shape[1])))

    small = ["ln_mix_g", "ln_mix_b", "ln_ff_g", "ln_ff_b", "pool_scale"]
    small_w = {"ln_mix_g": ln_mix_g, "ln_mix_b": ln_mix_b, "ln_ff_g": ln_ff_g, "ln_ff_b": ln_ff_b,
               "pool_scale": pool_scale}
    small_m = {"ln_mix_g": m_ln_mix_g, "ln_mix_b": m_ln_mix_b, "ln_ff_g": m_ln_ff_g,
               "ln_ff_b": m_ln_ff_b, "pool_scale": m_pool_scale}
    small_v = {"ln_mix_g": v_ln_mix_g, "ln_mix_b": v_ln_mix_b, "ln_ff_g": v_ln_ff_g,
               "ln_ff_b": v_ln_ff_b, "pool_scale": v_pool_scale}
    small_g = [tot1[0:1], tot1[1:2], tot2[0:1], tot2[1:2], pad_d(totp[0:1])]

    def pack(rows):
        return jnp.concatenate([pad_d(a) for a in rows] + [jnp.zeros((8 - len(rows), d), F32)], axis=0)

    outs = _adamw("adamw_small", pack([small_w[k] for k in small]), pack([small_m[k] for k in small]),
                  pack([small_v[k] for k in small]), [pack(small_g)])
    for i, k in enumerate(small):
        res[k] = [o[i:i + 1, :small_w[k].shape[1]] for o in outs]
    loss = tot2[2, 0]

    order = ["w_in", "w_pool", "pool_scale", "w_branch_attn", "w_branch_pool", "w_out", "ln_mix_g",
             "ln_mix_b", "w_ff1", "w_ff2", "ln_ff_g", "ln_ff_b"]
    result = [loss, grad_x[None]]
    for idx in range(4):
        result += [res[k][idx] for k in order]
    return tuple(result)
```

```python
import jax
import jax.numpy as jnp
from jax import lax
from jax.experimental import pallas as pl
from jax.experimental.pallas import tpu as pltpu

F32 = jnp.float32
BF16 = jnp.bfloat16
MESH = pl.DeviceIdType.MESH

HEAD_DIM = 128
SUB_BLOCK = 128
DILATIONS = (1, 4, 16)
POOL_WINDOWS = (2, 4, 8, 16)
POOL_HALO = 16
ROPE_THETA = 10000.0
LN_EPS = 1e-5
ALPHA = 2.0 ** 0.25
ADAM_LR, ADAM_B1, ADAM_B2, ADAM_EPS, ADAM_WD, ADAM_STEP = 0.001, 0.9, 0.999, 1e-08, 0.01, 10
NEG = -1e30
N_CHIPS = 4
VMEM_LIMIT = 56 * 1024 * 1024
EPILOGUE_ROWS = 128


def _params(sem=None, vmem=VMEM_LIMIT):
    kw = {"vmem_limit_bytes": vmem}
    if sem is not None:
        kw["dimension_semantics"] = sem
    return pltpu.CompilerParams(**kw)


def _dot(a, b, contract):
    return lax.dot_general(a, b, (contract, ((), ())), preferred_element_type=F32)


ANY_SPEC = pl.BlockSpec(memory_space=pl.ANY)
HBM_SPEC = pl.BlockSpec(memory_space=pltpu.HBM)
SEM_SPEC = pl.BlockSpec(memory_space=pltpu.SEMAPHORE)
DATAFLOW = pltpu.SideEffectType.DATAFLOW_SIDE_EFFECTING


def _hbm(a):
    return pltpu.with_memory_space_constraint(a, pltpu.HBM)


NN = ((1,), (0,))
NT = ((1,), (1,))
TN = ((0,), (0,))


def _mm(name, grid, a, a_spec, b, b_spec, contract, extras, extra_specs, out_shape, out_specs,
        epilogue, acc_shape, acc_as_ref=False, run_after=()):
    nk = grid[2]
    n_ex = len(extras)
    n_in = 2 + n_ex + len(run_after)
    n_out = len(out_shape)

    def body(*refs):
        a_ref, b_ref = refs[0], refs[1]
        ex = refs[2:2 + n_ex]
        outs = refs[n_in:n_in + n_out]
        if nk == 1:
            epilogue(_dot(a_ref[...], b_ref[...], contract), ex, outs)
        else:
            acc = refs[n_in + n_out]
            k = pl.program_id(2)

            @pl.when(k == 0)
            def _():
                acc[...] = jnp.zeros_like(acc)

            acc[...] += _dot(a_ref[...], b_ref[...], contract)

            @pl.when(k == nk - 1)
            def _():
                epilogue(acc if acc_as_ref else acc[...], ex, outs)

    scratch = [pltpu.VMEM(acc_shape, F32)] if nk > 1 else []
    return pl.pallas_call(
        body, name=name, grid=grid,
        in_specs=[a_spec, b_spec, *extra_specs, *[ANY_SPEC] * len(run_after)], out_specs=out_specs,
        out_shape=out_shape, scratch_shapes=scratch,
        compiler_params=_params(("parallel", "parallel", "arbitrary")),
    )(a, b, *extras, *run_after)


def _stats_rows(rows, width):
    idx = lax.broadcasted_iota(jnp.int32, (8, width), 0)
    out = jnp.zeros((8, width), F32)
    for r, v in enumerate(rows):
        out = jnp.where(idx == r, jnp.broadcast_to(v, (8, width)), out)
    return out


def _layer_norm_fwd(z):
    mu = jnp.mean(z, axis=-1, keepdims=True)
    zc = z - mu
    var = jnp.mean(zc * zc, axis=-1, keepdims=True)
    rstd = lax.rsqrt(var + LN_EPS)
    return zc * rstd, rstd


def _layer_norm_bwd(dy, xhat, rstd, g):
    dxh = dy * g
    m1 = jnp.mean(dxh, axis=-1, keepdims=True)
    m2 = jnp.mean(dxh * xhat, axis=-1, keepdims=True)
    return rstd * (dxh - m1 - xhat * m2)


def _in_proj(xb, w_in_st, cos_t, sin_t, aw):
    s, d = xb.shape
    n_sh = w_in_st.shape[2]
    tm, tn = min(s, 1024), aw
    per = n_sh // tn
    grid = (s // tm, (N_CHIPS * n_sh) // tn, 1)

    def epilogue(acc, ex, outs):
        cos_ref, sin_ref = ex
        (h_ref,) = outs
        seg = pl.program_id(1)

        @pl.when(seg < 2)
        def _():
            cos, sin = cos_ref[...], sin_ref[...]
            for hh in range(tn // HEAD_DIM):
                sl = slice(hh * HEAD_DIM, (hh + 1) * HEAD_DIM)
                t = acc[:, sl]
                h_ref[:, sl] = (t * cos + pltpu.roll(t, HEAD_DIM // 2, 1) * sin).astype(BF16)

        @pl.when((seg >= 2) & (seg < 4))
        def _():
            h_ref[...] = acc.astype(BF16)

        @pl.when(seg >= 4)
        def _():
            h_ref[...] = jax.nn.sigmoid(acc).astype(BF16)

    return _mm(
        "in_proj", grid, xb, pl.BlockSpec((tm, d), lambda i, j, k: (i, 0)),
        w_in_st, pl.BlockSpec((None, d, tn), lambda i, j, k: (j // per, 0, j % per)), NN,
        [cos_t, sin_t], [pl.BlockSpec((tm, HEAD_DIM), lambda i, j, k: (i, 0))] * 2,
        [jax.ShapeDtypeStruct((s, N_CHIPS * n_sh), BF16)],
        [pl.BlockSpec((tm, tn), lambda i, j, k: (i, j))], epilogue, None)[0]


def _band_masks(block_idx):
    qi = lax.broadcasted_iota(jnp.int32, (SUB_BLOCK, 2 * SUB_BLOCK), 0)
    kj = lax.broadcasted_iota(jnp.int32, (SUB_BLOCK, 2 * SUB_BLOCK), 1)
    first_key = jnp.where(block_idx > 0, 0, SUB_BLOCK)
    return (kj >= qi) & (kj <= qi + SUB_BLOCK) & (kj >= first_key)


def _attn_fwd(name, q, k, v, offs, cw, width):
    m = q.shape[0]
    nh = cw // HEAD_DIM
    grid = (m // SUB_BLOCK, width // cw)
    scale = HEAD_DIM ** -0.5

    def body(q_ref, kp_ref, kc_ref, vp_ref, vc_ref, o_ref, lse_ref):
        valid = _band_masks(pl.program_id(0))
        for hh in range(nh):
            sl = slice(hh * HEAD_DIM, (hh + 1) * HEAD_DIM)
            kcat = jnp.concatenate([kp_ref[:, sl], kc_ref[:, sl]], axis=0)
            vcat = jnp.concatenate([vp_ref[:, sl], vc_ref[:, sl]], axis=0)
            sc = _dot(q_ref[:, sl], kcat, NT) * scale
            sc = jnp.where(valid, sc, NEG)
            mx = jnp.max(sc, axis=-1, keepdims=True)
            p = jnp.exp(sc - mx)
            l = jnp.sum(p, axis=-1, keepdims=True)
            o = _dot(p.astype(BF16), vcat, NN) / l
            o_ref[:, sl] = o.astype(BF16)
            lse_ref[:, sl] = jnp.broadcast_to(mx + jnp.log(l), (SUB_BLOCK, HEAD_DIM))

    def cur(off):
        return pl.BlockSpec((SUB_BLOCK, cw), lambda n, j: (n, j + off))

    def prev(off):
        return pl.BlockSpec((SUB_BLOCK, cw), lambda n, j: (jnp.maximum(n - 1, 0), j + off))

    return pl.pallas_call(
        body, name=name, grid=grid,
        in_specs=[cur(offs[0]), prev(offs[1]), cur(offs[1]), prev(offs[2]), cur(offs[2])],
        out_specs=[cur(0), cur(0)],
        out_shape=[jax.ShapeDtypeStruct((m, width), BF16), jax.ShapeDtypeStruct((m, width), F32)],
        compiler_params=_params(("parallel", "parallel")),
    )(q, k, k, v, v)


def _attn_combine(os_, lses):
    s, aw = os_[0].shape
    tm = min(s, 512)

    def body(o1, o2, o3, l1, l2, l3, o_ref, lse_ref):
        a, b, c = l1[...], l2[...], l3[...]
        mx = jnp.maximum(jnp.maximum(a, b), c)
        ea, eb, ec = jnp.exp(a - mx), jnp.exp(b - mx), jnp.exp(c - mx)
        tot = ea + eb + ec
        o = (ea * o1[...].astype(F32) + eb * o2[...].astype(F32) + ec * o3[...].astype(F32)) / tot
        o_ref[...] = o.astype(BF16)
        lse_ref[...] = mx + jnp.log(tot)

    spec = pl.BlockSpec((tm, aw), lambda i: (i, 0))
    return pl.pallas_call(
        body, name="attn_combine", grid=(s // tm,), in_specs=[spec] * 6, out_specs=[spec, spec],
        out_shape=[jax.ShapeDtypeStruct((s, aw), BF16), jax.ShapeDtypeStruct((s, aw), F32)],
        compiler_params=_params(("parallel",)),
    )(*os_, *lses)


def _pool_counts(tm, rows, pgw, row0):
    t = lax.broadcasted_iota(jnp.int32, (rows, len(POOL_WINDOWS) * pgw), 0) + row0
    col = lax.broadcasted_iota(jnp.int32, (rows, len(POOL_WINDOWS) * pgw), 1)
    w = jnp.full((rows, len(POOL_WINDOWS) * pgw), POOL_WINDOWS[0], jnp.int32)
    for g in range(1, len(POOL_WINDOWS)):
        w = jnp.where(col >= g * pgw, POOL_WINDOWS[g], w)
    return jnp.minimum(t + 1, w).astype(F32)


def _window_sums(xs, direction, pgw):
    rows = xs.shape[0]
    acc = xs
    out = None
    col = lax.broadcasted_iota(jnp.int32, xs.shape, 1)
    for g, w in enumerate(POOL_WINDOWS):
        sh = w // 2
        acc = acc + pltpu.roll(acc, sh if direction > 0 else rows - sh, 0)
        out = acc if out is None else jnp.where(col >= g * pgw, acc, out)
    return out


def _pool_fwd(h, wp, scale, aw):
    s = h.shape[0]
    pw_ = aw
    pgw = pw_ // len(POOL_WINDOWS)
    tm = min(s, 512)
    hb = tm // POOL_HALO

    def body(u_ref, halo_ref, wp_ref, sc_ref, p_ref, pw_ref, y_ref):
        i = pl.program_id(0)
        u = u_ref[...].astype(F32)
        halo = halo_ref[...].astype(F32) * jnp.where(i > 0, 1.0, 0.0)
        xs = jnp.concatenate([halo, u], axis=0)
        sums = _window_sums(xs, +1, pgw)[POOL_HALO:]
        p = (sums / _pool_counts(tm, tm, pgw, i * tm) - u).astype(BF16)
        p_ref[...] = p
        sc = sc_ref[...]
        for g in range(len(POOL_WINDOWS)):
            sl = slice(g * pgw, (g + 1) * pgw)
            pw = _dot(p[:, sl], wp_ref[g], NN)
            pw_ref[:, sl] = pw.astype(BF16)
            y_ref[:, sl] = (pw * sc[:, sl]).astype(BF16)

    out = jax.ShapeDtypeStruct((s, pw_), BF16)
    row = pl.BlockSpec((tm, pw_), lambda i: (i, 0))
    return pl.pallas_call(
        body, name="pool_fwd", grid=(s // tm,),
        in_specs=[pl.BlockSpec((tm, pw_), lambda i: (i, 3)),
                  pl.BlockSpec((POOL_HALO, pw_), lambda i: (jnp.maximum(i * hb - 1, 0), 3)),
                  pl.BlockSpec(wp.shape, lambda i: (0, 0, 0)),
                  pl.BlockSpec((1, pw_), lambda i: (0, 0))],
        out_specs=[row, row, row], out_shape=[out, out, out],
        compiler_params=_params(("parallel",)),
    )(h, h, wp, scale)


def _branch_merge(o_attn, y, wba_st, wbp_st, h, aw, d):
    s = o_attn.shape[0]
    tn = wba_st.shape[2]
    tm = min(s, 1024)
    ga0 = 4 * aw // tn
    gp0 = (4 * aw + d) // tn

    def body(o_ref, y_ref, wa_ref, wp_ref, sga_ref, sgp_ref, ya_ref, yp_ref, mg_ref):
        ya = _dot(o_ref[...], wa_ref[...], NN)
        yp = _dot(y_ref[...], wp_ref[...], NN)
        ya_ref[...] = ya.astype(BF16)
        yp_ref[...] = yp.astype(BF16)
        mg_ref[...] = (sga_ref[...].astype(F32) * ya + sgp_ref[...].astype(F32) * yp).astype(BF16)

    out = jax.ShapeDtypeStruct((s, d), BF16)
    blk = pl.BlockSpec((tm, tn), lambda i, j: (i, j))
    return pl.pallas_call(
        body, name="branch_merge", grid=(s // tm, N_CHIPS),
        in_specs=[pl.BlockSpec((tm, aw), lambda i, j: (i, 0)),
                  pl.BlockSpec((tm, aw), lambda i, j: (i, 0)),
                  pl.BlockSpec((None, aw, tn), lambda i, j: (j, 0, 0)),
                  pl.BlockSpec((None, aw, tn), lambda i, j: (j, 0, 0)),
                  pl.BlockSpec((tm, tn), lambda i, j: (i, j + ga0)),
                  pl.BlockSpec((tm, tn), lambda i, j: (i, j + gp0))],
        out_specs=[blk, blk, blk], out_shape=[out, out, out],
        compiler_params=_params(("parallel", "parallel")),
    )(o_attn, y, wba_st, wbp_st, h, h)


def _mix_norm(merged, w_out, x, g1, b1):
    s, d = x.shape
    tm = min(s, 256)

    def epilogue(acc, ex, outs):
        x_ref, g_ref, b_ref = ex
        xh_ref, rs_ref, xb_ref = outs
        xhat, rstd = _layer_norm_fwd(ALPHA * x_ref[...] + acc)
        xh_ref[...] = xhat
        rs_ref[...] = rstd
        xb_ref[...] = (xhat * g_ref[...] + b_ref[...]).astype(BF16)

    row = pl.BlockSpec((tm, d), lambda i, j, k: (i, 0))
    vec = pl.BlockSpec((1, d), lambda i, j, k: (0, 0))
    return _mm(
        "mix_norm", (s // tm, 1, 1), merged, row, w_out, pl.BlockSpec((d, d), lambda i, j, k: (0, 0)),
        NN, [x, g1, b1], [row, vec, vec],
        [jax.ShapeDtypeStruct((s, d), F32), jax.ShapeDtypeStruct((s, 1), F32),
         jax.ShapeDtypeStruct((s, d), BF16)],
        [row, pl.BlockSpec((tm, 1), lambda i, j, k: (i, 0)), row], epilogue, None)


def _ff_up(x1b, w1_st):
    s, d = x1b.shape
    n_sh = w1_st.shape[2]
    tm, tn = min(s, 1024), min(n_sh, 1024)
    per = n_sh // tn

    def epilogue(acc, ex, outs):
        r = jnp.maximum(acc, 0.0)
        outs[0][...] = (r * r).astype(BF16)

    return _mm(
        "ff_up", (s // tm, N_CHIPS * per, 1), x1b, pl.BlockSpec((tm, d), lambda i, j, k: (i, 0)),
        w1_st, pl.BlockSpec((None, d, tn), lambda i, j, k: (j // per, 0, j % per)), NN, [], [],
        [jax.ShapeDtypeStruct((s, N_CHIPS * n_sh), BF16)],
        [pl.BlockSpec((tm, tn), lambda i, j, k: (i, j))], epilogue, None)[0]


def _ff_down_loss(r, w2, xhat1, g1, b1, g2, b2, target):
    s, d = xhat1.shape
    dff = r.shape[1]
    tm, tk = min(s, 512), min(dff, 1024)
    ch = min(tm, EPILOGUE_ROWS)

    def epilogue(acc_ref, ex, outs):
        xh1_ref, g1_ref, b1_ref, g2_ref, b2_ref, t_ref = ex
        dz_ref, dzb_ref, st_ref = outs
        g1v, b1v, g2v, b2v = g1_ref[...], b1_ref[...], g2_ref[...], b2_ref[...]
        dg = db = loss = None
        for c in range(tm // ch):
            rows = slice(c * ch, (c + 1) * ch)
            x1 = xh1_ref[rows, :] * g1v + b1v
            xhat2, rstd2 = _layer_norm_fwd(ALPHA * x1 + acc_ref[rows, :])
            err = xhat2 * g2v + b2v - t_ref[rows, :]
            dy = err * (1.0 / d)
            dz = _layer_norm_bwd(dy, xhat2, rstd2, g2v)
            dz_ref[rows, :] = dz
            dzb_ref[rows, :] = dz.astype(BF16)
            parts = (jnp.sum(dy * xhat2, axis=0, keepdims=True), jnp.sum(dy, axis=0, keepdims=True),
                     jnp.sum(jnp.sum(err * err, axis=-1, keepdims=True), axis=0, keepdims=True))
            dg, db, loss = parts if c == 0 else (dg + parts[0], db + parts[1], loss + parts[2])
        st_ref[...] = _stats_rows([dg, db, jnp.broadcast_to((0.5 / d) * loss, (1, d))], d)

    row = pl.BlockSpec((tm, d), lambda i, j, k: (i, 0))
    vec = pl.BlockSpec((1, d), lambda i, j, k: (0, 0))
    return _mm(
        "ff_down_loss", (s // tm, 1, dff // tk), r, pl.BlockSpec((tm, tk), lambda i, j, k: (i, k)),
        w2, pl.BlockSpec((tk, d), lambda i, j, k: (k, 0)), NN,
        [xhat1, g1, b1, g2, b2, target], [row, vec, vec, vec, vec, row],
        [jax.ShapeDtypeStruct((s, d), F32), jax.ShapeDtypeStruct((s, d), BF16),
         jax.ShapeDtypeStruct((s // tm, 8, d), F32)],
        [row, row, pl.BlockSpec((None, 8, d), lambda i, j, k: (i, 0, 0))], epilogue, (tm, d),
        acc_as_ref=True)


def _ff_down_bwd(dz2b, w2, r):
    s, d = dz2b.shape
    dff = r.shape[1]
    tm, tn = min(s, 1024), min(dff, 1024)

    def epilogue(acc, ex, outs):
        outs[0][...] = (acc * (2.0 * jnp.sqrt(ex[0][...].astype(F32)))).astype(BF16)

    blk = pl.BlockSpec((tm, tn), lambda i, j, k: (i, j))
    return _mm(
        "ff_down_bwd", (s // tm, dff // tn, 1), dz2b, pl.BlockSpec((tm, d), lambda i, j, k: (i, 0)),
        w2, pl.BlockSpec((tn, d), lambda i, j, k: (j, 0)), NT, [r], [blk],
        [jax.ShapeDtypeStruct((s, dff), BF16)], [blk], epilogue, None)[0]


def _wgrad(name, a, g, n_sh):
    s, rows = a.shape
    cols = g.shape[1]
    tm, tn, tk = min(rows, 2048), min(cols, 1024), min(s, 1024)
    if tn >= n_sh:
        span = tn // n_sh
        out_spec = pl.BlockSpec((span, tm, n_sh), lambda i, j, k: (j, i, 0))

        def epilogue(acc_ref, ex, outs):
            for sh in range(span):
                outs[0][sh] = acc_ref[:, sh * n_sh:(sh + 1) * n_sh].astype(BF16)
    else:
        per = n_sh // tn
        out_spec = pl.BlockSpec((None, tm, tn), lambda i, j, k: (j // per, i, j % per))

        def epilogue(acc_ref, ex, outs):
            outs[0][...] = acc_ref[...].astype(BF16)

    return _mm(
        name, (rows // tm, cols // tn, s // tk), a, pl.BlockSpec((tk, tm), lambda i, j, k: (k, i)),
        g, pl.BlockSpec((tk, tn), lambda i, j, k: (k, j)), TN, [], [],
        [jax.ShapeDtypeStruct((cols // n_sh, rows, n_sh), BF16)], [out_spec], epilogue,
        (tm, tn), acc_as_ref=True)[0]


def _ff_up_bwd(da, w1_st, dz2, xhat1, rstd1, g1, run_after):
    s, d = dz2.shape
    n_sh = w1_st.shape[2]
    tm, tk = min(s, 512), min(n_sh, 1024)
    per = n_sh // tk
    ch = min(tm, EPILOGUE_ROWS)

    def epilogue(acc_ref, ex, outs):
        dz2_ref, xh_ref, rs_ref, g_ref = ex
        dz_ref, dzb_ref, st_ref = outs
        gv = g_ref[...]
        dg = db = None
        for c in range(tm // ch):
            rows = slice(c * ch, (c + 1) * ch)
            dx1 = ALPHA * dz2_ref[rows, :] + acc_ref[rows, :]
            xhat = xh_ref[rows, :]
            dz = _layer_norm_bwd(dx1, xhat, rs_ref[rows, :], gv)
            dz_ref[rows, :] = dz
            dzb_ref[rows, :] = dz.astype(BF16)
            parts = (jnp.sum(dx1 * xhat, axis=0, keepdims=True), jnp.sum(dx1, axis=0, keepdims=True))
            dg, db = parts if c == 0 else (dg + parts[0], db + parts[1])
        st_ref[...] = _stats_rows([dg, db], d)

    row = pl.BlockSpec((tm, d), lambda i, j, k: (i, 0))
    return _mm(
        "ff_up_bwd", (s // tm, 1, N_CHIPS * per), da, pl.BlockSpec((tm, tk), lambda i, j, k: (i, k)),
        w1_st, pl.BlockSpec((None, d, tk), lambda i, j, k: (k // per, 0, k % per)), NT,
        [dz2, xhat1, rstd1, g1],
        [row, row, pl.BlockSpec((tm, 1), lambda i, j, k: (i, 0)), pl.BlockSpec((1, d), lambda i, j, k: (0, 0))],
        [jax.ShapeDtypeStruct((s, d), F32), jax.ShapeDtypeStruct((s, d), BF16),
         jax.ShapeDtypeStruct((s // tm, 8, d), F32)],
        [row, row, pl.BlockSpec((None, 8, d), lambda i, j, k: (i, 0, 0))], epilogue, (tm, d),
        acc_as_ref=True, run_after=run_after)


def _mix_bwd(dz1b, w_out, h, ya, yp, aw):
    s, d = dz1b.shape
    tm = min(s, 256)
    gblk = 4 * aw // d

    def epilogue(acc, ex, outs):
        sga_ref, sgp_ref, ya_ref, yp_ref = ex
        dya_ref, dyp_ref, dg_ref = outs
        sga, sgp = sga_ref[...].astype(F32), sgp_ref[...].astype(F32)
        dya_ref[...] = (acc * sga).astype(BF16)
        dyp_ref[...] = (acc * sgp).astype(BF16)
        dg_ref[:, :d] = (acc * ya_ref[...].astype(F32) * (sga * (1.0 - sga))).astype(BF16)
        dg_ref[:, d:] = (acc * yp_ref[...].astype(F32) * (sgp * (1.0 - sgp))).astype(BF16)

    row = pl.BlockSpec((tm, d), lambda i, j, k: (i, 0))
    return _mm(
        "mix_bwd", (s // tm, 1, 1), dz1b, row, w_out, pl.BlockSpec((d, d), lambda i, j, k: (0, 0)), NT,
        [h, h, ya, yp],
        [pl.BlockSpec((tm, d), lambda i, j, k: (i, gblk)),
         pl.BlockSpec((tm, d), lambda i, j, k: (i, gblk + 1)), row, row],
        [jax.ShapeDtypeStruct((s, d), BF16), jax.ShapeDtypeStruct((s, d), BF16),
         jax.ShapeDtypeStruct((s, 2 * d), BF16)],
        [row, row, pl.BlockSpec((tm, 2 * d), lambda i, j, k: (i, 0))], epilogue, None)


def _branch_in_bwd(name, dyb, wb_st, epilogue, extras, extra_specs, out_shape, out_specs):
    s = dyb.shape[0]
    aw, tk = wb_st.shape[1], wb_st.shape[2]
    tm = min(s, 1024)
    return _mm(
        name, (s // tm, 1, N_CHIPS), dyb, pl.BlockSpec((tm, tk), lambda i, j, k: (i, k)),
        wb_st, pl.BlockSpec((None, aw, tk), lambda i, j, k: (k, 0, 0)), NT,
        extras, extra_specs, out_shape, out_specs, epilogue, (tm, aw)), tm


def _attn_out_bwd(dya, wba_st, o_attn):
    s, aw = o_attn.shape

    def epilogue(acc, ex, outs):
        o = ex[0][...].astype(F32)
        do_ref, dl_ref = outs
        do_ref[...] = acc.astype(BF16)
        for hh in range(aw // HEAD_DIM):
            sl = slice(hh * HEAD_DIM, (hh + 1) * HEAD_DIM)
            dl = jnp.sum(acc[:, sl] * o[:, sl], axis=-1, keepdims=True)
            dl_ref[:, sl] = jnp.broadcast_to(dl, (acc.shape[0], HEAD_DIM))

    tm = min(s, 1024)
    row = pl.BlockSpec((tm, aw), lambda i, j, k: (i, 0))
    return _branch_in_bwd(
        "attn_out_bwd", dya, wba_st, epilogue, [o_attn], [row],
        [jax.ShapeDtypeStruct((s, aw), BF16), jax.ShapeDtypeStruct((s, aw), F32)], [row, row])[0]


def _pool_out_bwd(dyp, wbp_st, pw, scale):
    s, pw_ = pw.shape
    tm = min(s, 1024)

    def epilogue(acc, ex, outs):
        pw_ref, sc_ref = ex
        dpw_ref, st_ref = outs
        dpw_ref[...] = (acc * sc_ref[...]).astype(BF16)
        st_ref[...] = _stats_rows([jnp.sum(acc * pw_ref[...].astype(F32), axis=0, keepdims=True)], pw_)

    row = pl.BlockSpec((tm, pw_), lambda i, j, k: (i, 0))
    return _branch_in_bwd(
        "pool_out_bwd", dyp, wbp_st, epilogue, [pw, scale],
        [row, pl.BlockSpec((1, pw_), lambda i, j, k: (0, 0))],
        [jax.ShapeDtypeStruct((s, pw_), BF16), jax.ShapeDtypeStruct((s // tm, 8, pw_), F32)],
        [row, pl.BlockSpec((None, 8, pw_), lambda i, j, k: (i, 0, 0))])[0]


def _pool_bwd(dpw, p, wp):
    s, pw_ = p.shape
    ng = len(POOL_WINDOWS)
    pgw = pw_ // ng
    tm = min(s, 512)
    hb = tm // POOL_HALO
    nblk = s // tm

    def body(dpw_ref, nxt_ref, p_ref, wp_ref, dwp_ref, du_ref):
        i = pl.program_id(0)
        nxt = (nxt_ref[...].astype(F32) * jnp.where(i < nblk - 1, 1.0, 0.0)).astype(BF16)
        dpw_all = jnp.concatenate([dpw_ref[...], nxt], axis=0)

        @pl.when(i == 0)
        def _():
            dwp_ref[...] = jnp.zeros_like(dwp_ref)

        dps = []
        for g in range(ng):
            sl = slice(g * pgw, (g + 1) * pgw)
            dwp_ref[g] += _dot(p_ref[:, sl], dpw_ref[:, sl], TN)
            dps.append(_dot(dpw_all[:, sl], wp_ref[g], NT))
        dp = jnp.concatenate(dps, axis=1)
        dpn = dp / _pool_counts(tm, tm + POOL_HALO, pgw, i * tm)
        du_ref[...] = (_window_sums(dpn, -1, pgw)[:tm] - dp[:tm]).astype(BF16)

    row = pl.BlockSpec((tm, pw_), lambda i: (i, 0))
    full = pl.BlockSpec((ng, pgw, pgw), lambda i: (0, 0, 0))
    return pl.pallas_call(
        body, name="pool_bwd", grid=(nblk,),
        in_specs=[row, pl.BlockSpec((POOL_HALO, pw_), lambda i: (jnp.minimum((i + 1) * hb, s // POOL_HALO - 1), 0)),
                  row, full],
        out_specs=[full, row],
        out_shape=[jax.ShapeDtypeStruct((ng, pgw, pgw), F32), jax.ShapeDtypeStruct((s, pw_), BF16)],
        compiler_params=_params(("arbitrary",)),
    )(dpw, dpw, p, wp)


def _attn_bwd_q(name, q, k, v, do, lse, delta, offs, cw, width):
    m = do.shape[0]
    nh = cw // HEAD_DIM
    grid = (m // SUB_BLOCK, width // cw)
    scale = HEAD_DIM ** -0.5

    def body(q_ref, kp_ref, kc_ref, vp_ref, vc_ref, do_ref, lse_ref, dl_ref, dq_ref):
        valid = _band_masks(pl.program_id(0))
        for hh in range(nh):
            sl = slice(hh * HEAD_DIM, (hh + 1) * HEAD_DIM)
            kcat = jnp.concatenate([kp_ref[:, sl], kc_ref[:, sl]], axis=0)
            vcat = jnp.concatenate([vp_ref[:, sl], vc_ref[:, sl]], axis=0)
            lse2 = jnp.concatenate([lse_ref[:, sl]] * 2, axis=1)
            dl2 = jnp.concatenate([dl_ref[:, sl]] * 2, axis=1)
            sc = _dot(q_ref[:, sl], kcat, NT) * scale
            p = jnp.where(valid, jnp.exp(jnp.where(valid, sc, NEG) - lse2), 0.0)
            dp = _dot(do_ref[:, sl], vcat, NT)
            ds = (p * (dp - dl2)).astype(BF16)
            dq_ref[:, sl] = (_dot(ds, kcat, NN) * scale).astype(BF16)

    def cur(off):
        return pl.BlockSpec((SUB_BLOCK, cw), lambda n, j: (n, j + off))

    def prev(off):
        return pl.BlockSpec((SUB_BLOCK, cw), lambda n, j: (jnp.maximum(n - 1, 0), j + off))

    return pl.pallas_call(
        body, name=name, grid=grid,
        in_specs=[cur(offs[0]), prev(offs[1]), cur(offs[1]), prev(offs[2]), cur(offs[2]),
                  cur(0), cur(0), cur(0)],
        out_specs=cur(0), out_shape=jax.ShapeDtypeStruct((m, width), BF16),
        compiler_params=_params(("parallel", "parallel")),
    )(q, k, k, v, v, do, lse, delta)


def _attn_bwd_kv(name, q, k, v, do, lse, delta, offs, cw, width):
    m = do.shape[0]
    nh = cw // HEAD_DIM
    nblk = m // SUB_BLOCK
    grid = (nblk, width // cw)
    scale = HEAD_DIM ** -0.5

    def body(qc_ref, qn_ref, k_ref, v_ref, doc_ref, don_ref, lsec_ref, lsen_ref, dlc_ref, dln_ref,
             dk_ref, dv_ref):
        no_next = jnp.where(pl.program_id(0) < nblk - 1, 0, 4 * SUB_BLOCK)
        r = lax.broadcasted_iota(jnp.int32, (2 * SUB_BLOCK, SUB_BLOCK), 0)
        kj = lax.broadcasted_iota(jnp.int32, (2 * SUB_BLOCK, SUB_BLOCK), 1)
        valid = ((r < SUB_BLOCK) & (kj <= r)) | ((r >= SUB_BLOCK) & (kj >= r - SUB_BLOCK + no_next))
        for hh in range(nh):
            sl = slice(hh * HEAD_DIM, (hh + 1) * HEAD_DIM)
            qcat = jnp.concatenate([qc_ref[:, sl], qn_ref[:, sl]], axis=0)
            docat = jnp.concatenate([doc_ref[:, sl], don_ref[:, sl]], axis=0)
            lse2 = jnp.concatenate([lsec_ref[:, sl], lsen_ref[:, sl]], axis=0)
            dl2 = jnp.concatenate([dlc_ref[:, sl], dln_ref[:, sl]], axis=0)
            sc = _dot(qcat, k_ref[:, sl], NT) * scale
            p = jnp.where(valid, jnp.exp(jnp.where(valid, sc, NEG) - lse2), 0.0)
            dp = _dot(docat, v_ref[:, sl], NT)
            ds = (p * (dp - dl2)).astype(BF16)
            dv_ref[:, sl] = _dot(p.astype(BF16), docat, TN).astype(BF16)
            dk_ref[:, sl] = (_dot(ds, qcat, TN) * scale).astype(BF16)

    def cur(off):
        return pl.BlockSpec((SUB_BLOCK, cw), lambda n, j: (n, j + off))

    def nxt(off):
        return pl.BlockSpec((SUB_BLOCK, cw), lambda n, j: (jnp.minimum(n + 1, nblk - 1), j + off))

    out = jax.ShapeDtypeStruct((m, width), BF16)
    return pl.pallas_call(
        body, name=name, grid=grid,
        in_specs=[cur(offs[0]), nxt(offs[0]), cur(offs[1]), cur(offs[2]),
                  cur(0), nxt(0), cur(0), nxt(0), cur(0), nxt(0)],
        out_specs=[cur(0), cur(0)], out_shape=[out, out],
        compiler_params=_params(("parallel", "parallel")),
    )(q, q, k, v, do, do, lse, lse, delta, delta)


def _qkvu_grad(dqs, dks, dvs, du, cos_t, sin_t):
    s, aw = du.shape
    tm = min(s, 512)

    def body(*refs):
        ins, (cos_ref, sin_ref, du_ref, out_ref) = refs[:9], refs[9:]
        cos, sin = cos_ref[...], sin_ref[...]
        for part in range(2):
            tot = sum(r[...].astype(F32) for r in ins[3 * part:3 * part + 3])
            for hh in range(aw // HEAD_DIM):
                sl = slice(hh * HEAD_DIM, (hh + 1) * HEAD_DIM)
                t = tot[:, sl]
                out_ref[:, part * aw + hh * HEAD_DIM:part * aw + (hh + 1) * HEAD_DIM] = (
                    t * cos - pltpu.roll(t, HEAD_DIM // 2, 1) * sin).astype(BF16)
        out_ref[:, 2 * aw:3 * aw] = sum(r[...].astype(F32) for r in ins[6:9]).astype(BF16)
        out_ref[:, 3 * aw:] = du_ref[...]

    row = pl.BlockSpec((tm, aw), lambda i: (i, 0))
    tab = pl.BlockSpec((tm, HEAD_DIM), lambda i: (i, 0))
    return pl.pallas_call(
        body, name="qkvu_grad", grid=(s // tm,), in_specs=[row] * 9 + [tab, tab, row],
        out_specs=pl.BlockSpec((tm, 4 * aw), lambda i: (i, 0)),
        out_shape=jax.ShapeDtypeStruct((s, 4 * aw), BF16),
        compiler_params=_params(("parallel",)),
    )(*dqs, *dks, *dvs, cos_t, sin_t, du)


def _in_proj_bwd_x(name, dh, w_in_st, shard0, base, scale_base, run_after=()):
    s, kdim = dh.shape
    d, n_sh = w_in_st.shape[1], w_in_st.shape[2]
    tm, tk = min(s, 512), min(n_sh, 2048)
    per = n_sh // tk

    ch = min(tm, 2 * EPILOGUE_ROWS)

    def epilogue(acc_ref, ex, outs):
        for c in range(tm // ch):
            rows = slice(c * ch, (c + 1) * ch)
            outs[0][rows, :] = scale_base * ex[0][rows, :] + acc_ref[rows, :]

    row = pl.BlockSpec((tm, d), lambda i, j, k: (i, 0))
    return _mm(
        name, (s // tm, 1, kdim // tk), dh, pl.BlockSpec((tm, tk), lambda i, j, k: (i, k)),
        w_in_st, pl.BlockSpec((None, d, tk), lambda i, j, k: (shard0 + k // per, 0, k % per)), NT,
        [base], [row], [jax.ShapeDtypeStruct((s, d), F32)], [row], epilogue, (tm, d),
        acc_as_ref=True, run_after=run_after)[0]


def _chip_peers():
    x, y, c = lax.axis_index("x"), lax.axis_index("y"), lax.axis_index("c")
    return x, y, c, [(1 - x, y), (x, 1 - y), (1 - x, 1 - y)]


def _exchange_descriptor(gather, src, land, send, recv, p, peer, me, c, arriving):
    px, py = peer
    pid = 2 * px + py
    if gather:
        src_ref, dst_ref = src, land.at[pid if arriving else me]
    else:
        src_ref, dst_ref = src.at[pid], land.at[p]
    return pltpu.make_async_remote_copy(
        src_ref=src_ref, dst_ref=dst_ref, send_sem=send.at[p], recv_sem=recv.at[p],
        device_id=(px, py, c), device_id_type=MESH)


def _exchange_start(name, gather, srcs, land_shapes):
    n = len(srcs)
    lands = [_hbm(lax.empty(shape, src.dtype)) for shape, src in zip(land_shapes, srcs)]

    def body(*refs):
        src_refs, land_refs = refs[:n], refs[n:2 * n]
        sends, recvs = refs[2 * n:3 * n], refs[3 * n:4 * n]
        token = refs[6 * n]
        x, y, c, peers = _chip_peers()
        me = 2 * x + y
        for w in range(n):
            for p, peer in enumerate(peers):
                _exchange_descriptor(gather, src_refs[w], land_refs[w], sends[w], recvs[w], p, peer,
                                     me, c, arriving=False).start()
        token[...] = jnp.zeros_like(token)

    sem = pltpu.SemaphoreType.DMA((3,))
    outs = pl.pallas_call(
        body, name=name, in_specs=[HBM_SPEC] * (2 * n),
        out_specs=[SEM_SPEC] * (2 * n) + [HBM_SPEC] * (2 * n) + [pl.BlockSpec(memory_space=pltpu.VMEM)],
        out_shape=[sem] * (2 * n) + [pltpu.HBM(a.shape, a.dtype) for a in (*srcs, *lands)]
        + [jax.ShapeDtypeStruct((8, 128), F32)],
        input_output_aliases={i: 2 * n + i for i in range(2 * n)},
        compiler_params=pltpu.CompilerParams(has_side_effects=DATAFLOW),
    )(*[_hbm(a) for a in srcs], *lands)
    return {"send": outs[:n], "recv": outs[n:2 * n], "src": outs[2 * n:3 * n],
            "land": outs[3 * n:4 * n], "token": outs[4 * n]}


def _exchange_wait(name, gather, started, which, after):
    m = len(which)

    def body(*refs):
        src_refs, land_refs = refs[:m], refs[m:2 * m]
        sends, recvs = refs[2 * m:3 * m], refs[3 * m:4 * m]
        x, y, c, peers = _chip_peers()
        me = 2 * x + y
        for w in range(m):
            for p, peer in enumerate(peers):
                _exchange_descriptor(gather, src_refs[w], land_refs[w], sends[w], recvs[w], p, peer,
                                     me, c, arriving=False).wait_send()
                _exchange_descriptor(gather, src_refs[w], land_refs[w], sends[w], recvs[w], p, peer,
                                     me, c, arriving=True).wait_recv()

    pick = lambda key: [started[key][w] for w in which]
    bufs = pick("src") + pick("land")
    outs = pl.pallas_call(
        body, name=name, in_specs=[HBM_SPEC] * (2 * m) + [SEM_SPEC] * (2 * m) + [ANY_SPEC],
        out_specs=[HBM_SPEC] * (2 * m), out_shape=[pltpu.HBM(a.shape, a.dtype) for a in bufs],
        input_output_aliases={i: i for i in range(2 * m)},
        compiler_params=pltpu.CompilerParams(has_side_effects=DATAFLOW),
    )(*bufs, *pick("send"), *pick("recv"), after)
    return outs[:m], outs[m:]


def _place_own(name, shard, land, me):
    r, c = shard.shape
    tm = min(r, 512)

    def body(me_ref, shard_ref, land_ref, out_ref):
        out_ref[...] = shard_ref[...]

    return pl.pallas_call(
        body, name=name,
        grid_spec=pltpu.PrefetchScalarGridSpec(
            num_scalar_prefetch=1, grid=(r // tm,),
            in_specs=[pl.BlockSpec((tm, c), lambda i, me_ref: (i, 0)), ANY_SPEC],
            out_specs=pl.BlockSpec((None, tm, c), lambda i, me_ref: (me_ref[0], i, 0))),
        out_shape=jax.ShapeDtypeStruct(land.shape, land.dtype), input_output_aliases={2: 0},
        compiler_params=_params(("arbitrary",)),
    )(me, shard, land)


def _sum_slabs(name, grads, land, me):
    _, r, c = grads.shape
    tm = min(r, 256)

    def body(me_ref, own_ref, land_ref, out_ref):
        acc = own_ref[...].astype(F32)
        for p in range(3):
            acc = acc + land_ref[p].astype(F32)
        out_ref[...] = acc

    return pl.pallas_call(
        body, name=name,
        grid_spec=pltpu.PrefetchScalarGridSpec(
            num_scalar_prefetch=1, grid=(r // tm,),
            in_specs=[pl.BlockSpec((None, tm, c), lambda i, me_ref: (me_ref[0], i, 0)),
                      pl.BlockSpec((3, tm, c), lambda i, me_ref: (0, i, 0))],
            out_specs=pl.BlockSpec((tm, c), lambda i, me_ref: (i, 0))),
        out_shape=jax.ShapeDtypeStruct((r, c), F32), compiler_params=_params(("parallel",)),
    )(me, grads, land)


def _exchange_cores(name, parts):
    n = len(parts)

    def body(*refs):
        ins, outs = refs[:n], refs[n:2 * n]
        send, recv = refs[2 * n:]
        x, y, c = lax.axis_index("x"), lax.axis_index("y"), lax.axis_index("c")
        copies = [pltpu.make_async_remote_copy(
            src_ref=ins[w], dst_ref=outs[w], send_sem=send.at[w], recv_sem=recv.at[w],
            device_id=(x, y, 1 - c), device_id_type=MESH) for w in range(n)]
        for cp in copies:
            cp.start()
        for cp in copies:
            cp.wait()

    return pl.pallas_call(
        body, name=name, in_specs=[ANY_SPEC] * n, out_specs=[ANY_SPEC] * n,
        out_shape=[jax.ShapeDtypeStruct(p.shape, p.dtype) for p in parts],
        scratch_shapes=[pltpu.SemaphoreType.DMA((n,)), pltpu.SemaphoreType.DMA((n,))],
    )(*parts)


def _allreduce_stats(stats):
    n = len(stats)

    def body(*refs):
        ins, outs = refs[:n], refs[n:2 * n]
        mine, gath = refs[2 * n:3 * n], refs[3 * n:4 * n]
        send, recv = refs[4 * n:]
        x, y, c = lax.axis_index("x"), lax.axis_index("y"), lax.axis_index("c")
        me = 4 * x + 2 * y + c
        flips = [(bx, by, bc) for bx in (0, 1) for by in (0, 1) for bc in (0, 1)][1:]

        def peer(f):
            return (x + f[0] * (1 - 2 * x), y + f[1] * (1 - 2 * y), c + f[2] * (1 - 2 * c))

        copies = []
        for t in range(n):
            tot = ins[t][0]
            for b in range(1, ins[t].shape[0]):
                tot = tot + ins[t][b]
            mine[t][...] = tot
            gath[t][me] = tot
            for k, f in enumerate(flips):
                cp = pltpu.make_async_remote_copy(
                    src_ref=mine[t], dst_ref=gath[t].at[me], send_sem=send.at[t, k],
                    recv_sem=recv.at[t, k], device_id=peer(f), device_id_type=MESH)
                cp.start()
                copies.append(cp)
        for t in range(n):
            for k, f in enumerate(flips):
                px, py, pc = peer(f)
                pltpu.make_async_remote_copy(
                    src_ref=mine[t], dst_ref=gath[t].at[4 * px + 2 * py + pc], send_sem=send.at[t, k],
                    recv_sem=recv.at[t, k], device_id=(px, py, pc), device_id_type=MESH).wait_recv()
        for cp in copies:
            cp.wait_send()
        for t in range(n):
            tot = gath[t][0]
            for dev in range(1, 8):
                tot = tot + gath[t][dev]
            outs[t][...] = tot

    vm = pl.BlockSpec(memory_space=pltpu.VMEM)
    return pl.pallas_call(
        body, name="allreduce_stats", in_specs=[vm] * n, out_specs=[vm] * n,
        out_shape=[jax.ShapeDtypeStruct(s.shape[1:], F32) for s in stats],
        scratch_shapes=[pltpu.VMEM(s.shape[1:], F32) for s in stats]
        + [pltpu.VMEM((8, *s.shape[1:]), F32) for s in stats]
        + [pltpu.SemaphoreType.DMA((n, 7)), pltpu.SemaphoreType.DMA((n, 7))],
    )(*stats)


def _adamw(name, w, m, v, g_parts):
    r, c = w.shape
    tm = min(r, 128)
    n_g = len(g_parts)

    def body(*refs):
        w_ref, m_ref, v_ref = refs[:3]
        g_refs = refs[3:3 + n_g]
        g_out, d_out, m_out, v_out = refs[3 + n_g:]
        g = g_refs[0][...]
        for gr in g_refs[1:]:
            g = g + gr[...]
        m_new = ADAM_B1 * m_ref[...] + (1.0 - ADAM_B1) * g
        v_new = ADAM_B2 * v_ref[...] + (1.0 - ADAM_B2) * (g * g)
        m_hat = m_new / (1.0 - ADAM_B1 ** ADAM_STEP)
        v_hat = v_new / (1.0 - ADAM_B2 ** ADAM_STEP)
        g_out[...] = g
        d_out[...] = -ADAM_LR * (m_hat / (jnp.sqrt(v_hat) + ADAM_EPS) + ADAM_WD * w_ref[...])
        m_out[...] = m_new
        v_out[...] = v_new

    blk = pl.BlockSpec((tm, c), lambda i: (i, 0))
    out = jax.ShapeDtypeStruct((r, c), F32)
    return pl.pallas_call(
        body, name=name, grid=(r // tm,), in_specs=[blk] * (3 + n_g), out_specs=[blk] * 4,
        out_shape=[out] * 4, compiler_params=_params(("parallel",)),
    )(w, m, v, *g_parts)


def _rope_tables(positions):
    half = HEAD_DIM // 2
    inv_freq = ROPE_THETA ** (-jnp.arange(half, dtype=F32) / half)
    ang = positions.astype(F32)[0, :, None] * inv_freq
    cos, sin = jnp.cos(ang), jnp.sin(ang)
    return jnp.concatenate([cos, cos], axis=-1), jnp.concatenate([-sin, sin], axis=-1)


def kernel(x, positions, w_in, w_pool, pool_scale, w_branch_attn, w_branch_pool, w_out, ln_mix_g, ln_mix_b, w_ff1, w_ff2, ln_ff_g, ln_ff_b, loss_target, m_w_in, m_w_pool, m_pool_scale, m_w_branch_attn, m_w_branch_pool, m_w_out, m_ln_mix_g, m_ln_mix_b, m_w_ff1, m_w_ff2, m_ln_ff_g, m_ln_ff_b, v_w_in, v_w_pool, v_pool_scale, v_w_branch_attn, v_w_branch_pool, v_w_out, v_ln_mix_g, v_ln_mix_b, v_w_ff1, v_w_ff2, v_ln_ff_g, v_ln_ff_b):
    s, d = x.shape[1], x.shape[2]
    aw = d // 2
    ng = len(POOL_WINDOWS)
    pgw = aw // ng
    x2d, target = x[0], loss_target[0]
    xb = x2d.astype(BF16)
    cos_t, sin_t = _rope_tables(positions)

    big = {"w_in": w_in[0], "w_pool": w_pool[0].reshape(-1, pgw), "w_branch_attn": w_branch_attn[0],
           "w_branch_pool": w_branch_pool[0], "w_out": w_out[0], "w_ff1": w_ff1[0], "w_ff2": w_ff2[0]}
    names = list(big)
    me_chip = (2 * lax.axis_index("x") + lax.axis_index("y")).astype(jnp.int32).reshape(1)
    shards = [big[k].astype(BF16) for k in names]
    gathering = _exchange_start("gather_start", True, shards, [(N_CHIPS, *a.shape) for a in shards])

    def gathered(name, which, after):
        srcs, lands = _exchange_wait(f"gather_wait_{name}", True, gathering, which, after)
        return [_place_own(f"place_own_{names[w]}", srcs[i], lands[i], me_chip) for i, w in enumerate(which)]

    rows_sh = pgw // N_CHIPS
    dff = N_CHIPS * big["w_ff2"].shape[0]

    w_in_st, wp_st = gathered("in", [0, 1], gathering["token"])
    wp = wp_st.reshape(N_CHIPS, ng, rows_sh, pgw).transpose(1, 0, 2, 3).reshape(ng, pgw, pgw)
    h = _in_proj(xb, w_in_st, cos_t, sin_t, aw)

    def to_view(a, dil):
        return a.reshape(s // dil, dil * a.shape[1])

    qkv = {1: (h, h, h)}
    offs = {1: (0, 1, 2)}
    for dil in DILATIONS[1:]:
        qkv[dil] = tuple(to_view(h[:, i * aw:(i + 1) * aw], dil) for i in range(3))
        offs[dil] = (0, 0, 0)
    o_parts, lse_parts = [], []
    for dil in DILATIONS:
        o_p, lse_p = _attn_fwd(f"attn_fwd_d{dil}", *qkv[dil], offs[dil], aw, dil * aw)
        o_parts.append(o_p.reshape(s, aw))
        lse_parts.append(lse_p.reshape(s, aw))
    o_attn, lse = _attn_combine(o_parts, lse_parts)
    p, pw, y = _pool_fwd(h, wp, pool_scale, aw)
    wba_st, wbp_st, w_out_st = gathered("mix", [2, 3, 4], y)
    w_out_full = w_out_st.reshape(d, d)
    ya, yp, merged = _branch_merge(o_attn, y, wba_st, wbp_st, h, aw, d)
    xhat1, rstd1, x1b = _mix_norm(merged, w_out_full, x2d, ln_mix_g, ln_mix_b)
    w1_st, w2_st = gathered("ff", [5, 6], x1b)
    w2_full = w2_st.reshape(dff, d)
    r = _ff_up(x1b, w1_st)
    dz2, dz2b, st2 = _ff_down_loss(r, w2_full, xhat1, ln_mix_g, ln_mix_b, ln_ff_g, ln_ff_b, target)

    def scatter_start(name, grads):
        return _exchange_start(f"scatter_start_{name}", False, grads, [(3, *g.shape[1:]) for g in grads])

    da = _ff_down_bwd(dz2b, w2_full, r)
    g_w2 = _wgrad("wgrad_ff2", r, dz2b, d).reshape(N_CHIPS, dff // N_CHIPS, d)
    g_w1 = _wgrad("wgrad_ff1", x1b, da, dff // N_CHIPS)
    sent_ff = scatter_start("ff", [g_w1, g_w2])
    dz1, dz1b, st1 = _ff_up_bwd(da, w1_st, dz2, xhat1, rstd1, ln_mix_g, [sent_ff["token"]])
    dya, dyp, dgate = _mix_bwd(dz1b, w_out_full, h, ya, yp, aw)
    g_wout = _wgrad("wgrad_out", merged, dz1b, d).reshape(N_CHIPS, d // N_CHIPS, d)
    g_wba = _wgrad("wgrad_branch_attn", o_attn, dya, d // N_CHIPS)
    g_wbp = _wgrad("wgrad_branch_pool", y, dyp, d // N_CHIPS)
    do, delta = _attn_out_bwd(dya, wba_st, o_attn)
    dpw, stp = _pool_out_bwd(dyp, wbp_st, pw, pool_scale)
    dwp, du = _pool_bwd(dpw, p, wp)
    g_wp = dwp.reshape(ng, N_CHIPS, rows_sh, pgw).transpose(1, 0, 2, 3).reshape(
        N_CHIPS, ng * rows_sh, pgw).astype(BF16)
    sent_mix = scatter_start("mix", [g_wp, g_wba, g_wbp, g_wout])

    dqs, dks, dvs = [], [], []
    for dil in DILATIONS:
        dov, lsev, dlv = (to_view(a, dil) for a in (do, lse, delta))
        args = (*qkv[dil], dov, lsev, dlv, offs[dil], aw, dil * aw)
        dqs.append(_attn_bwd_q(f"attn_bwd_q_d{dil}", *args).reshape(s, aw))
        dk_p, dv_p = _attn_bwd_kv(f"attn_bwd_kv_d{dil}", *args)
        dks.append(dk_p.reshape(s, aw))
        dvs.append(dv_p.reshape(s, aw))
    dqkvu = _qkvu_grad(dqs, dks, dvs, du, cos_t, sin_t)
    g_win_a = _wgrad("wgrad_in_qkvu", xb, dqkvu, d)
    g_win_b = _wgrad("wgrad_in_gates", xb, dgate, d)
    g_win = jnp.concatenate([g_win_a, g_win_b], axis=0)
    sent_in = scatter_start("in", [g_win])
    dx_a = _in_proj_bwd_x("in_proj_bwd_qkvu", dqkvu, w_in_st, 0, dz1, ALPHA,
                          [sent_mix["token"], sent_in["token"]])
    grad_x = _in_proj_bwd_x("in_proj_bwd_gates", dgate, w_in_st, 2, dx_a, 1.0)

    def reduced(name, sent, keys, after):
        srcs, lands = _exchange_wait(f"scatter_wait_{name}", False, sent, list(range(len(keys))), after)
        parts = [_sum_slabs(f"sum_slabs_{k}", srcs[i], lands[i], me_chip) for i, k in enumerate(keys)]
        other = _exchange_cores(f"exchange_cores_{name}", parts)
        return {k: [parts[i], other[i]] for i, k in enumerate(keys)}

    g_parts = {}
    g_parts.update(reduced("ff", sent_ff, ["w_ff1", "w_ff2"], dx_a))
    g_parts.update(reduced("mix", sent_mix, ["w_pool", "w_branch_attn", "w_branch_pool", "w_out"], dx_a))
    g_parts.update(reduced("in", sent_in, ["w_in"], grad_x))
    tot2, tot1, totp = _allreduce_stats([st2, st1, stp])

    moments = {"w_in": (m_w_in, v_w_in), "w_pool": (m_w_pool, v_w_pool),
               "w_branch_attn": (m_w_branch_attn, v_w_branch_attn),
               "w_branch_pool": (m_w_branch_pool, v_w_branch_pool), "w_out": (m_w_out, v_w_out),
               "w_ff1": (m_w_ff1, v_w_ff1), "w_ff2": (m_w_ff2, v_w_ff2)}
    originals = {"w_in": w_in, "w_pool": w_pool, "w_branch_attn": w_branch_attn,
                 "w_branch_pool": w_branch_pool, "w_out": w_out, "w_ff1": w_ff1, "w_ff2": w_ff2}
    res = {}
    for i, k in enumerate(names):
        shape2d = big[k].shape
        mk, vk = (a.reshape(shape2d) for a in moments[k])
        outs = _adamw(f"adamw_{k}", big[k], mk, vk, g_parts[k])
        res[k] = [o.reshape(originals[k].shape) for o in outs]

    def pad_d(a):
        return jnp.pad(a, ((0, 0), (0, d - a.shape[1])))

    small = ["ln_mix_g", "ln_mix_b", "ln_ff_g", "ln_ff_b", "pool_scale"]
    small_w = {"ln_mix_g": ln_mix_g, "ln_mix_b": ln_mix_b, "ln_ff_g": ln_ff_g, "ln_ff_b": ln_ff_b,
               "pool_scale": pool_scale}
    small_m = {"ln_mix_g": m_ln_mix_g, "ln_mix_b": m_ln_mix_b, "ln_ff_g": m_ln_ff_g,
               "ln_ff_b": m_ln_ff_b, "pool_scale": m_pool_scale}
    small_v = {"ln_mix_g": v_ln_mix_g, "ln_mix_b": v_ln_mix_b, "ln_ff_g": v_ln_ff_g,
               "ln_ff_b": v_ln_ff_b, "pool_scale": v_pool_scale}
    small_g = [tot1[0:1], tot1[1:2], tot2[0:1], tot2[1:2], pad_d(totp[0:1])]

    def pack(rows):
        return jnp.concatenate([pad_d(a) for a in rows] + [jnp.zeros((8 - len(rows), d), F32)], axis=0)

    outs = _adamw("adamw_small", pack([small_w[k] for k in small]), pack([small_m[k] for k in small]),
                  pack([small_v[k] for k in small]), [pack(small_g)])
    for i, k in enumerate(small):
        res[k] = [o[i:i + 1, :small_w[k].shape[1]] for o in outs]
    loss = tot2[2, 0]

    order = ["w_in", "w_pool", "pool_scale", "w_branch_attn", "w_branch_pool", "w_out", "ln_mix_g",
             "ln_mix_b", "w_ff1", "w_ff2", "ln_ff_g", "ln_ff_b"]
    result = [loss, grad_x[None]]
    for idx in range(4):
        result += [res[k][idx] for k in order]
    return tuple(result)
```

```python
import jax
import jax.numpy as jnp
from jax import lax
from jax.experimental import pallas as pl
from jax.experimental.pallas import tpu as pltpu

F32 = jnp.float32
BF16 = jnp.bfloat16
MESH = pl.DeviceIdType.MESH

HEAD_DIM = 128
SUB_BLOCK = 128
DILATIONS = (1, 4, 16)
POOL_WINDOWS = (2, 4, 8, 16)
POOL_HALO = 16
ROPE_THETA = 10000.0
LN_EPS = 1e-5
ALPHA = 2.0 ** 0.25
ADAM_LR, ADAM_B1, ADAM_B2, ADAM_EPS, ADAM_WD, ADAM_STEP = 0.001, 0.9, 0.999, 1e-08, 0.01, 10
NEG = -1e30
N_CHIPS = 4
VMEM_LIMIT = 56 * 1024 * 1024
EPILOGUE_ROWS = 128


def _params(sem=None, vmem=VMEM_LIMIT):
    kw = {"vmem_limit_bytes": vmem}
    if sem is not None:
        kw["dimension_semantics"] = sem
    return pltpu.CompilerParams(**kw)


def _dot(a, b, contract):
    return lax.dot_general(a, b, (contract, ((), ())), preferred_element_type=F32)


ANY_SPEC = pl.BlockSpec(memory_space=pl.ANY)
HBM_SPEC = pl.BlockSpec(memory_space=pltpu.HBM)
SEM_SPEC = pl.BlockSpec(memory_space=pltpu.SEMAPHORE)
DATAFLOW = pltpu.SideEffectType.DATAFLOW_SIDE_EFFECTING


def _hbm(a):
    return pltpu.with_memory_space_constraint(a, pltpu.HBM)


NN = ((1,), (0,))
NT = ((1,), (1,))
TN = ((0,), (0,))


def _mm(name, grid, a, a_spec, b, b_spec, contract, extras, extra_specs, out_shape, out_specs,
        epilogue, acc_shape, acc_as_ref=False, run_after=(), scratch=(),
        semantics=("parallel", "parallel", "arbitrary")):
    nk = grid[2]
    n_ex = len(extras)
    n_in = 2 + n_ex + len(run_after)
    n_out = len(out_shape)
    n_scr = len(scratch)

    def body(*refs):
        a_ref, b_ref = refs[0], refs[1]
        ex = refs[2:2 + n_ex]
        outs = refs[n_in:n_in + n_out]
        scr = refs[n_in + n_out:n_in + n_out + n_scr]
        if nk == 1:
            epilogue(_dot(a_ref[...], b_ref[...], contract), ex, outs, *scr)
        else:
            acc = refs[n_in + n_out + n_scr]
            k = pl.program_id(2)

            @pl.when(k == 0)
            def _():
                acc[...] = jnp.zeros_like(acc)

            acc[...] += _dot(a_ref[...], b_ref[...], contract)

            @pl.when(k == nk - 1)
            def _():
                epilogue(acc if acc_as_ref else acc[...], ex, outs, *scr)

    acc_scratch = [pltpu.VMEM(acc_shape, F32)] if nk > 1 else []
    return pl.pallas_call(
        body, name=name, grid=grid,
        in_specs=[a_spec, b_spec, *extra_specs, *[ANY_SPEC] * len(run_after)], out_specs=out_specs,
        out_shape=out_shape, scratch_shapes=[*scratch, *acc_scratch],
        compiler_params=_params(semantics),
    )(a, b, *extras, *run_after)


def _stats_rows(rows, width):
    idx = lax.broadcasted_iota(jnp.int32, (8, width), 0)
    out = jnp.zeros((8, width), F32)
    for r, v in enumerate(rows):
        out = jnp.where(idx == r, jnp.broadcast_to(v, (8, width)), out)
    return out


def _layer_norm_fwd(z):
    mu = jnp.mean(z, axis=-1, keepdims=True)
    zc = z - mu
    var = jnp.mean(zc * zc, axis=-1, keepdims=True)
    rstd = lax.rsqrt(var + LN_EPS)
    return zc * rstd, rstd


def _layer_norm_bwd(dy, xhat, rstd, g):
    dxh = dy * g
    m1 = jnp.mean(dxh, axis=-1, keepdims=True)
    m2 = jnp.mean(dxh * xhat, axis=-1, keepdims=True)
    return rstd * (dxh - m1 - xhat * m2)


def _heads_scratch(rows, width):
    return pltpu.VMEM((width // HEAD_DIM, rows, HEAD_DIM), F32)


def _to_views(src_ref, view_refs, dtype):
    nh, rows, _ = src_ref.shape
    width = nh * HEAD_DIM
    for dil, view_ref in zip(DILATIONS[1:], view_refs):
        for r in range(dil):
            for hh in range(nh):
                c0 = r * width + hh * HEAD_DIM
                view_ref[:, c0:c0 + HEAD_DIM] = (
                    src_ref[hh, pl.ds(r, rows // dil, stride=dil), :].astype(dtype))


def _from_views(view_refs, dst_refs):
    nh, rows, _ = dst_refs[0].shape
    width = nh * HEAD_DIM
    for dil, view_ref, dst_ref in zip(DILATIONS[1:], view_refs, dst_refs):
        for r in range(dil):
            for hh in range(nh):
                c0 = r * width + hh * HEAD_DIM
                dst_ref[hh, pl.ds(r, rows // dil, stride=dil), :] = (
                    view_ref[:, c0:c0 + HEAD_DIM].astype(F32))


def _view_shape(rows, width, dil, parts=1):
    return (rows // dil, parts * dil * width)


def _in_proj(xb, w_in_st, cos_t, sin_t, aw):
    s, d = xb.shape
    n_sh = w_in_st.shape[2]
    tm, tn = min(s, 1024), aw
    per = n_sh // tn
    grid = (s // tm, (N_CHIPS * n_sh) // tn, 1)

    def epilogue(acc, ex, outs, scr):
        cos_ref, sin_ref = ex
        h_ref, v4_ref, v16_ref = outs
        seg = pl.program_id(1)

        heads = [slice(hh * HEAD_DIM, (hh + 1) * HEAD_DIM) for hh in range(tn // HEAD_DIM)]

        @pl.when(seg < 2)
        def _():
            cos, sin = cos_ref[...], sin_ref[...]
            for hh, sl in enumerate(heads):
                t = acc[:, sl]
                scr[hh] = t * cos + pltpu.roll(t, HEAD_DIM // 2, 1) * sin

        @pl.when(seg == 2)
        def _():
            for hh, sl in enumerate(heads):
                scr[hh] = acc[:, sl]

        @pl.when(seg < 3)
        def _():
            for hh, sl in enumerate(heads):
                h_ref[:, sl] = scr[hh].astype(BF16)
            _to_views(scr, (v4_ref, v16_ref), BF16)

        @pl.when(seg == 3)
        def _():
            h_ref[...] = acc.astype(BF16)

        @pl.when(seg >= 4)
        def _():
            h_ref[...] = jax.nn.sigmoid(acc).astype(BF16)

    def view_spec(dil):
        return pl.BlockSpec((tm // dil, dil * aw), lambda i, j, k: (i, jnp.minimum(j, 2)))

    return _mm(
        "in_proj", grid, xb, pl.BlockSpec((tm, d), lambda i, j, k: (i, 0)),
        w_in_st, pl.BlockSpec((None, d, tn), lambda i, j, k: (j // per, 0, j % per)), NN,
        [cos_t, sin_t], [pl.BlockSpec((tm, HEAD_DIM), lambda i, j, k: (i, 0))] * 2,
        [jax.ShapeDtypeStruct((s, N_CHIPS * n_sh), BF16)]
        + [jax.ShapeDtypeStruct(_view_shape(s, aw, dil, 3), BF16) for dil in DILATIONS[1:]],
        [pl.BlockSpec((tm, tn), lambda i, j, k: (i, j))] + [view_spec(dil) for dil in DILATIONS[1:]],
        epilogue, None, scratch=[_heads_scratch(tm, aw)],
        semantics=("parallel", "arbitrary", "arbitrary"))


def _band_masks(block_idx):
    qi = lax.broadcasted_iota(jnp.int32, (SUB_BLOCK, 2 * SUB_BLOCK), 0)
    kj = lax.broadcasted_iota(jnp.int32, (SUB_BLOCK, 2 * SUB_BLOCK), 1)
    first_key = jnp.where(block_idx > 0, 0, SUB_BLOCK)
    return (kj >= qi) & (kj <= qi + SUB_BLOCK) & (kj >= first_key)


def _attn_fwd(name, q, k, v, offs, cw, width):
    m = q.shape[0]
    nh = cw // HEAD_DIM
    grid = (m // SUB_BLOCK, width // cw)
    scale = HEAD_DIM ** -0.5

    def body(q_ref, kp_ref, kc_ref, vp_ref, vc_ref, o_ref, lse_ref):
        valid = _band_masks(pl.program_id(0))
        for hh in range(nh):
            sl = slice(hh * HEAD_DIM, (hh + 1) * HEAD_DIM)
            kcat = jnp.concatenate([kp_ref[:, sl], kc_ref[:, sl]], axis=0)
            vcat = jnp.concatenate([vp_ref[:, sl], vc_ref[:, sl]], axis=0)
            sc = _dot(q_ref[:, sl], kcat, NT) * scale
            sc = jnp.where(valid, sc, NEG)
            mx = jnp.max(sc, axis=-1, keepdims=True)
            p = jnp.exp(sc - mx)
            l = jnp.sum(p, axis=-1, keepdims=True)
            o = _dot(p.astype(BF16), vcat, NN) / l
            o_ref[:, sl] = o.astype(BF16)
            lse_ref[:, sl] = jnp.broadcast_to(mx + jnp.log(l), (SUB_BLOCK, HEAD_DIM))

    def cur(off):
        return pl.BlockSpec((SUB_BLOCK, cw), lambda n, j: (n, j + off))

    def prev(off):
        return pl.BlockSpec((SUB_BLOCK, cw), lambda n, j: (jnp.maximum(n - 1, 0), j + off))

    return pl.pallas_call(
        body, name=name, grid=grid,
        in_specs=[cur(offs[0]), prev(offs[1]), cur(offs[1]), prev(offs[2]), cur(offs[2])],
        out_specs=[cur(0), cur(0)],
        out_shape=[jax.ShapeDtypeStruct((m, width), BF16), jax.ShapeDtypeStruct((m, width), F32)],
        compiler_params=_params(("parallel", "parallel")),
    )(q, k, k, v, v)


def _view_spec(tm, aw, dil):
    return pl.BlockSpec((tm // dil, dil * aw), lambda i: (i, 0))


def _attn_combine(o1, l1, o_views, l_views):
    s, aw = o1.shape
    tm = min(s, 512)

    def body(o1_ref, l1_ref, o4_ref, o16_ref, l4_ref, l16_ref, o_ref, lse_ref, lse4_ref, lse16_ref,
             so4, so16, sl4, sl16, stot):
        _from_views((o4_ref, o16_ref), (so4, so16))
        _from_views((l4_ref, l16_ref), (sl4, sl16))
        for hh in range(aw // HEAD_DIM):
            sl = slice(hh * HEAD_DIM, (hh + 1) * HEAD_DIM)
            a, b, c = l1_ref[:, sl], sl4[hh], sl16[hh]
            mx = jnp.maximum(jnp.maximum(a, b), c)
            ea, eb, ec = jnp.exp(a - mx), jnp.exp(b - mx), jnp.exp(c - mx)
            tot = ea + eb + ec
            o = (ea * o1_ref[:, sl].astype(F32) + eb * so4[hh] + ec * so16[hh]) / tot
            o_ref[:, sl] = o.astype(BF16)
            lse_tot = mx + jnp.log(tot)
            stot[hh] = lse_tot
            lse_ref[:, sl] = lse_tot
        _to_views(stot, (lse4_ref, lse16_ref), F32)

    row = pl.BlockSpec((tm, aw), lambda i: (i, 0))
    views = [_view_spec(tm, aw, dil) for dil in DILATIONS[1:]]
    return pl.pallas_call(
        body, name="attn_combine", grid=(s // tm,), in_specs=[row, row, *views, *views],
        out_specs=[row, row, *views],
        out_shape=[jax.ShapeDtypeStruct((s, aw), BF16), jax.ShapeDtypeStruct((s, aw), F32)]
        + [jax.ShapeDtypeStruct(_view_shape(s, aw, dil), F32) for dil in DILATIONS[1:]],
        scratch_shapes=[_heads_scratch(tm, aw)] * 5,
        compiler_params=_params(("parallel",)),
    )(o1, l1, *o_views, *l_views)


def _pool_counts(tm, rows, pgw, row0):
    t = lax.broadcasted_iota(jnp.int32, (rows, len(POOL_WINDOWS) * pgw), 0) + row0
    col = lax.broadcasted_iota(jnp.int32, (rows, len(POOL_WINDOWS) * pgw), 1)
    w = jnp.full((rows, len(POOL_WINDOWS) * pgw), POOL_WINDOWS[0], jnp.int32)
    for g in range(1, len(POOL_WINDOWS)):
        w = jnp.where(col >= g * pgw, POOL_WINDOWS[g], w)
    return jnp.minimum(t + 1, w).astype(F32)


def _window_sums(xs, direction, pgw):
    rows = xs.shape[0]
    acc = xs
    out = None
    col = lax.broadcasted_iota(jnp.int32, xs.shape, 1)
    for g, w in enumerate(POOL_WINDOWS):
        sh = w // 2
        acc = acc + pltpu.roll(acc, sh if direction > 0 else rows - sh, 0)
        out = acc if out is None else jnp.where(col >= g * pgw, acc, out)
    return out


def _pool_fwd(h, wp, scale, aw):
    s = h.shape[0]
    pw_ = aw
    pgw = pw_ // len(POOL_WINDOWS)
    tm = min(s, 512)
    hb = tm // POOL_HALO

    def body(u_ref, halo_ref, wp_ref, sc_ref, p_ref, pw_ref, y_ref):
        i = pl.program_id(0)
        u = u_ref[...].astype(F32)
        halo = halo_ref[...].astype(F32) * jnp.where(i > 0, 1.0, 0.0)
        xs = jnp.concatenate([halo, u], axis=0)
        sums = _window_sums(xs, +1, pgw)[POOL_HALO:]
        p = (sums / _pool_counts(tm, tm, pgw, i * tm) - u).astype(BF16)
        p_ref[...] = p
        sc = sc_ref[...]
        for g in range(len(POOL_WINDOWS)):
            sl = slice(g * pgw, (g + 1) * pgw)
            pw = _dot(p[:, sl], wp_ref[g], NN)
            pw_ref[:, sl] = pw.astype(BF16)
            y_ref[:, sl] = (pw * sc[:, sl]).astype(BF16)

    out = jax.ShapeDtypeStruct((s, pw_), BF16)
    row = pl.BlockSpec((tm, pw_), lambda i: (i, 0))
    return pl.pallas_call(
        body, name="pool_fwd", grid=(s // tm,),
        in_specs=[pl.BlockSpec((tm, pw_), lambda i: (i, 3)),
                  pl.BlockSpec((POOL_HALO, pw_), lambda i: (jnp.maximum(i * hb - 1, 0), 3)),
                  pl.BlockSpec(wp.shape, lambda i: (0, 0, 0)),
                  pl.BlockSpec((1, pw_), lambda i: (0, 0))],
        out_specs=[row, row, row], out_shape=[out, out, out],
        compiler_params=_params(("parallel",)),
    )(h, h, wp, scale)


def _branch_merge(o_attn, y, wba_st, wbp_st, h, aw, d):
    s = o_attn.shape[0]
    tn = wba_st.shape[2]
    tm = min(s, 1024)
    ga0 = 4 * aw // tn
    gp0 = (4 * aw + d) // tn

    def body(o_ref, y_ref, wa_ref, wp_ref, sga_ref, sgp_ref, ya_ref, yp_ref, mg_ref):
        ya = _dot(o_ref[...], wa_ref[...], NN)
        yp = _dot(y_ref[...], wp_ref[...], NN)
        ya_ref[...] = ya.astype(BF16)
        yp_ref[...] = yp.astype(BF16)
        mg_ref[...] = (sga_ref[...].astype(F32) * ya + sgp_ref[...].astype(F32) * yp).astype(BF16)

    out = jax.ShapeDtypeStruct((s, d), BF16)
    blk = pl.BlockSpec((tm, tn), lambda i, j: (i, j))
    return pl.pallas_call(
        body, name="branch_merge", grid=(s // tm, N_CHIPS),
        in_specs=[pl.BlockSpec((tm, aw), lambda i, j: (i, 0)),
                  pl.BlockSpec((tm, aw), lambda i, j: (i, 0)),
                  pl.BlockSpec((None, aw, tn), lambda i, j: (j, 0, 0)),
                  pl.BlockSpec((None, aw, tn), lambda i, j: (j, 0, 0)),
                  pl.BlockSpec((tm, tn), lambda i, j: (i, j + ga0)),
                  pl.BlockSpec((tm, tn), lambda i, j: (i, j + gp0))],
        out_specs=[blk, blk, blk], out_shape=[out, out, out],
        compiler_params=_params(("parallel", "parallel")),
    )(o_attn, y, wba_st, wbp_st, h, h)


def _mix_norm(merged, w_out, x, g1, b1):
    s, d = x.shape
    tm = min(s, 256)

    def epilogue(acc, ex, outs):
        x_ref, g_ref, b_ref = ex
        xh_ref, rs_ref, xb_ref = outs
        xhat, rstd = _layer_norm_fwd(ALPHA * x_ref[...] + acc)
        xh_ref[...] = xhat
        rs_ref[...] = rstd
        xb_ref[...] = (xhat * g_ref[...] + b_ref[...]).astype(BF16)

    row = pl.BlockSpec((tm, d), lambda i, j, k: (i, 0))
    vec = pl.BlockSpec((1, d), lambda i, j, k: (0, 0))
    return _mm(
        "mix_norm", (s // tm, 1, 1), merged, row, w_out, pl.BlockSpec((d, d), lambda i, j, k: (0, 0)),
        NN, [x, g1, b1], [row, vec, vec],
        [jax.ShapeDtypeStruct((s, d), F32), jax.ShapeDtypeStruct((s, 1), F32),
         jax.ShapeDtypeStruct((s, d), BF16)],
        [row, pl.BlockSpec((tm, 1), lambda i, j, k: (i, 0)), row], epilogue, None)


def _ff_up(x1b, w1_st):
    s, d = x1b.shape
    n_sh = w1_st.shape[2]
    tm, tn = min(s, 1024), min(n_sh, 1024)
    per = n_sh // tn

    def epilogue(acc, ex, outs):
        r = jnp.maximum(acc, 0.0)
        outs[0][...] = (r * r).astype(BF16)

    return _mm(
        "ff_up", (s // tm, N_CHIPS * per, 1), x1b, pl.BlockSpec((tm, d), lambda i, j, k: (i, 0)),
        w1_st, pl.BlockSpec((None, d, tn), lambda i, j, k: (j // per, 0, j % per)), NN, [], [],
        [jax.ShapeDtypeStruct((s, N_CHIPS * n_sh), BF16)],
        [pl.BlockSpec((tm, tn), lambda i, j, k: (i, j))], epilogue, None)[0]


def _ff_down_loss(r, w2, xhat1, g1, b1, g2, b2, target):
    s, d = xhat1.shape
    dff = r.shape[1]
    tm, tk = min(s, 512), min(dff, 1024)
    ch = min(tm, EPILOGUE_ROWS)

    def epilogue(acc_ref, ex, outs):
        xh1_ref, g1_ref, b1_ref, g2_ref, b2_ref, t_ref = ex
        dz_ref, dzb_ref, st_ref = outs
        g1v, b1v, g2v, b2v = g1_ref[...], b1_ref[...], g2_ref[...], b2_ref[...]
        dg = db = loss = None
        for c in range(tm // ch):
            rows = slice(c * ch, (c + 1) * ch)
            x1 = xh1_ref[rows, :] * g1v + b1v
            xhat2, rstd2 = _layer_norm_fwd(ALPHA * x1 + acc_ref[rows, :])
            err = xhat2 * g2v + b2v - t_ref[rows, :]
            dy = err * (1.0 / d)
            dz = _layer_norm_bwd(dy, xhat2, rstd2, g2v)
            dz_ref[rows, :] = dz
            dzb_ref[rows, :] = dz.astype(BF16)
            parts = (jnp.sum(dy * xhat2, axis=0, keepdims=True), jnp.sum(dy, axis=0, keepdims=True),
                     jnp.sum(jnp.sum(err * err, axis=-1, keepdims=True), axis=0, keepdims=True))
            dg, db, loss = parts if c == 0 else (dg + parts[0], db + parts[1], loss + parts[2])
        st_ref[...] = _stats_rows([dg, db, jnp.broadcast_to((0.5 / d) * loss, (1, d))], d)

    row = pl.BlockSpec((tm, d), lambda i, j, k: (i, 0))
    vec = pl.BlockSpec((1, d), lambda i, j, k: (0, 0))
    return _mm(
        "ff_down_loss", (s // tm, 1, dff // tk), r, pl.BlockSpec((tm, tk), lambda i, j, k: (i, k)),
        w2, pl.BlockSpec((tk, d), lambda i, j, k: (k, 0)), NN,
        [xhat1, g1, b1, g2, b2, target], [row, vec, vec, vec, vec, row],
        [jax.ShapeDtypeStruct((s, d), F32), jax.ShapeDtypeStruct((s, d), BF16),
         jax.ShapeDtypeStruct((s // tm, 8, d), F32)],
        [row, row, pl.BlockSpec((None, 8, d), lambda i, j, k: (i, 0, 0))], epilogue, (tm, d),
        acc_as_ref=True)


def _ff_down_bwd(dz2b, w2, r):
    s, d = dz2b.shape
    dff = r.shape[1]
    tm, tn = min(s, 1024), min(dff, 1024)

    def epilogue(acc, ex, outs):
        outs[0][...] = (acc * (2.0 * jnp.sqrt(ex[0][...].astype(F32)))).astype(BF16)

    blk = pl.BlockSpec((tm, tn), lambda i, j, k: (i, j))
    return _mm(
        "ff_down_bwd", (s // tm, dff // tn, 1), dz2b, pl.BlockSpec((tm, d), lambda i, j, k: (i, 0)),
        w2, pl.BlockSpec((tn, d), lambda i, j, k: (j, 0)), NT, [r], [blk],
        [jax.ShapeDtypeStruct((s, dff), BF16)], [blk], epilogue, None)[0]


def _wgrad(name, a, g, n_sh):
    s, rows = a.shape
    cols = g.shape[1]
    tm, tn, tk = min(rows, 2048), min(cols, 1024), min(s, 1024)
    if tn >= n_sh:
        span = tn // n_sh
        out_spec = pl.BlockSpec((span, tm, n_sh), lambda i, j, k: (j, i, 0))

        def epilogue(acc_ref, ex, outs):
            for sh in range(span):
                outs[0][sh] = acc_ref[:, sh * n_sh:(sh + 1) * n_sh].astype(BF16)
    else:
        per = n_sh // tn
        out_spec = pl.BlockSpec((None, tm, tn), lambda i, j, k: (j // per, i, j % per))

        def epilogue(acc_ref, ex, outs):
            outs[0][...] = acc_ref[...].astype(BF16)

    return _mm(
        name, (rows // tm, cols // tn, s // tk), a, pl.BlockSpec((tk, tm), lambda i, j, k: (k, i)),
        g, pl.BlockSpec((tk, tn), lambda i, j, k: (k, j)), TN, [], [],
        [jax.ShapeDtypeStruct((cols // n_sh, rows, n_sh), BF16)], [out_spec], epilogue,
        (tm, tn), acc_as_ref=True)[0]


def _ff_up_bwd(da, w1_st, dz2, xhat1, rstd1, g1, run_after):
    s, d = dz2.shape
    n_sh = w1_st.shape[2]
    tm, tk = min(s, 512), min(n_sh, 1024)
    per = n_sh // tk
    ch = min(tm, EPILOGUE_ROWS)

    def epilogue(acc_ref, ex, outs):
        dz2_ref, xh_ref, rs_ref, g_ref = ex
        dz_ref, dzb_ref, st_ref = outs
        gv = g_ref[...]
        dg = db = None
        for c in range(tm // ch):
            rows = slice(c * ch, (c + 1) * ch)
            dx1 = ALPHA * dz2_ref[rows, :] + acc_ref[rows, :]
            xhat = xh_ref[rows, :]
            dz = _layer_norm_bwd(dx1, xhat, rs_ref[rows, :], gv)
            dz_ref[rows, :] = dz
            dzb_ref[rows, :] = dz.astype(BF16)
            parts = (jnp.sum(dx1 * xhat, axis=0, keepdims=True), jnp.sum(dx1, axis=0, keepdims=True))
            dg, db = parts if c == 0 else (dg + parts[0], db + parts[1])
        st_ref[...] = _stats_rows([dg, db], d)

    row = pl.BlockSpec((tm, d), lambda i, j, k: (i, 0))
    return _mm(
        "ff_up_bwd", (s // tm, 1, N_CHIPS * per), da, pl.BlockSpec((tm, tk), lambda i, j, k: (i, k)),
        w1_st, pl.BlockSpec((None, d, tk), lambda i, j, k: (k // per, 0, k % per)), NT,
        [dz2, xhat1, rstd1, g1],
        [row, row, pl.BlockSpec((tm, 1), lambda i, j, k: (i, 0)), pl.BlockSpec((1, d), lambda i, j, k: (0, 0))],
        [jax.ShapeDtypeStruct((s, d), F32), jax.ShapeDtypeStruct((s, d), BF16),
         jax.ShapeDtypeStruct((s // tm, 8, d), F32)],
        [row, row, pl.BlockSpec((None, 8, d), lambda i, j, k: (i, 0, 0))], epilogue, (tm, d),
        acc_as_ref=True, run_after=run_after)


def _mix_bwd(dz1b, w_out, h, ya, yp, aw):
    s, d = dz1b.shape
    tm = min(s, 256)
    gblk = 4 * aw // d

    def epilogue(acc, ex, outs):
        sga_ref, sgp_ref, ya_ref, yp_ref = ex
        dya_ref, dyp_ref, dg_ref = outs
        sga, sgp = sga_ref[...].astype(F32), sgp_ref[...].astype(F32)
        dya_ref[...] = (acc * sga).astype(BF16)
        dyp_ref[...] = (acc * sgp).astype(BF16)
        dg_ref[:, :d] = (acc * ya_ref[...].astype(F32) * (sga * (1.0 - sga))).astype(BF16)
        dg_ref[:, d:] = (acc * yp_ref[...].astype(F32) * (sgp * (1.0 - sgp))).astype(BF16)

    row = pl.BlockSpec((tm, d), lambda i, j, k: (i, 0))
    return _mm(
        "mix_bwd", (s // tm, 1, 1), dz1b, row, w_out, pl.BlockSpec((d, d), lambda i, j, k: (0, 0)), NT,
        [h, h, ya, yp],
        [pl.BlockSpec((tm, d), lambda i, j, k: (i, gblk)),
         pl.BlockSpec((tm, d), lambda i, j, k: (i, gblk + 1)), row, row],
        [jax.ShapeDtypeStruct((s, d), BF16), jax.ShapeDtypeStruct((s, d), BF16),
         jax.ShapeDtypeStruct((s, 2 * d), BF16)],
        [row, row, pl.BlockSpec((tm, 2 * d), lambda i, j, k: (i, 0))], epilogue, None)


def _branch_in_bwd(name, tm, dyb, wb_st, epilogue, extras, extra_specs, out_shape, out_specs,
                   scratch=()):
    s = dyb.shape[0]
    aw, tk = wb_st.shape[1], wb_st.shape[2]
    return _mm(
        name, (s // tm, 1, N_CHIPS), dyb, pl.BlockSpec((tm, tk), lambda i, j, k: (i, k)),
        wb_st, pl.BlockSpec((None, aw, tk), lambda i, j, k: (k, 0, 0)), NT,
        extras, extra_specs, out_shape, out_specs, epilogue, (tm, aw), acc_as_ref=True,
        scratch=scratch)


def _attn_out_bwd(dya, wba_st, o_attn):
    s, aw = o_attn.shape
    tm = min(s, 512)

    def epilogue(acc_ref, ex, outs, sdo, sdl):
        do_ref, dl_ref, do4_ref, do16_ref, dl4_ref, dl16_ref = outs
        for hh in range(aw // HEAD_DIM):
            sl = slice(hh * HEAD_DIM, (hh + 1) * HEAD_DIM)
            do = acc_ref[:, sl]
            dl = jnp.broadcast_to(
                jnp.sum(do * ex[0][:, sl].astype(F32), axis=-1, keepdims=True), (tm, HEAD_DIM))
            sdo[hh] = do
            sdl[hh] = dl
            do_ref[:, sl] = do.astype(BF16)
            dl_ref[:, sl] = dl
        _to_views(sdo, (do4_ref, do16_ref), BF16)
        _to_views(sdl, (dl4_ref, dl16_ref), F32)

    row = pl.BlockSpec((tm, aw), lambda i, j, k: (i, 0))
    views = [pl.BlockSpec((tm // dil, dil * aw), lambda i, j, k: (i, 0)) for dil in DILATIONS[1:]]
    view_shapes = [_view_shape(s, aw, dil) for dil in DILATIONS[1:]]
    return _branch_in_bwd(
        "attn_out_bwd", tm, dya, wba_st, epilogue, [o_attn], [row],
        [jax.ShapeDtypeStruct((s, aw), BF16), jax.ShapeDtypeStruct((s, aw), F32)]
        + [jax.ShapeDtypeStruct(sh, BF16) for sh in view_shapes]
        + [jax.ShapeDtypeStruct(sh, F32) for sh in view_shapes],
        [row, row, *views, *views], scratch=[_heads_scratch(tm, aw)] * 2)


def _pool_out_bwd(dyp, wbp_st, pw, scale):
    s, pw_ = pw.shape
    tm = min(s, 1024)

    def epilogue(acc_ref, ex, outs):
        pw_ref, sc_ref = ex
        dpw_ref, st_ref = outs
        acc = acc_ref[...]
        dpw_ref[...] = (acc * sc_ref[...]).astype(BF16)
        st_ref[...] = _stats_rows([jnp.sum(acc * pw_ref[...].astype(F32), axis=0, keepdims=True)], pw_)

    row = pl.BlockSpec((tm, pw_), lambda i, j, k: (i, 0))
    return _branch_in_bwd(
        "pool_out_bwd", tm, dyp, wbp_st, epilogue, [pw, scale],
        [row, pl.BlockSpec((1, pw_), lambda i, j, k: (0, 0))],
        [jax.ShapeDtypeStruct((s, pw_), BF16), jax.ShapeDtypeStruct((s // tm, 8, pw_), F32)],
        [row, pl.BlockSpec((None, 8, pw_), lambda i, j, k: (i, 0, 0))])


def _pool_bwd(dpw, p, wp):
    s, pw_ = p.shape
    ng = len(POOL_WINDOWS)
    pgw = pw_ // ng
    tm = min(s, 512)
    hb = tm // POOL_HALO
    nblk = s // tm

    def body(dpw_ref, nxt_ref, p_ref, wp_ref, dwp_ref, du_ref):
        i = pl.program_id(0)
        nxt = (nxt_ref[...].astype(F32) * jnp.where(i < nblk - 1, 1.0, 0.0)).astype(BF16)
        dpw_all = jnp.concatenate([dpw_ref[...], nxt], axis=0)

        @pl.when(i == 0)
        def _():
            dwp_ref[...] = jnp.zeros_like(dwp_ref)

        dps = []
        for g in range(ng):
            sl = slice(g * pgw, (g + 1) * pgw)
            dwp_ref[g] += _dot(p_ref[:, sl], dpw_ref[:, sl], TN)
            dps.append(_dot(dpw_all[:, sl], wp_ref[g], NT))
        dp = jnp.concatenate(dps, axis=1)
        dpn = dp / _pool_counts(tm, tm + POOL_HALO, pgw, i * tm)
        du_ref[...] = (_window_sums(dpn, -1, pgw)[:tm] - dp[:tm]).astype(BF16)

    row = pl.BlockSpec((tm, pw_), lambda i: (i, 0))
    full = pl.BlockSpec((ng, pgw, pgw), lambda i: (0, 0, 0))
    return pl.pallas_call(
        body, name="pool_bwd", grid=(nblk,),
        in_specs=[row, pl.BlockSpec((POOL_HALO, pw_), lambda i: (jnp.minimum((i + 1) * hb, s // POOL_HALO - 1), 0)),
                  row, full],
        out_specs=[full, row],
        out_shape=[jax.ShapeDtypeStruct((ng, pgw, pgw), F32), jax.ShapeDtypeStruct((s, pw_), BF16)],
        compiler_params=_params(("arbitrary",)),
    )(dpw, dpw, p, wp)


def _attn_bwd_q(name, q, k, v, do, lse, delta, offs, cw, width):
    m = do.shape[0]
    nh = cw // HEAD_DIM
    grid = (m // SUB_BLOCK, width // cw)
    scale = HEAD_DIM ** -0.5

    def body(q_ref, kp_ref, kc_ref, vp_ref, vc_ref, do_ref, lse_ref, dl_ref, dq_ref):
        valid = _band_masks(pl.program_id(0))
        for hh in range(nh):
            sl = slice(hh * HEAD_DIM, (hh + 1) * HEAD_DIM)
            kcat = jnp.concatenate([kp_ref[:, sl], kc_ref[:, sl]], axis=0)
            vcat = jnp.concatenate([vp_ref[:, sl], vc_ref[:, sl]], axis=0)
            lse2 = jnp.concatenate([lse_ref[:, sl]] * 2, axis=1)
            dl2 = jnp.concatenate([dl_ref[:, sl]] * 2, axis=1)
            sc = _dot(q_ref[:, sl], kcat, NT) * scale
            p = jnp.where(valid, jnp.exp(jnp.where(valid, sc, NEG) - lse2), 0.0)
            dp = _dot(do_ref[:, sl], vcat, NT)
            ds = (p * (dp - dl2)).astype(BF16)
            dq_ref[:, sl] = (_dot(ds, kcat, NN) * scale).astype(BF16)

    def cur(off):
        return pl.BlockSpec((SUB_BLOCK, cw), lambda n, j: (n, j + off))

    def prev(off):
        return pl.BlockSpec((SUB_BLOCK, cw), lambda n, j: (jnp.maximum(n - 1, 0), j + off))

    return pl.pallas_call(
        body, name=name, grid=grid,
        in_specs=[cur(offs[0]), prev(offs[1]), cur(offs[1]), prev(offs[2]), cur(offs[2]),
                  cur(0), cur(0), cur(0)],
        out_specs=cur(0), out_shape=jax.ShapeDtypeStruct((m, width), BF16),
        compiler_params=_params(("parallel", "parallel")),
    )(q, k, k, v, v, do, lse, delta)


def _attn_bwd_kv(name, q, k, v, do, lse, delta, offs, cw, width):
    m = do.shape[0]
    nh = cw // HEAD_DIM
    nblk = m // SUB_BLOCK
    grid = (nblk, width // cw)
    scale = HEAD_DIM ** -0.5

    def body(qc_ref, qn_ref, k_ref, v_ref, doc_ref, don_ref, lsec_ref, lsen_ref, dlc_ref, dln_ref,
             dk_ref, dv_ref):
        no_next = jnp.where(pl.program_id(0) < nblk - 1, 0, 4 * SUB_BLOCK)
        r = lax.broadcasted_iota(jnp.int32, (2 * SUB_BLOCK, SUB_BLOCK), 0)
        kj = lax.broadcasted_iota(jnp.int32, (2 * SUB_BLOCK, SUB_BLOCK), 1)
        valid = ((r < SUB_BLOCK) & (kj <= r)) | ((r >= SUB_BLOCK) & (kj >= r - SUB_BLOCK + no_next))
        for hh in range(nh):
            sl = slice(hh * HEAD_DIM, (hh + 1) * HEAD_DIM)
            qcat = jnp.concatenate([qc_ref[:, sl], qn_ref[:, sl]], axis=0)
            docat = jnp.concatenate([doc_ref[:, sl], don_ref[:, sl]], axis=0)
            lse2 = jnp.concatenate([lsec_ref[:, sl], lsen_ref[:, sl]], axis=0)
            dl2 = jnp.concatenate([dlc_ref[:, sl], dln_ref[:, sl]], axis=0)
            sc = _dot(qcat, k_ref[:, sl], NT) * scale
            p = jnp.where(valid, jnp.exp(jnp.where(valid, sc, NEG) - lse2), 0.0)
            dp = _dot(docat, v_ref[:, sl], NT)
            ds = (p * (dp - dl2)).astype(BF16)
            dv_ref[:, sl] = _dot(p.astype(BF16), docat, TN).astype(BF16)
            dk_ref[:, sl] = (_dot(ds, qcat, TN) * scale).astype(BF16)

    def cur(off):
        return pl.BlockSpec((SUB_BLOCK, cw), lambda n, j: (n, j + off))

    def nxt(off):
        return pl.BlockSpec((SUB_BLOCK, cw), lambda n, j: (jnp.minimum(n + 1, nblk - 1), j + off))

    out = jax.ShapeDtypeStruct((m, width), BF16)
    return pl.pallas_call(
        body, name=name, grid=grid,
        in_specs=[cur(offs[0]), nxt(offs[0]), cur(offs[1]), cur(offs[2]),
                  cur(0), nxt(0), cur(0), nxt(0), cur(0), nxt(0)],
        out_specs=[cur(0), cur(0)], out_shape=[out, out],
        compiler_params=_params(("parallel", "parallel")),
    )(q, q, k, v, do, do, lse, lse, delta, delta)


def _qkvu_grad(d1, d4, d16, du, cos_t, sin_t):
    s, aw = du.shape
    tm = min(s, 512)

    def body(*refs):
        nat, v4, v16 = refs[0:3], refs[3:6], refs[6:9]
        cos_ref, sin_ref, du_ref, out_ref, s4, s16 = refs[9:]
        cos, sin = cos_ref[...], sin_ref[...]
        for part in range(3):
            _from_views((v4[part], v16[part]), (s4, s16))
            for hh in range(aw // HEAD_DIM):
                sl = slice(hh * HEAD_DIM, (hh + 1) * HEAD_DIM)
                t = nat[part][:, sl].astype(F32) + s4[hh] + s16[hh]
                if part < 2:
                    t = t * cos - pltpu.roll(t, HEAD_DIM // 2, 1) * sin
                out_ref[:, part * aw + hh * HEAD_DIM:part * aw + (hh + 1) * HEAD_DIM] = t.astype(BF16)
        out_ref[:, 3 * aw:] = du_ref[...]

    row = pl.BlockSpec((tm, aw), lambda i: (i, 0))
    tab = pl.BlockSpec((tm, HEAD_DIM), lambda i: (i, 0))
    return pl.pallas_call(
        body, name="qkvu_grad", grid=(s // tm,),
        in_specs=[row] * 3 + [_view_spec(tm, aw, 4)] * 3 + [_view_spec(tm, aw, 16)] * 3 + [tab, tab, row],
        out_specs=pl.BlockSpec((tm, 4 * aw), lambda i: (i, 0)),
        out_shape=jax.ShapeDtypeStruct((s, 4 * aw), BF16),
        scratch_shapes=[_heads_scratch(tm, aw)] * 2,
        compiler_params=_params(("parallel",)),
    )(*d1, *d4, *d16, cos_t, sin_t, du)


def _in_proj_bwd_x(name, dh, w_in_st, shard0, base, scale_base, run_after=()):
    s, kdim = dh.shape
    d, n_sh = w_in_st.shape[1], w_in_st.shape[2]
    tm, tk = min(s, 512), min(n_sh, 2048)
    per = n_sh // tk

    ch = min(tm, 2 * EPILOGUE_ROWS)

    def epilogue(acc_ref, ex, outs):
        for c in range(tm // ch):
            rows = slice(c * ch, (c + 1) * ch)
            outs[0][rows, :] = scale_base * ex[0][rows, :] + acc_ref[rows, :]

    row = pl.BlockSpec((tm, d), lambda i, j, k: (i, 0))
    return _mm(
        name, (s // tm, 1, kdim // tk), dh, pl.BlockSpec((tm, tk), lambda i, j, k: (i, k)),
        w_in_st, pl.BlockSpec((None, d, tk), lambda i, j, k: (shard0 + k // per, 0, k % per)), NT,
        [base], [row], [jax.ShapeDtypeStruct((s, d), F32)], [row], epilogue, (tm, d),
        acc_as_ref=True, run_after=run_after)[0]


def _chip_peers():
    x, y, c = lax.axis_index("x"), lax.axis_index("y"), lax.axis_index("c")
    return x, y, c, [(1 - x, y), (x, 1 - y), (1 - x, 1 - y)]


def _exchange_descriptor(gather, src, land, send, recv, p, peer, me, c, arriving):
    px, py = peer
    pid = 2 * px + py
    if gather:
        src_ref, dst_ref = src, land.at[pid if arriving else me]
    else:
        src_ref, dst_ref = src.at[pid], land.at[p]
    return pltpu.make_async_remote_copy(
        src_ref=src_ref, dst_ref=dst_ref, send_sem=send.at[p], recv_sem=recv.at[p],
        device_id=(px, py, c), device_id_type=MESH)


def _exchange_start(name, gather, srcs, land_shapes):
    n = len(srcs)
    lands = [_hbm(lax.empty(shape, src.dtype)) for shape, src in zip(land_shapes, srcs)]

    def body(*refs):
        src_refs, land_refs = refs[:n], refs[n:2 * n]
        sends, recvs = refs[2 * n:3 * n], refs[3 * n:4 * n]
        token = refs[6 * n]
        x, y, c, peers = _chip_peers()
        me = 2 * x + y
        for w in range(n):
            for p, peer in enumerate(peers):
                _exchange_descriptor(gather, src_refs[w], land_refs[w], sends[w], recvs[w], p, peer,
                                     me, c, arriving=False).start()
        token[...] = jnp.zeros_like(token)

    sem = pltpu.SemaphoreType.DMA((3,))
    outs = pl.pallas_call(
        body, name=name, in_specs=[HBM_SPEC] * (2 * n),
        out_specs=[SEM_SPEC] * (2 * n) + [HBM_SPEC] * (2 * n) + [pl.BlockSpec(memory_space=pltpu.VMEM)],
        out_shape=[sem] * (2 * n) + [pltpu.HBM(a.shape, a.dtype) for a in (*srcs, *lands)]
        + [jax.ShapeDtypeStruct((8, 128), F32)],
        input_output_aliases={i: 2 * n + i for i in range(2 * n)},
        compiler_params=pltpu.CompilerParams(has_side_effects=DATAFLOW),
    )(*[_hbm(a) for a in srcs], *lands)
    return {"send": outs[:n], "recv": outs[n:2 * n], "src": outs[2 * n:3 * n],
            "land": outs[3 * n:4 * n], "token": outs[4 * n]}


def _exchange_wait(name, gather, started, which, after):
    m = len(which)

    def body(*refs):
        src_refs, land_refs = refs[:m], refs[m:2 * m]
        sends, recvs = refs[2 * m:3 * m], refs[3 * m:4 * m]
        x, y, c, peers = _chip_peers()
        me = 2 * x + y
        for w in range(m):
            for p, peer in enumerate(peers):
                _exchange_descriptor(gather, src_refs[w], land_refs[w], sends[w], recvs[w], p, peer,
                                     me, c, arriving=False).wait_send()
                _exchange_descriptor(gather, src_refs[w], land_refs[w], sends[w], recvs[w], p, peer,
                                     me, c, arriving=True).wait_recv()

    pick = lambda key: [started[key][w] for w in which]
    bufs = pick("src") + pick("land")
    outs = pl.pallas_call(
        body, name=name, in_specs=[HBM_SPEC] * (2 * m) + [SEM_SPEC] * (2 * m) + [ANY_SPEC],
        out_specs=[HBM_SPEC] * (2 * m), out_shape=[pltpu.HBM(a.shape, a.dtype) for a in bufs],
        input_output_aliases={i: i for i in range(2 * m)},
        compiler_params=pltpu.CompilerParams(has_side_effects=DATAFLOW),
    )(*bufs, *pick("send"), *pick("recv"), after)
    return outs[:m], outs[m:]


def _place_own(name, shard, land, me):
    r, c = shard.shape
    tm = min(r, 512)

    def body(me_ref, shard_ref, land_ref, out_ref):
        out_ref[...] = shard_ref[...]

    return pl.pallas_call(
        body, name=name,
        grid_spec=pltpu.PrefetchScalarGridSpec(
            num_scalar_prefetch=1, grid=(r // tm,),
            in_specs=[pl.BlockSpec((tm, c), lambda i, me_ref: (i, 0)), ANY_SPEC],
            out_specs=pl.BlockSpec((None, tm, c), lambda i, me_ref: (me_ref[0], i, 0))),
        out_shape=jax.ShapeDtypeStruct(land.shape, land.dtype), input_output_aliases={2: 0},
        compiler_params=_params(("arbitrary",)),
    )(me, shard, land)


def _sum_slabs(name, grads, land, me):
    _, r, c = grads.shape
    tm = min(r, 256)

    def body(me_ref, own_ref, land_ref, out_ref):
        acc = own_ref[...].astype(F32)
        for p in range(3):
            acc = acc + land_ref[p].astype(F32)
        out_ref[...] = acc

    return pl.pallas_call(
        body, name=name,
        grid_spec=pltpu.PrefetchScalarGridSpec(
            num_scalar_prefetch=1, grid=(r // tm,),
            in_specs=[pl.BlockSpec((None, tm, c), lambda i, me_ref: (me_ref[0], i, 0)),
                      pl.BlockSpec((3, tm, c), lambda i, me_ref: (0, i, 0))],
            out_specs=pl.BlockSpec((tm, c), lambda i, me_ref: (i, 0))),
        out_shape=jax.ShapeDtypeStruct((r, c), F32), compiler_params=_params(("parallel",)),
    )(me, grads, land)


def _exchange_cores(name, parts):
    n = len(parts)

    def body(*refs):
        ins, outs = refs[:n], refs[n:2 * n]
        send, recv = refs[2 * n:]
        x, y, c = lax.axis_index("x"), lax.axis_index("y"), lax.axis_index("c")
        copies = [pltpu.make_async_remote_copy(
            src_ref=ins[w], dst_ref=outs[w], send_sem=send.at[w], recv_sem=recv.at[w],
            device_id=(x, y, 1 - c), device_id_type=MESH) for w in range(n)]
        for cp in copies:
            cp.start()
        for cp in copies:
            cp.wait()

    return pl.pallas_call(
        body, name=name, in_specs=[ANY_SPEC] * n, out_specs=[ANY_SPEC] * n,
        out_shape=[jax.ShapeDtypeStruct(p.shape, p.dtype) for p in parts],
        scratch_shapes=[pltpu.SemaphoreType.DMA((n,)), pltpu.SemaphoreType.DMA((n,))],
    )(*parts)


def _allreduce_stats(stats):
    n = len(stats)

    def body(*refs):
        ins, outs = refs[:n], refs[n:2 * n]
        mine, gath = refs[2 * n:3 * n], refs[3 * n:4 * n]
        send, recv = refs[4 * n:]
        x, y, c = lax.axis_index("x"), lax.axis_index("y"), lax.axis_index("c")
        me = 4 * x + 2 * y + c
        flips = [(bx, by, bc) for bx in (0, 1) for by in (0, 1) for bc in (0, 1)][1:]

        def peer(f):
            return (x + f[0] * (1 - 2 * x), y + f[1] * (1 - 2 * y), c + f[2] * (1 - 2 * c))

        copies = []
        for t in range(n):
            tot = ins[t][0]
            for b in range(1, ins[t].shape[0]):
                tot = tot + ins[t][b]
            mine[t][...] = tot
            gath[t][me] = tot
            for k, f in enumerate(flips):
                cp = pltpu.make_async_remote_copy(
                    src_ref=mine[t], dst_ref=gath[t].at[me], send_sem=send.at[t, k],
                    recv_sem=recv.at[t, k], device_id=peer(f), device_id_type=MESH)
                cp.start()
                copies.append(cp)
        for t in range(n):
            for k, f in enumerate(flips):
                px, py, pc = peer(f)
                pltpu.make_async_remote_copy(
                    src_ref=mine[t], dst_ref=gath[t].at[4 * px + 2 * py + pc], send_sem=send.at[t, k],
                    recv_sem=recv.at[t, k], device_id=(px, py, pc), device_id_type=MESH).wait_recv()
        for cp in copies:
            cp.wait_send()
        for t in range(n):
            tot = gath[t][0]
            for dev in range(1, 8):
                tot = tot + gath[t][dev]
            outs[t][...] = tot

    vm = pl.BlockSpec(memory_space=pltpu.VMEM)
    return pl.pallas_call(
        body, name="allreduce_stats", in_specs=[vm] * n, out_specs=[vm] * n,
        out_shape=[jax.ShapeDtypeStruct(s.shape[1:], F32) for s in stats],
        scratch_shapes=[pltpu.VMEM(s.shape[1:], F32) for s in stats]
        + [pltpu.VMEM((8, *s.shape[1:]), F32) for s in stats]
        + [pltpu.SemaphoreType.DMA((n, 7)), pltpu.SemaphoreType.DMA((n, 7))],
    )(*stats)


def _adamw(name, w, m, v, g_parts):
    r, c = w.shape
    tm = min(r, 128)
    n_g = len(g_parts)

    def body(*refs):
        w_ref, m_ref, v_ref = refs[:3]
        g_refs = refs[3:3 + n_g]
        g_out, d_out, m_out, v_out = refs[3 + n_g:]
        g = g_refs[0][...]
        for gr in g_refs[1:]:
            g = g + gr[...]
        m_new = ADAM_B1 * m_ref[...] + (1.0 - ADAM_B1) * g
        v_new = ADAM_B2 * v_ref[...] + (1.0 - ADAM_B2) * (g * g)
        m_hat = m_new / (1.0 - ADAM_B1 ** ADAM_STEP)
        v_hat = v_new / (1.0 - ADAM_B2 ** ADAM_STEP)
        g_out[...] = g
        d_out[...] = -ADAM_LR * (m_hat / (jnp.sqrt(v_hat) + ADAM_EPS) + ADAM_WD * w_ref[...])
        m_out[...] = m_new
        v_out[...] = v_new

    blk = pl.BlockSpec((tm, c), lambda i: (i, 0))
    out = jax.ShapeDtypeStruct((r, c), F32)
    return pl.pallas_call(
        body, name=name, grid=(r // tm,), in_specs=[blk] * (3 + n_g), out_specs=[blk] * 4,
        out_shape=[out] * 4, compiler_params=_params(("parallel",)),
    )(w, m, v, *g_parts)


def _rope_tables(positions):
    half = HEAD_DIM // 2
    inv_freq = ROPE_THETA ** (-jnp.arange(half, dtype=F32) / half)
    ang = positions.astype(F32)[0, :, None] * inv_freq
    cos, sin = jnp.cos(ang), jnp.sin(ang)
    return jnp.concatenate([cos, cos], axis=-1), jnp.concatenate([-sin, sin], axis=-1)


def kernel(x, positions, w_in, w_pool, pool_scale, w_branch_attn, w_branch_pool, w_out, ln_mix_g, ln_mix_b, w_ff1, w_ff2, ln_ff_g, ln_ff_b, loss_target, m_w_in, m_w_pool, m_pool_scale, m_w_branch_attn, m_w_branch_pool, m_w_out, m_ln_mix_g, m_ln_mix_b, m_w_ff1, m_w_ff2, m_ln_ff_g, m_ln_ff_b, v_w_in, v_w_pool, v_pool_scale, v_w_branch_attn, v_w_branch_pool, v_w_out, v_ln_mix_g, v_ln_mix_b, v_w_ff1, v_w_ff2, v_ln_ff_g, v_ln_ff_b):
    s, d = x.shape[1], x.shape[2]
    aw = d // 2
    ng = len(POOL_WINDOWS)
    pgw = aw // ng
    x2d, target = x[0], loss_target[0]
    xb = x2d.astype(BF16)
    cos_t, sin_t = _rope_tables(positions)

    big = {"w_in": w_in[0], "w_pool": w_pool[0].reshape(-1, pgw), "w_branch_attn": w_branch_attn[0],
           "w_branch_pool": w_branch_pool[0], "w_out": w_out[0], "w_ff1": w_ff1[0], "w_ff2": w_ff2[0]}
    names = list(big)
    me_chip = (2 * lax.axis_index("x") + lax.axis_index("y")).astype(jnp.int32).reshape(1)
    shards = [big[k].astype(BF16) for k in names]
    gathering = _exchange_start("gather_start", True, shards, [(N_CHIPS, *a.shape) for a in shards])

    def gathered(name, which, after):
        srcs, lands = _exchange_wait(f"gather_wait_{name}", True, gathering, which, after)
        return [_place_own(f"place_own_{names[w]}", srcs[i], lands[i], me_chip) for i, w in enumerate(which)]

    rows_sh = pgw // N_CHIPS
    dff = N_CHIPS * big["w_ff2"].shape[0]

    w_in_st, wp_st = gathered("in", [0, 1], gathering["token"])
    wp = wp_st.reshape(N_CHIPS, ng, rows_sh, pgw).transpose(1, 0, 2, 3).reshape(ng, pgw, pgw)
    h, hv4, hv16 = _in_proj(xb, w_in_st, cos_t, sin_t, aw)
    qkv = {1: (h, h, h), 4: (hv4, hv4, hv4), 16: (hv16, hv16, hv16)}
    offs = {dil: (0, dil, 2 * dil) for dil in DILATIONS}
    o_parts, lse_parts = [], []
    for dil in DILATIONS:
        o_p, lse_p = _attn_fwd(f"attn_fwd_d{dil}", *qkv[dil], offs[dil], aw, dil * aw)
        o_parts.append(o_p)
        lse_parts.append(lse_p)
    o_attn, lse, lse4, lse16 = _attn_combine(o_parts[0], lse_parts[0], o_parts[1:], lse_parts[1:])
    p, pw, y = _pool_fwd(h, wp, pool_scale, aw)
    wba_st, wbp_st, w_out_st = gathered("mix", [2, 3, 4], y)
    w_out_full = w_out_st.reshape(d, d)
    ya, yp, merged = _branch_merge(o_attn, y, wba_st, wbp_st, h, aw, d)
    xhat1, rstd1, x1b = _mix_norm(merged, w_out_full, x2d, ln_mix_g, ln_mix_b)
    w1_st, w2_st = gathered("ff", [5, 6], x1b)
    w2_full = w2_st.reshape(dff, d)
    r = _ff_up(x1b, w1_st)
    dz2, dz2b, st2 = _ff_down_loss(r, w2_full, xhat1, ln_mix_g, ln_mix_b, ln_ff_g, ln_ff_b, target)

    def scatter_start(name, grads):
        return _exchange_start(f"scatter_start_{name}", False, grads, [(3, *g.shape[1:]) for g in grads])

    da = _ff_down_bwd(dz2b, w2_full, r)
    g_w2 = _wgrad("wgrad_ff2", r, dz2b, d).reshape(N_CHIPS, dff // N_CHIPS, d)
    g_w1 = _wgrad("wgrad_ff1", x1b, da, dff // N_CHIPS)
    sent_ff = scatter_start("ff", [g_w1, g_w2])
    dz1, dz1b, st1 = _ff_up_bwd(da, w1_st, dz2, xhat1, rstd1, ln_mix_g, [sent_ff["token"]])
    dya, dyp, dgate = _mix_bwd(dz1b, w_out_full, h, ya, yp, aw)
    g_wout = _wgrad("wgrad_out", merged, dz1b, d).reshape(N_CHIPS, d // N_CHIPS, d)
    g_wba = _wgrad("wgrad_branch_attn", o_attn, dya, d // N_CHIPS)
    g_wbp = _wgrad("wgrad_branch_pool", y, dyp, d // N_CHIPS)
    do, delta, do4, do16, delta4, delta16 = _attn_out_bwd(dya, wba_st, o_attn)
    dpw, stp = _pool_out_bwd(dyp, wbp_st, pw, pool_scale)
    dwp, du = _pool_bwd(dpw, p, wp)
    g_wp = dwp.reshape(ng, N_CHIPS, rows_sh, pgw).transpose(1, 0, 2, 3).reshape(
        N_CHIPS, ng * rows_sh, pgw).astype(BF16)
    sent_mix = scatter_start("mix", [g_wp, g_wba, g_wbp, g_wout])

    bwd_in = {1: (do, lse, delta), 4: (do4, lse4, delta4), 16: (do16, lse16, delta16)}
    dqkv = {}
    for dil in DILATIONS:
        args = (*qkv[dil], *bwd_in[dil], offs[dil], aw, dil * aw)
        dq_p = _attn_bwd_q(f"attn_bwd_q_d{dil}", *args)
        dqkv[dil] = (dq_p, *_attn_bwd_kv(f"attn_bwd_kv_d{dil}", *args))
    dqkvu = _qkvu_grad(dqkv[1], dqkv[4], dqkv[16], du, cos_t, sin_t)
    g_win_a = _wgrad("wgrad_in_qkvu", xb, dqkvu, d)
    g_win_b = _wgrad("wgrad_in_gates", xb, dgate, d)
    g_win = jnp.concatenate([g_win_a, g_win_b], axis=0)
    sent_in = scatter_start("in", [g_win])
    dx_a = _in_proj_bwd_x("in_proj_bwd_qkvu", dqkvu, w_in_st, 0, dz1, ALPHA,
                          [sent_mix["token"], sent_in["token"]])
    grad_x = _in_proj_bwd_x("in_proj_bwd_gates", dgate, w_in_st, 2, dx_a, 1.0)

    def reduced(name, sent, keys, after):
        srcs, lands = _exchange_wait(f"scatter_wait_{name}", False, sent, list(range(len(keys))), after)
        parts = [_sum_slabs(f"sum_slabs_{k}", srcs[i], lands[i], me_chip) for i, k in enumerate(keys)]
        other = _exchange_cores(f"exchange_cores_{name}", parts)
        return {k: [parts[i], other[i]] for i, k in enumerate(keys)}

    g_parts = {}
    g_parts.update(reduced("ff", sent_ff, ["w_ff1", "w_ff2"], dx_a))
    g_parts.update(reduced("mix", sent_mix, ["w_pool", "w_branch_attn", "w_branch_pool", "w_out"], dx_a))
    g_parts.update(reduced("in", sent_in, ["w_in"], grad_x))
    tot2, tot1, totp = _allreduce_stats([st2, st1, stp])

    moments = {"w_in": (m_w_in, v_w_in), "w_pool": (m_w_pool, v_w_pool),
               "w_branch_attn": (m_w_branch_attn, v_w_branch_attn),
               "w_branch_pool": (m_w_branch_pool, v_w_branch_pool), "w_out": (m_w_out, v_w_out),
               "w_ff1": (m_w_ff1, v_w_ff1), "w_ff2": (m_w_ff2, v_w_ff2)}
    originals = {"w_in": w_in, "w_pool": w_pool, "w_branch_attn": w_branch_attn,
                 "w_branch_pool": w_branch_pool, "w_out": w_out, "w_ff1": w_ff1, "w_ff2": w_ff2}
    res = {}
    for i, k in enumerate(names):
        shape2d = big[k].shape
        mk, vk = (a.reshape(shape2d) for a in moments[k])
        outs = _adamw(f"adamw_{k}", big[k], mk, vk, g_parts[k])
        res[k] = [o.reshape(originals[k].shape) for o in outs]

    def pad_d(a):
        return jnp.pad(a, ((0, 0), (0, d - a.shape[1])))

    small = ["ln_mix_g", "ln_mix_b", "ln_ff_g", "ln_ff_b", "pool_scale"]
    small_w = {"ln_mix_g": ln_mix_g, "ln_mix_b": ln_mix_b, "ln_ff_g": ln_ff_g, "ln_ff_b": ln_ff_b,
               "pool_scale": pool_scale}
    small_m = {"ln_mix_g": m_ln_mix_g, "ln_mix_b": m_ln_mix_b, "ln_ff_g": m_ln_ff_g,
               "ln_ff_b": m_ln_ff_b, "pool_scale": m_pool_scale}
    small_v = {"ln_mix_g": v_ln_mix_g, "ln_mix_b": v_ln_mix_b, "ln_ff_g": v_ln_ff_g,
               "ln_ff_b": v_ln_ff_b, "pool_scale": v_pool_scale}
    small_g = [tot1[0:1], tot1[1:2], tot2[0:1], tot2[1:2], pad_d(totp[0:1])]

    def pack(rows):
        return jnp.concatenate([pad_d(a) for a in rows] + [jnp.zeros((8 - len(rows), d), F32)], axis=0)

    outs = _adamw("adamw_small", pack([small_w[k] for k in small]), pack([small_m[k] for k in small]),
                  pack([small_v[k] for k in small]), [pack(small_g)])
    for i, k in enumerate(small):
        res[k] = [o[i:i + 1, :small_w[k].shape[1]] for o in outs]
    loss = tot2[2, 0]

    order = ["w_in", "w_pool", "pool_scale", "w_branch_attn", "w_branch_pool", "w_out", "ln_mix_g",
             "ln_mix_b", "w_ff1", "w_ff2", "ln_ff_g", "ln_ff_b"]
    result = [loss, grad_x[None]]
    for idx in range(4):
        result += [res[k][idx] for k in order]
    return tuple(result)
```

```python
import jax
import jax.numpy as jnp
from jax import lax
from jax.experimental import pallas as pl
from jax.experimental.pallas import tpu as pltpu

F32 = jnp.float32
BF16 = jnp.bfloat16
MESH = pl.DeviceIdType.MESH

HEAD_DIM = 128
SUB_BLOCK = 128
DILATIONS = (1, 4, 16)
POOL_WINDOWS = (2, 4, 8, 16)
POOL_HALO = 16
ROPE_THETA = 10000.0
LN_EPS = 1e-5
ALPHA = 2.0 ** 0.25
ADAM_LR, ADAM_B1, ADAM_B2, ADAM_EPS, ADAM_WD, ADAM_STEP = 0.001, 0.9, 0.999, 1e-08, 0.01, 10
NEG = -1e30
N_CHIPS = 4
VMEM_LIMIT = 56 * 1024 * 1024
EPILOGUE_ROWS = 128


def _params(sem=None, vmem=VMEM_LIMIT):
    kw = {"vmem_limit_bytes": vmem}
    if sem is not None:
        kw["dimension_semantics"] = sem
    return pltpu.CompilerParams(**kw)


def _dot(a, b, contract):
    return lax.dot_general(a, b, (contract, ((), ())), preferred_element_type=F32)


ANY_SPEC = pl.BlockSpec(memory_space=pl.ANY)
HBM_SPEC = pl.BlockSpec(memory_space=pltpu.HBM)
SEM_SPEC = pl.BlockSpec(memory_space=pltpu.SEMAPHORE)
DATAFLOW = pltpu.SideEffectType.DATAFLOW_SIDE_EFFECTING


def _hbm(a):
    return pltpu.with_memory_space_constraint(a, pltpu.HBM)


NN = ((1,), (0,))
NT = ((1,), (1,))
TN = ((0,), (0,))


def _mm(name, grid, a, a_spec, b, b_spec, contract, extras, extra_specs, out_shape, out_specs,
        epilogue, acc_shape, acc_as_ref=False, run_after=(), scratch=(),
        semantics=("parallel", "parallel", "arbitrary")):
    nk = grid[2]
    n_ex = len(extras)
    n_in = 2 + n_ex + len(run_after)
    n_out = len(out_shape)
    n_scr = len(scratch)

    def body(*refs):
        a_ref, b_ref = refs[0], refs[1]
        ex = refs[2:2 + n_ex]
        outs = refs[n_in:n_in + n_out]
        scr = refs[n_in + n_out:n_in + n_out + n_scr]
        if nk == 1:
            epilogue(_dot(a_ref[...], b_ref[...], contract), ex, outs, *scr)
        else:
            acc = refs[n_in + n_out + n_scr]
            k = pl.program_id(2)

            @pl.when(k == 0)
            def _():
                acc[...] = jnp.zeros_like(acc)

            acc[...] += _dot(a_ref[...], b_ref[...], contract)

            @pl.when(k == nk - 1)
            def _():
                epilogue(acc if acc_as_ref else acc[...], ex, outs, *scr)

    acc_scratch = [pltpu.VMEM(acc_shape, F32)] if nk > 1 else []
    return pl.pallas_call(
        body, name=name, grid=grid,
        in_specs=[a_spec, b_spec, *extra_specs, *[ANY_SPEC] * len(run_after)], out_specs=out_specs,
        out_shape=out_shape, scratch_shapes=[*scratch, *acc_scratch],
        compiler_params=_params(semantics),
    )(a, b, *extras, *run_after)


def _stats_rows(rows, width):
    idx = lax.broadcasted_iota(jnp.int32, (8, width), 0)
    out = jnp.zeros((8, width), F32)
    for r, v in enumerate(rows):
        out = jnp.where(idx == r, jnp.broadcast_to(v, (8, width)), out)
    return out


def _layer_norm_fwd(z):
    mu = jnp.mean(z, axis=-1, keepdims=True)
    zc = z - mu
    var = jnp.mean(zc * zc, axis=-1, keepdims=True)
    rstd = lax.rsqrt(var + LN_EPS)
    return zc * rstd, rstd


def _layer_norm_bwd(dy, xhat, rstd, g):
    dxh = dy * g
    m1 = jnp.mean(dxh, axis=-1, keepdims=True)
    m2 = jnp.mean(dxh * xhat, axis=-1, keepdims=True)
    return rstd * (dxh - m1 - xhat * m2)


def _heads_scratch(rows, width):
    return pltpu.VMEM((width // HEAD_DIM, rows, HEAD_DIM), F32)


def _to_views(src_ref, view_refs, dtype):
    nh, rows, _ = src_ref.shape
    width = nh * HEAD_DIM
    for dil, view_ref in zip(DILATIONS[1:], view_refs):
        for r in range(dil):
            for hh in range(nh):
                c0 = r * width + hh * HEAD_DIM
                view_ref[:, c0:c0 + HEAD_DIM] = (
                    src_ref[hh, pl.ds(r, rows // dil, stride=dil), :].astype(dtype))


def _from_views(view_refs, dst_refs):
    nh, rows, _ = dst_refs[0].shape
    width = nh * HEAD_DIM
    for dil, view_ref, dst_ref in zip(DILATIONS[1:], view_refs, dst_refs):
        for r in range(dil):
            for hh in range(nh):
                c0 = r * width + hh * HEAD_DIM
                dst_ref[hh, pl.ds(r, rows // dil, stride=dil), :] = (
                    view_ref[:, c0:c0 + HEAD_DIM].astype(F32))


def _view_shape(rows, width, dil, parts=1):
    return (rows // dil, parts * dil * width)


def _in_proj(xb, w_in_st, cos_t, sin_t, aw):
    s, d = xb.shape
    n_sh = w_in_st.shape[2]
    tm, tn = min(s, 1024), aw
    per = n_sh // tn
    grid = (s // tm, (N_CHIPS * n_sh) // tn, 1)

    def epilogue(acc, ex, outs, scr):
        cos_ref, sin_ref = ex
        h_ref, v4_ref, v16_ref = outs
        seg = pl.program_id(1)

        heads = [slice(hh * HEAD_DIM, (hh + 1) * HEAD_DIM) for hh in range(tn // HEAD_DIM)]

        @pl.when(seg < 2)
        def _():
            cos, sin = cos_ref[...], sin_ref[...]
            for hh, sl in enumerate(heads):
                t = acc[:, sl]
                scr[hh] = t * cos + pltpu.roll(t, HEAD_DIM // 2, 1) * sin

        @pl.when(seg == 2)
        def _():
            for hh, sl in enumerate(heads):
                scr[hh] = acc[:, sl]

        @pl.when(seg < 3)
        def _():
            for hh, sl in enumerate(heads):
                h_ref[:, sl] = scr[hh].astype(BF16)
            _to_views(scr, (v4_ref, v16_ref), BF16)

        @pl.when(seg == 3)
        def _():
            h_ref[...] = acc.astype(BF16)

        @pl.when(seg >= 4)
        def _():
            h_ref[...] = jax.nn.sigmoid(acc).astype(BF16)

    def view_spec(dil):
        return pl.BlockSpec((tm // dil, dil * aw), lambda i, j, k: (i, jnp.minimum(j, 2)))

    return _mm(
        "in_proj", grid, xb, pl.BlockSpec((tm, d), lambda i, j, k: (i, 0)),
        w_in_st, pl.BlockSpec((None, d, tn), lambda i, j, k: (j // per, 0, j % per)), NN,
        [cos_t, sin_t], [pl.BlockSpec((tm, HEAD_DIM), lambda i, j, k: (i, 0))] * 2,
        [jax.ShapeDtypeStruct((s, N_CHIPS * n_sh), BF16)]
        + [jax.ShapeDtypeStruct(_view_shape(s, aw, dil, 3), BF16) for dil in DILATIONS[1:]],
        [pl.BlockSpec((tm, tn), lambda i, j, k: (i, j))] + [view_spec(dil) for dil in DILATIONS[1:]],
        epilogue, None, scratch=[_heads_scratch(tm, aw)],
        semantics=("parallel", "arbitrary", "arbitrary"))


def _band_masks(block_idx):
    qi = lax.broadcasted_iota(jnp.int32, (SUB_BLOCK, 2 * SUB_BLOCK), 0)
    kj = lax.broadcasted_iota(jnp.int32, (SUB_BLOCK, 2 * SUB_BLOCK), 1)
    first_key = jnp.where(block_idx > 0, 0, SUB_BLOCK)
    return (kj >= qi) & (kj <= qi + SUB_BLOCK) & (kj >= first_key)


def _attn_fwd(name, q, k, v, offs, cw, width):
    m = q.shape[0]
    nh = cw // HEAD_DIM
    grid = (m // SUB_BLOCK, width // cw)
    scale = HEAD_DIM ** -0.5

    def body(q_ref, kp_ref, kc_ref, vp_ref, vc_ref, o_ref, lse_ref):
        valid = _band_masks(pl.program_id(0))
        heads = [slice(hh * HEAD_DIM, (hh + 1) * HEAD_DIM) for hh in range(nh)]
        scs = [_dot(q_ref[:, sl], jnp.concatenate([kp_ref[:, sl], kc_ref[:, sl]], axis=0), NT)
               for sl in heads]
        for hh, sl in enumerate(heads):
            vcat = jnp.concatenate([vp_ref[:, sl], vc_ref[:, sl]], axis=0)
            sc = jnp.where(valid, scs[hh] * scale, NEG)
            mx = jnp.max(sc, axis=-1, keepdims=True)
            p = jnp.exp(sc - mx)
            l = jnp.sum(p, axis=-1, keepdims=True)
            o = _dot(p.astype(BF16), vcat, NN) / l
            o_ref[:, sl] = o.astype(BF16)
            lse_ref[:, sl] = jnp.broadcast_to(mx + jnp.log(l), (SUB_BLOCK, HEAD_DIM))

    def cur(off):
        return pl.BlockSpec((SUB_BLOCK, cw), lambda n, j: (n, j + off))

    def prev(off):
        return pl.BlockSpec((SUB_BLOCK, cw), lambda n, j: (jnp.maximum(n - 1, 0), j + off))

    return pl.pallas_call(
        body, name=name, grid=grid,
        in_specs=[cur(offs[0]), prev(offs[1]), cur(offs[1]), prev(offs[2]), cur(offs[2])],
        out_specs=[cur(0), cur(0)],
        out_shape=[jax.ShapeDtypeStruct((m, width), BF16), jax.ShapeDtypeStruct((m, width), F32)],
        compiler_params=_params(("parallel", "parallel")),
    )(q, k, k, v, v)


def _view_spec(tm, aw, dil):
    return pl.BlockSpec((tm // dil, dil * aw), lambda i: (i, 0))


def _attn_combine(o1, l1, o_views, l_views):
    s, aw = o1.shape
    tm = min(s, 512)

    def body(o1_ref, l1_ref, o4_ref, o16_ref, l4_ref, l16_ref, o_ref, lse_ref, lse4_ref, lse16_ref,
             so4, so16, sl4, sl16, stot):
        _from_views((o4_ref, o16_ref), (so4, so16))
        _from_views((l4_ref, l16_ref), (sl4, sl16))
        for hh in range(aw // HEAD_DIM):
            sl = slice(hh * HEAD_DIM, (hh + 1) * HEAD_DIM)
            a, b, c = l1_ref[:, sl], sl4[hh], sl16[hh]
            mx = jnp.maximum(jnp.maximum(a, b), c)
            ea, eb, ec = jnp.exp(a - mx), jnp.exp(b - mx), jnp.exp(c - mx)
            tot = ea + eb + ec
            o = (ea * o1_ref[:, sl].astype(F32) + eb * so4[hh] + ec * so16[hh]) / tot
            o_ref[:, sl] = o.astype(BF16)
            lse_tot = mx + jnp.log(tot)
            stot[hh] = lse_tot
            lse_ref[:, sl] = lse_tot
        _to_views(stot, (lse4_ref, lse16_ref), F32)

    row = pl.BlockSpec((tm, aw), lambda i: (i, 0))
    views = [_view_spec(tm, aw, dil) for dil in DILATIONS[1:]]
    return pl.pallas_call(
        body, name="attn_combine", grid=(s // tm,), in_specs=[row, row, *views, *views],
        out_specs=[row, row, *views],
        out_shape=[jax.ShapeDtypeStruct((s, aw), BF16), jax.ShapeDtypeStruct((s, aw), F32)]
        + [jax.ShapeDtypeStruct(_view_shape(s, aw, dil), F32) for dil in DILATIONS[1:]],
        scratch_shapes=[_heads_scratch(tm, aw)] * 5,
        compiler_params=_params(("parallel",)),
    )(o1, l1, *o_views, *l_views)


def _pool_counts(tm, rows, pgw, row0):
    t = lax.broadcasted_iota(jnp.int32, (rows, len(POOL_WINDOWS) * pgw), 0) + row0
    col = lax.broadcasted_iota(jnp.int32, (rows, len(POOL_WINDOWS) * pgw), 1)
    w = jnp.full((rows, len(POOL_WINDOWS) * pgw), POOL_WINDOWS[0], jnp.int32)
    for g in range(1, len(POOL_WINDOWS)):
        w = jnp.where(col >= g * pgw, POOL_WINDOWS[g], w)
    return jnp.minimum(t + 1, w).astype(F32)


def _window_sums(xs, direction, pgw):
    rows = xs.shape[0]
    acc = xs
    out = None
    col = lax.broadcasted_iota(jnp.int32, xs.shape, 1)
    for g, w in enumerate(POOL_WINDOWS):
        sh = w // 2
        acc = acc + pltpu.roll(acc, sh if direction > 0 else rows - sh, 0)
        out = acc if out is None else jnp.where(col >= g * pgw, acc, out)
    return out


def _pool_fwd(h, wp, scale, aw):
    s = h.shape[0]
    pw_ = aw
    pgw = pw_ // len(POOL_WINDOWS)
    tm = min(s, 512)
    hb = tm // POOL_HALO

    def body(u_ref, halo_ref, wp_ref, sc_ref, p_ref, pw_ref, y_ref):
        i = pl.program_id(0)
        u = u_ref[...].astype(F32)
        halo = halo_ref[...].astype(F32) * jnp.where(i > 0, 1.0, 0.0)
        xs = jnp.concatenate([halo, u], axis=0)
        sums = _window_sums(xs, +1, pgw)[POOL_HALO:]
        p = (sums / _pool_counts(tm, tm, pgw, i * tm) - u).astype(BF16)
        p_ref[...] = p
        sc = sc_ref[...]
        for g in range(len(POOL_WINDOWS)):
            sl = slice(g * pgw, (g + 1) * pgw)
            pw = _dot(p[:, sl], wp_ref[g], NN)
            pw_ref[:, sl] = pw.astype(BF16)
            y_ref[:, sl] = (pw * sc[:, sl]).astype(BF16)

    out = jax.ShapeDtypeStruct((s, pw_), BF16)
    row = pl.BlockSpec((tm, pw_), lambda i: (i, 0))
    return pl.pallas_call(
        body, name="pool_fwd", grid=(s // tm,),
        in_specs=[pl.BlockSpec((tm, pw_), lambda i: (i, 3)),
                  pl.BlockSpec((POOL_HALO, pw_), lambda i: (jnp.maximum(i * hb - 1, 0), 3)),
                  pl.BlockSpec(wp.shape, lambda i: (0, 0, 0)),
                  pl.BlockSpec((1, pw_), lambda i: (0, 0))],
        out_specs=[row, row, row], out_shape=[out, out, out],
        compiler_params=_params(("parallel",)),
    )(h, h, wp, scale)


def _branch_merge(o_attn, y, wba_st, wbp_st, h, aw, d):
    s = o_attn.shape[0]
    tn = wba_st.shape[2]
    tm = min(s, 1024)
    ga0 = 4 * aw // tn
    gp0 = (4 * aw + d) // tn

    def body(o_ref, y_ref, wa_ref, wp_ref, sga_ref, sgp_ref, ya_ref, yp_ref, mg_ref):
        ya = _dot(o_ref[...], wa_ref[...], NN)
        yp = _dot(y_ref[...], wp_ref[...], NN)
        ya_ref[...] = ya.astype(BF16)
        yp_ref[...] = yp.astype(BF16)
        mg_ref[...] = (sga_ref[...].astype(F32) * ya + sgp_ref[...].astype(F32) * yp).astype(BF16)

    out = jax.ShapeDtypeStruct((s, d), BF16)
    blk = pl.BlockSpec((tm, tn), lambda i, j: (i, j))
    return pl.pallas_call(
        body, name="branch_merge", grid=(s // tm, N_CHIPS),
        in_specs=[pl.BlockSpec((tm, aw), lambda i, j: (i, 0)),
                  pl.BlockSpec((tm, aw), lambda i, j: (i, 0)),
                  pl.BlockSpec((None, aw, tn), lambda i, j: (j, 0, 0)),
                  pl.BlockSpec((None, aw, tn), lambda i, j: (j, 0, 0)),
                  pl.BlockSpec((tm, tn), lambda i, j: (i, j + ga0)),
                  pl.BlockSpec((tm, tn), lambda i, j: (i, j + gp0))],
        out_specs=[blk, blk, blk], out_shape=[out, out, out],
        compiler_params=_params(("parallel", "parallel")),
    )(o_attn, y, wba_st, wbp_st, h, h)


def _mix_norm(merged, w_out, x, g1, b1):
    s, d = x.shape
    tm = min(s, 256)

    def epilogue(acc, ex, outs):
        x_ref, g_ref, b_ref = ex
        xh_ref, rs_ref, xb_ref = outs
        xhat, rstd = _layer_norm_fwd(ALPHA * x_ref[...] + acc)
        xh_ref[...] = xhat
        rs_ref[...] = rstd
        xb_ref[...] = (xhat * g_ref[...] + b_ref[...]).astype(BF16)

    row = pl.BlockSpec((tm, d), lambda i, j, k: (i, 0))
    vec = pl.BlockSpec((1, d), lambda i, j, k: (0, 0))
    return _mm(
        "mix_norm", (s // tm, 1, 1), merged, row, w_out, pl.BlockSpec((d, d), lambda i, j, k: (0, 0)),
        NN, [x, g1, b1], [row, vec, vec],
        [jax.ShapeDtypeStruct((s, d), F32), jax.ShapeDtypeStruct((s, 1), F32),
         jax.ShapeDtypeStruct((s, d), BF16)],
        [row, pl.BlockSpec((tm, 1), lambda i, j, k: (i, 0)), row], epilogue, None)


def _ff_up(x1b, w1_st):
    s, d = x1b.shape
    n_sh = w1_st.shape[2]
    tm, tn = min(s, 1024), min(n_sh, 1024)
    per = n_sh // tn

    def epilogue(acc, ex, outs):
        r = jnp.maximum(acc, 0.0)
        outs[0][...] = (r * r).astype(BF16)

    return _mm(
        "ff_up", (s // tm, N_CHIPS * per, 1), x1b, pl.BlockSpec((tm, d), lambda i, j, k: (i, 0)),
        w1_st, pl.BlockSpec((None, d, tn), lambda i, j, k: (j // per, 0, j % per)), NN, [], [],
        [jax.ShapeDtypeStruct((s, N_CHIPS * n_sh), BF16)],
        [pl.BlockSpec((tm, tn), lambda i, j, k: (i, j))], epilogue, None)[0]


def _ff_down_loss(r, w2, xhat1, g1, b1, g2, b2, target):
    s, d = xhat1.shape
    dff = r.shape[1]
    tm, tk = min(s, 512), min(dff, 1024)
    ch = min(tm, EPILOGUE_ROWS)

    def epilogue(acc_ref, ex, outs):
        xh1_ref, g1_ref, b1_ref, g2_ref, b2_ref, t_ref = ex
        dz_ref, dzb_ref, st_ref = outs
        g1v, b1v, g2v, b2v = g1_ref[...], b1_ref[...], g2_ref[...], b2_ref[...]
        dg = db = loss = None
        for c in range(tm // ch):
            rows = slice(c * ch, (c + 1) * ch)
            x1 = xh1_ref[rows, :] * g1v + b1v
            xhat2, rstd2 = _layer_norm_fwd(ALPHA * x1 + acc_ref[rows, :])
            err = xhat2 * g2v + b2v - t_ref[rows, :]
            dy = err * (1.0 / d)
            dz = _layer_norm_bwd(dy, xhat2, rstd2, g2v)
            dz_ref[rows, :] = dz
            dzb_ref[rows, :] = dz.astype(BF16)
            parts = (jnp.sum(dy * xhat2, axis=0, keepdims=True), jnp.sum(dy, axis=0, keepdims=True),
                     jnp.sum(jnp.sum(err * err, axis=-1, keepdims=True), axis=0, keepdims=True))
            dg, db, loss = parts if c == 0 else (dg + parts[0], db + parts[1], loss + parts[2])
        st_ref[...] = _stats_rows([dg, db, jnp.broadcast_to((0.5 / d) * loss, (1, d))], d)

    row = pl.BlockSpec((tm, d), lambda i, j, k: (i, 0))
    vec = pl.BlockSpec((1, d), lambda i, j, k: (0, 0))
    return _mm(
        "ff_down_loss", (s // tm, 1, dff // tk), r, pl.BlockSpec((tm, tk), lambda i, j, k: (i, k)),
        w2, pl.BlockSpec((tk, d), lambda i, j, k: (k, 0)), NN,
        [xhat1, g1, b1, g2, b2, target], [row, vec, vec, vec, vec, row],
        [jax.ShapeDtypeStruct((s, d), F32), jax.ShapeDtypeStruct((s, d), BF16),
         jax.ShapeDtypeStruct((s // tm, 8, d), F32)],
        [row, row, pl.BlockSpec((None, 8, d), lambda i, j, k: (i, 0, 0))], epilogue, (tm, d),
        acc_as_ref=True)


def _ff_down_bwd(dz2b, w2, r):
    s, d = dz2b.shape
    dff = r.shape[1]
    tm, tn = min(s, 1024), min(dff, 1024)

    def epilogue(acc, ex, outs):
        outs[0][...] = (acc * (2.0 * jnp.sqrt(ex[0][...].astype(F32)))).astype(BF16)

    blk = pl.BlockSpec((tm, tn), lambda i, j, k: (i, j))
    return _mm(
        "ff_down_bwd", (s // tm, dff // tn, 1), dz2b, pl.BlockSpec((tm, d), lambda i, j, k: (i, 0)),
        w2, pl.BlockSpec((tn, d), lambda i, j, k: (j, 0)), NT, [r], [blk],
        [jax.ShapeDtypeStruct((s, dff), BF16)], [blk], epilogue, None)[0]


def _wgrad(name, a, g, n_sh):
    s, rows = a.shape
    cols = g.shape[1]
    tm, tn, tk = min(rows, 2048), min(cols, 1024), min(s, 1024)
    if tn >= n_sh:
        span = tn // n_sh
        out_spec = pl.BlockSpec((span, tm, n_sh), lambda i, j, k: (j, i, 0))

        def epilogue(acc_ref, ex, outs):
            for sh in range(span):
                outs[0][sh] = acc_ref[:, sh * n_sh:(sh + 1) * n_sh].astype(BF16)
    else:
        per = n_sh // tn
        out_spec = pl.BlockSpec((None, tm, tn), lambda i, j, k: (j // per, i, j % per))

        def epilogue(acc_ref, ex, outs):
            outs[0][...] = acc_ref[...].astype(BF16)

    return _mm(
        name, (rows // tm, cols // tn, s // tk), a, pl.BlockSpec((tk, tm), lambda i, j, k: (k, i)),
        g, pl.BlockSpec((tk, tn), lambda i, j, k: (k, j)), TN, [], [],
        [jax.ShapeDtypeStruct((cols // n_sh, rows, n_sh), BF16)], [out_spec], epilogue,
        (tm, tn), acc_as_ref=True)[0]


def _ff_up_bwd(da, w1_st, dz2, xhat1, rstd1, g1, run_after):
    s, d = dz2.shape
    n_sh = w1_st.shape[2]
    tm, tk = min(s, 512), min(n_sh, 1024)
    per = n_sh // tk
    ch = min(tm, EPILOGUE_ROWS)

    def epilogue(acc_ref, ex, outs):
        dz2_ref, xh_ref, rs_ref, g_ref = ex
        dz_ref, dzb_ref, st_ref = outs
        gv = g_ref[...]
        dg = db = None
        for c in range(tm // ch):
            rows = slice(c * ch, (c + 1) * ch)
            dx1 = ALPHA * dz2_ref[rows, :] + acc_ref[rows, :]
            xhat = xh_ref[rows, :]
            dz = _layer_norm_bwd(dx1, xhat, rs_ref[rows, :], gv)
            dz_ref[rows, :] = dz
            dzb_ref[rows, :] = dz.astype(BF16)
            parts = (jnp.sum(dx1 * xhat, axis=0, keepdims=True), jnp.sum(dx1, axis=0, keepdims=True))
            dg, db = parts if c == 0 else (dg + parts[0], db + parts[1])
        st_ref[...] = _stats_rows([dg, db], d)

    row = pl.BlockSpec((tm, d), lambda i, j, k: (i, 0))
    return _mm(
        "ff_up_bwd", (s // tm, 1, N_CHIPS * per), da, pl.BlockSpec((tm, tk), lambda i, j, k: (i, k)),
        w1_st, pl.BlockSpec((None, d, tk), lambda i, j, k: (k // per, 0, k % per)), NT,
        [dz2, xhat1, rstd1, g1],
        [row, row, pl.BlockSpec((tm, 1), lambda i, j, k: (i, 0)), pl.BlockSpec((1, d), lambda i, j, k: (0, 0))],
        [jax.ShapeDtypeStruct((s, d), F32), jax.ShapeDtypeStruct((s, d), BF16),
         jax.ShapeDtypeStruct((s // tm, 8, d), F32)],
        [row, row, pl.BlockSpec((None, 8, d), lambda i, j, k: (i, 0, 0))], epilogue, (tm, d),
        acc_as_ref=True, run_after=run_after)


def _mix_bwd(dz1b, w_out, h, ya, yp, aw):
    s, d = dz1b.shape
    tm = min(s, 256)
    gblk = 4 * aw // d

    def epilogue(acc, ex, outs):
        sga_ref, sgp_ref, ya_ref, yp_ref = ex
        dya_ref, dyp_ref, dg_ref = outs
        sga, sgp = sga_ref[...].astype(F32), sgp_ref[...].astype(F32)
        dya_ref[...] = (acc * sga).astype(BF16)
        dyp_ref[...] = (acc * sgp).astype(BF16)
        dg_ref[:, :d] = (acc * ya_ref[...].astype(F32) * (sga * (1.0 - sga))).astype(BF16)
        dg_ref[:, d:] = (acc * yp_ref[...].astype(F32) * (sgp * (1.0 - sgp))).astype(BF16)

    row = pl.BlockSpec((tm, d), lambda i, j, k: (i, 0))
    return _mm(
        "mix_bwd", (s // tm, 1, 1), dz1b, row, w_out, pl.BlockSpec((d, d), lambda i, j, k: (0, 0)), NT,
        [h, h, ya, yp],
        [pl.BlockSpec((tm, d), lambda i, j, k: (i, gblk)),
         pl.BlockSpec((tm, d), lambda i, j, k: (i, gblk + 1)), row, row],
        [jax.ShapeDtypeStruct((s, d), BF16), jax.ShapeDtypeStruct((s, d), BF16),
         jax.ShapeDtypeStruct((s, 2 * d), BF16)],
        [row, row, pl.BlockSpec((tm, 2 * d), lambda i, j, k: (i, 0))], epilogue, None)


def _branch_in_bwd(name, tm, dyb, wb_st, epilogue, extras, extra_specs, out_shape, out_specs,
                   scratch=()):
    s = dyb.shape[0]
    aw, tk = wb_st.shape[1], wb_st.shape[2]
    return _mm(
        name, (s // tm, 1, N_CHIPS), dyb, pl.BlockSpec((tm, tk), lambda i, j, k: (i, k)),
        wb_st, pl.BlockSpec((None, aw, tk), lambda i, j, k: (k, 0, 0)), NT,
        extras, extra_specs, out_shape, out_specs, epilogue, (tm, aw), acc_as_ref=True,
        scratch=scratch)


def _attn_out_bwd(dya, wba_st, o_attn):
    s, aw = o_attn.shape
    tm = min(s, 512)

    def epilogue(acc_ref, ex, outs, sdo, sdl):
        do_ref, dl_ref, do4_ref, do16_ref, dl4_ref, dl16_ref = outs
        for hh in range(aw // HEAD_DIM):
            sl = slice(hh * HEAD_DIM, (hh + 1) * HEAD_DIM)
            do = acc_ref[:, sl]
            dl = jnp.broadcast_to(
                jnp.sum(do * ex[0][:, sl].astype(F32), axis=-1, keepdims=True), (tm, HEAD_DIM))
            sdo[hh] = do
            sdl[hh] = dl
            do_ref[:, sl] = do.astype(BF16)
            dl_ref[:, sl] = dl
        _to_views(sdo, (do4_ref, do16_ref), BF16)
        _to_views(sdl, (dl4_ref, dl16_ref), F32)

    row = pl.BlockSpec((tm, aw), lambda i, j, k: (i, 0))
    views = [pl.BlockSpec((tm // dil, dil * aw), lambda i, j, k: (i, 0)) for dil in DILATIONS[1:]]
    view_shapes = [_view_shape(s, aw, dil) for dil in DILATIONS[1:]]
    return _branch_in_bwd(
        "attn_out_bwd", tm, dya, wba_st, epilogue, [o_attn], [row],
        [jax.ShapeDtypeStruct((s, aw), BF16), jax.ShapeDtypeStruct((s, aw), F32)]
        + [jax.ShapeDtypeStruct(sh, BF16) for sh in view_shapes]
        + [jax.ShapeDtypeStruct(sh, F32) for sh in view_shapes],
        [row, row, *views, *views], scratch=[_heads_scratch(tm, aw)] * 2)


def _pool_out_bwd(dyp, wbp_st, pw, scale):
    s, pw_ = pw.shape
    tm = min(s, 1024)

    def epilogue(acc_ref, ex, outs):
        pw_ref, sc_ref = ex
        dpw_ref, st_ref = outs
        acc = acc_ref[...]
        dpw_ref[...] = (acc * sc_ref[...]).astype(BF16)
        st_ref[...] = _stats_rows([jnp.sum(acc * pw_ref[...].astype(F32), axis=0, keepdims=True)], pw_)

    row = pl.BlockSpec((tm, pw_), lambda i, j, k: (i, 0))
    return _branch_in_bwd(
        "pool_out_bwd", tm, dyp, wbp_st, epilogue, [pw, scale],
        [row, pl.BlockSpec((1, pw_), lambda i, j, k: (0, 0))],
        [jax.ShapeDtypeStruct((s, pw_), BF16), jax.ShapeDtypeStruct((s // tm, 8, pw_), F32)],
        [row, pl.BlockSpec((None, 8, pw_), lambda i, j, k: (i, 0, 0))])


def _pool_bwd(dpw, p, wp):
    s, pw_ = p.shape
    ng = len(POOL_WINDOWS)
    pgw = pw_ // ng
    tm = min(s, 512)
    hb = tm // POOL_HALO
    nblk = s // tm

    def body(dpw_ref, nxt_ref, p_ref, wp_ref, dwp_ref, du_ref):
        i = pl.program_id(0)
        nxt = (nxt_ref[...].astype(F32) * jnp.where(i < nblk - 1, 1.0, 0.0)).astype(BF16)
        dpw_all = jnp.concatenate([dpw_ref[...], nxt], axis=0)

        @pl.when(i == 0)
        def _():
            dwp_ref[...] = jnp.zeros_like(dwp_ref)

        dps = []
        for g in range(ng):
            sl = slice(g * pgw, (g + 1) * pgw)
            dwp_ref[g] += _dot(p_ref[:, sl], dpw_ref[:, sl], TN)
            dps.append(_dot(dpw_all[:, sl], wp_ref[g], NT))
        dp = jnp.concatenate(dps, axis=1)
        dpn = dp / _pool_counts(tm, tm + POOL_HALO, pgw, i * tm)
        du_ref[...] = (_window_sums(dpn, -1, pgw)[:tm] - dp[:tm]).astype(BF16)

    row = pl.BlockSpec((tm, pw_), lambda i: (i, 0))
    full = pl.BlockSpec((ng, pgw, pgw), lambda i: (0, 0, 0))
    return pl.pallas_call(
        body, name="pool_bwd", grid=(nblk,),
        in_specs=[row, pl.BlockSpec((POOL_HALO, pw_), lambda i: (jnp.minimum((i + 1) * hb, s // POOL_HALO - 1), 0)),
                  row, full],
        out_specs=[full, row],
        out_shape=[jax.ShapeDtypeStruct((ng, pgw, pgw), F32), jax.ShapeDtypeStruct((s, pw_), BF16)],
        compiler_params=_params(("arbitrary",)),
    )(dpw, dpw, p, wp)


def _attn_bwd_q(name, q, k, v, do, lse, delta, offs, cw, width):
    m = do.shape[0]
    nh = cw // HEAD_DIM
    grid = (m // SUB_BLOCK, width // cw)
    scale = HEAD_DIM ** -0.5

    def body(q_ref, kp_ref, kc_ref, vp_ref, vc_ref, do_ref, lse_ref, dl_ref, dq_ref):
        valid = _band_masks(pl.program_id(0))
        heads = [slice(hh * HEAD_DIM, (hh + 1) * HEAD_DIM) for hh in range(nh)]
        kcats = [jnp.concatenate([kp_ref[:, sl], kc_ref[:, sl]], axis=0) for sl in heads]
        scs = [_dot(q_ref[:, sl], kcats[hh], NT) for hh, sl in enumerate(heads)]
        dps = [_dot(do_ref[:, sl], jnp.concatenate([vp_ref[:, sl], vc_ref[:, sl]], axis=0), NT)
               for sl in heads]
        for hh, sl in enumerate(heads):
            lse2 = jnp.concatenate([lse_ref[:, sl]] * 2, axis=1)
            dl2 = jnp.concatenate([dl_ref[:, sl]] * 2, axis=1)
            p = jnp.where(valid, jnp.exp(jnp.where(valid, scs[hh] * scale, NEG) - lse2), 0.0)
            ds = (p * (dps[hh] - dl2)).astype(BF16)
            dq_ref[:, sl] = (_dot(ds, kcats[hh], NN) * scale).astype(BF16)

    def cur(off):
        return pl.BlockSpec((SUB_BLOCK, cw), lambda n, j: (n, j + off))

    def prev(off):
        return pl.BlockSpec((SUB_BLOCK, cw), lambda n, j: (jnp.maximum(n - 1, 0), j + off))

    return pl.pallas_call(
        body, name=name, grid=grid,
        in_specs=[cur(offs[0]), prev(offs[1]), cur(offs[1]), prev(offs[2]), cur(offs[2]),
                  cur(0), cur(0), cur(0)],
        out_specs=cur(0), out_shape=jax.ShapeDtypeStruct((m, width), BF16),
        compiler_params=_params(("parallel", "parallel")),
    )(q, k, k, v, v, do, lse, delta)


def _attn_bwd_kv(name, q, k, v, do, lse, delta, offs, cw, width):
    m = do.shape[0]
    nh = cw // HEAD_DIM
    nblk = m // SUB_BLOCK
    grid = (nblk, width // cw)
    scale = HEAD_DIM ** -0.5

    def body(qc_ref, qn_ref, k_ref, v_ref, doc_ref, don_ref, lsec_ref, lsen_ref, dlc_ref, dln_ref,
             dk_ref, dv_ref):
        no_next = jnp.where(pl.program_id(0) < nblk - 1, 0, 4 * SUB_BLOCK)
        r = lax.broadcasted_iota(jnp.int32, (2 * SUB_BLOCK, SUB_BLOCK), 0)
        kj = lax.broadcasted_iota(jnp.int32, (2 * SUB_BLOCK, SUB_BLOCK), 1)
        valid = ((r < SUB_BLOCK) & (kj <= r)) | ((r >= SUB_BLOCK) & (kj >= r - SUB_BLOCK + no_next))
        heads = [slice(hh * HEAD_DIM, (hh + 1) * HEAD_DIM) for hh in range(nh)]
        qcats = [jnp.concatenate([qc_ref[:, sl], qn_ref[:, sl]], axis=0) for sl in heads]
        docats = [jnp.concatenate([doc_ref[:, sl], don_ref[:, sl]], axis=0) for sl in heads]
        scs = [_dot(qcats[hh], k_ref[:, sl], NT) for hh, sl in enumerate(heads)]
        dps = [_dot(docats[hh], v_ref[:, sl], NT) for hh, sl in enumerate(heads)]
        for hh, sl in enumerate(heads):
            lse2 = jnp.concatenate([lsec_ref[:, sl], lsen_ref[:, sl]], axis=0)
            dl2 = jnp.concatenate([dlc_ref[:, sl], dln_ref[:, sl]], axis=0)
            p = jnp.where(valid, jnp.exp(jnp.where(valid, scs[hh] * scale, NEG) - lse2), 0.0)
            ds = (p * (dps[hh] - dl2)).astype(BF16)
            dv_ref[:, sl] = _dot(p.astype(BF16), docats[hh], TN).astype(BF16)
            dk_ref[:, sl] = (_dot(ds, qcats[hh], TN) * scale).astype(BF16)

    def cur(off):
        return pl.BlockSpec((SUB_BLOCK, cw), lambda n, j: (n, j + off))

    def nxt(off):
        return pl.BlockSpec((SUB_BLOCK, cw), lambda n, j: (jnp.minimum(n + 1, nblk - 1), j + off))

    out = jax.ShapeDtypeStruct((m, width), BF16)
    return pl.pallas_call(
        body, name=name, grid=grid,
        in_specs=[cur(offs[0]), nxt(offs[0]), cur(offs[1]), cur(offs[2]),
                  cur(0), nxt(0), cur(0), nxt(0), cur(0), nxt(0)],
        out_specs=[cur(0), cur(0)], out_shape=[out, out],
        compiler_params=_params(("parallel", "parallel")),
    )(q, q, k, v, do, do, lse, lse, delta, delta)


def _qkvu_grad(d1, d4, d16, du, cos_t, sin_t):
    s, aw = du.shape
    tm = min(s, 512)

    def body(*refs):
        nat, v4, v16 = refs[0:3], refs[3:6], refs[6:9]
        cos_ref, sin_ref, du_ref, out_ref, s4, s16 = refs[9:]
        cos, sin = cos_ref[...], sin_ref[...]
        for part in range(3):
            _from_views((v4[part], v16[part]), (s4, s16))
            for hh in range(aw // HEAD_DIM):
                sl = slice(hh * HEAD_DIM, (hh + 1) * HEAD_DIM)
                t = nat[part][:, sl].astype(F32) + s4[hh] + s16[hh]
                if part < 2:
                    t = t * cos - pltpu.roll(t, HEAD_DIM // 2, 1) * sin
                out_ref[:, part * aw + hh * HEAD_DIM:part * aw + (hh + 1) * HEAD_DIM] = t.astype(BF16)
        out_ref[:, 3 * aw:] = du_ref[...]

    row = pl.BlockSpec((tm, aw), lambda i: (i, 0))
    tab = pl.BlockSpec((tm, HEAD_DIM), lambda i: (i, 0))
    return pl.pallas_call(
        body, name="qkvu_grad", grid=(s // tm,),
        in_specs=[row] * 3 + [_view_spec(tm, aw, 4)] * 3 + [_view_spec(tm, aw, 16)] * 3 + [tab, tab, row],
        out_specs=pl.BlockSpec((tm, 4 * aw), lambda i: (i, 0)),
        out_shape=jax.ShapeDtypeStruct((s, 4 * aw), BF16),
        scratch_shapes=[_heads_scratch(tm, aw)] * 2,
        compiler_params=_params(("parallel",)),
    )(*d1, *d4, *d16, cos_t, sin_t, du)


def _in_proj_bwd_x(name, dh, w_in_st, shard0, base, scale_base, run_after=()):
    s, kdim = dh.shape
    d, n_sh = w_in_st.shape[1], w_in_st.shape[2]
    tm, tk = min(s, 512), min(n_sh, 2048)
    per = n_sh // tk

    ch = min(tm, 2 * EPILOGUE_ROWS)

    def epilogue(acc_ref, ex, outs):
        for c in range(tm // ch):
            rows = slice(c * ch, (c + 1) * ch)
            outs[0][rows, :] = scale_base * ex[0][rows, :] + acc_ref[rows, :]

    row = pl.BlockSpec((tm, d), lambda i, j, k: (i, 0))
    return _mm(
        name, (s // tm, 1, kdim // tk), dh, pl.BlockSpec((tm, tk), lambda i, j, k: (i, k)),
        w_in_st, pl.BlockSpec((None, d, tk), lambda i, j, k: (shard0 + k // per, 0, k % per)), NT,
        [base], [row], [jax.ShapeDtypeStruct((s, d), F32)], [row], epilogue, (tm, d),
        acc_as_ref=True, run_after=run_after)[0]


def _chip_peers():
    x, y, c = lax.axis_index("x"), lax.axis_index("y"), lax.axis_index("c")
    return x, y, c, [(1 - x, y), (x, 1 - y), (1 - x, 1 - y)]


GATHER, SCATTER, SIBLING = "gather", "scatter", "sibling"


def _exchange_peers(mode):
    x, y, c, chips = _chip_peers()
    if mode == SIBLING:
        return x, y, c, [(x, y, 1 - c)]
    return x, y, c, [(px, py, c) for px, py in chips]


def _exchange_descriptor(mode, src, land, send, recv, p, peer, me, arriving):
    pid = 2 * peer[0] + peer[1]
    if mode == GATHER:
        src_ref, dst_ref = src, land.at[pid if arriving else me]
    elif mode == SCATTER:
        src_ref, dst_ref = src.at[pid], land.at[p]
    else:
        src_ref, dst_ref = src, land
    return pltpu.make_async_remote_copy(
        src_ref=src_ref, dst_ref=dst_ref, send_sem=send.at[p], recv_sem=recv.at[p],
        device_id=peer, device_id_type=MESH)


def _exchange_start(name, mode, srcs, land_shapes):
    n = len(srcs)
    lands = [_hbm(lax.empty(shape, src.dtype)) for shape, src in zip(land_shapes, srcs)]

    def body(*refs):
        src_refs, land_refs = refs[:n], refs[n:2 * n]
        sends, recvs = refs[2 * n:3 * n], refs[3 * n:4 * n]
        token = refs[6 * n]
        x, y, c, peers = _exchange_peers(mode)
        me = 2 * x + y
        for w in range(n):
            for p, peer in enumerate(peers):
                _exchange_descriptor(mode, src_refs[w], land_refs[w], sends[w], recvs[w], p, peer,
                                     me, arriving=False).start()
        token[...] = jnp.zeros_like(token)

    sem = pltpu.SemaphoreType.DMA((3,))
    outs = pl.pallas_call(
        body, name=name, in_specs=[HBM_SPEC] * (2 * n),
        out_specs=[SEM_SPEC] * (2 * n) + [HBM_SPEC] * (2 * n) + [pl.BlockSpec(memory_space=pltpu.VMEM)],
        out_shape=[sem] * (2 * n) + [pltpu.HBM(a.shape, a.dtype) for a in (*srcs, *lands)]
        + [jax.ShapeDtypeStruct((8, 128), F32)],
        input_output_aliases={i: 2 * n + i for i in range(2 * n)},
        compiler_params=pltpu.CompilerParams(has_side_effects=DATAFLOW),
    )(*[_hbm(a) for a in srcs], *lands)
    return {"send": outs[:n], "recv": outs[n:2 * n], "src": outs[2 * n:3 * n],
            "land": outs[3 * n:4 * n], "token": outs[4 * n]}


def _exchange_wait(name, mode, started, which, after):
    m = len(which)

    def body(*refs):
        src_refs, land_refs = refs[:m], refs[m:2 * m]
        sends, recvs = refs[2 * m:3 * m], refs[3 * m:4 * m]
        x, y, c, peers = _exchange_peers(mode)
        me = 2 * x + y
        for w in range(m):
            for p, peer in enumerate(peers):
                _exchange_descriptor(mode, src_refs[w], land_refs[w], sends[w], recvs[w], p, peer,
                                     me, arriving=False).wait_send()
                _exchange_descriptor(mode, src_refs[w], land_refs[w], sends[w], recvs[w], p, peer,
                                     me, arriving=True).wait_recv()

    pick = lambda key: [started[key][w] for w in which]
    bufs = pick("src") + pick("land")
    outs = pl.pallas_call(
        body, name=name, in_specs=[HBM_SPEC] * (2 * m) + [SEM_SPEC] * (2 * m) + [ANY_SPEC],
        out_specs=[HBM_SPEC] * (2 * m), out_shape=[pltpu.HBM(a.shape, a.dtype) for a in bufs],
        input_output_aliases={i: i for i in range(2 * m)},
        compiler_params=pltpu.CompilerParams(has_side_effects=DATAFLOW),
    )(*bufs, *pick("send"), *pick("recv"), after)
    return outs[:m], outs[m:]


def _place_own(name, shard, land, me):
    r, c = shard.shape
    tm = min(r, 512)

    def body(me_ref, shard_ref, land_ref, out_ref):
        out_ref[...] = shard_ref[...]

    return pl.pallas_call(
        body, name=name,
        grid_spec=pltpu.PrefetchScalarGridSpec(
            num_scalar_prefetch=1, grid=(r // tm,),
            in_specs=[pl.BlockSpec((tm, c), lambda i, me_ref: (i, 0)), ANY_SPEC],
            out_specs=pl.BlockSpec((None, tm, c), lambda i, me_ref: (me_ref[0], i, 0))),
        out_shape=jax.ShapeDtypeStruct(land.shape, land.dtype), input_output_aliases={2: 0},
        compiler_params=_params(("arbitrary",)),
    )(me, shard, land)


def _sum_slabs(name, grads, land, me):
    _, r, c = grads.shape
    tm = min(r, 256)

    def body(me_ref, own_ref, land_ref, out_ref):
        acc = own_ref[...].astype(F32)
        for p in range(3):
            acc = acc + land_ref[p].astype(F32)
        out_ref[...] = acc

    return pl.pallas_call(
        body, name=name,
        grid_spec=pltpu.PrefetchScalarGridSpec(
            num_scalar_prefetch=1, grid=(r // tm,),
            in_specs=[pl.BlockSpec((None, tm, c), lambda i, me_ref: (me_ref[0], i, 0)),
                      pl.BlockSpec((3, tm, c), lambda i, me_ref: (0, i, 0))],
            out_specs=pl.BlockSpec((tm, c), lambda i, me_ref: (i, 0))),
        out_shape=jax.ShapeDtypeStruct((r, c), F32), compiler_params=_params(("parallel",)),
    )(me, grads, land)


def _allreduce_stats(stats):
    n = len(stats)

    def body(*refs):
        ins, outs = refs[:n], refs[n:2 * n]
        mine, gath = refs[2 * n:3 * n], refs[3 * n:4 * n]
        send, recv = refs[4 * n:]
        x, y, c = lax.axis_index("x"), lax.axis_index("y"), lax.axis_index("c")
        me = 4 * x + 2 * y + c
        flips = [(bx, by, bc) for bx in (0, 1) for by in (0, 1) for bc in (0, 1)][1:]

        def peer(f):
            return (x + f[0] * (1 - 2 * x), y + f[1] * (1 - 2 * y), c + f[2] * (1 - 2 * c))

        copies = []
        for t in range(n):
            tot = ins[t][0]
            for b in range(1, ins[t].shape[0]):
                tot = tot + ins[t][b]
            mine[t][...] = tot
            gath[t][me] = tot
            for k, f in enumerate(flips):
                cp = pltpu.make_async_remote_copy(
                    src_ref=mine[t], dst_ref=gath[t].at[me], send_sem=send.at[t, k],
                    recv_sem=recv.at[t, k], device_id=peer(f), device_id_type=MESH)
                cp.start()
                copies.append(cp)
        for t in range(n):
            for k, f in enumerate(flips):
                px, py, pc = peer(f)
                pltpu.make_async_remote_copy(
                    src_ref=mine[t], dst_ref=gath[t].at[4 * px + 2 * py + pc], send_sem=send.at[t, k],
                    recv_sem=recv.at[t, k], device_id=(px, py, pc), device_id_type=MESH).wait_recv()
        for cp in copies:
            cp.wait_send()
        for t in range(n):
            tot = gath[t][0]
            for dev in range(1, 8):
                tot = tot + gath[t][dev]
            outs[t][...] = tot

    vm = pl.BlockSpec(memory_space=pltpu.VMEM)
    return pl.pallas_call(
        body, name="allreduce_stats", in_specs=[vm] * n, out_specs=[vm] * n,
        out_shape=[jax.ShapeDtypeStruct(s.shape[1:], F32) for s in stats],
        scratch_shapes=[pltpu.VMEM(s.shape[1:], F32) for s in stats]
        + [pltpu.VMEM((8, *s.shape[1:]), F32) for s in stats]
        + [pltpu.SemaphoreType.DMA((n, 7)), pltpu.SemaphoreType.DMA((n, 7))],
    )(*stats)


def _adamw(name, w, m, v, g_parts):
    r, c = w.shape
    tm = min(r, 128)
    n_g = len(g_parts)

    def body(*refs):
        w_ref, m_ref, v_ref = refs[:3]
        g_refs = refs[3:3 + n_g]
        g_out, d_out, m_out, v_out = refs[3 + n_g:]
        g = g_refs[0][...]
        for gr in g_refs[1:]:
            g = g + gr[...]
        m_new = ADAM_B1 * m_ref[...] + (1.0 - ADAM_B1) * g
        v_new = ADAM_B2 * v_ref[...] + (1.0 - ADAM_B2) * (g * g)
        m_hat = m_new / (1.0 - ADAM_B1 ** ADAM_STEP)
        v_hat = v_new / (1.0 - ADAM_B2 ** ADAM_STEP)
        g_out[...] = g
        d_out[...] = -ADAM_LR * (m_hat / (jnp.sqrt(v_hat) + ADAM_EPS) + ADAM_WD * w_ref[...])
        m_out[...] = m_new
        v_out[...] = v_new

    blk = pl.BlockSpec((tm, c), lambda i: (i, 0))
    out = jax.ShapeDtypeStruct((r, c), F32)
    return pl.pallas_call(
        body, name=name, grid=(r // tm,), in_specs=[blk] * (3 + n_g), out_specs=[blk] * 4,
        out_shape=[out] * 4, compiler_params=_params(("parallel",)),
    )(w, m, v, *g_parts)


def _rope_tables(positions):
    half = HEAD_DIM // 2
    inv_freq = ROPE_THETA ** (-jnp.arange(half, dtype=F32) / half)
    ang = positions.astype(F32)[0, :, None] * inv_freq
    cos, sin = jnp.cos(ang), jnp.sin(ang)
    return jnp.concatenate([cos, cos], axis=-1), jnp.concatenate([-sin, sin], axis=-1)


def kernel(x, positions, w_in, w_pool, pool_scale, w_branch_attn, w_branch_pool, w_out, ln_mix_g, ln_mix_b, w_ff1, w_ff2, ln_ff_g, ln_ff_b, loss_target, m_w_in, m_w_pool, m_pool_scale, m_w_branch_attn, m_w_branch_pool, m_w_out, m_ln_mix_g, m_ln_mix_b, m_w_ff1, m_w_ff2, m_ln_ff_g, m_ln_ff_b, v_w_in, v_w_pool, v_pool_scale, v_w_branch_attn, v_w_branch_pool, v_w_out, v_ln_mix_g, v_ln_mix_b, v_w_ff1, v_w_ff2, v_ln_ff_g, v_ln_ff_b):
    s, d = x.shape[1], x.shape[2]
    aw = d // 2
    ng = len(POOL_WINDOWS)
    pgw = aw // ng
    x2d, target = x[0], loss_target[0]
    xb = x2d.astype(BF16)
    cos_t, sin_t = _rope_tables(positions)

    big = {"w_in": w_in[0], "w_pool": w_pool[0].reshape(-1, pgw), "w_branch_attn": w_branch_attn[0],
           "w_branch_pool": w_branch_pool[0], "w_out": w_out[0], "w_ff1": w_ff1[0], "w_ff2": w_ff2[0]}
    names = list(big)
    me_chip = (2 * lax.axis_index("x") + lax.axis_index("y")).astype(jnp.int32).reshape(1)
    shards = [big[k].astype(BF16) for k in names]
    gathering = _exchange_start("gather_start", GATHER, shards, [(N_CHIPS, *a.shape) for a in shards])

    def gathered(name, which, after):
        srcs, lands = _exchange_wait(f"gather_wait_{name}", GATHER, gathering, which, after)
        return [_place_own(f"place_own_{names[w]}", srcs[i], lands[i], me_chip) for i, w in enumerate(which)]

    rows_sh = pgw // N_CHIPS
    dff = N_CHIPS * big["w_ff2"].shape[0]

    w_in_st, wp_st = gathered("in", [0, 1], gathering["token"])
    wp = wp_st.reshape(N_CHIPS, ng, rows_sh, pgw).transpose(1, 0, 2, 3).reshape(ng, pgw, pgw)
    h, hv4, hv16 = _in_proj(xb, w_in_st, cos_t, sin_t, aw)
    qkv = {1: (h, h, h), 4: (hv4, hv4, hv4), 16: (hv16, hv16, hv16)}
    offs = {dil: (0, dil, 2 * dil) for dil in DILATIONS}
    o_parts, lse_parts = [], []
    for dil in DILATIONS:
        o_p, lse_p = _attn_fwd(f"attn_fwd_d{dil}", *qkv[dil], offs[dil], aw, dil * aw)
        o_parts.append(o_p)
        lse_parts.append(lse_p)
    o_attn, lse, lse4, lse16 = _attn_combine(o_parts[0], lse_parts[0], o_parts[1:], lse_parts[1:])
    p, pw, y = _pool_fwd(h, wp, pool_scale, aw)
    wba_st, wbp_st, w_out_st = gathered("mix", [2, 3, 4], y)
    w_out_full = w_out_st.reshape(d, d)
    ya, yp, merged = _branch_merge(o_attn, y, wba_st, wbp_st, h, aw, d)
    xhat1, rstd1, x1b = _mix_norm(merged, w_out_full, x2d, ln_mix_g, ln_mix_b)
    w1_st, w2_st = gathered("ff", [5, 6], x1b)
    w2_full = w2_st.reshape(dff, d)
    r = _ff_up(x1b, w1_st)
    dz2, dz2b, st2 = _ff_down_loss(r, w2_full, xhat1, ln_mix_g, ln_mix_b, ln_ff_g, ln_ff_b, target)

    def scatter_start(name, grads):
        return _exchange_start(f"scatter_start_{name}", SCATTER, grads, [(3, *g.shape[1:]) for g in grads])

    da = _ff_down_bwd(dz2b, w2_full, r)
    g_w2 = _wgrad("wgrad_ff2", r, dz2b, d).reshape(N_CHIPS, dff // N_CHIPS, d)
    g_w1 = _wgrad("wgrad_ff1", x1b, da, dff // N_CHIPS)
    sent_ff = scatter_start("ff", [g_w1, g_w2])
    dz1, dz1b, st1 = _ff_up_bwd(da, w1_st, dz2, xhat1, rstd1, ln_mix_g, [sent_ff["token"]])
    dya, dyp, dgate = _mix_bwd(dz1b, w_out_full, h, ya, yp, aw)
    g_wout = _wgrad("wgrad_out", merged, dz1b, d).reshape(N_CHIPS, d // N_CHIPS, d)
    g_wba = _wgrad("wgrad_branch_attn", o_attn, dya, d // N_CHIPS)
    g_wbp = _wgrad("wgrad_branch_pool", y, dyp, d // N_CHIPS)
    do, delta, do4, do16, delta4, delta16 = _attn_out_bwd(dya, wba_st, o_attn)
    dpw, stp = _pool_out_bwd(dyp, wbp_st, pw, pool_scale)
    dwp, du = _pool_bwd(dpw, p, wp)
    g_wp = dwp.reshape(ng, N_CHIPS, rows_sh, pgw).transpose(1, 0, 2, 3).reshape(
        N_CHIPS, ng * rows_sh, pgw).astype(BF16)
    sent_mix = scatter_start("mix", [g_wp, g_wba, g_wbp, g_wout])

    bwd_in = {1: (do, lse, delta), 4: (do4, lse4, delta4), 16: (do16, lse16, delta16)}
    dqkv = {}
    for dil in DILATIONS:
        args = (*qkv[dil], *bwd_in[dil], offs[dil], aw, dil * aw)
        dq_p = _attn_bwd_q(f"attn_bwd_q_d{dil}", *args)
        dqkv[dil] = (dq_p, *_attn_bwd_kv(f"attn_bwd_kv_d{dil}", *args))
    dqkvu = _qkvu_grad(dqkv[1], dqkv[4], dqkv[16], du, cos_t, sin_t)
    g_win_a = _wgrad("wgrad_in_qkvu", xb, dqkvu, d)
    g_win_b = _wgrad("wgrad_in_gates", xb, dgate, d)
    g_win = jnp.concatenate([g_win_a, g_win_b], axis=0)
    sent_in = scatter_start("in", [g_win])
    dx_a = _in_proj_bwd_x("in_proj_bwd_qkvu", dqkvu, w_in_st, 0, dz1, ALPHA,
                          [sent_mix["token"], sent_in["token"]])
    grad_x = _in_proj_bwd_x("in_proj_bwd_gates", dgate, w_in_st, 2, dx_a, 1.0)

    moments = {"w_in": (m_w_in, v_w_in), "w_pool": (m_w_pool, v_w_pool),
               "w_branch_attn": (m_w_branch_attn, v_w_branch_attn),
               "w_branch_pool": (m_w_branch_pool, v_w_branch_pool), "w_out": (m_w_out, v_w_out),
               "w_ff1": (m_w_ff1, v_w_ff1), "w_ff2": (m_w_ff2, v_w_ff2)}
    originals = {"w_in": w_in, "w_pool": w_pool, "w_branch_attn": w_branch_attn,
                 "w_branch_pool": w_branch_pool, "w_out": w_out, "w_ff1": w_ff1, "w_ff2": w_ff2}
    res = {}

    def summed(name, sent, keys, after):
        srcs, lands = _exchange_wait(f"scatter_wait_{name}", SCATTER, sent, list(range(len(keys))), after)
        parts = [_sum_slabs(f"sum_slabs_{k}", srcs[i], lands[i], me_chip) for i, k in enumerate(keys)]
        return _exchange_start(f"cores_start_{name}", SIBLING, parts, [a.shape for a in parts])

    def updated(name, swapping, keys, after):
        mine, other = _exchange_wait(f"cores_wait_{name}", SIBLING, swapping, list(range(len(keys))), after)
        for i, k in enumerate(keys):
            mk, vk = (a.reshape(big[k].shape) for a in moments[k])
            outs = _adamw(f"adamw_{k}", big[k], mk, vk, [mine[i], other[i]])
            res[k] = [o.reshape(originals[k].shape) for o in outs]

    groups = {"ff": ["w_ff1", "w_ff2"], "mix": ["w_pool", "w_branch_attn", "w_branch_pool", "w_out"],
              "in": ["w_in"]}
    swap_ff = summed("ff", sent_ff, groups["ff"], grad_x)
    swap_mix = summed("mix", sent_mix, groups["mix"], swap_ff["token"])
    swap_in = summed("in", sent_in, groups["in"], swap_mix["token"])
    updated("ff", swap_ff, groups["ff"], swap_in["token"])
    updated("mix", swap_mix, groups["mix"], res["w_ff2"][0])
    updated("in", swap_in, groups["in"], res["w_out"][0])
    tot2, tot1, totp = _allreduce_stats([st2, st1, stp])

    def pad_d(a):
        return jnp.pad(a, ((0, 0), (0, d - a.shape[1])))

    small = ["ln_mix_g", "ln_mix_b", "ln_ff_g", "ln_ff_b", "pool_scale"]
    small_w = {"ln_mix_g": ln_mix_g, "ln_mix_b": ln_mix_b, "ln_ff_g": ln_ff_g, "ln_ff_b": ln_ff_b,
               "pool_scale": pool_scale}
    small_m = {"ln_mix_g": m_ln_mix_g, "ln_mix_b": m_ln_mix_b, "ln_ff_g": m_ln_ff_g,
               "ln_ff_b": m_ln_ff_b, "pool_scale": m_pool_scale}
    small_v = {"ln_mix_g": v_ln_mix_g, "ln_mix_b": v_ln_mix_b, "ln_ff_g": v_ln_ff_g,
               "ln_ff_b": v_ln_ff_b, "pool_scale": v_pool_scale}
    small_g = [tot1[0:1], tot1[1:2], tot2[0:1], tot2[1:2], pad_d(totp[0:1])]

    def pack(rows):
        return jnp.concatenate([pad_d(a) for a in rows] + [jnp.zeros((8 - len(rows), d), F32)], axis=0)

    outs = _adamw("adamw_small", pack([small_w[k] for k in small]), pack([small_m[k] for k in small]),
                  pack([small_v[k] for k in small]), [pack(small_g)])
    for i, k in enumerate(small):
        res[k] = [o[i:i + 1, :small_w[k].shape[1]] for o in outs]
    loss = tot2[2, 0]

    order = ["w_in", "w_pool", "pool_scale", "w_branch_attn", "w_branch_pool", "w_out", "ln_mix_g",
             "ln_mix_b", "w_ff1", "w_ff2", "ln_ff_g", "ln_ff_b"]
    result = [loss, grad_x[None]]
    for idx in range(4):
        result += [res[k][idx] for k in order]
    return tuple(result)
```

```python
import jax
import jax.numpy as jnp
from jax import lax
from jax.experimental import pallas as pl
from jax.experimental.pallas import tpu as pltpu

F32 = jnp.float32
BF16 = jnp.bfloat16
MESH = pl.DeviceIdType.MESH

HEAD_DIM = 128
SUB_BLOCK = 128
DILATIONS = (1, 4, 16)
POOL_WINDOWS = (2, 4, 8, 16)
POOL_HALO = 16
ROPE_THETA = 10000.0
LN_EPS = 1e-5
ALPHA = 2.0 ** 0.25
ADAM_LR, ADAM_B1, ADAM_B2, ADAM_EPS, ADAM_WD, ADAM_STEP = 0.001, 0.9, 0.999, 1e-08, 0.01, 10
NEG = -1e30
N_CHIPS = 4
VMEM_LIMIT = 56 * 1024 * 1024
EPILOGUE_ROWS = 128


def _params(sem=None, vmem=VMEM_LIMIT):
    kw = {"vmem_limit_bytes": vmem}
    if sem is not None:
        kw["dimension_semantics"] = sem
    return pltpu.CompilerParams(**kw)


def _dot(a, b, contract):
    return lax.dot_general(a, b, (contract, ((), ())), preferred_element_type=F32)


ANY_SPEC = pl.BlockSpec(memory_space=pl.ANY)
HBM_SPEC = pl.BlockSpec(memory_space=pltpu.HBM)
SEM_SPEC = pl.BlockSpec(memory_space=pltpu.SEMAPHORE)
DATAFLOW = pltpu.SideEffectType.DATAFLOW_SIDE_EFFECTING


def _hbm(a):
    return pltpu.with_memory_space_constraint(a, pltpu.HBM)


NN = ((1,), (0,))
NT = ((1,), (1,))
TN = ((0,), (0,))


def _mm(name, grid, a, a_spec, b, b_spec, contract, extras, extra_specs, out_shape, out_specs,
        epilogue, acc_shape, acc_as_ref=False, run_after=(), scratch=(),
        semantics=("parallel", "parallel", "arbitrary")):
    nk = grid[2]
    n_ex = len(extras)
    n_in = 2 + n_ex + len(run_after)
    n_out = len(out_shape)
    n_scr = len(scratch)

    def body(*refs):
        a_ref, b_ref = refs[0], refs[1]
        ex = refs[2:2 + n_ex]
        outs = refs[n_in:n_in + n_out]
        scr = refs[n_in + n_out:n_in + n_out + n_scr]
        if nk == 1:
            epilogue(_dot(a_ref[...], b_ref[...], contract), ex, outs, *scr)
        else:
            acc = refs[n_in + n_out + n_scr]
            k = pl.program_id(2)

            @pl.when(k == 0)
            def _():
                acc[...] = jnp.zeros_like(acc)

            acc[...] += _dot(a_ref[...], b_ref[...], contract)

            @pl.when(k == nk - 1)
            def _():
                epilogue(acc if acc_as_ref else acc[...], ex, outs, *scr)

    acc_scratch = [pltpu.VMEM(acc_shape, F32)] if nk > 1 else []
    return pl.pallas_call(
        body, name=name, grid=grid,
        in_specs=[a_spec, b_spec, *extra_specs, *[ANY_SPEC] * len(run_after)], out_specs=out_specs,
        out_shape=out_shape, scratch_shapes=[*scratch, *acc_scratch],
        compiler_params=_params(semantics),
    )(a, b, *extras, *run_after)


def _stats_rows(rows, width):
    idx = lax.broadcasted_iota(jnp.int32, (8, width), 0)
    out = jnp.zeros((8, width), F32)
    for r, v in enumerate(rows):
        out = jnp.where(idx == r, jnp.broadcast_to(v, (8, width)), out)
    return out


def _layer_norm_fwd(z):
    mu = jnp.mean(z, axis=-1, keepdims=True)
    zc = z - mu
    var = jnp.mean(zc * zc, axis=-1, keepdims=True)
    rstd = lax.rsqrt(var + LN_EPS)
    return zc * rstd, rstd


def _layer_norm_bwd(dy, xhat, rstd, g):
    dxh = dy * g
    m1 = jnp.mean(dxh, axis=-1, keepdims=True)
    m2 = jnp.mean(dxh * xhat, axis=-1, keepdims=True)
    return rstd * (dxh - m1 - xhat * m2)


def _heads_scratch(rows, width):
    return pltpu.VMEM((width // HEAD_DIM, rows, HEAD_DIM), F32)


def _to_views(src_ref, view_refs, dtype):
    nh, rows, _ = src_ref.shape
    width = nh * HEAD_DIM
    for dil, view_ref in zip(DILATIONS[1:], view_refs):
        for r in range(dil):
            for hh in range(nh):
                c0 = r * width + hh * HEAD_DIM
                view_ref[:, c0:c0 + HEAD_DIM] = (
                    src_ref[hh, pl.ds(r, rows // dil, stride=dil), :].astype(dtype))


def _from_views(view_refs, dst_refs):
    nh, rows, _ = dst_refs[0].shape
    width = nh * HEAD_DIM
    for dil, view_ref, dst_ref in zip(DILATIONS[1:], view_refs, dst_refs):
        for r in range(dil):
            for hh in range(nh):
                c0 = r * width + hh * HEAD_DIM
                dst_ref[hh, pl.ds(r, rows // dil, stride=dil), :] = (
                    view_ref[:, c0:c0 + HEAD_DIM].astype(F32))


def _view_shape(rows, width, dil, parts=1):
    return (rows // dil, parts * dil * width)


def _in_proj(xb, w_in_st, cos_t, sin_t, aw):
    s, d = xb.shape
    n_sh = w_in_st.shape[2]
    tm, tn = min(s, 1024), aw
    per = n_sh // tn
    grid = (s // tm, (N_CHIPS * n_sh) // tn, 1)

    def epilogue(acc, ex, outs, scr):
        cos_ref, sin_ref = ex
        h_ref, v4_ref, v16_ref = outs
        seg = pl.program_id(1)

        heads = [slice(hh * HEAD_DIM, (hh + 1) * HEAD_DIM) for hh in range(tn // HEAD_DIM)]

        @pl.when(seg < 2)
        def _():
            cos, sin = cos_ref[...], sin_ref[...]
            for hh, sl in enumerate(heads):
                t = acc[:, sl]
                scr[hh] = t * cos + pltpu.roll(t, HEAD_DIM // 2, 1) * sin

        @pl.when(seg == 2)
        def _():
            for hh, sl in enumerate(heads):
                scr[hh] = acc[:, sl]

        @pl.when(seg < 3)
        def _():
            for hh, sl in enumerate(heads):
                h_ref[:, sl] = scr[hh].astype(BF16)
            _to_views(scr, (v4_ref, v16_ref), BF16)

        @pl.when(seg == 3)
        def _():
            h_ref[...] = acc.astype(BF16)

        @pl.when(seg >= 4)
        def _():
            h_ref[...] = jax.nn.sigmoid(acc).astype(BF16)

    def view_spec(dil):
        return pl.BlockSpec((tm // dil, dil * aw), lambda i, j, k: (i, jnp.minimum(j, 2)))

    return _mm(
        "in_proj", grid, xb, pl.BlockSpec((tm, d), lambda i, j, k: (i, 0)),
        w_in_st, pl.BlockSpec((None, d, tn), lambda i, j, k: (j // per, 0, j % per)), NN,
        [cos_t, sin_t], [pl.BlockSpec((tm, HEAD_DIM), lambda i, j, k: (i, 0))] * 2,
        [jax.ShapeDtypeStruct((s, N_CHIPS * n_sh), BF16)]
        + [jax.ShapeDtypeStruct(_view_shape(s, aw, dil, 3), BF16) for dil in DILATIONS[1:]],
        [pl.BlockSpec((tm, tn), lambda i, j, k: (i, j))] + [view_spec(dil) for dil in DILATIONS[1:]],
        epilogue, None, scratch=[_heads_scratch(tm, aw)],
        semantics=("parallel", "arbitrary", "arbitrary"))


def _band_masks(block_idx):
    qi = lax.broadcasted_iota(jnp.int32, (SUB_BLOCK, 2 * SUB_BLOCK), 0)
    kj = lax.broadcasted_iota(jnp.int32, (SUB_BLOCK, 2 * SUB_BLOCK), 1)
    first_key = jnp.where(block_idx > 0, 0, SUB_BLOCK)
    return (kj >= qi) & (kj <= qi + SUB_BLOCK) & (kj >= first_key)


def _attn_fwd(name, q, k, v, offs, cw, width):
    m = q.shape[0]
    nh = cw // HEAD_DIM
    grid = (width // cw, m // SUB_BLOCK)
    scale = HEAD_DIM ** -0.5

    def body(q_ref, k_ref, v_ref, o_ref, lse_ref, kprev, vprev):
        n = pl.program_id(1)
        valid = _band_masks(n)

        @pl.when(n == 0)
        def _():
            kprev[...] = jnp.zeros_like(kprev)
            vprev[...] = jnp.zeros_like(vprev)

        heads = [slice(hh * HEAD_DIM, (hh + 1) * HEAD_DIM) for hh in range(nh)]
        scs = [_dot(q_ref[:, sl], jnp.concatenate([kprev[:, sl], k_ref[:, sl]], axis=0), NT)
               for sl in heads]
        for hh, sl in enumerate(heads):
            vcat = jnp.concatenate([vprev[:, sl], v_ref[:, sl]], axis=0)
            sc = jnp.where(valid, scs[hh] * scale, NEG)
            mx = jnp.max(sc, axis=-1, keepdims=True)
            p = jnp.exp(sc - mx)
            l = jnp.sum(p, axis=-1, keepdims=True)
            o = _dot(p.astype(BF16), vcat, NN) / l
            o_ref[:, sl] = o.astype(BF16)
            lse_ref[:, sl] = jnp.broadcast_to(mx + jnp.log(l), (SUB_BLOCK, HEAD_DIM))
        kprev[...] = k_ref[...]
        vprev[...] = v_ref[...]

    def cur(off):
        return pl.BlockSpec((SUB_BLOCK, cw), lambda j, n: (n, j + off))

    return pl.pallas_call(
        body, name=name, grid=grid,
        in_specs=[cur(offs[0]), cur(offs[1]), cur(offs[2])], out_specs=[cur(0), cur(0)],
        out_shape=[jax.ShapeDtypeStruct((m, width), BF16), jax.ShapeDtypeStruct((m, width), F32)],
        scratch_shapes=[pltpu.VMEM((SUB_BLOCK, cw), BF16)] * 2,
        compiler_params=_params(("parallel", "arbitrary")),
    )(q, k, v)


def _view_spec(tm, aw, dil):
    return pl.BlockSpec((tm // dil, dil * aw), lambda i: (i, 0))


def _attn_combine(o1, l1, o_views, l_views):
    s, aw = o1.shape
    tm = min(s, 512)

    def body(o1_ref, l1_ref, o4_ref, o16_ref, l4_ref, l16_ref, o_ref, lse_ref, lse4_ref, lse16_ref,
             so4, so16, sl4, sl16, stot):
        _from_views((o4_ref, o16_ref), (so4, so16))
        _from_views((l4_ref, l16_ref), (sl4, sl16))
        for hh in range(aw // HEAD_DIM):
            sl = slice(hh * HEAD_DIM, (hh + 1) * HEAD_DIM)
            a, b, c = l1_ref[:, sl], sl4[hh], sl16[hh]
            mx = jnp.maximum(jnp.maximum(a, b), c)
            ea, eb, ec = jnp.exp(a - mx), jnp.exp(b - mx), jnp.exp(c - mx)
            tot = ea + eb + ec
            o = (ea * o1_ref[:, sl].astype(F32) + eb * so4[hh] + ec * so16[hh]) / tot
            o_ref[:, sl] = o.astype(BF16)
            lse_tot = mx + jnp.log(tot)
            stot[hh] = lse_tot
            lse_ref[:, sl] = lse_tot
        _to_views(stot, (lse4_ref, lse16_ref), F32)

    row = pl.BlockSpec((tm, aw), lambda i: (i, 0))
    views = [_view_spec(tm, aw, dil) for dil in DILATIONS[1:]]
    return pl.pallas_call(
        body, name="attn_combine", grid=(s // tm,), in_specs=[row, row, *views, *views],
        out_specs=[row, row, *views],
        out_shape=[jax.ShapeDtypeStruct((s, aw), BF16), jax.ShapeDtypeStruct((s, aw), F32)]
        + [jax.ShapeDtypeStruct(_view_shape(s, aw, dil), F32) for dil in DILATIONS[1:]],
        scratch_shapes=[_heads_scratch(tm, aw)] * 5,
        compiler_params=_params(("parallel",)),
    )(o1, l1, *o_views, *l_views)


def _pool_counts(tm, rows, pgw, row0):
    t = lax.broadcasted_iota(jnp.int32, (rows, len(POOL_WINDOWS) * pgw), 0) + row0
    col = lax.broadcasted_iota(jnp.int32, (rows, len(POOL_WINDOWS) * pgw), 1)
    w = jnp.full((rows, len(POOL_WINDOWS) * pgw), POOL_WINDOWS[0], jnp.int32)
    for g in range(1, len(POOL_WINDOWS)):
        w = jnp.where(col >= g * pgw, POOL_WINDOWS[g], w)
    return jnp.minimum(t + 1, w).astype(F32)


def _window_sums(xs, direction, pgw):
    rows = xs.shape[0]
    acc = xs
    out = None
    col = lax.broadcasted_iota(jnp.int32, xs.shape, 1)
    for g, w in enumerate(POOL_WINDOWS):
        sh = w // 2
        acc = acc + pltpu.roll(acc, sh if direction > 0 else rows - sh, 0)
        out = acc if out is None else jnp.where(col >= g * pgw, acc, out)
    return out


def _pool_fwd(h, wp, scale, aw):
    s = h.shape[0]
    pw_ = aw
    pgw = pw_ // len(POOL_WINDOWS)
    tm = min(s, 512)
    hb = tm // POOL_HALO

    def body(u_ref, halo_ref, wp_ref, sc_ref, p_ref, pw_ref, y_ref):
        i = pl.program_id(0)
        u = u_ref[...].astype(F32)
        halo = halo_ref[...].astype(F32) * jnp.where(i > 0, 1.0, 0.0)
        xs = jnp.concatenate([halo, u], axis=0)
        sums = _window_sums(xs, +1, pgw)[POOL_HALO:]
        p = (sums / _pool_counts(tm, tm, pgw, i * tm) - u).astype(BF16)
        p_ref[...] = p
        sc = sc_ref[...]
        for g in range(len(POOL_WINDOWS)):
            sl = slice(g * pgw, (g + 1) * pgw)
            pw = _dot(p[:, sl], wp_ref[g], NN)
            pw_ref[:, sl] = pw.astype(BF16)
            y_ref[:, sl] = (pw * sc[:, sl]).astype(BF16)

    out = jax.ShapeDtypeStruct((s, pw_), BF16)
    row = pl.BlockSpec((tm, pw_), lambda i: (i, 0))
    return pl.pallas_call(
        body, name="pool_fwd", grid=(s // tm,),
        in_specs=[pl.BlockSpec((tm, pw_), lambda i: (i, 3)),
                  pl.BlockSpec((POOL_HALO, pw_), lambda i: (jnp.maximum(i * hb - 1, 0), 3)),
                  pl.BlockSpec(wp.shape, lambda i: (0, 0, 0)),
                  pl.BlockSpec((1, pw_), lambda i: (0, 0))],
        out_specs=[row, row, row], out_shape=[out, out, out],
        compiler_params=_params(("parallel",)),
    )(h, h, wp, scale)


def _branch_merge(o_attn, y, wba_st, wbp_st, h, aw, d):
    s = o_attn.shape[0]
    tn = wba_st.shape[2]
    tm = min(s, 1024)
    ga0 = 4 * aw // tn
    gp0 = (4 * aw + d) // tn

    def body(o_ref, y_ref, wa_ref, wp_ref, sga_ref, sgp_ref, ya_ref, yp_ref, mg_ref):
        ya = _dot(o_ref[...], wa_ref[...], NN)
        yp = _dot(y_ref[...], wp_ref[...], NN)
        ya_ref[...] = ya.astype(BF16)
        yp_ref[...] = yp.astype(BF16)
        mg_ref[...] = (sga_ref[...].astype(F32) * ya + sgp_ref[...].astype(F32) * yp).astype(BF16)

    out = jax.ShapeDtypeStruct((s, d), BF16)
    blk = pl.BlockSpec((tm, tn), lambda i, j: (i, j))
    return pl.pallas_call(
        body, name="branch_merge", grid=(s // tm, N_CHIPS),
        in_specs=[pl.BlockSpec((tm, aw), lambda i, j: (i, 0)),
                  pl.BlockSpec((tm, aw), lambda i, j: (i, 0)),
                  pl.BlockSpec((None, aw, tn), lambda i, j: (j, 0, 0)),
                  pl.BlockSpec((None, aw, tn), lambda i, j: (j, 0, 0)),
                  pl.BlockSpec((tm, tn), lambda i, j: (i, j + ga0)),
                  pl.BlockSpec((tm, tn), lambda i, j: (i, j + gp0))],
        out_specs=[blk, blk, blk], out_shape=[out, out, out],
        compiler_params=_params(("parallel", "parallel")),
    )(o_attn, y, wba_st, wbp_st, h, h)


def _mix_norm(merged, w_out, x, g1, b1):
    s, d = x.shape
    tm = min(s, 256)

    def epilogue(acc, ex, outs):
        x_ref, g_ref, b_ref = ex
        xh_ref, rs_ref, xb_ref = outs
        xhat, rstd = _layer_norm_fwd(ALPHA * x_ref[...] + acc)
        xh_ref[...] = xhat
        rs_ref[...] = rstd
        xb_ref[...] = (xhat * g_ref[...] + b_ref[...]).astype(BF16)

    row = pl.BlockSpec((tm, d), lambda i, j, k: (i, 0))
    vec = pl.BlockSpec((1, d), lambda i, j, k: (0, 0))
    return _mm(
        "mix_norm", (s // tm, 1, 1), merged, row, w_out, pl.BlockSpec((d, d), lambda i, j, k: (0, 0)),
        NN, [x, g1, b1], [row, vec, vec],
        [jax.ShapeDtypeStruct((s, d), F32), jax.ShapeDtypeStruct((s, 1), F32),
         jax.ShapeDtypeStruct((s, d), BF16)],
        [row, pl.BlockSpec((tm, 1), lambda i, j, k: (i, 0)), row], epilogue, None)


def _ff_up(x1b, w1_st):
    s, d = x1b.shape
    n_sh = w1_st.shape[2]
    tm, tn = min(s, 1024), min(n_sh, 1024)
    per = n_sh // tn

    def epilogue(acc, ex, outs):
        r = jnp.maximum(acc, 0.0)
        outs[0][...] = (r * r).astype(BF16)

    return _mm(
        "ff_up", (s // tm, N_CHIPS * per, 1), x1b, pl.BlockSpec((tm, d), lambda i, j, k: (i, 0)),
        w1_st, pl.BlockSpec((None, d, tn), lambda i, j, k: (j // per, 0, j % per)), NN, [], [],
        [jax.ShapeDtypeStruct((s, N_CHIPS * n_sh), BF16)],
        [pl.BlockSpec((tm, tn), lambda i, j, k: (i, j))], epilogue, None)[0]


def _ff_down_loss(r, w2, xhat1, g1, b1, g2, b2, target):
    s, d = xhat1.shape
    dff = r.shape[1]
    tm, tk = min(s, 512), min(dff, 1024)
    ch = min(tm, EPILOGUE_ROWS)

    def epilogue(acc_ref, ex, outs):
        xh1_ref, g1_ref, b1_ref, g2_ref, b2_ref, t_ref = ex
        dz_ref, dzb_ref, st_ref = outs
        g1v, b1v, g2v, b2v = g1_ref[...], b1_ref[...], g2_ref[...], b2_ref[...]
        dg = db = loss = None
        for c in range(tm // ch):
            rows = slice(c * ch, (c + 1) * ch)
            x1 = xh1_ref[rows, :] * g1v + b1v
            xhat2, rstd2 = _layer_norm_fwd(ALPHA * x1 + acc_ref[rows, :])
            err = xhat2 * g2v + b2v - t_ref[rows, :]
            dy = err * (1.0 / d)
            dz = _layer_norm_bwd(dy, xhat2, rstd2, g2v)
            dz_ref[rows, :] = dz
            dzb_ref[rows, :] = dz.astype(BF16)
            parts = (jnp.sum(dy * xhat2, axis=0, keepdims=True), jnp.sum(dy, axis=0, keepdims=True),
                     jnp.sum(jnp.sum(err * err, axis=-1, keepdims=True), axis=0, keepdims=True))
            dg, db, loss = parts if c == 0 else (dg + parts[0], db + parts[1], loss + parts[2])
        st_ref[...] = _stats_rows([dg, db, jnp.broadcast_to((0.5 / d) * loss, (1, d))], d)

    row = pl.BlockSpec((tm, d), lambda i, j, k: (i, 0))
    vec = pl.BlockSpec((1, d), lambda i, j, k: (0, 0))
    return _mm(
        "ff_down_loss", (s // tm, 1, dff // tk), r, pl.BlockSpec((tm, tk), lambda i, j, k: (i, k)),
        w2, pl.BlockSpec((tk, d), lambda i, j, k: (k, 0)), NN,
        [xhat1, g1, b1, g2, b2, target], [row, vec, vec, vec, vec, row],
        [jax.ShapeDtypeStruct((s, d), F32), jax.ShapeDtypeStruct((s, d), BF16),
         jax.ShapeDtypeStruct((s // tm, 8, d), F32)],
        [row, row, pl.BlockSpec((None, 8, d), lambda i, j, k: (i, 0, 0))], epilogue, (tm, d),
        acc_as_ref=True)


def _ff_down_bwd(dz2b, w2, r):
    s, d = dz2b.shape
    dff = r.shape[1]
    tm, tn = min(s, 1024), min(dff, 1024)

    def epilogue(acc, ex, outs):
        outs[0][...] = (acc * (2.0 * jnp.sqrt(ex[0][...].astype(F32)))).astype(BF16)

    blk = pl.BlockSpec((tm, tn), lambda i, j, k: (i, j))
    return _mm(
        "ff_down_bwd", (s // tm, dff // tn, 1), dz2b, pl.BlockSpec((tm, d), lambda i, j, k: (i, 0)),
        w2, pl.BlockSpec((tn, d), lambda i, j, k: (j, 0)), NT, [r], [blk],
        [jax.ShapeDtypeStruct((s, dff), BF16)], [blk], epilogue, None)[0]


def _wgrad(name, a, g, n_sh):
    s, rows = a.shape
    cols = g.shape[1]
    tm, tn, tk = min(rows, 2048), min(cols, 1024), min(s, 1024)
    if tn >= n_sh:
        span = tn // n_sh
        out_spec = pl.BlockSpec((span, tm, n_sh), lambda i, j, k: (j, i, 0))

        def epilogue(acc_ref, ex, outs):
            for sh in range(span):
                outs[0][sh] = acc_ref[:, sh * n_sh:(sh + 1) * n_sh].astype(BF16)
    else:
        per = n_sh // tn
        out_spec = pl.BlockSpec((None, tm, tn), lambda i, j, k: (j // per, i, j % per))

        def epilogue(acc_ref, ex, outs):
            outs[0][...] = acc_ref[...].astype(BF16)

    return _mm(
        name, (rows // tm, cols // tn, s // tk), a, pl.BlockSpec((tk, tm), lambda i, j, k: (k, i)),
        g, pl.BlockSpec((tk, tn), lambda i, j, k: (k, j)), TN, [], [],
        [jax.ShapeDtypeStruct((cols // n_sh, rows, n_sh), BF16)], [out_spec], epilogue,
        (tm, tn), acc_as_ref=True)[0]


def _ff_up_bwd(da, w1_st, dz2, xhat1, rstd1, g1, run_after):
    s, d = dz2.shape
    n_sh = w1_st.shape[2]
    tm, tk = min(s, 512), min(n_sh, 1024)
    per = n_sh // tk
    ch = min(tm, EPILOGUE_ROWS)

    def epilogue(acc_ref, ex, outs):
        dz2_ref, xh_ref, rs_ref, g_ref = ex
        dz_ref, dzb_ref, st_ref = outs
        gv = g_ref[...]
        dg = db = None
        for c in range(tm // ch):
            rows = slice(c * ch, (c + 1) * ch)
            dx1 = ALPHA * dz2_ref[rows, :] + acc_ref[rows, :]
            xhat = xh_ref[rows, :]
            dz = _layer_norm_bwd(dx1, xhat, rs_ref[rows, :], gv)
            dz_ref[rows, :] = dz
            dzb_ref[rows, :] = dz.astype(BF16)
            parts = (jnp.sum(dx1 * xhat, axis=0, keepdims=True), jnp.sum(dx1, axis=0, keepdims=True))
            dg, db = parts if c == 0 else (dg + parts[0], db + parts[1])
        st_ref[...] = _stats_rows([dg, db], d)

    row = pl.BlockSpec((tm, d), lambda i, j, k: (i, 0))
    return _mm(
        "ff_up_bwd", (s // tm, 1, N_CHIPS * per), da, pl.BlockSpec((tm, tk), lambda i, j, k: (i, k)),
        w1_st, pl.BlockSpec((None, d, tk), lambda i, j, k: (k // per, 0, k % per)), NT,
        [dz2, xhat1, rstd1, g1],
        [row, row, pl.BlockSpec((tm, 1), lambda i, j, k: (i, 0)), pl.BlockSpec((1, d), lambda i, j, k: (0, 0))],
        [jax.ShapeDtypeStruct((s, d), F32), jax.ShapeDtypeStruct((s, d), BF16),
         jax.ShapeDtypeStruct((s // tm, 8, d), F32)],
        [row, row, pl.BlockSpec((None, 8, d), lambda i, j, k: (i, 0, 0))], epilogue, (tm, d),
        acc_as_ref=True, run_after=run_after)


def _mix_bwd(dz1b, w_out, h, ya, yp, aw):
    s, d = dz1b.shape
    tm = min(s, 256)
    gblk = 4 * aw // d

    def epilogue(acc, ex, outs):
        sga_ref, sgp_ref, ya_ref, yp_ref = ex
        dya_ref, dyp_ref, dg_ref = outs
        sga, sgp = sga_ref[...].astype(F32), sgp_ref[...].astype(F32)
        dya_ref[...] = (acc * sga).astype(BF16)
        dyp_ref[...] = (acc * sgp).astype(BF16)
        dg_ref[:, :d] = (acc * ya_ref[...].astype(F32) * (sga * (1.0 - sga))).astype(BF16)
        dg_ref[:, d:] = (acc * yp_ref[...].astype(F32) * (sgp * (1.0 - sgp))).astype(BF16)

    row = pl.BlockSpec((tm, d), lambda i, j, k: (i, 0))
    return _mm(
        "mix_bwd", (s // tm, 1, 1), dz1b, row, w_out, pl.BlockSpec((d, d), lambda i, j, k: (0, 0)), NT,
        [h, h, ya, yp],
        [pl.BlockSpec((tm, d), lambda i, j, k: (i, gblk)),
         pl.BlockSpec((tm, d), lambda i, j, k: (i, gblk + 1)), row, row],
        [jax.ShapeDtypeStruct((s, d), BF16), jax.ShapeDtypeStruct((s, d), BF16),
         jax.ShapeDtypeStruct((s, 2 * d), BF16)],
        [row, row, pl.BlockSpec((tm, 2 * d), lambda i, j, k: (i, 0))], epilogue, None)


def _branch_in_bwd(name, tm, dyb, wb_st, epilogue, extras, extra_specs, out_shape, out_specs,
                   scratch=()):
    s = dyb.shape[0]
    aw, tk = wb_st.shape[1], wb_st.shape[2]
    return _mm(
        name, (s // tm, 1, N_CHIPS), dyb, pl.BlockSpec((tm, tk), lambda i, j, k: (i, k)),
        wb_st, pl.BlockSpec((None, aw, tk), lambda i, j, k: (k, 0, 0)), NT,
        extras, extra_specs, out_shape, out_specs, epilogue, (tm, aw), acc_as_ref=True,
        scratch=scratch)


def _attn_out_bwd(dya, wba_st, o_attn):
    s, aw = o_attn.shape
    tm = min(s, 512)

    def epilogue(acc_ref, ex, outs, sdo, sdl):
        do_ref, dl_ref, do4_ref, do16_ref, dl4_ref, dl16_ref = outs
        for hh in range(aw // HEAD_DIM):
            sl = slice(hh * HEAD_DIM, (hh + 1) * HEAD_DIM)
            do = acc_ref[:, sl]
            dl = jnp.broadcast_to(
                jnp.sum(do * ex[0][:, sl].astype(F32), axis=-1, keepdims=True), (tm, HEAD_DIM))
            sdo[hh] = do
            sdl[hh] = dl
            do_ref[:, sl] = do.astype(BF16)
            dl_ref[:, sl] = dl
        _to_views(sdo, (do4_ref, do16_ref), BF16)
        _to_views(sdl, (dl4_ref, dl16_ref), F32)

    row = pl.BlockSpec((tm, aw), lambda i, j, k: (i, 0))
    views = [pl.BlockSpec((tm // dil, dil * aw), lambda i, j, k: (i, 0)) for dil in DILATIONS[1:]]
    view_shapes = [_view_shape(s, aw, dil) for dil in DILATIONS[1:]]
    return _branch_in_bwd(
        "attn_out_bwd", tm, dya, wba_st, epilogue, [o_attn], [row],
        [jax.ShapeDtypeStruct((s, aw), BF16), jax.ShapeDtypeStruct((s, aw), F32)]
        + [jax.ShapeDtypeStruct(sh, BF16) for sh in view_shapes]
        + [jax.ShapeDtypeStruct(sh, F32) for sh in view_shapes],
        [row, row, *views, *views], scratch=[_heads_scratch(tm, aw)] * 2)


def _pool_out_bwd(dyp, wbp_st, pw, scale):
    s, pw_ = pw.shape
    tm = min(s, 1024)

    def epilogue(acc_ref, ex, outs):
        pw_ref, sc_ref = ex
        dpw_ref, st_ref = outs
        acc = acc_ref[...]
        dpw_ref[...] = (acc * sc_ref[...]).astype(BF16)
        st_ref[...] = _stats_rows([jnp.sum(acc * pw_ref[...].astype(F32), axis=0, keepdims=True)], pw_)

    row = pl.BlockSpec((tm, pw_), lambda i, j, k: (i, 0))
    return _branch_in_bwd(
        "pool_out_bwd", tm, dyp, wbp_st, epilogue, [pw, scale],
        [row, pl.BlockSpec((1, pw_), lambda i, j, k: (0, 0))],
        [jax.ShapeDtypeStruct((s, pw_), BF16), jax.ShapeDtypeStruct((s // tm, 8, pw_), F32)],
        [row, pl.BlockSpec((None, 8, pw_), lambda i, j, k: (i, 0, 0))])


def _pool_bwd(dpw, p, wp):
    s, pw_ = p.shape
    ng = len(POOL_WINDOWS)
    pgw = pw_ // ng
    tm = min(s, 512)
    hb = tm // POOL_HALO
    nblk = s // tm

    def body(dpw_ref, nxt_ref, p_ref, wp_ref, dwp_ref, du_ref):
        i = pl.program_id(0)
        nxt = (nxt_ref[...].astype(F32) * jnp.where(i < nblk - 1, 1.0, 0.0)).astype(BF16)
        dpw_all = jnp.concatenate([dpw_ref[...], nxt], axis=0)

        @pl.when(i == 0)
        def _():
            dwp_ref[...] = jnp.zeros_like(dwp_ref)

        dps = []
        for g in range(ng):
            sl = slice(g * pgw, (g + 1) * pgw)
            dwp_ref[g] += _dot(p_ref[:, sl], dpw_ref[:, sl], TN)
            dps.append(_dot(dpw_all[:, sl], wp_ref[g], NT))
        dp = jnp.concatenate(dps, axis=1)
        dpn = dp / _pool_counts(tm, tm + POOL_HALO, pgw, i * tm)
        du_ref[...] = (_window_sums(dpn, -1, pgw)[:tm] - dp[:tm]).astype(BF16)

    row = pl.BlockSpec((tm, pw_), lambda i: (i, 0))
    full = pl.BlockSpec((ng, pgw, pgw), lambda i: (0, 0, 0))
    return pl.pallas_call(
        body, name="pool_bwd", grid=(nblk,),
        in_specs=[row, pl.BlockSpec((POOL_HALO, pw_), lambda i: (jnp.minimum((i + 1) * hb, s // POOL_HALO - 1), 0)),
                  row, full],
        out_specs=[full, row],
        out_shape=[jax.ShapeDtypeStruct((ng, pgw, pgw), F32), jax.ShapeDtypeStruct((s, pw_), BF16)],
        compiler_params=_params(("arbitrary",)),
    )(dpw, dpw, p, wp)


def _attn_bwd(name, q, k, v, do, lse, delta, offs, cw, width):
    m = do.shape[0]
    nh = cw // HEAD_DIM
    nblk = m // SUB_BLOCK
    grid = (width // cw, nblk + 1)
    scale = HEAD_DIM ** -0.5

    def body(q_ref, k_ref, v_ref, do_ref, lse_ref, dl_ref, dq_ref, dk_ref, dv_ref,
             kprev, vprev, dk_carry, dv_carry):
        n = pl.program_id(1)

        @pl.when(n == 0)
        def _():
            for ref in (kprev, vprev, dk_carry, dv_carry):
                ref[...] = jnp.zeros_like(ref)

        qi = lax.broadcasted_iota(jnp.int32, (SUB_BLOCK, 2 * SUB_BLOCK), 0)
        kj = lax.broadcasted_iota(jnp.int32, (SUB_BLOCK, 2 * SUB_BLOCK), 1)
        first_key = jnp.where(n == 0, SUB_BLOCK, jnp.where(n == nblk, 4 * SUB_BLOCK, 0))
        valid = (kj >= qi) & (kj <= qi + SUB_BLOCK) & (kj >= first_key)
        heads = [slice(hh * HEAD_DIM, (hh + 1) * HEAD_DIM) for hh in range(nh)]
        kcats = [jnp.concatenate([kprev[:, sl], k_ref[:, sl]], axis=0) for sl in heads]
        scs = [_dot(q_ref[:, sl], kcats[hh], NT) for hh, sl in enumerate(heads)]
        dps = [_dot(do_ref[:, sl], jnp.concatenate([vprev[:, sl], v_ref[:, sl]], axis=0), NT)
               for sl in heads]
        dqs = []
        for hh, sl in enumerate(heads):
            lse2 = jnp.concatenate([lse_ref[:, sl]] * 2, axis=1)
            dl2 = jnp.concatenate([dl_ref[:, sl]] * 2, axis=1)
            p = jnp.where(valid, jnp.exp(jnp.where(valid, scs[hh] * scale, NEG) - lse2), 0.0)
            ds = (p * (dps[hh] - dl2)).astype(BF16)
            dqs.append((_dot(ds, kcats[hh], NN) * scale).astype(BF16))
            dk2 = _dot(ds, q_ref[:, sl], TN) * scale
            dv2 = _dot(p.astype(BF16), do_ref[:, sl], TN)
            dk_ref[:, sl] = (dk_carry[:, sl] + dk2[:SUB_BLOCK]).astype(BF16)
            dv_ref[:, sl] = (dv_carry[:, sl] + dv2[:SUB_BLOCK]).astype(BF16)
            dk_carry[:, sl] = dk2[SUB_BLOCK:]
            dv_carry[:, sl] = dv2[SUB_BLOCK:]

        @pl.when(n < nblk)
        def _():
            for hh, sl in enumerate(heads):
                dq_ref[:, sl] = dqs[hh]

        kprev[...] = k_ref[...]
        vprev[...] = v_ref[...]

    def cur(off):
        return pl.BlockSpec((SUB_BLOCK, cw), lambda j, n: (jnp.minimum(n, nblk - 1), j + off))

    lagged = pl.BlockSpec((SUB_BLOCK, cw), lambda j, n: (jnp.maximum(n - 1, 0), j))
    out = jax.ShapeDtypeStruct((m, width), BF16)
    return pl.pallas_call(
        body, name=name, grid=grid,
        in_specs=[cur(offs[0]), cur(offs[1]), cur(offs[2]), cur(0), cur(0), cur(0)],
        out_specs=[cur(0), lagged, lagged], out_shape=[out, out, out],
        scratch_shapes=[pltpu.VMEM((SUB_BLOCK, cw), BF16)] * 2 + [pltpu.VMEM((SUB_BLOCK, cw), F32)] * 2,
        compiler_params=_params(("parallel", "arbitrary")),
    )(q, k, v, do, lse, delta)


def _qkvu_grad(d1, d4, d16, du, cos_t, sin_t):
    s, aw = du.shape
    tm = min(s, 512)

    def body(*refs):
        nat, v4, v16 = refs[0:3], refs[3:6], refs[6:9]
        cos_ref, sin_ref, du_ref, out_ref, s4, s16 = refs[9:]
        cos, sin = cos_ref[...], sin_ref[...]
        for part in range(3):
            _from_views((v4[part], v16[part]), (s4, s16))
            for hh in range(aw // HEAD_DIM):
                sl = slice(hh * HEAD_DIM, (hh + 1) * HEAD_DIM)
                t = nat[part][:, sl].astype(F32) + s4[hh] + s16[hh]
                if part < 2:
                    t = t * cos - pltpu.roll(t, HEAD_DIM // 2, 1) * sin
                out_ref[:, part * aw + hh * HEAD_DIM:part * aw + (hh + 1) * HEAD_DIM] = t.astype(BF16)
        out_ref[:, 3 * aw:] = du_ref[...]

    row = pl.BlockSpec((tm, aw), lambda i: (i, 0))
    tab = pl.BlockSpec((tm, HEAD_DIM), lambda i: (i, 0))
    return pl.pallas_call(
        body, name="qkvu_grad", grid=(s // tm,),
        in_specs=[row] * 3 + [_view_spec(tm, aw, 4)] * 3 + [_view_spec(tm, aw, 16)] * 3 + [tab, tab, row],
        out_specs=pl.BlockSpec((tm, 4 * aw), lambda i: (i, 0)),
        out_shape=jax.ShapeDtypeStruct((s, 4 * aw), BF16),
        scratch_shapes=[_heads_scratch(tm, aw)] * 2,
        compiler_params=_params(("parallel",)),
    )(*d1, *d4, *d16, cos_t, sin_t, du)


def _in_proj_bwd_x(name, dh, w_in_st, shard0, base, scale_base, run_after=()):
    s, kdim = dh.shape
    d, n_sh = w_in_st.shape[1], w_in_st.shape[2]
    tm, tk = min(s, 512), min(n_sh, 2048)
    per = n_sh // tk

    ch = min(tm, 2 * EPILOGUE_ROWS)

    def epilogue(acc_ref, ex, outs):
        for c in range(tm // ch):
            rows = slice(c * ch, (c + 1) * ch)
            outs[0][rows, :] = scale_base * ex[0][rows, :] + acc_ref[rows, :]

    row = pl.BlockSpec((tm, d), lambda i, j, k: (i, 0))
    return _mm(
        name, (s // tm, 1, kdim // tk), dh, pl.BlockSpec((tm, tk), lambda i, j, k: (i, k)),
        w_in_st, pl.BlockSpec((None, d, tk), lambda i, j, k: (shard0 + k // per, 0, k % per)), NT,
        [base], [row], [jax.ShapeDtypeStruct((s, d), F32)], [row], epilogue, (tm, d),
        acc_as_ref=True, run_after=run_after)[0]


def _chip_peers():
    x, y, c = lax.axis_index("x"), lax.axis_index("y"), lax.axis_index("c")
    return x, y, c, [(1 - x, y), (x, 1 - y), (1 - x, 1 - y)]


GATHER, SCATTER, SIBLING = "gather", "scatter", "sibling"


def _exchange_peers(mode):
    x, y, c, chips = _chip_peers()
    if mode == SIBLING:
        return x, y, c, [(x, y, 1 - c)]
    return x, y, c, [(px, py, c) for px, py in chips]


def _exchange_descriptor(mode, src, land, send, recv, p, peer, me, arriving):
    pid = 2 * peer[0] + peer[1]
    if mode == GATHER:
        src_ref, dst_ref = src, land.at[pid if arriving else me]
    elif mode == SCATTER:
        src_ref, dst_ref = src.at[pid], land.at[p]
    else:
        src_ref, dst_ref = src, land
    return pltpu.make_async_remote_copy(
        src_ref=src_ref, dst_ref=dst_ref, send_sem=send.at[p], recv_sem=recv.at[p],
        device_id=peer, device_id_type=MESH)


def _exchange_start(name, mode, srcs, land_shapes):
    n = len(srcs)
    lands = [_hbm(lax.empty(shape, src.dtype)) for shape, src in zip(land_shapes, srcs)]

    def body(*refs):
        src_refs, land_refs = refs[:n], refs[n:2 * n]
        sends, recvs = refs[2 * n:3 * n], refs[3 * n:4 * n]
        token = refs[6 * n]
        x, y, c, peers = _exchange_peers(mode)
        me = 2 * x + y
        for w in range(n):
            for p, peer in enumerate(peers):
                _exchange_descriptor(mode, src_refs[w], land_refs[w], sends[w], recvs[w], p, peer,
                                     me, arriving=False).start()
        token[...] = jnp.zeros_like(token)

    sem = pltpu.SemaphoreType.DMA((3,))
    outs = pl.pallas_call(
        body, name=name, in_specs=[HBM_SPEC] * (2 * n),
        out_specs=[SEM_SPEC] * (2 * n) + [HBM_SPEC] * (2 * n) + [pl.BlockSpec(memory_space=pltpu.VMEM)],
        out_shape=[sem] * (2 * n) + [pltpu.HBM(a.shape, a.dtype) for a in (*srcs, *lands)]
        + [jax.ShapeDtypeStruct((8, 128), F32)],
        input_output_aliases={i: 2 * n + i for i in range(2 * n)},
        compiler_params=pltpu.CompilerParams(has_side_effects=DATAFLOW),
    )(*[_hbm(a) for a in srcs], *lands)
    return {"send": outs[:n], "recv": outs[n:2 * n], "src": outs[2 * n:3 * n],
            "land": outs[3 * n:4 * n], "token": outs[4 * n]}


def _exchange_wait(name, mode, started, which, after):
    m = len(which)

    def body(*refs):
        src_refs, land_refs = refs[:m], refs[m:2 * m]
        sends, recvs = refs[2 * m:3 * m], refs[3 * m:4 * m]
        x, y, c, peers = _exchange_peers(mode)
        me = 2 * x + y
        for w in range(m):
            for p, peer in enumerate(peers):
                _exchange_descriptor(mode, src_refs[w], land_refs[w], sends[w], recvs[w], p, peer,
                                     me, arriving=False).wait_send()
                _exchange_descriptor(mode, src_refs[w], land_refs[w], sends[w], recvs[w], p, peer,
                                     me, arriving=True).wait_recv()

    pick = lambda key: [started[key][w] for w in which]
    bufs = pick("src") + pick("land")
    outs = pl.pallas_call(
        body, name=name, in_specs=[HBM_SPEC] * (2 * m) + [SEM_SPEC] * (2 * m) + [ANY_SPEC],
        out_specs=[HBM_SPEC] * (2 * m), out_shape=[pltpu.HBM(a.shape, a.dtype) for a in bufs],
        input_output_aliases={i: i for i in range(2 * m)},
        compiler_params=pltpu.CompilerParams(has_side_effects=DATAFLOW),
    )(*bufs, *pick("send"), *pick("recv"), after)
    return outs[:m], outs[m:]


def _place_own(name, shard, land, me):
    r, c = shard.shape
    tm = min(r, 512)

    def body(me_ref, shard_ref, land_ref, out_ref):
        out_ref[...] = shard_ref[...]

    return pl.pallas_call(
        body, name=name,
        grid_spec=pltpu.PrefetchScalarGridSpec(
            num_scalar_prefetch=1, grid=(r // tm,),
            in_specs=[pl.BlockSpec((tm, c), lambda i, me_ref: (i, 0)), ANY_SPEC],
            out_specs=pl.BlockSpec((None, tm, c), lambda i, me_ref: (me_ref[0], i, 0))),
        out_shape=jax.ShapeDtypeStruct(land.shape, land.dtype), input_output_aliases={2: 0},
        compiler_params=_params(("arbitrary",)),
    )(me, shard, land)


def _sum_slabs(name, grads, land, me):
    _, r, c = grads.shape
    tm = min(r, 256)

    def body(me_ref, own_ref, land_ref, out_ref):
        acc = own_ref[...].astype(F32)
        for p in range(3):
            acc = acc + land_ref[p].astype(F32)
        out_ref[...] = acc

    return pl.pallas_call(
        body, name=name,
        grid_spec=pltpu.PrefetchScalarGridSpec(
            num_scalar_prefetch=1, grid=(r // tm,),
            in_specs=[pl.BlockSpec((None, tm, c), lambda i, me_ref: (me_ref[0], i, 0)),
                      pl.BlockSpec((3, tm, c), lambda i, me_ref: (0, i, 0))],
            out_specs=pl.BlockSpec((tm, c), lambda i, me_ref: (i, 0))),
        out_shape=jax.ShapeDtypeStruct((r, c), F32), compiler_params=_params(("parallel",)),
    )(me, grads, land)


def _allreduce_stats(stats):
    n = len(stats)

    def body(*refs):
        ins, outs = refs[:n], refs[n:2 * n]
        mine, gath = refs[2 * n:3 * n], refs[3 * n:4 * n]
        send, recv = refs[4 * n:]
        x, y, c = lax.axis_index("x"), lax.axis_index("y"), lax.axis_index("c")
        me = 4 * x + 2 * y + c
        flips = [(bx, by, bc) for bx in (0, 1) for by in (0, 1) for bc in (0, 1)][1:]

        def peer(f):
            return (x + f[0] * (1 - 2 * x), y + f[1] * (1 - 2 * y), c + f[2] * (1 - 2 * c))

        copies = []
        for t in range(n):
            tot = ins[t][0]
            for b in range(1, ins[t].shape[0]):
                tot = tot + ins[t][b]
            mine[t][...] = tot
            gath[t][me] = tot
            for k, f in enumerate(flips):
                cp = pltpu.make_async_remote_copy(
                    src_ref=mine[t], dst_ref=gath[t].at[me], send_sem=send.at[t, k],
                    recv_sem=recv.at[t, k], device_id=peer(f), device_id_type=MESH)
                cp.start()
                copies.append(cp)
        for t in range(n):
            for k, f in enumerate(flips):
                px, py, pc = peer(f)
                pltpu.make_async_remote_copy(
                    src_ref=mine[t], dst_ref=gath[t].at[4 * px + 2 * py + pc], send_sem=send.at[t, k],
                    recv_sem=recv.at[t, k], device_id=(px, py, pc), device_id_type=MESH).wait_recv()
        for cp in copies:
            cp.wait_send()
        for t in range(n):
            tot = gath[t][0]
            for dev in range(1, 8):
                tot = tot + gath[t][dev]
            outs[t][...] = tot

    vm = pl.BlockSpec(memory_space=pltpu.VMEM)
    return pl.pallas_call(
        body, name="allreduce_stats", in_specs=[vm] * n, out_specs=[vm] * n,
        out_shape=[jax.ShapeDtypeStruct(s.shape[1:], F32) for s in stats],
        scratch_shapes=[pltpu.VMEM(s.shape[1:], F32) for s in stats]
        + [pltpu.VMEM((8, *s.shape[1:]), F32) for s in stats]
        + [pltpu.SemaphoreType.DMA((n, 7)), pltpu.SemaphoreType.DMA((n, 7))],
    )(*stats)


def _adamw(name, w, m, v, g_parts):
    r, c = w.shape
    tm = min(r, 128)
    n_g = len(g_parts)

    def body(*refs):
        w_ref, m_ref, v_ref = refs[:3]
        g_refs = refs[3:3 + n_g]
        g_out, d_out, m_out, v_out = refs[3 + n_g:]
        g = g_refs[0][...]
        for gr in g_refs[1:]:
            g = g + gr[...]
        m_new = ADAM_B1 * m_ref[...] + (1.0 - ADAM_B1) * g
        v_new = ADAM_B2 * v_ref[...] + (1.0 - ADAM_B2) * (g * g)
        m_hat = m_new / (1.0 - ADAM_B1 ** ADAM_STEP)
        v_hat = v_new / (1.0 - ADAM_B2 ** ADAM_STEP)
        g_out[...] = g
        d_out[...] = -ADAM_LR * (m_hat / (jnp.sqrt(v_hat) + ADAM_EPS) + ADAM_WD * w_ref[...])
        m_out[...] = m_new
        v_out[...] = v_new

    blk = pl.BlockSpec((tm, c), lambda i: (i, 0))
    out = jax.ShapeDtypeStruct((r, c), F32)
    return pl.pallas_call(
        body, name=name, grid=(r // tm,), in_specs=[blk] * (3 + n_g), out_specs=[blk] * 4,
        out_shape=[out] * 4, compiler_params=_params(("parallel",)),
    )(w, m, v, *g_parts)


def _rope_tables(positions):
    half = HEAD_DIM // 2
    inv_freq = ROPE_THETA ** (-jnp.arange(half, dtype=F32) / half)
    ang = positions.astype(F32)[0, :, None] * inv_freq
    cos, sin = jnp.cos(ang), jnp.sin(ang)
    return jnp.concatenate([cos, cos], axis=-1), jnp.concatenate([-sin, sin], axis=-1)


def kernel(x, positions, w_in, w_pool, pool_scale, w_branch_attn, w_branch_pool, w_out, ln_mix_g, ln_mix_b, w_ff1, w_ff2, ln_ff_g, ln_ff_b, loss_target, m_w_in, m_w_pool, m_pool_scale, m_w_branch_attn, m_w_branch_pool, m_w_out, m_ln_mix_g, m_ln_mix_b, m_w_ff1, m_w_ff2, m_ln_ff_g, m_ln_ff_b, v_w_in, v_w_pool, v_pool_scale, v_w_branch_attn, v_w_branch_pool, v_w_out, v_ln_mix_g, v_ln_mix_b, v_w_ff1, v_w_ff2, v_ln_ff_g, v_ln_ff_b):
    s, d = x.shape[1], x.shape[2]
    aw = d // 2
    ng = len(POOL_WINDOWS)
    pgw = aw // ng
    x2d, target = x[0], loss_target[0]
    xb = x2d.astype(BF16)
    cos_t, sin_t = _rope_tables(positions)

    big = {"w_in": w_in[0], "w_pool": w_pool[0].reshape(-1, pgw), "w_branch_attn": w_branch_attn[0],
           "w_branch_pool": w_branch_pool[0], "w_out": w_out[0], "w_ff1": w_ff1[0], "w_ff2": w_ff2[0]}
    names = list(big)
    me_chip = (2 * lax.axis_index("x") + lax.axis_index("y")).astype(jnp.int32).reshape(1)
    shards = [big[k].astype(BF16) for k in names]
    gathering = _exchange_start("gather_start", GATHER, shards, [(N_CHIPS, *a.shape) for a in shards])

    def gathered(name, which, after):
        srcs, lands = _exchange_wait(f"gather_wait_{name}", GATHER, gathering, which, after)
        return [_place_own(f"place_own_{names[w]}", srcs[i], lands[i], me_chip) for i, w in enumerate(which)]

    rows_sh = pgw // N_CHIPS
    dff = N_CHIPS * big["w_ff2"].shape[0]

    w_in_st, wp_st = gathered("in", [0, 1], gathering["token"])
    wp = wp_st.reshape(N_CHIPS, ng, rows_sh, pgw).transpose(1, 0, 2, 3).reshape(ng, pgw, pgw)
    h, hv4, hv16 = _in_proj(xb, w_in_st, cos_t, sin_t, aw)
    qkv = {1: (h, h, h), 4: (hv4, hv4, hv4), 16: (hv16, hv16, hv16)}
    offs = {dil: (0, dil, 2 * dil) for dil in DILATIONS}
    o_parts, lse_parts = [], []
    for dil in DILATIONS:
        o_p, lse_p = _attn_fwd(f"attn_fwd_d{dil}", *qkv[dil], offs[dil], aw, dil * aw)
        o_parts.append(o_p)
        lse_parts.append(lse_p)
    o_attn, lse, lse4, lse16 = _attn_combine(o_parts[0], lse_parts[0], o_parts[1:], lse_parts[1:])
    p, pw, y = _pool_fwd(h, wp, pool_scale, aw)
    wba_st, wbp_st, w_out_st = gathered("mix", [2, 3, 4], y)
    w_out_full = w_out_st.reshape(d, d)
    ya, yp, merged = _branch_merge(o_attn, y, wba_st, wbp_st, h, aw, d)
    xhat1, rstd1, x1b = _mix_norm(merged, w_out_full, x2d, ln_mix_g, ln_mix_b)
    w1_st, w2_st = gathered("ff", [5, 6], x1b)
    w2_full = w2_st.reshape(dff, d)
    r = _ff_up(x1b, w1_st)
    dz2, dz2b, st2 = _ff_down_loss(r, w2_full, xhat1, ln_mix_g, ln_mix_b, ln_ff_g, ln_ff_b, target)

    def scatter_start(name, grads):
        return _exchange_start(f"scatter_start_{name}", SCATTER, grads, [(3, *g.shape[1:]) for g in grads])

    da = _ff_down_bwd(dz2b, w2_full, r)
    g_w2 = _wgrad("wgrad_ff2", r, dz2b, d).reshape(N_CHIPS, dff // N_CHIPS, d)
    g_w1 = _wgrad("wgrad_ff1", x1b, da, dff // N_CHIPS)
    sent_ff = scatter_start("ff", [g_w1, g_w2])
    dz1, dz1b, st1 = _ff_up_bwd(da, w1_st, dz2, xhat1, rstd1, ln_mix_g, [sent_ff["token"]])
    dya, dyp, dgate = _mix_bwd(dz1b, w_out_full, h, ya, yp, aw)
    g_wout = _wgrad("wgrad_out", merged, dz1b, d).reshape(N_CHIPS, d // N_CHIPS, d)
    g_wba = _wgrad("wgrad_branch_attn", o_attn, dya, d // N_CHIPS)
    g_wbp = _wgrad("wgrad_branch_pool", y, dyp, d // N_CHIPS)
    do, delta, do4, do16, delta4, delta16 = _attn_out_bwd(dya, wba_st, o_attn)
    dpw, stp = _pool_out_bwd(dyp, wbp_st, pw, pool_scale)
    dwp, du = _pool_bwd(dpw, p, wp)
    g_wp = dwp.reshape(ng, N_CHIPS, rows_sh, pgw).transpose(1, 0, 2, 3).reshape(
        N_CHIPS, ng * rows_sh, pgw).astype(BF16)
    sent_mix = scatter_start("mix", [g_wp, g_wba, g_wbp, g_wout])

    bwd_in = {1: (do, lse, delta), 4: (do4, lse4, delta4), 16: (do16, lse16, delta16)}
    dqkv = {}
    for dil in DILATIONS:
        args = (*qkv[dil], *bwd_in[dil], offs[dil], aw, dil * aw)
        dqkv[dil] = _attn_bwd(f"attn_bwd_d{dil}", *args)
    dqkvu = _qkvu_grad(dqkv[1], dqkv[4], dqkv[16], du, cos_t, sin_t)
    g_win_a = _wgrad("wgrad_in_qkvu", xb, dqkvu, d)
    g_win_b = _wgrad("wgrad_in_gates", xb, dgate, d)
    g_win = jnp.concatenate([g_win_a, g_win_b], axis=0)
    sent_in = scatter_start("in", [g_win])
    dx_a = _in_proj_bwd_x("in_proj_bwd_qkvu", dqkvu, w_in_st, 0, dz1, ALPHA,
                          [sent_mix["token"], sent_in["token"]])
    grad_x = _in_proj_bwd_x("in_proj_bwd_gates", dgate, w_in_st, 2, dx_a, 1.0)

    moments = {"w_in": (m_w_in, v_w_in), "w_pool": (m_w_pool, v_w_pool),
               "w_branch_attn": (m_w_branch_attn, v_w_branch_attn),
               "w_branch_pool": (m_w_branch_pool, v_w_branch_pool), "w_out": (m_w_out, v_w_out),
               "w_ff1": (m_w_ff1, v_w_ff1), "w_ff2": (m_w_ff2, v_w_ff2)}
    originals = {"w_in": w_in, "w_pool": w_pool, "w_branch_attn": w_branch_attn,
                 "w_branch_pool": w_branch_pool, "w_out": w_out, "w_ff1": w_ff1, "w_ff2": w_ff2}
    res = {}

    def summed(name, sent, keys, after):
        srcs, lands = _exchange_wait(f"scatter_wait_{name}", SCATTER, sent, list(range(len(keys))), after)
        parts = [_sum_slabs(f"sum_slabs_{k}", srcs[i], lands[i], me_chip) for i, k in enumerate(keys)]
        return _exchange_start(f"cores_start_{name}", SIBLING, parts, [a.shape for a in parts])

    def updated(name, swapping, keys, after):
        mine, other = _exchange_wait(f"cores_wait_{name}", SIBLING, swapping, list(range(len(keys))), after)
        for i, k in enumerate(keys):
            mk, vk = (a.reshape(big[k].shape) for a in moments[k])
            outs = _adamw(f"adamw_{k}", big[k], mk, vk, [mine[i], other[i]])
            res[k] = [o.reshape(originals[k].shape) for o in outs]

    groups = {"ff": ["w_ff1", "w_ff2"], "mix": ["w_pool", "w_branch_attn", "w_branch_pool", "w_out"],
              "in": ["w_in"]}
    swap_ff = summed("ff", sent_ff, groups["ff"], grad_x)
    swap_mix = summed("mix", sent_mix, groups["mix"], swap_ff["token"])
    swap_in = summed("in", sent_in, groups["in"], swap_mix["token"])
    updated("ff", swap_ff, groups["ff"], swap_in["token"])
    updated("mix", swap_mix, groups["mix"], res["w_ff2"][0])
    updated("in", swap_in, groups["in"], res["w_out"][0])
    tot2, tot1, totp = _allreduce_stats([st2, st1, stp])

    def pad_d(a):
        return jnp.pad(a, ((0, 0), (0, d - a.shape[1])))

    small = ["ln_mix_g", "ln_mix_b", "ln_ff_g", "ln_ff_b", "pool_scale"]
    small_w = {"ln_mix_g": ln_mix_g, "ln_mix_b": ln_mix_b, "ln_ff_g": ln_ff_g, "ln_ff_b": ln_ff_b,
               "pool_scale": pool_scale}
    small_m = {"ln_mix_g": m_ln_mix_g, "ln_mix_b": m_ln_mix_b, "ln_ff_g": m_ln_ff_g,
               "ln_ff_b": m_ln_ff_b, "pool_scale": m_pool_scale}
    small_v = {"ln_mix_g": v_ln_mix_g, "ln_mix_b": v_ln_mix_b, "ln_ff_g": v_ln_ff_g,
               "ln_ff_b": v_ln_ff_b, "pool_scale": v_pool_scale}
    small_g = [tot1[0:1], tot1[1:2], tot2[0:1], tot2[1:2], pad_d(totp[0:1])]

    def pack(rows):
        return jnp.concatenate([pad_d(a) for a in rows] + [jnp.zeros((8 - len(rows), d), F32)], axis=0)

    outs = _adamw("adamw_small", pack([small_w[k] for k in small]), pack([small_m[k] for k in small]),
                  pack([small_v[k] for k in small]), [pack(small_g)])
    for i, k in enumerate(small):
        res[k] = [o[i:i + 1, :small_w[k].shape[1]] for o in outs]
    loss = tot2[2, 0]

    order = ["w_in", "w_pool", "pool_scale", "w_branch_attn", "w_branch_pool", "w_out", "ln_mix_g",
             "ln_mix_b", "w_ff1", "w_ff2", "ln_ff_g", "ln_ff_b"]
    result = [loss, grad_x[None]]
    for idx in range(4):
        result += [res[k][idx] for k in order]
    return tuple(result)
```

```python
import jax
import jax.numpy as jnp
from jax import lax
from jax.experimental import pallas as pl
from jax.experimental.pallas import tpu as pltpu

F32 = jnp.float32
BF16 = jnp.bfloat16
MESH = pl.DeviceIdType.MESH

HEAD_DIM = 128
SUB_BLOCK = 128
DILATIONS = (1, 4, 16)
POOL_WINDOWS = (2, 4, 8, 16)
POOL_HALO = 16
ROPE_THETA = 10000.0
LN_EPS = 1e-5
ALPHA = 2.0 ** 0.25
ADAM_LR, ADAM_B1, ADAM_B2, ADAM_EPS, ADAM_WD, ADAM_STEP = 0.001, 0.9, 0.999, 1e-08, 0.01, 10
NEG = -1e30
N_CHIPS = 4
VMEM_LIMIT = 56 * 1024 * 1024
EPILOGUE_ROWS = 128


def _params(sem=None, vmem=VMEM_LIMIT):
    kw = {"vmem_limit_bytes": vmem}
    if sem is not None:
        kw["dimension_semantics"] = sem
    return pltpu.CompilerParams(**kw)


def _dot(a, b, contract):
    return lax.dot_general(a, b, (contract, ((), ())), preferred_element_type=F32)


ANY_SPEC = pl.BlockSpec(memory_space=pl.ANY)
HBM_SPEC = pl.BlockSpec(memory_space=pltpu.HBM)
SEM_SPEC = pl.BlockSpec(memory_space=pltpu.SEMAPHORE)
DATAFLOW = pltpu.SideEffectType.DATAFLOW_SIDE_EFFECTING


def _hbm(a):
    return pltpu.with_memory_space_constraint(a, pltpu.HBM)


NN = ((1,), (0,))
NT = ((1,), (1,))
TN = ((0,), (0,))


def _mm(name, grid, a, a_spec, b, b_spec, contract, extras, extra_specs, out_shape, out_specs,
        epilogue, acc_shape, acc_as_ref=False, run_after=(), scratch=(),
        semantics=("parallel", "parallel", "arbitrary")):
    nk = grid[2]
    n_ex = len(extras)
    n_in = 2 + n_ex + len(run_after)
    n_out = len(out_shape)
    n_scr = len(scratch)

    def body(*refs):
        a_ref, b_ref = refs[0], refs[1]
        ex = refs[2:2 + n_ex]
        outs = refs[n_in:n_in + n_out]
        scr = refs[n_in + n_out:n_in + n_out + n_scr]
        if nk == 1:
            epilogue(_dot(a_ref[...], b_ref[...], contract), ex, outs, *scr)
        else:
            acc = refs[n_in + n_out + n_scr]
            k = pl.program_id(2)

            @pl.when(k == 0)
            def _():
                acc[...] = jnp.zeros_like(acc)

            acc[...] += _dot(a_ref[...], b_ref[...], contract)

            @pl.when(k == nk - 1)
            def _():
                epilogue(acc if acc_as_ref else acc[...], ex, outs, *scr)

    acc_scratch = [pltpu.VMEM(acc_shape, F32)] if nk > 1 else []
    return pl.pallas_call(
        body, name=name, grid=grid,
        in_specs=[a_spec, b_spec, *extra_specs, *[ANY_SPEC] * len(run_after)], out_specs=out_specs,
        out_shape=out_shape, scratch_shapes=[*scratch, *acc_scratch],
        compiler_params=_params(semantics),
    )(a, b, *extras, *run_after)


def _stats_rows(rows, width):
    idx = lax.broadcasted_iota(jnp.int32, (8, width), 0)
    out = jnp.zeros((8, width), F32)
    for r, v in enumerate(rows):
        out = jnp.where(idx == r, jnp.broadcast_to(v, (8, width)), out)
    return out


def _layer_norm_fwd(z):
    mu = jnp.mean(z, axis=-1, keepdims=True)
    zc = z - mu
    var = jnp.mean(zc * zc, axis=-1, keepdims=True)
    rstd = lax.rsqrt(var + LN_EPS)
    return zc * rstd, rstd


def _layer_norm_bwd(dy, xhat, rstd, g):
    dxh = dy * g
    m1 = jnp.mean(dxh, axis=-1, keepdims=True)
    m2 = jnp.mean(dxh * xhat, axis=-1, keepdims=True)
    return rstd * (dxh - m1 - xhat * m2)


def _heads_scratch(rows, width):
    return pltpu.VMEM((width // HEAD_DIM, rows, HEAD_DIM), F32)


def _to_views(src_ref, view_refs, dtype):
    nh, rows, _ = src_ref.shape
    width = nh * HEAD_DIM
    for dil, view_ref in zip(DILATIONS[1:], view_refs):
        for r in range(dil):
            for hh in range(nh):
                c0 = r * width + hh * HEAD_DIM
                view_ref[:, c0:c0 + HEAD_DIM] = (
                    src_ref[hh, pl.ds(r, rows // dil, stride=dil), :].astype(dtype))


def _from_views(view_refs, dst_refs):
    nh, rows, _ = dst_refs[0].shape
    width = nh * HEAD_DIM
    for dil, view_ref, dst_ref in zip(DILATIONS[1:], view_refs, dst_refs):
        for r in range(dil):
            for hh in range(nh):
                c0 = r * width + hh * HEAD_DIM
                dst_ref[hh, pl.ds(r, rows // dil, stride=dil), :] = (
                    view_ref[:, c0:c0 + HEAD_DIM].astype(F32))


def _view_shape(rows, width, dil, parts=1):
    return (rows // dil, parts * dil * width)


def _in_proj(xb, w_in_st, cos_t, sin_t, aw):
    s, d = xb.shape
    n_sh = w_in_st.shape[2]
    tm, tn = min(s, 1024), aw
    per = n_sh // tn
    grid = (s // tm, (N_CHIPS * n_sh) // tn, 1)

    def epilogue(acc, ex, outs, scr):
        cos_ref, sin_ref = ex
        h_ref, v4_ref, v16_ref = outs
        seg = pl.program_id(1)

        heads = [slice(hh * HEAD_DIM, (hh + 1) * HEAD_DIM) for hh in range(tn // HEAD_DIM)]

        @pl.when(seg < 2)
        def _():
            cos, sin = cos_ref[...], sin_ref[...]
            for hh, sl in enumerate(heads):
                t = acc[:, sl]
                scr[hh] = t * cos + pltpu.roll(t, HEAD_DIM // 2, 1) * sin

        @pl.when(seg == 2)
        def _():
            for hh, sl in enumerate(heads):
                scr[hh] = acc[:, sl]

        @pl.when(seg < 3)
        def _():
            for hh, sl in enumerate(heads):
                h_ref[:, sl] = scr[hh].astype(BF16)
            _to_views(scr, (v4_ref, v16_ref), BF16)

        @pl.when(seg == 3)
        def _():
            h_ref[...] = acc.astype(BF16)

        @pl.when(seg >= 4)
        def _():
            h_ref[...] = jax.nn.sigmoid(acc).astype(BF16)

    def view_spec(dil):
        return pl.BlockSpec((tm // dil, dil * aw), lambda i, j, k: (i, jnp.minimum(j, 2)))

    return _mm(
        "in_proj", grid, xb, pl.BlockSpec((tm, d), lambda i, j, k: (i, 0)),
        w_in_st, pl.BlockSpec((None, d, tn), lambda i, j, k: (j // per, 0, j % per)), NN,
        [cos_t, sin_t], [pl.BlockSpec((tm, HEAD_DIM), lambda i, j, k: (i, 0))] * 2,
        [jax.ShapeDtypeStruct((s, N_CHIPS * n_sh), BF16)]
        + [jax.ShapeDtypeStruct(_view_shape(s, aw, dil, 3), BF16) for dil in DILATIONS[1:]],
        [pl.BlockSpec((tm, tn), lambda i, j, k: (i, j))] + [view_spec(dil) for dil in DILATIONS[1:]],
        epilogue, None, scratch=[_heads_scratch(tm, aw)],
        semantics=("parallel", "arbitrary", "arbitrary"))


def _band_masks(block_idx):
    qi = lax.broadcasted_iota(jnp.int32, (SUB_BLOCK, 2 * SUB_BLOCK), 0)
    kj = lax.broadcasted_iota(jnp.int32, (SUB_BLOCK, 2 * SUB_BLOCK), 1)
    first_key = jnp.where(block_idx > 0, 0, SUB_BLOCK)
    return (kj >= qi) & (kj <= qi + SUB_BLOCK) & (kj >= first_key)


def _attn_fwd(name, q, k, v, offs, cw, width):
    m = q.shape[0]
    nh = cw // HEAD_DIM
    grid = (width // cw, m // SUB_BLOCK)
    scale = HEAD_DIM ** -0.5

    def body(q_ref, k_ref, v_ref, o_ref, lse_ref, kprev, vprev):
        n = pl.program_id(1)
        valid = _band_masks(n)

        @pl.when(n == 0)
        def _():
            kprev[...] = jnp.zeros_like(kprev)
            vprev[...] = jnp.zeros_like(vprev)

        heads = [slice(hh * HEAD_DIM, (hh + 1) * HEAD_DIM) for hh in range(nh)]
        scs = [_dot(q_ref[:, sl], jnp.concatenate([kprev[:, sl], k_ref[:, sl]], axis=0), NT)
               for sl in heads]
        for hh, sl in enumerate(heads):
            vcat = jnp.concatenate([vprev[:, sl], v_ref[:, sl]], axis=0)
            sc = jnp.where(valid, scs[hh] * scale, NEG)
            mx = jnp.max(sc, axis=-1, keepdims=True)
            p = jnp.exp(sc - mx)
            l = jnp.sum(p, axis=-1, keepdims=True)
            o = _dot(p.astype(BF16), vcat, NN) / l
            o_ref[:, sl] = o.astype(BF16)
            lse_ref[:, sl] = jnp.broadcast_to(mx + jnp.log(l), (SUB_BLOCK, HEAD_DIM))
        kprev[...] = k_ref[...]
        vprev[...] = v_ref[...]

    def cur(off):
        return pl.BlockSpec((SUB_BLOCK, cw), lambda j, n: (n, j + off))

    return pl.pallas_call(
        body, name=name, grid=grid,
        in_specs=[cur(offs[0]), cur(offs[1]), cur(offs[2])], out_specs=[cur(0), cur(0)],
        out_shape=[jax.ShapeDtypeStruct((m, width), BF16), jax.ShapeDtypeStruct((m, width), F32)],
        scratch_shapes=[pltpu.VMEM((SUB_BLOCK, cw), BF16)] * 2,
        compiler_params=_params(("parallel", "arbitrary")),
    )(q, k, v)


def _view_spec(tm, aw, dil):
    return pl.BlockSpec((tm // dil, dil * aw), lambda i: (i, 0))


def _attn_combine(o1, l1, o_views, l_views):
    s, aw = o1.shape
    tm = min(s, 512)

    def body(o1_ref, l1_ref, o4_ref, o16_ref, l4_ref, l16_ref, o_ref, lse_ref, lse4_ref, lse16_ref,
             so4, so16, sl4, sl16, stot):
        _from_views((o4_ref, o16_ref), (so4, so16))
        _from_views((l4_ref, l16_ref), (sl4, sl16))
        for hh in range(aw // HEAD_DIM):
            sl = slice(hh * HEAD_DIM, (hh + 1) * HEAD_DIM)
            a, b, c = l1_ref[:, sl], sl4[hh], sl16[hh]
            mx = jnp.maximum(jnp.maximum(a, b), c)
            ea, eb, ec = jnp.exp(a - mx), jnp.exp(b - mx), jnp.exp(c - mx)
            tot = ea + eb + ec
            o = (ea * o1_ref[:, sl].astype(F32) + eb * so4[hh] + ec * so16[hh]) / tot
            o_ref[:, sl] = o.astype(BF16)
            lse_tot = mx + jnp.log(tot)
            stot[hh] = lse_tot
            lse_ref[:, sl] = lse_tot
        _to_views(stot, (lse4_ref, lse16_ref), F32)

    row = pl.BlockSpec((tm, aw), lambda i: (i, 0))
    views = [_view_spec(tm, aw, dil) for dil in DILATIONS[1:]]
    return pl.pallas_call(
        body, name="attn_combine", grid=(s // tm,), in_specs=[row, row, *views, *views],
        out_specs=[row, row, *views],
        out_shape=[jax.ShapeDtypeStruct((s, aw), BF16), jax.ShapeDtypeStruct((s, aw), F32)]
        + [jax.ShapeDtypeStruct(_view_shape(s, aw, dil), F32) for dil in DILATIONS[1:]],
        scratch_shapes=[_heads_scratch(tm, aw)] * 5,
        compiler_params=_params(("parallel",)),
    )(o1, l1, *o_views, *l_views)


def _pool_counts(tm, rows, pgw, row0):
    t = lax.broadcasted_iota(jnp.int32, (rows, len(POOL_WINDOWS) * pgw), 0) + row0
    col = lax.broadcasted_iota(jnp.int32, (rows, len(POOL_WINDOWS) * pgw), 1)
    w = jnp.full((rows, len(POOL_WINDOWS) * pgw), POOL_WINDOWS[0], jnp.int32)
    for g in range(1, len(POOL_WINDOWS)):
        w = jnp.where(col >= g * pgw, POOL_WINDOWS[g], w)
    return jnp.minimum(t + 1, w).astype(F32)


def _window_sums(xs, direction, pgw):
    rows = xs.shape[0]
    acc = xs
    out = None
    col = lax.broadcasted_iota(jnp.int32, xs.shape, 1)
    for g, w in enumerate(POOL_WINDOWS):
        sh = w // 2
        acc = acc + pltpu.roll(acc, sh if direction > 0 else rows - sh, 0)
        out = acc if out is None else jnp.where(col >= g * pgw, acc, out)
    return out


def _pool_fwd(h, wp, scale, aw):
    s = h.shape[0]
    pw_ = aw
    pgw = pw_ // len(POOL_WINDOWS)
    tm = min(s, 512)
    hb = tm // POOL_HALO

    def body(u_ref, halo_ref, wp_ref, sc_ref, p_ref, pw_ref, y_ref):
        i = pl.program_id(0)
        u = u_ref[...].astype(F32)
        halo = halo_ref[...].astype(F32) * jnp.where(i > 0, 1.0, 0.0)
        xs = jnp.concatenate([halo, u], axis=0)
        sums = _window_sums(xs, +1, pgw)[POOL_HALO:]
        p = (sums / _pool_counts(tm, tm, pgw, i * tm) - u).astype(BF16)
        p_ref[...] = p
        sc = sc_ref[...]
        for g in range(len(POOL_WINDOWS)):
            sl = slice(g * pgw, (g + 1) * pgw)
            pw = _dot(p[:, sl], wp_ref[g], NN)
            pw_ref[:, sl] = pw.astype(BF16)
            y_ref[:, sl] = (pw * sc[:, sl]).astype(BF16)

    out = jax.ShapeDtypeStruct((s, pw_), BF16)
    row = pl.BlockSpec((tm, pw_), lambda i: (i, 0))
    return pl.pallas_call(
        body, name="pool_fwd", grid=(s // tm,),
        in_specs=[pl.BlockSpec((tm, pw_), lambda i: (i, 3)),
                  pl.BlockSpec((POOL_HALO, pw_), lambda i: (jnp.maximum(i * hb - 1, 0), 3)),
                  pl.BlockSpec(wp.shape, lambda i: (0, 0, 0)),
                  pl.BlockSpec((1, pw_), lambda i: (0, 0))],
        out_specs=[row, row, row], out_shape=[out, out, out],
        compiler_params=_params(("parallel",)),
    )(h, h, wp, scale)


def _branch_merge(o_attn, y, wba_st, wbp_st, h, aw, d):
    s = o_attn.shape[0]
    tn = wba_st.shape[2]
    tm = min(s, 1024)
    ga0 = 4 * aw // tn
    gp0 = (4 * aw + d) // tn

    def body(o_ref, y_ref, wa_ref, wp_ref, sga_ref, sgp_ref, ya_ref, yp_ref, mg_ref):
        ya = _dot(o_ref[...], wa_ref[...], NN)
        yp = _dot(y_ref[...], wp_ref[...], NN)
        ya_ref[...] = ya.astype(BF16)
        yp_ref[...] = yp.astype(BF16)
        mg_ref[...] = (sga_ref[...].astype(F32) * ya + sgp_ref[...].astype(F32) * yp).astype(BF16)

    out = jax.ShapeDtypeStruct((s, d), BF16)
    blk = pl.BlockSpec((tm, tn), lambda i, j: (i, j))
    return pl.pallas_call(
        body, name="branch_merge", grid=(s // tm, N_CHIPS),
        in_specs=[pl.BlockSpec((tm, aw), lambda i, j: (i, 0)),
                  pl.BlockSpec((tm, aw), lambda i, j: (i, 0)),
                  pl.BlockSpec((None, aw, tn), lambda i, j: (j, 0, 0)),
                  pl.BlockSpec((None, aw, tn), lambda i, j: (j, 0, 0)),
                  pl.BlockSpec((tm, tn), lambda i, j: (i, j + ga0)),
                  pl.BlockSpec((tm, tn), lambda i, j: (i, j + gp0))],
        out_specs=[blk, blk, blk], out_shape=[out, out, out],
        compiler_params=_params(("parallel", "parallel")),
    )(o_attn, y, wba_st, wbp_st, h, h)


def _mix_norm(merged, w_out, x, g1, b1):
    s, d = x.shape
    tm = min(s, 256)

    def epilogue(acc, ex, outs):
        x_ref, g_ref, b_ref = ex
        xh_ref, rs_ref, xb_ref = outs
        xhat, rstd = _layer_norm_fwd(ALPHA * x_ref[...] + acc)
        xh_ref[...] = xhat
        rs_ref[...] = rstd
        xb_ref[...] = (xhat * g_ref[...] + b_ref[...]).astype(BF16)

    row = pl.BlockSpec((tm, d), lambda i, j, k: (i, 0))
    vec = pl.BlockSpec((1, d), lambda i, j, k: (0, 0))
    return _mm(
        "mix_norm", (s // tm, 1, 1), merged, row, w_out, pl.BlockSpec((d, d), lambda i, j, k: (0, 0)),
        NN, [x, g1, b1], [row, vec, vec],
        [jax.ShapeDtypeStruct((s, d), F32), jax.ShapeDtypeStruct((s, 1), F32),
         jax.ShapeDtypeStruct((s, d), BF16)],
        [row, pl.BlockSpec((tm, 1), lambda i, j, k: (i, 0)), row], epilogue, None)


def _ff_up(x1b, w1_st):
    s, d = x1b.shape
    n_sh = w1_st.shape[2]
    tm, tn = min(s, 1024), min(n_sh, 1024)
    per = n_sh // tn

    def epilogue(acc, ex, outs):
        r = jnp.maximum(acc, 0.0)
        outs[0][...] = (r * r).astype(BF16)

    return _mm(
        "ff_up", (s // tm, N_CHIPS * per, 1), x1b, pl.BlockSpec((tm, d), lambda i, j, k: (i, 0)),
        w1_st, pl.BlockSpec((None, d, tn), lambda i, j, k: (j // per, 0, j % per)), NN, [], [],
        [jax.ShapeDtypeStruct((s, N_CHIPS * n_sh), BF16)],
        [pl.BlockSpec((tm, tn), lambda i, j, k: (i, j))], epilogue, None)[0]


def _ff_down_loss(r, w2, xhat1, g1, b1, g2, b2, target):
    s, d = xhat1.shape
    dff = r.shape[1]
    tm, tk = min(s, 512), min(dff, 1024)
    ch = min(tm, EPILOGUE_ROWS)

    def epilogue(acc_ref, ex, outs):
        xh1_ref, g1_ref, b1_ref, g2_ref, b2_ref, t_ref = ex
        dz_ref, dzb_ref, st_ref = outs
        g1v, b1v, g2v, b2v = g1_ref[...], b1_ref[...], g2_ref[...], b2_ref[...]
        dg = db = loss = None
        for c in range(tm // ch):
            rows = slice(c * ch, (c + 1) * ch)
            x1 = xh1_ref[rows, :] * g1v + b1v
            xhat2, rstd2 = _layer_norm_fwd(ALPHA * x1 + acc_ref[rows, :])
            err = xhat2 * g2v + b2v - t_ref[rows, :]
            dy = err * (1.0 / d)
            dz = _layer_norm_bwd(dy, xhat2, rstd2, g2v)
            dz_ref[rows, :] = dz
            dzb_ref[rows, :] = dz.astype(BF16)
            parts = (jnp.sum(dy * xhat2, axis=0, keepdims=True), jnp.sum(dy, axis=0, keepdims=True),
                     jnp.sum(jnp.sum(err * err, axis=-1, keepdims=True), axis=0, keepdims=True))
            dg, db, loss = parts if c == 0 else (dg + parts[0], db + parts[1], loss + parts[2])
        st_ref[...] = _stats_rows([dg, db, jnp.broadcast_to((0.5 / d) * loss, (1, d))], d)

    row = pl.BlockSpec((tm, d), lambda i, j, k: (i, 0))
    vec = pl.BlockSpec((1, d), lambda i, j, k: (0, 0))
    return _mm(
        "ff_down_loss", (s // tm, 1, dff // tk), r, pl.BlockSpec((tm, tk), lambda i, j, k: (i, k)),
        w2, pl.BlockSpec((tk, d), lambda i, j, k: (k, 0)), NN,
        [xhat1, g1, b1, g2, b2, target], [row, vec, vec, vec, vec, row],
        [jax.ShapeDtypeStruct((s, d), F32), jax.ShapeDtypeStruct((s, d), BF16),
         jax.ShapeDtypeStruct((s // tm, 8, d), F32)],
        [row, row, pl.BlockSpec((None, 8, d), lambda i, j, k: (i, 0, 0))], epilogue, (tm, d),
        acc_as_ref=True)


def _ff_down_bwd(dz2b, w2, r):
    s, d = dz2b.shape
    dff = r.shape[1]
    tm, tn = min(s, 1024), min(dff, 1024)

    def epilogue(acc, ex, outs):
        outs[0][...] = (acc * (2.0 * jnp.sqrt(ex[0][...].astype(F32)))).astype(BF16)

    blk = pl.BlockSpec((tm, tn), lambda i, j, k: (i, j))
    return _mm(
        "ff_down_bwd", (s // tm, dff // tn, 1), dz2b, pl.BlockSpec((tm, d), lambda i, j, k: (i, 0)),
        w2, pl.BlockSpec((tn, d), lambda i, j, k: (j, 0)), NT, [r], [blk],
        [jax.ShapeDtypeStruct((s, dff), BF16)], [blk], epilogue, None)[0]


def _wgrad(name, a, g, n_sh):
    s, rows = a.shape
    cols = g.shape[1]
    tm, tn, tk = min(rows, 2048), min(cols, 1024), min(s, 1024)
    if tn >= n_sh:
        span = tn // n_sh
        out_spec = pl.BlockSpec((span, tm, n_sh), lambda i, j, k: (j, i, 0))

        def epilogue(acc_ref, ex, outs):
            for sh in range(span):
                outs[0][sh] = acc_ref[:, sh * n_sh:(sh + 1) * n_sh].astype(BF16)
    else:
        per = n_sh // tn
        out_spec = pl.BlockSpec((None, tm, tn), lambda i, j, k: (j // per, i, j % per))

        def epilogue(acc_ref, ex, outs):
            outs[0][...] = acc_ref[...].astype(BF16)

    return _mm(
        name, (rows // tm, cols // tn, s // tk), a, pl.BlockSpec((tk, tm), lambda i, j, k: (k, i)),
        g, pl.BlockSpec((tk, tn), lambda i, j, k: (k, j)), TN, [], [],
        [jax.ShapeDtypeStruct((cols // n_sh, rows, n_sh), BF16)], [out_spec], epilogue,
        (tm, tn), acc_as_ref=True)[0]


def _ff_up_bwd(da, w1_st, dz2, xhat1, rstd1, g1, run_after):
    s, d = dz2.shape
    n_sh = w1_st.shape[2]
    tm, tk = min(s, 512), min(n_sh, 1024)
    per = n_sh // tk
    ch = min(tm, EPILOGUE_ROWS)

    def epilogue(acc_ref, ex, outs):
        dz2_ref, xh_ref, rs_ref, g_ref = ex
        dz_ref, dzb_ref, st_ref = outs
        gv = g_ref[...]
        dg = db = None
        for c in range(tm // ch):
            rows = slice(c * ch, (c + 1) * ch)
            dx1 = ALPHA * dz2_ref[rows, :] + acc_ref[rows, :]
            xhat = xh_ref[rows, :]
            dz = _layer_norm_bwd(dx1, xhat, rs_ref[rows, :], gv)
            dz_ref[rows, :] = dz
            dzb_ref[rows, :] = dz.astype(BF16)
            parts = (jnp.sum(dx1 * xhat, axis=0, keepdims=True), jnp.sum(dx1, axis=0, keepdims=True))
            dg, db = parts if c == 0 else (dg + parts[0], db + parts[1])
        st_ref[...] = _stats_rows([dg, db], d)

    row = pl.BlockSpec((tm, d), lambda i, j, k: (i, 0))
    return _mm(
        "ff_up_bwd", (s // tm, 1, N_CHIPS * per), da, pl.BlockSpec((tm, tk), lambda i, j, k: (i, k)),
        w1_st, pl.BlockSpec((None, d, tk), lambda i, j, k: (k // per, 0, k % per)), NT,
        [dz2, xhat1, rstd1, g1],
        [row, row, pl.BlockSpec((tm, 1), lambda i, j, k: (i, 0)), pl.BlockSpec((1, d), lambda i, j, k: (0, 0))],
        [jax.ShapeDtypeStruct((s, d), F32), jax.ShapeDtypeStruct((s, d), BF16),
         jax.ShapeDtypeStruct((s // tm, 8, d), F32)],
        [row, row, pl.BlockSpec((None, 8, d), lambda i, j, k: (i, 0, 0))], epilogue, (tm, d),
        acc_as_ref=True, run_after=run_after)


def _mix_bwd(dz1b, w_out, h, ya, yp, aw):
    s, d = dz1b.shape
    tm = min(s, 256)
    gblk = 4 * aw // d

    def epilogue(acc, ex, outs):
        sga_ref, sgp_ref, ya_ref, yp_ref = ex
        dya_ref, dyp_ref, dg_ref = outs
        sga, sgp = sga_ref[...].astype(F32), sgp_ref[...].astype(F32)
        dya_ref[...] = (acc * sga).astype(BF16)
        dyp_ref[...] = (acc * sgp).astype(BF16)
        dg_ref[:, :d] = (acc * ya_ref[...].astype(F32) * (sga * (1.0 - sga))).astype(BF16)
        dg_ref[:, d:] = (acc * yp_ref[...].astype(F32) * (sgp * (1.0 - sgp))).astype(BF16)

    row = pl.BlockSpec((tm, d), lambda i, j, k: (i, 0))
    return _mm(
        "mix_bwd", (s // tm, 1, 1), dz1b, row, w_out, pl.BlockSpec((d, d), lambda i, j, k: (0, 0)), NT,
        [h, h, ya, yp],
        [pl.BlockSpec((tm, d), lambda i, j, k: (i, gblk)),
         pl.BlockSpec((tm, d), lambda i, j, k: (i, gblk + 1)), row, row],
        [jax.ShapeDtypeStruct((s, d), BF16), jax.ShapeDtypeStruct((s, d), BF16),
         jax.ShapeDtypeStruct((s, 2 * d), BF16)],
        [row, row, pl.BlockSpec((tm, 2 * d), lambda i, j, k: (i, 0))], epilogue, None)


def _branch_in_bwd(name, tm, dyb, wb_st, epilogue, extras, extra_specs, out_shape, out_specs,
                   scratch=()):
    s = dyb.shape[0]
    aw, tk = wb_st.shape[1], wb_st.shape[2]
    return _mm(
        name, (s // tm, 1, N_CHIPS), dyb, pl.BlockSpec((tm, tk), lambda i, j, k: (i, k)),
        wb_st, pl.BlockSpec((None, aw, tk), lambda i, j, k: (k, 0, 0)), NT,
        extras, extra_specs, out_shape, out_specs, epilogue, (tm, aw), acc_as_ref=True,
        scratch=scratch)


def _attn_out_bwd(dya, wba_st, o_attn):
    s, aw = o_attn.shape
    tm = min(s, 512)

    def epilogue(acc_ref, ex, outs, sdo, sdl):
        do_ref, dl_ref, do4_ref, do16_ref, dl4_ref, dl16_ref = outs
        for hh in range(aw // HEAD_DIM):
            sl = slice(hh * HEAD_DIM, (hh + 1) * HEAD_DIM)
            do = acc_ref[:, sl]
            dl = jnp.broadcast_to(
                jnp.sum(do * ex[0][:, sl].astype(F32), axis=-1, keepdims=True), (tm, HEAD_DIM))
            sdo[hh] = do
            sdl[hh] = dl
            do_ref[:, sl] = do.astype(BF16)
            dl_ref[:, sl] = dl
        _to_views(sdo, (do4_ref, do16_ref), BF16)
        _to_views(sdl, (dl4_ref, dl16_ref), F32)

    row = pl.BlockSpec((tm, aw), lambda i, j, k: (i, 0))
    views = [pl.BlockSpec((tm // dil, dil * aw), lambda i, j, k: (i, 0)) for dil in DILATIONS[1:]]
    view_shapes = [_view_shape(s, aw, dil) for dil in DILATIONS[1:]]
    return _branch_in_bwd(
        "attn_out_bwd", tm, dya, wba_st, epilogue, [o_attn], [row],
        [jax.ShapeDtypeStruct((s, aw), BF16), jax.ShapeDtypeStruct((s, aw), F32)]
        + [jax.ShapeDtypeStruct(sh, BF16) for sh in view_shapes]
        + [jax.ShapeDtypeStruct(sh, F32) for sh in view_shapes],
        [row, row, *views, *views], scratch=[_heads_scratch(tm, aw)] * 2)


def _pool_out_bwd(dyp, wbp_st, pw, scale):
    s, pw_ = pw.shape
    tm = min(s, 1024)

    def epilogue(acc_ref, ex, outs):
        pw_ref, sc_ref = ex
        dpw_ref, st_ref = outs
        acc = acc_ref[...]
        dpw_ref[...] = (acc * sc_ref[...]).astype(BF16)
        st_ref[...] = _stats_rows([jnp.sum(acc * pw_ref[...].astype(F32), axis=0, keepdims=True)], pw_)

    row = pl.BlockSpec((tm, pw_), lambda i, j, k: (i, 0))
    return _branch_in_bwd(
        "pool_out_bwd", tm, dyp, wbp_st, epilogue, [pw, scale],
        [row, pl.BlockSpec((1, pw_), lambda i, j, k: (0, 0))],
        [jax.ShapeDtypeStruct((s, pw_), BF16), jax.ShapeDtypeStruct((s // tm, 8, pw_), F32)],
        [row, pl.BlockSpec((None, 8, pw_), lambda i, j, k: (i, 0, 0))])


def _pool_bwd(dpw, p, wp):
    s, pw_ = p.shape
    ng = len(POOL_WINDOWS)
    pgw = pw_ // ng
    tm = min(s, 512)
    hb = tm // POOL_HALO
    nblk = s // tm

    def body(dpw_ref, nxt_ref, p_ref, wp_ref, dwp_ref, du_ref):
        i = pl.program_id(0)
        nxt = (nxt_ref[...].astype(F32) * jnp.where(i < nblk - 1, 1.0, 0.0)).astype(BF16)
        dpw_all = jnp.concatenate([dpw_ref[...], nxt], axis=0)

        @pl.when(i == 0)
        def _():
            dwp_ref[...] = jnp.zeros_like(dwp_ref)

        dps = []
        for g in range(ng):
            sl = slice(g * pgw, (g + 1) * pgw)
            dwp_ref[g] += _dot(p_ref[:, sl], dpw_ref[:, sl], TN)
            dps.append(_dot(dpw_all[:, sl], wp_ref[g], NT))
        dp = jnp.concatenate(dps, axis=1)
        dpn = dp / _pool_counts(tm, tm + POOL_HALO, pgw, i * tm)
        du_ref[...] = (_window_sums(dpn, -1, pgw)[:tm] - dp[:tm]).astype(BF16)

    row = pl.BlockSpec((tm, pw_), lambda i: (i, 0))
    full = pl.BlockSpec((ng, pgw, pgw), lambda i: (0, 0, 0))
    return pl.pallas_call(
        body, name="pool_bwd", grid=(nblk,),
        in_specs=[row, pl.BlockSpec((POOL_HALO, pw_), lambda i: (jnp.minimum((i + 1) * hb, s // POOL_HALO - 1), 0)),
                  row, full],
        out_specs=[full, row],
        out_shape=[jax.ShapeDtypeStruct((ng, pgw, pgw), F32), jax.ShapeDtypeStruct((s, pw_), BF16)],
        compiler_params=_params(("arbitrary",)),
    )(dpw, dpw, p, wp)


def _attn_bwd(name, q, k, v, do, lse, delta, offs, cw, width):
    m = do.shape[0]
    nh = cw // HEAD_DIM
    nblk = m // SUB_BLOCK
    grid = (width // cw, nblk + 1)
    scale = HEAD_DIM ** -0.5

    def body(q_ref, k_ref, v_ref, do_ref, lse_ref, dl_ref, dq_ref, dk_ref, dv_ref,
             kprev, vprev, dk_carry, dv_carry):
        n = pl.program_id(1)

        @pl.when(n == 0)
        def _():
            for ref in (kprev, vprev, dk_carry, dv_carry):
                ref[...] = jnp.zeros_like(ref)

        qi = lax.broadcasted_iota(jnp.int32, (SUB_BLOCK, 2 * SUB_BLOCK), 0)
        kj = lax.broadcasted_iota(jnp.int32, (SUB_BLOCK, 2 * SUB_BLOCK), 1)
        first_key = jnp.where(n == 0, SUB_BLOCK, jnp.where(n == nblk, 4 * SUB_BLOCK, 0))
        valid = (kj >= qi) & (kj <= qi + SUB_BLOCK) & (kj >= first_key)
        heads = [slice(hh * HEAD_DIM, (hh + 1) * HEAD_DIM) for hh in range(nh)]
        kcats = [jnp.concatenate([kprev[:, sl], k_ref[:, sl]], axis=0) for sl in heads]
        scs = [_dot(q_ref[:, sl], kcats[hh], NT) for hh, sl in enumerate(heads)]
        dps = [_dot(do_ref[:, sl], jnp.concatenate([vprev[:, sl], v_ref[:, sl]], axis=0), NT)
               for sl in heads]
        dqs = []
        for hh, sl in enumerate(heads):
            lse2 = jnp.concatenate([lse_ref[:, sl]] * 2, axis=1)
            dl2 = jnp.concatenate([dl_ref[:, sl]] * 2, axis=1)
            p = jnp.where(valid, jnp.exp(jnp.where(valid, scs[hh] * scale, NEG) - lse2), 0.0)
            ds = (p * (dps[hh] - dl2)).astype(BF16)
            dqs.append((_dot(ds, kcats[hh], NN) * scale).astype(BF16))
            dk2 = _dot(ds, q_ref[:, sl], TN) * scale
            dv2 = _dot(p.astype(BF16), do_ref[:, sl], TN)
            dk_ref[:, sl] = (dk_carry[:, sl] + dk2[:SUB_BLOCK]).astype(BF16)
            dv_ref[:, sl] = (dv_carry[:, sl] + dv2[:SUB_BLOCK]).astype(BF16)
            dk_carry[:, sl] = dk2[SUB_BLOCK:]
            dv_carry[:, sl] = dv2[SUB_BLOCK:]

        @pl.when(n < nblk)
        def _():
            for hh, sl in enumerate(heads):
                dq_ref[:, sl] = dqs[hh]

        kprev[...] = k_ref[...]
        vprev[...] = v_ref[...]

    def cur(off):
        return pl.BlockSpec((SUB_BLOCK, cw), lambda j, n: (jnp.minimum(n, nblk - 1), j + off))

    lagged = pl.BlockSpec((SUB_BLOCK, cw), lambda j, n: (jnp.maximum(n - 1, 0), j))
    out = jax.ShapeDtypeStruct((m, width), BF16)
    return pl.pallas_call(
        body, name=name, grid=grid,
        in_specs=[cur(offs[0]), cur(offs[1]), cur(offs[2]), cur(0), cur(0), cur(0)],
        out_specs=[cur(0), lagged, lagged], out_shape=[out, out, out],
        scratch_shapes=[pltpu.VMEM((SUB_BLOCK, cw), BF16)] * 2 + [pltpu.VMEM((SUB_BLOCK, cw), F32)] * 2,
        compiler_params=_params(("parallel", "arbitrary")),
    )(q, k, v, do, lse, delta)


def _qkvu_grad(d1, d4, d16, du, cos_t, sin_t):
    s, aw = du.shape
    tm = min(s, 512)

    def body(*refs):
        nat, v4, v16 = refs[0:3], refs[3:6], refs[6:9]
        cos_ref, sin_ref, du_ref, out_ref, s4, s16 = refs[9:]
        cos, sin = cos_ref[...], sin_ref[...]
        for part in range(3):
            _from_views((v4[part], v16[part]), (s4, s16))
            for hh in range(aw // HEAD_DIM):
                sl = slice(hh * HEAD_DIM, (hh + 1) * HEAD_DIM)
                t = nat[part][:, sl].astype(F32) + s4[hh] + s16[hh]
                if part < 2:
                    t = t * cos - pltpu.roll(t, HEAD_DIM // 2, 1) * sin
                out_ref[:, part * aw + hh * HEAD_DIM:part * aw + (hh + 1) * HEAD_DIM] = t.astype(BF16)
        out_ref[:, 3 * aw:] = du_ref[...]

    row = pl.BlockSpec((tm, aw), lambda i: (i, 0))
    tab = pl.BlockSpec((tm, HEAD_DIM), lambda i: (i, 0))
    return pl.pallas_call(
        body, name="qkvu_grad", grid=(s // tm,),
        in_specs=[row] * 3 + [_view_spec(tm, aw, 4)] * 3 + [_view_spec(tm, aw, 16)] * 3 + [tab, tab, row],
        out_specs=pl.BlockSpec((tm, 4 * aw), lambda i: (i, 0)),
        out_shape=jax.ShapeDtypeStruct((s, 4 * aw), BF16),
        scratch_shapes=[_heads_scratch(tm, aw)] * 2,
        compiler_params=_params(("parallel",)),
    )(*d1, *d4, *d16, cos_t, sin_t, du)


def _in_proj_bwd_x(name, dh, w_in_st, shard0, base, scale_base, run_after=()):
    s, kdim = dh.shape
    d, n_sh = w_in_st.shape[1], w_in_st.shape[2]
    tm, tk = min(s, 512), min(n_sh, 2048)
    per = n_sh // tk

    ch = min(tm, 2 * EPILOGUE_ROWS)

    def epilogue(acc_ref, ex, outs):
        for c in range(tm // ch):
            rows = slice(c * ch, (c + 1) * ch)
            outs[0][rows, :] = scale_base * ex[0][rows, :] + acc_ref[rows, :]

    row = pl.BlockSpec((tm, d), lambda i, j, k: (i, 0))
    return _mm(
        name, (s // tm, 1, kdim // tk), dh, pl.BlockSpec((tm, tk), lambda i, j, k: (i, k)),
        w_in_st, pl.BlockSpec((None, d, tk), lambda i, j, k: (shard0 + k // per, 0, k % per)), NT,
        [base], [row], [jax.ShapeDtypeStruct((s, d), F32)], [row], epilogue, (tm, d),
        acc_as_ref=True, run_after=run_after)[0]


def _chip_peers():
    x, y, c = lax.axis_index("x"), lax.axis_index("y"), lax.axis_index("c")
    return x, y, c, [(1 - x, y), (x, 1 - y), (1 - x, 1 - y)]


GATHER, GATHER_HALF, SCATTER, SIBLING = "gather", "gather_half", "scatter", "sibling"


def _exchange_peers(mode):
    x, y, c, chips = _chip_peers()
    if mode == SIBLING:
        return x, y, c, [(x, y, 1 - c)]
    return x, y, c, [(px, py, c) for px, py in chips]


def _core_half(ref_or_shape, c):
    rows = (ref_or_shape.shape[0]) // 2
    return pl.ds(c * rows, rows)


def _exchange_descriptor(mode, src, land, send, recv, p, peer, me, arriving):
    pid = 2 * peer[0] + peer[1]
    if mode == GATHER:
        src_ref, dst_ref = src, land.at[pid if arriving else me]
    elif mode == GATHER_HALF:
        rows = _core_half(src, peer[2])
        src_ref, dst_ref = src.at[rows], land.at[pid if arriving else me, rows]
    elif mode == SCATTER:
        src_ref, dst_ref = src.at[pid], land.at[p]
    else:
        src_ref, dst_ref = src, land
    return pltpu.make_async_remote_copy(
        src_ref=src_ref, dst_ref=dst_ref, send_sem=send.at[p], recv_sem=recv.at[p],
        device_id=peer, device_id_type=MESH)


def _exchange_start(name, mode, srcs, land_shapes):
    n = len(srcs)
    lands = [_hbm(lax.empty(shape, src.dtype)) for shape, src in zip(land_shapes, srcs)]

    def body(*refs):
        src_refs, land_refs = refs[:n], refs[n:2 * n]
        sends, recvs = refs[2 * n:3 * n], refs[3 * n:4 * n]
        token = refs[6 * n]
        x, y, c, peers = _exchange_peers(mode)
        me = 2 * x + y
        for w in range(n):
            for p, peer in enumerate(peers):
                _exchange_descriptor(mode, src_refs[w], land_refs[w], sends[w], recvs[w], p, peer,
                                     me, arriving=False).start()
        token[...] = jnp.zeros_like(token)

    sem = pltpu.SemaphoreType.DMA((3,))
    outs = pl.pallas_call(
        body, name=name, in_specs=[HBM_SPEC] * (2 * n),
        out_specs=[SEM_SPEC] * (2 * n) + [HBM_SPEC] * (2 * n) + [pl.BlockSpec(memory_space=pltpu.VMEM)],
        out_shape=[sem] * (2 * n) + [pltpu.HBM(a.shape, a.dtype) for a in (*srcs, *lands)]
        + [jax.ShapeDtypeStruct((8, 128), F32)],
        input_output_aliases={i: 2 * n + i for i in range(2 * n)},
        compiler_params=pltpu.CompilerParams(has_side_effects=DATAFLOW),
    )(*[_hbm(a) for a in srcs], *lands)
    return {"send": outs[:n], "recv": outs[n:2 * n], "src": outs[2 * n:3 * n],
            "land": outs[3 * n:4 * n], "token": outs[4 * n]}


def _exchange_wait(name, mode, started, which, after):
    m = len(which)

    def body(*refs):
        src_refs, land_refs = refs[:m], refs[m:2 * m]
        sends, recvs = refs[2 * m:3 * m], refs[3 * m:4 * m]
        x, y, c, peers = _exchange_peers(mode)
        me = 2 * x + y
        for w in range(m):
            for p, peer in enumerate(peers):
                _exchange_descriptor(mode, src_refs[w], land_refs[w], sends[w], recvs[w], p, peer,
                                     me, arriving=False).wait_send()
                _exchange_descriptor(mode, src_refs[w], land_refs[w], sends[w], recvs[w], p, peer,
                                     me, arriving=True).wait_recv()

    pick = lambda key: [started[key][w] for w in which]
    bufs = pick("src") + pick("land")
    outs = pl.pallas_call(
        body, name=name, in_specs=[HBM_SPEC] * (2 * m) + [SEM_SPEC] * (2 * m) + [ANY_SPEC],
        out_specs=[HBM_SPEC] * (2 * m), out_shape=[pltpu.HBM(a.shape, a.dtype) for a in bufs],
        input_output_aliases={i: i for i in range(2 * m)},
        compiler_params=pltpu.CompilerParams(has_side_effects=DATAFLOW),
    )(*bufs, *pick("send"), *pick("recv"), after)
    return outs[:m], outs[m:]


def _swap_halves(name, lands):
    n = len(lands)

    def body(*refs):
        bufs = refs[n:2 * n]
        send, recv = refs[2 * n:]
        x, y, c, chips = _chip_peers()
        started = []
        for w in range(n):
            half = bufs[w].shape[1] // 2
            for p, (px, py) in enumerate(chips):
                mine = bufs[w].at[2 * px + py, pl.ds(c * half, half)]
                cp = pltpu.make_async_remote_copy(
                    src_ref=mine, dst_ref=mine, send_sem=send.at[w, p], recv_sem=recv.at[w, p],
                    device_id=(x, y, 1 - c), device_id_type=MESH)
                cp.start()
                started.append(cp)
        for w in range(n):
            half = bufs[w].shape[1] // 2
            for p, (px, py) in enumerate(chips):
                theirs = bufs[w].at[2 * px + py, pl.ds((1 - c) * half, half)]
                pltpu.make_async_remote_copy(
                    src_ref=theirs, dst_ref=theirs, send_sem=send.at[w, p], recv_sem=recv.at[w, p],
                    device_id=(x, y, 1 - c), device_id_type=MESH).wait_recv()
        for cp in started:
            cp.wait_send()

    return pl.pallas_call(
        body, name=name, in_specs=[ANY_SPEC] * n, out_specs=[ANY_SPEC] * n,
        out_shape=[jax.ShapeDtypeStruct(a.shape, a.dtype) for a in lands],
        input_output_aliases={i: i for i in range(n)},
        scratch_shapes=[pltpu.SemaphoreType.DMA((n, 3)), pltpu.SemaphoreType.DMA((n, 3))],
    )(*lands)


def _place_own(name, shard, land, me):
    r, c = shard.shape
    tm = min(r, 512)

    def body(me_ref, shard_ref, land_ref, out_ref):
        out_ref[...] = shard_ref[...]

    return pl.pallas_call(
        body, name=name,
        grid_spec=pltpu.PrefetchScalarGridSpec(
            num_scalar_prefetch=1, grid=(r // tm,),
            in_specs=[pl.BlockSpec((tm, c), lambda i, me_ref: (i, 0)), ANY_SPEC],
            out_specs=pl.BlockSpec((None, tm, c), lambda i, me_ref: (me_ref[0], i, 0))),
        out_shape=jax.ShapeDtypeStruct(land.shape, land.dtype), input_output_aliases={2: 0},
        compiler_params=_params(("arbitrary",)),
    )(me, shard, land)


def _sum_slabs(name, grads, land, me):
    _, r, c = grads.shape
    tm = min(r, 256)

    def body(me_ref, own_ref, land_ref, out_ref):
        acc = own_ref[...].astype(F32)
        for p in range(3):
            acc = acc + land_ref[p].astype(F32)
        out_ref[...] = acc

    return pl.pallas_call(
        body, name=name,
        grid_spec=pltpu.PrefetchScalarGridSpec(
            num_scalar_prefetch=1, grid=(r // tm,),
            in_specs=[pl.BlockSpec((None, tm, c), lambda i, me_ref: (me_ref[0], i, 0)),
                      pl.BlockSpec((3, tm, c), lambda i, me_ref: (0, i, 0))],
            out_specs=pl.BlockSpec((tm, c), lambda i, me_ref: (i, 0))),
        out_shape=jax.ShapeDtypeStruct((r, c), F32), compiler_params=_params(("parallel",)),
    )(me, grads, land)


def _allreduce_stats(stats, run_after):
    n = len(stats)

    def body(*refs):
        ins, outs = refs[:n], refs[n + 1:2 * n + 1]
        mine, gath = refs[2 * n + 1:3 * n + 1], refs[3 * n + 1:4 * n + 1]
        send, recv = refs[4 * n + 1:]
        x, y, c = lax.axis_index("x"), lax.axis_index("y"), lax.axis_index("c")
        me = 4 * x + 2 * y + c
        flips = [(bx, by, bc) for bx in (0, 1) for by in (0, 1) for bc in (0, 1)][1:]

        def peer(f):
            return (x + f[0] * (1 - 2 * x), y + f[1] * (1 - 2 * y), c + f[2] * (1 - 2 * c))

        copies = []
        for t in range(n):
            tot = ins[t][0]
            for b in range(1, ins[t].shape[0]):
                tot = tot + ins[t][b]
            mine[t][...] = tot
            gath[t][me] = tot
            for k, f in enumerate(flips):
                cp = pltpu.make_async_remote_copy(
                    src_ref=mine[t], dst_ref=gath[t].at[me], send_sem=send.at[t, k],
                    recv_sem=recv.at[t, k], device_id=peer(f), device_id_type=MESH)
                cp.start()
                copies.append(cp)
        for t in range(n):
            for k, f in enumerate(flips):
                px, py, pc = peer(f)
                pltpu.make_async_remote_copy(
                    src_ref=mine[t], dst_ref=gath[t].at[4 * px + 2 * py + pc], send_sem=send.at[t, k],
                    recv_sem=recv.at[t, k], device_id=(px, py, pc), device_id_type=MESH).wait_recv()
        for cp in copies:
            cp.wait_send()
        for t in range(n):
            tot = gath[t][0]
            for dev in range(1, 8):
                tot = tot + gath[t][dev]
            outs[t][...] = tot

    vm = pl.BlockSpec(memory_space=pltpu.VMEM)
    return pl.pallas_call(
        body, name="allreduce_stats", in_specs=[vm] * n + [ANY_SPEC], out_specs=[vm] * n,
        out_shape=[jax.ShapeDtypeStruct(s.shape[1:], F32) for s in stats],
        scratch_shapes=[pltpu.VMEM(s.shape[1:], F32) for s in stats]
        + [pltpu.VMEM((8, *s.shape[1:]), F32) for s in stats]
        + [pltpu.SemaphoreType.DMA((n, 7)), pltpu.SemaphoreType.DMA((n, 7))],
    )(*stats, run_after)


def _adamw(name, w, m, v, g_parts):
    r, c = w.shape
    tm = min(r, 128)
    n_g = len(g_parts)

    def body(*refs):
        w_ref, m_ref, v_ref = refs[:3]
        g_refs = refs[3:3 + n_g]
        g_out, d_out, m_out, v_out = refs[3 + n_g:]
        g = g_refs[0][...]
        for gr in g_refs[1:]:
            g = g + gr[...]
        m_new = ADAM_B1 * m_ref[...] + (1.0 - ADAM_B1) * g
        v_new = ADAM_B2 * v_ref[...] + (1.0 - ADAM_B2) * (g * g)
        m_hat = m_new / (1.0 - ADAM_B1 ** ADAM_STEP)
        v_hat = v_new / (1.0 - ADAM_B2 ** ADAM_STEP)
        g_out[...] = g
        d_out[...] = -ADAM_LR * (m_hat / (jnp.sqrt(v_hat) + ADAM_EPS) + ADAM_WD * w_ref[...])
        m_out[...] = m_new
        v_out[...] = v_new

    blk = pl.BlockSpec((tm, c), lambda i: (i, 0))
    out = jax.ShapeDtypeStruct((r, c), F32)
    return pl.pallas_call(
        body, name=name, grid=(r // tm,), in_specs=[blk] * (3 + n_g), out_specs=[blk] * 4,
        out_shape=[out] * 4, compiler_params=_params(("parallel",)),
    )(w, m, v, *g_parts)


def _rope_tables(positions):
    half = HEAD_DIM // 2
    inv_freq = ROPE_THETA ** (-jnp.arange(half, dtype=F32) / half)
    ang = positions.astype(F32)[0, :, None] * inv_freq
    cos, sin = jnp.cos(ang), jnp.sin(ang)
    return jnp.concatenate([cos, cos], axis=-1), jnp.concatenate([-sin, sin], axis=-1)


def kernel(x, positions, w_in, w_pool, pool_scale, w_branch_attn, w_branch_pool, w_out, ln_mix_g, ln_mix_b, w_ff1, w_ff2, ln_ff_g, ln_ff_b, loss_target, m_w_in, m_w_pool, m_pool_scale, m_w_branch_attn, m_w_branch_pool, m_w_out, m_ln_mix_g, m_ln_mix_b, m_w_ff1, m_w_ff2, m_ln_ff_g, m_ln_ff_b, v_w_in, v_w_pool, v_pool_scale, v_w_branch_attn, v_w_branch_pool, v_w_out, v_ln_mix_g, v_ln_mix_b, v_w_ff1, v_w_ff2, v_ln_ff_g, v_ln_ff_b):
    s, d = x.shape[1], x.shape[2]
    aw = d // 2
    ng = len(POOL_WINDOWS)
    pgw = aw // ng
    x2d, target = x[0], loss_target[0]
    xb = x2d.astype(BF16)
    cos_t, sin_t = _rope_tables(positions)

    big = {"w_in": w_in[0], "w_pool": w_pool[0].reshape(-1, pgw), "w_branch_attn": w_branch_attn[0],
           "w_branch_pool": w_branch_pool[0], "w_out": w_out[0], "w_ff1": w_ff1[0], "w_ff2": w_ff2[0]}
    names = list(big)
    me_chip = (2 * lax.axis_index("x") + lax.axis_index("y")).astype(jnp.int32).reshape(1)
    shards = [big[k].astype(BF16) for k in names]
    land_shapes = [(N_CHIPS, *a.shape) for a in shards]
    gathering_in = _exchange_start("gather_start_in", GATHER_HALF, shards[:1], land_shapes[:1])
    gathering = _exchange_start("gather_start", GATHER, shards[1:], land_shapes[1:])

    def gathered(name, which, after):
        srcs, lands = _exchange_wait(f"gather_wait_{name}", GATHER, gathering, which, after)
        return [_place_own(f"place_own_{names[w + 1]}", srcs[i], lands[i], me_chip)
                for i, w in enumerate(which)]

    rows_sh = pgw // N_CHIPS
    dff = N_CHIPS * big["w_ff2"].shape[0]

    srcs, lands = _exchange_wait("gather_wait_in", GATHER_HALF, gathering_in, [0], gathering["token"])
    w_in_st = _place_own("place_own_w_in", srcs[0], _swap_halves("swap_halves_in", lands)[0], me_chip)
    h, hv4, hv16 = _in_proj(xb, w_in_st, cos_t, sin_t, aw)
    (wp_st,) = gathered("pool", [0], h)
    wp = wp_st.reshape(N_CHIPS, ng, rows_sh, pgw).transpose(1, 0, 2, 3).reshape(ng, pgw, pgw)
    qkv = {1: (h, h, h), 4: (hv4, hv4, hv4), 16: (hv16, hv16, hv16)}
    offs = {dil: (0, dil, 2 * dil) for dil in DILATIONS}
    o_parts, lse_parts = [], []
    for dil in DILATIONS:
        o_p, lse_p = _attn_fwd(f"attn_fwd_d{dil}", *qkv[dil], offs[dil], aw, dil * aw)
        o_parts.append(o_p)
        lse_parts.append(lse_p)
    o_attn, lse, lse4, lse16 = _attn_combine(o_parts[0], lse_parts[0], o_parts[1:], lse_parts[1:])
    p, pw, y = _pool_fwd(h, wp, pool_scale, aw)
    wba_st, wbp_st, w_out_st = gathered("mix", [1, 2, 3], y)
    w_out_full = w_out_st.reshape(d, d)
    ya, yp, merged = _branch_merge(o_attn, y, wba_st, wbp_st, h, aw, d)
    xhat1, rstd1, x1b = _mix_norm(merged, w_out_full, x2d, ln_mix_g, ln_mix_b)
    w1_st, w2_st = gathered("ff", [4, 5], x1b)
    w2_full = w2_st.reshape(dff, d)
    r = _ff_up(x1b, w1_st)
    dz2, dz2b, st2 = _ff_down_loss(r, w2_full, xhat1, ln_mix_g, ln_mix_b, ln_ff_g, ln_ff_b, target)

    def scatter_start(name, grads):
        return _exchange_start(f"scatter_start_{name}", SCATTER, grads, [(3, *g.shape[1:]) for g in grads])

    da = _ff_down_bwd(dz2b, w2_full, r)
    g_w2 = _wgrad("wgrad_ff2", r, dz2b, d).reshape(N_CHIPS, dff // N_CHIPS, d)
    g_w1 = _wgrad("wgrad_ff1", x1b, da, dff // N_CHIPS)
    sent_ff = scatter_start("ff", [g_w1, g_w2])
    dz1, dz1b, st1 = _ff_up_bwd(da, w1_st, dz2, xhat1, rstd1, ln_mix_g, [sent_ff["token"]])
    dya, dyp, dgate = _mix_bwd(dz1b, w_out_full, h, ya, yp, aw)
    g_wout = _wgrad("wgrad_out", merged, dz1b, d).reshape(N_CHIPS, d // N_CHIPS, d)
    g_wba = _wgrad("wgrad_branch_attn", o_attn, dya, d // N_CHIPS)
    g_wbp = _wgrad("wgrad_branch_pool", y, dyp, d // N_CHIPS)
    do, delta, do4, do16, delta4, delta16 = _attn_out_bwd(dya, wba_st, o_attn)
    dpw, stp = _pool_out_bwd(dyp, wbp_st, pw, pool_scale)
    dwp, du = _pool_bwd(dpw, p, wp)
    g_wp = dwp.reshape(ng, N_CHIPS, rows_sh, pgw).transpose(1, 0, 2, 3).reshape(
        N_CHIPS, ng * rows_sh, pgw).astype(BF16)
    sent_mix = scatter_start("mix", [g_wp, g_wba, g_wbp, g_wout])

    bwd_in = {1: (do, lse, delta), 4: (do4, lse4, delta4), 16: (do16, lse16, delta16)}
    dqkv = {}
    for dil in DILATIONS:
        args = (*qkv[dil], *bwd_in[dil], offs[dil], aw, dil * aw)
        dqkv[dil] = _attn_bwd(f"attn_bwd_d{dil}", *args)
    dqkvu = _qkvu_grad(dqkv[1], dqkv[4], dqkv[16], du, cos_t, sin_t)
    g_win_a = _wgrad("wgrad_in_qkvu", xb, dqkvu, d)
    g_win_b = _wgrad("wgrad_in_gates", xb, dgate, d)
    g_win = jnp.concatenate([g_win_a, g_win_b], axis=0)
    sent_in = scatter_start("in", [g_win])
    dx_a = _in_proj_bwd_x("in_proj_bwd_qkvu", dqkvu, w_in_st, 0, dz1, ALPHA,
                          [sent_mix["token"], sent_in["token"]])
    grad_x = _in_proj_bwd_x("in_proj_bwd_gates", dgate, w_in_st, 2, dx_a, 1.0)

    moments = {"w_in": (m_w_in, v_w_in), "w_pool": (m_w_pool, v_w_pool),
               "w_branch_attn": (m_w_branch_attn, v_w_branch_attn),
               "w_branch_pool": (m_w_branch_pool, v_w_branch_pool), "w_out": (m_w_out, v_w_out),
               "w_ff1": (m_w_ff1, v_w_ff1), "w_ff2": (m_w_ff2, v_w_ff2)}
    originals = {"w_in": w_in, "w_pool": w_pool, "w_branch_attn": w_branch_attn,
                 "w_branch_pool": w_branch_pool, "w_out": w_out, "w_ff1": w_ff1, "w_ff2": w_ff2}
    res = {}

    def summed(name, sent, keys, after):
        srcs, lands = _exchange_wait(f"scatter_wait_{name}", SCATTER, sent, list(range(len(keys))), after)
        parts = [_sum_slabs(f"sum_slabs_{k}", srcs[i], lands[i], me_chip) for i, k in enumerate(keys)]
        return _exchange_start(f"cores_start_{name}", SIBLING, parts, [a.shape for a in parts])

    def updated(name, swapping, keys, after):
        mine, other = _exchange_wait(f"cores_wait_{name}", SIBLING, swapping, list(range(len(keys))), after)
        for i, k in enumerate(keys):
            mk, vk = (a.reshape(big[k].shape) for a in moments[k])
            outs = _adamw(f"adamw_{k}", big[k], mk, vk, [mine[i], other[i]])
            res[k] = [o.reshape(originals[k].shape) for o in outs]

    groups = {"ff": ["w_ff1", "w_ff2"], "mix": ["w_pool", "w_branch_attn", "w_branch_pool", "w_out"],
              "in": ["w_in"]}
    swap_ff = summed("ff", sent_ff, groups["ff"], grad_x)
    swap_mix = summed("mix", sent_mix, groups["mix"], swap_ff["token"])
    swap_in = summed("in", sent_in, groups["in"], swap_mix["token"])
    updated("ff", swap_ff, groups["ff"], swap_in["token"])
    updated("mix", swap_mix, groups["mix"], res["w_ff2"][0])
    updated("in", swap_in, groups["in"], res["w_out"][0])
    tot2, tot1, totp = _allreduce_stats([st2, st1, stp], res["w_in"][0])

    def pad_d(a):
        return jnp.pad(a, ((0, 0), (0, d - a.shape[1])))

    small = ["ln_mix_g", "ln_mix_b", "ln_ff_g", "ln_ff_b", "pool_scale"]
    small_w = {"ln_mix_g": ln_mix_g, "ln_mix_b": ln_mix_b, "ln_ff_g": ln_ff_g, "ln_ff_b": ln_ff_b,
               "pool_scale": pool_scale}
    small_m = {"ln_mix_g": m_ln_mix_g, "ln_mix_b": m_ln_mix_b, "ln_ff_g": m_ln_ff_g,
               "ln_ff_b": m_ln_ff_b, "pool_scale": m_pool_scale}
    small_v = {"ln_mix_g": v_ln_mix_g, "ln_mix_b": v_ln_mix_b, "ln_ff_g": v_ln_ff_g,
               "ln_ff_b": v_ln_ff_b, "pool_scale": v_pool_scale}
    small_g = [tot1[0:1], tot1[1:2], tot2[0:1], tot2[1:2], pad_d(totp[0:1])]

    def pack(rows):
        return jnp.concatenate([pad_d(a) for a in rows] + [jnp.zeros((8 - len(rows), d), F32)], axis=0)

    outs = _adamw("adamw_small", pack([small_w[k] for k in small]), pack([small_m[k] for k in small]),
                  pack([small_v[k] for k in small]), [pack(small_g)])
    for i, k in enumerate(small):
        res[k] = [o[i:i + 1, :small_w[k].shape[1]] for o in outs]
    loss = tot2[2, 0]

    order = ["w_in", "w_pool", "pool_scale", "w_branch_attn", "w_branch_pool", "w_out", "ln_mix_g",
             "ln_mix_b", "w_ff1", "w_ff2", "ln_ff_g", "ln_ff_b"]
    result = [loss, grad_x[None]]
    for idx in range(4):
        result += [res[k][idx] for k in order]
    return tuple(result)
```

```python
import jax
import jax.numpy as jnp
from jax import lax
from jax.experimental import pallas as pl
from jax.experimental.pallas import tpu as pltpu

F32 = jnp.float32
BF16 = jnp.bfloat16
MESH = pl.DeviceIdType.MESH

HEAD_DIM = 128
SUB_BLOCK = 128
DILATIONS = (1, 4, 16)
POOL_WINDOWS = (2, 4, 8, 16)
POOL_HALO = 16
ROPE_THETA = 10000.0
LN_EPS = 1e-5
ALPHA = 2.0 ** 0.25
ADAM_LR, ADAM_B1, ADAM_B2, ADAM_EPS, ADAM_WD, ADAM_STEP = 0.001, 0.9, 0.999, 1e-08, 0.01, 10
NEG = -1e30
N_CHIPS = 4
VMEM_LIMIT = 56 * 1024 * 1024
EPILOGUE_ROWS = 128


def _params(sem=None, vmem=VMEM_LIMIT):
    kw = {"vmem_limit_bytes": vmem}
    if sem is not None:
        kw["dimension_semantics"] = sem
    return pltpu.CompilerParams(**kw)


def _dot(a, b, contract):
    return lax.dot_general(a, b, (contract, ((), ())), preferred_element_type=F32)


ANY_SPEC = pl.BlockSpec(memory_space=pl.ANY)
HBM_SPEC = pl.BlockSpec(memory_space=pltpu.HBM)
SEM_SPEC = pl.BlockSpec(memory_space=pltpu.SEMAPHORE)
DATAFLOW = pltpu.SideEffectType.DATAFLOW_SIDE_EFFECTING


def _hbm(a):
    return pltpu.with_memory_space_constraint(a, pltpu.HBM)


NN = ((1,), (0,))
NT = ((1,), (1,))
TN = ((0,), (0,))


def _mm(name, grid, a, a_spec, b, b_spec, contract, extras, extra_specs, out_shape, out_specs,
        epilogue, acc_shape, acc_as_ref=False, run_after=(), scratch=(),
        semantics=("parallel", "parallel", "arbitrary"), fill=None):
    nk = grid[2]
    n_ex = len(extras)
    n_in = 2 + n_ex + len(run_after) + (fill is not None)
    n_out = len(out_shape)
    n_scr = len(scratch)

    def body(*refs):
        a_ref, b_ref = refs[0], refs[1]
        ex = refs[2:2 + n_ex]
        outs = refs[n_in:n_in + n_out]
        scr = refs[n_in + n_out:n_in + n_out + n_scr]
        if nk == 1:
            epilogue(_dot(a_ref[...], b_ref[...], contract), ex, outs, *scr)
        else:
            acc = refs[n_in + n_out + n_scr]
            k = pl.program_id(2)

            @pl.when(k == 0)
            def _():
                acc[...] = jnp.zeros_like(acc)

            acc[...] += _dot(a_ref[...], b_ref[...], contract)

            @pl.when(k == nk - 1)
            def _():
                epilogue(acc if acc_as_ref else acc[...], ex, outs, *scr)

    acc_scratch = [pltpu.VMEM(acc_shape, F32)] if nk > 1 else []
    filled = [] if fill is None else [fill]
    return pl.pallas_call(
        body, name=name, grid=grid,
        in_specs=[a_spec, b_spec, *extra_specs, *[ANY_SPEC] * (len(run_after) + len(filled))],
        out_specs=out_specs, out_shape=out_shape, scratch_shapes=[*scratch, *acc_scratch],
        input_output_aliases={} if fill is None else {n_in - 1: 0},
        compiler_params=_params(semantics),
    )(a, b, *extras, *run_after, *filled)


def _stats_rows(rows, width):
    idx = lax.broadcasted_iota(jnp.int32, (8, width), 0)
    out = jnp.zeros((8, width), F32)
    for r, v in enumerate(rows):
        out = jnp.where(idx == r, jnp.broadcast_to(v, (8, width)), out)
    return out


def _layer_norm_fwd(z):
    mu = jnp.mean(z, axis=-1, keepdims=True)
    zc = z - mu
    var = jnp.mean(zc * zc, axis=-1, keepdims=True)
    rstd = lax.rsqrt(var + LN_EPS)
    return zc * rstd, rstd


def _layer_norm_bwd(dy, xhat, rstd, g):
    dxh = dy * g
    m1 = jnp.mean(dxh, axis=-1, keepdims=True)
    m2 = jnp.mean(dxh * xhat, axis=-1, keepdims=True)
    return rstd * (dxh - m1 - xhat * m2)


def _heads_scratch(rows, width):
    return pltpu.VMEM((width // HEAD_DIM, rows, HEAD_DIM), F32)


def _to_views(src_ref, view_refs, dtype):
    nh, rows, _ = src_ref.shape
    width = nh * HEAD_DIM
    for dil, view_ref in zip(DILATIONS[1:], view_refs):
        for r in range(dil):
            for hh in range(nh):
                c0 = r * width + hh * HEAD_DIM
                view_ref[:, c0:c0 + HEAD_DIM] = (
                    src_ref[hh, pl.ds(r, rows // dil, stride=dil), :].astype(dtype))


def _from_views(view_refs, dst_refs):
    nh, rows, _ = dst_refs[0].shape
    width = nh * HEAD_DIM
    for dil, view_ref, dst_ref in zip(DILATIONS[1:], view_refs, dst_refs):
        for r in range(dil):
            for hh in range(nh):
                c0 = r * width + hh * HEAD_DIM
                dst_ref[hh, pl.ds(r, rows // dil, stride=dil), :] = (
                    view_ref[:, c0:c0 + HEAD_DIM].astype(F32))


def _view_shape(rows, width, dil, parts=1):
    return (rows // dil, parts * dil * width)


def _in_proj(xb, w_in_st, cos_t, sin_t, aw):
    s, d = xb.shape
    n_sh = w_in_st.shape[2]
    tm, tn = min(s, 1024), aw
    per = n_sh // tn
    grid = (s // tm, (N_CHIPS * n_sh) // tn, 1)

    def epilogue(acc, ex, outs, scr):
        cos_ref, sin_ref = ex
        h_ref, v4_ref, v16_ref = outs
        seg = pl.program_id(1)

        heads = [slice(hh * HEAD_DIM, (hh + 1) * HEAD_DIM) for hh in range(tn // HEAD_DIM)]

        @pl.when(seg < 2)
        def _():
            cos, sin = cos_ref[...], sin_ref[...]
            for hh, sl in enumerate(heads):
                t = acc[:, sl]
                scr[hh] = t * cos + pltpu.roll(t, HEAD_DIM // 2, 1) * sin

        @pl.when(seg == 2)
        def _():
            for hh, sl in enumerate(heads):
                scr[hh] = acc[:, sl]

        @pl.when(seg < 3)
        def _():
            for hh, sl in enumerate(heads):
                h_ref[:, sl] = scr[hh].astype(BF16)
            _to_views(scr, (v4_ref, v16_ref), BF16)

        @pl.when(seg == 3)
        def _():
            h_ref[...] = acc.astype(BF16)

        @pl.when(seg >= 4)
        def _():
            h_ref[...] = jax.nn.sigmoid(acc).astype(BF16)

    def view_spec(dil):
        return pl.BlockSpec((tm // dil, dil * aw), lambda i, j, k: (i, jnp.minimum(j, 2)))

    return _mm(
        "in_proj", grid, xb, pl.BlockSpec((tm, d), lambda i, j, k: (i, 0)),
        w_in_st, pl.BlockSpec((None, d, tn), lambda i, j, k: (j // per, 0, j % per)), NN,
        [cos_t, sin_t], [pl.BlockSpec((tm, HEAD_DIM), lambda i, j, k: (i, 0))] * 2,
        [jax.ShapeDtypeStruct((s, N_CHIPS * n_sh), BF16)]
        + [jax.ShapeDtypeStruct(_view_shape(s, aw, dil, 3), BF16) for dil in DILATIONS[1:]],
        [pl.BlockSpec((tm, tn), lambda i, j, k: (i, j))] + [view_spec(dil) for dil in DILATIONS[1:]],
        epilogue, None, scratch=[_heads_scratch(tm, aw)],
        semantics=("parallel", "arbitrary", "arbitrary"))


def _pack_heads(cols):
    rows, rep = cols[0].shape[0], HEAD_DIM // len(cols)
    lane = lax.broadcasted_iota(jnp.int32, (rows, HEAD_DIM), 1)
    out = jnp.zeros((rows, HEAD_DIM), F32)
    for hh, col in enumerate(cols):
        out = jnp.where((lane >= hh * rep) & (lane < (hh + 1) * rep), col, out)
    return out


def _head_col(packed, hh, nh):
    lane = lax.broadcasted_iota(jnp.int32, packed.shape, 1)
    return jnp.sum(jnp.where(lane == hh * (HEAD_DIM // nh), packed, 0.0), axis=-1, keepdims=True)


def _band_masks(block_idx):
    qi = lax.broadcasted_iota(jnp.int32, (SUB_BLOCK, 2 * SUB_BLOCK), 0)
    kj = lax.broadcasted_iota(jnp.int32, (SUB_BLOCK, 2 * SUB_BLOCK), 1)
    first_key = jnp.where(block_idx > 0, 0, SUB_BLOCK)
    return (kj >= qi) & (kj <= qi + SUB_BLOCK) & (kj >= first_key)


def _attn_fwd(name, q, k, v, offs, cw, width):
    m = q.shape[0]
    nh = cw // HEAD_DIM
    grid = (width // cw, m // SUB_BLOCK)
    scale = HEAD_DIM ** -0.5

    def body(q_ref, k_ref, v_ref, o_ref, lse_ref, kprev, vprev):
        n = pl.program_id(1)
        valid = _band_masks(n)

        @pl.when(n == 0)
        def _():
            kprev[...] = jnp.zeros_like(kprev)
            vprev[...] = jnp.zeros_like(vprev)

        heads = [slice(hh * HEAD_DIM, (hh + 1) * HEAD_DIM) for hh in range(nh)]
        scs = [_dot(q_ref[:, sl], jnp.concatenate([kprev[:, sl], k_ref[:, sl]], axis=0), NT)
               for sl in heads]
        lses = []
        for hh, sl in enumerate(heads):
            vcat = jnp.concatenate([vprev[:, sl], v_ref[:, sl]], axis=0)
            sc = jnp.where(valid, scs[hh] * scale, NEG)
            mx = jnp.max(sc, axis=-1, keepdims=True)
            p = jnp.exp(sc - mx)
            l = jnp.sum(p, axis=-1, keepdims=True)
            o = _dot(p.astype(BF16), vcat, NN) / l
            o_ref[:, sl] = o.astype(BF16)
            lses.append(mx + jnp.log(l))
        lse_ref[...] = _pack_heads(lses)
        kprev[...] = k_ref[...]
        vprev[...] = v_ref[...]

    def cur(off):
        return pl.BlockSpec((SUB_BLOCK, cw), lambda j, n: (n, j + off))

    return pl.pallas_call(
        body, name=name, grid=grid,
        in_specs=[cur(offs[0]), cur(offs[1]), cur(offs[2])],
        out_specs=[cur(0), pl.BlockSpec((SUB_BLOCK, HEAD_DIM), lambda j, n: (n, j))],
        out_shape=[jax.ShapeDtypeStruct((m, width), BF16),
                   jax.ShapeDtypeStruct((m, width // cw * HEAD_DIM), F32)],
        scratch_shapes=[pltpu.VMEM((SUB_BLOCK, cw), BF16)] * 2,
        compiler_params=_params(("parallel", "arbitrary")),
    )(q, k, v)


def _view_spec(tm, aw, dil):
    return pl.BlockSpec((tm // dil, dil * aw), lambda i: (i, 0))


def _attn_combine(o1, l1, o_views, l_views):
    s, aw = o1.shape
    nh = aw // HEAD_DIM
    tm = min(s, 512)

    def body(o1_ref, l1_ref, o4_ref, o16_ref, l4_ref, l16_ref, o_ref, lse_ref, lse4_ref, lse16_ref,
             so4, so16, sl4, sl16, stot):
        _from_views((o4_ref, o16_ref), (so4, so16))
        _from_views((l4_ref, l16_ref), (sl4, sl16))
        a, b, c = l1_ref[...], sl4[0], sl16[0]
        mx = jnp.maximum(jnp.maximum(a, b), c)
        ea, eb, ec = jnp.exp(a - mx), jnp.exp(b - mx), jnp.exp(c - mx)
        tot = ea + eb + ec
        inv = 1.0 / tot
        wa, wb, wc = ea * inv, eb * inv, ec * inv
        lse_tot = mx + jnp.log(tot)
        stot[0] = lse_tot
        lse_ref[...] = lse_tot
        _to_views(stot, (lse4_ref, lse16_ref), F32)
        for hh in range(nh):
            sl = slice(hh * HEAD_DIM, (hh + 1) * HEAD_DIM)
            o = (_head_col(wa, hh, nh) * o1_ref[:, sl].astype(F32) + _head_col(wb, hh, nh) * so4[hh]
                 + _head_col(wc, hh, nh) * so16[hh])
            o_ref[:, sl] = o.astype(BF16)

    row = pl.BlockSpec((tm, aw), lambda i: (i, 0))
    stat = pl.BlockSpec((tm, HEAD_DIM), lambda i: (i, 0))
    views = [_view_spec(tm, aw, dil) for dil in DILATIONS[1:]]
    stat_views = [_view_spec(tm, HEAD_DIM, dil) for dil in DILATIONS[1:]]
    return pl.pallas_call(
        body, name="attn_combine", grid=(s // tm,), in_specs=[row, stat, *views, *stat_views],
        out_specs=[row, stat, *stat_views],
        out_shape=[jax.ShapeDtypeStruct((s, aw), BF16), jax.ShapeDtypeStruct((s, HEAD_DIM), F32)]
        + [jax.ShapeDtypeStruct(_view_shape(s, HEAD_DIM, dil), F32) for dil in DILATIONS[1:]],
        scratch_shapes=[_heads_scratch(tm, aw)] * 2 + [_heads_scratch(tm, HEAD_DIM)] * 3,
        compiler_params=_params(("parallel",)),
    )(o1, l1, *o_views, *l_views)


def _pool_counts(tm, rows, pgw, row0):
    t = lax.broadcasted_iota(jnp.int32, (rows, len(POOL_WINDOWS) * pgw), 0) + row0
    col = lax.broadcasted_iota(jnp.int32, (rows, len(POOL_WINDOWS) * pgw), 1)
    w = jnp.full((rows, len(POOL_WINDOWS) * pgw), POOL_WINDOWS[0], jnp.int32)
    for g in range(1, len(POOL_WINDOWS)):
        w = jnp.where(col >= g * pgw, POOL_WINDOWS[g], w)
    return jnp.minimum(t + 1, w).astype(F32)


def _window_sums(xs, direction, pgw):
    rows = xs.shape[0]
    acc = xs
    out = None
    col = lax.broadcasted_iota(jnp.int32, xs.shape, 1)
    for g, w in enumerate(POOL_WINDOWS):
        sh = w // 2
        acc = acc + pltpu.roll(acc, sh if direction > 0 else rows - sh, 0)
        out = acc if out is None else jnp.where(col >= g * pgw, acc, out)
    return out


def _pool_fwd(h, wp, scale, aw):
    s = h.shape[0]
    pw_ = aw
    pgw = pw_ // len(POOL_WINDOWS)
    tm = min(s, 512)
    hb = tm // POOL_HALO

    def body(u_ref, halo_ref, wp_ref, sc_ref, p_ref, pw_ref, y_ref):
        i = pl.program_id(0)
        u = u_ref[...].astype(F32)
        halo = halo_ref[...].astype(F32) * jnp.where(i > 0, 1.0, 0.0)
        xs = jnp.concatenate([halo, u], axis=0)
        sums = _window_sums(xs, +1, pgw)[POOL_HALO:]
        p = (sums / _pool_counts(tm, tm, pgw, i * tm) - u).astype(BF16)
        p_ref[...] = p
        sc = sc_ref[...]
        for g in range(len(POOL_WINDOWS)):
            sl = slice(g * pgw, (g + 1) * pgw)
            pw = _dot(p[:, sl], wp_ref[g], NN)
            pw_ref[:, sl] = pw.astype(BF16)
            y_ref[:, sl] = (pw * sc[:, sl]).astype(BF16)

    out = jax.ShapeDtypeStruct((s, pw_), BF16)
    row = pl.BlockSpec((tm, pw_), lambda i: (i, 0))
    return pl.pallas_call(
        body, name="pool_fwd", grid=(s // tm,),
        in_specs=[pl.BlockSpec((tm, pw_), lambda i: (i, 3)),
                  pl.BlockSpec((POOL_HALO, pw_), lambda i: (jnp.maximum(i * hb - 1, 0), 3)),
                  pl.BlockSpec(wp.shape, lambda i: (0, 0, 0)),
                  pl.BlockSpec((1, pw_), lambda i: (0, 0))],
        out_specs=[row, row, row], out_shape=[out, out, out],
        compiler_params=_params(("parallel",)),
    )(h, h, wp, scale)


def _branch_merge(o_attn, y, wba_st, wbp_st, h, aw, d):
    s = o_attn.shape[0]
    tn = wba_st.shape[2]
    tm = min(s, 1024)
    ga0 = 4 * aw // tn
    gp0 = (4 * aw + d) // tn

    def body(o_ref, y_ref, wa_ref, wp_ref, sga_ref, sgp_ref, ya_ref, yp_ref, mg_ref):
        ya = _dot(o_ref[...], wa_ref[...], NN)
        yp = _dot(y_ref[...], wp_ref[...], NN)
        ya_ref[...] = ya.astype(BF16)
        yp_ref[...] = yp.astype(BF16)
        mg_ref[...] = (sga_ref[...].astype(F32) * ya + sgp_ref[...].astype(F32) * yp).astype(BF16)

    out = jax.ShapeDtypeStruct((s, d), BF16)
    blk = pl.BlockSpec((tm, tn), lambda i, j: (i, j))
    return pl.pallas_call(
        body, name="branch_merge", grid=(s // tm, N_CHIPS),
        in_specs=[pl.BlockSpec((tm, aw), lambda i, j: (i, 0)),
                  pl.BlockSpec((tm, aw), lambda i, j: (i, 0)),
                  pl.BlockSpec((None, aw, tn), lambda i, j: (j, 0, 0)),
                  pl.BlockSpec((None, aw, tn), lambda i, j: (j, 0, 0)),
                  pl.BlockSpec((tm, tn), lambda i, j: (i, j + ga0)),
                  pl.BlockSpec((tm, tn), lambda i, j: (i, j + gp0))],
        out_specs=[blk, blk, blk], out_shape=[out, out, out],
        compiler_params=_params(("parallel", "parallel")),
    )(o_attn, y, wba_st, wbp_st, h, h)


def _mix_norm(merged, w_out, x, g1, b1):
    s, d = x.shape
    tm = min(s, 256)

    def epilogue(acc, ex, outs):
        x_ref, g_ref, b_ref = ex
        xh_ref, rs_ref, xb_ref = outs
        xhat, rstd = _layer_norm_fwd(ALPHA * x_ref[...] + acc)
        xh_ref[...] = xhat
        rs_ref[...] = rstd
        xb_ref[...] = (xhat * g_ref[...] + b_ref[...]).astype(BF16)

    row = pl.BlockSpec((tm, d), lambda i, j, k: (i, 0))
    vec = pl.BlockSpec((1, d), lambda i, j, k: (0, 0))
    return _mm(
        "mix_norm", (s // tm, 1, 1), merged, row, w_out, pl.BlockSpec((d, d), lambda i, j, k: (0, 0)),
        NN, [x, g1, b1], [row, vec, vec],
        [jax.ShapeDtypeStruct((s, d), F32), jax.ShapeDtypeStruct((s, 1), F32),
         jax.ShapeDtypeStruct((s, d), BF16)],
        [row, pl.BlockSpec((tm, 1), lambda i, j, k: (i, 0)), row], epilogue, None)


def _ff_up(x1b, w1_st):
    s, d = x1b.shape
    n_sh = w1_st.shape[2]
    tm, tn = min(s, 1024), min(n_sh, 1024)
    per = n_sh // tn

    def epilogue(acc, ex, outs):
        r = jnp.maximum(acc, 0.0)
        outs[0][...] = (r * r).astype(BF16)

    return _mm(
        "ff_up", (s // tm, N_CHIPS * per, 1), x1b, pl.BlockSpec((tm, d), lambda i, j, k: (i, 0)),
        w1_st, pl.BlockSpec((None, d, tn), lambda i, j, k: (j // per, 0, j % per)), NN, [], [],
        [jax.ShapeDtypeStruct((s, N_CHIPS * n_sh), BF16)],
        [pl.BlockSpec((tm, tn), lambda i, j, k: (i, j))], epilogue, None)[0]


def _ff_down_loss(r, w2, xhat1, g1, b1, g2, b2, target):
    s, d = xhat1.shape
    dff = r.shape[1]
    tm, tk = min(s, 512), min(dff, 1024)
    ch = min(tm, EPILOGUE_ROWS)

    def epilogue(acc_ref, ex, outs):
        xh1_ref, g1_ref, b1_ref, g2_ref, b2_ref, t_ref = ex
        dz_ref, dzb_ref, st_ref = outs
        g1v, b1v, g2v, b2v = g1_ref[...], b1_ref[...], g2_ref[...], b2_ref[...]
        dg = db = loss = None
        for c in range(tm // ch):
            rows = slice(c * ch, (c + 1) * ch)
            x1 = xh1_ref[rows, :] * g1v + b1v
            xhat2, rstd2 = _layer_norm_fwd(ALPHA * x1 + acc_ref[rows, :])
            err = xhat2 * g2v + b2v - t_ref[rows, :]
            dy = err * (1.0 / d)
            dz = _layer_norm_bwd(dy, xhat2, rstd2, g2v)
            dz_ref[rows, :] = dz
            dzb_ref[rows, :] = dz.astype(BF16)
            parts = (jnp.sum(dy * xhat2, axis=0, keepdims=True), jnp.sum(dy, axis=0, keepdims=True),
                     jnp.sum(jnp.sum(err * err, axis=-1, keepdims=True), axis=0, keepdims=True))
            dg, db, loss = parts if c == 0 else (dg + parts[0], db + parts[1], loss + parts[2])
        st_ref[...] = _stats_rows([dg, db, jnp.broadcast_to((0.5 / d) * loss, (1, d))], d)

    row = pl.BlockSpec((tm, d), lambda i, j, k: (i, 0))
    vec = pl.BlockSpec((1, d), lambda i, j, k: (0, 0))
    return _mm(
        "ff_down_loss", (s // tm, 1, dff // tk), r, pl.BlockSpec((tm, tk), lambda i, j, k: (i, k)),
        w2, pl.BlockSpec((tk, d), lambda i, j, k: (k, 0)), NN,
        [xhat1, g1, b1, g2, b2, target], [row, vec, vec, vec, vec, row],
        [jax.ShapeDtypeStruct((s, d), F32), jax.ShapeDtypeStruct((s, d), BF16),
         jax.ShapeDtypeStruct((s // tm, 8, d), F32)],
        [row, row, pl.BlockSpec((None, 8, d), lambda i, j, k: (i, 0, 0))], epilogue, (tm, d),
        acc_as_ref=True)


def _ff_down_bwd(dz2b, w2, r):
    s, d = dz2b.shape
    dff = r.shape[1]
    tm, tn = min(s, 1024), min(dff, 1024)

    def epilogue(acc, ex, outs):
        outs[0][...] = (acc * (2.0 * jnp.sqrt(ex[0][...].astype(F32)))).astype(BF16)

    blk = pl.BlockSpec((tm, tn), lambda i, j, k: (i, j))
    return _mm(
        "ff_down_bwd", (s // tm, dff // tn, 1), dz2b, pl.BlockSpec((tm, d), lambda i, j, k: (i, 0)),
        w2, pl.BlockSpec((tn, d), lambda i, j, k: (j, 0)), NT, [r], [blk],
        [jax.ShapeDtypeStruct((s, dff), BF16)], [blk], epilogue, None)[0]


def _wgrad(name, a, g, n_sh, shard0=0, n_shards=None, fill=None):
    s, rows = a.shape
    cols = g.shape[1]
    tm, tn, tk = min(rows, 2048), min(cols, 1024), min(s, 1024)
    if tn >= n_sh:
        span = tn // n_sh
        out_spec = pl.BlockSpec((span, tm, n_sh), lambda i, j, k: (j + shard0 // span, i, 0))

        def epilogue(acc_ref, ex, outs):
            for sh in range(span):
                outs[0][sh] = acc_ref[:, sh * n_sh:(sh + 1) * n_sh].astype(BF16)
    else:
        per = n_sh // tn
        out_spec = pl.BlockSpec((None, tm, tn), lambda i, j, k: (shard0 + j // per, i, j % per))

        def epilogue(acc_ref, ex, outs):
            outs[0][...] = acc_ref[...].astype(BF16)

    return _mm(
        name, (rows // tm, cols // tn, s // tk), a, pl.BlockSpec((tk, tm), lambda i, j, k: (k, i)),
        g, pl.BlockSpec((tk, tn), lambda i, j, k: (k, j)), TN, [], [],
        [jax.ShapeDtypeStruct((n_shards or cols // n_sh, rows, n_sh), BF16)], [out_spec], epilogue,
        (tm, tn), acc_as_ref=True, fill=fill)[0]


def _ff_up_bwd(da, w1_st, dz2, xhat1, rstd1, g1, run_after):
    s, d = dz2.shape
    n_sh = w1_st.shape[2]
    tm, tk = min(s, 512), min(n_sh, 1024)
    per = n_sh // tk
    ch = min(tm, EPILOGUE_ROWS)

    def epilogue(acc_ref, ex, outs):
        dz2_ref, xh_ref, rs_ref, g_ref = ex
        dz_ref, dzb_ref, st_ref = outs
        gv = g_ref[...]
        dg = db = None
        for c in range(tm // ch):
            rows = slice(c * ch, (c + 1) * ch)
            dx1 = ALPHA * dz2_ref[rows, :] + acc_ref[rows, :]
            xhat = xh_ref[rows, :]
            dz = _layer_norm_bwd(dx1, xhat, rs_ref[rows, :], gv)
            dz_ref[rows, :] = dz
            dzb_ref[rows, :] = dz.astype(BF16)
            parts = (jnp.sum(dx1 * xhat, axis=0, keepdims=True), jnp.sum(dx1, axis=0, keepdims=True))
            dg, db = parts if c == 0 else (dg + parts[0], db + parts[1])
        st_ref[...] = _stats_rows([dg, db], d)

    row = pl.BlockSpec((tm, d), lambda i, j, k: (i, 0))
    return _mm(
        "ff_up_bwd", (s // tm, 1, N_CHIPS * per), da, pl.BlockSpec((tm, tk), lambda i, j, k: (i, k)),
        w1_st, pl.BlockSpec((None, d, tk), lambda i, j, k: (k // per, 0, k % per)), NT,
        [dz2, xhat1, rstd1, g1],
        [row, row, pl.BlockSpec((tm, 1), lambda i, j, k: (i, 0)), pl.BlockSpec((1, d), lambda i, j, k: (0, 0))],
        [jax.ShapeDtypeStruct((s, d), F32), jax.ShapeDtypeStruct((s, d), BF16),
         jax.ShapeDtypeStruct((s // tm, 8, d), F32)],
        [row, row, pl.BlockSpec((None, 8, d), lambda i, j, k: (i, 0, 0))], epilogue, (tm, d),
        acc_as_ref=True, run_after=run_after)


def _mix_bwd(dz1b, w_out, h, ya, yp, aw):
    s, d = dz1b.shape
    tm = min(s, 256)
    gblk = 4 * aw // d

    def epilogue(acc, ex, outs):
        sga_ref, sgp_ref, ya_ref, yp_ref = ex
        dya_ref, dyp_ref, dg_ref = outs
        sga, sgp = sga_ref[...].astype(F32), sgp_ref[...].astype(F32)
        dya_ref[...] = (acc * sga).astype(BF16)
        dyp_ref[...] = (acc * sgp).astype(BF16)
        dg_ref[:, :d] = (acc * ya_ref[...].astype(F32) * (sga * (1.0 - sga))).astype(BF16)
        dg_ref[:, d:] = (acc * yp_ref[...].astype(F32) * (sgp * (1.0 - sgp))).astype(BF16)

    row = pl.BlockSpec((tm, d), lambda i, j, k: (i, 0))
    return _mm(
        "mix_bwd", (s // tm, 1, 1), dz1b, row, w_out, pl.BlockSpec((d, d), lambda i, j, k: (0, 0)), NT,
        [h, h, ya, yp],
        [pl.BlockSpec((tm, d), lambda i, j, k: (i, gblk)),
         pl.BlockSpec((tm, d), lambda i, j, k: (i, gblk + 1)), row, row],
        [jax.ShapeDtypeStruct((s, d), BF16), jax.ShapeDtypeStruct((s, d), BF16),
         jax.ShapeDtypeStruct((s, 2 * d), BF16)],
        [row, row, pl.BlockSpec((tm, 2 * d), lambda i, j, k: (i, 0))], epilogue, None)


def _branch_in_bwd(name, tm, dyb, wb_st, epilogue, extras, extra_specs, out_shape, out_specs,
                   scratch=()):
    s = dyb.shape[0]
    aw, tk = wb_st.shape[1], wb_st.shape[2]
    return _mm(
        name, (s // tm, 1, N_CHIPS), dyb, pl.BlockSpec((tm, tk), lambda i, j, k: (i, k)),
        wb_st, pl.BlockSpec((None, aw, tk), lambda i, j, k: (k, 0, 0)), NT,
        extras, extra_specs, out_shape, out_specs, epilogue, (tm, aw), acc_as_ref=True,
        scratch=scratch)


def _attn_out_bwd(dya, wba_st, o_attn):
    s, aw = o_attn.shape
    tm = min(s, 512)

    def epilogue(acc_ref, ex, outs, sdo, sdl):
        do_ref, dl_ref, do4_ref, do16_ref, dl4_ref, dl16_ref = outs
        deltas = []
        for hh in range(aw // HEAD_DIM):
            sl = slice(hh * HEAD_DIM, (hh + 1) * HEAD_DIM)
            do = acc_ref[:, sl]
            deltas.append(jnp.sum(do * ex[0][:, sl].astype(F32), axis=-1, keepdims=True))
            sdo[hh] = do
            do_ref[:, sl] = do.astype(BF16)
        packed = _pack_heads(deltas)
        sdl[0] = packed
        dl_ref[...] = packed
        _to_views(sdo, (do4_ref, do16_ref), BF16)
        _to_views(sdl, (dl4_ref, dl16_ref), F32)

    def specs(width):
        return ([pl.BlockSpec((tm, width), lambda i, j, k: (i, 0))]
                + [pl.BlockSpec((tm // dil, dil * width), lambda i, j, k: (i, 0)) for dil in DILATIONS[1:]])

    def shapes(width, dtype):
        return ([jax.ShapeDtypeStruct((s, width), dtype)]
                + [jax.ShapeDtypeStruct(_view_shape(s, width, dil), dtype) for dil in DILATIONS[1:]])

    do_specs, dl_specs = specs(aw), specs(HEAD_DIM)
    do_shapes, dl_shapes = shapes(aw, BF16), shapes(HEAD_DIM, F32)
    return _branch_in_bwd(
        "attn_out_bwd", tm, dya, wba_st, epilogue, [o_attn], [do_specs[0]],
        [do_shapes[0], dl_shapes[0], *do_shapes[1:], *dl_shapes[1:]],
        [do_specs[0], dl_specs[0], *do_specs[1:], *dl_specs[1:]],
        scratch=[_heads_scratch(tm, aw), _heads_scratch(tm, HEAD_DIM)])


def _pool_out_bwd(dyp, wbp_st, pw, scale):
    s, pw_ = pw.shape
    tm = min(s, 1024)

    def epilogue(acc_ref, ex, outs):
        pw_ref, sc_ref = ex
        dpw_ref, st_ref = outs
        acc = acc_ref[...]
        dpw_ref[...] = (acc * sc_ref[...]).astype(BF16)
        st_ref[...] = _stats_rows([jnp.sum(acc * pw_ref[...].astype(F32), axis=0, keepdims=True)], pw_)

    row = pl.BlockSpec((tm, pw_), lambda i, j, k: (i, 0))
    return _branch_in_bwd(
        "pool_out_bwd", tm, dyp, wbp_st, epilogue, [pw, scale],
        [row, pl.BlockSpec((1, pw_), lambda i, j, k: (0, 0))],
        [jax.ShapeDtypeStruct((s, pw_), BF16), jax.ShapeDtypeStruct((s // tm, 8, pw_), F32)],
        [row, pl.BlockSpec((None, 8, pw_), lambda i, j, k: (i, 0, 0))])


def _pool_bwd(dpw, p, wp):
    s, pw_ = p.shape
    ng = len(POOL_WINDOWS)
    pgw = pw_ // ng
    tm = min(s, 512)
    hb = tm // POOL_HALO
    nblk = s // tm

    def body(dpw_ref, nxt_ref, p_ref, wp_ref, dwp_ref, du_ref):
        i = pl.program_id(0)
        nxt = (nxt_ref[...].astype(F32) * jnp.where(i < nblk - 1, 1.0, 0.0)).astype(BF16)
        dpw_all = jnp.concatenate([dpw_ref[...], nxt], axis=0)

        @pl.when(i == 0)
        def _():
            dwp_ref[...] = jnp.zeros_like(dwp_ref)

        dps = []
        for g in range(ng):
            sl = slice(g * pgw, (g + 1) * pgw)
            dwp_ref[g] += _dot(p_ref[:, sl], dpw_ref[:, sl], TN)
            dps.append(_dot(dpw_all[:, sl], wp_ref[g], NT))
        dp = jnp.concatenate(dps, axis=1)
        dpn = dp / _pool_counts(tm, tm + POOL_HALO, pgw, i * tm)
        du_ref[...] = (_window_sums(dpn, -1, pgw)[:tm] - dp[:tm]).astype(BF16)

    row = pl.BlockSpec((tm, pw_), lambda i: (i, 0))
    full = pl.BlockSpec((ng, pgw, pgw), lambda i: (0, 0, 0))
    return pl.pallas_call(
        body, name="pool_bwd", grid=(nblk,),
        in_specs=[row, pl.BlockSpec((POOL_HALO, pw_), lambda i: (jnp.minimum((i + 1) * hb, s // POOL_HALO - 1), 0)),
                  row, full],
        out_specs=[full, row],
        out_shape=[jax.ShapeDtypeStruct((ng, pgw, pgw), F32), jax.ShapeDtypeStruct((s, pw_), BF16)],
        compiler_params=_params(("arbitrary",)),
    )(dpw, dpw, p, wp)


def _attn_bwd(name, q, k, v, do, lse, delta, offs, cw, width):
    m = do.shape[0]
    nh = cw // HEAD_DIM
    nblk = m // SUB_BLOCK
    grid = (width // cw, nblk + 1)
    scale = HEAD_DIM ** -0.5

    def body(q_ref, k_ref, v_ref, do_ref, lse_ref, dl_ref, dq_ref, dk_ref, dv_ref,
             kprev, vprev, dk_carry, dv_carry):
        n = pl.program_id(1)

        @pl.when(n == 0)
        def _():
            for ref in (kprev, vprev, dk_carry, dv_carry):
                ref[...] = jnp.zeros_like(ref)

        qi = lax.broadcasted_iota(jnp.int32, (SUB_BLOCK, 2 * SUB_BLOCK), 0)
        kj = lax.broadcasted_iota(jnp.int32, (SUB_BLOCK, 2 * SUB_BLOCK), 1)
        first_key = jnp.where(n == 0, SUB_BLOCK, jnp.where(n == nblk, 4 * SUB_BLOCK, 0))
        valid = (kj >= qi) & (kj <= qi + SUB_BLOCK) & (kj >= first_key)
        heads = [slice(hh * HEAD_DIM, (hh + 1) * HEAD_DIM) for hh in range(nh)]
        kcats = [jnp.concatenate([kprev[:, sl], k_ref[:, sl]], axis=0) for sl in heads]
        scs = [_dot(q_ref[:, sl], kcats[hh], NT) for hh, sl in enumerate(heads)]
        dps = [_dot(do_ref[:, sl], jnp.concatenate([vprev[:, sl], v_ref[:, sl]], axis=0), NT)
               for sl in heads]
        dqs = []
        lse_all, dl_all = lse_ref[...], dl_ref[...]
        for hh, sl in enumerate(heads):
            lse_h, dl_h = _head_col(lse_all, hh, nh), _head_col(dl_all, hh, nh)
            p = jnp.where(valid, jnp.exp(jnp.where(valid, scs[hh] * scale, NEG) - lse_h), 0.0)
            ds = (p * (dps[hh] - dl_h)).astype(BF16)
            dqs.append((_dot(ds, kcats[hh], NN) * scale).astype(BF16))
            dk2 = _dot(ds, q_ref[:, sl], TN) * scale
            dv2 = _dot(p.astype(BF16), do_ref[:, sl], TN)
            dk_ref[:, sl] = (dk_carry[:, sl] + dk2[:SUB_BLOCK]).astype(BF16)
            dv_ref[:, sl] = (dv_carry[:, sl] + dv2[:SUB_BLOCK]).astype(BF16)
            dk_carry[:, sl] = dk2[SUB_BLOCK:]
            dv_carry[:, sl] = dv2[SUB_BLOCK:]

        @pl.when(n < nblk)
        def _():
            for hh, sl in enumerate(heads):
                dq_ref[:, sl] = dqs[hh]

        kprev[...] = k_ref[...]
        vprev[...] = v_ref[...]

    def cur(off):
        return pl.BlockSpec((SUB_BLOCK, cw), lambda j, n: (jnp.minimum(n, nblk - 1), j + off))

    lagged = pl.BlockSpec((SUB_BLOCK, cw), lambda j, n: (jnp.maximum(n - 1, 0), j))
    stat = pl.BlockSpec((SUB_BLOCK, HEAD_DIM), lambda j, n: (jnp.minimum(n, nblk - 1), j))
    out = jax.ShapeDtypeStruct((m, width), BF16)
    return pl.pallas_call(
        body, name=name, grid=grid,
        in_specs=[cur(offs[0]), cur(offs[1]), cur(offs[2]), cur(0), stat, stat],
        out_specs=[cur(0), lagged, lagged], out_shape=[out, out, out],
        scratch_shapes=[pltpu.VMEM((SUB_BLOCK, cw), BF16)] * 2 + [pltpu.VMEM((SUB_BLOCK, cw), F32)] * 2,
        compiler_params=_params(("parallel", "arbitrary")),
    )(q, k, v, do, lse, delta)


def _qkvu_grad(d1, d4, d16, du, cos_t, sin_t):
    s, aw = du.shape
    tm = min(s, 512)

    def body(*refs):
        nat, v4, v16 = refs[0:3], refs[3:6], refs[6:9]
        cos_ref, sin_ref, du_ref, out_ref, s4, s16 = refs[9:]
        cos, sin = cos_ref[...], sin_ref[...]
        for part in range(3):
            _from_views((v4[part], v16[part]), (s4, s16))
            for hh in range(aw // HEAD_DIM):
                sl = slice(hh * HEAD_DIM, (hh + 1) * HEAD_DIM)
                t = nat[part][:, sl].astype(F32) + s4[hh] + s16[hh]
                if part < 2:
                    t = t * cos - pltpu.roll(t, HEAD_DIM // 2, 1) * sin
                out_ref[:, part * aw + hh * HEAD_DIM:part * aw + (hh + 1) * HEAD_DIM] = t.astype(BF16)
        out_ref[:, 3 * aw:] = du_ref[...]

    row = pl.BlockSpec((tm, aw), lambda i: (i, 0))
    tab = pl.BlockSpec((tm, HEAD_DIM), lambda i: (i, 0))
    return pl.pallas_call(
        body, name="qkvu_grad", grid=(s // tm,),
        in_specs=[row] * 3 + [_view_spec(tm, aw, 4)] * 3 + [_view_spec(tm, aw, 16)] * 3 + [tab, tab, row],
        out_specs=pl.BlockSpec((tm, 4 * aw), lambda i: (i, 0)),
        out_shape=jax.ShapeDtypeStruct((s, 4 * aw), BF16),
        scratch_shapes=[_heads_scratch(tm, aw)] * 2,
        compiler_params=_params(("parallel",)),
    )(*d1, *d4, *d16, cos_t, sin_t, du)


def _in_proj_bwd_x(name, dh, w_in_st, shard0, base, scale_base, run_after=()):
    s, kdim = dh.shape
    d, n_sh = w_in_st.shape[1], w_in_st.shape[2]
    tm, tk = min(s, 512), min(n_sh, 2048)
    per = n_sh // tk

    ch = min(tm, 2 * EPILOGUE_ROWS)

    def epilogue(acc_ref, ex, outs):
        for c in range(tm // ch):
            rows = slice(c * ch, (c + 1) * ch)
            outs[0][rows, :] = scale_base * ex[0][rows, :] + acc_ref[rows, :]

    row = pl.BlockSpec((tm, d), lambda i, j, k: (i, 0))
    return _mm(
        name, (s // tm, 1, kdim // tk), dh, pl.BlockSpec((tm, tk), lambda i, j, k: (i, k)),
        w_in_st, pl.BlockSpec((None, d, tk), lambda i, j, k: (shard0 + k // per, 0, k % per)), NT,
        [base], [row], [jax.ShapeDtypeStruct((s, d), F32)], [row], epilogue, (tm, d),
        acc_as_ref=True, run_after=run_after)[0]


def _chip_peers():
    x, y, c = lax.axis_index("x"), lax.axis_index("y"), lax.axis_index("c")
    return x, y, c, [(1 - x, y), (x, 1 - y), (1 - x, 1 - y)]


GATHER, GATHER_HALF, SCATTER, SIBLING = "gather", "gather_half", "scatter", "sibling"


def _exchange_peers(mode):
    x, y, c, chips = _chip_peers()
    if mode == SIBLING:
        return x, y, c, [(x, y, 1 - c)]
    return x, y, c, [(px, py, c) for px, py in chips]


def _core_half(ref_or_shape, c):
    rows = (ref_or_shape.shape[0]) // 2
    return pl.ds(c * rows, rows)


def _exchange_descriptor(mode, src, land, send, recv, p, peer, me, arriving):
    pid = 2 * peer[0] + peer[1]
    if mode == GATHER:
        src_ref, dst_ref = src, land.at[pid if arriving else me]
    elif mode == GATHER_HALF:
        rows = _core_half(src, peer[2])
        src_ref, dst_ref = src.at[rows], land.at[pid if arriving else me, rows]
    elif mode == SCATTER:
        src_ref, dst_ref = src.at[pid], land.at[p]
    else:
        src_ref, dst_ref = src, land
    return pltpu.make_async_remote_copy(
        src_ref=src_ref, dst_ref=dst_ref, send_sem=send.at[p], recv_sem=recv.at[p],
        device_id=peer, device_id_type=MESH)


def _exchange_start(name, mode, srcs, land_shapes):
    n = len(srcs)
    lands = [_hbm(lax.empty(shape, src.dtype)) for shape, src in zip(land_shapes, srcs)]

    def body(*refs):
        src_refs, land_refs = refs[:n], refs[n:2 * n]
        sends, recvs = refs[2 * n:3 * n], refs[3 * n:4 * n]
        token = refs[6 * n]
        x, y, c, peers = _exchange_peers(mode)
        me = 2 * x + y
        for w in range(n):
            for p, peer in enumerate(peers):
                _exchange_descriptor(mode, src_refs[w], land_refs[w], sends[w], recvs[w], p, peer,
                                     me, arriving=False).start()
        token[...] = jnp.zeros_like(token)

    sem = pltpu.SemaphoreType.DMA((3,))
    outs = pl.pallas_call(
        body, name=name, in_specs=[HBM_SPEC] * (2 * n),
        out_specs=[SEM_SPEC] * (2 * n) + [HBM_SPEC] * (2 * n) + [pl.BlockSpec(memory_space=pltpu.VMEM)],
        out_shape=[sem] * (2 * n) + [pltpu.HBM(a.shape, a.dtype) for a in (*srcs, *lands)]
        + [jax.ShapeDtypeStruct((8, 128), F32)],
        input_output_aliases={i: 2 * n + i for i in range(2 * n)},
        compiler_params=pltpu.CompilerParams(has_side_effects=DATAFLOW),
    )(*[_hbm(a) for a in srcs], *lands)
    return {"send": outs[:n], "recv": outs[n:2 * n], "src": outs[2 * n:3 * n],
            "land": outs[3 * n:4 * n], "token": outs[4 * n]}


def _exchange_wait(name, mode, started, which, after):
    m = len(which)

    def body(*refs):
        src_refs, land_refs = refs[:m], refs[m:2 * m]
        sends, recvs = refs[2 * m:3 * m], refs[3 * m:4 * m]
        x, y, c, peers = _exchange_peers(mode)
        me = 2 * x + y
        for w in range(m):
            for p, peer in enumerate(peers):
                _exchange_descriptor(mode, src_refs[w], land_refs[w], sends[w], recvs[w], p, peer,
                                     me, arriving=False).wait_send()
                _exchange_descriptor(mode, src_refs[w], land_refs[w], sends[w], recvs[w], p, peer,
                                     me, arriving=True).wait_recv()

    pick = lambda key: [started[key][w] for w in which]
    bufs = pick("src") + pick("land")
    after = list(after) if isinstance(after, (list, tuple)) else [after]
    outs = pl.pallas_call(
        body, name=name,
        in_specs=[HBM_SPEC] * (2 * m) + [SEM_SPEC] * (2 * m) + [ANY_SPEC] * len(after),
        out_specs=[HBM_SPEC] * (2 * m), out_shape=[pltpu.HBM(a.shape, a.dtype) for a in bufs],
        input_output_aliases={i: i for i in range(2 * m)},
        compiler_params=pltpu.CompilerParams(has_side_effects=DATAFLOW),
    )(*bufs, *pick("send"), *pick("recv"), *after)
    return outs[:m], outs[m:]


def _to_bf16(name, a, run_after):
    r, c = a.shape
    tm = min(r, 512)

    def body(a_ref, after_ref, out_ref):
        out_ref[...] = a_ref[...].astype(BF16)

    blk = pl.BlockSpec((tm, c), lambda i: (i, 0))
    return pl.pallas_call(
        body, name=name, grid=(r // tm,), in_specs=[blk, ANY_SPEC], out_specs=blk,
        out_shape=jax.ShapeDtypeStruct((r, c), BF16), compiler_params=_params(("parallel",)),
    )(a, run_after)


def _swap_halves(name, lands):
    n = len(lands)

    def body(*refs):
        bufs = refs[n:2 * n]
        send, recv = refs[2 * n:]
        x, y, c, chips = _chip_peers()
        started = []
        for w in range(n):
            half = bufs[w].shape[1] // 2
            for p, (px, py) in enumerate(chips):
                mine = bufs[w].at[2 * px + py, pl.ds(c * half, half)]
                cp = pltpu.make_async_remote_copy(
                    src_ref=mine, dst_ref=mine, send_sem=send.at[w, p], recv_sem=recv.at[w, p],
                    device_id=(x, y, 1 - c), device_id_type=MESH)
                cp.start()
                started.append(cp)
        for w in range(n):
            half = bufs[w].shape[1] // 2
            for p, (px, py) in enumerate(chips):
                theirs = bufs[w].at[2 * px + py, pl.ds((1 - c) * half, half)]
                pltpu.make_async_remote_copy(
                    src_ref=theirs, dst_ref=theirs, send_sem=send.at[w, p], recv_sem=recv.at[w, p],
                    device_id=(x, y, 1 - c), device_id_type=MESH).wait_recv()
        for cp in started:
            cp.wait_send()

    return pl.pallas_call(
        body, name=name, in_specs=[ANY_SPEC] * n, out_specs=[ANY_SPEC] * n,
        out_shape=[jax.ShapeDtypeStruct(a.shape, a.dtype) for a in lands],
        input_output_aliases={i: i for i in range(n)},
        scratch_shapes=[pltpu.SemaphoreType.DMA((n, 3)), pltpu.SemaphoreType.DMA((n, 3))],
    )(*lands)


def _place_own(name, shard, land, me):
    r, c = shard.shape
    tm = min(r, 512)

    def body(me_ref, shard_ref, land_ref, out_ref):
        out_ref[...] = shard_ref[...]

    return pl.pallas_call(
        body, name=name,
        grid_spec=pltpu.PrefetchScalarGridSpec(
            num_scalar_prefetch=1, grid=(r // tm,),
            in_specs=[pl.BlockSpec((tm, c), lambda i, me_ref: (i, 0)), ANY_SPEC],
            out_specs=pl.BlockSpec((None, tm, c), lambda i, me_ref: (me_ref[0], i, 0))),
        out_shape=jax.ShapeDtypeStruct(land.shape, land.dtype), input_output_aliases={2: 0},
        compiler_params=_params(("arbitrary",)),
    )(me, shard, land)


def _sum_slabs(name, grads, land, me):
    _, r, c = grads.shape
    tm = min(r, 256)

    def body(me_ref, own_ref, land_ref, out_ref):
        acc = own_ref[...].astype(F32)
        for p in range(3):
            acc = acc + land_ref[p].astype(F32)
        out_ref[...] = acc

    return pl.pallas_call(
        body, name=name,
        grid_spec=pltpu.PrefetchScalarGridSpec(
            num_scalar_prefetch=1, grid=(r // tm,),
            in_specs=[pl.BlockSpec((None, tm, c), lambda i, me_ref: (me_ref[0], i, 0)),
                      pl.BlockSpec((3, tm, c), lambda i, me_ref: (0, i, 0))],
            out_specs=pl.BlockSpec((tm, c), lambda i, me_ref: (i, 0))),
        out_shape=jax.ShapeDtypeStruct((r, c), F32), compiler_params=_params(("parallel",)),
    )(me, grads, land)


def _allreduce_stats(stats, run_after):
    n = len(stats)

    def body(*refs):
        ins, outs = refs[:n], refs[n + 1:2 * n + 1]
        mine, gath = refs[2 * n + 1:3 * n + 1], refs[3 * n + 1:4 * n + 1]
        send, recv = refs[4 * n + 1:]
        x, y, c = lax.axis_index("x"), lax.axis_index("y"), lax.axis_index("c")
        me = 4 * x + 2 * y + c
        flips = [(bx, by, bc) for bx in (0, 1) for by in (0, 1) for bc in (0, 1)][1:]

        def peer(f):
            return (x + f[0] * (1 - 2 * x), y + f[1] * (1 - 2 * y), c + f[2] * (1 - 2 * c))

        copies = []
        for t in range(n):
            tot = ins[t][0]
            for b in range(1, ins[t].shape[0]):
                tot = tot + ins[t][b]
            mine[t][...] = tot
            gath[t][me] = tot
            for k, f in enumerate(flips):
                cp = pltpu.make_async_remote_copy(
                    src_ref=mine[t], dst_ref=gath[t].at[me], send_sem=send.at[t, k],
                    recv_sem=recv.at[t, k], device_id=peer(f), device_id_type=MESH)
                cp.start()
                copies.append(cp)
        for t in range(n):
            for k, f in enumerate(flips):
                px, py, pc = peer(f)
                pltpu.make_async_remote_copy(
                    src_ref=mine[t], dst_ref=gath[t].at[4 * px + 2 * py + pc], send_sem=send.at[t, k],
                    recv_sem=recv.at[t, k], device_id=(px, py, pc), device_id_type=MESH).wait_recv()
        for cp in copies:
            cp.wait_send()
        for t in range(n):
            tot = gath[t][0]
            for dev in range(1, 8):
                tot = tot + gath[t][dev]
            outs[t][...] = tot

    vm = pl.BlockSpec(memory_space=pltpu.VMEM)
    return pl.pallas_call(
        body, name="allreduce_stats", in_specs=[vm] * n + [ANY_SPEC], out_specs=[vm] * n,
        out_shape=[jax.ShapeDtypeStruct(s.shape[1:], F32) for s in stats],
        scratch_shapes=[pltpu.VMEM(s.shape[1:], F32) for s in stats]
        + [pltpu.VMEM((8, *s.shape[1:]), F32) for s in stats]
        + [pltpu.SemaphoreType.DMA((n, 7)), pltpu.SemaphoreType.DMA((n, 7))],
    )(*stats, run_after)


def _adamw(name, w, m, v, g_parts):
    r, c = w.shape
    tm = min(r, 128)
    n_g = len(g_parts)

    def body(*refs):
        w_ref, m_ref, v_ref = refs[:3]
        g_refs = refs[3:3 + n_g]
        g_out, d_out, m_out, v_out = refs[3 + n_g:]
        g = g_refs[0][...]
        for gr in g_refs[1:]:
            g = g + gr[...]
        m_new = ADAM_B1 * m_ref[...] + (1.0 - ADAM_B1) * g
        v_new = ADAM_B2 * v_ref[...] + (1.0 - ADAM_B2) * (g * g)
        m_hat = m_new / (1.0 - ADAM_B1 ** ADAM_STEP)
        v_hat = v_new / (1.0 - ADAM_B2 ** ADAM_STEP)
        g_out[...] = g
        d_out[...] = -ADAM_LR * (m_hat / (jnp.sqrt(v_hat) + ADAM_EPS) + ADAM_WD * w_ref[...])
        m_out[...] = m_new
        v_out[...] = v_new

    blk = pl.BlockSpec((tm, c), lambda i: (i, 0))
    out = jax.ShapeDtypeStruct((r, c), F32)
    return pl.pallas_call(
        body, name=name, grid=(r // tm,), in_specs=[blk] * (3 + n_g), out_specs=[blk] * 4,
        out_shape=[out] * 4, compiler_params=_params(("parallel",)),
    )(w, m, v, *g_parts)


def _rope_tables(positions):
    half = HEAD_DIM // 2
    inv_freq = ROPE_THETA ** (-jnp.arange(half, dtype=F32) / half)
    ang = positions.astype(F32)[0, :, None] * inv_freq
    cos, sin = jnp.cos(ang), jnp.sin(ang)
    return jnp.concatenate([cos, cos], axis=-1), jnp.concatenate([-sin, sin], axis=-1)


def kernel(x, positions, w_in, w_pool, pool_scale, w_branch_attn, w_branch_pool, w_out, ln_mix_g, ln_mix_b, w_ff1, w_ff2, ln_ff_g, ln_ff_b, loss_target, m_w_in, m_w_pool, m_pool_scale, m_w_branch_attn, m_w_branch_pool, m_w_out, m_ln_mix_g, m_ln_mix_b, m_w_ff1, m_w_ff2, m_ln_ff_g, m_ln_ff_b, v_w_in, v_w_pool, v_pool_scale, v_w_branch_attn, v_w_branch_pool, v_w_out, v_ln_mix_g, v_ln_mix_b, v_w_ff1, v_w_ff2, v_ln_ff_g, v_ln_ff_b):
    s, d = x.shape[1], x.shape[2]
    aw = d // 2
    ng = len(POOL_WINDOWS)
    pgw = aw // ng
    x2d, target = x[0], loss_target[0]
    xb = x2d.astype(BF16)
    cos_t, sin_t = _rope_tables(positions)

    big = {"w_in": w_in[0], "w_pool": w_pool[0].reshape(-1, pgw), "w_branch_attn": w_branch_attn[0],
           "w_branch_pool": w_branch_pool[0], "w_out": w_out[0], "w_ff1": w_ff1[0], "w_ff2": w_ff2[0]}
    names = list(big)
    me_chip = (2 * lax.axis_index("x") + lax.axis_index("y")).astype(jnp.int32).reshape(1)
    land_shapes = [(N_CHIPS, *big[k].shape) for k in names]
    gathering_in = _exchange_start("gather_start_in", GATHER_HALF, [big["w_in"].astype(BF16)],
                                   land_shapes[:1])
    shards = [_to_bf16(f"to_bf16_{k}", big[k], gathering_in["token"]) for k in names[1:]]
    gathering = _exchange_start("gather_start", GATHER, shards, land_shapes[1:])

    def gathered(name, which, after):
        srcs, lands = _exchange_wait(f"gather_wait_{name}", GATHER, gathering, which, after)
        return [_place_own(f"place_own_{names[w + 1]}", srcs[i], lands[i], me_chip)
                for i, w in enumerate(which)]

    rows_sh = pgw // N_CHIPS
    dff = N_CHIPS * big["w_ff2"].shape[0]

    srcs, lands = _exchange_wait("gather_wait_in", GATHER_HALF, gathering_in, [0],
                                 [gathering["token"], xb, cos_t, sin_t])
    w_in_st = _place_own("place_own_w_in", srcs[0], _swap_halves("swap_halves_in", lands)[0], me_chip)
    h, hv4, hv16 = _in_proj(xb, w_in_st, cos_t, sin_t, aw)
    (wp_st,) = gathered("pool", [0], h)
    wp = wp_st.reshape(N_CHIPS, ng, rows_sh, pgw).transpose(1, 0, 2, 3).reshape(ng, pgw, pgw)
    qkv = {1: (h, h, h), 4: (hv4, hv4, hv4), 16: (hv16, hv16, hv16)}
    offs = {dil: (0, dil, 2 * dil) for dil in DILATIONS}
    o_parts, lse_parts = [], []
    for dil in DILATIONS:
        o_p, lse_p = _attn_fwd(f"attn_fwd_d{dil}", *qkv[dil], offs[dil], aw, dil * aw)
        o_parts.append(o_p)
        lse_parts.append(lse_p)
    o_attn, lse, lse4, lse16 = _attn_combine(o_parts[0], lse_parts[0], o_parts[1:], lse_parts[1:])
    p, pw, y = _pool_fwd(h, wp, pool_scale, aw)
    wba_st, wbp_st, w_out_st = gathered("mix", [1, 2, 3], y)
    w_out_full = w_out_st.reshape(d, d)
    ya, yp, merged = _branch_merge(o_attn, y, wba_st, wbp_st, h, aw, d)
    xhat1, rstd1, x1b = _mix_norm(merged, w_out_full, x2d, ln_mix_g, ln_mix_b)
    w1_st, w2_st = gathered("ff", [4, 5], x1b)
    w2_full = w2_st.reshape(dff, d)
    r = _ff_up(x1b, w1_st)
    dz2, dz2b, st2 = _ff_down_loss(r, w2_full, xhat1, ln_mix_g, ln_mix_b, ln_ff_g, ln_ff_b, target)

    def scatter_start(name, grads):
        return _exchange_start(f"scatter_start_{name}", SCATTER, grads, [(3, *g.shape[1:]) for g in grads])

    da = _ff_down_bwd(dz2b, w2_full, r)
    g_w2 = _wgrad("wgrad_ff2", r, dz2b, d).reshape(N_CHIPS, dff // N_CHIPS, d)
    g_w1 = _wgrad("wgrad_ff1", x1b, da, dff // N_CHIPS)
    sent_ff = scatter_start("ff", [g_w1, g_w2])
    dz1, dz1b, st1 = _ff_up_bwd(da, w1_st, dz2, xhat1, rstd1, ln_mix_g, [sent_ff["token"]])
    dya, dyp, dgate = _mix_bwd(dz1b, w_out_full, h, ya, yp, aw)
    g_wout = _wgrad("wgrad_out", merged, dz1b, d).reshape(N_CHIPS, d // N_CHIPS, d)
    g_wba = _wgrad("wgrad_branch_attn", o_attn, dya, d // N_CHIPS)
    g_wbp = _wgrad("wgrad_branch_pool", y, dyp, d // N_CHIPS)
    do, delta, do4, do16, delta4, delta16 = _attn_out_bwd(dya, wba_st, o_attn)
    dpw, stp = _pool_out_bwd(dyp, wbp_st, pw, pool_scale)
    dwp, du = _pool_bwd(dpw, p, wp)
    g_wp = dwp.reshape(ng, N_CHIPS, rows_sh, pgw).transpose(1, 0, 2, 3).reshape(
        N_CHIPS, ng * rows_sh, pgw).astype(BF16)
    sent_mix = scatter_start("mix", [g_wp, g_wba, g_wbp, g_wout])

    bwd_in = {1: (do, lse, delta), 4: (do4, lse4, delta4), 16: (do16, lse16, delta16)}
    dqkv = {}
    for dil in DILATIONS:
        args = (*qkv[dil], *bwd_in[dil], offs[dil], aw, dil * aw)
        dqkv[dil] = _attn_bwd(f"attn_bwd_d{dil}", *args)
    dqkvu = _qkvu_grad(dqkv[1], dqkv[4], dqkv[16], du, cos_t, sin_t)
    g_win = _wgrad("wgrad_in_gates", xb, dgate, d, shard0=2, n_shards=N_CHIPS)
    g_win = _wgrad("wgrad_in_qkvu", xb, dqkvu, d, n_shards=N_CHIPS, fill=g_win)
    sent_in = scatter_start("in", [g_win])
    dx_a = _in_proj_bwd_x("in_proj_bwd_qkvu", dqkvu, w_in_st, 0, dz1, ALPHA,
                          [sent_mix["token"], sent_in["token"]])
    grad_x = _in_proj_bwd_x("in_proj_bwd_gates", dgate, w_in_st, 2, dx_a, 1.0)

    moments = {"w_in": (m_w_in, v_w_in), "w_pool": (m_w_pool, v_w_pool),
               "w_branch_attn": (m_w_branch_attn, v_w_branch_attn),
               "w_branch_pool": (m_w_branch_pool, v_w_branch_pool), "w_out": (m_w_out, v_w_out),
               "w_ff1": (m_w_ff1, v_w_ff1), "w_ff2": (m_w_ff2, v_w_ff2)}
    originals = {"w_in": w_in, "w_pool": w_pool, "w_branch_attn": w_branch_attn,
                 "w_branch_pool": w_branch_pool, "w_out": w_out, "w_ff1": w_ff1, "w_ff2": w_ff2}
    res = {}

    def summed(name, sent, keys, after):
        srcs, lands = _exchange_wait(f"scatter_wait_{name}", SCATTER, sent, list(range(len(keys))), after)
        parts = [_sum_slabs(f"sum_slabs_{k}", srcs[i], lands[i], me_chip) for i, k in enumerate(keys)]
        return _exchange_start(f"cores_start_{name}", SIBLING, parts, [a.shape for a in parts])

    def updated(name, swapping, keys, after):
        mine, other = _exchange_wait(f"cores_wait_{name}", SIBLING, swapping, list(range(len(keys))), after)
        for i, k in enumerate(keys):
            mk, vk = (a.reshape(big[k].shape) for a in moments[k])
            outs = _adamw(f"adamw_{k}", big[k], mk, vk, [mine[i], other[i]])
            res[k] = [o.reshape(originals[k].shape) for o in outs]

    groups = {"ff": ["w_ff1", "w_ff2"], "mix": ["w_pool", "w_branch_attn", "w_branch_pool", "w_out"],
              "in": ["w_in"]}
    swap_ff = summed("ff", sent_ff, groups["ff"], grad_x)
    swap_mix = summed("mix", sent_mix, groups["mix"], swap_ff["token"])
    swap_in = summed("in", sent_in, groups["in"], swap_mix["token"])
    updated("ff", swap_ff, groups["ff"], swap_in["token"])
    updated("mix", swap_mix, groups["mix"], res["w_ff2"][0])
    updated("in", swap_in, groups["in"], res["w_out"][0])
    tot2, tot1, totp = _allreduce_stats([st2, st1, stp], res["w_in"][0])

    def pad_d(a):
        return jnp.pad(a, ((0, 0), (0, d - a.shape[1])))

    small = ["ln_mix_g", "ln_mix_b", "ln_ff_g", "ln_ff_b", "pool_scale"]
    small_w = {"ln_mix_g": ln_mix_g, "ln_mix_b": ln_mix_b, "ln_ff_g": ln_ff_g, "ln_ff_b": ln_ff_b,
               "pool_scale": pool_scale}
    small_m = {"ln_mix_g": m_ln_mix_g, "ln_mix_b": m_ln_mix_b, "ln_ff_g": m_ln_ff_g,
               "ln_ff_b": m_ln_ff_b, "pool_scale": m_pool_scale}
    small_v = {"ln_mix_g": v_ln_mix_g, "ln_mix_b": v_ln_mix_b, "ln_ff_g": v_ln_ff_g,
               "ln_ff_b": v_ln_ff_b, "pool_scale": v_pool_scale}
    small_g = [tot1[0:1], tot1[1:2], tot2[0:1], tot2[1:2], pad_d(totp[0:1])]

    def pack(rows):
        return jnp.concatenate([pad_d(a) for a in rows] + [jnp.zeros((8 - len(rows), d), F32)], axis=0)

    outs = _adamw("adamw_small", pack([small_w[k] for k in small]), pack([small_m[k] for k in small]),
                  pack([small_v[k] for k in small]), [pack(small_g)])
    for i, k in enumerate(small):
        res[k] = [o[i:i + 1, :small_w[k].shape[1]] for o in outs]
    loss = tot2[2, 0]

    order = ["w_in", "w_pool", "pool_scale", "w_branch_attn", "w_branch_pool", "w_out", "ln_mix_g",
             "ln_mix_b", "w_ff1", "w_ff2", "ln_ff_g", "ln_ff_b"]
    result = [loss, grad_x[None]]
    for idx in range(4):
        result += [res[k][idx] for k in order]
    return tuple(result)
```

```python
import jax
import jax.numpy as jnp
from jax import lax
from jax.experimental import pallas as pl
from jax.experimental.pallas import tpu as pltpu

F32 = jnp.float32
BF16 = jnp.bfloat16
MESH = pl.DeviceIdType.MESH

HEAD_DIM = 128
SUB_BLOCK = 128
DILATIONS = (1, 4, 16)
POOL_WINDOWS = (2, 4, 8, 16)
POOL_HALO = 16
ROPE_THETA = 10000.0
LN_EPS = 1e-5
ALPHA = 2.0 ** 0.25
ADAM_LR, ADAM_B1, ADAM_B2, ADAM_EPS, ADAM_WD, ADAM_STEP = 0.001, 0.9, 0.999, 1e-08, 0.01, 10
NEG = -1e30
N_CHIPS = 4
VMEM_LIMIT = 56 * 1024 * 1024
EPILOGUE_ROWS = 128


def _params(sem=None, vmem=VMEM_LIMIT):
    kw = {"vmem_limit_bytes": vmem}
    if sem is not None:
        kw["dimension_semantics"] = sem
    return pltpu.CompilerParams(**kw)


def _dot(a, b, contract):
    return lax.dot_general(a, b, (contract, ((), ())), preferred_element_type=F32)


ANY_SPEC = pl.BlockSpec(memory_space=pl.ANY)
HBM_SPEC = pl.BlockSpec(memory_space=pltpu.HBM)
SEM_SPEC = pl.BlockSpec(memory_space=pltpu.SEMAPHORE)
DATAFLOW = pltpu.SideEffectType.DATAFLOW_SIDE_EFFECTING


def _hbm(a):
    return pltpu.with_memory_space_constraint(a, pltpu.HBM)


NN = ((1,), (0,))
NT = ((1,), (1,))
TN = ((0,), (0,))


def _mm(name, grid, a, a_spec, b, b_spec, contract, extras, extra_specs, out_shape, out_specs,
        epilogue, acc_shape, acc_as_ref=False, run_after=(), scratch=(),
        semantics=("parallel", "parallel", "arbitrary"), fill=None):
    nk = grid[2]
    n_ex = len(extras)
    n_in = 2 + n_ex + len(run_after) + (fill is not None)
    n_out = len(out_shape)
    n_scr = len(scratch)

    def body(*refs):
        a_ref, b_ref = refs[0], refs[1]
        ex = refs[2:2 + n_ex]
        outs = refs[n_in:n_in + n_out]
        scr = refs[n_in + n_out:n_in + n_out + n_scr]
        if nk == 1:
            epilogue(_dot(a_ref[...], b_ref[...], contract), ex, outs, *scr)
        else:
            acc = refs[n_in + n_out + n_scr]
            k = pl.program_id(2)

            @pl.when(k == 0)
            def _():
                acc[...] = jnp.zeros_like(acc)

            acc[...] += _dot(a_ref[...], b_ref[...], contract)

            @pl.when(k == nk - 1)
            def _():
                epilogue(acc if acc_as_ref else acc[...], ex, outs, *scr)

    acc_scratch = [pltpu.VMEM(acc_shape, F32)] if nk > 1 else []
    filled = [] if fill is None else [fill]
    return pl.pallas_call(
        body, name=name, grid=grid,
        in_specs=[a_spec, b_spec, *extra_specs, *[ANY_SPEC] * (len(run_after) + len(filled))],
        out_specs=out_specs, out_shape=out_shape, scratch_shapes=[*scratch, *acc_scratch],
        input_output_aliases={} if fill is None else {n_in - 1: 0},
        compiler_params=_params(semantics),
    )(a, b, *extras, *run_after, *filled)


def _stats_rows(rows, width):
    idx = lax.broadcasted_iota(jnp.int32, (8, width), 0)
    out = jnp.zeros((8, width), F32)
    for r, v in enumerate(rows):
        out = jnp.where(idx == r, jnp.broadcast_to(v, (8, width)), out)
    return out


def _layer_norm_fwd(z):
    mu = jnp.mean(z, axis=-1, keepdims=True)
    zc = z - mu
    var = jnp.mean(zc * zc, axis=-1, keepdims=True)
    rstd = lax.rsqrt(var + LN_EPS)
    return zc * rstd, rstd


def _layer_norm_bwd(dy, xhat, rstd, g):
    dxh = dy * g
    m1 = jnp.mean(dxh, axis=-1, keepdims=True)
    m2 = jnp.mean(dxh * xhat, axis=-1, keepdims=True)
    return rstd * (dxh - m1 - xhat * m2)


def _heads_scratch(rows, width):
    return pltpu.VMEM((width // HEAD_DIM, rows, HEAD_DIM), F32)


def _to_views(src_ref, view_refs, dtype, heads=None):
    nh, rows, _ = src_ref.shape
    width = nh * HEAD_DIM
    for dil, view_ref in zip(DILATIONS[1:], view_refs):
        for r in range(dil):
            for hh in (range(nh) if heads is None else heads):
                c0 = r * width + hh * HEAD_DIM
                view_ref[:, c0:c0 + HEAD_DIM] = (
                    src_ref[hh, pl.ds(r, rows // dil, stride=dil), :].astype(dtype))


def _from_views(view_refs, dst_refs):
    nh, rows, _ = dst_refs[0].shape
    width = nh * HEAD_DIM
    for dil, view_ref, dst_ref in zip(DILATIONS[1:], view_refs, dst_refs):
        for r in range(dil):
            for hh in range(nh):
                c0 = r * width + hh * HEAD_DIM
                dst_ref[hh, pl.ds(r, rows // dil, stride=dil), :] = (
                    view_ref[:, c0:c0 + HEAD_DIM].astype(F32))


def _view_shape(rows, width, dil, parts=1):
    return (rows // dil, parts * dil * width)


def _in_proj(xb, w_in_st, cos_t, sin_t, aw):
    s, d = xb.shape
    n_sh = w_in_st.shape[2]
    tm, tn = min(s, 1024), aw
    per = n_sh // tn
    grid = (s // tm, (N_CHIPS * n_sh) // tn, 1)

    def epilogue(acc, ex, outs, scr):
        cos_ref, sin_ref = ex
        h_ref, v4_ref, v16_ref = outs
        seg = pl.program_id(1)

        heads = [slice(hh * HEAD_DIM, (hh + 1) * HEAD_DIM) for hh in range(tn // HEAD_DIM)]

        @pl.when(seg < 2)
        def _():
            cos, sin = cos_ref[...], sin_ref[...]
            for hh, sl in enumerate(heads):
                t = acc[:, sl]
                scr[hh] = t * cos + pltpu.roll(t, HEAD_DIM // 2, 1) * sin

        @pl.when(seg == 2)
        def _():
            for hh, sl in enumerate(heads):
                scr[hh] = acc[:, sl]

        @pl.when(seg < 3)
        def _():
            for hh, sl in enumerate(heads):
                h_ref[:, sl] = scr[hh].astype(BF16)
            _to_views(scr, (v4_ref, v16_ref), BF16)

        @pl.when(seg == 3)
        def _():
            h_ref[...] = acc.astype(BF16)

        @pl.when(seg >= 4)
        def _():
            h_ref[...] = (0.5 * jnp.tanh(0.5 * acc) + 0.5).astype(BF16)

    def view_spec(dil):
        return pl.BlockSpec((tm // dil, dil * aw), lambda i, j, k: (i, jnp.minimum(j, 2)))

    return _mm(
        "in_proj", grid, xb, pl.BlockSpec((tm, d), lambda i, j, k: (i, 0)),
        w_in_st, pl.BlockSpec((None, d, tn), lambda i, j, k: (j // per, 0, j % per)), NN,
        [cos_t, sin_t], [pl.BlockSpec((tm, HEAD_DIM), lambda i, j, k: (i, 0))] * 2,
        [jax.ShapeDtypeStruct((s, N_CHIPS * n_sh), BF16)]
        + [jax.ShapeDtypeStruct(_view_shape(s, aw, dil, 3), BF16) for dil in DILATIONS[1:]],
        [pl.BlockSpec((tm, tn), lambda i, j, k: (i, j))] + [view_spec(dil) for dil in DILATIONS[1:]],
        epilogue, None, scratch=[_heads_scratch(tm, aw)],
        semantics=("parallel", "arbitrary", "arbitrary"))


def _pack_heads(cols):
    rows, rep = cols[0].shape[0], HEAD_DIM // len(cols)
    lane = lax.broadcasted_iota(jnp.int32, (rows, HEAD_DIM), 1)
    out = jnp.zeros((rows, HEAD_DIM), F32)
    for hh, col in enumerate(cols):
        out = jnp.where((lane >= hh * rep) & (lane < (hh + 1) * rep), col, out)
    return out


def _head_col(packed, hh, nh):
    lane = lax.broadcasted_iota(jnp.int32, packed.shape, 1)
    return jnp.sum(jnp.where(lane == hh * (HEAD_DIM // nh), packed, 0.0), axis=-1, keepdims=True)


def _band_masks(block_idx):
    qi = lax.broadcasted_iota(jnp.int32, (SUB_BLOCK, 2 * SUB_BLOCK), 0)
    kj = lax.broadcasted_iota(jnp.int32, (SUB_BLOCK, 2 * SUB_BLOCK), 1)
    first_key = jnp.where(block_idx > 0, 0, SUB_BLOCK)
    return (kj >= qi) & (kj <= qi + SUB_BLOCK) & (kj >= first_key)


def _attn_fwd(name, q, k, v, offs, cw, width):
    m = q.shape[0]
    nh = cw // HEAD_DIM
    grid = (width // cw, m // SUB_BLOCK)
    scale = HEAD_DIM ** -0.5

    def body(q_ref, k_ref, v_ref, o_ref, lse_ref, kprev, vprev):
        n = pl.program_id(1)
        valid = _band_masks(n)

        @pl.when(n == 0)
        def _():
            kprev[...] = jnp.zeros_like(kprev)
            vprev[...] = jnp.zeros_like(vprev)

        heads = [slice(hh * HEAD_DIM, (hh + 1) * HEAD_DIM) for hh in range(nh)]
        scs = [_dot(q_ref[:, sl], jnp.concatenate([kprev[:, sl], k_ref[:, sl]], axis=0), NT)
               for sl in heads]
        lses = []
        for hh, sl in enumerate(heads):
            vcat = jnp.concatenate([vprev[:, sl], v_ref[:, sl]], axis=0)
            sc = jnp.where(valid, scs[hh] * scale, NEG)
            mx = jnp.max(sc, axis=-1, keepdims=True)
            p = jnp.exp(sc - mx)
            l = jnp.sum(p, axis=-1, keepdims=True)
            o = _dot(p.astype(BF16), vcat, NN) / l
            o_ref[:, sl] = o.astype(BF16)
            lses.append(mx + jnp.log(l))
        lse_ref[...] = _pack_heads(lses)
        kprev[...] = k_ref[...]
        vprev[...] = v_ref[...]

    def cur(off):
        return pl.BlockSpec((SUB_BLOCK, cw), lambda j, n: (n, j + off))

    return pl.pallas_call(
        body, name=name, grid=grid,
        in_specs=[cur(offs[0]), cur(offs[1]), cur(offs[2])],
        out_specs=[cur(0), pl.BlockSpec((SUB_BLOCK, HEAD_DIM), lambda j, n: (n, j))],
        out_shape=[jax.ShapeDtypeStruct((m, width), BF16),
                   jax.ShapeDtypeStruct((m, width // cw * HEAD_DIM), F32)],
        scratch_shapes=[pltpu.VMEM((SUB_BLOCK, cw), BF16)] * 2,
        compiler_params=_params(("parallel", "arbitrary")),
    )(q, k, v)


def _view_spec(tm, aw, dil):
    return pl.BlockSpec((tm // dil, dil * aw), lambda i: (i, 0))


def _attn_combine(o1, l1, o_views, l_views):
    s, aw = o1.shape
    nh = aw // HEAD_DIM
    tm = min(s, 512)

    def body(o1_ref, l1_ref, o4_ref, o16_ref, l4_ref, l16_ref, o_ref, lse_ref, lse4_ref, lse16_ref,
             so4, so16, sl4, sl16, stot):
        _from_views((o4_ref, o16_ref), (so4, so16))
        _from_views((l4_ref, l16_ref), (sl4, sl16))
        a, b, c = l1_ref[...], sl4[0], sl16[0]
        mx = jnp.maximum(jnp.maximum(a, b), c)
        ea, eb, ec = jnp.exp(a - mx), jnp.exp(b - mx), jnp.exp(c - mx)
        tot = ea + eb + ec
        inv = 1.0 / tot
        wa, wb, wc = ea * inv, eb * inv, ec * inv
        lse_tot = mx + jnp.log(tot)
        stot[0] = lse_tot
        lse_ref[...] = lse_tot
        _to_views(stot, (lse4_ref, lse16_ref), F32)
        for hh in range(nh):
            sl = slice(hh * HEAD_DIM, (hh + 1) * HEAD_DIM)
            o = (_head_col(wa, hh, nh) * o1_ref[:, sl].astype(F32) + _head_col(wb, hh, nh) * so4[hh]
                 + _head_col(wc, hh, nh) * so16[hh])
            o_ref[:, sl] = o.astype(BF16)

    row = pl.BlockSpec((tm, aw), lambda i: (i, 0))
    stat = pl.BlockSpec((tm, HEAD_DIM), lambda i: (i, 0))
    views = [_view_spec(tm, aw, dil) for dil in DILATIONS[1:]]
    stat_views = [_view_spec(tm, HEAD_DIM, dil) for dil in DILATIONS[1:]]
    return pl.pallas_call(
        body, name="attn_combine", grid=(s // tm,), in_specs=[row, stat, *views, *stat_views],
        out_specs=[row, stat, *stat_views],
        out_shape=[jax.ShapeDtypeStruct((s, aw), BF16), jax.ShapeDtypeStruct((s, HEAD_DIM), F32)]
        + [jax.ShapeDtypeStruct(_view_shape(s, HEAD_DIM, dil), F32) for dil in DILATIONS[1:]],
        scratch_shapes=[_heads_scratch(tm, aw)] * 2 + [_heads_scratch(tm, HEAD_DIM)] * 3,
        compiler_params=_params(("parallel",)),
    )(o1, l1, *o_views, *l_views)


def _pool_counts(tm, rows, pgw, row0):
    t = lax.broadcasted_iota(jnp.int32, (rows, len(POOL_WINDOWS) * pgw), 0) + row0
    col = lax.broadcasted_iota(jnp.int32, (rows, len(POOL_WINDOWS) * pgw), 1)
    w = jnp.full((rows, len(POOL_WINDOWS) * pgw), POOL_WINDOWS[0], jnp.int32)
    for g in range(1, len(POOL_WINDOWS)):
        w = jnp.where(col >= g * pgw, POOL_WINDOWS[g], w)
    return jnp.minimum(t + 1, w).astype(F32)


def _window_sums(xs, direction, pgw):
    rows = xs.shape[0]
    acc = xs
    out = None
    col = lax.broadcasted_iota(jnp.int32, xs.shape, 1)
    for g, w in enumerate(POOL_WINDOWS):
        sh = w // 2
        acc = acc + pltpu.roll(acc, sh if direction > 0 else rows - sh, 0)
        out = acc if out is None else jnp.where(col >= g * pgw, acc, out)
    return out


def _pool_fwd(h, wp, scale, aw):
    s = h.shape[0]
    pw_ = aw
    pgw = pw_ // len(POOL_WINDOWS)
    tm = min(s, 512)
    hb = tm // POOL_HALO

    def body(u_ref, halo_ref, wp_ref, sc_ref, p_ref, pw_ref, y_ref):
        i = pl.program_id(0)
        u = u_ref[...].astype(F32)
        halo = halo_ref[...].astype(F32) * jnp.where(i > 0, 1.0, 0.0)
        xs = jnp.concatenate([halo, u], axis=0)
        sums = _window_sums(xs, +1, pgw)[POOL_HALO:]
        p = (sums / _pool_counts(tm, tm, pgw, i * tm) - u).astype(BF16)
        p_ref[...] = p
        sc = sc_ref[...]
        for g in range(len(POOL_WINDOWS)):
            sl = slice(g * pgw, (g + 1) * pgw)
            pw = _dot(p[:, sl], wp_ref[g], NN)
            pw_ref[:, sl] = pw.astype(BF16)
            y_ref[:, sl] = (pw * sc[:, sl]).astype(BF16)

    out = jax.ShapeDtypeStruct((s, pw_), BF16)
    row = pl.BlockSpec((tm, pw_), lambda i: (i, 0))
    return pl.pallas_call(
        body, name="pool_fwd", grid=(s // tm,),
        in_specs=[pl.BlockSpec((tm, pw_), lambda i: (i, 3)),
                  pl.BlockSpec((POOL_HALO, pw_), lambda i: (jnp.maximum(i * hb - 1, 0), 3)),
                  pl.BlockSpec(wp.shape, lambda i: (0, 0, 0)),
                  pl.BlockSpec((1, pw_), lambda i: (0, 0))],
        out_specs=[row, row, row], out_shape=[out, out, out],
        compiler_params=_params(("parallel",)),
    )(h, h, wp, scale)


def _branch_merge(o_attn, y, wba_st, wbp_st, h, aw, d):
    s = o_attn.shape[0]
    tn = wba_st.shape[2]
    tm = min(s, 1024)
    ga0 = 4 * aw // tn
    gp0 = (4 * aw + d) // tn

    def body(o_ref, y_ref, wa_ref, wp_ref, sga_ref, sgp_ref, ya_ref, yp_ref, mg_ref):
        ya = _dot(o_ref[...], wa_ref[...], NN)
        yp = _dot(y_ref[...], wp_ref[...], NN)
        ya_ref[...] = ya.astype(BF16)
        yp_ref[...] = yp.astype(BF16)
        mg_ref[...] = (sga_ref[...].astype(F32) * ya + sgp_ref[...].astype(F32) * yp).astype(BF16)

    out = jax.ShapeDtypeStruct((s, d), BF16)
    blk = pl.BlockSpec((tm, tn), lambda i, j: (i, j))
    return pl.pallas_call(
        body, name="branch_merge", grid=(s // tm, N_CHIPS),
        in_specs=[pl.BlockSpec((tm, aw), lambda i, j: (i, 0)),
                  pl.BlockSpec((tm, aw), lambda i, j: (i, 0)),
                  pl.BlockSpec((None, aw, tn), lambda i, j: (j, 0, 0)),
                  pl.BlockSpec((None, aw, tn), lambda i, j: (j, 0, 0)),
                  pl.BlockSpec((tm, tn), lambda i, j: (i, j + ga0)),
                  pl.BlockSpec((tm, tn), lambda i, j: (i, j + gp0))],
        out_specs=[blk, blk, blk], out_shape=[out, out, out],
        compiler_params=_params(("parallel", "parallel")),
    )(o_attn, y, wba_st, wbp_st, h, h)


def _mix_norm(merged, w_out, x, g1, b1):
    s, d = x.shape
    tm = min(s, 256)

    def epilogue(acc, ex, outs):
        x_ref, g_ref, b_ref = ex
        xh_ref, rs_ref, xb_ref = outs
        xhat, rstd = _layer_norm_fwd(ALPHA * x_ref[...] + acc)
        xh_ref[...] = xhat
        rs_ref[...] = rstd
        xb_ref[...] = (xhat * g_ref[...] + b_ref[...]).astype(BF16)

    row = pl.BlockSpec((tm, d), lambda i, j, k: (i, 0))
    vec = pl.BlockSpec((1, d), lambda i, j, k: (0, 0))
    return _mm(
        "mix_norm", (s // tm, 1, 1), merged, row, w_out, pl.BlockSpec((d, d), lambda i, j, k: (0, 0)),
        NN, [x, g1, b1], [row, vec, vec],
        [jax.ShapeDtypeStruct((s, d), F32), jax.ShapeDtypeStruct((s, 1), F32),
         jax.ShapeDtypeStruct((s, d), BF16)],
        [row, pl.BlockSpec((tm, 1), lambda i, j, k: (i, 0)), row], epilogue, None)


def _ff_up(x1b, w1_st):
    s, d = x1b.shape
    n_sh = w1_st.shape[2]
    tm, tn = min(s, 1024), min(n_sh, 1024)
    per = n_sh // tn

    def epilogue(acc, ex, outs):
        r = jnp.maximum(acc, 0.0)
        outs[0][...] = (r * r).astype(BF16)
        outs[1][...] = (2.0 * r).astype(BF16)

    blk = pl.BlockSpec((tm, tn), lambda i, j, k: (i, j))
    out = jax.ShapeDtypeStruct((s, N_CHIPS * n_sh), BF16)
    return _mm(
        "ff_up", (s // tm, N_CHIPS * per, 1), x1b, pl.BlockSpec((tm, d), lambda i, j, k: (i, 0)),
        w1_st, pl.BlockSpec((None, d, tn), lambda i, j, k: (j // per, 0, j % per)), NN, [], [],
        [out, out], [blk, blk], epilogue, None)


def _ff_down_loss(r, w2, xhat1, g1, b1, g2, b2, target):
    s, d = xhat1.shape
    dff = r.shape[1]
    tm, tk = min(s, 512), min(dff, 1024)
    ch = min(tm, EPILOGUE_ROWS)

    def epilogue(acc_ref, ex, outs):
        xh1_ref, g1_ref, b1_ref, g2_ref, b2_ref, t_ref = ex
        dz_ref, dzb_ref, st_ref = outs
        g1v, b1v, g2v, b2v = g1_ref[...], b1_ref[...], g2_ref[...], b2_ref[...]
        dg = db = loss = None
        for c in range(tm // ch):
            rows = slice(c * ch, (c + 1) * ch)
            x1 = xh1_ref[rows, :] * g1v + b1v
            xhat2, rstd2 = _layer_norm_fwd(ALPHA * x1 + acc_ref[rows, :])
            err = xhat2 * g2v + b2v - t_ref[rows, :]
            dy = err * (1.0 / d)
            dz = _layer_norm_bwd(dy, xhat2, rstd2, g2v)
            dz_ref[rows, :] = dz
            dzb_ref[rows, :] = dz.astype(BF16)
            parts = (jnp.sum(dy * xhat2, axis=0, keepdims=True), jnp.sum(dy, axis=0, keepdims=True),
                     jnp.sum(jnp.sum(err * err, axis=-1, keepdims=True), axis=0, keepdims=True))
            dg, db, loss = parts if c == 0 else (dg + parts[0], db + parts[1], loss + parts[2])
        st_ref[...] = _stats_rows([dg, db, jnp.broadcast_to((0.5 / d) * loss, (1, d))], d)

    row = pl.BlockSpec((tm, d), lambda i, j, k: (i, 0))
    vec = pl.BlockSpec((1, d), lambda i, j, k: (0, 0))
    return _mm(
        "ff_down_loss", (s // tm, 1, dff // tk), r, pl.BlockSpec((tm, tk), lambda i, j, k: (i, k)),
        w2, pl.BlockSpec((tk, d), lambda i, j, k: (k, 0)), NN,
        [xhat1, g1, b1, g2, b2, target], [row, vec, vec, vec, vec, row],
        [jax.ShapeDtypeStruct((s, d), F32), jax.ShapeDtypeStruct((s, d), BF16),
         jax.ShapeDtypeStruct((s // tm, 8, d), F32)],
        [row, row, pl.BlockSpec((None, 8, d), lambda i, j, k: (i, 0, 0))], epilogue, (tm, d),
        acc_as_ref=True)


def _ff_down_bwd(dz2b, w2, r_slope):
    s, d = dz2b.shape
    dff = r_slope.shape[1]
    tm, tn = min(s, 1024), min(dff, 1024)

    def epilogue(acc, ex, outs):
        outs[0][...] = (acc * ex[0][...].astype(F32)).astype(BF16)

    blk = pl.BlockSpec((tm, tn), lambda i, j, k: (i, j))
    return _mm(
        "ff_down_bwd", (s // tm, dff // tn, 1), dz2b, pl.BlockSpec((tm, d), lambda i, j, k: (i, 0)),
        w2, pl.BlockSpec((tn, d), lambda i, j, k: (j, 0)), NT, [r_slope], [blk],
        [jax.ShapeDtypeStruct((s, dff), BF16)], [blk], epilogue, None)[0]


def _wgrad(name, a, g, n_sh, shard0=0, n_shards=None, fill=None):
    s, rows = a.shape
    cols = g.shape[1]
    tm, tn, tk = min(rows, 2048), min(cols, 1024), min(s, 1024)
    if tn >= n_sh:
        span = tn // n_sh
        out_spec = pl.BlockSpec((span, tm, n_sh), lambda i, j, k: (j + shard0 // span, i, 0))

        def epilogue(acc_ref, ex, outs):
            for sh in range(span):
                outs[0][sh] = acc_ref[:, sh * n_sh:(sh + 1) * n_sh].astype(BF16)
    else:
        per = n_sh // tn
        out_spec = pl.BlockSpec((None, tm, tn), lambda i, j, k: (shard0 + j // per, i, j % per))

        def epilogue(acc_ref, ex, outs):
            outs[0][...] = acc_ref[...].astype(BF16)

    return _mm(
        name, (rows // tm, cols // tn, s // tk), a, pl.BlockSpec((tk, tm), lambda i, j, k: (k, i)),
        g, pl.BlockSpec((tk, tn), lambda i, j, k: (k, j)), TN, [], [],
        [jax.ShapeDtypeStruct((n_shards or cols // n_sh, rows, n_sh), BF16)], [out_spec], epilogue,
        (tm, tn), acc_as_ref=True, fill=fill)[0]


def _ff_up_bwd(da, w1_st, dz2, xhat1, rstd1, g1, run_after):
    s, d = dz2.shape
    n_sh = w1_st.shape[2]
    tm, tk = min(s, 512), min(n_sh, 1024)
    per = n_sh // tk
    ch = min(tm, EPILOGUE_ROWS)

    def epilogue(acc_ref, ex, outs):
        dz2_ref, xh_ref, rs_ref, g_ref = ex
        dz_ref, dzb_ref, st_ref = outs
        gv = g_ref[...]
        dg = db = None
        for c in range(tm // ch):
            rows = slice(c * ch, (c + 1) * ch)
            dx1 = ALPHA * dz2_ref[rows, :] + acc_ref[rows, :]
            xhat = xh_ref[rows, :]
            dz = _layer_norm_bwd(dx1, xhat, rs_ref[rows, :], gv)
            dz_ref[rows, :] = dz
            dzb_ref[rows, :] = dz.astype(BF16)
            parts = (jnp.sum(dx1 * xhat, axis=0, keepdims=True), jnp.sum(dx1, axis=0, keepdims=True))
            dg, db = parts if c == 0 else (dg + parts[0], db + parts[1])
        st_ref[...] = _stats_rows([dg, db], d)

    row = pl.BlockSpec((tm, d), lambda i, j, k: (i, 0))
    return _mm(
        "ff_up_bwd", (s // tm, 1, N_CHIPS * per), da, pl.BlockSpec((tm, tk), lambda i, j, k: (i, k)),
        w1_st, pl.BlockSpec((None, d, tk), lambda i, j, k: (k // per, 0, k % per)), NT,
        [dz2, xhat1, rstd1, g1],
        [row, row, pl.BlockSpec((tm, 1), lambda i, j, k: (i, 0)), pl.BlockSpec((1, d), lambda i, j, k: (0, 0))],
        [jax.ShapeDtypeStruct((s, d), F32), jax.ShapeDtypeStruct((s, d), BF16),
         jax.ShapeDtypeStruct((s // tm, 8, d), F32)],
        [row, row, pl.BlockSpec((None, 8, d), lambda i, j, k: (i, 0, 0))], epilogue, (tm, d),
        acc_as_ref=True, run_after=run_after)


def _mix_bwd(dz1b, w_out, h, ya, yp, aw):
    s, d = dz1b.shape
    tm = min(s, 256)
    gblk = 4 * aw // d

    def epilogue(acc, ex, outs):
        sga_ref, sgp_ref, ya_ref, yp_ref = ex
        dya_ref, dyp_ref, dg_ref = outs
        sga, sgp = sga_ref[...].astype(F32), sgp_ref[...].astype(F32)
        dya_ref[...] = (acc * sga).astype(BF16)
        dyp_ref[...] = (acc * sgp).astype(BF16)
        dg_ref[:, :d] = (acc * ya_ref[...].astype(F32) * (sga * (1.0 - sga))).astype(BF16)
        dg_ref[:, d:] = (acc * yp_ref[...].astype(F32) * (sgp * (1.0 - sgp))).astype(BF16)

    row = pl.BlockSpec((tm, d), lambda i, j, k: (i, 0))
    return _mm(
        "mix_bwd", (s // tm, 1, 1), dz1b, row, w_out, pl.BlockSpec((d, d), lambda i, j, k: (0, 0)), NT,
        [h, h, ya, yp],
        [pl.BlockSpec((tm, d), lambda i, j, k: (i, gblk)),
         pl.BlockSpec((tm, d), lambda i, j, k: (i, gblk + 1)), row, row],
        [jax.ShapeDtypeStruct((s, d), BF16), jax.ShapeDtypeStruct((s, d), BF16),
         jax.ShapeDtypeStruct((s, 2 * d), BF16)],
        [row, row, pl.BlockSpec((tm, 2 * d), lambda i, j, k: (i, 0))], epilogue, None)


def _branch_in_bwd(name, tm, dyb, wb_st, epilogue, extras, extra_specs, out_shape, out_specs,
                   scratch=()):
    s, d = dyb.shape
    aw = wb_st.shape[1]
    wb_t = wb_st.transpose(0, 2, 1).reshape(d, aw)
    return _mm(
        name, (s // tm, 1, 1), dyb, pl.BlockSpec((tm, d), lambda i, j, k: (i, 0)),
        wb_t, pl.BlockSpec((d, aw), lambda i, j, k: (0, 0)), NN,
        extras, extra_specs, out_shape, out_specs, epilogue, None, scratch=scratch)


def _attn_out_bwd(dya, wba_st, o_attn):
    s, aw = o_attn.shape
    tm = min(s, 512)

    def epilogue(acc_ref, ex, outs, sdo, sdl):
        do_ref, dl_ref, do4_ref, do16_ref, dl4_ref, dl16_ref = outs
        deltas = []
        for hh in range(aw // HEAD_DIM):
            sl = slice(hh * HEAD_DIM, (hh + 1) * HEAD_DIM)
            do = acc_ref[:, sl]
            deltas.append(jnp.sum(do * ex[0][:, sl].astype(F32), axis=-1, keepdims=True))
            sdo[hh] = do
            do_ref[:, sl] = do.astype(BF16)
        packed = _pack_heads(deltas)
        sdl[0] = packed
        dl_ref[...] = packed
        _to_views(sdo, (do4_ref, do16_ref), BF16)
        _to_views(sdl, (dl4_ref, dl16_ref), F32)

    def specs(width):
        return ([pl.BlockSpec((tm, width), lambda i, j, k: (i, 0))]
                + [pl.BlockSpec((tm // dil, dil * width), lambda i, j, k: (i, 0)) for dil in DILATIONS[1:]])

    def shapes(width, dtype):
        return ([jax.ShapeDtypeStruct((s, width), dtype)]
                + [jax.ShapeDtypeStruct(_view_shape(s, width, dil), dtype) for dil in DILATIONS[1:]])

    do_specs, dl_specs = specs(aw), specs(HEAD_DIM)
    do_shapes, dl_shapes = shapes(aw, BF16), shapes(HEAD_DIM, F32)
    return _branch_in_bwd(
        "attn_out_bwd", tm, dya, wba_st, epilogue, [o_attn], [do_specs[0]],
        [do_shapes[0], dl_shapes[0], *do_shapes[1:], *dl_shapes[1:]],
        [do_specs[0], dl_specs[0], *do_specs[1:], *dl_specs[1:]],
        scratch=[_heads_scratch(tm, aw), _heads_scratch(tm, HEAD_DIM)])


def _pool_out_bwd(dyp, wbp_st, pw, scale):
    s, pw_ = pw.shape
    tm = min(s, 1024)

    def epilogue(acc_ref, ex, outs):
        pw_ref, sc_ref = ex
        dpw_ref, st_ref = outs
        acc = acc_ref[...]
        dpw_ref[...] = (acc * sc_ref[...]).astype(BF16)
        st_ref[...] = _stats_rows([jnp.sum(acc * pw_ref[...].astype(F32), axis=0, keepdims=True)], pw_)

    row = pl.BlockSpec((tm, pw_), lambda i, j, k: (i, 0))
    return _branch_in_bwd(
        "pool_out_bwd", tm, dyp, wbp_st, epilogue, [pw, scale],
        [row, pl.BlockSpec((1, pw_), lambda i, j, k: (0, 0))],
        [jax.ShapeDtypeStruct((s, pw_), BF16), jax.ShapeDtypeStruct((s // tm, 8, pw_), F32)],
        [row, pl.BlockSpec((None, 8, pw_), lambda i, j, k: (i, 0, 0))])


def _pool_bwd(dpw, p, wp):
    s, pw_ = p.shape
    ng = len(POOL_WINDOWS)
    pgw = pw_ // ng
    tm = min(s, 512)
    hb = tm // POOL_HALO
    nblk = s // tm

    def body(dpw_ref, nxt_ref, p_ref, wp_ref, dwp_ref, du_ref):
        i = pl.program_id(0)
        nxt = (nxt_ref[...].astype(F32) * jnp.where(i < nblk - 1, 1.0, 0.0)).astype(BF16)
        dpw_all = jnp.concatenate([dpw_ref[...], nxt], axis=0)

        @pl.when(i == 0)
        def _():
            dwp_ref[...] = jnp.zeros_like(dwp_ref)

        dps = []
        for g in range(ng):
            sl = slice(g * pgw, (g + 1) * pgw)
            dwp_ref[g] += _dot(p_ref[:, sl], dpw_ref[:, sl], TN)
            dps.append(_dot(dpw_all[:, sl], wp_ref[g], NT))
        dp = jnp.concatenate(dps, axis=1)
        dpn = dp / _pool_counts(tm, tm + POOL_HALO, pgw, i * tm)
        du_ref[...] = (_window_sums(dpn, -1, pgw)[:tm] - dp[:tm]).astype(BF16)

    row = pl.BlockSpec((tm, pw_), lambda i: (i, 0))
    full = pl.BlockSpec((ng, pgw, pgw), lambda i: (0, 0, 0))
    return pl.pallas_call(
        body, name="pool_bwd", grid=(nblk,),
        in_specs=[row, pl.BlockSpec((POOL_HALO, pw_), lambda i: (jnp.minimum((i + 1) * hb, s // POOL_HALO - 1), 0)),
                  row, full],
        out_specs=[full, row],
        out_shape=[jax.ShapeDtypeStruct((ng, pgw, pgw), F32), jax.ShapeDtypeStruct((s, pw_), BF16)],
        compiler_params=_params(("arbitrary",)),
    )(dpw, dpw, p, wp)


def _attn_bwd(name, q, k, v, do, lse, delta, offs, cw, width):
    m = do.shape[0]
    nh = cw // HEAD_DIM
    nblk = m // SUB_BLOCK
    grid = (width // cw, nblk + 1)
    scale = HEAD_DIM ** -0.5

    def body(q_ref, k_ref, v_ref, do_ref, lse_ref, dl_ref, dq_ref, dk_ref, dv_ref,
             kprev, vprev, dk_carry, dv_carry):
        n = pl.program_id(1)

        @pl.when(n == 0)
        def _():
            for ref in (kprev, vprev, dk_carry, dv_carry):
                ref[...] = jnp.zeros_like(ref)

        qi = lax.broadcasted_iota(jnp.int32, (SUB_BLOCK, 2 * SUB_BLOCK), 0)
        kj = lax.broadcasted_iota(jnp.int32, (SUB_BLOCK, 2 * SUB_BLOCK), 1)
        first_key = jnp.where(n == 0, SUB_BLOCK, jnp.where(n == nblk, 4 * SUB_BLOCK, 0))
        valid = (kj >= qi) & (kj <= qi + SUB_BLOCK) & (kj >= first_key)
        heads = [slice(hh * HEAD_DIM, (hh + 1) * HEAD_DIM) for hh in range(nh)]
        kcats = [jnp.concatenate([kprev[:, sl], k_ref[:, sl]], axis=0) for sl in heads]
        scs = [_dot(q_ref[:, sl], kcats[hh], NT) for hh, sl in enumerate(heads)]
        dps = [_dot(do_ref[:, sl], jnp.concatenate([vprev[:, sl], v_ref[:, sl]], axis=0), NT)
               for sl in heads]
        dqs = []
        lse_all, dl_all = lse_ref[...], dl_ref[...]
        for hh, sl in enumerate(heads):
            lse_h, dl_h = _head_col(lse_all, hh, nh), _head_col(dl_all, hh, nh)
            p = jnp.where(valid, jnp.exp(jnp.where(valid, scs[hh] * scale, NEG) - lse_h), 0.0)
            ds = (p * (dps[hh] - dl_h)).astype(BF16)
            dqs.append((_dot(ds, kcats[hh], NN) * scale).astype(BF16))
            dk2 = _dot(ds, q_ref[:, sl], TN) * scale
            dv2 = _dot(p.astype(BF16), do_ref[:, sl], TN)
            dk_ref[:, sl] = (dk_carry[:, sl] + dk2[:SUB_BLOCK]).astype(BF16)
            dv_ref[:, sl] = (dv_carry[:, sl] + dv2[:SUB_BLOCK]).astype(BF16)
            dk_carry[:, sl] = dk2[SUB_BLOCK:]
            dv_carry[:, sl] = dv2[SUB_BLOCK:]

        @pl.when(n < nblk)
        def _():
            for hh, sl in enumerate(heads):
                dq_ref[:, sl] = dqs[hh]

        kprev[...] = k_ref[...]
        vprev[...] = v_ref[...]

    def cur(off):
        return pl.BlockSpec((SUB_BLOCK, cw), lambda j, n: (jnp.minimum(n, nblk - 1), j + off))

    lagged = pl.BlockSpec((SUB_BLOCK, cw), lambda j, n: (jnp.maximum(n - 1, 0), j))
    stat = pl.BlockSpec((SUB_BLOCK, HEAD_DIM), lambda j, n: (jnp.minimum(n, nblk - 1), j))
    out = jax.ShapeDtypeStruct((m, width), BF16)
    return pl.pallas_call(
        body, name=name, grid=grid,
        in_specs=[cur(offs[0]), cur(offs[1]), cur(offs[2]), cur(0), stat, stat],
        out_specs=[cur(0), lagged, lagged], out_shape=[out, out, out],
        scratch_shapes=[pltpu.VMEM((SUB_BLOCK, cw), BF16)] * 2 + [pltpu.VMEM((SUB_BLOCK, cw), F32)] * 2,
        compiler_params=_params(("parallel", "arbitrary")),
    )(q, k, v, do, lse, delta)


def _qkvu_grad(d1, d4, d16, du, cos_t, sin_t):
    s, aw = du.shape
    tm = min(s, 512)

    def body(*refs):
        nat, v4, v16 = refs[0:3], refs[3:6], refs[6:9]
        cos_ref, sin_ref, du_ref, out_ref, s4, s16 = refs[9:]
        cos, sin = cos_ref[...], sin_ref[...]
        for part in range(3):
            _from_views((v4[part], v16[part]), (s4, s16))
            for hh in range(aw // HEAD_DIM):
                sl = slice(hh * HEAD_DIM, (hh + 1) * HEAD_DIM)
                t = nat[part][:, sl].astype(F32) + s4[hh] + s16[hh]
                if part < 2:
                    t = t * cos - pltpu.roll(t, HEAD_DIM // 2, 1) * sin
                out_ref[:, part * aw + hh * HEAD_DIM:part * aw + (hh + 1) * HEAD_DIM] = t.astype(BF16)
        out_ref[:, 3 * aw:] = du_ref[...]

    row = pl.BlockSpec((tm, aw), lambda i: (i, 0))
    tab = pl.BlockSpec((tm, HEAD_DIM), lambda i: (i, 0))
    return pl.pallas_call(
        body, name="qkvu_grad", grid=(s // tm,),
        in_specs=[row] * 3 + [_view_spec(tm, aw, 4)] * 3 + [_view_spec(tm, aw, 16)] * 3 + [tab, tab, row],
        out_specs=pl.BlockSpec((tm, 4 * aw), lambda i: (i, 0)),
        out_shape=jax.ShapeDtypeStruct((s, 4 * aw), BF16),
        scratch_shapes=[_heads_scratch(tm, aw)] * 2,
        compiler_params=_params(("parallel",)),
    )(*d1, *d4, *d16, cos_t, sin_t, du)


def _in_proj_bwd_x(name, dh, w_in_st, shard0, base, scale_base, run_after=()):
    s, kdim = dh.shape
    d, n_sh = w_in_st.shape[1], w_in_st.shape[2]
    tm, tk = min(s, 512), min(n_sh, 2048)
    per = n_sh // tk

    ch = min(tm, 2 * EPILOGUE_ROWS)

    def epilogue(acc_ref, ex, outs):
        for c in range(tm // ch):
            rows = slice(c * ch, (c + 1) * ch)
            outs[0][rows, :] = scale_base * ex[0][rows, :] + acc_ref[rows, :]

    row = pl.BlockSpec((tm, d), lambda i, j, k: (i, 0))
    return _mm(
        name, (s // tm, 1, kdim // tk), dh, pl.BlockSpec((tm, tk), lambda i, j, k: (i, k)),
        w_in_st, pl.BlockSpec((None, d, tk), lambda i, j, k: (shard0 + k // per, 0, k % per)), NT,
        [base], [row], [jax.ShapeDtypeStruct((s, d), F32)], [row], epilogue, (tm, d),
        acc_as_ref=True, run_after=run_after)[0]


def _chip_peers():
    x, y, c = lax.axis_index("x"), lax.axis_index("y"), lax.axis_index("c")
    return x, y, c, [(1 - x, y), (x, 1 - y), (1 - x, 1 - y)]


GATHER, GATHER_HALF, SCATTER, SIBLING = "gather", "gather_half", "scatter", "sibling"


def _exchange_peers(mode):
    x, y, c, chips = _chip_peers()
    if mode == SIBLING:
        return x, y, c, [(x, y, 1 - c)]
    return x, y, c, [(px, py, c) for px, py in chips]


def _core_half(ref_or_shape, c):
    rows = (ref_or_shape.shape[0]) // 2
    return pl.ds(c * rows, rows)


def _exchange_descriptor(mode, src, land, send, recv, p, peer, me, arriving):
    pid = 2 * peer[0] + peer[1]
    if mode == GATHER:
        src_ref, dst_ref = src, land.at[pid if arriving else me]
    elif mode == GATHER_HALF:
        rows = _core_half(src, peer[2])
        src_ref, dst_ref = src.at[rows], land.at[pid if arriving else me, rows]
    elif mode == SCATTER:
        src_ref, dst_ref = src.at[pid], land.at[p]
    else:
        src_ref, dst_ref = src, land
    return pltpu.make_async_remote_copy(
        src_ref=src_ref, dst_ref=dst_ref, send_sem=send.at[p], recv_sem=recv.at[p],
        device_id=peer, device_id_type=MESH)


def _exchange_start(name, mode, srcs, land_shapes):
    n = len(srcs)
    lands = [_hbm(lax.empty(shape, src.dtype)) for shape, src in zip(land_shapes, srcs)]

    def body(*refs):
        src_refs, land_refs = refs[:n], refs[n:2 * n]
        sends, recvs = refs[2 * n:3 * n], refs[3 * n:4 * n]
        token = refs[6 * n]
        x, y, c, peers = _exchange_peers(mode)
        me = 2 * x + y
        for w in range(n):
            for p, peer in enumerate(peers):
                _exchange_descriptor(mode, src_refs[w], land_refs[w], sends[w], recvs[w], p, peer,
                                     me, arriving=False).start()
        token[...] = jnp.zeros_like(token)

    sem = pltpu.SemaphoreType.DMA((3,))
    outs = pl.pallas_call(
        body, name=name, in_specs=[HBM_SPEC] * (2 * n),
        out_specs=[SEM_SPEC] * (2 * n) + [HBM_SPEC] * (2 * n) + [pl.BlockSpec(memory_space=pltpu.VMEM)],
        out_shape=[sem] * (2 * n) + [pltpu.HBM(a.shape, a.dtype) for a in (*srcs, *lands)]
        + [jax.ShapeDtypeStruct((8, 128), F32)],
        input_output_aliases={i: 2 * n + i for i in range(2 * n)},
        compiler_params=pltpu.CompilerParams(has_side_effects=DATAFLOW),
    )(*[_hbm(a) for a in srcs], *lands)
    return {"send": outs[:n], "recv": outs[n:2 * n], "src": outs[2 * n:3 * n],
            "land": outs[3 * n:4 * n], "token": outs[4 * n]}


def _exchange_wait(name, mode, started, which, after):
    m = len(which)

    def body(*refs):
        src_refs, land_refs = refs[:m], refs[m:2 * m]
        sends, recvs = refs[2 * m:3 * m], refs[3 * m:4 * m]
        x, y, c, peers = _exchange_peers(mode)
        me = 2 * x + y
        for w in range(m):
            for p, peer in enumerate(peers):
                _exchange_descriptor(mode, src_refs[w], land_refs[w], sends[w], recvs[w], p, peer,
                                     me, arriving=False).wait_send()
                _exchange_descriptor(mode, src_refs[w], land_refs[w], sends[w], recvs[w], p, peer,
                                     me, arriving=True).wait_recv()

    pick = lambda key: [started[key][w] for w in which]
    bufs = pick("src") + pick("land")
    after = list(after) if isinstance(after, (list, tuple)) else [after]
    outs = pl.pallas_call(
        body, name=name,
        in_specs=[HBM_SPEC] * (2 * m) + [SEM_SPEC] * (2 * m) + [ANY_SPEC] * len(after),
        out_specs=[HBM_SPEC] * (2 * m), out_shape=[pltpu.HBM(a.shape, a.dtype) for a in bufs],
        input_output_aliases={i: i for i in range(2 * m)},
        compiler_params=pltpu.CompilerParams(has_side_effects=DATAFLOW),
    )(*bufs, *pick("send"), *pick("recv"), *after)
    return outs[:m], outs[m:]


def _to_bf16(name, a, run_after):
    r, c = a.shape
    tm = min(r, 512)

    def body(a_ref, after_ref, out_ref):
        out_ref[...] = a_ref[...].astype(BF16)

    blk = pl.BlockSpec((tm, c), lambda i: (i, 0))
    return pl.pallas_call(
        body, name=name, grid=(r // tm,), in_specs=[blk, ANY_SPEC], out_specs=blk,
        out_shape=jax.ShapeDtypeStruct((r, c), BF16), compiler_params=_params(("parallel",)),
    )(a, run_after)


def _swap_halves(name, lands):
    n = len(lands)

    def body(*refs):
        bufs = refs[n:2 * n]
        send, recv = refs[2 * n:]
        x, y, c, chips = _chip_peers()
        started = []
        for w in range(n):
            half = bufs[w].shape[1] // 2
            for p, (px, py) in enumerate(chips):
                mine = bufs[w].at[2 * px + py, pl.ds(c * half, half)]
                cp = pltpu.make_async_remote_copy(
                    src_ref=mine, dst_ref=mine, send_sem=send.at[w, p], recv_sem=recv.at[w, p],
                    device_id=(x, y, 1 - c), device_id_type=MESH)
                cp.start()
                started.append(cp)
        for w in range(n):
            half = bufs[w].shape[1] // 2
            for p, (px, py) in enumerate(chips):
                theirs = bufs[w].at[2 * px + py, pl.ds((1 - c) * half, half)]
                pltpu.make_async_remote_copy(
                    src_ref=theirs, dst_ref=theirs, send_sem=send.at[w, p], recv_sem=recv.at[w, p],
                    device_id=(x, y, 1 - c), device_id_type=MESH).wait_recv()
        for cp in started:
            cp.wait_send()

    return pl.pallas_call(
        body, name=name, in_specs=[ANY_SPEC] * n, out_specs=[ANY_SPEC] * n,
        out_shape=[jax.ShapeDtypeStruct(a.shape, a.dtype) for a in lands],
        input_output_aliases={i: i for i in range(n)},
        scratch_shapes=[pltpu.SemaphoreType.DMA((n, 3)), pltpu.SemaphoreType.DMA((n, 3))],
    )(*lands)


def _place_own(name, shard, land, me):
    r, c = shard.shape
    tm = min(r, 512)

    def body(me_ref, shard_ref, land_ref, out_ref):
        out_ref[...] = shard_ref[...]

    return pl.pallas_call(
        body, name=name,
        grid_spec=pltpu.PrefetchScalarGridSpec(
            num_scalar_prefetch=1, grid=(r // tm,),
            in_specs=[pl.BlockSpec((tm, c), lambda i, me_ref: (i, 0)), ANY_SPEC],
            out_specs=pl.BlockSpec((None, tm, c), lambda i, me_ref: (me_ref[0], i, 0))),
        out_shape=jax.ShapeDtypeStruct(land.shape, land.dtype), input_output_aliases={2: 0},
        compiler_params=_params(("arbitrary",)),
    )(me, shard, land)


def _sum_slabs(name, grads, land, me):
    _, r, c = grads.shape
    tm = min(r, 256)

    def body(me_ref, own_ref, land_ref, out_ref):
        acc = own_ref[...].astype(F32)
        for p in range(3):
            acc = acc + land_ref[p].astype(F32)
        out_ref[...] = acc

    return pl.pallas_call(
        body, name=name,
        grid_spec=pltpu.PrefetchScalarGridSpec(
            num_scalar_prefetch=1, grid=(r // tm,),
            in_specs=[pl.BlockSpec((None, tm, c), lambda i, me_ref: (me_ref[0], i, 0)),
                      pl.BlockSpec((3, tm, c), lambda i, me_ref: (0, i, 0))],
            out_specs=pl.BlockSpec((tm, c), lambda i, me_ref: (i, 0))),
        out_shape=jax.ShapeDtypeStruct((r, c), F32), compiler_params=_params(("parallel",)),
    )(me, grads, land)


def _allreduce_stats(stats, run_after):
    n = len(stats)

    def body(*refs):
        ins, outs = refs[:n], refs[n + 1:2 * n + 1]
        mine, gath = refs[2 * n + 1:3 * n + 1], refs[3 * n + 1:4 * n + 1]
        send, recv = refs[4 * n + 1:]
        x, y, c = lax.axis_index("x"), lax.axis_index("y"), lax.axis_index("c")
        me = 4 * x + 2 * y + c
        flips = [(bx, by, bc) for bx in (0, 1) for by in (0, 1) for bc in (0, 1)][1:]

        def peer(f):
            return (x + f[0] * (1 - 2 * x), y + f[1] * (1 - 2 * y), c + f[2] * (1 - 2 * c))

        copies = []
        for t in range(n):
            tot = ins[t][0]
            for b in range(1, ins[t].shape[0]):
                tot = tot + ins[t][b]
            mine[t][...] = tot
            gath[t][me] = tot
            for k, f in enumerate(flips):
                cp = pltpu.make_async_remote_copy(
                    src_ref=mine[t], dst_ref=gath[t].at[me], send_sem=send.at[t, k],
                    recv_sem=recv.at[t, k], device_id=peer(f), device_id_type=MESH)
                cp.start()
                copies.append(cp)
        for t in range(n):
            for k, f in enumerate(flips):
                px, py, pc = peer(f)
                pltpu.make_async_remote_copy(
                    src_ref=mine[t], dst_ref=gath[t].at[4 * px + 2 * py + pc], send_sem=send.at[t, k],
                    recv_sem=recv.at[t, k], device_id=(px, py, pc), device_id_type=MESH).wait_recv()
        for cp in copies:
            cp.wait_send()
        for t in range(n):
            tot = gath[t][0]
            for dev in range(1, 8):
                tot = tot + gath[t][dev]
            outs[t][...] = tot

    vm = pl.BlockSpec(memory_space=pltpu.VMEM)
    return pl.pallas_call(
        body, name="allreduce_stats", in_specs=[vm] * n + [ANY_SPEC], out_specs=[vm] * n,
        out_shape=[jax.ShapeDtypeStruct(s.shape[1:], F32) for s in stats],
        scratch_shapes=[pltpu.VMEM(s.shape[1:], F32) for s in stats]
        + [pltpu.VMEM((8, *s.shape[1:]), F32) for s in stats]
        + [pltpu.SemaphoreType.DMA((n, 7)), pltpu.SemaphoreType.DMA((n, 7))],
    )(*stats, run_after)


def _adamw(name, w, m, v, g_parts):
    r, c = w.shape
    tm = min(r, 128)
    n_g = len(g_parts)

    def body(*refs):
        w_ref, m_ref, v_ref = refs[:3]
        g_refs = refs[3:3 + n_g]
        g_out, d_out, m_out, v_out = refs[3 + n_g:]
        g = g_refs[0][...]
        for gr in g_refs[1:]:
            g = g + gr[...]
        m_new = ADAM_B1 * m_ref[...] + (1.0 - ADAM_B1) * g
        v_new = ADAM_B2 * v_ref[...] + (1.0 - ADAM_B2) * (g * g)
        m_hat = m_new / (1.0 - ADAM_B1 ** ADAM_STEP)
        v_hat = v_new / (1.0 - ADAM_B2 ** ADAM_STEP)
        g_out[...] = g
        d_out[...] = -ADAM_LR * (m_hat / (jnp.sqrt(v_hat) + ADAM_EPS) + ADAM_WD * w_ref[...])
        m_out[...] = m_new
        v_out[...] = v_new

    blk = pl.BlockSpec((tm, c), lambda i: (i, 0))
    out = jax.ShapeDtypeStruct((r, c), F32)
    return pl.pallas_call(
        body, name=name, grid=(r // tm,), in_specs=[blk] * (3 + n_g), out_specs=[blk] * 4,
        out_shape=[out] * 4, compiler_params=_params(("parallel",)),
    )(w, m, v, *g_parts)


def _rope_tables(positions):
    half = HEAD_DIM // 2
    inv_freq = ROPE_THETA ** (-jnp.arange(half, dtype=F32) / half)
    ang = positions.astype(F32)[0, :, None] * inv_freq
    cos, sin = jnp.cos(ang), jnp.sin(ang)
    return jnp.concatenate([cos, cos], axis=-1), jnp.concatenate([-sin, sin], axis=-1)


def kernel(x, positions, w_in, w_pool, pool_scale, w_branch_attn, w_branch_pool, w_out, ln_mix_g, ln_mix_b, w_ff1, w_ff2, ln_ff_g, ln_ff_b, loss_target, m_w_in, m_w_pool, m_pool_scale, m_w_branch_attn, m_w_branch_pool, m_w_out, m_ln_mix_g, m_ln_mix_b, m_w_ff1, m_w_ff2, m_ln_ff_g, m_ln_ff_b, v_w_in, v_w_pool, v_pool_scale, v_w_branch_attn, v_w_branch_pool, v_w_out, v_ln_mix_g, v_ln_mix_b, v_w_ff1, v_w_ff2, v_ln_ff_g, v_ln_ff_b):
    s, d = x.shape[1], x.shape[2]
    aw = d // 2
    ng = len(POOL_WINDOWS)
    pgw = aw // ng
    x2d, target = x[0], loss_target[0]
    xb = x2d.astype(BF16)
    cos_t, sin_t = _rope_tables(positions)

    big = {"w_in": w_in[0], "w_pool": w_pool[0].reshape(-1, pgw), "w_branch_attn": w_branch_attn[0],
           "w_branch_pool": w_branch_pool[0], "w_out": w_out[0], "w_ff1": w_ff1[0], "w_ff2": w_ff2[0]}
    names = list(big)
    me_chip = (2 * lax.axis_index("x") + lax.axis_index("y")).astype(jnp.int32).reshape(1)
    land_shapes = [(N_CHIPS, *big[k].shape) for k in names]
    gathering_in = _exchange_start("gather_start_in", GATHER_HALF,
                                   [_to_bf16("to_bf16_w_in", big["w_in"], positions)], land_shapes[:1])
    shards = [_to_bf16(f"to_bf16_{k}", big[k], gathering_in["token"]) for k in names[1:]]
    gathering = _exchange_start("gather_start", GATHER, shards, land_shapes[1:])

    def gathered(name, which, after):
        srcs, lands = _exchange_wait(f"gather_wait_{name}", GATHER, gathering, which, after)
        return [_place_own(f"place_own_{names[w + 1]}", srcs[i], lands[i], me_chip)
                for i, w in enumerate(which)]

    rows_sh = pgw // N_CHIPS
    dff = N_CHIPS * big["w_ff2"].shape[0]

    srcs, lands = _exchange_wait("gather_wait_in", GATHER_HALF, gathering_in, [0],
                                 [gathering["token"], xb, cos_t, sin_t])
    w_in_st = _place_own("place_own_w_in", srcs[0], _swap_halves("swap_halves_in", lands)[0], me_chip)
    h, hv4, hv16 = _in_proj(xb, w_in_st, cos_t, sin_t, aw)
    (wp_st,) = gathered("pool", [0], h)
    wp = wp_st.reshape(N_CHIPS, ng, rows_sh, pgw).transpose(1, 0, 2, 3).reshape(ng, pgw, pgw)
    qkv = {1: (h, h, h), 4: (hv4, hv4, hv4), 16: (hv16, hv16, hv16)}
    offs = {dil: (0, dil, 2 * dil) for dil in DILATIONS}
    o_parts, lse_parts = [], []
    for dil in DILATIONS:
        o_p, lse_p = _attn_fwd(f"attn_fwd_d{dil}", *qkv[dil], offs[dil], aw, dil * aw)
        o_parts.append(o_p)
        lse_parts.append(lse_p)
    o_attn, lse, lse4, lse16 = _attn_combine(o_parts[0], lse_parts[0], o_parts[1:], lse_parts[1:])
    p, pw, y = _pool_fwd(h, wp, pool_scale, aw)
    wba_st, wbp_st, w_out_st = gathered("mix", [1, 2, 3], y)
    w_out_full = w_out_st.reshape(d, d)
    ya, yp, merged = _branch_merge(o_attn, y, wba_st, wbp_st, h, aw, d)
    xhat1, rstd1, x1b = _mix_norm(merged, w_out_full, x2d, ln_mix_g, ln_mix_b)
    w1_st, w2_st = gathered("ff", [4, 5], x1b)
    w2_full = w2_st.reshape(dff, d)
    r, r_slope = _ff_up(x1b, w1_st)
    dz2, dz2b, st2 = _ff_down_loss(r, w2_full, xhat1, ln_mix_g, ln_mix_b, ln_ff_g, ln_ff_b, target)

    def scatter_start(name, grads):
        return _exchange_start(f"scatter_start_{name}", SCATTER, grads, [(3, *g.shape[1:]) for g in grads])

    da = _ff_down_bwd(dz2b, w2_full, r_slope)
    g_w2 = _wgrad("wgrad_ff2", r, dz2b, d).reshape(N_CHIPS, dff // N_CHIPS, d)
    g_w1 = _wgrad("wgrad_ff1", x1b, da, dff // N_CHIPS)
    sent_ff = scatter_start("ff", [g_w1, g_w2])
    dz1, dz1b, st1 = _ff_up_bwd(da, w1_st, dz2, xhat1, rstd1, ln_mix_g, [sent_ff["token"]])
    dya, dyp, dgate = _mix_bwd(dz1b, w_out_full, h, ya, yp, aw)
    g_wout = _wgrad("wgrad_out", merged, dz1b, d).reshape(N_CHIPS, d // N_CHIPS, d)
    g_wba = _wgrad("wgrad_branch_attn", o_attn, dya, d // N_CHIPS)
    g_wbp = _wgrad("wgrad_branch_pool", y, dyp, d // N_CHIPS)
    do, delta, do4, do16, delta4, delta16 = _attn_out_bwd(dya, wba_st, o_attn)
    dpw, stp = _pool_out_bwd(dyp, wbp_st, pw, pool_scale)
    dwp, du = _pool_bwd(dpw, p, wp)
    g_wp = dwp.reshape(ng, N_CHIPS, rows_sh, pgw).transpose(1, 0, 2, 3).reshape(
        N_CHIPS, ng * rows_sh, pgw).astype(BF16)
    sent_mix = scatter_start("mix", [g_wp, g_wba, g_wbp, g_wout])

    bwd_in = {1: (do, lse, delta), 4: (do4, lse4, delta4), 16: (do16, lse16, delta16)}
    dqkv = {}
    for dil in DILATIONS:
        args = (*qkv[dil], *bwd_in[dil], offs[dil], aw, dil * aw)
        dqkv[dil] = _attn_bwd(f"attn_bwd_d{dil}", *args)
    dqkvu = _qkvu_grad(dqkv[1], dqkv[4], dqkv[16], du, cos_t, sin_t)
    g_win = _wgrad("wgrad_in_gates", xb, dgate, d, shard0=2, n_shards=N_CHIPS)
    g_win = _wgrad("wgrad_in_qkvu", xb, dqkvu, d, n_shards=N_CHIPS, fill=g_win)
    sent_in = scatter_start("in", [g_win])
    dx_a = _in_proj_bwd_x("in_proj_bwd_qkvu", dqkvu, w_in_st, 0, dz1, ALPHA,
                          [sent_mix["token"], sent_in["token"]])
    grad_x = _in_proj_bwd_x("in_proj_bwd_gates", dgate, w_in_st, 2, dx_a, 1.0)

    moments = {"w_in": (m_w_in, v_w_in), "w_pool": (m_w_pool, v_w_pool),
               "w_branch_attn": (m_w_branch_attn, v_w_branch_attn),
               "w_branch_pool": (m_w_branch_pool, v_w_branch_pool), "w_out": (m_w_out, v_w_out),
               "w_ff1": (m_w_ff1, v_w_ff1), "w_ff2": (m_w_ff2, v_w_ff2)}
    originals = {"w_in": w_in, "w_pool": w_pool, "w_branch_attn": w_branch_attn,
                 "w_branch_pool": w_branch_pool, "w_out": w_out, "w_ff1": w_ff1, "w_ff2": w_ff2}
    res = {}

    def summed(name, sent, keys, after):
        srcs, lands = _exchange_wait(f"scatter_wait_{name}", SCATTER, sent, list(range(len(keys))), after)
        parts = [_sum_slabs(f"sum_slabs_{k}", srcs[i], lands[i], me_chip) for i, k in enumerate(keys)]
        return _exchange_start(f"cores_start_{name}", SIBLING, parts, [a.shape for a in parts])

    def updated(name, swapping, keys, after):
        mine, other = _exchange_wait(f"cores_wait_{name}", SIBLING, swapping, list(range(len(keys))), after)
        for i, k in enumerate(keys):
            mk, vk = (a.reshape(big[k].shape) for a in moments[k])
            outs = _adamw(f"adamw_{k}", big[k], mk, vk, [mine[i], other[i]])
            res[k] = [o.reshape(originals[k].shape) for o in outs]

    groups = {"ff": ["w_ff1", "w_ff2"], "mix": ["w_pool", "w_branch_attn", "w_branch_pool", "w_out"],
              "in": ["w_in"]}
    swap_ff = summed("ff", sent_ff, groups["ff"], grad_x)
    swap_mix = summed("mix", sent_mix, groups["mix"], swap_ff["token"])
    swap_in = summed("in", sent_in, groups["in"], swap_mix["token"])
    updated("ff", swap_ff, groups["ff"], swap_in["token"])
    updated("mix", swap_mix, groups["mix"], res["w_ff2"][0])
    updated("in", swap_in, groups["in"], res["w_out"][0])
    tot2, tot1, totp = _allreduce_stats([st2, st1, stp], res["w_in"][0])

    def pad_d(a):
        return jnp.pad(a, ((0, 0), (0, d - a.shape[1])))

    small = ["ln_mix_g", "ln_mix_b", "ln_ff_g", "ln_ff_b", "pool_scale"]
    small_w = {"ln_mix_g": ln_mix_g, "ln_mix_b": ln_mix_b, "ln_ff_g": ln_ff_g, "ln_ff_b": ln_ff_b,
               "pool_scale": pool_scale}
    small_m = {"ln_mix_g": m_ln_mix_g, "ln_mix_b": m_ln_mix_b, "ln_ff_g": m_ln_ff_g,
               "ln_ff_b": m_ln_ff_b, "pool_scale": m_pool_scale}
    small_v = {"ln_mix_g": v_ln_mix_g, "ln_mix_b": v_ln_mix_b, "ln_ff_g": v_ln_ff_g,
               "ln_ff_b": v_ln_ff_b, "pool_scale": v_pool_scale}
    small_g = [tot1[0:1], tot1[1:2], tot2[0:1], tot2[1:2], pad_d(totp[0:1])]

    def pack(rows):
        return jnp.concatenate([pad_d(a) for a in rows] + [jnp.zeros((8 - len(rows), d), F32)], axis=0)

    outs = _adamw("adamw_small", pack([small_w[k] for k in small]), pack([small_m[k] for k in small]),
                  pack([small_v[k] for k in small]), [pack(small_g)])
    for i, k in enumerate(small):
        res[k] = [o[i:i + 1, :small_w[k].shape[1]] for o in outs]
    loss = tot2[2, 0]

    order = ["w_in", "w_pool", "pool_scale", "w_branch_attn", "w_branch_pool", "w_out", "ln_mix_g",
             "ln_mix_b", "w_ff1", "w_ff2", "ln_ff_g", "ln_ff_b"]
    result = [loss, grad_x[None]]
    for idx in range(4):
        result += [res[k][idx] for k in order]
    return tuple(result)
```

```python
import jax
import jax.numpy as jnp
from jax import lax
from jax.experimental import pallas as pl
from jax.experimental.pallas import tpu as pltpu

F32 = jnp.float32
BF16 = jnp.bfloat16
MESH = pl.DeviceIdType.MESH

HEAD_DIM = 128
SUB_BLOCK = 128
DILATIONS = (1, 4, 16)
POOL_WINDOWS = (2, 4, 8, 16)
POOL_HALO = 16
ROPE_THETA = 10000.0
LN_EPS = 1e-5
ALPHA = 2.0 ** 0.25
ADAM_LR, ADAM_B1, ADAM_B2, ADAM_EPS, ADAM_WD, ADAM_STEP = 0.001, 0.9, 0.999, 1e-08, 0.01, 10
NEG = -1e30
N_CHIPS = 4
VMEM_LIMIT = 56 * 1024 * 1024
EPILOGUE_ROWS = 128

def _params(sem=None, vmem=VMEM_LIMIT):
    kw = {"vmem_limit_bytes": vmem}
    if sem is not None:
        kw["dimension_semantics"] = sem
    return pltpu.CompilerParams(**kw)


def _dot(a, b, contract):
    return lax.dot_general(a, b, (contract, ((), ())), preferred_element_type=F32)


ANY_SPEC = pl.BlockSpec(memory_space=pl.ANY)
HBM_SPEC = pl.BlockSpec(memory_space=pltpu.HBM)
SEM_SPEC = pl.BlockSpec(memory_space=pltpu.SEMAPHORE)
DATAFLOW = pltpu.SideEffectType.DATAFLOW_SIDE_EFFECTING


def _hbm(a):
    return pltpu.with_memory_space_constraint(a, pltpu.HBM)


NN = ((1,), (0,))
NT = ((1,), (1,))
TN = ((0,), (0,))


def _mm(name, grid, a, a_spec, b, b_spec, contract, extras, extra_specs, out_shape, out_specs,
        epilogue, acc_shape, acc_as_ref=False, run_after=(), scratch=(),
        semantics=("parallel", "parallel", "arbitrary"), fill=None):
    nk = grid[2]
    n_ex = len(extras)
    n_in = 2 + n_ex + len(run_after) + (fill is not None)
    n_out = len(out_shape)
    n_scr = len(scratch)

    def body(*refs):
        a_ref, b_ref = refs[0], refs[1]
        ex = refs[2:2 + n_ex]
        outs = refs[n_in:n_in + n_out]
        scr = refs[n_in + n_out:n_in + n_out + n_scr]
        if nk == 1:
            epilogue(_dot(a_ref[...], b_ref[...], contract), ex, outs, *scr)
        else:
            acc = refs[n_in + n_out + n_scr]
            k = pl.program_id(2)

            @pl.when(k == 0)
            def _():
                acc[...] = jnp.zeros_like(acc)

            acc[...] += _dot(a_ref[...], b_ref[...], contract)

            @pl.when(k == nk - 1)
            def _():
                epilogue(acc if acc_as_ref else acc[...], ex, outs, *scr)

    acc_scratch = [pltpu.VMEM(acc_shape, F32)] if nk > 1 else []
    filled = [] if fill is None else [fill]
    return pl.pallas_call(
        body, name=name, grid=grid,
        in_specs=[a_spec, b_spec, *extra_specs, *[ANY_SPEC] * (len(run_after) + len(filled))],
        out_specs=out_specs, out_shape=out_shape, scratch_shapes=[*scratch, *acc_scratch],
        input_output_aliases={} if fill is None else {n_in - 1: 0},
        compiler_params=_params(semantics),
    )(a, b, *extras, *run_after, *filled)


def _stats_rows(rows, width):
    idx = lax.broadcasted_iota(jnp.int32, (8, width), 0)
    out = jnp.zeros((8, width), F32)
    for r, v in enumerate(rows):
        out = jnp.where(idx == r, jnp.broadcast_to(v, (8, width)), out)
    return out


def _layer_norm_fwd(z):
    mu = jnp.mean(z, axis=-1, keepdims=True)
    zc = z - mu
    var = jnp.mean(zc * zc, axis=-1, keepdims=True)
    rstd = lax.rsqrt(var + LN_EPS)
    return zc * rstd, rstd


def _layer_norm_bwd(dy, xhat, rstd, g):
    dxh = dy * g
    m1 = jnp.mean(dxh, axis=-1, keepdims=True)
    m2 = jnp.mean(dxh * xhat, axis=-1, keepdims=True)
    return rstd * (dxh - m1 - xhat * m2)


def _heads_scratch(rows, width):
    return pltpu.VMEM((width // HEAD_DIM, rows, HEAD_DIM), F32)


def _to_views(src_ref, view_refs, dtype, heads=None):
    nh, rows, _ = src_ref.shape
    width = nh * HEAD_DIM
    for dil, view_ref in zip(DILATIONS[1:], view_refs):
        for r in range(dil):
            for hh in (range(nh) if heads is None else heads):
                c0 = r * width + hh * HEAD_DIM
                view_ref[:, c0:c0 + HEAD_DIM] = (
                    src_ref[hh, pl.ds(r, rows // dil, stride=dil), :].astype(dtype))


def _from_views(view_refs, dst_refs):
    nh, rows, _ = dst_refs[0].shape
    width = nh * HEAD_DIM
    for dil, view_ref, dst_ref in zip(DILATIONS[1:], view_refs, dst_refs):
        for r in range(dil):
            for hh in range(nh):
                c0 = r * width + hh * HEAD_DIM
                dst_ref[hh, pl.ds(r, rows // dil, stride=dil), :] = (
                    view_ref[:, c0:c0 + HEAD_DIM].astype(F32))


def _view_shape(rows, width, dil, parts=1):
    return (rows // dil, parts * dil * width)


def _in_proj(xb, w_in_st, cos_t, sin_t, aw):
    s, d = xb.shape
    n_sh = w_in_st.shape[2]
    tm, tn = min(s, 1024), aw
    per = n_sh // tn
    grid = (s // tm, (N_CHIPS * n_sh) // tn, 1)

    def epilogue(acc, ex, outs, scr):
        cos_ref, sin_ref = ex
        h_ref, v4_ref, v16_ref = outs
        seg = pl.program_id(1)

        heads = [slice(hh * HEAD_DIM, (hh + 1) * HEAD_DIM) for hh in range(tn // HEAD_DIM)]

        @pl.when(seg < 2)
        def _():
            cos, sin = cos_ref[...], sin_ref[...]
            for hh, sl in enumerate(heads):
                t = acc[:, sl]
                scr[hh] = t * cos + pltpu.roll(t, HEAD_DIM // 2, 1) * sin

        @pl.when(seg == 2)
        def _():
            for hh, sl in enumerate(heads):
                scr[hh] = acc[:, sl]

        @pl.when(seg < 3)
        def _():
            for hh, sl in enumerate(heads):
                h_ref[:, sl] = scr[hh].astype(BF16)
            _to_views(scr, (v4_ref, v16_ref), BF16)

        @pl.when(seg == 3)
        def _():
            h_ref[...] = acc.astype(BF16)

        @pl.when(seg >= 4)
        def _():
            h_ref[...] = (0.5 * jnp.tanh(0.5 * acc) + 0.5).astype(BF16)

    def view_spec(dil):
        return pl.BlockSpec((tm // dil, dil * aw), lambda i, j, k: (i, jnp.minimum(j, 2)))

    return _mm(
        "in_proj", grid, xb, pl.BlockSpec((tm, d), lambda i, j, k: (i, 0)),
        w_in_st, pl.BlockSpec((None, d, tn), lambda i, j, k: (j // per, 0, j % per)), NN,
        [cos_t, sin_t], [pl.BlockSpec((tm, HEAD_DIM), lambda i, j, k: (i, 0))] * 2,
        [jax.ShapeDtypeStruct((s, N_CHIPS * n_sh), BF16)]
        + [jax.ShapeDtypeStruct(_view_shape(s, aw, dil, 3), BF16) for dil in DILATIONS[1:]],
        [pl.BlockSpec((tm, tn), lambda i, j, k: (i, j))] + [view_spec(dil) for dil in DILATIONS[1:]],
        epilogue, None, scratch=[_heads_scratch(tm, aw)],
        semantics=("parallel", "arbitrary", "arbitrary"))


def _pack_heads(cols):
    rows, rep = cols[0].shape[0], HEAD_DIM // len(cols)
    lane = lax.broadcasted_iota(jnp.int32, (rows, HEAD_DIM), 1)
    out = jnp.zeros((rows, HEAD_DIM), F32)
    for hh, col in enumerate(cols):
        out = jnp.where((lane >= hh * rep) & (lane < (hh + 1) * rep), col, out)
    return out


def _head_col(packed, hh, nh):
    lane = lax.broadcasted_iota(jnp.int32, packed.shape, 1)
    return jnp.sum(jnp.where(lane == hh * (HEAD_DIM // nh), packed, 0.0), axis=-1, keepdims=True)


def _band_masks(block_idx):
    qi = lax.broadcasted_iota(jnp.int32, (SUB_BLOCK, 2 * SUB_BLOCK), 0)
    kj = lax.broadcasted_iota(jnp.int32, (SUB_BLOCK, 2 * SUB_BLOCK), 1)
    first_key = jnp.where(block_idx > 0, 0, SUB_BLOCK)
    return (kj >= qi) & (kj <= qi + SUB_BLOCK) & (kj >= first_key)


def _attn_fwd(name, q, k, v, offs, cw, width):
    m = q.shape[0]
    nh = cw // HEAD_DIM
    grid = (width // cw, m // SUB_BLOCK)
    scale = HEAD_DIM ** -0.5

    def body(q_ref, k_ref, v_ref, o_ref, lse_ref, kprev, vprev):
        n = pl.program_id(1)
        valid = _band_masks(n)

        @pl.when(n == 0)
        def _():
            kprev[...] = jnp.zeros_like(kprev)
            vprev[...] = jnp.zeros_like(vprev)

        heads = [slice(hh * HEAD_DIM, (hh + 1) * HEAD_DIM) for hh in range(nh)]
        scs = [_dot(q_ref[:, sl], jnp.concatenate([kprev[:, sl], k_ref[:, sl]], axis=0), NT)
               for sl in heads]
        lses = []
        for hh, sl in enumerate(heads):
            vcat = jnp.concatenate([vprev[:, sl], v_ref[:, sl]], axis=0)
            sc = jnp.where(valid, scs[hh] * scale, NEG)
            mx = jnp.max(sc, axis=-1, keepdims=True)
            p = jnp.exp(sc - mx)
            l = jnp.sum(p, axis=-1, keepdims=True)
            o = _dot(p.astype(BF16), vcat, NN) / l
            o_ref[:, sl] = o.astype(BF16)
            lses.append(mx + jnp.log(l))
        lse_ref[...] = _pack_heads(lses)
        kprev[...] = k_ref[...]
        vprev[...] = v_ref[...]

    def cur(off):
        return pl.BlockSpec((SUB_BLOCK, cw), lambda j, n: (n, j + off))

    return pl.pallas_call(
        body, name=name, grid=grid,
        in_specs=[cur(offs[0]), cur(offs[1]), cur(offs[2])],
        out_specs=[cur(0), pl.BlockSpec((SUB_BLOCK, HEAD_DIM), lambda j, n: (n, j))],
        out_shape=[jax.ShapeDtypeStruct((m, width), BF16),
                   jax.ShapeDtypeStruct((m, width // cw * HEAD_DIM), F32)],
        scratch_shapes=[pltpu.VMEM((SUB_BLOCK, cw), BF16)] * 2,
        compiler_params=_params(("parallel", "arbitrary")),
    )(q, k, v)


def _view_spec(tm, aw, dil):
    return pl.BlockSpec((tm // dil, dil * aw), lambda i: (i, 0))


def _attn_combine(o1, l1, o_views, l_views):
    s, aw = o1.shape
    nh = aw // HEAD_DIM
    tm = min(s, 512)

    def body(o1_ref, l1_ref, o4_ref, o16_ref, l4_ref, l16_ref, o_ref, lse_ref, lse4_ref, lse16_ref,
             so4, so16, sl4, sl16, stot):
        _from_views((o4_ref, o16_ref), (so4, so16))
        _from_views((l4_ref, l16_ref), (sl4, sl16))
        a, b, c = l1_ref[...], sl4[0], sl16[0]
        mx = jnp.maximum(jnp.maximum(a, b), c)
        ea, eb, ec = jnp.exp(a - mx), jnp.exp(b - mx), jnp.exp(c - mx)
        tot = ea + eb + ec
        inv = 1.0 / tot
        wa, wb, wc = ea * inv, eb * inv, ec * inv
        lse_tot = mx + jnp.log(tot)
        stot[0] = lse_tot
        lse_ref[...] = lse_tot
        _to_views(stot, (lse4_ref, lse16_ref), F32)
        for hh in range(nh):
            sl = slice(hh * HEAD_DIM, (hh + 1) * HEAD_DIM)
            o = (_head_col(wa, hh, nh) * o1_ref[:, sl].astype(F32) + _head_col(wb, hh, nh) * so4[hh]
                 + _head_col(wc, hh, nh) * so16[hh])
            o_ref[:, sl] = o.astype(BF16)

    row = pl.BlockSpec((tm, aw), lambda i: (i, 0))
    stat = pl.BlockSpec((tm, HEAD_DIM), lambda i: (i, 0))
    views = [_view_spec(tm, aw, dil) for dil in DILATIONS[1:]]
    stat_views = [_view_spec(tm, HEAD_DIM, dil) for dil in DILATIONS[1:]]
    return pl.pallas_call(
        body, name="attn_combine", grid=(s // tm,), in_specs=[row, stat, *views, *stat_views],
        out_specs=[row, stat, *stat_views],
        out_shape=[jax.ShapeDtypeStruct((s, aw), BF16), jax.ShapeDtypeStruct((s, HEAD_DIM), F32)]
        + [jax.ShapeDtypeStruct(_view_shape(s, HEAD_DIM, dil), F32) for dil in DILATIONS[1:]],
        scratch_shapes=[_heads_scratch(tm, aw)] * 2 + [_heads_scratch(tm, HEAD_DIM)] * 3,
        compiler_params=_params(("parallel",)),
    )(o1, l1, *o_views, *l_views)


def _pool_counts(tm, rows, pgw, row0):
    t = lax.broadcasted_iota(jnp.int32, (rows, len(POOL_WINDOWS) * pgw), 0) + row0
    col = lax.broadcasted_iota(jnp.int32, (rows, len(POOL_WINDOWS) * pgw), 1)
    w = jnp.full((rows, len(POOL_WINDOWS) * pgw), POOL_WINDOWS[0], jnp.int32)
    for g in range(1, len(POOL_WINDOWS)):
        w = jnp.where(col >= g * pgw, POOL_WINDOWS[g], w)
    return jnp.minimum(t + 1, w).astype(F32)


def _window_sums(xs, direction, pgw):
    rows = xs.shape[0]
    acc = xs
    out = None
    col = lax.broadcasted_iota(jnp.int32, xs.shape, 1)
    for g, w in enumerate(POOL_WINDOWS):
        sh = w // 2
        acc = acc + pltpu.roll(acc, sh if direction > 0 else rows - sh, 0)
        out = acc if out is None else jnp.where(col >= g * pgw, acc, out)
    return out


def _pool_fwd(h, wp, scale, aw):
    s = h.shape[0]
    pw_ = aw
    pgw = pw_ // len(POOL_WINDOWS)
    tm = min(s, 512)
    hb = tm // POOL_HALO

    def body(u_ref, halo_ref, wp_ref, sc_ref, p_ref, pw_ref, y_ref):
        i = pl.program_id(0)
        u = u_ref[...].astype(F32)
        halo = halo_ref[...].astype(F32) * jnp.where(i > 0, 1.0, 0.0)
        xs = jnp.concatenate([halo, u], axis=0)
        sums = _window_sums(xs, +1, pgw)[POOL_HALO:]
        p = (sums / _pool_counts(tm, tm, pgw, i * tm) - u).astype(BF16)
        p_ref[...] = p
        sc = sc_ref[...]
        for g in range(len(POOL_WINDOWS)):
            sl = slice(g * pgw, (g + 1) * pgw)
            pw = _dot(p[:, sl], wp_ref[g], NN)
            pw_ref[:, sl] = pw.astype(BF16)
            y_ref[:, sl] = (pw * sc[:, sl]).astype(BF16)

    out = jax.ShapeDtypeStruct((s, pw_), BF16)
    row = pl.BlockSpec((tm, pw_), lambda i: (i, 0))
    return pl.pallas_call(
        body, name="pool_fwd", grid=(s // tm,),
        in_specs=[pl.BlockSpec((tm, pw_), lambda i: (i, 3)),
                  pl.BlockSpec((POOL_HALO, pw_), lambda i: (jnp.maximum(i * hb - 1, 0), 3)),
                  pl.BlockSpec(wp.shape, lambda i: (0, 0, 0)),
                  pl.BlockSpec((1, pw_), lambda i: (0, 0))],
        out_specs=[row, row, row], out_shape=[out, out, out],
        compiler_params=_params(("parallel",)),
    )(h, h, wp, scale)


def _branch_merge(o_attn, y, wba_st, wbp_st, h, aw, d):
    s = o_attn.shape[0]
    tn = wba_st.shape[2]
    tm = min(s, 1024)
    ga0 = 4 * aw // tn
    gp0 = (4 * aw + d) // tn

    def body(o_ref, y_ref, wa_ref, wp_ref, sga_ref, sgp_ref, ya_ref, yp_ref, mg_ref):
        ya = _dot(o_ref[...], wa_ref[...], NN)
        yp = _dot(y_ref[...], wp_ref[...], NN)
        ya_ref[...] = ya.astype(BF16)
        yp_ref[...] = yp.astype(BF16)
        mg_ref[...] = (sga_ref[...].astype(F32) * ya + sgp_ref[...].astype(F32) * yp).astype(BF16)

    out = jax.ShapeDtypeStruct((s, d), BF16)
    blk = pl.BlockSpec((tm, tn), lambda i, j: (i, j))
    return pl.pallas_call(
        body, name="branch_merge", grid=(s // tm, N_CHIPS),
        in_specs=[pl.BlockSpec((tm, aw), lambda i, j: (i, 0)),
                  pl.BlockSpec((tm, aw), lambda i, j: (i, 0)),
                  pl.BlockSpec((None, aw, tn), lambda i, j: (j, 0, 0)),
                  pl.BlockSpec((None, aw, tn), lambda i, j: (j, 0, 0)),
                  pl.BlockSpec((tm, tn), lambda i, j: (i, j + ga0)),
                  pl.BlockSpec((tm, tn), lambda i, j: (i, j + gp0))],
        out_specs=[blk, blk, blk], out_shape=[out, out, out],
        compiler_params=_params(("parallel", "parallel")),
    )(o_attn, y, wba_st, wbp_st, h, h)


def _mix_norm(merged, w_out, x, g1, b1):
    s, d = x.shape
    tm = min(s, 256)

    def epilogue(acc, ex, outs):
        x_ref, g_ref, b_ref = ex
        xh_ref, rs_ref, xb_ref = outs
        xhat, rstd = _layer_norm_fwd(ALPHA * x_ref[...] + acc)
        xh_ref[...] = xhat
        rs_ref[...] = rstd
        xb_ref[...] = (xhat * g_ref[...] + b_ref[...]).astype(BF16)

    row = pl.BlockSpec((tm, d), lambda i, j, k: (i, 0))
    vec = pl.BlockSpec((1, d), lambda i, j, k: (0, 0))
    return _mm(
        "mix_norm", (s // tm, 1, 1), merged, row, w_out, pl.BlockSpec((d, d), lambda i, j, k: (0, 0)),
        NN, [x, g1, b1], [row, vec, vec],
        [jax.ShapeDtypeStruct((s, d), F32), jax.ShapeDtypeStruct((s, 1), F32),
         jax.ShapeDtypeStruct((s, d), BF16)],
        [row, pl.BlockSpec((tm, 1), lambda i, j, k: (i, 0)), row], epilogue, None)


def _ff_up(x1b, w1_st):
    s, d = x1b.shape
    n_sh = w1_st.shape[2]
    tm, tn = min(s, 1024), min(n_sh, 1024)
    per = n_sh // tn

    def epilogue(acc, ex, outs):
        r = jnp.maximum(acc, 0.0)
        outs[0][...] = (r * r).astype(BF16)
        outs[1][...] = (2.0 * r).astype(BF16)

    blk = pl.BlockSpec((tm, tn), lambda i, j, k: (i, j))
    out = jax.ShapeDtypeStruct((s, N_CHIPS * n_sh), BF16)
    return _mm(
        "ff_up", (s // tm, N_CHIPS * per, 1), x1b, pl.BlockSpec((tm, d), lambda i, j, k: (i, 0)),
        w1_st, pl.BlockSpec((None, d, tn), lambda i, j, k: (j // per, 0, j % per)), NN, [], [],
        [out, out], [blk, blk], epilogue, None)


def _ff_down_loss(r, w2, xhat1, g1, b1, g2, b2, target):
    s, d = xhat1.shape
    dff = r.shape[1]
    tm, tk = min(s, 512), min(dff, 1024)
    ch = min(tm, EPILOGUE_ROWS)

    def epilogue(acc_ref, ex, outs):
        xh1_ref, g1_ref, b1_ref, g2_ref, b2_ref, t_ref = ex
        dz_ref, dzb_ref, st_ref = outs
        g1v, b1v, g2v, b2v = g1_ref[...], b1_ref[...], g2_ref[...], b2_ref[...]
        dg = db = loss = None
        for c in range(tm // ch):
            rows = slice(c * ch, (c + 1) * ch)
            x1 = xh1_ref[rows, :] * g1v + b1v
            xhat2, rstd2 = _layer_norm_fwd(ALPHA * x1 + acc_ref[rows, :])
            err = xhat2 * g2v + b2v - t_ref[rows, :]
            dy = err * (1.0 / d)
            dz = _layer_norm_bwd(dy, xhat2, rstd2, g2v)
            dz_ref[rows, :] = dz
            dzb_ref[rows, :] = dz.astype(BF16)
            parts = (jnp.sum(dy * xhat2, axis=0, keepdims=True), jnp.sum(dy, axis=0, keepdims=True),
                     jnp.sum(jnp.sum(err * err, axis=-1, keepdims=True), axis=0, keepdims=True))
            dg, db, loss = parts if c == 0 else (dg + parts[0], db + parts[1], loss + parts[2])
        st_ref[...] = _stats_rows([dg, db, jnp.broadcast_to((0.5 / d) * loss, (1, d))], d)

    row = pl.BlockSpec((tm, d), lambda i, j, k: (i, 0))
    vec = pl.BlockSpec((1, d), lambda i, j, k: (0, 0))
    return _mm(
        "ff_down_loss", (s // tm, 1, dff // tk), r, pl.BlockSpec((tm, tk), lambda i, j, k: (i, k)),
        w2, pl.BlockSpec((tk, d), lambda i, j, k: (k, 0)), NN,
        [xhat1, g1, b1, g2, b2, target], [row, vec, vec, vec, vec, row],
        [jax.ShapeDtypeStruct((s, d), F32), jax.ShapeDtypeStruct((s, d), BF16),
         jax.ShapeDtypeStruct((s // tm, 8, d), F32)],
        [row, row, pl.BlockSpec((None, 8, d), lambda i, j, k: (i, 0, 0))], epilogue, (tm, d),
        acc_as_ref=True)


def _ff_down_bwd(dz2b, w2, r_slope):
    s, d = dz2b.shape
    dff = r_slope.shape[1]
    tm, tn = min(s, 1024), min(dff, 1024)

    def epilogue(acc, ex, outs):
        outs[0][...] = (acc * ex[0][...].astype(F32)).astype(BF16)

    blk = pl.BlockSpec((tm, tn), lambda i, j, k: (i, j))
    return _mm(
        "ff_down_bwd", (s // tm, dff // tn, 1), dz2b, pl.BlockSpec((tm, d), lambda i, j, k: (i, 0)),
        w2, pl.BlockSpec((tn, d), lambda i, j, k: (j, 0)), NT, [r_slope], [blk],
        [jax.ShapeDtypeStruct((s, dff), BF16)], [blk], epilogue, None)[0]


def _wgrad(name, a, g, n_sh, shard0=0, n_shards=None, fill=None):
    s, rows = a.shape
    cols = g.shape[1]
    tm, tn, tk = min(rows, 2048), min(cols, 1024), min(s, 2048)
    if tn >= n_sh:
        span = tn // n_sh
        out_spec = pl.BlockSpec((span, tm, n_sh), lambda i, j, k: (j + shard0 // span, i, 0))

        def epilogue(acc_ref, ex, outs):
            for sh in range(span):
                outs[0][sh] = acc_ref[:, sh * n_sh:(sh + 1) * n_sh].astype(BF16)
    else:
        per = n_sh // tn
        out_spec = pl.BlockSpec((None, tm, tn), lambda i, j, k: (shard0 + j // per, i, j % per))

        def epilogue(acc_ref, ex, outs):
            outs[0][...] = acc_ref[...].astype(BF16)

    return _mm(
        name, (rows // tm, cols // tn, s // tk), a, pl.BlockSpec((tk, tm), lambda i, j, k: (k, i)),
        g, pl.BlockSpec((tk, tn), lambda i, j, k: (k, j)), TN, [], [],
        [jax.ShapeDtypeStruct((n_shards or cols // n_sh, rows, n_sh), BF16)], [out_spec], epilogue,
        (tm, tn), acc_as_ref=True, fill=fill)[0]


def _ff_up_bwd(da, w1_st, dz2, xhat1, rstd1, g1, run_after):
    s, d = dz2.shape
    n_sh = w1_st.shape[2]
    tm, tk = min(s, 512), min(n_sh, 1024)
    per = n_sh // tk
    ch = min(tm, EPILOGUE_ROWS)

    def epilogue(acc_ref, ex, outs):
        dz2_ref, xh_ref, rs_ref, g_ref = ex
        dz_ref, dzb_ref, st_ref = outs
        gv = g_ref[...]
        dg = db = None
        for c in range(tm // ch):
            rows = slice(c * ch, (c + 1) * ch)
            dx1 = ALPHA * dz2_ref[rows, :] + acc_ref[rows, :]
            xhat = xh_ref[rows, :]
            dz = _layer_norm_bwd(dx1, xhat, rs_ref[rows, :], gv)
            dz_ref[rows, :] = dz
            dzb_ref[rows, :] = dz.astype(BF16)
            parts = (jnp.sum(dx1 * xhat, axis=0, keepdims=True), jnp.sum(dx1, axis=0, keepdims=True))
            dg, db = parts if c == 0 else (dg + parts[0], db + parts[1])
        st_ref[...] = _stats_rows([dg, db], d)

    row = pl.BlockSpec((tm, d), lambda i, j, k: (i, 0))
    return _mm(
        "ff_up_bwd", (s // tm, 1, N_CHIPS * per), da, pl.BlockSpec((tm, tk), lambda i, j, k: (i, k)),
        w1_st, pl.BlockSpec((None, d, tk), lambda i, j, k: (k // per, 0, k % per)), NT,
        [dz2, xhat1, rstd1, g1],
        [row, row, pl.BlockSpec((tm, 1), lambda i, j, k: (i, 0)), pl.BlockSpec((1, d), lambda i, j, k: (0, 0))],
        [jax.ShapeDtypeStruct((s, d), F32), jax.ShapeDtypeStruct((s, d), BF16),
         jax.ShapeDtypeStruct((s // tm, 8, d), F32)],
        [row, row, pl.BlockSpec((None, 8, d), lambda i, j, k: (i, 0, 0))], epilogue, (tm, d),
        acc_as_ref=True, run_after=run_after)


def _mix_bwd(dz1b, w_out, h, ya, yp, aw):
    s, d = dz1b.shape
    tm = min(s, 256)
    gblk = 4 * aw // d

    def epilogue(acc, ex, outs):
        sga_ref, sgp_ref, ya_ref, yp_ref = ex
        dya_ref, dyp_ref, dg_ref = outs
        sga, sgp = sga_ref[...].astype(F32), sgp_ref[...].astype(F32)
        dya_ref[...] = (acc * sga).astype(BF16)
        dyp_ref[...] = (acc * sgp).astype(BF16)
        dg_ref[:, :d] = (acc * ya_ref[...].astype(F32) * (sga * (1.0 - sga))).astype(BF16)
        dg_ref[:, d:] = (acc * yp_ref[...].astype(F32) * (sgp * (1.0 - sgp))).astype(BF16)

    row = pl.BlockSpec((tm, d), lambda i, j, k: (i, 0))
    return _mm(
        "mix_bwd", (s // tm, 1, 1), dz1b, row, w_out, pl.BlockSpec((d, d), lambda i, j, k: (0, 0)), NT,
        [h, h, ya, yp],
        [pl.BlockSpec((tm, d), lambda i, j, k: (i, gblk)),
         pl.BlockSpec((tm, d), lambda i, j, k: (i, gblk + 1)), row, row],
        [jax.ShapeDtypeStruct((s, d), BF16), jax.ShapeDtypeStruct((s, d), BF16),
         jax.ShapeDtypeStruct((s, 2 * d), BF16)],
        [row, row, pl.BlockSpec((tm, 2 * d), lambda i, j, k: (i, 0))], epilogue, None)


def _branch_in_bwd(name, tm, dyb, wb_st, epilogue, extras, extra_specs, out_shape, out_specs,
                   scratch=()):
    s, d = dyb.shape
    aw = wb_st.shape[1]
    wb_t = wb_st.transpose(0, 2, 1).reshape(d, aw)
    return _mm(
        name, (s // tm, 1, 1), dyb, pl.BlockSpec((tm, d), lambda i, j, k: (i, 0)),
        wb_t, pl.BlockSpec((d, aw), lambda i, j, k: (0, 0)), NN,
        extras, extra_specs, out_shape, out_specs, epilogue, None, scratch=scratch)


def _attn_out_bwd(dya, wba_st, o_attn):
    s, aw = o_attn.shape
    tm = min(s, 512)

    def epilogue(acc_ref, ex, outs, sdo, sdl):
        do_ref, dl_ref, do4_ref, do16_ref, dl4_ref, dl16_ref = outs
        deltas = []
        for hh in range(aw // HEAD_DIM):
            sl = slice(hh * HEAD_DIM, (hh + 1) * HEAD_DIM)
            do = acc_ref[:, sl]
            deltas.append(jnp.sum(do * ex[0][:, sl].astype(F32), axis=-1, keepdims=True))
            sdo[hh] = do
            do_ref[:, sl] = do.astype(BF16)
        packed = _pack_heads(deltas)
        sdl[0] = packed
        dl_ref[...] = packed
        _to_views(sdo, (do4_ref, do16_ref), BF16)
        _to_views(sdl, (dl4_ref, dl16_ref), F32)

    def specs(width):
        return ([pl.BlockSpec((tm, width), lambda i, j, k: (i, 0))]
                + [pl.BlockSpec((tm // dil, dil * width), lambda i, j, k: (i, 0)) for dil in DILATIONS[1:]])

    def shapes(width, dtype):
        return ([jax.ShapeDtypeStruct((s, width), dtype)]
                + [jax.ShapeDtypeStruct(_view_shape(s, width, dil), dtype) for dil in DILATIONS[1:]])

    do_specs, dl_specs = specs(aw), specs(HEAD_DIM)
    do_shapes, dl_shapes = shapes(aw, BF16), shapes(HEAD_DIM, F32)
    return _branch_in_bwd(
        "attn_out_bwd", tm, dya, wba_st, epilogue, [o_attn], [do_specs[0]],
        [do_shapes[0], dl_shapes[0], *do_shapes[1:], *dl_shapes[1:]],
        [do_specs[0], dl_specs[0], *do_specs[1:], *dl_specs[1:]],
        scratch=[_heads_scratch(tm, aw), _heads_scratch(tm, HEAD_DIM)])


def _pool_out_bwd(dyp, wbp_st, pw, scale):
    s, pw_ = pw.shape
    tm = min(s, 1024)

    def epilogue(acc_ref, ex, outs):
        pw_ref, sc_ref = ex
        dpw_ref, st_ref = outs
        acc = acc_ref[...]
        dpw_ref[...] = (acc * sc_ref[...]).astype(BF16)
        st_ref[...] = _stats_rows([jnp.sum(acc * pw_ref[...].astype(F32), axis=0, keepdims=True)], pw_)

    row = pl.BlockSpec((tm, pw_), lambda i, j, k: (i, 0))
    return _branch_in_bwd(
        "pool_out_bwd", tm, dyp, wbp_st, epilogue, [pw, scale],
        [row, pl.BlockSpec((1, pw_), lambda i, j, k: (0, 0))],
        [jax.ShapeDtypeStruct((s, pw_), BF16), jax.ShapeDtypeStruct((s // tm, 8, pw_), F32)],
        [row, pl.BlockSpec((None, 8, pw_), lambda i, j, k: (i, 0, 0))])


def _pool_bwd(dpw, p, wp):
    s, pw_ = p.shape
    ng = len(POOL_WINDOWS)
    pgw = pw_ // ng
    tm = min(s, 512)
    hb = tm // POOL_HALO
    nblk = s // tm

    def body(dpw_ref, nxt_ref, p_ref, wp_ref, dwp_ref, du_ref):
        i = pl.program_id(0)
        nxt = (nxt_ref[...].astype(F32) * jnp.where(i < nblk - 1, 1.0, 0.0)).astype(BF16)
        dpw_all = jnp.concatenate([dpw_ref[...], nxt], axis=0)

        @pl.when(i == 0)
        def _():
            dwp_ref[...] = jnp.zeros_like(dwp_ref)

        dps = []
        for g in range(ng):
            sl = slice(g * pgw, (g + 1) * pgw)
            dwp_ref[g] += _dot(p_ref[:, sl], dpw_ref[:, sl], TN)
            dps.append(_dot(dpw_all[:, sl], wp_ref[g], NT))
        dp = jnp.concatenate(dps, axis=1)
        dpn = dp / _pool_counts(tm, tm + POOL_HALO, pgw, i * tm)
        du_ref[...] = (_window_sums(dpn, -1, pgw)[:tm] - dp[:tm]).astype(BF16)

    row = pl.BlockSpec((tm, pw_), lambda i: (i, 0))
    full = pl.BlockSpec((ng, pgw, pgw), lambda i: (0, 0, 0))
    return pl.pallas_call(
        body, name="pool_bwd", grid=(nblk,),
        in_specs=[row, pl.BlockSpec((POOL_HALO, pw_), lambda i: (jnp.minimum((i + 1) * hb, s // POOL_HALO - 1), 0)),
                  row, full],
        out_specs=[full, row],
        out_shape=[jax.ShapeDtypeStruct((ng, pgw, pgw), F32), jax.ShapeDtypeStruct((s, pw_), BF16)],
        compiler_params=_params(("arbitrary",)),
    )(dpw, dpw, p, wp)


def _attn_bwd(name, q, k, v, do, lse, delta, offs, cw, width):
    m = do.shape[0]
    nh = cw // HEAD_DIM
    nblk = m // SUB_BLOCK
    grid = (width // cw, nblk + 1)
    scale = HEAD_DIM ** -0.5

    def body(q_ref, k_ref, v_ref, do_ref, lse_ref, dl_ref, dq_ref, dk_ref, dv_ref,
             kprev, vprev, dk_carry, dv_carry):
        n = pl.program_id(1)

        @pl.when(n == 0)
        def _():
            for ref in (kprev, vprev, dk_carry, dv_carry):
                ref[...] = jnp.zeros_like(ref)

        qi = lax.broadcasted_iota(jnp.int32, (SUB_BLOCK, 2 * SUB_BLOCK), 0)
        kj = lax.broadcasted_iota(jnp.int32, (SUB_BLOCK, 2 * SUB_BLOCK), 1)
        first_key = jnp.where(n == 0, SUB_BLOCK, jnp.where(n == nblk, 4 * SUB_BLOCK, 0))
        valid = (kj >= qi) & (kj <= qi + SUB_BLOCK) & (kj >= first_key)
        heads = [slice(hh * HEAD_DIM, (hh + 1) * HEAD_DIM) for hh in range(nh)]
        kcats = [jnp.concatenate([kprev[:, sl], k_ref[:, sl]], axis=0) for sl in heads]
        scs = [_dot(q_ref[:, sl], kcats[hh], NT) for hh, sl in enumerate(heads)]
        dps = [_dot(do_ref[:, sl], jnp.concatenate([vprev[:, sl], v_ref[:, sl]], axis=0), NT)
               for sl in heads]
        dqs = []
        lse_all, dl_all = lse_ref[...], dl_ref[...]
        for hh, sl in enumerate(heads):
            lse_h, dl_h = _head_col(lse_all, hh, nh), _head_col(dl_all, hh, nh)
            p = jnp.where(valid, jnp.exp(jnp.where(valid, scs[hh] * scale, NEG) - lse_h), 0.0)
            ds = (p * (dps[hh] - dl_h)).astype(BF16)
            dqs.append((_dot(ds, kcats[hh], NN) * scale).astype(BF16))
            dk2 = _dot(ds, q_ref[:, sl], TN) * scale
            dv2 = _dot(p.astype(BF16), do_ref[:, sl], TN)
            dk_ref[:, sl] = (dk_carry[:, sl] + dk2[:SUB_BLOCK]).astype(BF16)
            dv_ref[:, sl] = (dv_carry[:, sl] + dv2[:SUB_BLOCK]).astype(BF16)
            dk_carry[:, sl] = dk2[SUB_BLOCK:]
            dv_carry[:, sl] = dv2[SUB_BLOCK:]

        @pl.when(n < nblk)
        def _():
            for hh, sl in enumerate(heads):
                dq_ref[:, sl] = dqs[hh]

        kprev[...] = k_ref[...]
        vprev[...] = v_ref[...]

    def cur(off):
        return pl.BlockSpec((SUB_BLOCK, cw), lambda j, n: (jnp.minimum(n, nblk - 1), j + off))

    lagged = pl.BlockSpec((SUB_BLOCK, cw), lambda j, n: (jnp.maximum(n - 1, 0), j))
    stat = pl.BlockSpec((SUB_BLOCK, HEAD_DIM), lambda j, n: (jnp.minimum(n, nblk - 1), j))
    out = jax.ShapeDtypeStruct((m, width), BF16)
    return pl.pallas_call(
        body, name=name, grid=grid,
        in_specs=[cur(offs[0]), cur(offs[1]), cur(offs[2]), cur(0), stat, stat],
        out_specs=[cur(0), lagged, lagged], out_shape=[out, out, out],
        scratch_shapes=[pltpu.VMEM((SUB_BLOCK, cw), BF16)] * 2 + [pltpu.VMEM((SUB_BLOCK, cw), F32)] * 2,
        compiler_params=_params(("parallel", "arbitrary")),
    )(q, k, v, do, lse, delta)


def _qkvu_grad(d1, d4, d16, du, cos_t, sin_t):
    s, aw = du.shape
    tm = min(s, 512)

    def body(*refs):
        nat, v4, v16 = refs[0:3], refs[3:6], refs[6:9]
        cos_ref, sin_ref, du_ref, out_ref, s4, s16 = refs[9:]
        cos, sin = cos_ref[...], sin_ref[...]
        for part in range(3):
            _from_views((v4[part], v16[part]), (s4, s16))
            for hh in range(aw // HEAD_DIM):
                sl = slice(hh * HEAD_DIM, (hh + 1) * HEAD_DIM)
                t = nat[part][:, sl].astype(F32) + s4[hh] + s16[hh]
                if part < 2:
                    t = t * cos - pltpu.roll(t, HEAD_DIM // 2, 1) * sin
                out_ref[:, part * aw + hh * HEAD_DIM:part * aw + (hh + 1) * HEAD_DIM] = t.astype(BF16)
        out_ref[:, 3 * aw:] = du_ref[...]

    row = pl.BlockSpec((tm, aw), lambda i: (i, 0))
    tab = pl.BlockSpec((tm, HEAD_DIM), lambda i: (i, 0))
    return pl.pallas_call(
        body, name="qkvu_grad", grid=(s // tm,),
        in_specs=[row] * 3 + [_view_spec(tm, aw, 4)] * 3 + [_view_spec(tm, aw, 16)] * 3 + [tab, tab, row],
        out_specs=pl.BlockSpec((tm, 4 * aw), lambda i: (i, 0)),
        out_shape=jax.ShapeDtypeStruct((s, 4 * aw), BF16),
        scratch_shapes=[_heads_scratch(tm, aw)] * 2,
        compiler_params=_params(("parallel",)),
    )(*d1, *d4, *d16, cos_t, sin_t, du)


def _in_proj_bwd_x(name, dh, w_in_st, shard0, base, scale_base, run_after=()):
    s, kdim = dh.shape
    d, n_sh = w_in_st.shape[1], w_in_st.shape[2]
    tm, tk = min(s, 512), min(n_sh, 2048)
    per = n_sh // tk

    ch = min(tm, 2 * EPILOGUE_ROWS)

    def epilogue(acc_ref, ex, outs):
        for c in range(tm // ch):
            rows = slice(c * ch, (c + 1) * ch)
            outs[0][rows, :] = scale_base * ex[0][rows, :] + acc_ref[rows, :]

    row = pl.BlockSpec((tm, d), lambda i, j, k: (i, 0))
    return _mm(
        name, (s // tm, 1, kdim // tk), dh, pl.BlockSpec((tm, tk), lambda i, j, k: (i, k)),
        w_in_st, pl.BlockSpec((None, d, tk), lambda i, j, k: (shard0 + k // per, 0, k % per)), NT,
        [base], [row], [jax.ShapeDtypeStruct((s, d), F32)], [row], epilogue, (tm, d),
        acc_as_ref=True, run_after=run_after)[0]


def _chip_peers():
    x, y, c = lax.axis_index("x"), lax.axis_index("y"), lax.axis_index("c")
    return x, y, c, [(1 - x, y), (x, 1 - y), (1 - x, 1 - y)]


GATHER, GATHER_HALF, SCATTER, SIBLING = "gather", "gather_half", "scatter", "sibling"


def _exchange_peers(mode):
    x, y, c, chips = _chip_peers()
    if mode == SIBLING:
        return x, y, c, [(x, y, 1 - c)]
    return x, y, c, [(px, py, c) for px, py in chips]


def _core_half(ref_or_shape, c):
    rows = (ref_or_shape.shape[0]) // 2
    return pl.ds(c * rows, rows)


def _exchange_descriptor(mode, src, land, send, recv, p, peer, me, arriving):
    pid = 2 * peer[0] + peer[1]
    if mode == GATHER:
        src_ref, dst_ref = src, land.at[pid if arriving else me]
    elif mode == GATHER_HALF:
        rows = _core_half(src, peer[2])
        src_ref, dst_ref = src.at[rows], land.at[pid if arriving else me, rows]
    elif mode == SCATTER:
        src_ref, dst_ref = src.at[pid], land.at[p]
    else:
        src_ref, dst_ref = src, land
    return pltpu.make_async_remote_copy(
        src_ref=src_ref, dst_ref=dst_ref, send_sem=send.at[p], recv_sem=recv.at[p],
        device_id=peer, device_id_type=MESH)


def _exchange_start(name, mode, srcs, land_shapes):
    n = len(srcs)
    lands = [_hbm(lax.empty(shape, src.dtype)) for shape, src in zip(land_shapes, srcs)]

    def body(*refs):
        src_refs, land_refs = refs[:n], refs[n:2 * n]
        sends, recvs = refs[2 * n:3 * n], refs[3 * n:4 * n]
        token = refs[6 * n]
        x, y, c, peers = _exchange_peers(mode)
        me = 2 * x + y
        for w in range(n):
            for p, peer in enumerate(peers):
                _exchange_descriptor(mode, src_refs[w], land_refs[w], sends[w], recvs[w], p, peer,
                                     me, arriving=False).start()
        token[...] = jnp.zeros_like(token)

    sem = pltpu.SemaphoreType.DMA((3,))
    outs = pl.pallas_call(
        body, name=name, in_specs=[HBM_SPEC] * (2 * n),
        out_specs=[SEM_SPEC] * (2 * n) + [HBM_SPEC] * (2 * n) + [pl.BlockSpec(memory_space=pltpu.VMEM)],
        out_shape=[sem] * (2 * n) + [pltpu.HBM(a.shape, a.dtype) for a in (*srcs, *lands)]
        + [jax.ShapeDtypeStruct((8, 128), F32)],
        input_output_aliases={i: 2 * n + i for i in range(2 * n)},
        compiler_params=pltpu.CompilerParams(has_side_effects=DATAFLOW),
    )(*[_hbm(a) for a in srcs], *lands)
    return {"send": outs[:n], "recv": outs[n:2 * n], "src": outs[2 * n:3 * n],
            "land": outs[3 * n:4 * n], "token": outs[4 * n]}


def _exchange_wait(name, mode, started, which, after):
    m = len(which)

    def body(*refs):
        src_refs, land_refs = refs[:m], refs[m:2 * m]
        sends, recvs = refs[2 * m:3 * m], refs[3 * m:4 * m]
        x, y, c, peers = _exchange_peers(mode)
        me = 2 * x + y
        for w in range(m):
            for p, peer in enumerate(peers):
                _exchange_descriptor(mode, src_refs[w], land_refs[w], sends[w], recvs[w], p, peer,
                                     me, arriving=False).wait_send()
                _exchange_descriptor(mode, src_refs[w], land_refs[w], sends[w], recvs[w], p, peer,
                                     me, arriving=True).wait_recv()

    pick = lambda key: [started[key][w] for w in which]
    bufs = pick("src") + pick("land")
    after = list(after) if isinstance(after, (list, tuple)) else [after]
    outs = pl.pallas_call(
        body, name=name,
        in_specs=[HBM_SPEC] * (2 * m) + [SEM_SPEC] * (2 * m) + [ANY_SPEC] * len(after),
        out_specs=[HBM_SPEC] * (2 * m), out_shape=[pltpu.HBM(a.shape, a.dtype) for a in bufs],
        input_output_aliases={i: i for i in range(2 * m)},
        compiler_params=pltpu.CompilerParams(has_side_effects=DATAFLOW),
    )(*bufs, *pick("send"), *pick("recv"), *after)
    return outs[:m], outs[m:]


def _to_bf16(name, a, run_after):
    r, c = a.shape
    tm = min(r, 512)

    def body(a_ref, after_ref, out_ref):
        out_ref[...] = a_ref[...].astype(BF16)

    blk = pl.BlockSpec((tm, c), lambda i: (i, 0))
    return pl.pallas_call(
        body, name=name, grid=(r // tm,), in_specs=[blk, ANY_SPEC], out_specs=blk,
        out_shape=jax.ShapeDtypeStruct((r, c), BF16), compiler_params=_params(("parallel",)),
    )(a, run_after)


def _swap_halves(name, lands):
    n = len(lands)

    def body(*refs):
        bufs = refs[n:2 * n]
        send, recv = refs[2 * n:]
        x, y, c, chips = _chip_peers()
        started = []
        for w in range(n):
            half = bufs[w].shape[1] // 2
            for p, (px, py) in enumerate(chips):
                mine = bufs[w].at[2 * px + py, pl.ds(c * half, half)]
                cp = pltpu.make_async_remote_copy(
                    src_ref=mine, dst_ref=mine, send_sem=send.at[w, p], recv_sem=recv.at[w, p],
                    device_id=(x, y, 1 - c), device_id_type=MESH)
                cp.start()
                started.append(cp)
        for w in range(n):
            half = bufs[w].shape[1] // 2
            for p, (px, py) in enumerate(chips):
                theirs = bufs[w].at[2 * px + py, pl.ds((1 - c) * half, half)]
                pltpu.make_async_remote_copy(
                    src_ref=theirs, dst_ref=theirs, send_sem=send.at[w, p], recv_sem=recv.at[w, p],
                    device_id=(x, y, 1 - c), device_id_type=MESH).wait_recv()
        for cp in started:
            cp.wait_send()

    return pl.pallas_call(
        body, name=name, in_specs=[ANY_SPEC] * n, out_specs=[ANY_SPEC] * n,
        out_shape=[jax.ShapeDtypeStruct(a.shape, a.dtype) for a in lands],
        input_output_aliases={i: i for i in range(n)},
        scratch_shapes=[pltpu.SemaphoreType.DMA((n, 3)), pltpu.SemaphoreType.DMA((n, 3))],
    )(*lands)


def _place_own(name, shard, land, me):
    r, c = shard.shape
    tm = min(r, 512)

    def body(me_ref, shard_ref, land_ref, out_ref):
        out_ref[...] = shard_ref[...]

    return pl.pallas_call(
        body, name=name,
        grid_spec=pltpu.PrefetchScalarGridSpec(
            num_scalar_prefetch=1, grid=(r // tm,),
            in_specs=[pl.BlockSpec((tm, c), lambda i, me_ref: (i, 0)), ANY_SPEC],
            out_specs=pl.BlockSpec((None, tm, c), lambda i, me_ref: (me_ref[0], i, 0))),
        out_shape=jax.ShapeDtypeStruct(land.shape, land.dtype), input_output_aliases={2: 0},
        compiler_params=_params(("arbitrary",)),
    )(me, shard, land)


def _sum_slabs(name, grads, land, me):
    _, r, c = grads.shape
    tm = min(r, 256)

    def body(me_ref, own_ref, land_ref, out_ref):
        acc = own_ref[...].astype(F32)
        for p in range(3):
            acc = acc + land_ref[p].astype(F32)
        out_ref[...] = acc

    return pl.pallas_call(
        body, name=name,
        grid_spec=pltpu.PrefetchScalarGridSpec(
            num_scalar_prefetch=1, grid=(r // tm,),
            in_specs=[pl.BlockSpec((None, tm, c), lambda i, me_ref: (me_ref[0], i, 0)),
                      pl.BlockSpec((3, tm, c), lambda i, me_ref: (0, i, 0))],
            out_specs=pl.BlockSpec((tm, c), lambda i, me_ref: (i, 0))),
        out_shape=jax.ShapeDtypeStruct((r, c), F32), compiler_params=_params(("parallel",)),
    )(me, grads, land)


def _allreduce_stats(stats, run_after):
    n = len(stats)

    def body(*refs):
        ins, outs = refs[:n], refs[n + 1:2 * n + 1]
        mine, gath = refs[2 * n + 1:3 * n + 1], refs[3 * n + 1:4 * n + 1]
        send, recv = refs[4 * n + 1:]
        x, y, c = lax.axis_index("x"), lax.axis_index("y"), lax.axis_index("c")
        me = 4 * x + 2 * y + c
        flips = [(bx, by, bc) for bx in (0, 1) for by in (0, 1) for bc in (0, 1)][1:]

        def peer(f):
            return (x + f[0] * (1 - 2 * x), y + f[1] * (1 - 2 * y), c + f[2] * (1 - 2 * c))

        copies = []
        for t in range(n):
            tot = ins[t][0]
            for b in range(1, ins[t].shape[0]):
                tot = tot + ins[t][b]
            mine[t][...] = tot
            gath[t][me] = tot
            for k, f in enumerate(flips):
                cp = pltpu.make_async_remote_copy(
                    src_ref=mine[t], dst_ref=gath[t].at[me], send_sem=send.at[t, k],
                    recv_sem=recv.at[t, k], device_id=peer(f), device_id_type=MESH)
                cp.start()
                copies.append(cp)
        for t in range(n):
            for k, f in enumerate(flips):
                px, py, pc = peer(f)
                pltpu.make_async_remote_copy(
                    src_ref=mine[t], dst_ref=gath[t].at[4 * px + 2 * py + pc], send_sem=send.at[t, k],
                    recv_sem=recv.at[t, k], device_id=(px, py, pc), device_id_type=MESH).wait_recv()
        for cp in copies:
            cp.wait_send()
        for t in range(n):
            tot = gath[t][0]
            for dev in range(1, 8):
                tot = tot + gath[t][dev]
            outs[t][...] = tot

    vm = pl.BlockSpec(memory_space=pltpu.VMEM)
    return pl.pallas_call(
        body, name="allreduce_stats", in_specs=[vm] * n + [ANY_SPEC], out_specs=[vm] * n,
        out_shape=[jax.ShapeDtypeStruct(s.shape[1:], F32) for s in stats],
        scratch_shapes=[pltpu.VMEM(s.shape[1:], F32) for s in stats]
        + [pltpu.VMEM((8, *s.shape[1:]), F32) for s in stats]
        + [pltpu.SemaphoreType.DMA((n, 7)), pltpu.SemaphoreType.DMA((n, 7))],
    )(*stats, run_after)


def _adamw(name, w, m, v, g_parts):
    r, c = w.shape
    tm = min(r, 128)
    n_g = len(g_parts)

    def body(*refs):
        w_ref, m_ref, v_ref = refs[:3]
        g_refs = refs[3:3 + n_g]
        g_out, d_out, m_out, v_out = refs[3 + n_g:]
        g = g_refs[0][...]
        for gr in g_refs[1:]:
            g = g + gr[...]
        m_new = ADAM_B1 * m_ref[...] + (1.0 - ADAM_B1) * g
        v_new = ADAM_B2 * v_ref[...] + (1.0 - ADAM_B2) * (g * g)
        m_hat = m_new / (1.0 - ADAM_B1 ** ADAM_STEP)
        v_hat = v_new / (1.0 - ADAM_B2 ** ADAM_STEP)
        g_out[...] = g
        d_out[...] = -ADAM_LR * (m_hat / (jnp.sqrt(v_hat) + ADAM_EPS) + ADAM_WD * w_ref[...])
        m_out[...] = m_new
        v_out[...] = v_new

    blk = pl.BlockSpec((tm, c), lambda i: (i, 0))
    out = jax.ShapeDtypeStruct((r, c), F32)
    return pl.pallas_call(
        body, name=name, grid=(r // tm,), in_specs=[blk] * (3 + n_g), out_specs=[blk] * 4,
        out_shape=[out] * 4, compiler_params=_params(("parallel",)),
    )(w, m, v, *g_parts)


def _rope_tables(positions):
    half = HEAD_DIM // 2
    inv_freq = ROPE_THETA ** (-jnp.arange(half, dtype=F32) / half)
    ang = positions.astype(F32)[0, :, None] * inv_freq
    cos, sin = jnp.cos(ang), jnp.sin(ang)
    return jnp.concatenate([cos, cos], axis=-1), jnp.concatenate([-sin, sin], axis=-1)


def kernel(x, positions, w_in, w_pool, pool_scale, w_branch_attn, w_branch_pool, w_out, ln_mix_g, ln_mix_b, w_ff1, w_ff2, ln_ff_g, ln_ff_b, loss_target, m_w_in, m_w_pool, m_pool_scale, m_w_branch_attn, m_w_branch_pool, m_w_out, m_ln_mix_g, m_ln_mix_b, m_w_ff1, m_w_ff2, m_ln_ff_g, m_ln_ff_b, v_w_in, v_w_pool, v_pool_scale, v_w_branch_attn, v_w_branch_pool, v_w_out, v_ln_mix_g, v_ln_mix_b, v_w_ff1, v_w_ff2, v_ln_ff_g, v_ln_ff_b):
    s, d = x.shape[1], x.shape[2]
    aw = d // 2
    ng = len(POOL_WINDOWS)
    pgw = aw // ng
    x2d, target = x[0], loss_target[0]
    xb = x2d.astype(BF16)
    cos_t, sin_t = _rope_tables(positions)

    big = {"w_in": w_in[0], "w_pool": w_pool[0].reshape(-1, pgw), "w_branch_attn": w_branch_attn[0],
           "w_branch_pool": w_branch_pool[0], "w_out": w_out[0], "w_ff1": w_ff1[0], "w_ff2": w_ff2[0]}
    names = list(big)
    me_chip = (2 * lax.axis_index("x") + lax.axis_index("y")).astype(jnp.int32).reshape(1)
    land_shapes = [(N_CHIPS, *big[k].shape) for k in names]
    gathering_in = _exchange_start("gather_start_in", GATHER_HALF,
                                   [_to_bf16("to_bf16_w_in", big["w_in"], positions)], land_shapes[:1])
    shards = [_to_bf16(f"to_bf16_{k}", big[k], gathering_in["token"]) for k in names[1:]]
    gathering = _exchange_start("gather_start", GATHER, shards, land_shapes[1:])

    def gathered(name, which, after):
        srcs, lands = _exchange_wait(f"gather_wait_{name}", GATHER, gathering, which, after)
        return [_place_own(f"place_own_{names[w + 1]}", srcs[i], lands[i], me_chip)
                for i, w in enumerate(which)]

    rows_sh = pgw // N_CHIPS
    dff = N_CHIPS * big["w_ff2"].shape[0]

    srcs, lands = _exchange_wait("gather_wait_in", GATHER_HALF, gathering_in, [0],
                                 [gathering["token"], xb, cos_t, sin_t])
    w_in_st = _place_own("place_own_w_in", srcs[0], _swap_halves("swap_halves_in", lands)[0], me_chip)
    h, hv4, hv16 = _in_proj(xb, w_in_st, cos_t, sin_t, aw)
    (wp_st,) = gathered("pool", [0], h)
    wp = wp_st.reshape(N_CHIPS, ng, rows_sh, pgw).transpose(1, 0, 2, 3).reshape(ng, pgw, pgw)
    qkv = {1: (h, h, h), 4: (hv4, hv4, hv4), 16: (hv16, hv16, hv16)}
    offs = {dil: (0, dil, 2 * dil) for dil in DILATIONS}
    o_parts, lse_parts = [], []
    for dil in DILATIONS:
        o_p, lse_p = _attn_fwd(f"attn_fwd_d{dil}", *qkv[dil], offs[dil], aw, dil * aw)
        o_parts.append(o_p)
        lse_parts.append(lse_p)
    o_attn, lse, lse4, lse16 = _attn_combine(o_parts[0], lse_parts[0], o_parts[1:], lse_parts[1:])
    p, pw, y = _pool_fwd(h, wp, pool_scale, aw)
    wba_st, wbp_st, w_out_st = gathered("mix", [1, 2, 3], y)
    w_out_full = w_out_st.reshape(d, d)
    ya, yp, merged = _branch_merge(o_attn, y, wba_st, wbp_st, h, aw, d)
    xhat1, rstd1, x1b = _mix_norm(merged, w_out_full, x2d, ln_mix_g, ln_mix_b)
    w1_st, w2_st = gathered("ff", [4, 5], x1b)
    w2_full = w2_st.reshape(dff, d)
    r, r_slope = _ff_up(x1b, w1_st)
    dz2, dz2b, st2 = _ff_down_loss(r, w2_full, xhat1, ln_mix_g, ln_mix_b, ln_ff_g, ln_ff_b, target)

    def scatter_start(name, grads):
        return _exchange_start(f"scatter_start_{name}", SCATTER, grads, [(3, *g.shape[1:]) for g in grads])

    da = _ff_down_bwd(dz2b, w2_full, r_slope)
    g_w2 = _wgrad("wgrad_ff2", r, dz2b, d).reshape(N_CHIPS, dff // N_CHIPS, d)
    g_w1 = _wgrad("wgrad_ff1", x1b, da, dff // N_CHIPS)
    sent_ff = scatter_start("ff", [g_w1, g_w2])
    dz1, dz1b, st1 = _ff_up_bwd(da, w1_st, dz2, xhat1, rstd1, ln_mix_g, [sent_ff["token"]])
    dya, dyp, dgate = _mix_bwd(dz1b, w_out_full, h, ya, yp, aw)
    g_wout = _wgrad("wgrad_out", merged, dz1b, d).reshape(N_CHIPS, d // N_CHIPS, d)
    g_wba = _wgrad("wgrad_branch_attn", o_attn, dya, d // N_CHIPS)
    g_wbp = _wgrad("wgrad_branch_pool", y, dyp, d // N_CHIPS)
    do, delta, do4, do16, delta4, delta16 = _attn_out_bwd(dya, wba_st, o_attn)
    dpw, stp = _pool_out_bwd(dyp, wbp_st, pw, pool_scale)
    dwp, du = _pool_bwd(dpw, p, wp)
    g_wp = dwp.reshape(ng, N_CHIPS, rows_sh, pgw).transpose(1, 0, 2, 3).reshape(
        N_CHIPS, ng * rows_sh, pgw).astype(BF16)
    sent_mix = scatter_start("mix", [g_wp, g_wba, g_wbp, g_wout])

    bwd_in = {1: (do, lse, delta), 4: (do4, lse4, delta4), 16: (do16, lse16, delta16)}
    dqkv = {}
    for dil in DILATIONS:
        args = (*qkv[dil], *bwd_in[dil], offs[dil], aw, dil * aw)
        dqkv[dil] = _attn_bwd(f"attn_bwd_d{dil}", *args)
    dqkvu = _qkvu_grad(dqkv[1], dqkv[4], dqkv[16], du, cos_t, sin_t)
    g_win = _wgrad("wgrad_in_gates", xb, dgate, d, shard0=2, n_shards=N_CHIPS)
    g_win = _wgrad("wgrad_in_qkvu", xb, dqkvu, d, n_shards=N_CHIPS, fill=g_win)
    sent_in = scatter_start("in", [g_win])
    dx_a = _in_proj_bwd_x("in_proj_bwd_qkvu", dqkvu, w_in_st, 0, dz1, ALPHA,
                          [sent_mix["token"], sent_in["token"]])
    grad_x = _in_proj_bwd_x("in_proj_bwd_gates", dgate, w_in_st, 2, dx_a, 1.0)

    moments = {"w_in": (m_w_in, v_w_in), "w_pool": (m_w_pool, v_w_pool),
               "w_branch_attn": (m_w_branch_attn, v_w_branch_attn),
               "w_branch_pool": (m_w_branch_pool, v_w_branch_pool), "w_out": (m_w_out, v_w_out),
               "w_ff1": (m_w_ff1, v_w_ff1), "w_ff2": (m_w_ff2, v_w_ff2)}
    originals = {"w_in": w_in, "w_pool": w_pool, "w_branch_attn": w_branch_attn,
                 "w_branch_pool": w_branch_pool, "w_out": w_out, "w_ff1": w_ff1, "w_ff2": w_ff2}
    res = {}

    def summed(name, sent, keys, after):
        srcs, lands = _exchange_wait(f"scatter_wait_{name}", SCATTER, sent, list(range(len(keys))), after)
        parts = [_sum_slabs(f"sum_slabs_{k}", srcs[i], lands[i], me_chip) for i, k in enumerate(keys)]
        return _exchange_start(f"cores_start_{name}", SIBLING, parts, [a.shape for a in parts])

    def updated(name, swapping, keys, after):
        mine, other = _exchange_wait(f"cores_wait_{name}", SIBLING, swapping, list(range(len(keys))), after)
        for i, k in enumerate(keys):
            mk, vk = (a.reshape(big[k].shape) for a in moments[k])
            outs = _adamw(f"adamw_{k}", big[k], mk, vk, [mine[i], other[i]])
            res[k] = [o.reshape(originals[k].shape) for o in outs]

    groups = {"ff": ["w_ff1", "w_ff2"], "mix": ["w_pool", "w_branch_attn", "w_branch_pool", "w_out"],
              "in": ["w_in"]}
    swap_ff = summed("ff", sent_ff, groups["ff"], grad_x)
    swap_mix = summed("mix", sent_mix, groups["mix"], swap_ff["token"])
    swap_in = summed("in", sent_in, groups["in"], swap_mix["token"])
    updated("ff", swap_ff, groups["ff"], swap_in["token"])
    updated("mix", swap_mix, groups["mix"], res["w_ff2"][0])
    updated("in", swap_in, groups["in"], res["w_out"][0])
    tot2, tot1, totp = _allreduce_stats([st2, st1, stp], res["w_in"][0])

    def pad_d(a):
        return jnp.pad(a, ((0, 0), (0, d - a.shape[1])))

    small = ["ln_mix_g", "ln_mix_b", "ln_ff_g", "ln_ff_b", "pool_scale"]
    small_w = {"ln_mix_g": ln_mix_g, "ln_mix_b": ln_mix_b, "ln_ff_g": ln_ff_g, "ln_ff_b": ln_ff_b,
               "pool_scale": pool_scale}
    small_m = {"ln_mix_g": m_ln_mix_g, "ln_mix_b": m_ln_mix_b, "ln_ff_g": m_ln_ff_g,
               "ln_ff_b": m_ln_ff_b, "pool_scale": m_pool_scale}
    small_v = {"ln_mix_g": v_ln_mix_g, "ln_mix_b": v_ln_mix_b, "ln_ff_g": v_ln_ff_g,
               "ln_ff_b": v_ln_ff_b, "pool_scale": v_pool_scale}
    small_g = [tot1[0:1], tot1[1:2], tot2[0:1], tot2[1:2], pad_d(totp[0:1])]

    def pack(rows):
        return jnp.concatenate([pad_d(a) for a in rows] + [jnp.zeros((8 - len(rows), d), F32)], axis=0)

    outs = _adamw("adamw_small", pack([small_w[k] for k in small]), pack([small_m[k] for k in small]),
                  pack([small_v[k] for k in small]), [pack(small_g)])
    for i, k in enumerate(small):
        res[k] = [o[i:i + 1, :small_w[k].shape[1]] for o in outs]
    loss = tot2[2, 0]

    order = ["w_in", "w_pool", "pool_scale", "w_branch_attn", "w_branch_pool", "w_out", "ln_mix_g",
             "ln_mix_b", "w_ff1", "w_ff2", "ln_ff_g", "ln_ff_b"]
    result = [loss, grad_x[None]]
    for idx in range(4):
        result += [res[k][idx] for k in order]
    return tuple(result)
```

```python
import jax
import jax.numpy as jnp
from jax import lax
from jax.experimental import pallas as pl
from jax.experimental.pallas import tpu as pltpu

F32 = jnp.float32
BF16 = jnp.bfloat16
MESH = pl.DeviceIdType.MESH

HEAD_DIM = 128
SUB_BLOCK = 128
DILATIONS = (1, 4, 16)
POOL_WINDOWS = (2, 4, 8, 16)
POOL_HALO = 16
ROPE_THETA = 10000.0
LN_EPS = 1e-5
ALPHA = 2.0 ** 0.25
ADAM_LR, ADAM_B1, ADAM_B2, ADAM_EPS, ADAM_WD, ADAM_STEP = 0.001, 0.9, 0.999, 1e-08, 0.01, 10
NEG = -1e30
N_CHIPS = 4
VMEM_LIMIT = 62 * 1024 * 1024
EPILOGUE_ROWS = 128

def _params(sem=None, vmem=VMEM_LIMIT):
    kw = {"vmem_limit_bytes": vmem}
    if sem is not None:
        kw["dimension_semantics"] = sem
    return pltpu.CompilerParams(**kw)


def _dot(a, b, contract):
    return lax.dot_general(a, b, (contract, ((), ())), preferred_element_type=F32)


ANY_SPEC = pl.BlockSpec(memory_space=pl.ANY)
HBM_SPEC = pl.BlockSpec(memory_space=pltpu.HBM)
SEM_SPEC = pl.BlockSpec(memory_space=pltpu.SEMAPHORE)
DATAFLOW = pltpu.SideEffectType.DATAFLOW_SIDE_EFFECTING


def _hbm(a):
    return pltpu.with_memory_space_constraint(a, pltpu.HBM)


NN = ((1,), (0,))
NT = ((1,), (1,))
TN = ((0,), (0,))


def _mm(name, grid, a, a_spec, b, b_spec, contract, extras, extra_specs, out_shape, out_specs,
        epilogue, acc_shape, acc_as_ref=False, run_after=(), scratch=(),
        semantics=("parallel", "parallel", "arbitrary"), fill=None):
    nk = grid[2]
    n_ex = len(extras)
    n_in = 2 + n_ex + len(run_after) + (fill is not None)
    n_out = len(out_shape)
    n_scr = len(scratch)

    def body(*refs):
        a_ref, b_ref = refs[0], refs[1]
        ex = refs[2:2 + n_ex]
        outs = refs[n_in:n_in + n_out]
        scr = refs[n_in + n_out:n_in + n_out + n_scr]
        if nk == 1:
            epilogue(_dot(a_ref[...], b_ref[...], contract), ex, outs, *scr)
        else:
            acc = refs[n_in + n_out + n_scr]
            k = pl.program_id(2)

            @pl.when(k == 0)
            def _():
                acc[...] = jnp.zeros_like(acc)

            acc[...] += _dot(a_ref[...], b_ref[...], contract)

            @pl.when(k == nk - 1)
            def _():
                epilogue(acc if acc_as_ref else acc[...], ex, outs, *scr)

    acc_scratch = [pltpu.VMEM(acc_shape, F32)] if nk > 1 else []
    filled = [] if fill is None else [fill]
    return pl.pallas_call(
        body, name=name, grid=grid,
        in_specs=[a_spec, b_spec, *extra_specs, *[ANY_SPEC] * (len(run_after) + len(filled))],
        out_specs=out_specs, out_shape=out_shape, scratch_shapes=[*scratch, *acc_scratch],
        input_output_aliases={} if fill is None else {n_in - 1: 0},
        compiler_params=_params(semantics),
    )(a, b, *extras, *run_after, *filled)


def _stats_rows(rows, width):
    idx = lax.broadcasted_iota(jnp.int32, (8, width), 0)
    out = jnp.zeros((8, width), F32)
    for r, v in enumerate(rows):
        out = jnp.where(idx == r, jnp.broadcast_to(v, (8, width)), out)
    return out


def _layer_norm_fwd(z):
    mu = jnp.mean(z, axis=-1, keepdims=True)
    zc = z - mu
    var = jnp.mean(zc * zc, axis=-1, keepdims=True)
    rstd = lax.rsqrt(var + LN_EPS)
    return zc * rstd, rstd


def _layer_norm_bwd(dy, xhat, rstd, g):
    dxh = dy * g
    m1 = jnp.mean(dxh, axis=-1, keepdims=True)
    m2 = jnp.mean(dxh * xhat, axis=-1, keepdims=True)
    return rstd * (dxh - m1 - xhat * m2)


def _heads_scratch(rows, width):
    return pltpu.VMEM((width // HEAD_DIM, rows, HEAD_DIM), F32)


def _to_views(src_ref, view_refs, dtype, heads=None):
    nh, rows, _ = src_ref.shape
    width = nh * HEAD_DIM
    for dil, view_ref in zip(DILATIONS[1:], view_refs):
        for r in range(dil):
            for hh in (range(nh) if heads is None else heads):
                c0 = r * width + hh * HEAD_DIM
                view_ref[:, c0:c0 + HEAD_DIM] = (
                    src_ref[hh, pl.ds(r, rows // dil, stride=dil), :].astype(dtype))


def _from_views(view_refs, dst_refs):
    nh, rows, _ = dst_refs[0].shape
    width = nh * HEAD_DIM
    for dil, view_ref, dst_ref in zip(DILATIONS[1:], view_refs, dst_refs):
        for r in range(dil):
            for hh in range(nh):
                c0 = r * width + hh * HEAD_DIM
                dst_ref[hh, pl.ds(r, rows // dil, stride=dil), :] = (
                    view_ref[:, c0:c0 + HEAD_DIM].astype(F32))


def _view_shape(rows, width, dil, parts=1):
    return (rows // dil, parts * dil * width)


def _in_proj(xb, w_in_st, cos_t, sin_t, aw):
    s, d = xb.shape
    n_sh = w_in_st.shape[2]
    tm, tn = min(s, 1024), aw
    per = n_sh // tn
    grid = (s // tm, (N_CHIPS * n_sh) // tn, 1)

    def epilogue(acc, ex, outs, scr):
        cos_ref, sin_ref = ex
        h_ref, v4_ref, v16_ref = outs
        seg = pl.program_id(1)

        heads = [slice(hh * HEAD_DIM, (hh + 1) * HEAD_DIM) for hh in range(tn // HEAD_DIM)]

        @pl.when(seg < 2)
        def _():
            cos, sin = cos_ref[...], sin_ref[...]
            for hh, sl in enumerate(heads):
                t = acc[:, sl]
                scr[hh] = t * cos + pltpu.roll(t, HEAD_DIM // 2, 1) * sin

        @pl.when(seg == 2)
        def _():
            for hh, sl in enumerate(heads):
                scr[hh] = acc[:, sl]

        @pl.when(seg < 3)
        def _():
            for hh, sl in enumerate(heads):
                h_ref[:, sl] = scr[hh].astype(BF16)
            _to_views(scr, (v4_ref, v16_ref), BF16)

        @pl.when(seg == 3)
        def _():
            h_ref[...] = acc.astype(BF16)

        @pl.when(seg >= 4)
        def _():
            h_ref[...] = (0.5 * jnp.tanh(0.5 * acc) + 0.5).astype(BF16)

    def view_spec(dil):
        return pl.BlockSpec((tm // dil, dil * aw), lambda i, j, k: (i, jnp.minimum(j, 2)))

    return _mm(
        "in_proj", grid, xb, pl.BlockSpec((tm, d), lambda i, j, k: (i, 0)),
        w_in_st, pl.BlockSpec((None, d, tn), lambda i, j, k: (j // per, 0, j % per)), NN,
        [cos_t, sin_t], [pl.BlockSpec((tm, HEAD_DIM), lambda i, j, k: (i, 0))] * 2,
        [jax.ShapeDtypeStruct((s, N_CHIPS * n_sh), BF16)]
        + [jax.ShapeDtypeStruct(_view_shape(s, aw, dil, 3), BF16) for dil in DILATIONS[1:]],
        [pl.BlockSpec((tm, tn), lambda i, j, k: (i, j))] + [view_spec(dil) for dil in DILATIONS[1:]],
        epilogue, None, scratch=[_heads_scratch(tm, aw)],
        semantics=("parallel", "arbitrary", "arbitrary"))


def _pack_heads(cols):
    rows, rep = cols[0].shape[0], HEAD_DIM // len(cols)
    lane = lax.broadcasted_iota(jnp.int32, (rows, HEAD_DIM), 1)
    out = jnp.zeros((rows, HEAD_DIM), F32)
    for hh, col in enumerate(cols):
        out = jnp.where((lane >= hh * rep) & (lane < (hh + 1) * rep), col, out)
    return out


def _head_col(packed, hh, nh):
    lane = lax.broadcasted_iota(jnp.int32, packed.shape, 1)
    return jnp.sum(jnp.where(lane == hh * (HEAD_DIM // nh), packed, 0.0), axis=-1, keepdims=True)


def _band_masks(block_idx):
    qi = lax.broadcasted_iota(jnp.int32, (SUB_BLOCK, 2 * SUB_BLOCK), 0)
    kj = lax.broadcasted_iota(jnp.int32, (SUB_BLOCK, 2 * SUB_BLOCK), 1)
    first_key = jnp.where(block_idx > 0, 0, SUB_BLOCK)
    return (kj >= qi) & (kj <= qi + SUB_BLOCK) & (kj >= first_key)


def _attn_fwd(name, q, k, v, offs, cw, width):
    m = q.shape[0]
    nh = cw // HEAD_DIM
    grid = (width // cw, m // SUB_BLOCK)
    scale = HEAD_DIM ** -0.5

    def body(q_ref, k_ref, v_ref, o_ref, lse_ref, kprev, vprev):
        n = pl.program_id(1)
        valid = _band_masks(n)

        @pl.when(n == 0)
        def _():
            kprev[...] = jnp.zeros_like(kprev)
            vprev[...] = jnp.zeros_like(vprev)

        heads = [slice(hh * HEAD_DIM, (hh + 1) * HEAD_DIM) for hh in range(nh)]
        scs = [_dot(q_ref[:, sl], jnp.concatenate([kprev[:, sl], k_ref[:, sl]], axis=0), NT)
               for sl in heads]
        lses = []
        for hh, sl in enumerate(heads):
            vcat = jnp.concatenate([vprev[:, sl], v_ref[:, sl]], axis=0)
            sc = jnp.where(valid, scs[hh] * scale, NEG)
            mx = jnp.max(sc, axis=-1, keepdims=True)
            p = jnp.exp(sc - mx)
            l = jnp.sum(p, axis=-1, keepdims=True)
            o = _dot(p.astype(BF16), vcat, NN) / l
            o_ref[:, sl] = o.astype(BF16)
            lses.append(mx + jnp.log(l))
        lse_ref[...] = _pack_heads(lses)
        kprev[...] = k_ref[...]
        vprev[...] = v_ref[...]

    def cur(off):
        return pl.BlockSpec((SUB_BLOCK, cw), lambda j, n: (n, j + off))

    return pl.pallas_call(
        body, name=name, grid=grid,
        in_specs=[cur(offs[0]), cur(offs[1]), cur(offs[2])],
        out_specs=[cur(0), pl.BlockSpec((SUB_BLOCK, HEAD_DIM), lambda j, n: (n, j))],
        out_shape=[jax.ShapeDtypeStruct((m, width), BF16),
                   jax.ShapeDtypeStruct((m, width // cw * HEAD_DIM), F32)],
        scratch_shapes=[pltpu.VMEM((SUB_BLOCK, cw), BF16)] * 2,
        compiler_params=_params(("parallel", "arbitrary")),
    )(q, k, v)


def _view_spec(tm, aw, dil):
    return pl.BlockSpec((tm // dil, dil * aw), lambda i: (i, 0))


def _attn_combine(o1, l1, o_views, l_views):
    s, aw = o1.shape
    nh = aw // HEAD_DIM
    tm = min(s, 512)

    def body(o1_ref, l1_ref, o4_ref, o16_ref, l4_ref, l16_ref, o_ref, lse_ref, lse4_ref, lse16_ref,
             so4, so16, sl4, sl16, stot):
        _from_views((o4_ref, o16_ref), (so4, so16))
        _from_views((l4_ref, l16_ref), (sl4, sl16))
        a, b, c = l1_ref[...], sl4[0], sl16[0]
        mx = jnp.maximum(jnp.maximum(a, b), c)
        ea, eb, ec = jnp.exp(a - mx), jnp.exp(b - mx), jnp.exp(c - mx)
        tot = ea + eb + ec
        inv = 1.0 / tot
        wa, wb, wc = ea * inv, eb * inv, ec * inv
        lse_tot = mx + jnp.log(tot)
        stot[0] = lse_tot
        lse_ref[...] = lse_tot
        _to_views(stot, (lse4_ref, lse16_ref), F32)
        for hh in range(nh):
            sl = slice(hh * HEAD_DIM, (hh + 1) * HEAD_DIM)
            o = (_head_col(wa, hh, nh) * o1_ref[:, sl].astype(F32) + _head_col(wb, hh, nh) * so4[hh]
                 + _head_col(wc, hh, nh) * so16[hh])
            o_ref[:, sl] = o.astype(BF16)

    row = pl.BlockSpec((tm, aw), lambda i: (i, 0))
    stat = pl.BlockSpec((tm, HEAD_DIM), lambda i: (i, 0))
    views = [_view_spec(tm, aw, dil) for dil in DILATIONS[1:]]
    stat_views = [_view_spec(tm, HEAD_DIM, dil) for dil in DILATIONS[1:]]
    return pl.pallas_call(
        body, name="attn_combine", grid=(s // tm,), in_specs=[row, stat, *views, *stat_views],
        out_specs=[row, stat, *stat_views],
        out_shape=[jax.ShapeDtypeStruct((s, aw), BF16), jax.ShapeDtypeStruct((s, HEAD_DIM), F32)]
        + [jax.ShapeDtypeStruct(_view_shape(s, HEAD_DIM, dil), F32) for dil in DILATIONS[1:]],
        scratch_shapes=[_heads_scratch(tm, aw)] * 2 + [_heads_scratch(tm, HEAD_DIM)] * 3,
        compiler_params=_params(("parallel",)),
    )(o1, l1, *o_views, *l_views)


def _pool_counts(tm, rows, pgw, row0):
    t = lax.broadcasted_iota(jnp.int32, (rows, len(POOL_WINDOWS) * pgw), 0) + row0
    col = lax.broadcasted_iota(jnp.int32, (rows, len(POOL_WINDOWS) * pgw), 1)
    w = jnp.full((rows, len(POOL_WINDOWS) * pgw), POOL_WINDOWS[0], jnp.int32)
    for g in range(1, len(POOL_WINDOWS)):
        w = jnp.where(col >= g * pgw, POOL_WINDOWS[g], w)
    return jnp.minimum(t + 1, w).astype(F32)


def _window_sums(xs, direction, pgw):
    rows = xs.shape[0]
    acc = xs
    out = None
    col = lax.broadcasted_iota(jnp.int32, xs.shape, 1)
    for g, w in enumerate(POOL_WINDOWS):
        sh = w // 2
        acc = acc + pltpu.roll(acc, sh if direction > 0 else rows - sh, 0)
        out = acc if out is None else jnp.where(col >= g * pgw, acc, out)
    return out


def _pool_fwd(h, wp, scale, aw):
    s = h.shape[0]
    pw_ = aw
    pgw = pw_ // len(POOL_WINDOWS)
    tm = min(s, 512)
    hb = tm // POOL_HALO

    def body(u_ref, halo_ref, wp_ref, sc_ref, p_ref, pw_ref, y_ref):
        i = pl.program_id(0)
        u = u_ref[...].astype(F32)
        halo = halo_ref[...].astype(F32) * jnp.where(i > 0, 1.0, 0.0)
        xs = jnp.concatenate([halo, u], axis=0)
        sums = _window_sums(xs, +1, pgw)[POOL_HALO:]
        p = (sums / _pool_counts(tm, tm, pgw, i * tm) - u).astype(BF16)
        p_ref[...] = p
        sc = sc_ref[...]
        for g in range(len(POOL_WINDOWS)):
            sl = slice(g * pgw, (g + 1) * pgw)
            pw = _dot(p[:, sl], wp_ref[g], NN)
            pw_ref[:, sl] = pw.astype(BF16)
            y_ref[:, sl] = (pw * sc[:, sl]).astype(BF16)

    out = jax.ShapeDtypeStruct((s, pw_), BF16)
    row = pl.BlockSpec((tm, pw_), lambda i: (i, 0))
    return pl.pallas_call(
        body, name="pool_fwd", grid=(s // tm,),
        in_specs=[pl.BlockSpec((tm, pw_), lambda i: (i, 3)),
                  pl.BlockSpec((POOL_HALO, pw_), lambda i: (jnp.maximum(i * hb - 1, 0), 3)),
                  pl.BlockSpec(wp.shape, lambda i: (0, 0, 0)),
                  pl.BlockSpec((1, pw_), lambda i: (0, 0))],
        out_specs=[row, row, row], out_shape=[out, out, out],
        compiler_params=_params(("parallel",)),
    )(h, h, wp, scale)


def _branch_merge(o_attn, y, wba_st, wbp_st, h, aw, d):
    s = o_attn.shape[0]
    tn = wba_st.shape[2]
    tm = min(s, 1024)
    ga0 = 4 * aw // tn
    gp0 = (4 * aw + d) // tn

    def body(o_ref, y_ref, wa_ref, wp_ref, sga_ref, sgp_ref, ya_ref, yp_ref, mg_ref):
        ya = _dot(o_ref[...], wa_ref[...], NN)
        yp = _dot(y_ref[...], wp_ref[...], NN)
        ya_ref[...] = ya.astype(BF16)
        yp_ref[...] = yp.astype(BF16)
        mg_ref[...] = (sga_ref[...].astype(F32) * ya + sgp_ref[...].astype(F32) * yp).astype(BF16)

    out = jax.ShapeDtypeStruct((s, d), BF16)
    blk = pl.BlockSpec((tm, tn), lambda i, j: (i, j))
    return pl.pallas_call(
        body, name="branch_merge", grid=(s // tm, N_CHIPS),
        in_specs=[pl.BlockSpec((tm, aw), lambda i, j: (i, 0)),
                  pl.BlockSpec((tm, aw), lambda i, j: (i, 0)),
                  pl.BlockSpec((None, aw, tn), lambda i, j: (j, 0, 0)),
                  pl.BlockSpec((None, aw, tn), lambda i, j: (j, 0, 0)),
                  pl.BlockSpec((tm, tn), lambda i, j: (i, j + ga0)),
                  pl.BlockSpec((tm, tn), lambda i, j: (i, j + gp0))],
        out_specs=[blk, blk, blk], out_shape=[out, out, out],
        compiler_params=_params(("parallel", "parallel")),
    )(o_attn, y, wba_st, wbp_st, h, h)


def _mix_norm(merged, w_out, x, g1, b1):
    s, d = x.shape
    tm = min(s, 256)

    def epilogue(acc, ex, outs):
        x_ref, g_ref, b_ref = ex
        xh_ref, rs_ref, xb_ref = outs
        xhat, rstd = _layer_norm_fwd(ALPHA * x_ref[...] + acc)
        xh_ref[...] = xhat
        rs_ref[...] = rstd
        xb_ref[...] = (xhat * g_ref[...] + b_ref[...]).astype(BF16)

    row = pl.BlockSpec((tm, d), lambda i, j, k: (i, 0))
    vec = pl.BlockSpec((1, d), lambda i, j, k: (0, 0))
    return _mm(
        "mix_norm", (s // tm, 1, 1), merged, row, w_out, pl.BlockSpec((d, d), lambda i, j, k: (0, 0)),
        NN, [x, g1, b1], [row, vec, vec],
        [jax.ShapeDtypeStruct((s, d), F32), jax.ShapeDtypeStruct((s, 1), F32),
         jax.ShapeDtypeStruct((s, d), BF16)],
        [row, pl.BlockSpec((tm, 1), lambda i, j, k: (i, 0)), row], epilogue, None)


def _ff_up(x1b, w1_st):
    s, d = x1b.shape
    n_sh = w1_st.shape[2]
    tm, tn = min(s, 1024), min(n_sh, 1024)
    per = n_sh // tn

    def epilogue(acc, ex, outs):
        r = jnp.maximum(acc, 0.0)
        outs[0][...] = (r * r).astype(BF16)
        outs[1][...] = (2.0 * r).astype(BF16)

    blk = pl.BlockSpec((tm, tn), lambda i, j, k: (i, j))
    out = jax.ShapeDtypeStruct((s, N_CHIPS * n_sh), BF16)
    return _mm(
        "ff_up", (s // tm, N_CHIPS * per, 1), x1b, pl.BlockSpec((tm, d), lambda i, j, k: (i, 0)),
        w1_st, pl.BlockSpec((None, d, tn), lambda i, j, k: (j // per, 0, j % per)), NN, [], [],
        [out, out], [blk, blk], epilogue, None)


def _ff_down_loss(r, w2, xhat1, g1, b1, g2, b2, target):
    s, d = xhat1.shape
    dff = r.shape[1]
    tm, tk = min(s, 512), min(dff, 2048)
    ch = min(tm, EPILOGUE_ROWS)

    def epilogue(acc_ref, ex, outs):
        xh1_ref, g1_ref, b1_ref, g2_ref, b2_ref, t_ref = ex
        dz_ref, dzb_ref, st_ref = outs
        g1v, b1v, g2v, b2v = g1_ref[...], b1_ref[...], g2_ref[...], b2_ref[...]
        dg = db = loss = None
        for c in range(tm // ch):
            rows = slice(c * ch, (c + 1) * ch)
            x1 = xh1_ref[rows, :] * g1v + b1v
            xhat2, rstd2 = _layer_norm_fwd(ALPHA * x1 + acc_ref[rows, :])
            err = xhat2 * g2v + b2v - t_ref[rows, :]
            dy = err * (1.0 / d)
            dz = _layer_norm_bwd(dy, xhat2, rstd2, g2v)
            dz_ref[rows, :] = dz
            dzb_ref[rows, :] = dz.astype(BF16)
            parts = (jnp.sum(dy * xhat2, axis=0, keepdims=True), jnp.sum(dy, axis=0, keepdims=True),
                     jnp.sum(jnp.sum(err * err, axis=-1, keepdims=True), axis=0, keepdims=True))
            dg, db, loss = parts if c == 0 else (dg + parts[0], db + parts[1], loss + parts[2])
        st_ref[...] = _stats_rows([dg, db, jnp.broadcast_to((0.5 / d) * loss, (1, d))], d)

    row = pl.BlockSpec((tm, d), lambda i, j, k: (i, 0))
    vec = pl.BlockSpec((1, d), lambda i, j, k: (0, 0))
    return _mm(
        "ff_down_loss", (s // tm, 1, dff // tk), r, pl.BlockSpec((tm, tk), lambda i, j, k: (i, k)),
        w2, pl.BlockSpec((tk, d), lambda i, j, k: (k, 0)), NN,
        [xhat1, g1, b1, g2, b2, target], [row, vec, vec, vec, vec, row],
        [jax.ShapeDtypeStruct((s, d), F32), jax.ShapeDtypeStruct((s, d), BF16),
         jax.ShapeDtypeStruct((s // tm, 8, d), F32)],
        [row, row, pl.BlockSpec((None, 8, d), lambda i, j, k: (i, 0, 0))], epilogue, (tm, d),
        acc_as_ref=True)


def _ff_down_bwd(dz2b, w2, r_slope):
    s, d = dz2b.shape
    dff = r_slope.shape[1]
    tm, tn = min(s, 1024), min(dff, 1024)

    def epilogue(acc, ex, outs):
        outs[0][...] = (acc * ex[0][...].astype(F32)).astype(BF16)

    blk = pl.BlockSpec((tm, tn), lambda i, j, k: (i, j))
    return _mm(
        "ff_down_bwd", (s // tm, dff // tn, 1), dz2b, pl.BlockSpec((tm, d), lambda i, j, k: (i, 0)),
        w2, pl.BlockSpec((tn, d), lambda i, j, k: (j, 0)), NT, [r_slope], [blk],
        [jax.ShapeDtypeStruct((s, dff), BF16)], [blk], epilogue, None)[0]


def _wgrad(name, a, g, n_sh, shard0=0, n_shards=None, fill=None):
    s, rows = a.shape
    cols = g.shape[1]
    tm, tn, tk = min(rows, 2048), min(cols, 1024), min(s, 2048)
    if tn >= n_sh:
        span = tn // n_sh
        out_spec = pl.BlockSpec((span, tm, n_sh), lambda i, j, k: (j + shard0 // span, i, 0))

        def epilogue(acc_ref, ex, outs):
            for sh in range(span):
                outs[0][sh] = acc_ref[:, sh * n_sh:(sh + 1) * n_sh].astype(BF16)
    else:
        per = n_sh // tn
        out_spec = pl.BlockSpec((None, tm, tn), lambda i, j, k: (shard0 + j // per, i, j % per))

        def epilogue(acc_ref, ex, outs):
            outs[0][...] = acc_ref[...].astype(BF16)

    return _mm(
        name, (rows // tm, cols // tn, s // tk), a, pl.BlockSpec((tk, tm), lambda i, j, k: (k, i)),
        g, pl.BlockSpec((tk, tn), lambda i, j, k: (k, j)), TN, [], [],
        [jax.ShapeDtypeStruct((n_shards or cols // n_sh, rows, n_sh), BF16)], [out_spec], epilogue,
        (tm, tn), acc_as_ref=True, fill=fill)[0]


def _ff_up_bwd(da, w1_st, dz2, xhat1, rstd1, g1, run_after):
    s, d = dz2.shape
    n_sh = w1_st.shape[2]
    tm, tk = min(s, 512), min(n_sh, 2048)
    per = n_sh // tk
    ch = min(tm, EPILOGUE_ROWS)

    def epilogue(acc_ref, ex, outs):
        dz2_ref, xh_ref, rs_ref, g_ref = ex
        dz_ref, dzb_ref, st_ref = outs
        gv = g_ref[...]
        dg = db = None
        for c in range(tm // ch):
            rows = slice(c * ch, (c + 1) * ch)
            dx1 = ALPHA * dz2_ref[rows, :] + acc_ref[rows, :]
            xhat = xh_ref[rows, :]
            dz = _layer_norm_bwd(dx1, xhat, rs_ref[rows, :], gv)
            dz_ref[rows, :] = dz
            dzb_ref[rows, :] = dz.astype(BF16)
            parts = (jnp.sum(dx1 * xhat, axis=0, keepdims=True), jnp.sum(dx1, axis=0, keepdims=True))
            dg, db = parts if c == 0 else (dg + parts[0], db + parts[1])
        st_ref[...] = _stats_rows([dg, db], d)

    row = pl.BlockSpec((tm, d), lambda i, j, k: (i, 0))
    return _mm(
        "ff_up_bwd", (s // tm, 1, N_CHIPS * per), da, pl.BlockSpec((tm, tk), lambda i, j, k: (i, k)),
        w1_st, pl.BlockSpec((None, d, tk), lambda i, j, k: (k // per, 0, k % per)), NT,
        [dz2, xhat1, rstd1, g1],
        [row, row, pl.BlockSpec((tm, 1), lambda i, j, k: (i, 0)), pl.BlockSpec((1, d), lambda i, j, k: (0, 0))],
        [jax.ShapeDtypeStruct((s, d), F32), jax.ShapeDtypeStruct((s, d), BF16),
         jax.ShapeDtypeStruct((s // tm, 8, d), F32)],
        [row, row, pl.BlockSpec((None, 8, d), lambda i, j, k: (i, 0, 0))], epilogue, (tm, d),
        acc_as_ref=True, run_after=run_after)


def _mix_bwd(dz1b, w_out, h, ya, yp, aw):
    s, d = dz1b.shape
    tm = min(s, 256)
    gblk = 4 * aw // d

    def epilogue(acc, ex, outs):
        sga_ref, sgp_ref, ya_ref, yp_ref = ex
        dya_ref, dyp_ref, dg_ref = outs
        sga, sgp = sga_ref[...].astype(F32), sgp_ref[...].astype(F32)
        dya_ref[...] = (acc * sga).astype(BF16)
        dyp_ref[...] = (acc * sgp).astype(BF16)
        dg_ref[:, :d] = (acc * ya_ref[...].astype(F32) * (sga * (1.0 - sga))).astype(BF16)
        dg_ref[:, d:] = (acc * yp_ref[...].astype(F32) * (sgp * (1.0 - sgp))).astype(BF16)

    row = pl.BlockSpec((tm, d), lambda i, j, k: (i, 0))
    return _mm(
        "mix_bwd", (s // tm, 1, 1), dz1b, row, w_out, pl.BlockSpec((d, d), lambda i, j, k: (0, 0)), NT,
        [h, h, ya, yp],
        [pl.BlockSpec((tm, d), lambda i, j, k: (i, gblk)),
         pl.BlockSpec((tm, d), lambda i, j, k: (i, gblk + 1)), row, row],
        [jax.ShapeDtypeStruct((s, d), BF16), jax.ShapeDtypeStruct((s, d), BF16),
         jax.ShapeDtypeStruct((s, 2 * d), BF16)],
        [row, row, pl.BlockSpec((tm, 2 * d), lambda i, j, k: (i, 0))], epilogue, None)


def _branch_in_bwd(name, tm, dyb, wb_st, epilogue, extras, extra_specs, out_shape, out_specs,
                   scratch=()):
    s, d = dyb.shape
    aw = wb_st.shape[1]
    wb_t = wb_st.transpose(0, 2, 1).reshape(d, aw)
    return _mm(
        name, (s // tm, 1, 1), dyb, pl.BlockSpec((tm, d), lambda i, j, k: (i, 0)),
        wb_t, pl.BlockSpec((d, aw), lambda i, j, k: (0, 0)), NN,
        extras, extra_specs, out_shape, out_specs, epilogue, None, scratch=scratch)


def _attn_out_bwd(dya, wba_st, o_attn):
    s, aw = o_attn.shape
    tm = min(s, 512)

    def epilogue(acc_ref, ex, outs, sdo, sdl):
        do_ref, dl_ref, do4_ref, do16_ref, dl4_ref, dl16_ref = outs
        deltas = []
        for hh in range(aw // HEAD_DIM):
            sl = slice(hh * HEAD_DIM, (hh + 1) * HEAD_DIM)
            do = acc_ref[:, sl]
            deltas.append(jnp.sum(do * ex[0][:, sl].astype(F32), axis=-1, keepdims=True))
            sdo[hh] = do
            do_ref[:, sl] = do.astype(BF16)
        packed = _pack_heads(deltas)
        sdl[0] = packed
        dl_ref[...] = packed
        _to_views(sdo, (do4_ref, do16_ref), BF16)
        _to_views(sdl, (dl4_ref, dl16_ref), F32)

    def specs(width):
        return ([pl.BlockSpec((tm, width), lambda i, j, k: (i, 0))]
                + [pl.BlockSpec((tm // dil, dil * width), lambda i, j, k: (i, 0)) for dil in DILATIONS[1:]])

    def shapes(width, dtype):
        return ([jax.ShapeDtypeStruct((s, width), dtype)]
                + [jax.ShapeDtypeStruct(_view_shape(s, width, dil), dtype) for dil in DILATIONS[1:]])

    do_specs, dl_specs = specs(aw), specs(HEAD_DIM)
    do_shapes, dl_shapes = shapes(aw, BF16), shapes(HEAD_DIM, F32)
    return _branch_in_bwd(
        "attn_out_bwd", tm, dya, wba_st, epilogue, [o_attn], [do_specs[0]],
        [do_shapes[0], dl_shapes[0], *do_shapes[1:], *dl_shapes[1:]],
        [do_specs[0], dl_specs[0], *do_specs[1:], *dl_specs[1:]],
        scratch=[_heads_scratch(tm, aw), _heads_scratch(tm, HEAD_DIM)])


def _pool_out_bwd(dyp, wbp_st, pw, scale):
    s, pw_ = pw.shape
    tm = min(s, 1024)

    def epilogue(acc_ref, ex, outs):
        pw_ref, sc_ref = ex
        dpw_ref, st_ref = outs
        acc = acc_ref[...]
        dpw_ref[...] = (acc * sc_ref[...]).astype(BF16)
        st_ref[...] = _stats_rows([jnp.sum(acc * pw_ref[...].astype(F32), axis=0, keepdims=True)], pw_)

    row = pl.BlockSpec((tm, pw_), lambda i, j, k: (i, 0))
    return _branch_in_bwd(
        "pool_out_bwd", tm, dyp, wbp_st, epilogue, [pw, scale],
        [row, pl.BlockSpec((1, pw_), lambda i, j, k: (0, 0))],
        [jax.ShapeDtypeStruct((s, pw_), BF16), jax.ShapeDtypeStruct((s // tm, 8, pw_), F32)],
        [row, pl.BlockSpec((None, 8, pw_), lambda i, j, k: (i, 0, 0))])


def _pool_bwd(dpw, p, wp):
    s, pw_ = p.shape
    ng = len(POOL_WINDOWS)
    pgw = pw_ // ng
    tm = min(s, 512)
    hb = tm // POOL_HALO
    nblk = s // tm

    def body(dpw_ref, nxt_ref, p_ref, wp_ref, dwp_ref, du_ref):
        i = pl.program_id(0)
        nxt = (nxt_ref[...].astype(F32) * jnp.where(i < nblk - 1, 1.0, 0.0)).astype(BF16)
        dpw_all = jnp.concatenate([dpw_ref[...], nxt], axis=0)

        @pl.when(i == 0)
        def _():
            dwp_ref[...] = jnp.zeros_like(dwp_ref)

        dps = []
        for g in range(ng):
            sl = slice(g * pgw, (g + 1) * pgw)
            dwp_ref[g] += _dot(p_ref[:, sl], dpw_ref[:, sl], TN)
            dps.append(_dot(dpw_all[:, sl], wp_ref[g], NT))
        dp = jnp.concatenate(dps, axis=1)
        dpn = dp / _pool_counts(tm, tm + POOL_HALO, pgw, i * tm)
        du_ref[...] = (_window_sums(dpn, -1, pgw)[:tm] - dp[:tm]).astype(BF16)

    row = pl.BlockSpec((tm, pw_), lambda i: (i, 0))
    full = pl.BlockSpec((ng, pgw, pgw), lambda i: (0, 0, 0))
    return pl.pallas_call(
        body, name="pool_bwd", grid=(nblk,),
        in_specs=[row, pl.BlockSpec((POOL_HALO, pw_), lambda i: (jnp.minimum((i + 1) * hb, s // POOL_HALO - 1), 0)),
                  row, full],
        out_specs=[full, row],
        out_shape=[jax.ShapeDtypeStruct((ng, pgw, pgw), F32), jax.ShapeDtypeStruct((s, pw_), BF16)],
        compiler_params=_params(("arbitrary",)),
    )(dpw, dpw, p, wp)


def _attn_bwd(name, q, k, v, do, lse, delta, offs, cw, width):
    m = do.shape[0]
    nh = cw // HEAD_DIM
    nblk = m // SUB_BLOCK
    grid = (width // cw, nblk + 1)
    scale = HEAD_DIM ** -0.5

    def body(q_ref, k_ref, v_ref, do_ref, lse_ref, dl_ref, dq_ref, dk_ref, dv_ref,
             kprev, vprev, dk_carry, dv_carry):
        n = pl.program_id(1)

        @pl.when(n == 0)
        def _():
            for ref in (kprev, vprev, dk_carry, dv_carry):
                ref[...] = jnp.zeros_like(ref)

        qi = lax.broadcasted_iota(jnp.int32, (SUB_BLOCK, 2 * SUB_BLOCK), 0)
        kj = lax.broadcasted_iota(jnp.int32, (SUB_BLOCK, 2 * SUB_BLOCK), 1)
        first_key = jnp.where(n == 0, SUB_BLOCK, jnp.where(n == nblk, 4 * SUB_BLOCK, 0))
        valid = (kj >= qi) & (kj <= qi + SUB_BLOCK) & (kj >= first_key)
        heads = [slice(hh * HEAD_DIM, (hh + 1) * HEAD_DIM) for hh in range(nh)]
        kcats = [jnp.concatenate([kprev[:, sl], k_ref[:, sl]], axis=0) for sl in heads]
        scs = [_dot(q_ref[:, sl], kcats[hh], NT) for hh, sl in enumerate(heads)]
        dps = [_dot(do_ref[:, sl], jnp.concatenate([vprev[:, sl], v_ref[:, sl]], axis=0), NT)
               for sl in heads]
        dqs = []
        lse_all, dl_all = lse_ref[...], dl_ref[...]
        for hh, sl in enumerate(heads):
            lse_h, dl_h = _head_col(lse_all, hh, nh), _head_col(dl_all, hh, nh)
            p = jnp.where(valid, jnp.exp(jnp.where(valid, scs[hh] * scale, NEG) - lse_h), 0.0)
            ds = (p * (dps[hh] - dl_h)).astype(BF16)
            dqs.append((_dot(ds, kcats[hh], NN) * scale).astype(BF16))
            dk2 = _dot(ds, q_ref[:, sl], TN) * scale
            dv2 = _dot(p.astype(BF16), do_ref[:, sl], TN)
            dk_ref[:, sl] = (dk_carry[:, sl] + dk2[:SUB_BLOCK]).astype(BF16)
            dv_ref[:, sl] = (dv_carry[:, sl] + dv2[:SUB_BLOCK]).astype(BF16)
            dk_carry[:, sl] = dk2[SUB_BLOCK:]
            dv_carry[:, sl] = dv2[SUB_BLOCK:]

        @pl.when(n < nblk)
        def _():
            for hh, sl in enumerate(heads):
                dq_ref[:, sl] = dqs[hh]

        kprev[...] = k_ref[...]
        vprev[...] = v_ref[...]

    def cur(off):
        return pl.BlockSpec((SUB_BLOCK, cw), lambda j, n: (jnp.minimum(n, nblk - 1), j + off))

    lagged = pl.BlockSpec((SUB_BLOCK, cw), lambda j, n: (jnp.maximum(n - 1, 0), j))
    stat = pl.BlockSpec((SUB_BLOCK, HEAD_DIM), lambda j, n: (jnp.minimum(n, nblk - 1), j))
    out = jax.ShapeDtypeStruct((m, width), BF16)
    return pl.pallas_call(
        body, name=name, grid=grid,
        in_specs=[cur(offs[0]), cur(offs[1]), cur(offs[2]), cur(0), stat, stat],
        out_specs=[cur(0), lagged, lagged], out_shape=[out, out, out],
        scratch_shapes=[pltpu.VMEM((SUB_BLOCK, cw), BF16)] * 2 + [pltpu.VMEM((SUB_BLOCK, cw), F32)] * 2,
        compiler_params=_params(("parallel", "arbitrary")),
    )(q, k, v, do, lse, delta)


def _qkvu_grad(d1, d4, d16, du, cos_t, sin_t):
    s, aw = du.shape
    tm = min(s, 512)

    def body(*refs):
        nat, v4, v16 = refs[0:3], refs[3:6], refs[6:9]
        cos_ref, sin_ref, du_ref, out_ref, s4, s16 = refs[9:]
        cos, sin = cos_ref[...], sin_ref[...]
        for part in range(3):
            _from_views((v4[part], v16[part]), (s4, s16))
            for hh in range(aw // HEAD_DIM):
                sl = slice(hh * HEAD_DIM, (hh + 1) * HEAD_DIM)
                t = nat[part][:, sl].astype(F32) + s4[hh] + s16[hh]
                if part < 2:
                    t = t * cos - pltpu.roll(t, HEAD_DIM // 2, 1) * sin
                out_ref[:, part * aw + hh * HEAD_DIM:part * aw + (hh + 1) * HEAD_DIM] = t.astype(BF16)
        out_ref[:, 3 * aw:] = du_ref[...]

    row = pl.BlockSpec((tm, aw), lambda i: (i, 0))
    tab = pl.BlockSpec((tm, HEAD_DIM), lambda i: (i, 0))
    return pl.pallas_call(
        body, name="qkvu_grad", grid=(s // tm,),
        in_specs=[row] * 3 + [_view_spec(tm, aw, 4)] * 3 + [_view_spec(tm, aw, 16)] * 3 + [tab, tab, row],
        out_specs=pl.BlockSpec((tm, 4 * aw), lambda i: (i, 0)),
        out_shape=jax.ShapeDtypeStruct((s, 4 * aw), BF16),
        scratch_shapes=[_heads_scratch(tm, aw)] * 2,
        compiler_params=_params(("parallel",)),
    )(*d1, *d4, *d16, cos_t, sin_t, du)


def _in_proj_bwd_x(name, dh, w_in_st, shard0, base, scale_base, run_after=()):
    s, kdim = dh.shape
    d, n_sh = w_in_st.shape[1], w_in_st.shape[2]
    tm, tk = min(s, 512), min(n_sh, 2048)
    per = n_sh // tk

    ch = min(tm, 2 * EPILOGUE_ROWS)

    def epilogue(acc_ref, ex, outs):
        for c in range(tm // ch):
            rows = slice(c * ch, (c + 1) * ch)
            outs[0][rows, :] = scale_base * ex[0][rows, :] + acc_ref[rows, :]

    row = pl.BlockSpec((tm, d), lambda i, j, k: (i, 0))
    return _mm(
        name, (s // tm, 1, kdim // tk), dh, pl.BlockSpec((tm, tk), lambda i, j, k: (i, k)),
        w_in_st, pl.BlockSpec((None, d, tk), lambda i, j, k: (shard0 + k // per, 0, k % per)), NT,
        [base], [row], [jax.ShapeDtypeStruct((s, d), F32)], [row], epilogue, (tm, d),
        acc_as_ref=True, run_after=run_after)[0]


def _chip_peers():
    x, y, c = lax.axis_index("x"), lax.axis_index("y"), lax.axis_index("c")
    return x, y, c, [(1 - x, y), (x, 1 - y), (1 - x, 1 - y)]


GATHER, GATHER_HALF, SCATTER, SIBLING = "gather", "gather_half", "scatter", "sibling"


def _exchange_peers(mode):
    x, y, c, chips = _chip_peers()
    if mode == SIBLING:
        return x, y, c, [(x, y, 1 - c)]
    return x, y, c, [(px, py, c) for px, py in chips]


def _core_half(ref_or_shape, c):
    rows = (ref_or_shape.shape[0]) // 2
    return pl.ds(c * rows, rows)


def _exchange_descriptor(mode, src, land, send, recv, p, peer, me, arriving):
    pid = 2 * peer[0] + peer[1]
    if mode == GATHER:
        src_ref, dst_ref = src, land.at[pid if arriving else me]
    elif mode == GATHER_HALF:
        rows = _core_half(src, peer[2])
        src_ref, dst_ref = src.at[rows], land.at[pid if arriving else me, rows]
    elif mode == SCATTER:
        src_ref, dst_ref = src.at[pid], land.at[p]
    else:
        src_ref, dst_ref = src, land
    return pltpu.make_async_remote_copy(
        src_ref=src_ref, dst_ref=dst_ref, send_sem=send.at[p], recv_sem=recv.at[p],
        device_id=peer, device_id_type=MESH)


def _exchange_start(name, mode, srcs, land_shapes):
    n = len(srcs)
    lands = [_hbm(lax.empty(shape, src.dtype)) for shape, src in zip(land_shapes, srcs)]

    def body(*refs):
        src_refs, land_refs = refs[:n], refs[n:2 * n]
        sends, recvs = refs[2 * n:3 * n], refs[3 * n:4 * n]
        token = refs[6 * n]
        x, y, c, peers = _exchange_peers(mode)
        me = 2 * x + y
        for w in range(n):
            for p, peer in enumerate(peers):
                _exchange_descriptor(mode, src_refs[w], land_refs[w], sends[w], recvs[w], p, peer,
                                     me, arriving=False).start()
        token[...] = jnp.zeros_like(token)

    sem = pltpu.SemaphoreType.DMA((3,))
    outs = pl.pallas_call(
        body, name=name, in_specs=[HBM_SPEC] * (2 * n),
        out_specs=[SEM_SPEC] * (2 * n) + [HBM_SPEC] * (2 * n) + [pl.BlockSpec(memory_space=pltpu.VMEM)],
        out_shape=[sem] * (2 * n) + [pltpu.HBM(a.shape, a.dtype) for a in (*srcs, *lands)]
        + [jax.ShapeDtypeStruct((8, 128), F32)],
        input_output_aliases={i: 2 * n + i for i in range(2 * n)},
        compiler_params=pltpu.CompilerParams(has_side_effects=DATAFLOW),
    )(*[_hbm(a) for a in srcs], *lands)
    return {"send": outs[:n], "recv": outs[n:2 * n], "src": outs[2 * n:3 * n],
            "land": outs[3 * n:4 * n], "token": outs[4 * n]}


def _exchange_wait(name, mode, started, which, after):
    m = len(which)

    def body(*refs):
        src_refs, land_refs = refs[:m], refs[m:2 * m]
        sends, recvs = refs[2 * m:3 * m], refs[3 * m:4 * m]
        x, y, c, peers = _exchange_peers(mode)
        me = 2 * x + y
        for w in range(m):
            for p, peer in enumerate(peers):
                _exchange_descriptor(mode, src_refs[w], land_refs[w], sends[w], recvs[w], p, peer,
                                     me, arriving=False).wait_send()
                _exchange_descriptor(mode, src_refs[w], land_refs[w], sends[w], recvs[w], p, peer,
                                     me, arriving=True).wait_recv()

    pick = lambda key: [started[key][w] for w in which]
    bufs = pick("src") + pick("land")
    after = list(after) if isinstance(after, (list, tuple)) else [after]
    outs = pl.pallas_call(
        body, name=name,
        in_specs=[HBM_SPEC] * (2 * m) + [SEM_SPEC] * (2 * m) + [ANY_SPEC] * len(after),
        out_specs=[HBM_SPEC] * (2 * m), out_shape=[pltpu.HBM(a.shape, a.dtype) for a in bufs],
        input_output_aliases={i: i for i in range(2 * m)},
        compiler_params=pltpu.CompilerParams(has_side_effects=DATAFLOW),
    )(*bufs, *pick("send"), *pick("recv"), *after)
    return outs[:m], outs[m:]


def _to_bf16(name, a, run_after):
    r, c = a.shape
    tm = min(r, 512)

    def body(a_ref, after_ref, out_ref):
        out_ref[...] = a_ref[...].astype(BF16)

    blk = pl.BlockSpec((tm, c), lambda i: (i, 0))
    return pl.pallas_call(
        body, name=name, grid=(r // tm,), in_specs=[blk, ANY_SPEC], out_specs=blk,
        out_shape=jax.ShapeDtypeStruct((r, c), BF16), compiler_params=_params(("parallel",)),
    )(a, run_after)


def _swap_halves(name, lands):
    n = len(lands)

    def body(*refs):
        bufs = refs[n:2 * n]
        send, recv = refs[2 * n:]
        x, y, c, chips = _chip_peers()
        started = []
        for w in range(n):
            half = bufs[w].shape[1] // 2
            for p, (px, py) in enumerate(chips):
                mine = bufs[w].at[2 * px + py, pl.ds(c * half, half)]
                cp = pltpu.make_async_remote_copy(
                    src_ref=mine, dst_ref=mine, send_sem=send.at[w, p], recv_sem=recv.at[w, p],
                    device_id=(x, y, 1 - c), device_id_type=MESH)
                cp.start()
                started.append(cp)
        for w in range(n):
            half = bufs[w].shape[1] // 2
            for p, (px, py) in enumerate(chips):
                theirs = bufs[w].at[2 * px + py, pl.ds((1 - c) * half, half)]
                pltpu.make_async_remote_copy(
                    src_ref=theirs, dst_ref=theirs, send_sem=send.at[w, p], recv_sem=recv.at[w, p],
                    device_id=(x, y, 1 - c), device_id_type=MESH).wait_recv()
        for cp in started:
            cp.wait_send()

    return pl.pallas_call(
        body, name=name, in_specs=[ANY_SPEC] * n, out_specs=[ANY_SPEC] * n,
        out_shape=[jax.ShapeDtypeStruct(a.shape, a.dtype) for a in lands],
        input_output_aliases={i: i for i in range(n)},
        scratch_shapes=[pltpu.SemaphoreType.DMA((n, 3)), pltpu.SemaphoreType.DMA((n, 3))],
    )(*lands)


def _place_own(name, shard, land, me):
    r, c = shard.shape
    tm = min(r, 512)

    def body(me_ref, shard_ref, land_ref, out_ref):
        out_ref[...] = shard_ref[...]

    return pl.pallas_call(
        body, name=name,
        grid_spec=pltpu.PrefetchScalarGridSpec(
            num_scalar_prefetch=1, grid=(r // tm,),
            in_specs=[pl.BlockSpec((tm, c), lambda i, me_ref: (i, 0)), ANY_SPEC],
            out_specs=pl.BlockSpec((None, tm, c), lambda i, me_ref: (me_ref[0], i, 0))),
        out_shape=jax.ShapeDtypeStruct(land.shape, land.dtype), input_output_aliases={2: 0},
        compiler_params=_params(("arbitrary",)),
    )(me, shard, land)


def _sum_slabs(name, grads, land, me):
    _, r, c = grads.shape
    tm = min(r, 256)

    def body(me_ref, own_ref, land_ref, out_ref):
        acc = own_ref[...].astype(F32)
        for p in range(3):
            acc = acc + land_ref[p].astype(F32)
        out_ref[...] = acc

    return pl.pallas_call(
        body, name=name,
        grid_spec=pltpu.PrefetchScalarGridSpec(
            num_scalar_prefetch=1, grid=(r // tm,),
            in_specs=[pl.BlockSpec((None, tm, c), lambda i, me_ref: (me_ref[0], i, 0)),
                      pl.BlockSpec((3, tm, c), lambda i, me_ref: (0, i, 0))],
            out_specs=pl.BlockSpec((tm, c), lambda i, me_ref: (i, 0))),
        out_shape=jax.ShapeDtypeStruct((r, c), F32), compiler_params=_params(("parallel",)),
    )(me, grads, land)


def _allreduce_stats(stats, run_after):
    n = len(stats)

    def body(*refs):
        ins, outs = refs[:n], refs[n + 1:2 * n + 1]
        mine, gath = refs[2 * n + 1:3 * n + 1], refs[3 * n + 1:4 * n + 1]
        send, recv = refs[4 * n + 1:]
        x, y, c = lax.axis_index("x"), lax.axis_index("y"), lax.axis_index("c")
        me = 4 * x + 2 * y + c
        flips = [(bx, by, bc) for bx in (0, 1) for by in (0, 1) for bc in (0, 1)][1:]

        def peer(f):
            return (x + f[0] * (1 - 2 * x), y + f[1] * (1 - 2 * y), c + f[2] * (1 - 2 * c))

        copies = []
        for t in range(n):
            tot = ins[t][0]
            for b in range(1, ins[t].shape[0]):
                tot = tot + ins[t][b]
            mine[t][...] = tot
            gath[t][me] = tot
            for k, f in enumerate(flips):
                cp = pltpu.make_async_remote_copy(
                    src_ref=mine[t], dst_ref=gath[t].at[me], send_sem=send.at[t, k],
                    recv_sem=recv.at[t, k], device_id=peer(f), device_id_type=MESH)
                cp.start()
                copies.append(cp)
        for t in range(n):
            for k, f in enumerate(flips):
                px, py, pc = peer(f)
                pltpu.make_async_remote_copy(
                    src_ref=mine[t], dst_ref=gath[t].at[4 * px + 2 * py + pc], send_sem=send.at[t, k],
                    recv_sem=recv.at[t, k], device_id=(px, py, pc), device_id_type=MESH).wait_recv()
        for cp in copies:
            cp.wait_send()
        for t in range(n):
            tot = gath[t][0]
            for dev in range(1, 8):
                tot = tot + gath[t][dev]
            outs[t][...] = tot

    vm = pl.BlockSpec(memory_space=pltpu.VMEM)
    return pl.pallas_call(
        body, name="allreduce_stats", in_specs=[vm] * n + [ANY_SPEC], out_specs=[vm] * n,
        out_shape=[jax.ShapeDtypeStruct(s.shape[1:], F32) for s in stats],
        scratch_shapes=[pltpu.VMEM(s.shape[1:], F32) for s in stats]
        + [pltpu.VMEM((8, *s.shape[1:]), F32) for s in stats]
        + [pltpu.SemaphoreType.DMA((n, 7)), pltpu.SemaphoreType.DMA((n, 7))],
    )(*stats, run_after)


def _adamw(name, w, m, v, g_parts):
    r, c = w.shape
    tm = min(r, 128)
    n_g = len(g_parts)

    def body(*refs):
        w_ref, m_ref, v_ref = refs[:3]
        g_refs = refs[3:3 + n_g]
        g_out, d_out, m_out, v_out = refs[3 + n_g:]
        g = g_refs[0][...]
        for gr in g_refs[1:]:
            g = g + gr[...]
        m_new = ADAM_B1 * m_ref[...] + (1.0 - ADAM_B1) * g
        v_new = ADAM_B2 * v_ref[...] + (1.0 - ADAM_B2) * (g * g)
        m_hat = m_new / (1.0 - ADAM_B1 ** ADAM_STEP)
        v_hat = v_new / (1.0 - ADAM_B2 ** ADAM_STEP)
        g_out[...] = g
        d_out[...] = -ADAM_LR * (m_hat / (jnp.sqrt(v_hat) + ADAM_EPS) + ADAM_WD * w_ref[...])
        m_out[...] = m_new
        v_out[...] = v_new

    blk = pl.BlockSpec((tm, c), lambda i: (i, 0))
    out = jax.ShapeDtypeStruct((r, c), F32)
    return pl.pallas_call(
        body, name=name, grid=(r // tm,), in_specs=[blk] * (3 + n_g), out_specs=[blk] * 4,
        out_shape=[out] * 4, compiler_params=_params(("parallel",)),
    )(w, m, v, *g_parts)


def _rope_tables(positions):
    half = HEAD_DIM // 2
    inv_freq = ROPE_THETA ** (-jnp.arange(half, dtype=F32) / half)
    ang = positions.astype(F32)[0, :, None] * inv_freq
    cos, sin = jnp.cos(ang), jnp.sin(ang)
    return jnp.concatenate([cos, cos], axis=-1), jnp.concatenate([-sin, sin], axis=-1)


def kernel(x, positions, w_in, w_pool, pool_scale, w_branch_attn, w_branch_pool, w_out, ln_mix_g, ln_mix_b, w_ff1, w_ff2, ln_ff_g, ln_ff_b, loss_target, m_w_in, m_w_pool, m_pool_scale, m_w_branch_attn, m_w_branch_pool, m_w_out, m_ln_mix_g, m_ln_mix_b, m_w_ff1, m_w_ff2, m_ln_ff_g, m_ln_ff_b, v_w_in, v_w_pool, v_pool_scale, v_w_branch_attn, v_w_branch_pool, v_w_out, v_ln_mix_g, v_ln_mix_b, v_w_ff1, v_w_ff2, v_ln_ff_g, v_ln_ff_b):
    s, d = x.shape[1], x.shape[2]
    aw = d // 2
    ng = len(POOL_WINDOWS)
    pgw = aw // ng
    x2d, target = x[0], loss_target[0]
    xb = x2d.astype(BF16)
    cos_t, sin_t = _rope_tables(positions)

    big = {"w_in": w_in[0], "w_pool": w_pool[0].reshape(-1, pgw), "w_branch_attn": w_branch_attn[0],
           "w_branch_pool": w_branch_pool[0], "w_out": w_out[0], "w_ff1": w_ff1[0], "w_ff2": w_ff2[0]}
    names = list(big)
    me_chip = (2 * lax.axis_index("x") + lax.axis_index("y")).astype(jnp.int32).reshape(1)
    land_shapes = [(N_CHIPS, *big[k].shape) for k in names]
    gathering_in = _exchange_start("gather_start_in", GATHER_HALF,
                                   [_to_bf16("to_bf16_w_in", big["w_in"], positions)], land_shapes[:1])
    shards = [_to_bf16(f"to_bf16_{k}", big[k], gathering_in["token"]) for k in names[1:]]
    gathering = _exchange_start("gather_start", GATHER, shards, land_shapes[1:])

    def gathered(name, which, after):
        srcs, lands = _exchange_wait(f"gather_wait_{name}", GATHER, gathering, which, after)
        return [_place_own(f"place_own_{names[w + 1]}", srcs[i], lands[i], me_chip)
                for i, w in enumerate(which)]

    rows_sh = pgw // N_CHIPS
    dff = N_CHIPS * big["w_ff2"].shape[0]

    srcs, lands = _exchange_wait("gather_wait_in", GATHER_HALF, gathering_in, [0],
                                 [gathering["token"], xb, cos_t, sin_t])
    w_in_st = _place_own("place_own_w_in", srcs[0], _swap_halves("swap_halves_in", lands)[0], me_chip)
    h, hv4, hv16 = _in_proj(xb, w_in_st, cos_t, sin_t, aw)
    (wp_st,) = gathered("pool", [0], h)
    wp = wp_st.reshape(N_CHIPS, ng, rows_sh, pgw).transpose(1, 0, 2, 3).reshape(ng, pgw, pgw)
    qkv = {1: (h, h, h), 4: (hv4, hv4, hv4), 16: (hv16, hv16, hv16)}
    offs = {dil: (0, dil, 2 * dil) for dil in DILATIONS}
    o_parts, lse_parts = [], []
    for dil in DILATIONS:
        o_p, lse_p = _attn_fwd(f"attn_fwd_d{dil}", *qkv[dil], offs[dil], aw, dil * aw)
        o_parts.append(o_p)
        lse_parts.append(lse_p)
    o_attn, lse, lse4, lse16 = _attn_combine(o_parts[0], lse_parts[0], o_parts[1:], lse_parts[1:])
    p, pw, y = _pool_fwd(h, wp, pool_scale, aw)
    wba_st, wbp_st, w_out_st = gathered("mix", [1, 2, 3], y)
    w_out_full = w_out_st.reshape(d, d)
    ya, yp, merged = _branch_merge(o_attn, y, wba_st, wbp_st, h, aw, d)
    xhat1, rstd1, x1b = _mix_norm(merged, w_out_full, x2d, ln_mix_g, ln_mix_b)
    w1_st, w2_st = gathered("ff", [4, 5], x1b)
    w2_full = w2_st.reshape(dff, d)
    r, r_slope = _ff_up(x1b, w1_st)
    dz2, dz2b, st2 = _ff_down_loss(r, w2_full, xhat1, ln_mix_g, ln_mix_b, ln_ff_g, ln_ff_b, target)

    def scatter_start(name, grads):
        return _exchange_start(f"scatter_start_{name}", SCATTER, grads, [(3, *g.shape[1:]) for g in grads])

    da = _ff_down_bwd(dz2b, w2_full, r_slope)
    g_w2 = _wgrad("wgrad_ff2", r, dz2b, d).reshape(N_CHIPS, dff // N_CHIPS, d)
    g_w1 = _wgrad("wgrad_ff1", x1b, da, dff // N_CHIPS)
    sent_ff = scatter_start("ff", [g_w1, g_w2])
    dz1, dz1b, st1 = _ff_up_bwd(da, w1_st, dz2, xhat1, rstd1, ln_mix_g, [sent_ff["token"]])
    dya, dyp, dgate = _mix_bwd(dz1b, w_out_full, h, ya, yp, aw)
    g_wout = _wgrad("wgrad_out", merged, dz1b, d).reshape(N_CHIPS, d // N_CHIPS, d)
    g_wba = _wgrad("wgrad_branch_attn", o_attn, dya, d // N_CHIPS)
    g_wbp = _wgrad("wgrad_branch_pool", y, dyp, d // N_CHIPS)
    do, delta, do4, do16, delta4, delta16 = _attn_out_bwd(dya, wba_st, o_attn)
    dpw, stp = _pool_out_bwd(dyp, wbp_st, pw, pool_scale)
    dwp, du = _pool_bwd(dpw, p, wp)
    g_wp = dwp.reshape(ng, N_CHIPS, rows_sh, pgw).transpose(1, 0, 2, 3).reshape(
        N_CHIPS, ng * rows_sh, pgw).astype(BF16)
    sent_mix = scatter_start("mix", [g_wp, g_wba, g_wbp, g_wout])

    bwd_in = {1: (do, lse, delta), 4: (do4, lse4, delta4), 16: (do16, lse16, delta16)}
    dqkv = {}
    for dil in DILATIONS:
        args = (*qkv[dil], *bwd_in[dil], offs[dil], aw, dil * aw)
        dqkv[dil] = _attn_bwd(f"attn_bwd_d{dil}", *args)
    dqkvu = _qkvu_grad(dqkv[1], dqkv[4], dqkv[16], du, cos_t, sin_t)
    g_win = _wgrad("wgrad_in_gates", xb, dgate, d, shard0=2, n_shards=N_CHIPS)
    g_win = _wgrad("wgrad_in_qkvu", xb, dqkvu, d, n_shards=N_CHIPS, fill=g_win)
    sent_in = scatter_start("in", [g_win])
    dx_a = _in_proj_bwd_x("in_proj_bwd_qkvu", dqkvu, w_in_st, 0, dz1, ALPHA,
                          [sent_mix["token"], sent_in["token"]])
    grad_x = _in_proj_bwd_x("in_proj_bwd_gates", dgate, w_in_st, 2, dx_a, 1.0)

    moments = {"w_in": (m_w_in, v_w_in), "w_pool": (m_w_pool, v_w_pool),
               "w_branch_attn": (m_w_branch_attn, v_w_branch_attn),
               "w_branch_pool": (m_w_branch_pool, v_w_branch_pool), "w_out": (m_w_out, v_w_out),
               "w_ff1": (m_w_ff1, v_w_ff1), "w_ff2": (m_w_ff2, v_w_ff2)}
    originals = {"w_in": w_in, "w_pool": w_pool, "w_branch_attn": w_branch_attn,
                 "w_branch_pool": w_branch_pool, "w_out": w_out, "w_ff1": w_ff1, "w_ff2": w_ff2}
    res = {}

    def summed(name, sent, keys, after):
        srcs, lands = _exchange_wait(f"scatter_wait_{name}", SCATTER, sent, list(range(len(keys))), after)
        parts = [_sum_slabs(f"sum_slabs_{k}", srcs[i], lands[i], me_chip) for i, k in enumerate(keys)]
        return _exchange_start(f"cores_start_{name}", SIBLING, parts, [a.shape for a in parts])

    def updated(name, swapping, keys, after):
        mine, other = _exchange_wait(f"cores_wait_{name}", SIBLING, swapping, list(range(len(keys))), after)
        for i, k in enumerate(keys):
            mk, vk = (a.reshape(big[k].shape) for a in moments[k])
            outs = _adamw(f"adamw_{k}", big[k], mk, vk, [mine[i], other[i]])
            res[k] = [o.reshape(originals[k].shape) for o in outs]

    groups = {"ff": ["w_ff1", "w_ff2"], "mix": ["w_pool", "w_branch_attn", "w_branch_pool", "w_out"],
              "in": ["w_in"]}
    swap_ff = summed("ff", sent_ff, groups["ff"], grad_x)
    swap_mix = summed("mix", sent_mix, groups["mix"], swap_ff["token"])
    swap_in = summed("in", sent_in, groups["in"], swap_mix["token"])
    updated("ff", swap_ff, groups["ff"], swap_in["token"])
    updated("mix", swap_mix, groups["mix"], res["w_ff2"][0])
    updated("in", swap_in, groups["in"], res["w_out"][0])
    tot2, tot1, totp = _allreduce_stats([st2, st1, stp], res["w_in"][0])

    def pad_d(a):
        return jnp.pad(a, ((0, 0), (0, d - a.shape[1])))

    small = ["ln_mix_g", "ln_mix_b", "ln_ff_g", "ln_ff_b", "pool_scale"]
    small_w = {"ln_mix_g": ln_mix_g, "ln_mix_b": ln_mix_b, "ln_ff_g": ln_ff_g, "ln_ff_b": ln_ff_b,
               "pool_scale": pool_scale}
    small_m = {"ln_mix_g": m_ln_mix_g, "ln_mix_b": m_ln_mix_b, "ln_ff_g": m_ln_ff_g,
               "ln_ff_b": m_ln_ff_b, "pool_scale": m_pool_scale}
    small_v = {"ln_mix_g": v_ln_mix_g, "ln_mix_b": v_ln_mix_b, "ln_ff_g": v_ln_ff_g,
               "ln_ff_b": v_ln_ff_b, "pool_scale": v_pool_scale}
    small_g = [tot1[0:1], tot1[1:2], tot2[0:1], tot2[1:2], pad_d(totp[0:1])]

    def pack(rows):
        return jnp.concatenate([pad_d(a) for a in rows] + [jnp.zeros((8 - len(rows), d), F32)], axis=0)

    outs = _adamw("adamw_small", pack([small_w[k] for k in small]), pack([small_m[k] for k in small]),
                  pack([small_v[k] for k in small]), [pack(small_g)])
    for i, k in enumerate(small):
        res[k] = [o[i:i + 1, :small_w[k].shape[1]] for o in outs]
    loss = tot2[2, 0]

    order = ["w_in", "w_pool", "pool_scale", "w_branch_attn", "w_branch_pool", "w_out", "ln_mix_g",
             "ln_mix_b", "w_ff1", "w_ff2", "ln_ff_g", "ln_ff_b"]
    result = [loss, grad_x[None]]
    for idx in range(4):
        result += [res[k][idx] for k in order]
    return tuple(result)
```

```python
import jax
import jax.numpy as jnp
from jax import lax
from jax.experimental import pallas as pl
from jax.experimental.pallas import tpu as pltpu

F32 = jnp.float32
BF16 = jnp.bfloat16
MESH = pl.DeviceIdType.MESH

HEAD_DIM = 128
SUB_BLOCK = 128
ATTN_STEP_ROWS = 256
DILATIONS = (1, 4, 16)
POOL_WINDOWS = (2, 4, 8, 16)
POOL_HALO = 16
ROPE_THETA = 10000.0
LN_EPS = 1e-5
ALPHA = 2.0 ** 0.25
ADAM_LR, ADAM_B1, ADAM_B2, ADAM_EPS, ADAM_WD, ADAM_STEP = 0.001, 0.9, 0.999, 1e-08, 0.01, 10
NEG = -1e30
N_CHIPS = 4
VMEM_LIMIT = 62 * 1024 * 1024
EPILOGUE_ROWS = 128

def _params(sem=None, vmem=VMEM_LIMIT):
    kw = {"vmem_limit_bytes": vmem}
    if sem is not None:
        kw["dimension_semantics"] = sem
    return pltpu.CompilerParams(**kw)


def _dot(a, b, contract):
    return lax.dot_general(a, b, (contract, ((), ())), preferred_element_type=F32)


ANY_SPEC = pl.BlockSpec(memory_space=pl.ANY)
HBM_SPEC = pl.BlockSpec(memory_space=pltpu.HBM)
SEM_SPEC = pl.BlockSpec(memory_space=pltpu.SEMAPHORE)
DATAFLOW = pltpu.SideEffectType.DATAFLOW_SIDE_EFFECTING


def _hbm(a):
    return pltpu.with_memory_space_constraint(a, pltpu.HBM)


NN = ((1,), (0,))
NT = ((1,), (1,))
TN = ((0,), (0,))


def _mm(name, grid, a, a_spec, b, b_spec, contract, extras, extra_specs, out_shape, out_specs,
        epilogue, acc_shape, acc_as_ref=False, run_after=(), scratch=(),
        semantics=("parallel", "parallel", "arbitrary"), fill=None):
    nk = grid[2]
    n_ex = len(extras)
    n_in = 2 + n_ex + len(run_after) + (fill is not None)
    n_out = len(out_shape)
    n_scr = len(scratch)

    def body(*refs):
        a_ref, b_ref = refs[0], refs[1]
        ex = refs[2:2 + n_ex]
        outs = refs[n_in:n_in + n_out]
        scr = refs[n_in + n_out:n_in + n_out + n_scr]
        if nk == 1:
            epilogue(_dot(a_ref[...], b_ref[...], contract), ex, outs, *scr)
        else:
            acc = refs[n_in + n_out + n_scr]
            k = pl.program_id(2)

            @pl.when(k == 0)
            def _():
                acc[...] = jnp.zeros_like(acc)

            acc[...] += _dot(a_ref[...], b_ref[...], contract)

            @pl.when(k == nk - 1)
            def _():
                epilogue(acc if acc_as_ref else acc[...], ex, outs, *scr)

    acc_scratch = [pltpu.VMEM(acc_shape, F32)] if nk > 1 else []
    filled = [] if fill is None else [fill]
    return pl.pallas_call(
        body, name=name, grid=grid,
        in_specs=[a_spec, b_spec, *extra_specs, *[ANY_SPEC] * (len(run_after) + len(filled))],
        out_specs=out_specs, out_shape=out_shape, scratch_shapes=[*scratch, *acc_scratch],
        input_output_aliases={} if fill is None else {n_in - 1: 0},
        compiler_params=_params(semantics),
    )(a, b, *extras, *run_after, *filled)


def _stats_rows(rows, width):
    idx = lax.broadcasted_iota(jnp.int32, (8, width), 0)
    out = jnp.zeros((8, width), F32)
    for r, v in enumerate(rows):
        out = jnp.where(idx == r, jnp.broadcast_to(v, (8, width)), out)
    return out


def _layer_norm_fwd(z):
    mu = jnp.mean(z, axis=-1, keepdims=True)
    zc = z - mu
    var = jnp.mean(zc * zc, axis=-1, keepdims=True)
    rstd = lax.rsqrt(var + LN_EPS)
    return zc * rstd, rstd


def _layer_norm_bwd(dy, xhat, rstd, g):
    dxh = dy * g
    m1 = jnp.mean(dxh, axis=-1, keepdims=True)
    m2 = jnp.mean(dxh * xhat, axis=-1, keepdims=True)
    return rstd * (dxh - m1 - xhat * m2)


def _heads_scratch(rows, width):
    return pltpu.VMEM((width // HEAD_DIM, rows, HEAD_DIM), F32)


def _to_views(src_ref, view_refs, dtype, heads=None):
    nh, rows, _ = src_ref.shape
    width = nh * HEAD_DIM
    for dil, view_ref in zip(DILATIONS[1:], view_refs):
        for r in range(dil):
            for hh in (range(nh) if heads is None else heads):
                c0 = r * width + hh * HEAD_DIM
                view_ref[:, c0:c0 + HEAD_DIM] = (
                    src_ref[hh, pl.ds(r, rows // dil, stride=dil), :].astype(dtype))


def _from_views(view_refs, dst_refs):
    nh, rows, _ = dst_refs[0].shape
    width = nh * HEAD_DIM
    for dil, view_ref, dst_ref in zip(DILATIONS[1:], view_refs, dst_refs):
        for r in range(dil):
            for hh in range(nh):
                c0 = r * width + hh * HEAD_DIM
                dst_ref[hh, pl.ds(r, rows // dil, stride=dil), :] = (
                    view_ref[:, c0:c0 + HEAD_DIM].astype(F32))


def _view_shape(rows, width, dil, parts=1):
    return (rows // dil, parts * dil * width)


def _in_proj(xb, w_in_st, cos_t, sin_t, aw):
    s, d = xb.shape
    n_sh = w_in_st.shape[2]
    tm, tn = min(s, 1024), aw
    per = n_sh // tn
    grid = (s // tm, (N_CHIPS * n_sh) // tn, 1)

    def epilogue(acc, ex, outs, scr):
        cos_ref, sin_ref = ex
        h_ref, v4_ref, v16_ref = outs
        seg = pl.program_id(1)

        heads = [slice(hh * HEAD_DIM, (hh + 1) * HEAD_DIM) for hh in range(tn // HEAD_DIM)]

        @pl.when(seg < 2)
        def _():
            cos, sin = cos_ref[...], sin_ref[...]
            for hh, sl in enumerate(heads):
                t = acc[:, sl]
                scr[hh] = t * cos + pltpu.roll(t, HEAD_DIM // 2, 1) * sin

        @pl.when(seg == 2)
        def _():
            for hh, sl in enumerate(heads):
                scr[hh] = acc[:, sl]

        @pl.when(seg < 3)
        def _():
            for hh, sl in enumerate(heads):
                h_ref[:, sl] = scr[hh].astype(BF16)
            _to_views(scr, (v4_ref, v16_ref), BF16)

        @pl.when(seg == 3)
        def _():
            h_ref[...] = acc.astype(BF16)

        @pl.when(seg >= 4)
        def _():
            h_ref[...] = (0.5 * jnp.tanh(0.5 * acc) + 0.5).astype(BF16)

    def view_spec(dil):
        return pl.BlockSpec((tm // dil, dil * aw), lambda i, j, k: (i, jnp.minimum(j, 2)))

    return _mm(
        "in_proj", grid, xb, pl.BlockSpec((tm, d), lambda i, j, k: (i, 0)),
        w_in_st, pl.BlockSpec((None, d, tn), lambda i, j, k: (j // per, 0, j % per)), NN,
        [cos_t, sin_t], [pl.BlockSpec((tm, HEAD_DIM), lambda i, j, k: (i, 0))] * 2,
        [jax.ShapeDtypeStruct((s, N_CHIPS * n_sh), BF16)]
        + [jax.ShapeDtypeStruct(_view_shape(s, aw, dil, 3), BF16) for dil in DILATIONS[1:]],
        [pl.BlockSpec((tm, tn), lambda i, j, k: (i, j))] + [view_spec(dil) for dil in DILATIONS[1:]],
        epilogue, None, scratch=[_heads_scratch(tm, aw)],
        semantics=("parallel", "arbitrary", "arbitrary"))


def _pack_heads(cols):
    rows, rep = cols[0].shape[0], HEAD_DIM // len(cols)
    lane = lax.broadcasted_iota(jnp.int32, (rows, HEAD_DIM), 1)
    out = jnp.zeros((rows, HEAD_DIM), F32)
    for hh, col in enumerate(cols):
        out = jnp.where((lane >= hh * rep) & (lane < (hh + 1) * rep), col, out)
    return out


def _head_col(packed, hh, nh):
    lane = lax.broadcasted_iota(jnp.int32, packed.shape, 1)
    return jnp.sum(jnp.where(lane == hh * (HEAD_DIM // nh), packed, 0.0), axis=-1, keepdims=True)


def _band_masks(block_idx):
    qi = lax.broadcasted_iota(jnp.int32, (SUB_BLOCK, 2 * SUB_BLOCK), 0)
    kj = lax.broadcasted_iota(jnp.int32, (SUB_BLOCK, 2 * SUB_BLOCK), 1)
    first_key = jnp.where(block_idx > 0, 0, SUB_BLOCK)
    return (kj >= qi) & (kj <= qi + SUB_BLOCK) & (kj >= first_key)


def _attn_fwd(name, q, k, v, offs, cw, width):
    m = q.shape[0]
    nh = cw // HEAD_DIM
    rb = min(m, ATTN_STEP_ROWS)
    nsub = rb // SUB_BLOCK
    grid = (width // cw, m // rb)
    scale = HEAD_DIM ** -0.5

    def body(q_ref, k_ref, v_ref, o_ref, lse_ref, kprev, vprev):
        n = pl.program_id(1)

        @pl.when(n == 0)
        def _():
            kprev[...] = jnp.zeros_like(kprev)
            vprev[...] = jnp.zeros_like(vprev)

        heads = [slice(hh * HEAD_DIM, (hh + 1) * HEAD_DIM) for hh in range(nh)]
        valids = [_band_masks(n)] + [_band_masks(1)] * (nsub - 1)

        def rows(sub):
            return slice(sub * SUB_BLOCK, (sub + 1) * SUB_BLOCK)

        def cat(prev, ref, sub, sl):
            if sub == 0:
                return jnp.concatenate([prev[:, sl], ref[rows(0), sl]], axis=0)
            return ref[(sub - 1) * SUB_BLOCK:(sub + 1) * SUB_BLOCK, sl]

        scs = [[_dot(q_ref[rows(sub), sl], cat(kprev, k_ref, sub, sl), NT) for sl in heads]
               for sub in range(nsub)]
        for sub in range(nsub):
            lses = []
            for hh, sl in enumerate(heads):
                sc = jnp.where(valids[sub], scs[sub][hh] * scale, NEG)
                mx = jnp.max(sc, axis=-1, keepdims=True)
                p = jnp.exp(sc - mx)
                l = jnp.sum(p, axis=-1, keepdims=True)
                o = _dot(p.astype(BF16), cat(vprev, v_ref, sub, sl), NN) / l
                o_ref[rows(sub), sl] = o.astype(BF16)
                lses.append(mx + jnp.log(l))
            lse_ref[rows(sub), :] = _pack_heads(lses)
        kprev[...] = k_ref[rows(nsub - 1), :]
        vprev[...] = v_ref[rows(nsub - 1), :]

    def cur(off):
        return pl.BlockSpec((rb, cw), lambda j, n: (n, j + off))

    return pl.pallas_call(
        body, name=name, grid=grid,
        in_specs=[cur(offs[0]), cur(offs[1]), cur(offs[2])],
        out_specs=[cur(0), pl.BlockSpec((rb, HEAD_DIM), lambda j, n: (n, j))],
        out_shape=[jax.ShapeDtypeStruct((m, width), BF16),
                   jax.ShapeDtypeStruct((m, width // cw * HEAD_DIM), F32)],
        scratch_shapes=[pltpu.VMEM((SUB_BLOCK, cw), BF16)] * 2,
        compiler_params=_params(("parallel", "arbitrary")),
    )(q, k, v)


def _view_spec(tm, aw, dil):
    return pl.BlockSpec((tm // dil, dil * aw), lambda i: (i, 0))


def _attn_combine(o1, l1, o_views, l_views):
    s, aw = o1.shape
    nh = aw // HEAD_DIM
    tm = min(s, 512)

    def body(o1_ref, l1_ref, o4_ref, o16_ref, l4_ref, l16_ref, o_ref, lse_ref, lse4_ref, lse16_ref,
             so4, so16, sl4, sl16, stot):
        _from_views((o4_ref, o16_ref), (so4, so16))
        _from_views((l4_ref, l16_ref), (sl4, sl16))
        a, b, c = l1_ref[...], sl4[0], sl16[0]
        mx = jnp.maximum(jnp.maximum(a, b), c)
        ea, eb, ec = jnp.exp(a - mx), jnp.exp(b - mx), jnp.exp(c - mx)
        tot = ea + eb + ec
        inv = 1.0 / tot
        wa, wb, wc = ea * inv, eb * inv, ec * inv
        lse_tot = mx + jnp.log(tot)
        stot[0] = lse_tot
        lse_ref[...] = lse_tot
        _to_views(stot, (lse4_ref, lse16_ref), F32)
        for hh in range(nh):
            sl = slice(hh * HEAD_DIM, (hh + 1) * HEAD_DIM)
            o = (_head_col(wa, hh, nh) * o1_ref[:, sl].astype(F32) + _head_col(wb, hh, nh) * so4[hh]
                 + _head_col(wc, hh, nh) * so16[hh])
            o_ref[:, sl] = o.astype(BF16)

    row = pl.BlockSpec((tm, aw), lambda i: (i, 0))
    stat = pl.BlockSpec((tm, HEAD_DIM), lambda i: (i, 0))
    views = [_view_spec(tm, aw, dil) for dil in DILATIONS[1:]]
    stat_views = [_view_spec(tm, HEAD_DIM, dil) for dil in DILATIONS[1:]]
    return pl.pallas_call(
        body, name="attn_combine", grid=(s // tm,), in_specs=[row, stat, *views, *stat_views],
        out_specs=[row, stat, *stat_views],
        out_shape=[jax.ShapeDtypeStruct((s, aw), BF16), jax.ShapeDtypeStruct((s, HEAD_DIM), F32)]
        + [jax.ShapeDtypeStruct(_view_shape(s, HEAD_DIM, dil), F32) for dil in DILATIONS[1:]],
        scratch_shapes=[_heads_scratch(tm, aw)] * 2 + [_heads_scratch(tm, HEAD_DIM)] * 3,
        compiler_params=_params(("parallel",)),
    )(o1, l1, *o_views, *l_views)


def _pool_counts(tm, rows, pgw, row0):
    t = lax.broadcasted_iota(jnp.int32, (rows, len(POOL_WINDOWS) * pgw), 0) + row0
    col = lax.broadcasted_iota(jnp.int32, (rows, len(POOL_WINDOWS) * pgw), 1)
    w = jnp.full((rows, len(POOL_WINDOWS) * pgw), POOL_WINDOWS[0], jnp.int32)
    for g in range(1, len(POOL_WINDOWS)):
        w = jnp.where(col >= g * pgw, POOL_WINDOWS[g], w)
    return jnp.minimum(t + 1, w).astype(F32)


def _window_sums(xs, direction, pgw):
    rows = xs.shape[0]
    acc = xs
    out = None
    col = lax.broadcasted_iota(jnp.int32, xs.shape, 1)
    for g, w in enumerate(POOL_WINDOWS):
        sh = w // 2
        acc = acc + pltpu.roll(acc, sh if direction > 0 else rows - sh, 0)
        out = acc if out is None else jnp.where(col >= g * pgw, acc, out)
    return out


def _pool_fwd(h, wp, scale, aw):
    s = h.shape[0]
    pw_ = aw
    pgw = pw_ // len(POOL_WINDOWS)
    tm = min(s, 512)
    hb = tm // POOL_HALO

    def body(u_ref, halo_ref, wp_ref, sc_ref, p_ref, pw_ref, y_ref):
        i = pl.program_id(0)
        u = u_ref[...].astype(F32)
        halo = halo_ref[...].astype(F32) * jnp.where(i > 0, 1.0, 0.0)
        xs = jnp.concatenate([halo, u], axis=0)
        sums = _window_sums(xs, +1, pgw)[POOL_HALO:]
        p = (sums / _pool_counts(tm, tm, pgw, i * tm) - u).astype(BF16)
        p_ref[...] = p
        sc = sc_ref[...]
        for g in range(len(POOL_WINDOWS)):
            sl = slice(g * pgw, (g + 1) * pgw)
            pw = _dot(p[:, sl], wp_ref[g], NN)
            pw_ref[:, sl] = pw.astype(BF16)
            y_ref[:, sl] = (pw * sc[:, sl]).astype(BF16)

    out = jax.ShapeDtypeStruct((s, pw_), BF16)
    row = pl.BlockSpec((tm, pw_), lambda i: (i, 0))
    return pl.pallas_call(
        body, name="pool_fwd", grid=(s // tm,),
        in_specs=[pl.BlockSpec((tm, pw_), lambda i: (i, 3)),
                  pl.BlockSpec((POOL_HALO, pw_), lambda i: (jnp.maximum(i * hb - 1, 0), 3)),
                  pl.BlockSpec(wp.shape, lambda i: (0, 0, 0)),
                  pl.BlockSpec((1, pw_), lambda i: (0, 0))],
        out_specs=[row, row, row], out_shape=[out, out, out],
        compiler_params=_params(("parallel",)),
    )(h, h, wp, scale)


def _branch_merge(o_attn, y, wba_st, wbp_st, h, aw, d):
    s = o_attn.shape[0]
    tn = wba_st.shape[2]
    tm = min(s, 1024)
    ga0 = 4 * aw // tn
    gp0 = (4 * aw + d) // tn

    def body(o_ref, y_ref, wa_ref, wp_ref, sga_ref, sgp_ref, ya_ref, yp_ref, mg_ref):
        ya = _dot(o_ref[...], wa_ref[...], NN)
        yp = _dot(y_ref[...], wp_ref[...], NN)
        ya_ref[...] = ya.astype(BF16)
        yp_ref[...] = yp.astype(BF16)
        mg_ref[...] = (sga_ref[...].astype(F32) * ya + sgp_ref[...].astype(F32) * yp).astype(BF16)

    out = jax.ShapeDtypeStruct((s, d), BF16)
    blk = pl.BlockSpec((tm, tn), lambda i, j: (i, j))
    return pl.pallas_call(
        body, name="branch_merge", grid=(s // tm, N_CHIPS),
        in_specs=[pl.BlockSpec((tm, aw), lambda i, j: (i, 0)),
                  pl.BlockSpec((tm, aw), lambda i, j: (i, 0)),
                  pl.BlockSpec((None, aw, tn), lambda i, j: (j, 0, 0)),
                  pl.BlockSpec((None, aw, tn), lambda i, j: (j, 0, 0)),
                  pl.BlockSpec((tm, tn), lambda i, j: (i, j + ga0)),
                  pl.BlockSpec((tm, tn), lambda i, j: (i, j + gp0))],
        out_specs=[blk, blk, blk], out_shape=[out, out, out],
        compiler_params=_params(("parallel", "parallel")),
    )(o_attn, y, wba_st, wbp_st, h, h)


def _mix_norm(merged, w_out, x, g1, b1):
    s, d = x.shape
    tm = min(s, 256)

    def epilogue(acc, ex, outs):
        x_ref, g_ref, b_ref = ex
        xh_ref, rs_ref, xb_ref = outs
        xhat, rstd = _layer_norm_fwd(ALPHA * x_ref[...] + acc)
        xh_ref[...] = xhat
        rs_ref[...] = rstd
        xb_ref[...] = (xhat * g_ref[...] + b_ref[...]).astype(BF16)

    row = pl.BlockSpec((tm, d), lambda i, j, k: (i, 0))
    vec = pl.BlockSpec((1, d), lambda i, j, k: (0, 0))
    return _mm(
        "mix_norm", (s // tm, 1, 1), merged, row, w_out, pl.BlockSpec((d, d), lambda i, j, k: (0, 0)),
        NN, [x, g1, b1], [row, vec, vec],
        [jax.ShapeDtypeStruct((s, d), F32), jax.ShapeDtypeStruct((s, 1), F32),
         jax.ShapeDtypeStruct((s, d), BF16)],
        [row, pl.BlockSpec((tm, 1), lambda i, j, k: (i, 0)), row], epilogue, None)


def _ff_up(x1b, w1_st):
    s, d = x1b.shape
    n_sh = w1_st.shape[2]
    tm, tn = min(s, 1024), min(n_sh, 1024)
    per = n_sh // tn

    def epilogue(acc, ex, outs):
        r = jnp.maximum(acc, 0.0)
        outs[0][...] = (r * r).astype(BF16)
        outs[1][...] = (2.0 * r).astype(BF16)

    blk = pl.BlockSpec((tm, tn), lambda i, j, k: (i, j))
    out = jax.ShapeDtypeStruct((s, N_CHIPS * n_sh), BF16)
    return _mm(
        "ff_up", (s // tm, N_CHIPS * per, 1), x1b, pl.BlockSpec((tm, d), lambda i, j, k: (i, 0)),
        w1_st, pl.BlockSpec((None, d, tn), lambda i, j, k: (j // per, 0, j % per)), NN, [], [],
        [out, out], [blk, blk], epilogue, None)


def _ff_down_loss(r, w2, xhat1, g1, b1, g2, b2, target):
    s, d = xhat1.shape
    dff = r.shape[1]
    tm, tk = min(s, 512), min(dff, 2048)
    ch = min(tm, EPILOGUE_ROWS)

    def epilogue(acc_ref, ex, outs):
        xh1_ref, g1_ref, b1_ref, g2_ref, b2_ref, t_ref = ex
        dz_ref, dzb_ref, st_ref = outs
        g1v, b1v, g2v, b2v = g1_ref[...], b1_ref[...], g2_ref[...], b2_ref[...]
        dg = db = loss = None
        for c in range(tm // ch):
            rows = slice(c * ch, (c + 1) * ch)
            x1 = xh1_ref[rows, :] * g1v + b1v
            xhat2, rstd2 = _layer_norm_fwd(ALPHA * x1 + acc_ref[rows, :])
            err = xhat2 * g2v + b2v - t_ref[rows, :]
            dy = err * (1.0 / d)
            dz = _layer_norm_bwd(dy, xhat2, rstd2, g2v)
            dz_ref[rows, :] = dz
            dzb_ref[rows, :] = dz.astype(BF16)
            parts = (jnp.sum(dy * xhat2, axis=0, keepdims=True), jnp.sum(dy, axis=0, keepdims=True),
                     jnp.sum(jnp.sum(err * err, axis=-1, keepdims=True), axis=0, keepdims=True))
            dg, db, loss = parts if c == 0 else (dg + parts[0], db + parts[1], loss + parts[2])
        st_ref[...] = _stats_rows([dg, db, jnp.broadcast_to((0.5 / d) * loss, (1, d))], d)

    row = pl.BlockSpec((tm, d), lambda i, j, k: (i, 0))
    vec = pl.BlockSpec((1, d), lambda i, j, k: (0, 0))
    return _mm(
        "ff_down_loss", (s // tm, 1, dff // tk), r, pl.BlockSpec((tm, tk), lambda i, j, k: (i, k)),
        w2, pl.BlockSpec((tk, d), lambda i, j, k: (k, 0)), NN,
        [xhat1, g1, b1, g2, b2, target], [row, vec, vec, vec, vec, row],
        [jax.ShapeDtypeStruct((s, d), F32), jax.ShapeDtypeStruct((s, d), BF16),
         jax.ShapeDtypeStruct((s // tm, 8, d), F32)],
        [row, row, pl.BlockSpec((None, 8, d), lambda i, j, k: (i, 0, 0))], epilogue, (tm, d),
        acc_as_ref=True)


def _ff_down_bwd(dz2b, w2, r_slope):
    s, d = dz2b.shape
    dff = r_slope.shape[1]
    tm, tn = min(s, 1024), min(dff, 1024)

    def epilogue(acc, ex, outs):
        outs[0][...] = (acc * ex[0][...].astype(F32)).astype(BF16)

    blk = pl.BlockSpec((tm, tn), lambda i, j, k: (i, j))
    return _mm(
        "ff_down_bwd", (s // tm, dff // tn, 1), dz2b, pl.BlockSpec((tm, d), lambda i, j, k: (i, 0)),
        w2, pl.BlockSpec((tn, d), lambda i, j, k: (j, 0)), NT, [r_slope], [blk],
        [jax.ShapeDtypeStruct((s, dff), BF16)], [blk], epilogue, None)[0]


def _wgrad(name, a, g, n_sh, shard0=0, n_shards=None, fill=None):
    s, rows = a.shape
    cols = g.shape[1]
    tm, tn, tk = min(rows, 2048), min(cols, 1024), min(s, 2048)
    if tn >= n_sh:
        span = tn // n_sh
        out_spec = pl.BlockSpec((span, tm, n_sh), lambda i, j, k: (j + shard0 // span, i, 0))

        def epilogue(acc_ref, ex, outs):
            for sh in range(span):
                outs[0][sh] = acc_ref[:, sh * n_sh:(sh + 1) * n_sh].astype(BF16)
    else:
        per = n_sh // tn
        out_spec = pl.BlockSpec((None, tm, tn), lambda i, j, k: (shard0 + j // per, i, j % per))

        def epilogue(acc_ref, ex, outs):
            outs[0][...] = acc_ref[...].astype(BF16)

    return _mm(
        name, (rows // tm, cols // tn, s // tk), a, pl.BlockSpec((tk, tm), lambda i, j, k: (k, i)),
        g, pl.BlockSpec((tk, tn), lambda i, j, k: (k, j)), TN, [], [],
        [jax.ShapeDtypeStruct((n_shards or cols // n_sh, rows, n_sh), BF16)], [out_spec], epilogue,
        (tm, tn), acc_as_ref=True, fill=fill)[0]


def _ff_up_bwd(da, w1_st, dz2, xhat1, rstd1, g1, run_after):
    s, d = dz2.shape
    n_sh = w1_st.shape[2]
    tm, tk = min(s, 512), min(n_sh, 2048)
    per = n_sh // tk
    ch = min(tm, EPILOGUE_ROWS)

    def epilogue(acc_ref, ex, outs):
        dz2_ref, xh_ref, rs_ref, g_ref = ex
        dz_ref, dzb_ref, st_ref = outs
        gv = g_ref[...]
        dg = db = None
        for c in range(tm // ch):
            rows = slice(c * ch, (c + 1) * ch)
            dx1 = ALPHA * dz2_ref[rows, :] + acc_ref[rows, :]
            xhat = xh_ref[rows, :]
            dz = _layer_norm_bwd(dx1, xhat, rs_ref[rows, :], gv)
            dz_ref[rows, :] = dz
            dzb_ref[rows, :] = dz.astype(BF16)
            parts = (jnp.sum(dx1 * xhat, axis=0, keepdims=True), jnp.sum(dx1, axis=0, keepdims=True))
            dg, db = parts if c == 0 else (dg + parts[0], db + parts[1])
        st_ref[...] = _stats_rows([dg, db], d)

    row = pl.BlockSpec((tm, d), lambda i, j, k: (i, 0))
    return _mm(
        "ff_up_bwd", (s // tm, 1, N_CHIPS * per), da, pl.BlockSpec((tm, tk), lambda i, j, k: (i, k)),
        w1_st, pl.BlockSpec((None, d, tk), lambda i, j, k: (k // per, 0, k % per)), NT,
        [dz2, xhat1, rstd1, g1],
        [row, row, pl.BlockSpec((tm, 1), lambda i, j, k: (i, 0)), pl.BlockSpec((1, d), lambda i, j, k: (0, 0))],
        [jax.ShapeDtypeStruct((s, d), F32), jax.ShapeDtypeStruct((s, d), BF16),
         jax.ShapeDtypeStruct((s // tm, 8, d), F32)],
        [row, row, pl.BlockSpec((None, 8, d), lambda i, j, k: (i, 0, 0))], epilogue, (tm, d),
        acc_as_ref=True, run_after=run_after)


def _mix_bwd(dz1b, w_out, h, ya, yp, aw):
    s, d = dz1b.shape
    tm = min(s, 256)
    gblk = 4 * aw // d

    def epilogue(acc, ex, outs):
        sga_ref, sgp_ref, ya_ref, yp_ref = ex
        dya_ref, dyp_ref, dg_ref = outs
        sga, sgp = sga_ref[...].astype(F32), sgp_ref[...].astype(F32)
        dya_ref[...] = (acc * sga).astype(BF16)
        dyp_ref[...] = (acc * sgp).astype(BF16)
        dg_ref[:, :d] = (acc * ya_ref[...].astype(F32) * (sga * (1.0 - sga))).astype(BF16)
        dg_ref[:, d:] = (acc * yp_ref[...].astype(F32) * (sgp * (1.0 - sgp))).astype(BF16)

    row = pl.BlockSpec((tm, d), lambda i, j, k: (i, 0))
    return _mm(
        "mix_bwd", (s // tm, 1, 1), dz1b, row, w_out, pl.BlockSpec((d, d), lambda i, j, k: (0, 0)), NT,
        [h, h, ya, yp],
        [pl.BlockSpec((tm, d), lambda i, j, k: (i, gblk)),
         pl.BlockSpec((tm, d), lambda i, j, k: (i, gblk + 1)), row, row],
        [jax.ShapeDtypeStruct((s, d), BF16), jax.ShapeDtypeStruct((s, d), BF16),
         jax.ShapeDtypeStruct((s, 2 * d), BF16)],
        [row, row, pl.BlockSpec((tm, 2 * d), lambda i, j, k: (i, 0))], epilogue, None)


def _branch_in_bwd(name, tm, dyb, wb_st, epilogue, extras, extra_specs, out_shape, out_specs,
                   scratch=()):
    s, d = dyb.shape
    aw = wb_st.shape[1]
    wb_t = wb_st.transpose(0, 2, 1).reshape(d, aw)
    return _mm(
        name, (s // tm, 1, 1), dyb, pl.BlockSpec((tm, d), lambda i, j, k: (i, 0)),
        wb_t, pl.BlockSpec((d, aw), lambda i, j, k: (0, 0)), NN,
        extras, extra_specs, out_shape, out_specs, epilogue, None, scratch=scratch)


def _attn_out_bwd(dya, wba_st, o_attn):
    s, aw = o_attn.shape
    tm = min(s, 512)

    def epilogue(acc_ref, ex, outs, sdo, sdl):
        do_ref, dl_ref, do4_ref, do16_ref, dl4_ref, dl16_ref = outs
        deltas = []
        for hh in range(aw // HEAD_DIM):
            sl = slice(hh * HEAD_DIM, (hh + 1) * HEAD_DIM)
            do = acc_ref[:, sl]
            deltas.append(jnp.sum(do * ex[0][:, sl].astype(F32), axis=-1, keepdims=True))
            sdo[hh] = do
            do_ref[:, sl] = do.astype(BF16)
        packed = _pack_heads(deltas)
        sdl[0] = packed
        dl_ref[...] = packed
        _to_views(sdo, (do4_ref, do16_ref), BF16)
        _to_views(sdl, (dl4_ref, dl16_ref), F32)

    def specs(width):
        return ([pl.BlockSpec((tm, width), lambda i, j, k: (i, 0))]
                + [pl.BlockSpec((tm // dil, dil * width), lambda i, j, k: (i, 0)) for dil in DILATIONS[1:]])

    def shapes(width, dtype):
        return ([jax.ShapeDtypeStruct((s, width), dtype)]
                + [jax.ShapeDtypeStruct(_view_shape(s, width, dil), dtype) for dil in DILATIONS[1:]])

    do_specs, dl_specs = specs(aw), specs(HEAD_DIM)
    do_shapes, dl_shapes = shapes(aw, BF16), shapes(HEAD_DIM, F32)
    return _branch_in_bwd(
        "attn_out_bwd", tm, dya, wba_st, epilogue, [o_attn], [do_specs[0]],
        [do_shapes[0], dl_shapes[0], *do_shapes[1:], *dl_shapes[1:]],
        [do_specs[0], dl_specs[0], *do_specs[1:], *dl_specs[1:]],
        scratch=[_heads_scratch(tm, aw), _heads_scratch(tm, HEAD_DIM)])


def _pool_out_bwd(dyp, wbp_st, pw, scale):
    s, pw_ = pw.shape
    tm = min(s, 1024)

    def epilogue(acc_ref, ex, outs):
        pw_ref, sc_ref = ex
        dpw_ref, st_ref = outs
        acc = acc_ref[...]
        dpw_ref[...] = (acc * sc_ref[...]).astype(BF16)
        st_ref[...] = _stats_rows([jnp.sum(acc * pw_ref[...].astype(F32), axis=0, keepdims=True)], pw_)

    row = pl.BlockSpec((tm, pw_), lambda i, j, k: (i, 0))
    return _branch_in_bwd(
        "pool_out_bwd", tm, dyp, wbp_st, epilogue, [pw, scale],
        [row, pl.BlockSpec((1, pw_), lambda i, j, k: (0, 0))],
        [jax.ShapeDtypeStruct((s, pw_), BF16), jax.ShapeDtypeStruct((s // tm, 8, pw_), F32)],
        [row, pl.BlockSpec((None, 8, pw_), lambda i, j, k: (i, 0, 0))])


def _pool_bwd(dpw, p, wp):
    s, pw_ = p.shape
    ng = len(POOL_WINDOWS)
    pgw = pw_ // ng
    tm = min(s, 512)
    hb = tm // POOL_HALO
    nblk = s // tm

    def body(dpw_ref, nxt_ref, p_ref, wp_ref, dwp_ref, du_ref):
        i = pl.program_id(0)
        nxt = (nxt_ref[...].astype(F32) * jnp.where(i < nblk - 1, 1.0, 0.0)).astype(BF16)
        dpw_all = jnp.concatenate([dpw_ref[...], nxt], axis=0)

        @pl.when(i == 0)
        def _():
            dwp_ref[...] = jnp.zeros_like(dwp_ref)

        dps = []
        for g in range(ng):
            sl = slice(g * pgw, (g + 1) * pgw)
            dwp_ref[g] += _dot(p_ref[:, sl], dpw_ref[:, sl], TN)
            dps.append(_dot(dpw_all[:, sl], wp_ref[g], NT))
        dp = jnp.concatenate(dps, axis=1)
        dpn = dp / _pool_counts(tm, tm + POOL_HALO, pgw, i * tm)
        du_ref[...] = (_window_sums(dpn, -1, pgw)[:tm] - dp[:tm]).astype(BF16)

    row = pl.BlockSpec((tm, pw_), lambda i: (i, 0))
    full = pl.BlockSpec((ng, pgw, pgw), lambda i: (0, 0, 0))
    return pl.pallas_call(
        body, name="pool_bwd", grid=(nblk,),
        in_specs=[row, pl.BlockSpec((POOL_HALO, pw_), lambda i: (jnp.minimum((i + 1) * hb, s // POOL_HALO - 1), 0)),
                  row, full],
        out_specs=[full, row],
        out_shape=[jax.ShapeDtypeStruct((ng, pgw, pgw), F32), jax.ShapeDtypeStruct((s, pw_), BF16)],
        compiler_params=_params(("arbitrary",)),
    )(dpw, dpw, p, wp)


def _attn_bwd(name, q, k, v, do, lse, delta, offs, cw, width):
    m = do.shape[0]
    nh = cw // HEAD_DIM
    rb = min(m, ATTN_STEP_ROWS)
    nsub = rb // SUB_BLOCK
    nstep = m // rb
    grid = (width // cw, nstep + 1)
    scale = HEAD_DIM ** -0.5
    last = slice(rb - SUB_BLOCK, rb)

    def body(q_ref, k_ref, v_ref, do_ref, lse_ref, dl_ref, dq_ref, dk_ref, dv_ref,
             kprev, vprev, dk_carry, dv_carry):
        n = pl.program_id(1)

        @pl.when(n == 0)
        def _():
            for ref in (kprev, vprev, dk_carry, dv_carry):
                ref[...] = jnp.zeros_like(ref)

        qi = lax.broadcasted_iota(jnp.int32, (SUB_BLOCK, 2 * SUB_BLOCK), 0)
        kj = lax.broadcasted_iota(jnp.int32, (SUB_BLOCK, 2 * SUB_BLOCK), 1)
        band = (kj >= qi) & (kj <= qi + SUB_BLOCK)
        flush = jnp.where(n == nstep, 4 * SUB_BLOCK, 0)
        valids = [band & (kj >= jnp.where(n == 0, SUB_BLOCK, flush))] + [band & (kj >= flush)] * (nsub - 1)
        heads = [slice(hh * HEAD_DIM, (hh + 1) * HEAD_DIM) for hh in range(nh)]

        def rows(sub):
            return slice(sub * SUB_BLOCK, (sub + 1) * SUB_BLOCK)

        def cat(prev, ref, sub, sl):
            if sub == 0:
                return jnp.concatenate([prev[:, sl], ref[rows(0), sl]], axis=0)
            return ref[(sub - 1) * SUB_BLOCK:(sub + 1) * SUB_BLOCK, sl]

        kcats = [[cat(kprev, k_ref, sub, sl) for sl in heads] for sub in range(nsub)]
        scs = [[_dot(q_ref[rows(sub), sl], kcats[sub][hh], NT) for hh, sl in enumerate(heads)]
               for sub in range(nsub)]
        dps = [[_dot(do_ref[rows(sub), sl], cat(vprev, v_ref, sub, sl), NT) for sl in heads]
               for sub in range(nsub)]
        dqs = []
        stats = [(lse_ref[rows(sub), :], dl_ref[rows(sub), :]) for sub in range(nsub)]
        for hh, sl in enumerate(heads):
            dk2, dv2 = [], []
            for sub in range(nsub):
                lse_h, dl_h = _head_col(stats[sub][0], hh, nh), _head_col(stats[sub][1], hh, nh)
                valid = valids[sub]
                p = jnp.where(valid, jnp.exp(jnp.where(valid, scs[sub][hh] * scale, NEG) - lse_h), 0.0)
                ds = (p * (dps[sub][hh] - dl_h)).astype(BF16)
                dqs.append((sub, sl, (_dot(ds, kcats[sub][hh], NN) * scale).astype(BF16)))
                dk2.append(_dot(ds, q_ref[rows(sub), sl], TN) * scale)
                dv2.append(_dot(p.astype(BF16), do_ref[rows(sub), sl], TN))
            for out_ref, carry, new in ((dk_ref, dk_carry, dk2), (dv_ref, dv_carry, dv2)):
                out_ref[last, sl] = (carry[last, sl] + new[0][:SUB_BLOCK]).astype(BF16)
                if nsub > 1:
                    out_ref[:rb - SUB_BLOCK, sl] = carry[:rb - SUB_BLOCK, sl].astype(BF16)
                for sub in range(nsub):
                    val = new[sub][SUB_BLOCK:]
                    if sub + 1 < nsub:
                        val = val + new[sub + 1][:SUB_BLOCK]
                    carry[rows(sub), sl] = val

        @pl.when(n < nstep)
        def _():
            for sub, sl, val in dqs:
                dq_ref[rows(sub), sl] = val

        kprev[...] = k_ref[last, :]
        vprev[...] = v_ref[last, :]

    def cur(off):
        return pl.BlockSpec((rb, cw), lambda j, n: (jnp.minimum(n, nstep - 1), j + off))

    lagged = pl.BlockSpec((rb, cw), lambda j, n: (jnp.maximum(n - 1, 0), j))
    stat = pl.BlockSpec((rb, HEAD_DIM), lambda j, n: (jnp.minimum(n, nstep - 1), j))
    out = jax.ShapeDtypeStruct((m, width), BF16)
    return pl.pallas_call(
        body, name=name, grid=grid,
        in_specs=[cur(offs[0]), cur(offs[1]), cur(offs[2]), cur(0), stat, stat],
        out_specs=[cur(0), lagged, lagged], out_shape=[out, out, out],
        scratch_shapes=[pltpu.VMEM((SUB_BLOCK, cw), BF16)] * 2 + [pltpu.VMEM((rb, cw), F32)] * 2,
        compiler_params=_params(("parallel", "arbitrary")),
    )(q, k, v, do, lse, delta)


def _qkvu_grad(d1, d4, d16, du, cos_t, sin_t):
    s, aw = du.shape
    tm = min(s, 512)

    def body(*refs):
        nat, v4, v16 = refs[0:3], refs[3:6], refs[6:9]
        cos_ref, sin_ref, du_ref, out_ref, s4, s16 = refs[9:]
        cos, sin = cos_ref[...], sin_ref[...]
        for part in range(3):
            _from_views((v4[part], v16[part]), (s4, s16))
            for hh in range(aw // HEAD_DIM):
                sl = slice(hh * HEAD_DIM, (hh + 1) * HEAD_DIM)
                t = nat[part][:, sl].astype(F32) + s4[hh] + s16[hh]
                if part < 2:
                    t = t * cos - pltpu.roll(t, HEAD_DIM // 2, 1) * sin
                out_ref[:, part * aw + hh * HEAD_DIM:part * aw + (hh + 1) * HEAD_DIM] = t.astype(BF16)
        out_ref[:, 3 * aw:] = du_ref[...]

    row = pl.BlockSpec((tm, aw), lambda i: (i, 0))
    tab = pl.BlockSpec((tm, HEAD_DIM), lambda i: (i, 0))
    return pl.pallas_call(
        body, name="qkvu_grad", grid=(s // tm,),
        in_specs=[row] * 3 + [_view_spec(tm, aw, 4)] * 3 + [_view_spec(tm, aw, 16)] * 3 + [tab, tab, row],
        out_specs=pl.BlockSpec((tm, 4 * aw), lambda i: (i, 0)),
        out_shape=jax.ShapeDtypeStruct((s, 4 * aw), BF16),
        scratch_shapes=[_heads_scratch(tm, aw)] * 2,
        compiler_params=_params(("parallel",)),
    )(*d1, *d4, *d16, cos_t, sin_t, du)


def _in_proj_bwd_x(name, dh, w_in_st, shard0, base, scale_base, run_after=()):
    s, kdim = dh.shape
    d, n_sh = w_in_st.shape[1], w_in_st.shape[2]
    tm, tk = min(s, 512), min(n_sh, 2048)
    per = n_sh // tk

    ch = min(tm, 2 * EPILOGUE_ROWS)

    def epilogue(acc_ref, ex, outs):
        for c in range(tm // ch):
            rows = slice(c * ch, (c + 1) * ch)
            outs[0][rows, :] = scale_base * ex[0][rows, :] + acc_ref[rows, :]

    row = pl.BlockSpec((tm, d), lambda i, j, k: (i, 0))
    return _mm(
        name, (s // tm, 1, kdim // tk), dh, pl.BlockSpec((tm, tk), lambda i, j, k: (i, k)),
        w_in_st, pl.BlockSpec((None, d, tk), lambda i, j, k: (shard0 + k // per, 0, k % per)), NT,
        [base], [row], [jax.ShapeDtypeStruct((s, d), F32)], [row], epilogue, (tm, d),
        acc_as_ref=True, run_after=run_after)[0]


def _chip_peers():
    x, y, c = lax.axis_index("x"), lax.axis_index("y"), lax.axis_index("c")
    return x, y, c, [(1 - x, y), (x, 1 - y), (1 - x, 1 - y)]


GATHER, GATHER_HALF, SCATTER, SIBLING = "gather", "gather_half", "scatter", "sibling"


def _exchange_peers(mode):
    x, y, c, chips = _chip_peers()
    if mode == SIBLING:
        return x, y, c, [(x, y, 1 - c)]
    return x, y, c, [(px, py, c) for px, py in chips]


def _core_half(ref_or_shape, c):
    rows = (ref_or_shape.shape[0]) // 2
    return pl.ds(c * rows, rows)


def _exchange_descriptor(mode, src, land, send, recv, p, peer, me, arriving):
    pid = 2 * peer[0] + peer[1]
    if mode == GATHER:
        src_ref, dst_ref = src, land.at[pid if arriving else me]
    elif mode == GATHER_HALF:
        rows = _core_half(src, peer[2])
        src_ref, dst_ref = src.at[rows], land.at[pid if arriving else me, rows]
    elif mode == SCATTER:
        src_ref, dst_ref = src.at[pid], land.at[p]
    else:
        src_ref, dst_ref = src, land
    return pltpu.make_async_remote_copy(
        src_ref=src_ref, dst_ref=dst_ref, send_sem=send.at[p], recv_sem=recv.at[p],
        device_id=peer, device_id_type=MESH)


def _exchange_start(name, mode, srcs, land_shapes):
    n = len(srcs)
    lands = [_hbm(lax.empty(shape, src.dtype)) for shape, src in zip(land_shapes, srcs)]

    def body(*refs):
        src_refs, land_refs = refs[:n], refs[n:2 * n]
        sends, recvs = refs[2 * n:3 * n], refs[3 * n:4 * n]
        token = refs[6 * n]
        x, y, c, peers = _exchange_peers(mode)
        me = 2 * x + y
        for w in range(n):
            for p, peer in enumerate(peers):
                _exchange_descriptor(mode, src_refs[w], land_refs[w], sends[w], recvs[w], p, peer,
                                     me, arriving=False).start()
        token[...] = jnp.zeros_like(token)

    sem = pltpu.SemaphoreType.DMA((3,))
    outs = pl.pallas_call(
        body, name=name, in_specs=[HBM_SPEC] * (2 * n),
        out_specs=[SEM_SPEC] * (2 * n) + [HBM_SPEC] * (2 * n) + [pl.BlockSpec(memory_space=pltpu.VMEM)],
        out_shape=[sem] * (2 * n) + [pltpu.HBM(a.shape, a.dtype) for a in (*srcs, *lands)]
        + [jax.ShapeDtypeStruct((8, 128), F32)],
        input_output_aliases={i: 2 * n + i for i in range(2 * n)},
        compiler_params=pltpu.CompilerParams(has_side_effects=DATAFLOW),
    )(*[_hbm(a) for a in srcs], *lands)
    return {"send": outs[:n], "recv": outs[n:2 * n], "src": outs[2 * n:3 * n],
            "land": outs[3 * n:4 * n], "token": outs[4 * n]}


def _exchange_wait(name, mode, started, which, after):
    m = len(which)

    def body(*refs):
        src_refs, land_refs = refs[:m], refs[m:2 * m]
        sends, recvs = refs[2 * m:3 * m], refs[3 * m:4 * m]
        x, y, c, peers = _exchange_peers(mode)
        me = 2 * x + y
        for w in range(m):
            for p, peer in enumerate(peers):
                _exchange_descriptor(mode, src_refs[w], land_refs[w], sends[w], recvs[w], p, peer,
                                     me, arriving=False).wait_send()
                _exchange_descriptor(mode, src_refs[w], land_refs[w], sends[w], recvs[w], p, peer,
                                     me, arriving=True).wait_recv()

    pick = lambda key: [started[key][w] for w in which]
    bufs = pick("src") + pick("land")
    after = list(after) if isinstance(after, (list, tuple)) else [after]
    outs = pl.pallas_call(
        body, name=name,
        in_specs=[HBM_SPEC] * (2 * m) + [SEM_SPEC] * (2 * m) + [ANY_SPEC] * len(after),
        out_specs=[HBM_SPEC] * (2 * m), out_shape=[pltpu.HBM(a.shape, a.dtype) for a in bufs],
        input_output_aliases={i: i for i in range(2 * m)},
        compiler_params=pltpu.CompilerParams(has_side_effects=DATAFLOW),
    )(*bufs, *pick("send"), *pick("recv"), *after)
    return outs[:m], outs[m:]


def _to_bf16(name, a, run_after):
    r, c = a.shape
    tm = min(r, 512)

    def body(a_ref, after_ref, out_ref):
        out_ref[...] = a_ref[...].astype(BF16)

    blk = pl.BlockSpec((tm, c), lambda i: (i, 0))
    return pl.pallas_call(
        body, name=name, grid=(r // tm,), in_specs=[blk, ANY_SPEC], out_specs=blk,
        out_shape=jax.ShapeDtypeStruct((r, c), BF16), compiler_params=_params(("parallel",)),
    )(a, run_after)


def _swap_halves(name, lands):
    n = len(lands)

    def body(*refs):
        bufs = refs[n:2 * n]
        send, recv = refs[2 * n:]
        x, y, c, chips = _chip_peers()
        started = []
        for w in range(n):
            half = bufs[w].shape[1] // 2
            for p, (px, py) in enumerate(chips):
                mine = bufs[w].at[2 * px + py, pl.ds(c * half, half)]
                cp = pltpu.make_async_remote_copy(
                    src_ref=mine, dst_ref=mine, send_sem=send.at[w, p], recv_sem=recv.at[w, p],
                    device_id=(x, y, 1 - c), device_id_type=MESH)
                cp.start()
                started.append(cp)
        for w in range(n):
            half = bufs[w].shape[1] // 2
            for p, (px, py) in enumerate(chips):
                theirs = bufs[w].at[2 * px + py, pl.ds((1 - c) * half, half)]
                pltpu.make_async_remote_copy(
                    src_ref=theirs, dst_ref=theirs, send_sem=send.at[w, p], recv_sem=recv.at[w, p],
                    device_id=(x, y, 1 - c), device_id_type=MESH).wait_recv()
        for cp in started:
            cp.wait_send()

    return pl.pallas_call(
        body, name=name, in_specs=[ANY_SPEC] * n, out_specs=[ANY_SPEC] * n,
        out_shape=[jax.ShapeDtypeStruct(a.shape, a.dtype) for a in lands],
        input_output_aliases={i: i for i in range(n)},
        scratch_shapes=[pltpu.SemaphoreType.DMA((n, 3)), pltpu.SemaphoreType.DMA((n, 3))],
    )(*lands)


def _place_own(name, shard, land, me):
    r, c = shard.shape
    tm = min(r, 512)

    def body(me_ref, shard_ref, land_ref, out_ref):
        out_ref[...] = shard_ref[...]

    return pl.pallas_call(
        body, name=name,
        grid_spec=pltpu.PrefetchScalarGridSpec(
            num_scalar_prefetch=1, grid=(r // tm,),
            in_specs=[pl.BlockSpec((tm, c), lambda i, me_ref: (i, 0)), ANY_SPEC],
            out_specs=pl.BlockSpec((None, tm, c), lambda i, me_ref: (me_ref[0], i, 0))),
        out_shape=jax.ShapeDtypeStruct(land.shape, land.dtype), input_output_aliases={2: 0},
        compiler_params=_params(("arbitrary",)),
    )(me, shard, land)


def _sum_slabs(name, grads, land, me):
    _, r, c = grads.shape
    tm = min(r, 256)

    def body(me_ref, own_ref, land_ref, out_ref):
        acc = own_ref[...].astype(F32)
        for p in range(3):
            acc = acc + land_ref[p].astype(F32)
        out_ref[...] = acc

    return pl.pallas_call(
        body, name=name,
        grid_spec=pltpu.PrefetchScalarGridSpec(
            num_scalar_prefetch=1, grid=(r // tm,),
            in_specs=[pl.BlockSpec((None, tm, c), lambda i, me_ref: (me_ref[0], i, 0)),
                      pl.BlockSpec((3, tm, c), lambda i, me_ref: (0, i, 0))],
            out_specs=pl.BlockSpec((tm, c), lambda i, me_ref: (i, 0))),
        out_shape=jax.ShapeDtypeStruct((r, c), F32), compiler_params=_params(("parallel",)),
    )(me, grads, land)


def _allreduce_stats(stats, run_after):
    n = len(stats)

    def body(*refs):
        ins, outs = refs[:n], refs[n + 1:2 * n + 1]
        mine, gath = refs[2 * n + 1:3 * n + 1], refs[3 * n + 1:4 * n + 1]
        send, recv = refs[4 * n + 1:]
        x, y, c = lax.axis_index("x"), lax.axis_index("y"), lax.axis_index("c")
        me = 4 * x + 2 * y + c
        flips = [(bx, by, bc) for bx in (0, 1) for by in (0, 1) for bc in (0, 1)][1:]

        def peer(f):
            return (x + f[0] * (1 - 2 * x), y + f[1] * (1 - 2 * y), c + f[2] * (1 - 2 * c))

        copies = []
        for t in range(n):
            tot = ins[t][0]
            for b in range(1, ins[t].shape[0]):
                tot = tot + ins[t][b]
            mine[t][...] = tot
            gath[t][me] = tot
            for k, f in enumerate(flips):
                cp = pltpu.make_async_remote_copy(
                    src_ref=mine[t], dst_ref=gath[t].at[me], send_sem=send.at[t, k],
                    recv_sem=recv.at[t, k], device_id=peer(f), device_id_type=MESH)
                cp.start()
                copies.append(cp)
        for t in range(n):
            for k, f in enumerate(flips):
                px, py, pc = peer(f)
                pltpu.make_async_remote_copy(
                    src_ref=mine[t], dst_ref=gath[t].at[4 * px + 2 * py + pc], send_sem=send.at[t, k],
                    recv_sem=recv.at[t, k], device_id=(px, py, pc), device_id_type=MESH).wait_recv()
        for cp in copies:
            cp.wait_send()
        for t in range(n):
            tot = gath[t][0]
            for dev in range(1, 8):
                tot = tot + gath[t][dev]
            outs[t][...] = tot

    vm = pl.BlockSpec(memory_space=pltpu.VMEM)
    return pl.pallas_call(
        body, name="allreduce_stats", in_specs=[vm] * n + [ANY_SPEC], out_specs=[vm] * n,
        out_shape=[jax.ShapeDtypeStruct(s.shape[1:], F32) for s in stats],
        scratch_shapes=[pltpu.VMEM(s.shape[1:], F32) for s in stats]
        + [pltpu.VMEM((8, *s.shape[1:]), F32) for s in stats]
        + [pltpu.SemaphoreType.DMA((n, 7)), pltpu.SemaphoreType.DMA((n, 7))],
    )(*stats, run_after)


def _adamw(name, w, m, v, g_parts):
    r, c = w.shape
    tm = min(r, 128)
    n_g = len(g_parts)

    def body(*refs):
        w_ref, m_ref, v_ref = refs[:3]
        g_refs = refs[3:3 + n_g]
        g_out, d_out, m_out, v_out = refs[3 + n_g:]
        g = g_refs[0][...]
        for gr in g_refs[1:]:
            g = g + gr[...]
        m_new = ADAM_B1 * m_ref[...] + (1.0 - ADAM_B1) * g
        v_new = ADAM_B2 * v_ref[...] + (1.0 - ADAM_B2) * (g * g)
        m_hat = m_new / (1.0 - ADAM_B1 ** ADAM_STEP)
        v_hat = v_new / (1.0 - ADAM_B2 ** ADAM_STEP)
        g_out[...] = g
        d_out[...] = -ADAM_LR * (m_hat / (jnp.sqrt(v_hat) + ADAM_EPS) + ADAM_WD * w_ref[...])
        m_out[...] = m_new
        v_out[...] = v_new

    blk = pl.BlockSpec((tm, c), lambda i: (i, 0))
    out = jax.ShapeDtypeStruct((r, c), F32)
    return pl.pallas_call(
        body, name=name, grid=(r // tm,), in_specs=[blk] * (3 + n_g), out_specs=[blk] * 4,
        out_shape=[out] * 4, compiler_params=_params(("parallel",)),
    )(w, m, v, *g_parts)


def _rope_tables(positions):
    half = HEAD_DIM // 2
    inv_freq = ROPE_THETA ** (-jnp.arange(half, dtype=F32) / half)
    ang = positions.astype(F32)[0, :, None] * inv_freq
    cos, sin = jnp.cos(ang), jnp.sin(ang)
    return jnp.concatenate([cos, cos], axis=-1), jnp.concatenate([-sin, sin], axis=-1)


def kernel(x, positions, w_in, w_pool, pool_scale, w_branch_attn, w_branch_pool, w_out, ln_mix_g, ln_mix_b, w_ff1, w_ff2, ln_ff_g, ln_ff_b, loss_target, m_w_in, m_w_pool, m_pool_scale, m_w_branch_attn, m_w_branch_pool, m_w_out, m_ln_mix_g, m_ln_mix_b, m_w_ff1, m_w_ff2, m_ln_ff_g, m_ln_ff_b, v_w_in, v_w_pool, v_pool_scale, v_w_branch_attn, v_w_branch_pool, v_w_out, v_ln_mix_g, v_ln_mix_b, v_w_ff1, v_w_ff2, v_ln_ff_g, v_ln_ff_b):
    s, d = x.shape[1], x.shape[2]
    aw = d // 2
    ng = len(POOL_WINDOWS)
    pgw = aw // ng
    x2d, target = x[0], loss_target[0]
    xb = x2d.astype(BF16)
    cos_t, sin_t = _rope_tables(positions)

    big = {"w_in": w_in[0], "w_pool": w_pool[0].reshape(-1, pgw), "w_branch_attn": w_branch_attn[0],
           "w_branch_pool": w_branch_pool[0], "w_out": w_out[0], "w_ff1": w_ff1[0], "w_ff2": w_ff2[0]}
    names = list(big)
    me_chip = (2 * lax.axis_index("x") + lax.axis_index("y")).astype(jnp.int32).reshape(1)
    land_shapes = [(N_CHIPS, *big[k].shape) for k in names]
    gathering_in = _exchange_start("gather_start_in", GATHER_HALF,
                                   [_to_bf16("to_bf16_w_in", big["w_in"], positions)], land_shapes[:1])
    shards = [_to_bf16(f"to_bf16_{k}", big[k], gathering_in["token"]) for k in names[1:]]
    gathering = _exchange_start("gather_start", GATHER, shards, land_shapes[1:])

    def gathered(name, which, after):
        srcs, lands = _exchange_wait(f"gather_wait_{name}", GATHER, gathering, which, after)
        return [_place_own(f"place_own_{names[w + 1]}", srcs[i], lands[i], me_chip)
                for i, w in enumerate(which)]

    rows_sh = pgw // N_CHIPS
    dff = N_CHIPS * big["w_ff2"].shape[0]

    srcs, lands = _exchange_wait("gather_wait_in", GATHER_HALF, gathering_in, [0],
                                 [gathering["token"], xb, cos_t, sin_t])
    w_in_st = _place_own("place_own_w_in", srcs[0], _swap_halves("swap_halves_in", lands)[0], me_chip)
    h, hv4, hv16 = _in_proj(xb, w_in_st, cos_t, sin_t, aw)
    (wp_st,) = gathered("pool", [0], h)
    wp = wp_st.reshape(N_CHIPS, ng, rows_sh, pgw).transpose(1, 0, 2, 3).reshape(ng, pgw, pgw)
    qkv = {1: (h, h, h), 4: (hv4, hv4, hv4), 16: (hv16, hv16, hv16)}
    offs = {dil: (0, dil, 2 * dil) for dil in DILATIONS}
    o_parts, lse_parts = [], []
    for dil in DILATIONS:
        o_p, lse_p = _attn_fwd(f"attn_fwd_d{dil}", *qkv[dil], offs[dil], aw, dil * aw)
        o_parts.append(o_p)
        lse_parts.append(lse_p)
    o_attn, lse, lse4, lse16 = _attn_combine(o_parts[0], lse_parts[0], o_parts[1:], lse_parts[1:])
    p, pw, y = _pool_fwd(h, wp, pool_scale, aw)
    wba_st, wbp_st, w_out_st = gathered("mix", [1, 2, 3], y)
    w_out_full = w_out_st.reshape(d, d)
    ya, yp, merged = _branch_merge(o_attn, y, wba_st, wbp_st, h, aw, d)
    xhat1, rstd1, x1b = _mix_norm(merged, w_out_full, x2d, ln_mix_g, ln_mix_b)
    w1_st, w2_st = gathered("ff", [4, 5], x1b)
    w2_full = w2_st.reshape(dff, d)
    r, r_slope = _ff_up(x1b, w1_st)
    dz2, dz2b, st2 = _ff_down_loss(r, w2_full, xhat1, ln_mix_g, ln_mix_b, ln_ff_g, ln_ff_b, target)

    def scatter_start(name, grads):
        return _exchange_start(f"scatter_start_{name}", SCATTER, grads, [(3, *g.shape[1:]) for g in grads])

    da = _ff_down_bwd(dz2b, w2_full, r_slope)
    g_w2 = _wgrad("wgrad_ff2", r, dz2b, d).reshape(N_CHIPS, dff // N_CHIPS, d)
    g_w1 = _wgrad("wgrad_ff1", x1b, da, dff // N_CHIPS)
    sent_ff = scatter_start("ff", [g_w1, g_w2])
    dz1, dz1b, st1 = _ff_up_bwd(da, w1_st, dz2, xhat1, rstd1, ln_mix_g, [sent_ff["token"]])
    dya, dyp, dgate = _mix_bwd(dz1b, w_out_full, h, ya, yp, aw)
    g_wout = _wgrad("wgrad_out", merged, dz1b, d).reshape(N_CHIPS, d // N_CHIPS, d)
    g_wba = _wgrad("wgrad_branch_attn", o_attn, dya, d // N_CHIPS)
    g_wbp = _wgrad("wgrad_branch_pool", y, dyp, d // N_CHIPS)
    do, delta, do4, do16, delta4, delta16 = _attn_out_bwd(dya, wba_st, o_attn)
    dpw, stp = _pool_out_bwd(dyp, wbp_st, pw, pool_scale)
    dwp, du = _pool_bwd(dpw, p, wp)
    g_wp = dwp.reshape(ng, N_CHIPS, rows_sh, pgw).transpose(1, 0, 2, 3).reshape(
        N_CHIPS, ng * rows_sh, pgw).astype(BF16)
    sent_mix = scatter_start("mix", [g_wp, g_wba, g_wbp, g_wout])

    bwd_in = {1: (do, lse, delta), 4: (do4, lse4, delta4), 16: (do16, lse16, delta16)}
    dqkv = {}
    for dil in DILATIONS:
        args = (*qkv[dil], *bwd_in[dil], offs[dil], aw, dil * aw)
        dqkv[dil] = _attn_bwd(f"attn_bwd_d{dil}", *args)
    dqkvu = _qkvu_grad(dqkv[1], dqkv[4], dqkv[16], du, cos_t, sin_t)
    g_win = _wgrad("wgrad_in_gates", xb, dgate, d, shard0=2, n_shards=N_CHIPS)
    g_win = _wgrad("wgrad_in_qkvu", xb, dqkvu, d, n_shards=N_CHIPS, fill=g_win)
    sent_in = scatter_start("in", [g_win])
    dx_a = _in_proj_bwd_x("in_proj_bwd_qkvu", dqkvu, w_in_st, 0, dz1, ALPHA,
                          [sent_mix["token"], sent_in["token"]])
    grad_x = _in_proj_bwd_x("in_proj_bwd_gates", dgate, w_in_st, 2, dx_a, 1.0)

    moments = {"w_in": (m_w_in, v_w_in), "w_pool": (m_w_pool, v_w_pool),
               "w_branch_attn": (m_w_branch_attn, v_w_branch_attn),
               "w_branch_pool": (m_w_branch_pool, v_w_branch_pool), "w_out": (m_w_out, v_w_out),
               "w_ff1": (m_w_ff1, v_w_ff1), "w_ff2": (m_w_ff2, v_w_ff2)}
    originals = {"w_in": w_in, "w_pool": w_pool, "w_branch_attn": w_branch_attn,
                 "w_branch_pool": w_branch_pool, "w_out": w_out, "w_ff1": w_ff1, "w_ff2": w_ff2}
    res = {}

    def summed(name, sent, keys, after):
        srcs, lands = _exchange_wait(f"scatter_wait_{name}", SCATTER, sent, list(range(len(keys))), after)
        parts = [_sum_slabs(f"sum_slabs_{k}", srcs[i], lands[i], me_chip) for i, k in enumerate(keys)]
        return _exchange_start(f"cores_start_{name}", SIBLING, parts, [a.shape for a in parts])

    def updated(name, swapping, keys, after):
        mine, other = _exchange_wait(f"cores_wait_{name}", SIBLING, swapping, list(range(len(keys))), after)
        for i, k in enumerate(keys):
            mk, vk = (a.reshape(big[k].shape) for a in moments[k])
            outs = _adamw(f"adamw_{k}", big[k], mk, vk, [mine[i], other[i]])
            res[k] = [o.reshape(originals[k].shape) for o in outs]

    groups = {"ff": ["w_ff1", "w_ff2"], "mix": ["w_pool", "w_branch_attn", "w_branch_pool", "w_out"],
              "in": ["w_in"]}
    swap_ff = summed("ff", sent_ff, groups["ff"], grad_x)
    swap_mix = summed("mix", sent_mix, groups["mix"], swap_ff["token"])
    swap_in = summed("in", sent_in, groups["in"], swap_mix["token"])
    updated("ff", swap_ff, groups["ff"], swap_in["token"])
    updated("mix", swap_mix, groups["mix"], res["w_ff2"][0])
    updated("in", swap_in, groups["in"], res["w_out"][0])
    tot2, tot1, totp = _allreduce_stats([st2, st1, stp], res["w_in"][0])

    def pad_d(a):
        return jnp.pad(a, ((0, 0), (0, d - a.shape[1])))

    small = ["ln_mix_g", "ln_mix_b", "ln_ff_g", "ln_ff_b", "pool_scale"]
    small_w = {"ln_mix_g": ln_mix_g, "ln_mix_b": ln_mix_b, "ln_ff_g": ln_ff_g, "ln_ff_b": ln_ff_b,
               "pool_scale": pool_scale}
    small_m = {"ln_mix_g": m_ln_mix_g, "ln_mix_b": m_ln_mix_b, "ln_ff_g": m_ln_ff_g,
               "ln_ff_b": m_ln_ff_b, "pool_scale": m_pool_scale}
    small_v = {"ln_mix_g": v_ln_mix_g, "ln_mix_b": v_ln_mix_b, "ln_ff_g": v_ln_ff_g,
               "ln_ff_b": v_ln_ff_b, "pool_scale": v_pool_scale}
    small_g = [tot1[0:1], tot1[1:2], tot2[0:1], tot2[1:2], pad_d(totp[0:1])]

    def pack(rows):
        return jnp.concatenate([pad_d(a) for a in rows] + [jnp.zeros((8 - len(rows), d), F32)], axis=0)

    outs = _adamw("adamw_small", pack([small_w[k] for k in small]), pack([small_m[k] for k in small]),
                  pack([small_v[k] for k in small]), [pack(small_g)])
    for i, k in enumerate(small):
        res[k] = [o[i:i + 1, :small_w[k].shape[1]] for o in outs]
    loss = tot2[2, 0]

    order = ["w_in", "w_pool", "pool_scale", "w_branch_attn", "w_branch_pool", "w_out", "ln_mix_g",
             "ln_mix_b", "w_ff1", "w_ff2", "ln_ff_g", "ln_ff_b"]
    result = [loss, grad_x[None]]
    for idx in range(4):
        result += [res[k][idx] for k in order]
    return tuple(result)
```

```python
import jax
import jax.numpy as jnp
from jax import lax
from jax.experimental import pallas as pl
from jax.experimental.pallas import tpu as pltpu

F32 = jnp.float32
BF16 = jnp.bfloat16
MESH = pl.DeviceIdType.MESH

HEAD_DIM = 128
SUB_BLOCK = 128
ATTN_STEP_ROWS = 512
DILATIONS = (1, 4, 16)
POOL_WINDOWS = (2, 4, 8, 16)
POOL_HALO = 16
ROPE_THETA = 10000.0
LN_EPS = 1e-5
ALPHA = 2.0 ** 0.25
ADAM_LR, ADAM_B1, ADAM_B2, ADAM_EPS, ADAM_WD, ADAM_STEP = 0.001, 0.9, 0.999, 1e-08, 0.01, 10
NEG = -1e30
N_CHIPS = 4
VMEM_LIMIT = 62 * 1024 * 1024
EPILOGUE_ROWS = 128

def _params(sem=None, vmem=VMEM_LIMIT):
    kw = {"vmem_limit_bytes": vmem}
    if sem is not None:
        kw["dimension_semantics"] = sem
    return pltpu.CompilerParams(**kw)


def _dot(a, b, contract):
    return lax.dot_general(a, b, (contract, ((), ())), preferred_element_type=F32)


ANY_SPEC = pl.BlockSpec(memory_space=pl.ANY)
HBM_SPEC = pl.BlockSpec(memory_space=pltpu.HBM)
SEM_SPEC = pl.BlockSpec(memory_space=pltpu.SEMAPHORE)
DATAFLOW = pltpu.SideEffectType.DATAFLOW_SIDE_EFFECTING


def _hbm(a):
    return pltpu.with_memory_space_constraint(a, pltpu.HBM)


NN = ((1,), (0,))
NT = ((1,), (1,))
TN = ((0,), (0,))


def _mm(name, grid, a, a_spec, b, b_spec, contract, extras, extra_specs, out_shape, out_specs,
        epilogue, acc_shape, acc_as_ref=False, run_after=(), scratch=(),
        semantics=("parallel", "parallel", "arbitrary"), fill=None):
    nk = grid[2]
    n_ex = len(extras)
    n_in = 2 + n_ex + len(run_after) + (fill is not None)
    n_out = len(out_shape)
    n_scr = len(scratch)

    def body(*refs):
        a_ref, b_ref = refs[0], refs[1]
        ex = refs[2:2 + n_ex]
        outs = refs[n_in:n_in + n_out]
        scr = refs[n_in + n_out:n_in + n_out + n_scr]
        if nk == 1:
            epilogue(_dot(a_ref[...], b_ref[...], contract), ex, outs, *scr)
        else:
            acc = refs[n_in + n_out + n_scr]
            k = pl.program_id(2)

            @pl.when(k == 0)
            def _():
                acc[...] = jnp.zeros_like(acc)

            acc[...] += _dot(a_ref[...], b_ref[...], contract)

            @pl.when(k == nk - 1)
            def _():
                epilogue(acc if acc_as_ref else acc[...], ex, outs, *scr)

    acc_scratch = [pltpu.VMEM(acc_shape, F32)] if nk > 1 else []
    filled = [] if fill is None else [fill]
    return pl.pallas_call(
        body, name=name, grid=grid,
        in_specs=[a_spec, b_spec, *extra_specs, *[ANY_SPEC] * (len(run_after) + len(filled))],
        out_specs=out_specs, out_shape=out_shape, scratch_shapes=[*scratch, *acc_scratch],
        input_output_aliases={} if fill is None else {n_in - 1: 0},
        compiler_params=_params(semantics),
    )(a, b, *extras, *run_after, *filled)


def _stats_rows(rows, width):
    idx = lax.broadcasted_iota(jnp.int32, (8, width), 0)
    out = jnp.zeros((8, width), F32)
    for r, v in enumerate(rows):
        out = jnp.where(idx == r, jnp.broadcast_to(v, (8, width)), out)
    return out


def _layer_norm_fwd(z):
    mu = jnp.mean(z, axis=-1, keepdims=True)
    zc = z - mu
    var = jnp.mean(zc * zc, axis=-1, keepdims=True)
    rstd = lax.rsqrt(var + LN_EPS)
    return zc * rstd, rstd


def _layer_norm_bwd(dy, xhat, rstd, g):
    dxh = dy * g
    m1 = jnp.mean(dxh, axis=-1, keepdims=True)
    m2 = jnp.mean(dxh * xhat, axis=-1, keepdims=True)
    return rstd * (dxh - m1 - xhat * m2)


def _heads_scratch(rows, width):
    return pltpu.VMEM((width // HEAD_DIM, rows, HEAD_DIM), F32)


def _to_views(src_ref, view_refs, dtype, heads=None):
    nh, rows, _ = src_ref.shape
    width = nh * HEAD_DIM
    for dil, view_ref in zip(DILATIONS[1:], view_refs):
        for r in range(dil):
            for hh in (range(nh) if heads is None else heads):
                c0 = r * width + hh * HEAD_DIM
                view_ref[:, c0:c0 + HEAD_DIM] = (
                    src_ref[hh, pl.ds(r, rows // dil, stride=dil), :].astype(dtype))


def _from_views(view_refs, dst_refs):
    nh, rows, _ = dst_refs[0].shape
    width = nh * HEAD_DIM
    for dil, view_ref, dst_ref in zip(DILATIONS[1:], view_refs, dst_refs):
        for r in range(dil):
            for hh in range(nh):
                c0 = r * width + hh * HEAD_DIM
                dst_ref[hh, pl.ds(r, rows // dil, stride=dil), :] = (
                    view_ref[:, c0:c0 + HEAD_DIM].astype(F32))


def _view_shape(rows, width, dil, parts=1):
    return (rows // dil, parts * dil * width)


def _in_proj(xb, w_in_st, cos_t, sin_t, aw):
    s, d = xb.shape
    n_sh = w_in_st.shape[2]
    tm, tn = min(s, 1024), aw
    per = n_sh // tn
    grid = (s // tm, (N_CHIPS * n_sh) // tn, 1)

    def epilogue(acc, ex, outs, scr):
        cos_ref, sin_ref = ex
        h_ref, v4_ref, v16_ref = outs
        seg = pl.program_id(1)

        heads = [slice(hh * HEAD_DIM, (hh + 1) * HEAD_DIM) for hh in range(tn // HEAD_DIM)]

        @pl.when(seg < 2)
        def _():
            cos, sin = cos_ref[...], sin_ref[...]
            for hh, sl in enumerate(heads):
                t = acc[:, sl]
                scr[hh] = t * cos + pltpu.roll(t, HEAD_DIM // 2, 1) * sin

        @pl.when(seg == 2)
        def _():
            for hh, sl in enumerate(heads):
                scr[hh] = acc[:, sl]

        @pl.when(seg < 3)
        def _():
            for hh, sl in enumerate(heads):
                h_ref[:, sl] = scr[hh].astype(BF16)
            _to_views(scr, (v4_ref, v16_ref), BF16)

        @pl.when(seg == 3)
        def _():
            h_ref[...] = acc.astype(BF16)

        @pl.when(seg >= 4)
        def _():
            h_ref[...] = (0.5 * jnp.tanh(0.5 * acc) + 0.5).astype(BF16)

    def view_spec(dil):
        return pl.BlockSpec((tm // dil, dil * aw), lambda i, j, k: (i, jnp.minimum(j, 2)))

    return _mm(
        "in_proj", grid, xb, pl.BlockSpec((tm, d), lambda i, j, k: (i, 0)),
        w_in_st, pl.BlockSpec((None, d, tn), lambda i, j, k: (j // per, 0, j % per)), NN,
        [cos_t, sin_t], [pl.BlockSpec((tm, HEAD_DIM), lambda i, j, k: (i, 0))] * 2,
        [jax.ShapeDtypeStruct((s, N_CHIPS * n_sh), BF16)]
        + [jax.ShapeDtypeStruct(_view_shape(s, aw, dil, 3), BF16) for dil in DILATIONS[1:]],
        [pl.BlockSpec((tm, tn), lambda i, j, k: (i, j))] + [view_spec(dil) for dil in DILATIONS[1:]],
        epilogue, None, scratch=[_heads_scratch(tm, aw)],
        semantics=("parallel", "arbitrary", "arbitrary"))


def _pack_heads(cols):
    rows, rep = cols[0].shape[0], HEAD_DIM // len(cols)
    lane = lax.broadcasted_iota(jnp.int32, (rows, HEAD_DIM), 1)
    out = jnp.zeros((rows, HEAD_DIM), F32)
    for hh, col in enumerate(cols):
        out = jnp.where((lane >= hh * rep) & (lane < (hh + 1) * rep), col, out)
    return out


def _head_col(packed, hh, nh):
    lane = lax.broadcasted_iota(jnp.int32, packed.shape, 1)
    return jnp.sum(jnp.where(lane == hh * (HEAD_DIM // nh), packed, 0.0), axis=-1, keepdims=True)


def _band_masks(block_idx):
    qi = lax.broadcasted_iota(jnp.int32, (SUB_BLOCK, 2 * SUB_BLOCK), 0)
    kj = lax.broadcasted_iota(jnp.int32, (SUB_BLOCK, 2 * SUB_BLOCK), 1)
    first_key = jnp.where(block_idx > 0, 0, SUB_BLOCK)
    return (kj >= qi) & (kj <= qi + SUB_BLOCK) & (kj >= first_key)


def _attn_fwd(name, q, k, v, offs, cw, width):
    m = q.shape[0]
    nh = cw // HEAD_DIM
    rb = min(m, ATTN_STEP_ROWS)
    nsub = rb // SUB_BLOCK
    grid = (width // cw, m // rb)
    scale = HEAD_DIM ** -0.5

    def body(q_ref, k_ref, v_ref, o_ref, lse_ref, kprev, vprev):
        n = pl.program_id(1)

        @pl.when(n == 0)
        def _():
            kprev[...] = jnp.zeros_like(kprev)
            vprev[...] = jnp.zeros_like(vprev)

        heads = [slice(hh * HEAD_DIM, (hh + 1) * HEAD_DIM) for hh in range(nh)]
        valids = [_band_masks(n)] + [_band_masks(1)] * (nsub - 1)

        def rows(sub):
            return slice(sub * SUB_BLOCK, (sub + 1) * SUB_BLOCK)

        def cat(prev, ref, sub, sl):
            if sub == 0:
                return jnp.concatenate([prev[:, sl], ref[rows(0), sl]], axis=0)
            return ref[(sub - 1) * SUB_BLOCK:(sub + 1) * SUB_BLOCK, sl]

        scs = [[_dot(q_ref[rows(sub), sl], cat(kprev, k_ref, sub, sl), NT) for sl in heads]
               for sub in range(nsub)]
        for sub in range(nsub):
            lses = []
            for hh, sl in enumerate(heads):
                sc = jnp.where(valids[sub], scs[sub][hh] * scale, NEG)
                mx = jnp.max(sc, axis=-1, keepdims=True)
                p = jnp.exp(sc - mx)
                l = jnp.sum(p, axis=-1, keepdims=True)
                o = _dot(p.astype(BF16), cat(vprev, v_ref, sub, sl), NN) / l
                o_ref[rows(sub), sl] = o.astype(BF16)
                lses.append(mx + jnp.log(l))
            lse_ref[rows(sub), :] = _pack_heads(lses)
        kprev[...] = k_ref[rows(nsub - 1), :]
        vprev[...] = v_ref[rows(nsub - 1), :]

    def cur(off):
        return pl.BlockSpec((rb, cw), lambda j, n: (n, j + off))

    return pl.pallas_call(
        body, name=name, grid=grid,
        in_specs=[cur(offs[0]), cur(offs[1]), cur(offs[2])],
        out_specs=[cur(0), pl.BlockSpec((rb, HEAD_DIM), lambda j, n: (n, j))],
        out_shape=[jax.ShapeDtypeStruct((m, width), BF16),
                   jax.ShapeDtypeStruct((m, width // cw * HEAD_DIM), F32)],
        scratch_shapes=[pltpu.VMEM((SUB_BLOCK, cw), BF16)] * 2,
        compiler_params=_params(("parallel", "arbitrary")),
    )(q, k, v)


def _view_spec(tm, aw, dil):
    return pl.BlockSpec((tm // dil, dil * aw), lambda i: (i, 0))


def _attn_combine(o1, l1, o_views, l_views):
    s, aw = o1.shape
    nh = aw // HEAD_DIM
    tm = min(s, 512)

    def body(o1_ref, l1_ref, o4_ref, o16_ref, l4_ref, l16_ref, o_ref, lse_ref, lse4_ref, lse16_ref,
             so4, so16, sl4, sl16, stot):
        _from_views((o4_ref, o16_ref), (so4, so16))
        _from_views((l4_ref, l16_ref), (sl4, sl16))
        a, b, c = l1_ref[...], sl4[0], sl16[0]
        mx = jnp.maximum(jnp.maximum(a, b), c)
        ea, eb, ec = jnp.exp(a - mx), jnp.exp(b - mx), jnp.exp(c - mx)
        tot = ea + eb + ec
        inv = 1.0 / tot
        wa, wb, wc = ea * inv, eb * inv, ec * inv
        lse_tot = mx + jnp.log(tot)
        stot[0] = lse_tot
        lse_ref[...] = lse_tot
        _to_views(stot, (lse4_ref, lse16_ref), F32)
        for hh in range(nh):
            sl = slice(hh * HEAD_DIM, (hh + 1) * HEAD_DIM)
            o = (_head_col(wa, hh, nh) * o1_ref[:, sl].astype(F32) + _head_col(wb, hh, nh) * so4[hh]
                 + _head_col(wc, hh, nh) * so16[hh])
            o_ref[:, sl] = o.astype(BF16)

    row = pl.BlockSpec((tm, aw), lambda i: (i, 0))
    stat = pl.BlockSpec((tm, HEAD_DIM), lambda i: (i, 0))
    views = [_view_spec(tm, aw, dil) for dil in DILATIONS[1:]]
    stat_views = [_view_spec(tm, HEAD_DIM, dil) for dil in DILATIONS[1:]]
    return pl.pallas_call(
        body, name="attn_combine", grid=(s // tm,), in_specs=[row, stat, *views, *stat_views],
        out_specs=[row, stat, *stat_views],
        out_shape=[jax.ShapeDtypeStruct((s, aw), BF16), jax.ShapeDtypeStruct((s, HEAD_DIM), F32)]
        + [jax.ShapeDtypeStruct(_view_shape(s, HEAD_DIM, dil), F32) for dil in DILATIONS[1:]],
        scratch_shapes=[_heads_scratch(tm, aw)] * 2 + [_heads_scratch(tm, HEAD_DIM)] * 3,
        compiler_params=_params(("parallel",)),
    )(o1, l1, *o_views, *l_views)


def _pool_counts(tm, rows, pgw, row0):
    t = lax.broadcasted_iota(jnp.int32, (rows, len(POOL_WINDOWS) * pgw), 0) + row0
    col = lax.broadcasted_iota(jnp.int32, (rows, len(POOL_WINDOWS) * pgw), 1)
    w = jnp.full((rows, len(POOL_WINDOWS) * pgw), POOL_WINDOWS[0], jnp.int32)
    for g in range(1, len(POOL_WINDOWS)):
        w = jnp.where(col >= g * pgw, POOL_WINDOWS[g], w)
    return jnp.minimum(t + 1, w).astype(F32)


def _window_sums(xs, direction, pgw):
    rows = xs.shape[0]
    acc = xs
    out = None
    col = lax.broadcasted_iota(jnp.int32, xs.shape, 1)
    for g, w in enumerate(POOL_WINDOWS):
        sh = w // 2
        acc = acc + pltpu.roll(acc, sh if direction > 0 else rows - sh, 0)
        out = acc if out is None else jnp.where(col >= g * pgw, acc, out)
    return out


def _pool_fwd(h, wp, scale, aw):
    s = h.shape[0]
    pw_ = aw
    pgw = pw_ // len(POOL_WINDOWS)
    tm = min(s, 512)
    hb = tm // POOL_HALO

    def body(u_ref, halo_ref, wp_ref, sc_ref, p_ref, pw_ref, y_ref):
        i = pl.program_id(0)
        u = u_ref[...].astype(F32)
        halo = halo_ref[...].astype(F32) * jnp.where(i > 0, 1.0, 0.0)
        xs = jnp.concatenate([halo, u], axis=0)
        sums = _window_sums(xs, +1, pgw)[POOL_HALO:]
        p = (sums / _pool_counts(tm, tm, pgw, i * tm) - u).astype(BF16)
        p_ref[...] = p
        sc = sc_ref[...]
        for g in range(len(POOL_WINDOWS)):
            sl = slice(g * pgw, (g + 1) * pgw)
            pw = _dot(p[:, sl], wp_ref[g], NN)
            pw_ref[:, sl] = pw.astype(BF16)
            y_ref[:, sl] = (pw * sc[:, sl]).astype(BF16)

    out = jax.ShapeDtypeStruct((s, pw_), BF16)
    row = pl.BlockSpec((tm, pw_), lambda i: (i, 0))
    return pl.pallas_call(
        body, name="pool_fwd", grid=(s // tm,),
        in_specs=[pl.BlockSpec((tm, pw_), lambda i: (i, 3)),
                  pl.BlockSpec((POOL_HALO, pw_), lambda i: (jnp.maximum(i * hb - 1, 0), 3)),
                  pl.BlockSpec(wp.shape, lambda i: (0, 0, 0)),
                  pl.BlockSpec((1, pw_), lambda i: (0, 0))],
        out_specs=[row, row, row], out_shape=[out, out, out],
        compiler_params=_params(("parallel",)),
    )(h, h, wp, scale)


def _branch_merge(o_attn, y, wba_st, wbp_st, h, aw, d):
    s = o_attn.shape[0]
    tn = wba_st.shape[2]
    tm = min(s, 1024)
    ga0 = 4 * aw // tn
    gp0 = (4 * aw + d) // tn

    def body(o_ref, y_ref, wa_ref, wp_ref, sga_ref, sgp_ref, ya_ref, yp_ref, mg_ref):
        ya = _dot(o_ref[...], wa_ref[...], NN)
        yp = _dot(y_ref[...], wp_ref[...], NN)
        ya_ref[...] = ya.astype(BF16)
        yp_ref[...] = yp.astype(BF16)
        mg_ref[...] = (sga_ref[...].astype(F32) * ya + sgp_ref[...].astype(F32) * yp).astype(BF16)

    out = jax.ShapeDtypeStruct((s, d), BF16)
    blk = pl.BlockSpec((tm, tn), lambda i, j: (i, j))
    return pl.pallas_call(
        body, name="branch_merge", grid=(s // tm, N_CHIPS),
        in_specs=[pl.BlockSpec((tm, aw), lambda i, j: (i, 0)),
                  pl.BlockSpec((tm, aw), lambda i, j: (i, 0)),
                  pl.BlockSpec((None, aw, tn), lambda i, j: (j, 0, 0)),
                  pl.BlockSpec((None, aw, tn), lambda i, j: (j, 0, 0)),
                  pl.BlockSpec((tm, tn), lambda i, j: (i, j + ga0)),
                  pl.BlockSpec((tm, tn), lambda i, j: (i, j + gp0))],
        out_specs=[blk, blk, blk], out_shape=[out, out, out],
        compiler_params=_params(("parallel", "parallel")),
    )(o_attn, y, wba_st, wbp_st, h, h)


def _mix_norm(merged, w_out, x, g1, b1):
    s, d = x.shape
    tm = min(s, 256)

    def epilogue(acc, ex, outs):
        x_ref, g_ref, b_ref = ex
        xh_ref, rs_ref, xb_ref = outs
        xhat, rstd = _layer_norm_fwd(ALPHA * x_ref[...] + acc)
        xh_ref[...] = xhat
        rs_ref[...] = rstd
        xb_ref[...] = (xhat * g_ref[...] + b_ref[...]).astype(BF16)

    row = pl.BlockSpec((tm, d), lambda i, j, k: (i, 0))
    vec = pl.BlockSpec((1, d), lambda i, j, k: (0, 0))
    return _mm(
        "mix_norm", (s // tm, 1, 1), merged, row, w_out, pl.BlockSpec((d, d), lambda i, j, k: (0, 0)),
        NN, [x, g1, b1], [row, vec, vec],
        [jax.ShapeDtypeStruct((s, d), F32), jax.ShapeDtypeStruct((s, 1), F32),
         jax.ShapeDtypeStruct((s, d), BF16)],
        [row, pl.BlockSpec((tm, 1), lambda i, j, k: (i, 0)), row], epilogue, None)


def _ff_up(x1b, w1_st):
    s, d = x1b.shape
    n_sh = w1_st.shape[2]
    tm, tn = min(s, 1024), min(n_sh, 1024)
    per = n_sh // tn

    def epilogue(acc, ex, outs):
        r = jnp.maximum(acc, 0.0)
        outs[0][...] = (r * r).astype(BF16)
        outs[1][...] = (2.0 * r).astype(BF16)

    blk = pl.BlockSpec((tm, tn), lambda i, j, k: (i, j))
    out = jax.ShapeDtypeStruct((s, N_CHIPS * n_sh), BF16)
    return _mm(
        "ff_up", (s // tm, N_CHIPS * per, 1), x1b, pl.BlockSpec((tm, d), lambda i, j, k: (i, 0)),
        w1_st, pl.BlockSpec((None, d, tn), lambda i, j, k: (j // per, 0, j % per)), NN, [], [],
        [out, out], [blk, blk], epilogue, None)


def _ff_down_loss(r, w2, xhat1, g1, b1, g2, b2, target):
    s, d = xhat1.shape
    dff = r.shape[1]
    tm, tk = min(s, 512), min(dff, 2048)
    ch = min(tm, EPILOGUE_ROWS)

    def epilogue(acc_ref, ex, outs):
        xh1_ref, g1_ref, b1_ref, g2_ref, b2_ref, t_ref = ex
        dz_ref, dzb_ref, st_ref = outs
        g1v, b1v, g2v, b2v = g1_ref[...], b1_ref[...], g2_ref[...], b2_ref[...]
        dg = db = loss = None
        for c in range(tm // ch):
            rows = slice(c * ch, (c + 1) * ch)
            x1 = xh1_ref[rows, :] * g1v + b1v
            xhat2, rstd2 = _layer_norm_fwd(ALPHA * x1 + acc_ref[rows, :])
            err = xhat2 * g2v + b2v - t_ref[rows, :]
            dy = err * (1.0 / d)
            dz = _layer_norm_bwd(dy, xhat2, rstd2, g2v)
            dz_ref[rows, :] = dz
            dzb_ref[rows, :] = dz.astype(BF16)
            parts = (jnp.sum(dy * xhat2, axis=0, keepdims=True), jnp.sum(dy, axis=0, keepdims=True),
                     jnp.sum(jnp.sum(err * err, axis=-1, keepdims=True), axis=0, keepdims=True))
            dg, db, loss = parts if c == 0 else (dg + parts[0], db + parts[1], loss + parts[2])
        st_ref[...] = _stats_rows([dg, db, jnp.broadcast_to((0.5 / d) * loss, (1, d))], d)

    row = pl.BlockSpec((tm, d), lambda i, j, k: (i, 0))
    vec = pl.BlockSpec((1, d), lambda i, j, k: (0, 0))
    return _mm(
        "ff_down_loss", (s // tm, 1, dff // tk), r, pl.BlockSpec((tm, tk), lambda i, j, k: (i, k)),
        w2, pl.BlockSpec((tk, d), lambda i, j, k: (k, 0)), NN,
        [xhat1, g1, b1, g2, b2, target], [row, vec, vec, vec, vec, row],
        [jax.ShapeDtypeStruct((s, d), F32), jax.ShapeDtypeStruct((s, d), BF16),
         jax.ShapeDtypeStruct((s // tm, 8, d), F32)],
        [row, row, pl.BlockSpec((None, 8, d), lambda i, j, k: (i, 0, 0))], epilogue, (tm, d),
        acc_as_ref=True)


def _ff_down_bwd(dz2b, w2, r_slope):
    s, d = dz2b.shape
    dff = r_slope.shape[1]
    tm, tn = min(s, 1024), min(dff, 1024)

    def epilogue(acc, ex, outs):
        outs[0][...] = (acc * ex[0][...].astype(F32)).astype(BF16)

    blk = pl.BlockSpec((tm, tn), lambda i, j, k: (i, j))
    return _mm(
        "ff_down_bwd", (s // tm, dff // tn, 1), dz2b, pl.BlockSpec((tm, d), lambda i, j, k: (i, 0)),
        w2, pl.BlockSpec((tn, d), lambda i, j, k: (j, 0)), NT, [r_slope], [blk],
        [jax.ShapeDtypeStruct((s, dff), BF16)], [blk], epilogue, None)[0]


def _wgrad(name, a, g, n_sh, shard0=0, n_shards=None, fill=None):
    s, rows = a.shape
    cols = g.shape[1]
    tm, tn, tk = min(rows, 2048), min(cols, 1024), min(s, 2048)
    if tn >= n_sh:
        span = tn // n_sh
        out_spec = pl.BlockSpec((span, tm, n_sh), lambda i, j, k: (j + shard0 // span, i, 0))

        def epilogue(acc_ref, ex, outs):
            for sh in range(span):
                outs[0][sh] = acc_ref[:, sh * n_sh:(sh + 1) * n_sh].astype(BF16)
    else:
        per = n_sh // tn
        out_spec = pl.BlockSpec((None, tm, tn), lambda i, j, k: (shard0 + j // per, i, j % per))

        def epilogue(acc_ref, ex, outs):
            outs[0][...] = acc_ref[...].astype(BF16)

    return _mm(
        name, (rows // tm, cols // tn, s // tk), a, pl.BlockSpec((tk, tm), lambda i, j, k: (k, i)),
        g, pl.BlockSpec((tk, tn), lambda i, j, k: (k, j)), TN, [], [],
        [jax.ShapeDtypeStruct((n_shards or cols // n_sh, rows, n_sh), BF16)], [out_spec], epilogue,
        (tm, tn), acc_as_ref=True, fill=fill)[0]


def _ff_up_bwd(da, w1_st, dz2, xhat1, rstd1, g1, run_after):
    s, d = dz2.shape
    n_sh = w1_st.shape[2]
    tm, tk = min(s, 512), min(n_sh, 2048)
    per = n_sh // tk
    ch = min(tm, EPILOGUE_ROWS)

    def epilogue(acc_ref, ex, outs):
        dz2_ref, xh_ref, rs_ref, g_ref = ex
        dz_ref, dzb_ref, st_ref = outs
        gv = g_ref[...]
        dg = db = None
        for c in range(tm // ch):
            rows = slice(c * ch, (c + 1) * ch)
            dx1 = ALPHA * dz2_ref[rows, :] + acc_ref[rows, :]
            xhat = xh_ref[rows, :]
            dz = _layer_norm_bwd(dx1, xhat, rs_ref[rows, :], gv)
            dz_ref[rows, :] = dz
            dzb_ref[rows, :] = dz.astype(BF16)
            parts = (jnp.sum(dx1 * xhat, axis=0, keepdims=True), jnp.sum(dx1, axis=0, keepdims=True))
            dg, db = parts if c == 0 else (dg + parts[0], db + parts[1])
        st_ref[...] = _stats_rows([dg, db], d)

    row = pl.BlockSpec((tm, d), lambda i, j, k: (i, 0))
    return _mm(
        "ff_up_bwd", (s // tm, 1, N_CHIPS * per), da, pl.BlockSpec((tm, tk), lambda i, j, k: (i, k)),
        w1_st, pl.BlockSpec((None, d, tk), lambda i, j, k: (k // per, 0, k % per)), NT,
        [dz2, xhat1, rstd1, g1],
        [row, row, pl.BlockSpec((tm, 1), lambda i, j, k: (i, 0)), pl.BlockSpec((1, d), lambda i, j, k: (0, 0))],
        [jax.ShapeDtypeStruct((s, d), F32), jax.ShapeDtypeStruct((s, d), BF16),
         jax.ShapeDtypeStruct((s // tm, 8, d), F32)],
        [row, row, pl.BlockSpec((None, 8, d), lambda i, j, k: (i, 0, 0))], epilogue, (tm, d),
        acc_as_ref=True, run_after=run_after)


def _mix_bwd(dz1b, w_out, h, ya, yp, aw):
    s, d = dz1b.shape
    tm = min(s, 256)
    gblk = 4 * aw // d

    def epilogue(acc, ex, outs):
        sga_ref, sgp_ref, ya_ref, yp_ref = ex
        dya_ref, dyp_ref, dg_ref = outs
        sga, sgp = sga_ref[...].astype(F32), sgp_ref[...].astype(F32)
        dya_ref[...] = (acc * sga).astype(BF16)
        dyp_ref[...] = (acc * sgp).astype(BF16)
        dg_ref[:, :d] = (acc * ya_ref[...].astype(F32) * (sga * (1.0 - sga))).astype(BF16)
        dg_ref[:, d:] = (acc * yp_ref[...].astype(F32) * (sgp * (1.0 - sgp))).astype(BF16)

    row = pl.BlockSpec((tm, d), lambda i, j, k: (i, 0))
    return _mm(
        "mix_bwd", (s // tm, 1, 1), dz1b, row, w_out, pl.BlockSpec((d, d), lambda i, j, k: (0, 0)), NT,
        [h, h, ya, yp],
        [pl.BlockSpec((tm, d), lambda i, j, k: (i, gblk)),
         pl.BlockSpec((tm, d), lambda i, j, k: (i, gblk + 1)), row, row],
        [jax.ShapeDtypeStruct((s, d), BF16), jax.ShapeDtypeStruct((s, d), BF16),
         jax.ShapeDtypeStruct((s, 2 * d), BF16)],
        [row, row, pl.BlockSpec((tm, 2 * d), lambda i, j, k: (i, 0))], epilogue, None)


def _branch_in_bwd(name, tm, dyb, wb_st, epilogue, extras, extra_specs, out_shape, out_specs,
                   scratch=()):
    s, d = dyb.shape
    aw = wb_st.shape[1]
    wb_t = wb_st.transpose(0, 2, 1).reshape(d, aw)
    return _mm(
        name, (s // tm, 1, 1), dyb, pl.BlockSpec((tm, d), lambda i, j, k: (i, 0)),
        wb_t, pl.BlockSpec((d, aw), lambda i, j, k: (0, 0)), NN,
        extras, extra_specs, out_shape, out_specs, epilogue, None, scratch=scratch)


def _attn_out_bwd(dya, wba_st, o_attn):
    s, aw = o_attn.shape
    tm = min(s, 512)

    def epilogue(acc_ref, ex, outs, sdo, sdl):
        do_ref, dl_ref, do4_ref, do16_ref, dl4_ref, dl16_ref = outs
        deltas = []
        for hh in range(aw // HEAD_DIM):
            sl = slice(hh * HEAD_DIM, (hh + 1) * HEAD_DIM)
            do = acc_ref[:, sl]
            deltas.append(jnp.sum(do * ex[0][:, sl].astype(F32), axis=-1, keepdims=True))
            sdo[hh] = do
            do_ref[:, sl] = do.astype(BF16)
        packed = _pack_heads(deltas)
        sdl[0] = packed
        dl_ref[...] = packed
        _to_views(sdo, (do4_ref, do16_ref), BF16)
        _to_views(sdl, (dl4_ref, dl16_ref), F32)

    def specs(width):
        return ([pl.BlockSpec((tm, width), lambda i, j, k: (i, 0))]
                + [pl.BlockSpec((tm // dil, dil * width), lambda i, j, k: (i, 0)) for dil in DILATIONS[1:]])

    def shapes(width, dtype):
        return ([jax.ShapeDtypeStruct((s, width), dtype)]
                + [jax.ShapeDtypeStruct(_view_shape(s, width, dil), dtype) for dil in DILATIONS[1:]])

    do_specs, dl_specs = specs(aw), specs(HEAD_DIM)
    do_shapes, dl_shapes = shapes(aw, BF16), shapes(HEAD_DIM, F32)
    return _branch_in_bwd(
        "attn_out_bwd", tm, dya, wba_st, epilogue, [o_attn], [do_specs[0]],
        [do_shapes[0], dl_shapes[0], *do_shapes[1:], *dl_shapes[1:]],
        [do_specs[0], dl_specs[0], *do_specs[1:], *dl_specs[1:]],
        scratch=[_heads_scratch(tm, aw), _heads_scratch(tm, HEAD_DIM)])


def _pool_out_bwd(dyp, wbp_st, pw, scale):
    s, pw_ = pw.shape
    tm = min(s, 1024)

    def epilogue(acc_ref, ex, outs):
        pw_ref, sc_ref = ex
        dpw_ref, st_ref = outs
        acc = acc_ref[...]
        dpw_ref[...] = (acc * sc_ref[...]).astype(BF16)
        st_ref[...] = _stats_rows([jnp.sum(acc * pw_ref[...].astype(F32), axis=0, keepdims=True)], pw_)

    row = pl.BlockSpec((tm, pw_), lambda i, j, k: (i, 0))
    return _branch_in_bwd(
        "pool_out_bwd", tm, dyp, wbp_st, epilogue, [pw, scale],
        [row, pl.BlockSpec((1, pw_), lambda i, j, k: (0, 0))],
        [jax.ShapeDtypeStruct((s, pw_), BF16), jax.ShapeDtypeStruct((s // tm, 8, pw_), F32)],
        [row, pl.BlockSpec((None, 8, pw_), lambda i, j, k: (i, 0, 0))])


def _pool_bwd(dpw, p, wp):
    s, pw_ = p.shape
    ng = len(POOL_WINDOWS)
    pgw = pw_ // ng
    tm = min(s, 512)
    hb = tm // POOL_HALO
    nblk = s // tm

    def body(dpw_ref, nxt_ref, p_ref, wp_ref, dwp_ref, du_ref):
        i = pl.program_id(0)
        nxt = (nxt_ref[...].astype(F32) * jnp.where(i < nblk - 1, 1.0, 0.0)).astype(BF16)
        dpw_all = jnp.concatenate([dpw_ref[...], nxt], axis=0)

        @pl.when(i == 0)
        def _():
            dwp_ref[...] = jnp.zeros_like(dwp_ref)

        dps = []
        for g in range(ng):
            sl = slice(g * pgw, (g + 1) * pgw)
            dwp_ref[g] += _dot(p_ref[:, sl], dpw_ref[:, sl], TN)
            dps.append(_dot(dpw_all[:, sl], wp_ref[g], NT))
        dp = jnp.concatenate(dps, axis=1)
        dpn = dp / _pool_counts(tm, tm + POOL_HALO, pgw, i * tm)
        du_ref[...] = (_window_sums(dpn, -1, pgw)[:tm] - dp[:tm]).astype(BF16)

    row = pl.BlockSpec((tm, pw_), lambda i: (i, 0))
    full = pl.BlockSpec((ng, pgw, pgw), lambda i: (0, 0, 0))
    return pl.pallas_call(
        body, name="pool_bwd", grid=(nblk,),
        in_specs=[row, pl.BlockSpec((POOL_HALO, pw_), lambda i: (jnp.minimum((i + 1) * hb, s // POOL_HALO - 1), 0)),
                  row, full],
        out_specs=[full, row],
        out_shape=[jax.ShapeDtypeStruct((ng, pgw, pgw), F32), jax.ShapeDtypeStruct((s, pw_), BF16)],
        compiler_params=_params(("arbitrary",)),
    )(dpw, dpw, p, wp)


def _attn_bwd(name, q, k, v, do, lse, delta, offs, cw, width):
    m = do.shape[0]
    nh = cw // HEAD_DIM
    rb = min(m, ATTN_STEP_ROWS)
    nsub = rb // SUB_BLOCK
    nstep = m // rb
    single = nstep == 1
    grid = (width // cw, nstep + (not single))
    scale = HEAD_DIM ** -0.5
    last = slice(rb - SUB_BLOCK, rb)

    def body(q_ref, k_ref, v_ref, do_ref, lse_ref, dl_ref, dq_ref, dk_ref, dv_ref,
             kprev, vprev, dk_carry, dv_carry):
        n = pl.program_id(1)

        @pl.when(n == 0)
        def _():
            for ref in (kprev, vprev, dk_carry, dv_carry):
                ref[...] = jnp.zeros_like(ref)

        qi = lax.broadcasted_iota(jnp.int32, (SUB_BLOCK, 2 * SUB_BLOCK), 0)
        kj = lax.broadcasted_iota(jnp.int32, (SUB_BLOCK, 2 * SUB_BLOCK), 1)
        band = (kj >= qi) & (kj <= qi + SUB_BLOCK)
        flush = jnp.where(n == nstep, 4 * SUB_BLOCK, 0)
        valids = [band & (kj >= jnp.where(n == 0, SUB_BLOCK, flush))] + [band & (kj >= flush)] * (nsub - 1)
        heads = [slice(hh * HEAD_DIM, (hh + 1) * HEAD_DIM) for hh in range(nh)]

        def rows(sub):
            return slice(sub * SUB_BLOCK, (sub + 1) * SUB_BLOCK)

        def cat(prev, ref, sub, sl):
            if sub == 0:
                return jnp.concatenate([prev[:, sl], ref[rows(0), sl]], axis=0)
            return ref[(sub - 1) * SUB_BLOCK:(sub + 1) * SUB_BLOCK, sl]

        kcats = [[cat(kprev, k_ref, sub, sl) for sl in heads] for sub in range(nsub)]
        scs = [[_dot(q_ref[rows(sub), sl], kcats[sub][hh], NT) for hh, sl in enumerate(heads)]
               for sub in range(nsub)]
        dps = [[_dot(do_ref[rows(sub), sl], cat(vprev, v_ref, sub, sl), NT) for sl in heads]
               for sub in range(nsub)]
        dqs = []
        stats = [(lse_ref[rows(sub), :], dl_ref[rows(sub), :]) for sub in range(nsub)]
        for hh, sl in enumerate(heads):
            dk2, dv2 = [], []
            for sub in range(nsub):
                lse_h, dl_h = _head_col(stats[sub][0], hh, nh), _head_col(stats[sub][1], hh, nh)
                valid = valids[sub]
                p = jnp.where(valid, jnp.exp(jnp.where(valid, scs[sub][hh] * scale, NEG) - lse_h), 0.0)
                ds = (p * (dps[sub][hh] - dl_h)).astype(BF16)
                dqs.append((sub, sl, (_dot(ds, kcats[sub][hh], NN) * scale).astype(BF16)))
                dk2.append(_dot(ds, q_ref[rows(sub), sl], TN) * scale)
                dv2.append(_dot(p.astype(BF16), do_ref[rows(sub), sl], TN))
            for out_ref, carry, new in ((dk_ref, dk_carry, dk2), (dv_ref, dv_carry, dv2)):
                if not single:
                    out_ref[last, sl] = (carry[last, sl] + new[0][:SUB_BLOCK]).astype(BF16)
                    if nsub > 1:
                        out_ref[:rb - SUB_BLOCK, sl] = carry[:rb - SUB_BLOCK, sl].astype(BF16)
                for sub in range(nsub):
                    val = new[sub][SUB_BLOCK:]
                    if sub + 1 < nsub:
                        val = val + new[sub + 1][:SUB_BLOCK]
                    if single:
                        out_ref[rows(sub), sl] = val.astype(BF16)
                    else:
                        carry[rows(sub), sl] = val

        @pl.when(n < nstep)
        def _():
            for sub, sl, val in dqs:
                dq_ref[rows(sub), sl] = val

        kprev[...] = k_ref[last, :]
        vprev[...] = v_ref[last, :]

    def cur(off):
        return pl.BlockSpec((rb, cw), lambda j, n: (jnp.minimum(n, nstep - 1), j + off))

    lagged = pl.BlockSpec((rb, cw), lambda j, n: (jnp.maximum(n - 1, 0), j))
    stat = pl.BlockSpec((rb, HEAD_DIM), lambda j, n: (jnp.minimum(n, nstep - 1), j))
    out = jax.ShapeDtypeStruct((m, width), BF16)
    return pl.pallas_call(
        body, name=name, grid=grid,
        in_specs=[cur(offs[0]), cur(offs[1]), cur(offs[2]), cur(0), stat, stat],
        out_specs=[cur(0), lagged, lagged], out_shape=[out, out, out],
        scratch_shapes=[pltpu.VMEM((SUB_BLOCK, cw), BF16)] * 2 + [pltpu.VMEM((rb, cw), F32)] * 2,
        compiler_params=_params(("parallel", "arbitrary")),
    )(q, k, v, do, lse, delta)


def _qkvu_grad(d1, d4, d16, du, cos_t, sin_t):
    s, aw = du.shape
    tm = min(s, 512)

    def body(*refs):
        nat, v4, v16 = refs[0:3], refs[3:6], refs[6:9]
        cos_ref, sin_ref, du_ref, out_ref, s4, s16 = refs[9:]
        cos, sin = cos_ref[...], sin_ref[...]
        for part in range(3):
            _from_views((v4[part], v16[part]), (s4, s16))
            for hh in range(aw // HEAD_DIM):
                sl = slice(hh * HEAD_DIM, (hh + 1) * HEAD_DIM)
                t = nat[part][:, sl].astype(F32) + s4[hh] + s16[hh]
                if part < 2:
                    t = t * cos - pltpu.roll(t, HEAD_DIM // 2, 1) * sin
                out_ref[:, part * aw + hh * HEAD_DIM:part * aw + (hh + 1) * HEAD_DIM] = t.astype(BF16)
        out_ref[:, 3 * aw:] = du_ref[...]

    row = pl.BlockSpec((tm, aw), lambda i: (i, 0))
    tab = pl.BlockSpec((tm, HEAD_DIM), lambda i: (i, 0))
    return pl.pallas_call(
        body, name="qkvu_grad", grid=(s // tm,),
        in_specs=[row] * 3 + [_view_spec(tm, aw, 4)] * 3 + [_view_spec(tm, aw, 16)] * 3 + [tab, tab, row],
        out_specs=pl.BlockSpec((tm, 4 * aw), lambda i: (i, 0)),
        out_shape=jax.ShapeDtypeStruct((s, 4 * aw), BF16),
        scratch_shapes=[_heads_scratch(tm, aw)] * 2,
        compiler_params=_params(("parallel",)),
    )(*d1, *d4, *d16, cos_t, sin_t, du)


def _in_proj_bwd_x(name, dh, w_in_st, shard0, base, scale_base, run_after=()):
    s, kdim = dh.shape
    d, n_sh = w_in_st.shape[1], w_in_st.shape[2]
    tm, tk = min(s, 512), min(n_sh, 2048)
    per = n_sh // tk

    ch = min(tm, 2 * EPILOGUE_ROWS)

    def epilogue(acc_ref, ex, outs):
        for c in range(tm // ch):
            rows = slice(c * ch, (c + 1) * ch)
            outs[0][rows, :] = scale_base * ex[0][rows, :] + acc_ref[rows, :]

    row = pl.BlockSpec((tm, d), lambda i, j, k: (i, 0))
    return _mm(
        name, (s // tm, 1, kdim // tk), dh, pl.BlockSpec((tm, tk), lambda i, j, k: (i, k)),
        w_in_st, pl.BlockSpec((None, d, tk), lambda i, j, k: (shard0 + k // per, 0, k % per)), NT,
        [base], [row], [jax.ShapeDtypeStruct((s, d), F32)], [row], epilogue, (tm, d),
        acc_as_ref=True, run_after=run_after)[0]


def _chip_peers():
    x, y, c = lax.axis_index("x"), lax.axis_index("y"), lax.axis_index("c")
    return x, y, c, [(1 - x, y), (x, 1 - y), (1 - x, 1 - y)]


GATHER, GATHER_HALF, SCATTER, SIBLING = "gather", "gather_half", "scatter", "sibling"


def _exchange_peers(mode):
    x, y, c, chips = _chip_peers()
    if mode == SIBLING:
        return x, y, c, [(x, y, 1 - c)]
    return x, y, c, [(px, py, c) for px, py in chips]


def _core_half(ref_or_shape, c):
    rows = (ref_or_shape.shape[0]) // 2
    return pl.ds(c * rows, rows)


def _exchange_descriptor(mode, src, land, send, recv, p, peer, me, arriving):
    pid = 2 * peer[0] + peer[1]
    if mode == GATHER:
        src_ref, dst_ref = src, land.at[pid if arriving else me]
    elif mode == GATHER_HALF:
        rows = _core_half(src, peer[2])
        src_ref, dst_ref = src.at[rows], land.at[pid if arriving else me, rows]
    elif mode == SCATTER:
        src_ref, dst_ref = src.at[pid], land.at[p]
    else:
        src_ref, dst_ref = src, land
    return pltpu.make_async_remote_copy(
        src_ref=src_ref, dst_ref=dst_ref, send_sem=send.at[p], recv_sem=recv.at[p],
        device_id=peer, device_id_type=MESH)


def _exchange_start(name, mode, srcs, land_shapes):
    n = len(srcs)
    lands = [_hbm(lax.empty(shape, src.dtype)) for shape, src in zip(land_shapes, srcs)]

    def body(*refs):
        src_refs, land_refs = refs[:n], refs[n:2 * n]
        sends, recvs = refs[2 * n:3 * n], refs[3 * n:4 * n]
        token = refs[6 * n]
        x, y, c, peers = _exchange_peers(mode)
        me = 2 * x + y
        for w in range(n):
            for p, peer in enumerate(peers):
                _exchange_descriptor(mode, src_refs[w], land_refs[w], sends[w], recvs[w], p, peer,
                                     me, arriving=False).start()
        token[...] = jnp.zeros_like(token)

    sem = pltpu.SemaphoreType.DMA((3,))
    outs = pl.pallas_call(
        body, name=name, in_specs=[HBM_SPEC] * (2 * n),
        out_specs=[SEM_SPEC] * (2 * n) + [HBM_SPEC] * (2 * n) + [pl.BlockSpec(memory_space=pltpu.VMEM)],
        out_shape=[sem] * (2 * n) + [pltpu.HBM(a.shape, a.dtype) for a in (*srcs, *lands)]
        + [jax.ShapeDtypeStruct((8, 128), F32)],
        input_output_aliases={i: 2 * n + i for i in range(2 * n)},
        compiler_params=pltpu.CompilerParams(has_side_effects=DATAFLOW),
    )(*[_hbm(a) for a in srcs], *lands)
    return {"send": outs[:n], "recv": outs[n:2 * n], "src": outs[2 * n:3 * n],
            "land": outs[3 * n:4 * n], "token": outs[4 * n]}


def _exchange_wait(name, mode, started, which, after):
    m = len(which)

    def body(*refs):
        src_refs, land_refs = refs[:m], refs[m:2 * m]
        sends, recvs = refs[2 * m:3 * m], refs[3 * m:4 * m]
        x, y, c, peers = _exchange_peers(mode)
        me = 2 * x + y
        for w in range(m):
            for p, peer in enumerate(peers):
                _exchange_descriptor(mode, src_refs[w], land_refs[w], sends[w], recvs[w], p, peer,
                                     me, arriving=False).wait_send()
                _exchange_descriptor(mode, src_refs[w], land_refs[w], sends[w], recvs[w], p, peer,
                                     me, arriving=True).wait_recv()

    pick = lambda key: [started[key][w] for w in which]
    bufs = pick("src") + pick("land")
    after = list(after) if isinstance(after, (list, tuple)) else [after]
    outs = pl.pallas_call(
        body, name=name,
        in_specs=[HBM_SPEC] * (2 * m) + [SEM_SPEC] * (2 * m) + [ANY_SPEC] * len(after),
        out_specs=[HBM_SPEC] * (2 * m), out_shape=[pltpu.HBM(a.shape, a.dtype) for a in bufs],
        input_output_aliases={i: i for i in range(2 * m)},
        compiler_params=pltpu.CompilerParams(has_side_effects=DATAFLOW),
    )(*bufs, *pick("send"), *pick("recv"), *after)
    return outs[:m], outs[m:]


def _to_bf16(name, a, run_after):
    r, c = a.shape
    tm = min(r, 512)

    def body(a_ref, after_ref, out_ref):
        out_ref[...] = a_ref[...].astype(BF16)

    blk = pl.BlockSpec((tm, c), lambda i: (i, 0))
    return pl.pallas_call(
        body, name=name, grid=(r // tm,), in_specs=[blk, ANY_SPEC], out_specs=blk,
        out_shape=jax.ShapeDtypeStruct((r, c), BF16), compiler_params=_params(("parallel",)),
    )(a, run_after)


def _swap_halves(name, lands):
    n = len(lands)

    def body(*refs):
        bufs = refs[n:2 * n]
        send, recv = refs[2 * n:]
        x, y, c, chips = _chip_peers()
        started = []
        for w in range(n):
            half = bufs[w].shape[1] // 2
            for p, (px, py) in enumerate(chips):
                mine = bufs[w].at[2 * px + py, pl.ds(c * half, half)]
                cp = pltpu.make_async_remote_copy(
                    src_ref=mine, dst_ref=mine, send_sem=send.at[w, p], recv_sem=recv.at[w, p],
                    device_id=(x, y, 1 - c), device_id_type=MESH)
                cp.start()
                started.append(cp)
        for w in range(n):
            half = bufs[w].shape[1] // 2
            for p, (px, py) in enumerate(chips):
                theirs = bufs[w].at[2 * px + py, pl.ds((1 - c) * half, half)]
                pltpu.make_async_remote_copy(
                    src_ref=theirs, dst_ref=theirs, send_sem=send.at[w, p], recv_sem=recv.at[w, p],
                    device_id=(x, y, 1 - c), device_id_type=MESH).wait_recv()
        for cp in started:
            cp.wait_send()

    return pl.pallas_call(
        body, name=name, in_specs=[ANY_SPEC] * n, out_specs=[ANY_SPEC] * n,
        out_shape=[jax.ShapeDtypeStruct(a.shape, a.dtype) for a in lands],
        input_output_aliases={i: i for i in range(n)},
        scratch_shapes=[pltpu.SemaphoreType.DMA((n, 3)), pltpu.SemaphoreType.DMA((n, 3))],
    )(*lands)


def _place_own(name, shard, land, me):
    r, c = shard.shape
    tm = min(r, 512)

    def body(me_ref, shard_ref, land_ref, out_ref):
        out_ref[...] = shard_ref[...]

    return pl.pallas_call(
        body, name=name,
        grid_spec=pltpu.PrefetchScalarGridSpec(
            num_scalar_prefetch=1, grid=(r // tm,),
            in_specs=[pl.BlockSpec((tm, c), lambda i, me_ref: (i, 0)), ANY_SPEC],
            out_specs=pl.BlockSpec((None, tm, c), lambda i, me_ref: (me_ref[0], i, 0))),
        out_shape=jax.ShapeDtypeStruct(land.shape, land.dtype), input_output_aliases={2: 0},
        compiler_params=_params(("arbitrary",)),
    )(me, shard, land)


def _sum_slabs(name, grads, land, me):
    _, r, c = grads.shape
    tm = min(r, 256)

    def body(me_ref, own_ref, land_ref, out_ref):
        acc = own_ref[...].astype(F32)
        for p in range(3):
            acc = acc + land_ref[p].astype(F32)
        out_ref[...] = acc

    return pl.pallas_call(
        body, name=name,
        grid_spec=pltpu.PrefetchScalarGridSpec(
            num_scalar_prefetch=1, grid=(r // tm,),
            in_specs=[pl.BlockSpec((None, tm, c), lambda i, me_ref: (me_ref[0], i, 0)),
                      pl.BlockSpec((3, tm, c), lambda i, me_ref: (0, i, 0))],
            out_specs=pl.BlockSpec((tm, c), lambda i, me_ref: (i, 0))),
        out_shape=jax.ShapeDtypeStruct((r, c), F32), compiler_params=_params(("parallel",)),
    )(me, grads, land)


def _allreduce_stats(stats, run_after):
    n = len(stats)

    def body(*refs):
        ins, outs = refs[:n], refs[n + 1:2 * n + 1]
        mine, gath = refs[2 * n + 1:3 * n + 1], refs[3 * n + 1:4 * n + 1]
        send, recv = refs[4 * n + 1:]
        x, y, c = lax.axis_index("x"), lax.axis_index("y"), lax.axis_index("c")
        me = 4 * x + 2 * y + c
        flips = [(bx, by, bc) for bx in (0, 1) for by in (0, 1) for bc in (0, 1)][1:]

        def peer(f):
            return (x + f[0] * (1 - 2 * x), y + f[1] * (1 - 2 * y), c + f[2] * (1 - 2 * c))

        copies = []
        for t in range(n):
            tot = ins[t][0]
            for b in range(1, ins[t].shape[0]):
                tot = tot + ins[t][b]
            mine[t][...] = tot
            gath[t][me] = tot
            for k, f in enumerate(flips):
                cp = pltpu.make_async_remote_copy(
                    src_ref=mine[t], dst_ref=gath[t].at[me], send_sem=send.at[t, k],
                    recv_sem=recv.at[t, k], device_id=peer(f), device_id_type=MESH)
                cp.start()
                copies.append(cp)
        for t in range(n):
            for k, f in enumerate(flips):
                px, py, pc = peer(f)
                pltpu.make_async_remote_copy(
                    src_ref=mine[t], dst_ref=gath[t].at[4 * px + 2 * py + pc], send_sem=send.at[t, k],
                    recv_sem=recv.at[t, k], device_id=(px, py, pc), device_id_type=MESH).wait_recv()
        for cp in copies:
            cp.wait_send()
        for t in range(n):
            tot = gath[t][0]
            for dev in range(1, 8):
                tot = tot + gath[t][dev]
            outs[t][...] = tot

    vm = pl.BlockSpec(memory_space=pltpu.VMEM)
    return pl.pallas_call(
        body, name="allreduce_stats", in_specs=[vm] * n + [ANY_SPEC], out_specs=[vm] * n,
        out_shape=[jax.ShapeDtypeStruct(s.shape[1:], F32) for s in stats],
        scratch_shapes=[pltpu.VMEM(s.shape[1:], F32) for s in stats]
        + [pltpu.VMEM((8, *s.shape[1:]), F32) for s in stats]
        + [pltpu.SemaphoreType.DMA((n, 7)), pltpu.SemaphoreType.DMA((n, 7))],
    )(*stats, run_after)


def _adamw(name, w, m, v, g_parts):
    r, c = w.shape
    tm = min(r, 128)
    n_g = len(g_parts)

    def body(*refs):
        w_ref, m_ref, v_ref = refs[:3]
        g_refs = refs[3:3 + n_g]
        g_out, d_out, m_out, v_out = refs[3 + n_g:]
        g = g_refs[0][...]
        for gr in g_refs[1:]:
            g = g + gr[...]
        m_new = ADAM_B1 * m_ref[...] + (1.0 - ADAM_B1) * g
        v_new = ADAM_B2 * v_ref[...] + (1.0 - ADAM_B2) * (g * g)
        m_hat = m_new / (1.0 - ADAM_B1 ** ADAM_STEP)
        v_hat = v_new / (1.0 - ADAM_B2 ** ADAM_STEP)
        g_out[...] = g
        d_out[...] = -ADAM_LR * (m_hat / (jnp.sqrt(v_hat) + ADAM_EPS) + ADAM_WD * w_ref[...])
        m_out[...] = m_new
        v_out[...] = v_new

    blk = pl.BlockSpec((tm, c), lambda i: (i, 0))
    out = jax.ShapeDtypeStruct((r, c), F32)
    return pl.pallas_call(
        body, name=name, grid=(r // tm,), in_specs=[blk] * (3 + n_g), out_specs=[blk] * 4,
        out_shape=[out] * 4, compiler_params=_params(("parallel",)),
    )(w, m, v, *g_parts)


def _rope_tables(positions):
    half = HEAD_DIM // 2
    inv_freq = ROPE_THETA ** (-jnp.arange(half, dtype=F32) / half)
    ang = positions.astype(F32)[0, :, None] * inv_freq
    cos, sin = jnp.cos(ang), jnp.sin(ang)
    return jnp.concatenate([cos, cos], axis=-1), jnp.concatenate([-sin, sin], axis=-1)


def kernel(x, positions, w_in, w_pool, pool_scale, w_branch_attn, w_branch_pool, w_out, ln_mix_g, ln_mix_b, w_ff1, w_ff2, ln_ff_g, ln_ff_b, loss_target, m_w_in, m_w_pool, m_pool_scale, m_w_branch_attn, m_w_branch_pool, m_w_out, m_ln_mix_g, m_ln_mix_b, m_w_ff1, m_w_ff2, m_ln_ff_g, m_ln_ff_b, v_w_in, v_w_pool, v_pool_scale, v_w_branch_attn, v_w_branch_pool, v_w_out, v_ln_mix_g, v_ln_mix_b, v_w_ff1, v_w_ff2, v_ln_ff_g, v_ln_ff_b):
    s, d = x.shape[1], x.shape[2]
    aw = d // 2
    ng = len(POOL_WINDOWS)
    pgw = aw // ng
    x2d, target = x[0], loss_target[0]
    xb = x2d.astype(BF16)
    cos_t, sin_t = _rope_tables(positions)

    big = {"w_in": w_in[0], "w_pool": w_pool[0].reshape(-1, pgw), "w_branch_attn": w_branch_attn[0],
           "w_branch_pool": w_branch_pool[0], "w_out": w_out[0], "w_ff1": w_ff1[0], "w_ff2": w_ff2[0]}
    names = list(big)
    me_chip = (2 * lax.axis_index("x") + lax.axis_index("y")).astype(jnp.int32).reshape(1)
    land_shapes = [(N_CHIPS, *big[k].shape) for k in names]
    gathering_in = _exchange_start("gather_start_in", GATHER_HALF,
                                   [_to_bf16("to_bf16_w_in", big["w_in"], positions)], land_shapes[:1])
    shards = [_to_bf16(f"to_bf16_{k}", big[k], gathering_in["token"]) for k in names[1:]]
    gathering = _exchange_start("gather_start", GATHER, shards, land_shapes[1:])

    def gathered(name, which, after):
        srcs, lands = _exchange_wait(f"gather_wait_{name}", GATHER, gathering, which, after)
        return [_place_own(f"place_own_{names[w + 1]}", srcs[i], lands[i], me_chip)
                for i, w in enumerate(which)]

    rows_sh = pgw // N_CHIPS
    dff = N_CHIPS * big["w_ff2"].shape[0]

    srcs, lands = _exchange_wait("gather_wait_in", GATHER_HALF, gathering_in, [0],
                                 [gathering["token"], xb, cos_t, sin_t])
    w_in_st = _place_own("place_own_w_in", srcs[0], _swap_halves("swap_halves_in", lands)[0], me_chip)
    h, hv4, hv16 = _in_proj(xb, w_in_st, cos_t, sin_t, aw)
    (wp_st,) = gathered("pool", [0], h)
    wp = wp_st.reshape(N_CHIPS, ng, rows_sh, pgw).transpose(1, 0, 2, 3).reshape(ng, pgw, pgw)
    qkv = {1: (h, h, h), 4: (hv4, hv4, hv4), 16: (hv16, hv16, hv16)}
    offs = {dil: (0, dil, 2 * dil) for dil in DILATIONS}
    o_parts, lse_parts = [], []
    for dil in DILATIONS:
        o_p, lse_p = _attn_fwd(f"attn_fwd_d{dil}", *qkv[dil], offs[dil], aw, dil * aw)
        o_parts.append(o_p)
        lse_parts.append(lse_p)
    o_attn, lse, lse4, lse16 = _attn_combine(o_parts[0], lse_parts[0], o_parts[1:], lse_parts[1:])
    p, pw, y = _pool_fwd(h, wp, pool_scale, aw)
    wba_st, wbp_st, w_out_st = gathered("mix", [1, 2, 3], y)
    w_out_full = w_out_st.reshape(d, d)
    ya, yp, merged = _branch_merge(o_attn, y, wba_st, wbp_st, h, aw, d)
    xhat1, rstd1, x1b = _mix_norm(merged, w_out_full, x2d, ln_mix_g, ln_mix_b)
    w1_st, w2_st = gathered("ff", [4, 5], x1b)
    w2_full = w2_st.reshape(dff, d)
    r, r_slope = _ff_up(x1b, w1_st)
    dz2, dz2b, st2 = _ff_down_loss(r, w2_full, xhat1, ln_mix_g, ln_mix_b, ln_ff_g, ln_ff_b, target)

    def scatter_start(name, grads):
        return _exchange_start(f"scatter_start_{name}", SCATTER, grads, [(3, *g.shape[1:]) for g in grads])

    da = _ff_down_bwd(dz2b, w2_full, r_slope)
    g_w2 = _wgrad("wgrad_ff2", r, dz2b, d).reshape(N_CHIPS, dff // N_CHIPS, d)
    g_w1 = _wgrad("wgrad_ff1", x1b, da, dff // N_CHIPS)
    sent_ff = scatter_start("ff", [g_w1, g_w2])
    dz1, dz1b, st1 = _ff_up_bwd(da, w1_st, dz2, xhat1, rstd1, ln_mix_g, [sent_ff["token"]])
    dya, dyp, dgate = _mix_bwd(dz1b, w_out_full, h, ya, yp, aw)
    g_wout = _wgrad("wgrad_out", merged, dz1b, d).reshape(N_CHIPS, d // N_CHIPS, d)
    g_wba = _wgrad("wgrad_branch_attn", o_attn, dya, d // N_CHIPS)
    g_wbp = _wgrad("wgrad_branch_pool", y, dyp, d // N_CHIPS)
    do, delta, do4, do16, delta4, delta16 = _attn_out_bwd(dya, wba_st, o_attn)
    dpw, stp = _pool_out_bwd(dyp, wbp_st, pw, pool_scale)
    dwp, du = _pool_bwd(dpw, p, wp)
    g_wp = dwp.reshape(ng, N_CHIPS, rows_sh, pgw).transpose(1, 0, 2, 3).reshape(
        N_CHIPS, ng * rows_sh, pgw).astype(BF16)
    sent_mix = scatter_start("mix", [g_wp, g_wba, g_wbp, g_wout])

    bwd_in = {1: (do, lse, delta), 4: (do4, lse4, delta4), 16: (do16, lse16, delta16)}
    dqkv = {}
    for dil in DILATIONS:
        args = (*qkv[dil], *bwd_in[dil], offs[dil], aw, dil * aw)
        dqkv[dil] = _attn_bwd(f"attn_bwd_d{dil}", *args)
    dqkvu = _qkvu_grad(dqkv[1], dqkv[4], dqkv[16], du, cos_t, sin_t)
    g_win = _wgrad("wgrad_in_gates", xb, dgate, d, shard0=2, n_shards=N_CHIPS)
    g_win = _wgrad("wgrad_in_qkvu", xb, dqkvu, d, n_shards=N_CHIPS, fill=g_win)
    sent_in = scatter_start("in", [g_win])
    dx_a = _in_proj_bwd_x("in_proj_bwd_qkvu", dqkvu, w_in_st, 0, dz1, ALPHA,
                          [sent_mix["token"], sent_in["token"]])
    grad_x = _in_proj_bwd_x("in_proj_bwd_gates", dgate, w_in_st, 2, dx_a, 1.0)

    moments = {"w_in": (m_w_in, v_w_in), "w_pool": (m_w_pool, v_w_pool),
               "w_branch_attn": (m_w_branch_attn, v_w_branch_attn),
               "w_branch_pool": (m_w_branch_pool, v_w_branch_pool), "w_out": (m_w_out, v_w_out),
               "w_ff1": (m_w_ff1, v_w_ff1), "w_ff2": (m_w_ff2, v_w_ff2)}
    originals = {"w_in": w_in, "w_pool": w_pool, "w_branch_attn": w_branch_attn,
                 "w_branch_pool": w_branch_pool, "w_out": w_out, "w_ff1": w_ff1, "w_ff2": w_ff2}
    res = {}

    def summed(name, sent, keys, after):
        srcs, lands = _exchange_wait(f"scatter_wait_{name}", SCATTER, sent, list(range(len(keys))), after)
        parts = [_sum_slabs(f"sum_slabs_{k}", srcs[i], lands[i], me_chip) for i, k in enumerate(keys)]
        return _exchange_start(f"cores_start_{name}", SIBLING, parts, [a.shape for a in parts])

    def updated(name, swapping, keys, after):
        mine, other = _exchange_wait(f"cores_wait_{name}", SIBLING, swapping, list(range(len(keys))), after)
        for i, k in enumerate(keys):
            mk, vk = (a.reshape(big[k].shape) for a in moments[k])
            outs = _adamw(f"adamw_{k}", big[k], mk, vk, [mine[i], other[i]])
            res[k] = [o.reshape(originals[k].shape) for o in outs]

    groups = {"ff": ["w_ff1", "w_ff2"], "mix": ["w_pool", "w_branch_attn", "w_branch_pool", "w_out"],
              "in": ["w_in"]}
    swap_ff = summed("ff", sent_ff, groups["ff"], grad_x)
    swap_mix = summed("mix", sent_mix, groups["mix"], swap_ff["token"])
    swap_in = summed("in", sent_in, groups["in"], swap_mix["token"])
    updated("ff", swap_ff, groups["ff"], swap_in["token"])
    updated("mix", swap_mix, groups["mix"], res["w_ff2"][0])
    updated("in", swap_in, groups["in"], res["w_out"][0])
    tot2, tot1, totp = _allreduce_stats([st2, st1, stp], res["w_in"][0])

    def pad_d(a):
        return jnp.pad(a, ((0, 0), (0, d - a.shape[1])))

    small = ["ln_mix_g", "ln_mix_b", "ln_ff_g", "ln_ff_b", "pool_scale"]
    small_w = {"ln_mix_g": ln_mix_g, "ln_mix_b": ln_mix_b, "ln_ff_g": ln_ff_g, "ln_ff_b": ln_ff_b,
               "pool_scale": pool_scale}
    small_m = {"ln_mix_g": m_ln_mix_g, "ln_mix_b": m_ln_mix_b, "ln_ff_g": m_ln_ff_g,
               "ln_ff_b": m_ln_ff_b, "pool_scale": m_pool_scale}
    small_v = {"ln_mix_g": v_ln_mix_g, "ln_mix_b": v_ln_mix_b, "ln_ff_g": v_ln_ff_g,
               "ln_ff_b": v_ln_ff_b, "pool_scale": v_pool_scale}
    small_g = [tot1[0:1], tot1[1:2], tot2[0:1], tot2[1:2], pad_d(totp[0:1])]

    def pack(rows):
        return jnp.concatenate([pad_d(a) for a in rows] + [jnp.zeros((8 - len(rows), d), F32)], axis=0)

    outs = _adamw("adamw_small", pack([small_w[k] for k in small]), pack([small_m[k] for k in small]),
                  pack([small_v[k] for k in small]), [pack(small_g)])
    for i, k in enumerate(small):
        res[k] = [o[i:i + 1, :small_w[k].shape[1]] for o in outs]
    loss = tot2[2, 0]

    order = ["w_in", "w_pool", "pool_scale", "w_branch_attn", "w_branch_pool", "w_out", "ln_mix_g",
             "ln_mix_b", "w_ff1", "w_ff2", "ln_ff_g", "ln_ff_b"]
    result = [loss, grad_x[None]]
    for idx in range(4):
        result += [res[k][idx] for k in order]
    return tuple(result)
```

```python
import jax
import jax.numpy as jnp
from jax import lax
from jax.experimental import pallas as pl
from jax.experimental.pallas import tpu as pltpu

F32 = jnp.float32
BF16 = jnp.bfloat16
MESH = pl.DeviceIdType.MESH

HEAD_DIM = 128
SUB_BLOCK = 128
ATTN_STEP_ROWS = 512
DILATIONS = (1, 4, 16)
POOL_WINDOWS = (2, 4, 8, 16)
POOL_HALO = 16
ROPE_THETA = 10000.0
LN_EPS = 1e-5
ALPHA = 2.0 ** 0.25
ADAM_LR, ADAM_B1, ADAM_B2, ADAM_EPS, ADAM_WD, ADAM_STEP = 0.001, 0.9, 0.999, 1e-08, 0.01, 10
NEG = -1e30
N_CHIPS = 4
VMEM_LIMIT = 62 * 1024 * 1024
EPILOGUE_ROWS = 128

def _params(sem=None, vmem=VMEM_LIMIT):
    kw = {"vmem_limit_bytes": vmem}
    if sem is not None:
        kw["dimension_semantics"] = sem
    return pltpu.CompilerParams(**kw)


def _dot(a, b, contract):
    return lax.dot_general(a, b, (contract, ((), ())), preferred_element_type=F32)


ANY_SPEC = pl.BlockSpec(memory_space=pl.ANY)
HBM_SPEC = pl.BlockSpec(memory_space=pltpu.HBM)
SEM_SPEC = pl.BlockSpec(memory_space=pltpu.SEMAPHORE)
DATAFLOW = pltpu.SideEffectType.DATAFLOW_SIDE_EFFECTING


def _hbm(a):
    return pltpu.with_memory_space_constraint(a, pltpu.HBM)


NN = ((1,), (0,))
NT = ((1,), (1,))
TN = ((0,), (0,))


def _mm(name, grid, a, a_spec, b, b_spec, contract, extras, extra_specs, out_shape, out_specs,
        epilogue, acc_shape, acc_as_ref=False, run_after=(), scratch=(),
        semantics=("parallel", "parallel", "arbitrary"), fill=None):
    nk = grid[2]
    n_ex = len(extras)
    n_in = 2 + n_ex + len(run_after) + (fill is not None)
    n_out = len(out_shape)
    n_scr = len(scratch)

    def body(*refs):
        a_ref, b_ref = refs[0], refs[1]
        ex = refs[2:2 + n_ex]
        outs = refs[n_in:n_in + n_out]
        scr = refs[n_in + n_out:n_in + n_out + n_scr]
        if nk == 1:
            epilogue(_dot(a_ref[...], b_ref[...], contract), ex, outs, *scr)
        else:
            acc = refs[n_in + n_out + n_scr]
            k = pl.program_id(2)

            @pl.when(k == 0)
            def _():
                acc[...] = jnp.zeros_like(acc)

            acc[...] += _dot(a_ref[...], b_ref[...], contract)

            @pl.when(k == nk - 1)
            def _():
                epilogue(acc if acc_as_ref else acc[...], ex, outs, *scr)

    acc_scratch = [pltpu.VMEM(acc_shape, F32)] if nk > 1 else []
    filled = [] if fill is None else [fill]
    return pl.pallas_call(
        body, name=name, grid=grid,
        in_specs=[a_spec, b_spec, *extra_specs, *[ANY_SPEC] * (len(run_after) + len(filled))],
        out_specs=out_specs, out_shape=out_shape, scratch_shapes=[*scratch, *acc_scratch],
        input_output_aliases={} if fill is None else {n_in - 1: 0},
        compiler_params=_params(semantics),
    )(a, b, *extras, *run_after, *filled)


def _stats_rows(rows, width):
    idx = lax.broadcasted_iota(jnp.int32, (8, width), 0)
    out = jnp.zeros((8, width), F32)
    for r, v in enumerate(rows):
        out = jnp.where(idx == r, jnp.broadcast_to(v, (8, width)), out)
    return out


def _layer_norm_fwd(z):
    mu = jnp.mean(z, axis=-1, keepdims=True)
    zc = z - mu
    var = jnp.mean(zc * zc, axis=-1, keepdims=True)
    rstd = lax.rsqrt(var + LN_EPS)
    return zc * rstd, rstd


def _layer_norm_bwd(dy, xhat, rstd, g):
    dxh = dy * g
    m1 = jnp.mean(dxh, axis=-1, keepdims=True)
    m2 = jnp.mean(dxh * xhat, axis=-1, keepdims=True)
    return rstd * (dxh - m1 - xhat * m2)


def _heads_scratch(rows, width):
    return pltpu.VMEM((width // HEAD_DIM, rows, HEAD_DIM), F32)


def _to_views(src_ref, view_refs, dtype, heads=None):
    nh, rows, _ = src_ref.shape
    width = nh * HEAD_DIM
    for dil, view_ref in zip(DILATIONS[1:], view_refs):
        for r in range(dil):
            for hh in (range(nh) if heads is None else heads):
                c0 = r * width + hh * HEAD_DIM
                view_ref[:, c0:c0 + HEAD_DIM] = (
                    src_ref[hh, pl.ds(r, rows // dil, stride=dil), :].astype(dtype))


def _from_views(view_refs, dst_refs):
    nh, rows, _ = dst_refs[0].shape
    width = nh * HEAD_DIM
    for dil, view_ref, dst_ref in zip(DILATIONS[1:], view_refs, dst_refs):
        for r in range(dil):
            for hh in range(nh):
                c0 = r * width + hh * HEAD_DIM
                dst_ref[hh, pl.ds(r, rows // dil, stride=dil), :] = (
                    view_ref[:, c0:c0 + HEAD_DIM].astype(F32))


def _view_shape(rows, width, dil, parts=1):
    return (rows // dil, parts * dil * width)


def _in_proj(xb, w_in_st, cos_t, sin_t, aw):
    s, d = xb.shape
    n_sh = w_in_st.shape[2]
    tm, tn = min(s, 1024), aw
    per = n_sh // tn
    grid = (s // tm, (N_CHIPS * n_sh) // tn, 1)

    def epilogue(acc, ex, outs, scr):
        cos_ref, sin_ref = ex
        h_ref, v4_ref, v16_ref = outs
        seg = pl.program_id(1)

        heads = [slice(hh * HEAD_DIM, (hh + 1) * HEAD_DIM) for hh in range(tn // HEAD_DIM)]

        @pl.when(seg < 2)
        def _():
            cos, sin = cos_ref[...], sin_ref[...]
            for hh, sl in enumerate(heads):
                t = acc[:, sl]
                scr[hh] = t * cos + pltpu.roll(t, HEAD_DIM // 2, 1) * sin

        @pl.when(seg == 2)
        def _():
            for hh, sl in enumerate(heads):
                scr[hh] = acc[:, sl]

        @pl.when(seg < 3)
        def _():
            for hh, sl in enumerate(heads):
                h_ref[:, sl] = scr[hh].astype(BF16)
            _to_views(scr, (v4_ref, v16_ref), BF16)

        @pl.when(seg == 3)
        def _():
            h_ref[...] = acc.astype(BF16)

        @pl.when(seg >= 4)
        def _():
            h_ref[...] = (0.5 * jnp.tanh(0.5 * acc) + 0.5).astype(BF16)

    def view_spec(dil):
        return pl.BlockSpec((tm // dil, dil * aw), lambda i, j, k: (i, jnp.minimum(j, 2)))

    return _mm(
        "in_proj", grid, xb, pl.BlockSpec((tm, d), lambda i, j, k: (i, 0)),
        w_in_st, pl.BlockSpec((None, d, tn), lambda i, j, k: (j // per, 0, j % per)), NN,
        [cos_t, sin_t], [pl.BlockSpec((tm, HEAD_DIM), lambda i, j, k: (i, 0))] * 2,
        [jax.ShapeDtypeStruct((s, N_CHIPS * n_sh), BF16)]
        + [jax.ShapeDtypeStruct(_view_shape(s, aw, dil, 3), BF16) for dil in DILATIONS[1:]],
        [pl.BlockSpec((tm, tn), lambda i, j, k: (i, j))] + [view_spec(dil) for dil in DILATIONS[1:]],
        epilogue, None, scratch=[_heads_scratch(tm, aw)],
        semantics=("parallel", "arbitrary", "arbitrary"))


def _pack_heads(cols):
    rows, rep = cols[0].shape[0], HEAD_DIM // len(cols)
    lane = lax.broadcasted_iota(jnp.int32, (rows, HEAD_DIM), 1)
    out = jnp.zeros((rows, HEAD_DIM), F32)
    for hh, col in enumerate(cols):
        out = jnp.where((lane >= hh * rep) & (lane < (hh + 1) * rep), col, out)
    return out


def _head_col(packed, hh, nh):
    lane = lax.broadcasted_iota(jnp.int32, packed.shape, 1)
    return jnp.sum(jnp.where(lane == hh * (HEAD_DIM // nh), packed, 0.0), axis=-1, keepdims=True)


def _band_masks(block_idx):
    qi = lax.broadcasted_iota(jnp.int32, (SUB_BLOCK, 2 * SUB_BLOCK), 0)
    kj = lax.broadcasted_iota(jnp.int32, (SUB_BLOCK, 2 * SUB_BLOCK), 1)
    first_key = jnp.where(block_idx > 0, 0, SUB_BLOCK)
    return (kj >= qi) & (kj <= qi + SUB_BLOCK) & (kj >= first_key)


def _attn_fwd(name, q, k, v, offs, cw, width):
    m = q.shape[0]
    nh = cw // HEAD_DIM
    rb = min(m, ATTN_STEP_ROWS)
    nsub = rb // SUB_BLOCK
    grid = (width // cw, m // rb)
    scale = HEAD_DIM ** -0.5

    def body(q_ref, k_ref, v_ref, o_ref, lse_ref, kprev, vprev):
        n = pl.program_id(1)

        @pl.when(n == 0)
        def _():
            kprev[...] = jnp.zeros_like(kprev)
            vprev[...] = jnp.zeros_like(vprev)

        heads = [slice(hh * HEAD_DIM, (hh + 1) * HEAD_DIM) for hh in range(nh)]
        valids = [_band_masks(n)] + [_band_masks(1)] * (nsub - 1)

        def rows(sub):
            return slice(sub * SUB_BLOCK, (sub + 1) * SUB_BLOCK)

        def cat(prev, ref, sub, sl):
            if sub == 0:
                return jnp.concatenate([prev[:, sl], ref[rows(0), sl]], axis=0)
            return ref[(sub - 1) * SUB_BLOCK:(sub + 1) * SUB_BLOCK, sl]

        scs = [[_dot(q_ref[rows(sub), sl], cat(kprev, k_ref, sub, sl), NT) for sl in heads]
               for sub in range(nsub)]
        for sub in range(nsub):
            lses = []
            for hh, sl in enumerate(heads):
                sc = jnp.where(valids[sub], scs[sub][hh] * scale, NEG)
                mx = jnp.max(sc, axis=-1, keepdims=True)
                p = jnp.exp(sc - mx)
                l = jnp.sum(p, axis=-1, keepdims=True)
                o = _dot(p.astype(BF16), cat(vprev, v_ref, sub, sl), NN) / l
                o_ref[rows(sub), sl] = o.astype(BF16)
                lses.append(mx + jnp.log(l))
            lse_ref[rows(sub), :] = _pack_heads(lses)
        kprev[...] = k_ref[rows(nsub - 1), :]
        vprev[...] = v_ref[rows(nsub - 1), :]

    def cur(off):
        return pl.BlockSpec((rb, cw), lambda j, n: (n, j + off))

    return pl.pallas_call(
        body, name=name, grid=grid,
        in_specs=[cur(offs[0]), cur(offs[1]), cur(offs[2])],
        out_specs=[cur(0), pl.BlockSpec((rb, HEAD_DIM), lambda j, n: (n, j))],
        out_shape=[jax.ShapeDtypeStruct((m, width), BF16),
                   jax.ShapeDtypeStruct((m, width // cw * HEAD_DIM), F32)],
        scratch_shapes=[pltpu.VMEM((SUB_BLOCK, cw), BF16)] * 2,
        compiler_params=_params(("parallel", "arbitrary")),
    )(q, k, v)


def _view_spec(tm, aw, dil):
    return pl.BlockSpec((tm // dil, dil * aw), lambda i: (i, 0))


def _attn_combine(o1, l1, o_views, l_views):
    s, aw = o1.shape
    nh = aw // HEAD_DIM
    tm = min(s, 512)

    def body(o1_ref, l1_ref, o4_ref, o16_ref, l4_ref, l16_ref, o_ref, lse_ref, lse4_ref, lse16_ref,
             so4, so16, sl4, sl16, stot):
        _from_views((o4_ref, o16_ref), (so4, so16))
        _from_views((l4_ref, l16_ref), (sl4, sl16))
        a, b, c = l1_ref[...], sl4[0], sl16[0]
        mx = jnp.maximum(jnp.maximum(a, b), c)
        ea, eb, ec = jnp.exp(a - mx), jnp.exp(b - mx), jnp.exp(c - mx)
        tot = ea + eb + ec
        inv = 1.0 / tot
        wa, wb, wc = ea * inv, eb * inv, ec * inv
        lse_tot = mx + jnp.log(tot)
        stot[0] = lse_tot
        lse_ref[...] = lse_tot
        _to_views(stot, (lse4_ref, lse16_ref), F32)
        for hh in range(nh):
            sl = slice(hh * HEAD_DIM, (hh + 1) * HEAD_DIM)
            o = (_head_col(wa, hh, nh) * o1_ref[:, sl].astype(F32) + _head_col(wb, hh, nh) * so4[hh]
                 + _head_col(wc, hh, nh) * so16[hh])
            o_ref[:, sl] = o.astype(BF16)

    row = pl.BlockSpec((tm, aw), lambda i: (i, 0))
    stat = pl.BlockSpec((tm, HEAD_DIM), lambda i: (i, 0))
    views = [_view_spec(tm, aw, dil) for dil in DILATIONS[1:]]
    stat_views = [_view_spec(tm, HEAD_DIM, dil) for dil in DILATIONS[1:]]
    return pl.pallas_call(
        body, name="attn_combine", grid=(s // tm,), in_specs=[row, stat, *views, *stat_views],
        out_specs=[row, stat, *stat_views],
        out_shape=[jax.ShapeDtypeStruct((s, aw), BF16), jax.ShapeDtypeStruct((s, HEAD_DIM), F32)]
        + [jax.ShapeDtypeStruct(_view_shape(s, HEAD_DIM, dil), F32) for dil in DILATIONS[1:]],
        scratch_shapes=[_heads_scratch(tm, aw)] * 2 + [_heads_scratch(tm, HEAD_DIM)] * 3,
        compiler_params=_params(("parallel",)),
    )(o1, l1, *o_views, *l_views)


def _pool_counts(tm, rows, pgw, row0):
    t = lax.broadcasted_iota(jnp.int32, (rows, len(POOL_WINDOWS) * pgw), 0) + row0
    col = lax.broadcasted_iota(jnp.int32, (rows, len(POOL_WINDOWS) * pgw), 1)
    w = jnp.full((rows, len(POOL_WINDOWS) * pgw), POOL_WINDOWS[0], jnp.int32)
    for g in range(1, len(POOL_WINDOWS)):
        w = jnp.where(col >= g * pgw, POOL_WINDOWS[g], w)
    return jnp.minimum(t + 1, w).astype(F32)


def _window_sums(xs, direction, pgw):
    rows = xs.shape[0]
    acc = xs
    out = None
    col = lax.broadcasted_iota(jnp.int32, xs.shape, 1)
    for g, w in enumerate(POOL_WINDOWS):
        sh = w // 2
        acc = acc + pltpu.roll(acc, sh if direction > 0 else rows - sh, 0)
        out = acc if out is None else jnp.where(col >= g * pgw, acc, out)
    return out


def _pool_fwd(h, wp, scale, aw):
    s = h.shape[0]
    pw_ = aw
    pgw = pw_ // len(POOL_WINDOWS)
    tm = min(s, 512)
    hb = tm // POOL_HALO

    def body(u_ref, halo_ref, wp_ref, sc_ref, p_ref, pw_ref, y_ref):
        i = pl.program_id(0)
        u = u_ref[...].astype(F32)
        halo = halo_ref[...].astype(F32) * jnp.where(i > 0, 1.0, 0.0)
        xs = jnp.concatenate([halo, u], axis=0)
        sums = _window_sums(xs, +1, pgw)[POOL_HALO:]
        p = (sums / _pool_counts(tm, tm, pgw, i * tm) - u).astype(BF16)
        p_ref[...] = p
        sc = sc_ref[...]
        for g in range(len(POOL_WINDOWS)):
            sl = slice(g * pgw, (g + 1) * pgw)
            pw = _dot(p[:, sl], wp_ref[g], NN)
            pw_ref[:, sl] = pw.astype(BF16)
            y_ref[:, sl] = (pw * sc[:, sl]).astype(BF16)

    out = jax.ShapeDtypeStruct((s, pw_), BF16)
    row = pl.BlockSpec((tm, pw_), lambda i: (i, 0))
    return pl.pallas_call(
        body, name="pool_fwd", grid=(s // tm,),
        in_specs=[pl.BlockSpec((tm, pw_), lambda i: (i, 3)),
                  pl.BlockSpec((POOL_HALO, pw_), lambda i: (jnp.maximum(i * hb - 1, 0), 3)),
                  pl.BlockSpec(wp.shape, lambda i: (0, 0, 0)),
                  pl.BlockSpec((1, pw_), lambda i: (0, 0))],
        out_specs=[row, row, row], out_shape=[out, out, out],
        compiler_params=_params(("parallel",)),
    )(h, h, wp, scale)


def _branch_merge(o_attn, y, wba_st, wbp_st, h, aw, d):
    s = o_attn.shape[0]
    tn = wba_st.shape[2]
    tm = min(s, 1024)
    ga0 = 4 * aw // tn
    gp0 = (4 * aw + d) // tn

    def body(o_ref, y_ref, wa_ref, wp_ref, sga_ref, sgp_ref, ya_ref, yp_ref, mg_ref):
        ya = _dot(o_ref[...], wa_ref[...], NN)
        yp = _dot(y_ref[...], wp_ref[...], NN)
        ya_ref[...] = ya.astype(BF16)
        yp_ref[...] = yp.astype(BF16)
        mg_ref[...] = (sga_ref[...].astype(F32) * ya + sgp_ref[...].astype(F32) * yp).astype(BF16)

    out = jax.ShapeDtypeStruct((s, d), BF16)
    blk = pl.BlockSpec((tm, tn), lambda i, j: (i, j))
    return pl.pallas_call(
        body, name="branch_merge", grid=(s // tm, N_CHIPS),
        in_specs=[pl.BlockSpec((tm, aw), lambda i, j: (i, 0)),
                  pl.BlockSpec((tm, aw), lambda i, j: (i, 0)),
                  pl.BlockSpec((None, aw, tn), lambda i, j: (j, 0, 0)),
                  pl.BlockSpec((None, aw, tn), lambda i, j: (j, 0, 0)),
                  pl.BlockSpec((tm, tn), lambda i, j: (i, j + ga0)),
                  pl.BlockSpec((tm, tn), lambda i, j: (i, j + gp0))],
        out_specs=[blk, blk, blk], out_shape=[out, out, out],
        compiler_params=_params(("parallel", "parallel")),
    )(o_attn, y, wba_st, wbp_st, h, h)


def _mix_norm(merged, w_out, x, g1, b1):
    s, d = x.shape
    tm = min(s, 256)

    def epilogue(acc, ex, outs):
        x_ref, g_ref, b_ref = ex
        xh_ref, rs_ref, xb_ref = outs
        xhat, rstd = _layer_norm_fwd(ALPHA * x_ref[...] + acc)
        xh_ref[...] = xhat
        rs_ref[...] = rstd
        xb_ref[...] = (xhat * g_ref[...] + b_ref[...]).astype(BF16)

    row = pl.BlockSpec((tm, d), lambda i, j, k: (i, 0))
    vec = pl.BlockSpec((1, d), lambda i, j, k: (0, 0))
    return _mm(
        "mix_norm", (s // tm, 1, 1), merged, row, w_out, pl.BlockSpec((d, d), lambda i, j, k: (0, 0)),
        NN, [x, g1, b1], [row, vec, vec],
        [jax.ShapeDtypeStruct((s, d), F32), jax.ShapeDtypeStruct((s, 1), F32),
         jax.ShapeDtypeStruct((s, d), BF16)],
        [row, pl.BlockSpec((tm, 1), lambda i, j, k: (i, 0)), row], epilogue, None)


def _ff_up(x1b, w1_st):
    s, d = x1b.shape
    n_sh = w1_st.shape[2]
    tm, tn = min(s, 1024), min(n_sh, 1024)
    per = n_sh // tn

    def epilogue(acc, ex, outs):
        r = jnp.maximum(acc, 0.0)
        outs[0][...] = (r * r).astype(BF16)
        outs[1][...] = (2.0 * r).astype(BF16)

    blk = pl.BlockSpec((tm, tn), lambda i, j, k: (i, j))
    out = jax.ShapeDtypeStruct((s, N_CHIPS * n_sh), BF16)
    return _mm(
        "ff_up", (s // tm, N_CHIPS * per, 1), x1b, pl.BlockSpec((tm, d), lambda i, j, k: (i, 0)),
        w1_st, pl.BlockSpec((None, d, tn), lambda i, j, k: (j // per, 0, j % per)), NN, [], [],
        [out, out], [blk, blk], epilogue, None)


def _ff_down_loss(r, w2, xhat1, g1, b1, g2, b2, target):
    s, d = xhat1.shape
    dff = r.shape[1]
    tm, tk = min(s, 512), min(dff, 2048)
    ch = min(tm, EPILOGUE_ROWS)

    def epilogue(acc_ref, ex, outs):
        xh1_ref, g1_ref, b1_ref, g2_ref, b2_ref, t_ref = ex
        dz_ref, dzb_ref, st_ref = outs
        g1v, b1v, g2v, b2v = g1_ref[...], b1_ref[...], g2_ref[...], b2_ref[...]
        dg = db = loss = None
        for c in range(tm // ch):
            rows = slice(c * ch, (c + 1) * ch)
            x1 = xh1_ref[rows, :] * g1v + b1v
            xhat2, rstd2 = _layer_norm_fwd(ALPHA * x1 + acc_ref[rows, :])
            err = xhat2 * g2v + b2v - t_ref[rows, :]
            dy = err * (1.0 / d)
            dz = _layer_norm_bwd(dy, xhat2, rstd2, g2v)
            dz_ref[rows, :] = dz
            dzb_ref[rows, :] = dz.astype(BF16)
            parts = (jnp.sum(dy * xhat2, axis=0, keepdims=True), jnp.sum(dy, axis=0, keepdims=True),
                     jnp.sum(jnp.sum(err * err, axis=-1, keepdims=True), axis=0, keepdims=True))
            dg, db, loss = parts if c == 0 else (dg + parts[0], db + parts[1], loss + parts[2])
        st_ref[...] = _stats_rows([dg, db, jnp.broadcast_to((0.5 / d) * loss, (1, d))], d)

    row = pl.BlockSpec((tm, d), lambda i, j, k: (i, 0))
    vec = pl.BlockSpec((1, d), lambda i, j, k: (0, 0))
    return _mm(
        "ff_down_loss", (s // tm, 1, dff // tk), r, pl.BlockSpec((tm, tk), lambda i, j, k: (i, k)),
        w2, pl.BlockSpec((tk, d), lambda i, j, k: (k, 0)), NN,
        [xhat1, g1, b1, g2, b2, target], [row, vec, vec, vec, vec, row],
        [jax.ShapeDtypeStruct((s, d), F32), jax.ShapeDtypeStruct((s, d), BF16),
         jax.ShapeDtypeStruct((s // tm, 8, d), F32)],
        [row, row, pl.BlockSpec((None, 8, d), lambda i, j, k: (i, 0, 0))], epilogue, (tm, d),
        acc_as_ref=True)


def _ff_down_bwd(dz2b, w2, r_slope):
    s, d = dz2b.shape
    dff = r_slope.shape[1]
    tm, tn = min(s, 1024), min(dff, 1024)

    def epilogue(acc, ex, outs):
        outs[0][...] = (acc * ex[0][...].astype(F32)).astype(BF16)

    blk = pl.BlockSpec((tm, tn), lambda i, j, k: (i, j))
    return _mm(
        "ff_down_bwd", (s // tm, dff // tn, 1), dz2b, pl.BlockSpec((tm, d), lambda i, j, k: (i, 0)),
        w2, pl.BlockSpec((tn, d), lambda i, j, k: (j, 0)), NT, [r_slope], [blk],
        [jax.ShapeDtypeStruct((s, dff), BF16)], [blk], epilogue, None)[0]


def _wgrad(name, a, g, n_sh, shard0=0, n_shards=None, fill=None):
    s, rows = a.shape
    cols = g.shape[1]
    tm, tn, tk = min(rows, 2048), min(cols, 1024), min(s, 2048)
    if tn >= n_sh:
        span = tn // n_sh
        out_spec = pl.BlockSpec((span, tm, n_sh), lambda i, j, k: (j + shard0 // span, i, 0))

        def epilogue(acc_ref, ex, outs):
            for sh in range(span):
                outs[0][sh] = acc_ref[:, sh * n_sh:(sh + 1) * n_sh].astype(BF16)
    else:
        per = n_sh // tn
        out_spec = pl.BlockSpec((None, tm, tn), lambda i, j, k: (shard0 + j // per, i, j % per))

        def epilogue(acc_ref, ex, outs):
            outs[0][...] = acc_ref[...].astype(BF16)

    return _mm(
        name, (rows // tm, cols // tn, s // tk), a, pl.BlockSpec((tk, tm), lambda i, j, k: (k, i)),
        g, pl.BlockSpec((tk, tn), lambda i, j, k: (k, j)), TN, [], [],
        [jax.ShapeDtypeStruct((n_shards or cols // n_sh, rows, n_sh), BF16)], [out_spec], epilogue,
        (tm, tn), acc_as_ref=True, fill=fill)[0]


def _ff_up_bwd(da, w1_st, dz2, xhat1, rstd1, g1, run_after):
    s, d = dz2.shape
    n_sh = w1_st.shape[2]
    tm, tk = min(s, 512), min(n_sh, 2048)
    per = n_sh // tk
    ch = min(tm, EPILOGUE_ROWS)

    def epilogue(acc_ref, ex, outs):
        dz2_ref, xh_ref, rs_ref, g_ref = ex
        dz_ref, dzb_ref, st_ref = outs
        gv = g_ref[...]
        dg = db = None
        for c in range(tm // ch):
            rows = slice(c * ch, (c + 1) * ch)
            dx1 = ALPHA * dz2_ref[rows, :] + acc_ref[rows, :]
            xhat = xh_ref[rows, :]
            dz = _layer_norm_bwd(dx1, xhat, rs_ref[rows, :], gv)
            dz_ref[rows, :] = dz
            dzb_ref[rows, :] = dz.astype(BF16)
            parts = (jnp.sum(dx1 * xhat, axis=0, keepdims=True), jnp.sum(dx1, axis=0, keepdims=True))
            dg, db = parts if c == 0 else (dg + parts[0], db + parts[1])
        st_ref[...] = _stats_rows([dg, db], d)

    row = pl.BlockSpec((tm, d), lambda i, j, k: (i, 0))
    return _mm(
        "ff_up_bwd", (s // tm, 1, N_CHIPS * per), da, pl.BlockSpec((tm, tk), lambda i, j, k: (i, k)),
        w1_st, pl.BlockSpec((None, d, tk), lambda i, j, k: (k // per, 0, k % per)), NT,
        [dz2, xhat1, rstd1, g1],
        [row, row, pl.BlockSpec((tm, 1), lambda i, j, k: (i, 0)), pl.BlockSpec((1, d), lambda i, j, k: (0, 0))],
        [jax.ShapeDtypeStruct((s, d), F32), jax.ShapeDtypeStruct((s, d), BF16),
         jax.ShapeDtypeStruct((s // tm, 8, d), F32)],
        [row, row, pl.BlockSpec((None, 8, d), lambda i, j, k: (i, 0, 0))], epilogue, (tm, d),
        acc_as_ref=True, run_after=run_after)


def _mix_bwd(dz1b, w_out, h, ya, yp, aw):
    s, d = dz1b.shape
    tm = min(s, 256)
    gblk = 4 * aw // d

    def epilogue(acc, ex, outs):
        sga_ref, sgp_ref, ya_ref, yp_ref = ex
        dya_ref, dyp_ref, dg_ref = outs
        sga, sgp = sga_ref[...].astype(F32), sgp_ref[...].astype(F32)
        dya_ref[...] = (acc * sga).astype(BF16)
        dyp_ref[...] = (acc * sgp).astype(BF16)
        dg_ref[:, :d] = (acc * ya_ref[...].astype(F32) * (sga * (1.0 - sga))).astype(BF16)
        dg_ref[:, d:] = (acc * yp_ref[...].astype(F32) * (sgp * (1.0 - sgp))).astype(BF16)

    row = pl.BlockSpec((tm, d), lambda i, j, k: (i, 0))
    return _mm(
        "mix_bwd", (s // tm, 1, 1), dz1b, row, w_out, pl.BlockSpec((d, d), lambda i, j, k: (0, 0)), NT,
        [h, h, ya, yp],
        [pl.BlockSpec((tm, d), lambda i, j, k: (i, gblk)),
         pl.BlockSpec((tm, d), lambda i, j, k: (i, gblk + 1)), row, row],
        [jax.ShapeDtypeStruct((s, d), BF16), jax.ShapeDtypeStruct((s, d), BF16),
         jax.ShapeDtypeStruct((s, 4 * d), BF16)],
        [row, row, pl.BlockSpec((tm, 2 * d), lambda i, j, k: (i, 1))], epilogue, None)


def _branch_in_bwd(name, tm, dyb, wb_st, epilogue, extras, extra_specs, out_shape, out_specs,
                   scratch=()):
    s, d = dyb.shape
    aw = wb_st.shape[1]
    wb_t = wb_st.transpose(0, 2, 1).reshape(d, aw)
    return _mm(
        name, (s // tm, 1, 1), dyb, pl.BlockSpec((tm, d), lambda i, j, k: (i, 0)),
        wb_t, pl.BlockSpec((d, aw), lambda i, j, k: (0, 0)), NN,
        extras, extra_specs, out_shape, out_specs, epilogue, None, scratch=scratch)


def _attn_out_bwd(dya, wba_st, o_attn):
    s, aw = o_attn.shape
    tm = min(s, 512)

    def epilogue(acc_ref, ex, outs, sdo, sdl):
        do_ref, dl_ref, do4_ref, do16_ref, dl4_ref, dl16_ref = outs
        deltas = []
        for hh in range(aw // HEAD_DIM):
            sl = slice(hh * HEAD_DIM, (hh + 1) * HEAD_DIM)
            do = acc_ref[:, sl]
            deltas.append(jnp.sum(do * ex[0][:, sl].astype(F32), axis=-1, keepdims=True))
            sdo[hh] = do
            do_ref[:, sl] = do.astype(BF16)
        packed = _pack_heads(deltas)
        sdl[0] = packed
        dl_ref[...] = packed
        _to_views(sdo, (do4_ref, do16_ref), BF16)
        _to_views(sdl, (dl4_ref, dl16_ref), F32)

    def specs(width):
        return ([pl.BlockSpec((tm, width), lambda i, j, k: (i, 0))]
                + [pl.BlockSpec((tm // dil, dil * width), lambda i, j, k: (i, 0)) for dil in DILATIONS[1:]])

    def shapes(width, dtype):
        return ([jax.ShapeDtypeStruct((s, width), dtype)]
                + [jax.ShapeDtypeStruct(_view_shape(s, width, dil), dtype) for dil in DILATIONS[1:]])

    do_specs, dl_specs = specs(aw), specs(HEAD_DIM)
    do_shapes, dl_shapes = shapes(aw, BF16), shapes(HEAD_DIM, F32)
    return _branch_in_bwd(
        "attn_out_bwd", tm, dya, wba_st, epilogue, [o_attn], [do_specs[0]],
        [do_shapes[0], dl_shapes[0], *do_shapes[1:], *dl_shapes[1:]],
        [do_specs[0], dl_specs[0], *do_specs[1:], *dl_specs[1:]],
        scratch=[_heads_scratch(tm, aw), _heads_scratch(tm, HEAD_DIM)])


def _pool_out_bwd(dyp, wbp_st, pw, scale):
    s, pw_ = pw.shape
    tm = min(s, 1024)

    def epilogue(acc_ref, ex, outs):
        pw_ref, sc_ref = ex
        dpw_ref, st_ref = outs
        acc = acc_ref[...]
        dpw_ref[...] = (acc * sc_ref[...]).astype(BF16)
        st_ref[...] = _stats_rows([jnp.sum(acc * pw_ref[...].astype(F32), axis=0, keepdims=True)], pw_)

    row = pl.BlockSpec((tm, pw_), lambda i, j, k: (i, 0))
    return _branch_in_bwd(
        "pool_out_bwd", tm, dyp, wbp_st, epilogue, [pw, scale],
        [row, pl.BlockSpec((1, pw_), lambda i, j, k: (0, 0))],
        [jax.ShapeDtypeStruct((s, pw_), BF16), jax.ShapeDtypeStruct((s // tm, 8, pw_), F32)],
        [row, pl.BlockSpec((None, 8, pw_), lambda i, j, k: (i, 0, 0))])


def _pool_bwd(dpw, p, wp):
    s, pw_ = p.shape
    ng = len(POOL_WINDOWS)
    pgw = pw_ // ng
    tm = min(s, 512)
    hb = tm // POOL_HALO
    nblk = s // tm

    def body(dpw_ref, nxt_ref, p_ref, wp_ref, dwp_ref, du_ref):
        i = pl.program_id(0)
        nxt = (nxt_ref[...].astype(F32) * jnp.where(i < nblk - 1, 1.0, 0.0)).astype(BF16)
        dpw_all = jnp.concatenate([dpw_ref[...], nxt], axis=0)

        @pl.when(i == 0)
        def _():
            dwp_ref[...] = jnp.zeros_like(dwp_ref)

        dps = []
        for g in range(ng):
            sl = slice(g * pgw, (g + 1) * pgw)
            dwp_ref[g] += _dot(p_ref[:, sl], dpw_ref[:, sl], TN)
            dps.append(_dot(dpw_all[:, sl], wp_ref[g], NT))
        dp = jnp.concatenate(dps, axis=1)
        dpn = dp / _pool_counts(tm, tm + POOL_HALO, pgw, i * tm)
        du_ref[...] = (_window_sums(dpn, -1, pgw)[:tm] - dp[:tm]).astype(BF16)

    row = pl.BlockSpec((tm, pw_), lambda i: (i, 0))
    full = pl.BlockSpec((ng, pgw, pgw), lambda i: (0, 0, 0))
    return pl.pallas_call(
        body, name="pool_bwd", grid=(nblk,),
        in_specs=[row, pl.BlockSpec((POOL_HALO, pw_), lambda i: (jnp.minimum((i + 1) * hb, s // POOL_HALO - 1), 0)),
                  row, full],
        out_specs=[full, row],
        out_shape=[jax.ShapeDtypeStruct((ng, pgw, pgw), F32), jax.ShapeDtypeStruct((s, pw_), BF16)],
        compiler_params=_params(("arbitrary",)),
    )(dpw, dpw, p, wp)


def _attn_bwd(name, q, k, v, do, lse, delta, offs, cw, width):
    m = do.shape[0]
    nh = cw // HEAD_DIM
    rb = min(m, ATTN_STEP_ROWS)
    nsub = rb // SUB_BLOCK
    nstep = m // rb
    single = nstep == 1
    grid = (width // cw, nstep + (not single))
    scale = HEAD_DIM ** -0.5
    last = slice(rb - SUB_BLOCK, rb)

    def body(q_ref, k_ref, v_ref, do_ref, lse_ref, dl_ref, dq_ref, dk_ref, dv_ref,
             kprev, vprev, dk_carry, dv_carry):
        n = pl.program_id(1)

        @pl.when(n == 0)
        def _():
            for ref in (kprev, vprev, dk_carry, dv_carry):
                ref[...] = jnp.zeros_like(ref)

        qi = lax.broadcasted_iota(jnp.int32, (SUB_BLOCK, 2 * SUB_BLOCK), 0)
        kj = lax.broadcasted_iota(jnp.int32, (SUB_BLOCK, 2 * SUB_BLOCK), 1)
        band = (kj >= qi) & (kj <= qi + SUB_BLOCK)
        flush = jnp.where(n == nstep, 4 * SUB_BLOCK, 0)
        valids = [band & (kj >= jnp.where(n == 0, SUB_BLOCK, flush))] + [band & (kj >= flush)] * (nsub - 1)
        heads = [slice(hh * HEAD_DIM, (hh + 1) * HEAD_DIM) for hh in range(nh)]

        def rows(sub):
            return slice(sub * SUB_BLOCK, (sub + 1) * SUB_BLOCK)

        def cat(prev, ref, sub, sl):
            if sub == 0:
                return jnp.concatenate([prev[:, sl], ref[rows(0), sl]], axis=0)
            return ref[(sub - 1) * SUB_BLOCK:(sub + 1) * SUB_BLOCK, sl]

        kcats = [[cat(kprev, k_ref, sub, sl) for sl in heads] for sub in range(nsub)]
        scs = [[_dot(q_ref[rows(sub), sl], kcats[sub][hh], NT) for hh, sl in enumerate(heads)]
               for sub in range(nsub)]
        dps = [[_dot(do_ref[rows(sub), sl], cat(vprev, v_ref, sub, sl), NT) for sl in heads]
               for sub in range(nsub)]
        dqs = []
        stats = [(lse_ref[rows(sub), :], dl_ref[rows(sub), :]) for sub in range(nsub)]
        for hh, sl in enumerate(heads):
            dk2, dv2 = [], []
            for sub in range(nsub):
                lse_h, dl_h = _head_col(stats[sub][0], hh, nh), _head_col(stats[sub][1], hh, nh)
                valid = valids[sub]
                p = jnp.where(valid, jnp.exp(jnp.where(valid, scs[sub][hh] * scale, NEG) - lse_h), 0.0)
                ds = (p * (dps[sub][hh] - dl_h)).astype(BF16)
                dqs.append((sub, sl, (_dot(ds, kcats[sub][hh], NN) * scale).astype(BF16)))
                dk2.append(_dot(ds, q_ref[rows(sub), sl], TN) * scale)
                dv2.append(_dot(p.astype(BF16), do_ref[rows(sub), sl], TN))
            for out_ref, carry, new in ((dk_ref, dk_carry, dk2), (dv_ref, dv_carry, dv2)):
                if not single:
                    out_ref[last, sl] = (carry[last, sl] + new[0][:SUB_BLOCK]).astype(BF16)
                    if nsub > 1:
                        out_ref[:rb - SUB_BLOCK, sl] = carry[:rb - SUB_BLOCK, sl].astype(BF16)
                for sub in range(nsub):
                    val = new[sub][SUB_BLOCK:]
                    if sub + 1 < nsub:
                        val = val + new[sub + 1][:SUB_BLOCK]
                    if single:
                        out_ref[rows(sub), sl] = val.astype(BF16)
                    else:
                        carry[rows(sub), sl] = val

        @pl.when(n < nstep)
        def _():
            for sub, sl, val in dqs:
                dq_ref[rows(sub), sl] = val

        kprev[...] = k_ref[last, :]
        vprev[...] = v_ref[last, :]

    def cur(off):
        return pl.BlockSpec((rb, cw), lambda j, n: (jnp.minimum(n, nstep - 1), j + off))

    lagged = pl.BlockSpec((rb, cw), lambda j, n: (jnp.maximum(n - 1, 0), j))
    stat = pl.BlockSpec((rb, HEAD_DIM), lambda j, n: (jnp.minimum(n, nstep - 1), j))
    out = jax.ShapeDtypeStruct((m, width), BF16)
    return pl.pallas_call(
        body, name=name, grid=grid,
        in_specs=[cur(offs[0]), cur(offs[1]), cur(offs[2]), cur(0), stat, stat],
        out_specs=[cur(0), lagged, lagged], out_shape=[out, out, out],
        scratch_shapes=[pltpu.VMEM((SUB_BLOCK, cw), BF16)] * 2 + [pltpu.VMEM((rb, cw), F32)] * 2,
        compiler_params=_params(("parallel", "arbitrary")),
    )(q, k, v, do, lse, delta)


def _qkvu_grad(d1, d4, d16, du, cos_t, sin_t, dh):
    s, aw = du.shape
    tm = min(s, 512)

    def body(*refs):
        nat, v4, v16 = refs[0:3], refs[3:6], refs[6:9]
        cos_ref, sin_ref, du_ref, _, out_ref, s4, s16 = refs[9:]
        cos, sin = cos_ref[...], sin_ref[...]
        for part in range(3):
            _from_views((v4[part], v16[part]), (s4, s16))
            for hh in range(aw // HEAD_DIM):
                sl = slice(hh * HEAD_DIM, (hh + 1) * HEAD_DIM)
                t = nat[part][:, sl].astype(F32) + s4[hh] + s16[hh]
                if part < 2:
                    t = t * cos - pltpu.roll(t, HEAD_DIM // 2, 1) * sin
                out_ref[:, part * aw + hh * HEAD_DIM:part * aw + (hh + 1) * HEAD_DIM] = t.astype(BF16)
        out_ref[:, 3 * aw:] = du_ref[...]

    row = pl.BlockSpec((tm, aw), lambda i: (i, 0))
    tab = pl.BlockSpec((tm, HEAD_DIM), lambda i: (i, 0))
    return pl.pallas_call(
        body, name="qkvu_grad", grid=(s // tm,),
        in_specs=[row] * 3 + [_view_spec(tm, aw, 4)] * 3 + [_view_spec(tm, aw, 16)] * 3
        + [tab, tab, row, ANY_SPEC],
        out_specs=pl.BlockSpec((tm, 4 * aw), lambda i: (i, 0)),
        out_shape=jax.ShapeDtypeStruct(dh.shape, BF16), input_output_aliases={12: 0},
        scratch_shapes=[_heads_scratch(tm, aw)] * 2,
        compiler_params=_params(("parallel",)),
    )(*d1, *d4, *d16, cos_t, sin_t, du, dh)


def _in_proj_bwd_x(name, dh, w_in_st, base, scale_base, run_after=()):
    s, kdim = dh.shape
    d, n_sh = w_in_st.shape[1], w_in_st.shape[2]
    tm, tk = min(s, 512), min(n_sh, 2048)
    per = n_sh // tk

    ch = min(tm, 2 * EPILOGUE_ROWS)

    def epilogue(acc_ref, ex, outs):
        for c in range(tm // ch):
            rows = slice(c * ch, (c + 1) * ch)
            outs[0][rows, :] = scale_base * ex[0][rows, :] + acc_ref[rows, :]

    row = pl.BlockSpec((tm, d), lambda i, j, k: (i, 0))
    return _mm(
        name, (s // tm, 1, kdim // tk), dh, pl.BlockSpec((tm, tk), lambda i, j, k: (i, k)),
        w_in_st, pl.BlockSpec((None, d, tk), lambda i, j, k: (k // per, 0, k % per)), NT,
        [base], [row], [jax.ShapeDtypeStruct((s, d), F32)], [row], epilogue, (tm, d),
        acc_as_ref=True, run_after=run_after)[0]


def _chip_peers():
    x, y, c = lax.axis_index("x"), lax.axis_index("y"), lax.axis_index("c")
    return x, y, c, [(1 - x, y), (x, 1 - y), (1 - x, 1 - y)]


GATHER, GATHER_HALF, SCATTER, SIBLING = "gather", "gather_half", "scatter", "sibling"


def _exchange_peers(mode):
    x, y, c, chips = _chip_peers()
    if mode == SIBLING:
        return x, y, c, [(x, y, 1 - c)]
    return x, y, c, [(px, py, c) for px, py in chips]


def _core_half(ref_or_shape, c):
    rows = (ref_or_shape.shape[0]) // 2
    return pl.ds(c * rows, rows)


def _exchange_descriptor(mode, src, land, send, recv, p, peer, me, arriving):
    pid = 2 * peer[0] + peer[1]
    if mode == GATHER:
        src_ref, dst_ref = src, land.at[pid if arriving else me]
    elif mode == GATHER_HALF:
        rows = _core_half(src, peer[2])
        src_ref, dst_ref = src.at[rows], land.at[pid if arriving else me, rows]
    elif mode == SCATTER:
        src_ref, dst_ref = src.at[pid], land.at[p]
    else:
        src_ref, dst_ref = src, land
    return pltpu.make_async_remote_copy(
        src_ref=src_ref, dst_ref=dst_ref, send_sem=send.at[p], recv_sem=recv.at[p],
        device_id=peer, device_id_type=MESH)


def _exchange_start(name, mode, srcs, land_shapes):
    n = len(srcs)
    lands = [_hbm(lax.empty(shape, src.dtype)) for shape, src in zip(land_shapes, srcs)]

    def body(*refs):
        src_refs, land_refs = refs[:n], refs[n:2 * n]
        sends, recvs = refs[2 * n:3 * n], refs[3 * n:4 * n]
        token = refs[6 * n]
        x, y, c, peers = _exchange_peers(mode)
        me = 2 * x + y
        for w in range(n):
            for p, peer in enumerate(peers):
                _exchange_descriptor(mode, src_refs[w], land_refs[w], sends[w], recvs[w], p, peer,
                                     me, arriving=False).start()
        token[...] = jnp.zeros_like(token)

    sem = pltpu.SemaphoreType.DMA((3,))
    outs = pl.pallas_call(
        body, name=name, in_specs=[HBM_SPEC] * (2 * n),
        out_specs=[SEM_SPEC] * (2 * n) + [HBM_SPEC] * (2 * n) + [pl.BlockSpec(memory_space=pltpu.VMEM)],
        out_shape=[sem] * (2 * n) + [pltpu.HBM(a.shape, a.dtype) for a in (*srcs, *lands)]
        + [jax.ShapeDtypeStruct((8, 128), F32)],
        input_output_aliases={i: 2 * n + i for i in range(2 * n)},
        compiler_params=pltpu.CompilerParams(has_side_effects=DATAFLOW),
    )(*[_hbm(a) for a in srcs], *lands)
    return {"send": outs[:n], "recv": outs[n:2 * n], "src": outs[2 * n:3 * n],
            "land": outs[3 * n:4 * n], "token": outs[4 * n]}


def _exchange_wait(name, mode, started, which, after):
    m = len(which)

    def body(*refs):
        src_refs, land_refs = refs[:m], refs[m:2 * m]
        sends, recvs = refs[2 * m:3 * m], refs[3 * m:4 * m]
        x, y, c, peers = _exchange_peers(mode)
        me = 2 * x + y
        for w in range(m):
            for p, peer in enumerate(peers):
                _exchange_descriptor(mode, src_refs[w], land_refs[w], sends[w], recvs[w], p, peer,
                                     me, arriving=False).wait_send()
                _exchange_descriptor(mode, src_refs[w], land_refs[w], sends[w], recvs[w], p, peer,
                                     me, arriving=True).wait_recv()

    pick = lambda key: [started[key][w] for w in which]
    bufs = pick("src") + pick("land")
    after = list(after) if isinstance(after, (list, tuple)) else [after]
    outs = pl.pallas_call(
        body, name=name,
        in_specs=[HBM_SPEC] * (2 * m) + [SEM_SPEC] * (2 * m) + [ANY_SPEC] * len(after),
        out_specs=[HBM_SPEC] * (2 * m), out_shape=[pltpu.HBM(a.shape, a.dtype) for a in bufs],
        input_output_aliases={i: i for i in range(2 * m)},
        compiler_params=pltpu.CompilerParams(has_side_effects=DATAFLOW),
    )(*bufs, *pick("send"), *pick("recv"), *after)
    return outs[:m], outs[m:]


def _to_bf16(name, a, run_after):
    r, c = a.shape
    tm = min(r, 512)

    def body(a_ref, after_ref, out_ref):
        out_ref[...] = a_ref[...].astype(BF16)

    blk = pl.BlockSpec((tm, c), lambda i: (i, 0))
    return pl.pallas_call(
        body, name=name, grid=(r // tm,), in_specs=[blk, ANY_SPEC], out_specs=blk,
        out_shape=jax.ShapeDtypeStruct((r, c), BF16), compiler_params=_params(("parallel",)),
    )(a, run_after)


def _swap_halves(name, lands):
    n = len(lands)

    def body(*refs):
        bufs = refs[n:2 * n]
        send, recv = refs[2 * n:]
        x, y, c, chips = _chip_peers()
        started = []
        for w in range(n):
            half = bufs[w].shape[1] // 2
            for p, (px, py) in enumerate(chips):
                mine = bufs[w].at[2 * px + py, pl.ds(c * half, half)]
                cp = pltpu.make_async_remote_copy(
                    src_ref=mine, dst_ref=mine, send_sem=send.at[w, p], recv_sem=recv.at[w, p],
                    device_id=(x, y, 1 - c), device_id_type=MESH)
                cp.start()
                started.append(cp)
        for w in range(n):
            half = bufs[w].shape[1] // 2
            for p, (px, py) in enumerate(chips):
                theirs = bufs[w].at[2 * px + py, pl.ds((1 - c) * half, half)]
                pltpu.make_async_remote_copy(
                    src_ref=theirs, dst_ref=theirs, send_sem=send.at[w, p], recv_sem=recv.at[w, p],
                    device_id=(x, y, 1 - c), device_id_type=MESH).wait_recv()
        for cp in started:
            cp.wait_send()

    return pl.pallas_call(
        body, name=name, in_specs=[ANY_SPEC] * n, out_specs=[ANY_SPEC] * n,
        out_shape=[jax.ShapeDtypeStruct(a.shape, a.dtype) for a in lands],
        input_output_aliases={i: i for i in range(n)},
        scratch_shapes=[pltpu.SemaphoreType.DMA((n, 3)), pltpu.SemaphoreType.DMA((n, 3))],
    )(*lands)


def _place_own(name, shard, land, me):
    r, c = shard.shape
    tm = min(r, 512)

    def body(me_ref, shard_ref, land_ref, out_ref):
        out_ref[...] = shard_ref[...]

    return pl.pallas_call(
        body, name=name,
        grid_spec=pltpu.PrefetchScalarGridSpec(
            num_scalar_prefetch=1, grid=(r // tm,),
            in_specs=[pl.BlockSpec((tm, c), lambda i, me_ref: (i, 0)), ANY_SPEC],
            out_specs=pl.BlockSpec((None, tm, c), lambda i, me_ref: (me_ref[0], i, 0))),
        out_shape=jax.ShapeDtypeStruct(land.shape, land.dtype), input_output_aliases={2: 0},
        compiler_params=_params(("arbitrary",)),
    )(me, shard, land)


def _sum_slabs(name, grads, land, me):
    _, r, c = grads.shape
    tm = min(r, 256)

    def body(me_ref, own_ref, land_ref, out_ref):
        acc = own_ref[...].astype(F32)
        for p in range(3):
            acc = acc + land_ref[p].astype(F32)
        out_ref[...] = acc

    return pl.pallas_call(
        body, name=name,
        grid_spec=pltpu.PrefetchScalarGridSpec(
            num_scalar_prefetch=1, grid=(r // tm,),
            in_specs=[pl.BlockSpec((None, tm, c), lambda i, me_ref: (me_ref[0], i, 0)),
                      pl.BlockSpec((3, tm, c), lambda i, me_ref: (0, i, 0))],
            out_specs=pl.BlockSpec((tm, c), lambda i, me_ref: (i, 0))),
        out_shape=jax.ShapeDtypeStruct((r, c), F32), compiler_params=_params(("parallel",)),
    )(me, grads, land)


def _allreduce_stats(stats, run_after):
    n = len(stats)

    def body(*refs):
        ins, outs = refs[:n], refs[n + 1:2 * n + 1]
        mine, gath = refs[2 * n + 1:3 * n + 1], refs[3 * n + 1:4 * n + 1]
        send, recv = refs[4 * n + 1:]
        x, y, c = lax.axis_index("x"), lax.axis_index("y"), lax.axis_index("c")
        me = 4 * x + 2 * y + c
        flips = [(bx, by, bc) for bx in (0, 1) for by in (0, 1) for bc in (0, 1)][1:]

        def peer(f):
            return (x + f[0] * (1 - 2 * x), y + f[1] * (1 - 2 * y), c + f[2] * (1 - 2 * c))

        copies = []
        for t in range(n):
            tot = ins[t][0]
            for b in range(1, ins[t].shape[0]):
                tot = tot + ins[t][b]
            mine[t][...] = tot
            gath[t][me] = tot
            for k, f in enumerate(flips):
                cp = pltpu.make_async_remote_copy(
                    src_ref=mine[t], dst_ref=gath[t].at[me], send_sem=send.at[t, k],
                    recv_sem=recv.at[t, k], device_id=peer(f), device_id_type=MESH)
                cp.start()
                copies.append(cp)
        for t in range(n):
            for k, f in enumerate(flips):
                px, py, pc = peer(f)
                pltpu.make_async_remote_copy(
                    src_ref=mine[t], dst_ref=gath[t].at[4 * px + 2 * py + pc], send_sem=send.at[t, k],
                    recv_sem=recv.at[t, k], device_id=(px, py, pc), device_id_type=MESH).wait_recv()
        for cp in copies:
            cp.wait_send()
        for t in range(n):
            tot = gath[t][0]
            for dev in range(1, 8):
                tot = tot + gath[t][dev]
            outs[t][...] = tot

    vm = pl.BlockSpec(memory_space=pltpu.VMEM)
    return pl.pallas_call(
        body, name="allreduce_stats", in_specs=[vm] * n + [ANY_SPEC], out_specs=[vm] * n,
        out_shape=[jax.ShapeDtypeStruct(s.shape[1:], F32) for s in stats],
        scratch_shapes=[pltpu.VMEM(s.shape[1:], F32) for s in stats]
        + [pltpu.VMEM((8, *s.shape[1:]), F32) for s in stats]
        + [pltpu.SemaphoreType.DMA((n, 7)), pltpu.SemaphoreType.DMA((n, 7))],
    )(*stats, run_after)


def _adamw(name, w, m, v, g_parts):
    r, c = w.shape
    tm = min(r, 128)
    n_g = len(g_parts)

    def body(*refs):
        w_ref, m_ref, v_ref = refs[:3]
        g_refs = refs[3:3 + n_g]
        g_out, d_out, m_out, v_out = refs[3 + n_g:]
        g = g_refs[0][...]
        for gr in g_refs[1:]:
            g = g + gr[...]
        m_new = ADAM_B1 * m_ref[...] + (1.0 - ADAM_B1) * g
        v_new = ADAM_B2 * v_ref[...] + (1.0 - ADAM_B2) * (g * g)
        m_hat = m_new / (1.0 - ADAM_B1 ** ADAM_STEP)
        v_hat = v_new / (1.0 - ADAM_B2 ** ADAM_STEP)
        g_out[...] = g
        d_out[...] = -ADAM_LR * (m_hat / (jnp.sqrt(v_hat) + ADAM_EPS) + ADAM_WD * w_ref[...])
        m_out[...] = m_new
        v_out[...] = v_new

    blk = pl.BlockSpec((tm, c), lambda i: (i, 0))
    out = jax.ShapeDtypeStruct((r, c), F32)
    return pl.pallas_call(
        body, name=name, grid=(r // tm,), in_specs=[blk] * (3 + n_g), out_specs=[blk] * 4,
        out_shape=[out] * 4, compiler_params=_params(("parallel",)),
    )(w, m, v, *g_parts)


def _rope_tables(positions):
    half = HEAD_DIM // 2
    inv_freq = ROPE_THETA ** (-jnp.arange(half, dtype=F32) / half)
    ang = positions.astype(F32)[0, :, None] * inv_freq
    cos, sin = jnp.cos(ang), jnp.sin(ang)
    return jnp.concatenate([cos, cos], axis=-1), jnp.concatenate([-sin, sin], axis=-1)


def kernel(x, positions, w_in, w_pool, pool_scale, w_branch_attn, w_branch_pool, w_out, ln_mix_g, ln_mix_b, w_ff1, w_ff2, ln_ff_g, ln_ff_b, loss_target, m_w_in, m_w_pool, m_pool_scale, m_w_branch_attn, m_w_branch_pool, m_w_out, m_ln_mix_g, m_ln_mix_b, m_w_ff1, m_w_ff2, m_ln_ff_g, m_ln_ff_b, v_w_in, v_w_pool, v_pool_scale, v_w_branch_attn, v_w_branch_pool, v_w_out, v_ln_mix_g, v_ln_mix_b, v_w_ff1, v_w_ff2, v_ln_ff_g, v_ln_ff_b):
    s, d = x.shape[1], x.shape[2]
    aw = d // 2
    ng = len(POOL_WINDOWS)
    pgw = aw // ng
    x2d, target = x[0], loss_target[0]
    xb = x2d.astype(BF16)
    cos_t, sin_t = _rope_tables(positions)

    big = {"w_in": w_in[0], "w_pool": w_pool[0].reshape(-1, pgw), "w_branch_attn": w_branch_attn[0],
           "w_branch_pool": w_branch_pool[0], "w_out": w_out[0], "w_ff1": w_ff1[0], "w_ff2": w_ff2[0]}
    names = list(big)
    me_chip = (2 * lax.axis_index("x") + lax.axis_index("y")).astype(jnp.int32).reshape(1)
    land_shapes = [(N_CHIPS, *big[k].shape) for k in names]
    gathering_in = _exchange_start("gather_start_in", GATHER_HALF,
                                   [_to_bf16("to_bf16_w_in", big["w_in"], positions)], land_shapes[:1])
    shards = [_to_bf16(f"to_bf16_{k}", big[k], gathering_in["token"]) for k in names[1:]]
    gathering = _exchange_start("gather_start", GATHER, shards, land_shapes[1:])

    def gathered(name, which, after):
        srcs, lands = _exchange_wait(f"gather_wait_{name}", GATHER, gathering, which, after)
        return [_place_own(f"place_own_{names[w + 1]}", srcs[i], lands[i], me_chip)
                for i, w in enumerate(which)]

    rows_sh = pgw // N_CHIPS
    dff = N_CHIPS * big["w_ff2"].shape[0]

    srcs, lands = _exchange_wait("gather_wait_in", GATHER_HALF, gathering_in, [0],
                                 [gathering["token"], xb, cos_t, sin_t])
    w_in_st = _place_own("place_own_w_in", srcs[0], _swap_halves("swap_halves_in", lands)[0], me_chip)
    h, hv4, hv16 = _in_proj(xb, w_in_st, cos_t, sin_t, aw)
    (wp_st,) = gathered("pool", [0], h)
    wp = wp_st.reshape(N_CHIPS, ng, rows_sh, pgw).transpose(1, 0, 2, 3).reshape(ng, pgw, pgw)
    qkv = {1: (h, h, h), 4: (hv4, hv4, hv4), 16: (hv16, hv16, hv16)}
    offs = {dil: (0, dil, 2 * dil) for dil in DILATIONS}
    o_parts, lse_parts = [], []
    for dil in DILATIONS:
        o_p, lse_p = _attn_fwd(f"attn_fwd_d{dil}", *qkv[dil], offs[dil], aw, dil * aw)
        o_parts.append(o_p)
        lse_parts.append(lse_p)
    o_attn, lse, lse4, lse16 = _attn_combine(o_parts[0], lse_parts[0], o_parts[1:], lse_parts[1:])
    p, pw, y = _pool_fwd(h, wp, pool_scale, aw)
    wba_st, wbp_st, w_out_st = gathered("mix", [1, 2, 3], y)
    w_out_full = w_out_st.reshape(d, d)
    ya, yp, merged = _branch_merge(o_attn, y, wba_st, wbp_st, h, aw, d)
    xhat1, rstd1, x1b = _mix_norm(merged, w_out_full, x2d, ln_mix_g, ln_mix_b)
    w1_st, w2_st = gathered("ff", [4, 5], x1b)
    w2_full = w2_st.reshape(dff, d)
    r, r_slope = _ff_up(x1b, w1_st)
    dz2, dz2b, st2 = _ff_down_loss(r, w2_full, xhat1, ln_mix_g, ln_mix_b, ln_ff_g, ln_ff_b, target)

    def scatter_start(name, grads):
        return _exchange_start(f"scatter_start_{name}", SCATTER, grads, [(3, *g.shape[1:]) for g in grads])

    da = _ff_down_bwd(dz2b, w2_full, r_slope)
    g_w2 = _wgrad("wgrad_ff2", r, dz2b, d).reshape(N_CHIPS, dff // N_CHIPS, d)
    g_w1 = _wgrad("wgrad_ff1", x1b, da, dff // N_CHIPS)
    sent_ff = scatter_start("ff", [g_w1, g_w2])
    dz1, dz1b, st1 = _ff_up_bwd(da, w1_st, dz2, xhat1, rstd1, ln_mix_g, [sent_ff["token"]])
    dya, dyp, dh_gates = _mix_bwd(dz1b, w_out_full, h, ya, yp, aw)
    g_wout = _wgrad("wgrad_out", merged, dz1b, d).reshape(N_CHIPS, d // N_CHIPS, d)
    g_wba = _wgrad("wgrad_branch_attn", o_attn, dya, d // N_CHIPS)
    g_wbp = _wgrad("wgrad_branch_pool", y, dyp, d // N_CHIPS)
    do, delta, do4, do16, delta4, delta16 = _attn_out_bwd(dya, wba_st, o_attn)
    dpw, stp = _pool_out_bwd(dyp, wbp_st, pw, pool_scale)
    dwp, du = _pool_bwd(dpw, p, wp)
    g_wp = dwp.reshape(ng, N_CHIPS, rows_sh, pgw).transpose(1, 0, 2, 3).reshape(
        N_CHIPS, ng * rows_sh, pgw).astype(BF16)
    sent_mix = scatter_start("mix", [g_wp, g_wba, g_wbp, g_wout])

    bwd_in = {1: (do, lse, delta), 4: (do4, lse4, delta4), 16: (do16, lse16, delta16)}
    dqkv = {}
    for dil in DILATIONS:
        args = (*qkv[dil], *bwd_in[dil], offs[dil], aw, dil * aw)
        dqkv[dil] = _attn_bwd(f"attn_bwd_d{dil}", *args)
    dh = _qkvu_grad(dqkv[1], dqkv[4], dqkv[16], du, cos_t, sin_t, dh_gates)
    g_win = _wgrad("wgrad_in", xb, dh, d)
    sent_in = scatter_start("in", [g_win])
    grad_x = _in_proj_bwd_x("in_proj_bwd", dh, w_in_st, dz1, ALPHA, [sent_mix["token"], sent_in["token"]])

    moments = {"w_in": (m_w_in, v_w_in), "w_pool": (m_w_pool, v_w_pool),
               "w_branch_attn": (m_w_branch_attn, v_w_branch_attn),
               "w_branch_pool": (m_w_branch_pool, v_w_branch_pool), "w_out": (m_w_out, v_w_out),
               "w_ff1": (m_w_ff1, v_w_ff1), "w_ff2": (m_w_ff2, v_w_ff2)}
    originals = {"w_in": w_in, "w_pool": w_pool, "w_branch_attn": w_branch_attn,
                 "w_branch_pool": w_branch_pool, "w_out": w_out, "w_ff1": w_ff1, "w_ff2": w_ff2}
    res = {}

    def summed(name, sent, keys, after):
        srcs, lands = _exchange_wait(f"scatter_wait_{name}", SCATTER, sent, list(range(len(keys))), after)
        parts = [_sum_slabs(f"sum_slabs_{k}", srcs[i], lands[i], me_chip) for i, k in enumerate(keys)]
        return _exchange_start(f"cores_start_{name}", SIBLING, parts, [a.shape for a in parts])

    def updated(name, swapping, keys, after):
        mine, other = _exchange_wait(f"cores_wait_{name}", SIBLING, swapping, list(range(len(keys))), after)
        for i, k in enumerate(keys):
            mk, vk = (a.reshape(big[k].shape) for a in moments[k])
            outs = _adamw(f"adamw_{k}", big[k], mk, vk, [mine[i], other[i]])
            res[k] = [o.reshape(originals[k].shape) for o in outs]

    groups = {"ff": ["w_ff1", "w_ff2"], "mix": ["w_pool", "w_branch_attn", "w_branch_pool", "w_out"],
              "in": ["w_in"]}
    swap_ff = summed("ff", sent_ff, groups["ff"], grad_x)
    swap_mix = summed("mix", sent_mix, groups["mix"], swap_ff["token"])
    swap_in = summed("in", sent_in, groups["in"], swap_mix["token"])
    updated("ff", swap_ff, groups["ff"], swap_in["token"])
    updated("mix", swap_mix, groups["mix"], res["w_ff2"][0])
    updated("in", swap_in, groups["in"], res["w_out"][0])
    tot2, tot1, totp = _allreduce_stats([st2, st1, stp], res["w_in"][0])

    def pad_d(a):
        return jnp.pad(a, ((0, 0), (0, d - a.shape[1])))

    small = ["ln_mix_g", "ln_mix_b", "ln_ff_g", "ln_ff_b", "pool_scale"]
    small_w = {"ln_mix_g": ln_mix_g, "ln_mix_b": ln_mix_b, "ln_ff_g": ln_ff_g, "ln_ff_b": ln_ff_b,
               "pool_scale": pool_scale}
    small_m = {"ln_mix_g": m_ln_mix_g, "ln_mix_b": m_ln_mix_b, "ln_ff_g": m_ln_ff_g,
               "ln_ff_b": m_ln_ff_b, "pool_scale": m_pool_scale}
    small_v = {"ln_mix_g": v_ln_mix_g, "ln_mix_b": v_ln_mix_b, "ln_ff_g": v_ln_ff_g,
               "ln_ff_b": v_ln_ff_b, "pool_scale": v_pool_scale}
    small_g = [tot1[0:1], tot1[1:2], tot2[0:1], tot2[1:2], pad_d(totp[0:1])]

    def pack(rows):
        return jnp.concatenate([pad_d(a) for a in rows] + [jnp.zeros((8 - len(rows), d), F32)], axis=0)

    outs = _adamw("adamw_small", pack([small_w[k] for k in small]), pack([small_m[k] for k in small]),
                  pack([small_v[k] for k in small]), [pack(small_g)])
    for i, k in enumerate(small):
        res[k] = [o[i:i + 1, :small_w[k].shape[1]] for o in outs]
    loss = tot2[2, 0]

    order = ["w_in", "w_pool", "pool_scale", "w_branch_attn", "w_branch_pool", "w_out", "ln_mix_g",
             "ln_mix_b", "w_ff1", "w_ff2", "ln_ff_g", "ln_ff_b"]
    result = [loss, grad_x[None]]
    for idx in range(4):
        result += [res[k][idx] for k in order]
    return tuple(result)
```

```python
import jax
import jax.numpy as jnp
from jax import lax
from jax.experimental import pallas as pl
from jax.experimental.pallas import tpu as pltpu

F32 = jnp.float32
BF16 = jnp.bfloat16
MESH = pl.DeviceIdType.MESH

HEAD_DIM = 128
SUB_BLOCK = 128
ATTN_STEP_ROWS = 512
DILATIONS = (1, 4, 16)
POOL_WINDOWS = (2, 4, 8, 16)
POOL_HALO = 16
ROPE_THETA = 10000.0
LN_EPS = 1e-5
ALPHA = 2.0 ** 0.25
ADAM_LR, ADAM_B1, ADAM_B2, ADAM_EPS, ADAM_WD, ADAM_STEP = 0.001, 0.9, 0.999, 1e-08, 0.01, 10
NEG = -1e30
N_CHIPS = 4
VMEM_LIMIT = 62 * 1024 * 1024
EPILOGUE_ROWS = 128

def _params(sem=None, vmem=VMEM_LIMIT):
    kw = {"vmem_limit_bytes": vmem}
    if sem is not None:
        kw["dimension_semantics"] = sem
    return pltpu.CompilerParams(**kw)


def _dot(a, b, contract):
    return lax.dot_general(a, b, (contract, ((), ())), preferred_element_type=F32)


ANY_SPEC = pl.BlockSpec(memory_space=pl.ANY)
HBM_SPEC = pl.BlockSpec(memory_space=pltpu.HBM)
SEM_SPEC = pl.BlockSpec(memory_space=pltpu.SEMAPHORE)
DATAFLOW = pltpu.SideEffectType.DATAFLOW_SIDE_EFFECTING


def _hbm(a):
    return pltpu.with_memory_space_constraint(a, pltpu.HBM)


NN = ((1,), (0,))
NT = ((1,), (1,))
TN = ((0,), (0,))


def _mm(name, grid, a, a_spec, b, b_spec, contract, extras, extra_specs, out_shape, out_specs,
        epilogue, acc_shape, acc_as_ref=False, run_after=(), scratch=(),
        semantics=("parallel", "parallel", "arbitrary")):
    nk = grid[2]
    n_ex = len(extras)
    n_in = 2 + n_ex + len(run_after)
    n_out = len(out_shape)
    n_scr = len(scratch)

    def body(*refs):
        a_ref, b_ref = refs[0], refs[1]
        ex = refs[2:2 + n_ex]
        outs = refs[n_in:n_in + n_out]
        scr = refs[n_in + n_out:n_in + n_out + n_scr]
        if nk == 1:
            epilogue(_dot(a_ref[...], b_ref[...], contract), ex, outs, *scr)
        else:
            acc = refs[n_in + n_out + n_scr]
            k = pl.program_id(2)

            @pl.when(k == 0)
            def _():
                acc[...] = jnp.zeros_like(acc)

            acc[...] += _dot(a_ref[...], b_ref[...], contract)

            @pl.when(k == nk - 1)
            def _():
                epilogue(acc if acc_as_ref else acc[...], ex, outs, *scr)

    acc_scratch = [pltpu.VMEM(acc_shape, F32)] if nk > 1 else []
    return pl.pallas_call(
        body, name=name, grid=grid,
        in_specs=[a_spec, b_spec, *extra_specs, *[ANY_SPEC] * len(run_after)],
        out_specs=out_specs, out_shape=out_shape, scratch_shapes=[*scratch, *acc_scratch],
        compiler_params=_params(semantics),
    )(a, b, *extras, *run_after)


def _stats_rows(rows, width):
    idx = lax.broadcasted_iota(jnp.int32, (8, width), 0)
    out = jnp.zeros((8, width), F32)
    for r, v in enumerate(rows):
        out = jnp.where(idx == r, jnp.broadcast_to(v, (8, width)), out)
    return out


def _layer_norm_fwd(z):
    mu = jnp.mean(z, axis=-1, keepdims=True)
    zc = z - mu
    var = jnp.mean(zc * zc, axis=-1, keepdims=True)
    rstd = lax.rsqrt(var + LN_EPS)
    return zc * rstd, rstd


def _layer_norm_bwd(dy, xhat, rstd, g):
    dxh = dy * g
    m1 = jnp.mean(dxh, axis=-1, keepdims=True)
    m2 = jnp.mean(dxh * xhat, axis=-1, keepdims=True)
    return rstd * (dxh - m1 - xhat * m2)


def _heads_scratch(rows, width):
    return pltpu.VMEM((width // HEAD_DIM, rows, HEAD_DIM), F32)


def _to_views(src_ref, view_refs, dtype, heads=None):
    nh, rows, _ = src_ref.shape
    width = nh * HEAD_DIM
    for dil, view_ref in zip(DILATIONS[1:], view_refs):
        for r in range(dil):
            for hh in (range(nh) if heads is None else heads):
                c0 = r * width + hh * HEAD_DIM
                view_ref[:, c0:c0 + HEAD_DIM] = (
                    src_ref[hh, pl.ds(r, rows // dil, stride=dil), :].astype(dtype))


def _from_views(view_refs, dst_refs):
    nh, rows, _ = dst_refs[0].shape
    width = nh * HEAD_DIM
    for dil, view_ref, dst_ref in zip(DILATIONS[1:], view_refs, dst_refs):
        for r in range(dil):
            for hh in range(nh):
                c0 = r * width + hh * HEAD_DIM
                dst_ref[hh, pl.ds(r, rows // dil, stride=dil), :] = (
                    view_ref[:, c0:c0 + HEAD_DIM].astype(F32))


def _view_shape(rows, width, dil, parts=1):
    return (rows // dil, parts * dil * width)


def _in_proj(xb, w_in_st, cos_t, sin_t, aw):
    s, d = xb.shape
    n_sh = w_in_st.shape[2]
    tm, tn = min(s, 1024), aw
    per = n_sh // tn
    grid = (s // tm, (N_CHIPS * n_sh) // tn, 1)

    def epilogue(acc, ex, outs, scr):
        cos_ref, sin_ref = ex
        h_ref, v4_ref, v16_ref = outs
        seg = pl.program_id(1)

        heads = [slice(hh * HEAD_DIM, (hh + 1) * HEAD_DIM) for hh in range(tn // HEAD_DIM)]

        @pl.when(seg < 2)
        def _():
            cos, sin = cos_ref[...], sin_ref[...]
            for hh, sl in enumerate(heads):
                t = acc[:, sl]
                scr[hh] = t * cos + pltpu.roll(t, HEAD_DIM // 2, 1) * sin

        @pl.when(seg == 2)
        def _():
            for hh, sl in enumerate(heads):
                scr[hh] = acc[:, sl]

        @pl.when(seg < 3)
        def _():
            for hh, sl in enumerate(heads):
                h_ref[:, sl] = scr[hh].astype(BF16)
            _to_views(scr, (v4_ref, v16_ref), BF16)

        @pl.when(seg == 3)
        def _():
            h_ref[...] = acc.astype(BF16)

        @pl.when(seg >= 4)
        def _():
            h_ref[...] = (0.5 * jnp.tanh(0.5 * acc) + 0.5).astype(BF16)

    def view_spec(dil):
        return pl.BlockSpec((tm // dil, dil * aw), lambda i, j, k: (i, jnp.minimum(j, 2)))

    return _mm(
        "in_proj", grid, xb, pl.BlockSpec((tm, d), lambda i, j, k: (i, 0)),
        w_in_st, pl.BlockSpec((None, d, tn), lambda i, j, k: (j // per, 0, j % per)), NN,
        [cos_t, sin_t], [pl.BlockSpec((tm, HEAD_DIM), lambda i, j, k: (i, 0))] * 2,
        [jax.ShapeDtypeStruct((s, N_CHIPS * n_sh), BF16)]
        + [jax.ShapeDtypeStruct(_view_shape(s, aw, dil, 3), BF16) for dil in DILATIONS[1:]],
        [pl.BlockSpec((tm, tn), lambda i, j, k: (i, j))] + [view_spec(dil) for dil in DILATIONS[1:]],
        epilogue, None, scratch=[_heads_scratch(tm, aw)],
        semantics=("parallel", "arbitrary", "arbitrary"))


def _pack_heads(cols):
    rows, rep = cols[0].shape[0], HEAD_DIM // len(cols)
    lane = lax.broadcasted_iota(jnp.int32, (rows, HEAD_DIM), 1)
    out = jnp.zeros((rows, HEAD_DIM), F32)
    for hh, col in enumerate(cols):
        out = jnp.where((lane >= hh * rep) & (lane < (hh + 1) * rep), col, out)
    return out


def _head_col(packed, hh, nh):
    lane = lax.broadcasted_iota(jnp.int32, packed.shape, 1)
    return jnp.sum(jnp.where(lane == hh * (HEAD_DIM // nh), packed, 0.0), axis=-1, keepdims=True)


def _band_masks(block_idx):
    qi = lax.broadcasted_iota(jnp.int32, (SUB_BLOCK, 2 * SUB_BLOCK), 0)
    kj = lax.broadcasted_iota(jnp.int32, (SUB_BLOCK, 2 * SUB_BLOCK), 1)
    first_key = jnp.where(block_idx > 0, 0, SUB_BLOCK)
    return (kj >= qi) & (kj <= qi + SUB_BLOCK) & (kj >= first_key)


def _attn_fwd(name, q, k, v, offs, cw, width):
    m = q.shape[0]
    nh = cw // HEAD_DIM
    rb = min(m, ATTN_STEP_ROWS)
    nsub = rb // SUB_BLOCK
    grid = (width // cw, m // rb)
    scale = HEAD_DIM ** -0.5

    def body(q_ref, k_ref, v_ref, o_ref, lse_ref, kprev, vprev):
        n = pl.program_id(1)

        @pl.when(n == 0)
        def _():
            kprev[...] = jnp.zeros_like(kprev)
            vprev[...] = jnp.zeros_like(vprev)

        heads = [slice(hh * HEAD_DIM, (hh + 1) * HEAD_DIM) for hh in range(nh)]
        valids = [_band_masks(n)] + [_band_masks(1)] * (nsub - 1)

        def rows(sub):
            return slice(sub * SUB_BLOCK, (sub + 1) * SUB_BLOCK)

        def cat(prev, ref, sub, sl):
            if sub == 0:
                return jnp.concatenate([prev[:, sl], ref[rows(0), sl]], axis=0)
            return ref[(sub - 1) * SUB_BLOCK:(sub + 1) * SUB_BLOCK, sl]

        scs = [[_dot(q_ref[rows(sub), sl], cat(kprev, k_ref, sub, sl), NT) for sl in heads]
               for sub in range(nsub)]
        for sub in range(nsub):
            lses = []
            for hh, sl in enumerate(heads):
                sc = jnp.where(valids[sub], scs[sub][hh] * scale, NEG)
                mx = jnp.max(sc, axis=-1, keepdims=True)
                p = jnp.exp(sc - mx)
                l = jnp.sum(p, axis=-1, keepdims=True)
                o = _dot(p.astype(BF16), cat(vprev, v_ref, sub, sl), NN) / l
                o_ref[rows(sub), sl] = o.astype(BF16)
                lses.append(mx + jnp.log(l))
            lse_ref[rows(sub), :] = _pack_heads(lses)
        kprev[...] = k_ref[rows(nsub - 1), :]
        vprev[...] = v_ref[rows(nsub - 1), :]

    def cur(off):
        return pl.BlockSpec((rb, cw), lambda j, n: (n, j + off))

    return pl.pallas_call(
        body, name=name, grid=grid,
        in_specs=[cur(offs[0]), cur(offs[1]), cur(offs[2])],
        out_specs=[cur(0), pl.BlockSpec((rb, HEAD_DIM), lambda j, n: (n, j))],
        out_shape=[jax.ShapeDtypeStruct((m, width), BF16),
                   jax.ShapeDtypeStruct((m, width // cw * HEAD_DIM), F32)],
        scratch_shapes=[pltpu.VMEM((SUB_BLOCK, cw), BF16)] * 2,
        compiler_params=_params(("parallel", "arbitrary")),
    )(q, k, v)


def _view_spec(tm, aw, dil):
    return pl.BlockSpec((tm // dil, dil * aw), lambda i: (i, 0))


def _attn_combine(o1, l1, o_views, l_views):
    s, aw = o1.shape
    nh = aw // HEAD_DIM
    tm = min(s, 512)

    def body(o1_ref, l1_ref, o4_ref, o16_ref, l4_ref, l16_ref, o_ref, lse_ref, lse4_ref, lse16_ref,
             so4, so16, sl4, sl16, stot):
        _from_views((o4_ref, o16_ref), (so4, so16))
        _from_views((l4_ref, l16_ref), (sl4, sl16))
        a, b, c = l1_ref[...], sl4[0], sl16[0]
        mx = jnp.maximum(jnp.maximum(a, b), c)
        ea, eb, ec = jnp.exp(a - mx), jnp.exp(b - mx), jnp.exp(c - mx)
        tot = ea + eb + ec
        inv = 1.0 / tot
        wa, wb, wc = ea * inv, eb * inv, ec * inv
        lse_tot = mx + jnp.log(tot)
        stot[0] = lse_tot
        lse_ref[...] = lse_tot
        _to_views(stot, (lse4_ref, lse16_ref), F32)
        for hh in range(nh):
            sl = slice(hh * HEAD_DIM, (hh + 1) * HEAD_DIM)
            o = (_head_col(wa, hh, nh) * o1_ref[:, sl].astype(F32) + _head_col(wb, hh, nh) * so4[hh]
                 + _head_col(wc, hh, nh) * so16[hh])
            o_ref[:, sl] = o.astype(BF16)

    row = pl.BlockSpec((tm, aw), lambda i: (i, 0))
    stat = pl.BlockSpec((tm, HEAD_DIM), lambda i: (i, 0))
    views = [_view_spec(tm, aw, dil) for dil in DILATIONS[1:]]
    stat_views = [_view_spec(tm, HEAD_DIM, dil) for dil in DILATIONS[1:]]
    return pl.pallas_call(
        body, name="attn_combine", grid=(s // tm,), in_specs=[row, stat, *views, *stat_views],
        out_specs=[row, stat, *stat_views],
        out_shape=[jax.ShapeDtypeStruct((s, aw), BF16), jax.ShapeDtypeStruct((s, HEAD_DIM), F32)]
        + [jax.ShapeDtypeStruct(_view_shape(s, HEAD_DIM, dil), F32) for dil in DILATIONS[1:]],
        scratch_shapes=[_heads_scratch(tm, aw)] * 2 + [_heads_scratch(tm, HEAD_DIM)] * 3,
        compiler_params=_params(("parallel",)),
    )(o1, l1, *o_views, *l_views)


def _pool_counts(tm, rows, pgw, row0):
    t = lax.broadcasted_iota(jnp.int32, (rows, len(POOL_WINDOWS) * pgw), 0) + row0
    col = lax.broadcasted_iota(jnp.int32, (rows, len(POOL_WINDOWS) * pgw), 1)
    w = jnp.full((rows, len(POOL_WINDOWS) * pgw), POOL_WINDOWS[0], jnp.int32)
    for g in range(1, len(POOL_WINDOWS)):
        w = jnp.where(col >= g * pgw, POOL_WINDOWS[g], w)
    return jnp.minimum(t + 1, w).astype(F32)


def _window_sums(xs, direction, pgw):
    rows = xs.shape[0]
    acc = xs
    out = None
    col = lax.broadcasted_iota(jnp.int32, xs.shape, 1)
    for g, w in enumerate(POOL_WINDOWS):
        sh = w // 2
        acc = acc + pltpu.roll(acc, sh if direction > 0 else rows - sh, 0)
        out = acc if out is None else jnp.where(col >= g * pgw, acc, out)
    return out


def _pool_fwd(h, wp, scale, aw):
    s = h.shape[0]
    pw_ = aw
    pgw = pw_ // len(POOL_WINDOWS)
    tm = min(s, 512)
    hb = tm // POOL_HALO

    def body(u_ref, halo_ref, wp_ref, sc_ref, p_ref, pw_ref, y_ref):
        i = pl.program_id(0)
        u = u_ref[...].astype(F32)
        halo = halo_ref[...].astype(F32) * jnp.where(i > 0, 1.0, 0.0)
        xs = jnp.concatenate([halo, u], axis=0)
        sums = _window_sums(xs, +1, pgw)[POOL_HALO:]
        p = (sums / _pool_counts(tm, tm, pgw, i * tm) - u).astype(BF16)
        p_ref[...] = p
        sc = sc_ref[...]
        for g in range(len(POOL_WINDOWS)):
            sl = slice(g * pgw, (g + 1) * pgw)
            pw = _dot(p[:, sl], wp_ref[g], NN)
            pw_ref[:, sl] = pw.astype(BF16)
            y_ref[:, sl] = (pw * sc[:, sl]).astype(BF16)

    out = jax.ShapeDtypeStruct((s, pw_), BF16)
    row = pl.BlockSpec((tm, pw_), lambda i: (i, 0))
    return pl.pallas_call(
        body, name="pool_fwd", grid=(s // tm,),
        in_specs=[pl.BlockSpec((tm, pw_), lambda i: (i, 3)),
                  pl.BlockSpec((POOL_HALO, pw_), lambda i: (jnp.maximum(i * hb - 1, 0), 3)),
                  pl.BlockSpec(wp.shape, lambda i: (0, 0, 0)),
                  pl.BlockSpec((1, pw_), lambda i: (0, 0))],
        out_specs=[row, row, row], out_shape=[out, out, out],
        compiler_params=_params(("parallel",)),
    )(h, h, wp, scale)


def _branch_merge(o_attn, y, wba_st, wbp_st, h, aw, d):
    s = o_attn.shape[0]
    tn = wba_st.shape[2]
    tm = min(s, 1024)
    ga0 = 4 * aw // tn
    gp0 = (4 * aw + d) // tn

    def body(o_ref, y_ref, wa_ref, wp_ref, sga_ref, sgp_ref, ya_ref, yp_ref, mg_ref):
        ya = _dot(o_ref[...], wa_ref[...], NN)
        yp = _dot(y_ref[...], wp_ref[...], NN)
        ya_ref[...] = ya.astype(BF16)
        yp_ref[...] = yp.astype(BF16)
        mg_ref[...] = (sga_ref[...].astype(F32) * ya + sgp_ref[...].astype(F32) * yp).astype(BF16)

    out = jax.ShapeDtypeStruct((s, d), BF16)
    blk = pl.BlockSpec((tm, tn), lambda i, j: (i, j))
    return pl.pallas_call(
        body, name="branch_merge", grid=(s // tm, N_CHIPS),
        in_specs=[pl.BlockSpec((tm, aw), lambda i, j: (i, 0)),
                  pl.BlockSpec((tm, aw), lambda i, j: (i, 0)),
                  pl.BlockSpec((None, aw, tn), lambda i, j: (j, 0, 0)),
                  pl.BlockSpec((None, aw, tn), lambda i, j: (j, 0, 0)),
                  pl.BlockSpec((tm, tn), lambda i, j: (i, j + ga0)),
                  pl.BlockSpec((tm, tn), lambda i, j: (i, j + gp0))],
        out_specs=[blk, blk, blk], out_shape=[out, out, out],
        compiler_params=_params(("parallel", "parallel")),
    )(o_attn, y, wba_st, wbp_st, h, h)


def _mix_norm(merged, w_out, x, g1, b1):
    s, d = x.shape
    tm = min(s, 256)

    def epilogue(acc, ex, outs):
        x_ref, g_ref, b_ref = ex
        xh_ref, rs_ref, xb_ref = outs
        xhat, rstd = _layer_norm_fwd(ALPHA * x_ref[...] + acc)
        xh_ref[...] = xhat
        rs_ref[...] = rstd
        xb_ref[...] = (xhat * g_ref[...] + b_ref[...]).astype(BF16)

    row = pl.BlockSpec((tm, d), lambda i, j, k: (i, 0))
    vec = pl.BlockSpec((1, d), lambda i, j, k: (0, 0))
    return _mm(
        "mix_norm", (s // tm, 1, 1), merged, row, w_out, pl.BlockSpec((d, d), lambda i, j, k: (0, 0)),
        NN, [x, g1, b1], [row, vec, vec],
        [jax.ShapeDtypeStruct((s, d), F32), jax.ShapeDtypeStruct((s, 1), F32),
         jax.ShapeDtypeStruct((s, d), BF16)],
        [row, pl.BlockSpec((tm, 1), lambda i, j, k: (i, 0)), row], epilogue, None)


def _ff_up(x1b, w1_st):
    s, d = x1b.shape
    n_sh = w1_st.shape[2]
    tm, tn = min(s, 1024), min(n_sh, 1024)
    per = n_sh // tn

    def epilogue(acc, ex, outs):
        r = jnp.maximum(acc, 0.0)
        outs[0][...] = (r * r).astype(BF16)
        outs[1][...] = (2.0 * r).astype(BF16)

    blk = pl.BlockSpec((tm, tn), lambda i, j, k: (i, j))
    out = jax.ShapeDtypeStruct((s, N_CHIPS * n_sh), BF16)
    return _mm(
        "ff_up", (s // tm, N_CHIPS * per, 1), x1b, pl.BlockSpec((tm, d), lambda i, j, k: (i, 0)),
        w1_st, pl.BlockSpec((None, d, tn), lambda i, j, k: (j // per, 0, j % per)), NN, [], [],
        [out, out], [blk, blk], epilogue, None)


def _ff_down_loss(r, w2, xhat1, g1, b1, g2, b2, target):
    s, d = xhat1.shape
    dff = r.shape[1]
    tm, tk = min(s, 512), min(dff, 2048)
    ch = min(tm, EPILOGUE_ROWS)

    def epilogue(acc_ref, ex, outs):
        xh1_ref, g1_ref, b1_ref, g2_ref, b2_ref, t_ref = ex
        dz_ref, dzb_ref, st_ref = outs
        g1v, b1v, g2v, b2v = g1_ref[...], b1_ref[...], g2_ref[...], b2_ref[...]
        dg = db = loss = None
        for c in range(tm // ch):
            rows = slice(c * ch, (c + 1) * ch)
            x1 = xh1_ref[rows, :] * g1v + b1v
            xhat2, rstd2 = _layer_norm_fwd(ALPHA * x1 + acc_ref[rows, :])
            err = xhat2 * g2v + b2v - t_ref[rows, :]
            dy = err * (1.0 / d)
            dz = _layer_norm_bwd(dy, xhat2, rstd2, g2v)
            dz_ref[rows, :] = dz
            dzb_ref[rows, :] = dz.astype(BF16)
            parts = (jnp.sum(dy * xhat2, axis=0, keepdims=True), jnp.sum(dy, axis=0, keepdims=True),
                     jnp.sum(jnp.sum(err * err, axis=-1, keepdims=True), axis=0, keepdims=True))
            dg, db, loss = parts if c == 0 else (dg + parts[0], db + parts[1], loss + parts[2])
        st_ref[...] = _stats_rows([dg, db, jnp.broadcast_to((0.5 / d) * loss, (1, d))], d)

    row = pl.BlockSpec((tm, d), lambda i, j, k: (i, 0))
    vec = pl.BlockSpec((1, d), lambda i, j, k: (0, 0))
    return _mm(
        "ff_down_loss", (s // tm, 1, dff // tk), r, pl.BlockSpec((tm, tk), lambda i, j, k: (i, k)),
        w2, pl.BlockSpec((tk, d), lambda i, j, k: (k, 0)), NN,
        [xhat1, g1, b1, g2, b2, target], [row, vec, vec, vec, vec, row],
        [jax.ShapeDtypeStruct((s, d), F32), jax.ShapeDtypeStruct((s, d), BF16),
         jax.ShapeDtypeStruct((s // tm, 8, d), F32)],
        [row, row, pl.BlockSpec((None, 8, d), lambda i, j, k: (i, 0, 0))], epilogue, (tm, d),
        acc_as_ref=True)


def _ff_down_bwd(dz2b, w2, r_slope):
    s, d = dz2b.shape
    dff = r_slope.shape[1]
    tm, tn = min(s, 1024), min(dff, 1024)

    def epilogue(acc, ex, outs):
        outs[0][...] = (acc * ex[0][...].astype(F32)).astype(BF16)

    blk = pl.BlockSpec((tm, tn), lambda i, j, k: (i, j))
    return _mm(
        "ff_down_bwd", (s // tm, dff // tn, 1), dz2b, pl.BlockSpec((tm, d), lambda i, j, k: (i, 0)),
        w2, pl.BlockSpec((tn, d), lambda i, j, k: (j, 0)), NT, [r_slope], [blk],
        [jax.ShapeDtypeStruct((s, dff), BF16)], [blk], epilogue, None)[0]


def _wgrad(name, a, g, n_sh):
    s, rows = a.shape
    cols = g.shape[1]
    tm, tn, tk = min(rows, 2048), min(cols, 1024), min(s, 2048)
    if tn >= n_sh:
        span = tn // n_sh
        out_spec = pl.BlockSpec((span, tm, n_sh), lambda i, j, k: (j, i, 0))

        def epilogue(acc_ref, ex, outs):
            for sh in range(span):
                outs[0][sh] = acc_ref[:, sh * n_sh:(sh + 1) * n_sh].astype(BF16)
    else:
        per = n_sh // tn
        out_spec = pl.BlockSpec((None, tm, tn), lambda i, j, k: (j // per, i, j % per))

        def epilogue(acc_ref, ex, outs):
            outs[0][...] = acc_ref[...].astype(BF16)

    return _mm(
        name, (rows // tm, cols // tn, s // tk), a, pl.BlockSpec((tk, tm), lambda i, j, k: (k, i)),
        g, pl.BlockSpec((tk, tn), lambda i, j, k: (k, j)), TN, [], [],
        [jax.ShapeDtypeStruct((cols // n_sh, rows, n_sh), BF16)], [out_spec], epilogue,
        (tm, tn), acc_as_ref=True)[0]


def _ff_up_bwd(da, w1_st, dz2, xhat1, rstd1, g1, run_after):
    s, d = dz2.shape
    n_sh = w1_st.shape[2]
    tm, tk = min(s, 512), min(n_sh, 2048)
    per = n_sh // tk
    ch = min(tm, EPILOGUE_ROWS)

    def epilogue(acc_ref, ex, outs):
        dz2_ref, xh_ref, rs_ref, g_ref = ex
        dz_ref, dzb_ref, st_ref = outs
        gv = g_ref[...]
        dg = db = None
        for c in range(tm // ch):
            rows = slice(c * ch, (c + 1) * ch)
            dx1 = ALPHA * dz2_ref[rows, :] + acc_ref[rows, :]
            xhat = xh_ref[rows, :]
            dz = _layer_norm_bwd(dx1, xhat, rs_ref[rows, :], gv)
            dz_ref[rows, :] = dz
            dzb_ref[rows, :] = dz.astype(BF16)
            parts = (jnp.sum(dx1 * xhat, axis=0, keepdims=True), jnp.sum(dx1, axis=0, keepdims=True))
            dg, db = parts if c == 0 else (dg + parts[0], db + parts[1])
        st_ref[...] = _stats_rows([dg, db], d)

    row = pl.BlockSpec((tm, d), lambda i, j, k: (i, 0))
    return _mm(
        "ff_up_bwd", (s // tm, 1, N_CHIPS * per), da, pl.BlockSpec((tm, tk), lambda i, j, k: (i, k)),
        w1_st, pl.BlockSpec((None, d, tk), lambda i, j, k: (k // per, 0, k % per)), NT,
        [dz2, xhat1, rstd1, g1],
        [row, row, pl.BlockSpec((tm, 1), lambda i, j, k: (i, 0)), pl.BlockSpec((1, d), lambda i, j, k: (0, 0))],
        [jax.ShapeDtypeStruct((s, d), F32), jax.ShapeDtypeStruct((s, d), BF16),
         jax.ShapeDtypeStruct((s // tm, 8, d), F32)],
        [row, row, pl.BlockSpec((None, 8, d), lambda i, j, k: (i, 0, 0))], epilogue, (tm, d),
        acc_as_ref=True, run_after=run_after)


def _mix_bwd(dz1b, w_out, h, ya, yp, aw):
    s, d = dz1b.shape
    tm = min(s, 256)
    gblk = 4 * aw // d

    def epilogue(acc, ex, outs):
        sga_ref, sgp_ref, ya_ref, yp_ref = ex
        dya_ref, dyp_ref, dg_ref = outs
        sga, sgp = sga_ref[...].astype(F32), sgp_ref[...].astype(F32)
        dya_ref[...] = (acc * sga).astype(BF16)
        dyp_ref[...] = (acc * sgp).astype(BF16)
        dg_ref[:, :d] = (acc * ya_ref[...].astype(F32) * (sga * (1.0 - sga))).astype(BF16)
        dg_ref[:, d:] = (acc * yp_ref[...].astype(F32) * (sgp * (1.0 - sgp))).astype(BF16)

    row = pl.BlockSpec((tm, d), lambda i, j, k: (i, 0))
    return _mm(
        "mix_bwd", (s // tm, 1, 1), dz1b, row, w_out, pl.BlockSpec((d, d), lambda i, j, k: (0, 0)), NT,
        [h, h, ya, yp],
        [pl.BlockSpec((tm, d), lambda i, j, k: (i, gblk)),
         pl.BlockSpec((tm, d), lambda i, j, k: (i, gblk + 1)), row, row],
        [jax.ShapeDtypeStruct((s, d), BF16), jax.ShapeDtypeStruct((s, d), BF16),
         jax.ShapeDtypeStruct((s, 4 * d), BF16)],
        [row, row, pl.BlockSpec((tm, 2 * d), lambda i, j, k: (i, 1))], epilogue, None)


def _branch_in_bwd(name, tm, dyb, wb_st, epilogue, extras, extra_specs, out_shape, out_specs,
                   scratch=()):
    s, d = dyb.shape
    aw = wb_st.shape[1]
    wb_t = wb_st.transpose(0, 2, 1).reshape(d, aw)
    return _mm(
        name, (s // tm, 1, 1), dyb, pl.BlockSpec((tm, d), lambda i, j, k: (i, 0)),
        wb_t, pl.BlockSpec((d, aw), lambda i, j, k: (0, 0)), NN,
        extras, extra_specs, out_shape, out_specs, epilogue, None, scratch=scratch)


def _attn_out_bwd(dya, wba_st, o_attn):
    s, aw = o_attn.shape
    tm = min(s, 512)

    def epilogue(acc_ref, ex, outs, sdo, sdl):
        do_ref, dl_ref, do4_ref, do16_ref, dl4_ref, dl16_ref = outs
        deltas = []
        for hh in range(aw // HEAD_DIM):
            sl = slice(hh * HEAD_DIM, (hh + 1) * HEAD_DIM)
            do = acc_ref[:, sl]
            deltas.append(jnp.sum(do * ex[0][:, sl].astype(F32), axis=-1, keepdims=True))
            sdo[hh] = do
            do_ref[:, sl] = do.astype(BF16)
        packed = _pack_heads(deltas)
        sdl[0] = packed
        dl_ref[...] = packed
        _to_views(sdo, (do4_ref, do16_ref), BF16)
        _to_views(sdl, (dl4_ref, dl16_ref), F32)

    def specs(width):
        return ([pl.BlockSpec((tm, width), lambda i, j, k: (i, 0))]
                + [pl.BlockSpec((tm // dil, dil * width), lambda i, j, k: (i, 0)) for dil in DILATIONS[1:]])

    def shapes(width, dtype):
        return ([jax.ShapeDtypeStruct((s, width), dtype)]
                + [jax.ShapeDtypeStruct(_view_shape(s, width, dil), dtype) for dil in DILATIONS[1:]])

    do_specs, dl_specs = specs(aw), specs(HEAD_DIM)
    do_shapes, dl_shapes = shapes(aw, BF16), shapes(HEAD_DIM, F32)
    return _branch_in_bwd(
        "attn_out_bwd", tm, dya, wba_st, epilogue, [o_attn], [do_specs[0]],
        [do_shapes[0], dl_shapes[0], *do_shapes[1:], *dl_shapes[1:]],
        [do_specs[0], dl_specs[0], *do_specs[1:], *dl_specs[1:]],
        scratch=[_heads_scratch(tm, aw), _heads_scratch(tm, HEAD_DIM)])


def _pool_out_bwd(dyp, wbp_st, pw, scale):
    s, pw_ = pw.shape
    tm = min(s, 1024)

    def epilogue(acc_ref, ex, outs):
        pw_ref, sc_ref = ex
        dpw_ref, st_ref = outs
        acc = acc_ref[...]
        dpw_ref[...] = (acc * sc_ref[...]).astype(BF16)
        st_ref[...] = _stats_rows([jnp.sum(acc * pw_ref[...].astype(F32), axis=0, keepdims=True)], pw_)

    row = pl.BlockSpec((tm, pw_), lambda i, j, k: (i, 0))
    return _branch_in_bwd(
        "pool_out_bwd", tm, dyp, wbp_st, epilogue, [pw, scale],
        [row, pl.BlockSpec((1, pw_), lambda i, j, k: (0, 0))],
        [jax.ShapeDtypeStruct((s, pw_), BF16), jax.ShapeDtypeStruct((s // tm, 8, pw_), F32)],
        [row, pl.BlockSpec((None, 8, pw_), lambda i, j, k: (i, 0, 0))])


def _pool_bwd(dpw, p, wp):
    s, pw_ = p.shape
    ng = len(POOL_WINDOWS)
    pgw = pw_ // ng
    tm = min(s, 512)
    hb = tm // POOL_HALO
    nblk = s // tm

    def body(dpw_ref, nxt_ref, p_ref, wp_ref, dwp_ref, du_ref):
        i = pl.program_id(0)
        nxt = (nxt_ref[...].astype(F32) * jnp.where(i < nblk - 1, 1.0, 0.0)).astype(BF16)
        dpw_all = jnp.concatenate([dpw_ref[...], nxt], axis=0)

        @pl.when(i == 0)
        def _():
            dwp_ref[...] = jnp.zeros_like(dwp_ref)

        dps = []
        for g in range(ng):
            sl = slice(g * pgw, (g + 1) * pgw)
            dwp_ref[g] += _dot(p_ref[:, sl], dpw_ref[:, sl], TN)
            dps.append(_dot(dpw_all[:, sl], wp_ref[g], NT))
        dp = jnp.concatenate(dps, axis=1)
        dpn = dp / _pool_counts(tm, tm + POOL_HALO, pgw, i * tm)
        du_ref[...] = (_window_sums(dpn, -1, pgw)[:tm] - dp[:tm]).astype(BF16)

    row = pl.BlockSpec((tm, pw_), lambda i: (i, 0))
    full = pl.BlockSpec((ng, pgw, pgw), lambda i: (0, 0, 0))
    return pl.pallas_call(
        body, name="pool_bwd", grid=(nblk,),
        in_specs=[row, pl.BlockSpec((POOL_HALO, pw_), lambda i: (jnp.minimum((i + 1) * hb, s // POOL_HALO - 1), 0)),
                  row, full],
        out_specs=[full, row],
        out_shape=[jax.ShapeDtypeStruct((ng, pgw, pgw), F32), jax.ShapeDtypeStruct((s, pw_), BF16)],
        compiler_params=_params(("arbitrary",)),
    )(dpw, dpw, p, wp)


def _attn_bwd(name, q, k, v, do, lse, delta, offs, cw, width):
    m = do.shape[0]
    nh = cw // HEAD_DIM
    rb = min(m, ATTN_STEP_ROWS)
    nsub = rb // SUB_BLOCK
    nstep = m // rb
    single = nstep == 1
    grid = (width // cw, nstep + (not single))
    scale = HEAD_DIM ** -0.5
    last = slice(rb - SUB_BLOCK, rb)

    def body(q_ref, k_ref, v_ref, do_ref, lse_ref, dl_ref, dq_ref, dk_ref, dv_ref,
             kprev, vprev, dk_carry, dv_carry):
        n = pl.program_id(1)

        @pl.when(n == 0)
        def _():
            for ref in (kprev, vprev, dk_carry, dv_carry):
                ref[...] = jnp.zeros_like(ref)

        qi = lax.broadcasted_iota(jnp.int32, (SUB_BLOCK, 2 * SUB_BLOCK), 0)
        kj = lax.broadcasted_iota(jnp.int32, (SUB_BLOCK, 2 * SUB_BLOCK), 1)
        band = (kj >= qi) & (kj <= qi + SUB_BLOCK)
        flush = jnp.where(n == nstep, 4 * SUB_BLOCK, 0)
        valids = [band & (kj >= jnp.where(n == 0, SUB_BLOCK, flush))] + [band & (kj >= flush)] * (nsub - 1)
        heads = [slice(hh * HEAD_DIM, (hh + 1) * HEAD_DIM) for hh in range(nh)]

        def rows(sub):
            return slice(sub * SUB_BLOCK, (sub + 1) * SUB_BLOCK)

        def cat(prev, ref, sub, sl):
            if sub == 0:
                return jnp.concatenate([prev[:, sl], ref[rows(0), sl]], axis=0)
            return ref[(sub - 1) * SUB_BLOCK:(sub + 1) * SUB_BLOCK, sl]

        kcats = [[cat(kprev, k_ref, sub, sl) for sl in heads] for sub in range(nsub)]
        scs = [[_dot(q_ref[rows(sub), sl], kcats[sub][hh], NT) for hh, sl in enumerate(heads)]
               for sub in range(nsub)]
        dps = [[_dot(do_ref[rows(sub), sl], cat(vprev, v_ref, sub, sl), NT) for sl in heads]
               for sub in range(nsub)]
        dqs = []
        stats = [(lse_ref[rows(sub), :], dl_ref[rows(sub), :]) for sub in range(nsub)]
        for hh, sl in enumerate(heads):
            dk2, dv2 = [], []
            for sub in range(nsub):
                lse_h, dl_h = _head_col(stats[sub][0], hh, nh), _head_col(stats[sub][1], hh, nh)
                valid = valids[sub]
                p = jnp.where(valid, jnp.exp(jnp.where(valid, scs[sub][hh] * scale, NEG) - lse_h), 0.0)
                ds = (p * (dps[sub][hh] - dl_h)).astype(BF16)
                dqs.append((sub, sl, (_dot(ds, kcats[sub][hh], NN) * scale).astype(BF16)))
                dk2.append(_dot(ds, q_ref[rows(sub), sl], TN) * scale)
                dv2.append(_dot(p.astype(BF16), do_ref[rows(sub), sl], TN))
            for out_ref, carry, new in ((dk_ref, dk_carry, dk2), (dv_ref, dv_carry, dv2)):
                if not single:
                    out_ref[last, sl] = (carry[last, sl] + new[0][:SUB_BLOCK]).astype(BF16)
                    if nsub > 1:
                        out_ref[:rb - SUB_BLOCK, sl] = carry[:rb - SUB_BLOCK, sl].astype(BF16)
                for sub in range(nsub):
                    val = new[sub][SUB_BLOCK:]
                    if sub + 1 < nsub:
                        val = val + new[sub + 1][:SUB_BLOCK]
                    if single:
                        out_ref[rows(sub), sl] = val.astype(BF16)
                    else:
                        carry[rows(sub), sl] = val

        @pl.when(n < nstep)
        def _():
            for sub, sl, val in dqs:
                dq_ref[rows(sub), sl] = val

        kprev[...] = k_ref[last, :]
        vprev[...] = v_ref[last, :]

    def cur(off):
        return pl.BlockSpec((rb, cw), lambda j, n: (jnp.minimum(n, nstep - 1), j + off))

    lagged = pl.BlockSpec((rb, cw), lambda j, n: (jnp.maximum(n - 1, 0), j))
    stat = pl.BlockSpec((rb, HEAD_DIM), lambda j, n: (jnp.minimum(n, nstep - 1), j))
    out = jax.ShapeDtypeStruct((m, width), BF16)
    return pl.pallas_call(
        body, name=name, grid=grid,
        in_specs=[cur(offs[0]), cur(offs[1]), cur(offs[2]), cur(0), stat, stat],
        out_specs=[cur(0), lagged, lagged], out_shape=[out, out, out],
        scratch_shapes=[pltpu.VMEM((SUB_BLOCK, cw), BF16)] * 2 + [pltpu.VMEM((rb, cw), F32)] * 2,
        compiler_params=_params(("parallel", "arbitrary")),
    )(q, k, v, do, lse, delta)


def _qkvu_grad(d1, d4, d16, du, cos_t, sin_t, dh):
    s, aw = du.shape
    tm = min(s, 512)

    def body(*refs):
        nat, v4, v16 = refs[0:3], refs[3:6], refs[6:9]
        cos_ref, sin_ref, du_ref, _, out_ref, s4, s16 = refs[9:]
        cos, sin = cos_ref[...], sin_ref[...]
        for part in range(3):
            _from_views((v4[part], v16[part]), (s4, s16))
            for hh in range(aw // HEAD_DIM):
                sl = slice(hh * HEAD_DIM, (hh + 1) * HEAD_DIM)
                t = nat[part][:, sl].astype(F32) + s4[hh] + s16[hh]
                if part < 2:
                    t = t * cos - pltpu.roll(t, HEAD_DIM // 2, 1) * sin
                out_ref[:, part * aw + hh * HEAD_DIM:part * aw + (hh + 1) * HEAD_DIM] = t.astype(BF16)
        out_ref[:, 3 * aw:] = du_ref[...]

    row = pl.BlockSpec((tm, aw), lambda i: (i, 0))
    tab = pl.BlockSpec((tm, HEAD_DIM), lambda i: (i, 0))
    return pl.pallas_call(
        body, name="qkvu_grad", grid=(s // tm,),
        in_specs=[row] * 3 + [_view_spec(tm, aw, 4)] * 3 + [_view_spec(tm, aw, 16)] * 3
        + [tab, tab, row, ANY_SPEC],
        out_specs=pl.BlockSpec((tm, 4 * aw), lambda i: (i, 0)),
        out_shape=jax.ShapeDtypeStruct(dh.shape, BF16), input_output_aliases={12: 0},
        scratch_shapes=[_heads_scratch(tm, aw)] * 2,
        compiler_params=_params(("parallel",)),
    )(*d1, *d4, *d16, cos_t, sin_t, du, dh)


def _in_proj_bwd_x(name, dh, w_in_st, base, scale_base, run_after=()):
    s, kdim = dh.shape
    d, n_sh = w_in_st.shape[1], w_in_st.shape[2]
    tm, tk = min(s, 512), min(n_sh, 2048)
    per = n_sh // tk

    ch = min(tm, 2 * EPILOGUE_ROWS)

    def epilogue(acc_ref, ex, outs):
        for c in range(tm // ch):
            rows = slice(c * ch, (c + 1) * ch)
            outs[0][rows, :] = scale_base * ex[0][rows, :] + acc_ref[rows, :]

    row = pl.BlockSpec((tm, d), lambda i, j, k: (i, 0))
    return _mm(
        name, (s // tm, 1, kdim // tk), dh, pl.BlockSpec((tm, tk), lambda i, j, k: (i, k)),
        w_in_st, pl.BlockSpec((None, d, tk), lambda i, j, k: (k // per, 0, k % per)), NT,
        [base], [row], [jax.ShapeDtypeStruct((s, d), F32)], [row], epilogue, (tm, d),
        acc_as_ref=True, run_after=run_after)[0]


def _chip_peers():
    x, y, c = lax.axis_index("x"), lax.axis_index("y"), lax.axis_index("c")
    return x, y, c, [(1 - x, y), (x, 1 - y), (1 - x, 1 - y)]


GATHER, GATHER_HALF, SCATTER, SIBLING = "gather", "gather_half", "scatter", "sibling"


def _exchange_peers(mode):
    x, y, c, chips = _chip_peers()
    if mode == SIBLING:
        return x, y, c, [(x, y, 1 - c)]
    return x, y, c, [(px, py, c) for px, py in chips]


def _core_half(ref_or_shape, c):
    rows = (ref_or_shape.shape[0]) // 2
    return pl.ds(c * rows, rows)


def _exchange_descriptor(mode, src, land, send, recv, p, peer, me, arriving):
    pid = 2 * peer[0] + peer[1]
    if mode == GATHER:
        src_ref, dst_ref = src, land.at[pid if arriving else me]
    elif mode == GATHER_HALF:
        rows = _core_half(src, peer[2])
        src_ref, dst_ref = src.at[rows], land.at[pid if arriving else me, rows]
    elif mode == SCATTER:
        src_ref, dst_ref = src.at[pid], land.at[p]
    else:
        src_ref, dst_ref = src, land
    return pltpu.make_async_remote_copy(
        src_ref=src_ref, dst_ref=dst_ref, send_sem=send.at[p], recv_sem=recv.at[p],
        device_id=peer, device_id_type=MESH)


def _exchange_start(name, mode, srcs, land_shapes):
    n = len(srcs)
    lands = [_hbm(lax.empty(shape, src.dtype)) for shape, src in zip(land_shapes, srcs)]

    def body(*refs):
        src_refs, land_refs = refs[:n], refs[n:2 * n]
        sends, recvs = refs[2 * n:3 * n], refs[3 * n:4 * n]
        token = refs[6 * n]
        x, y, c, peers = _exchange_peers(mode)
        me = 2 * x + y
        for w in range(n):
            for p, peer in enumerate(peers):
                _exchange_descriptor(mode, src_refs[w], land_refs[w], sends[w], recvs[w], p, peer,
                                     me, arriving=False).start()
        token[...] = jnp.zeros_like(token)

    sem = pltpu.SemaphoreType.DMA((3,))
    outs = pl.pallas_call(
        body, name=name, in_specs=[HBM_SPEC] * (2 * n),
        out_specs=[SEM_SPEC] * (2 * n) + [HBM_SPEC] * (2 * n) + [pl.BlockSpec(memory_space=pltpu.VMEM)],
        out_shape=[sem] * (2 * n) + [pltpu.HBM(a.shape, a.dtype) for a in (*srcs, *lands)]
        + [jax.ShapeDtypeStruct((8, 128), F32)],
        input_output_aliases={i: 2 * n + i for i in range(2 * n)},
        compiler_params=pltpu.CompilerParams(has_side_effects=DATAFLOW),
    )(*[_hbm(a) for a in srcs], *lands)
    return {"send": outs[:n], "recv": outs[n:2 * n], "src": outs[2 * n:3 * n],
            "land": outs[3 * n:4 * n], "token": outs[4 * n]}


def _exchange_wait(name, mode, started, which, after):
    m = len(which)

    def body(*refs):
        src_refs, land_refs = refs[:m], refs[m:2 * m]
        sends, recvs = refs[2 * m:3 * m], refs[3 * m:4 * m]
        x, y, c, peers = _exchange_peers(mode)
        me = 2 * x + y
        for w in range(m):
            for p, peer in enumerate(peers):
                _exchange_descriptor(mode, src_refs[w], land_refs[w], sends[w], recvs[w], p, peer,
                                     me, arriving=False).wait_send()
                _exchange_descriptor(mode, src_refs[w], land_refs[w], sends[w], recvs[w], p, peer,
                                     me, arriving=True).wait_recv()

    pick = lambda key: [started[key][w] for w in which]
    bufs = pick("src") + pick("land")
    after = list(after) if isinstance(after, (list, tuple)) else [after]
    outs = pl.pallas_call(
        body, name=name,
        in_specs=[HBM_SPEC] * (2 * m) + [SEM_SPEC] * (2 * m) + [ANY_SPEC] * len(after),
        out_specs=[HBM_SPEC] * (2 * m), out_shape=[pltpu.HBM(a.shape, a.dtype) for a in bufs],
        input_output_aliases={i: i for i in range(2 * m)},
        compiler_params=pltpu.CompilerParams(has_side_effects=DATAFLOW),
    )(*bufs, *pick("send"), *pick("recv"), *after)
    return outs[:m], outs[m:]


def _to_bf16(name, a, run_after):
    r, c = a.shape
    tm = min(r, 512)

    def body(a_ref, after_ref, out_ref):
        out_ref[...] = a_ref[...].astype(BF16)

    blk = pl.BlockSpec((tm, c), lambda i: (i, 0))
    return pl.pallas_call(
        body, name=name, grid=(r // tm,), in_specs=[blk, ANY_SPEC], out_specs=blk,
        out_shape=jax.ShapeDtypeStruct((r, c), BF16), compiler_params=_params(("parallel",)),
    )(a, run_after)


def _swap_halves(name, lands):
    n = len(lands)

    def body(*refs):
        bufs = refs[n:2 * n]
        send, recv = refs[2 * n:]
        x, y, c, chips = _chip_peers()
        started = []
        for w in range(n):
            half = bufs[w].shape[1] // 2
            for p, (px, py) in enumerate(chips):
                mine = bufs[w].at[2 * px + py, pl.ds(c * half, half)]
                cp = pltpu.make_async_remote_copy(
                    src_ref=mine, dst_ref=mine, send_sem=send.at[w, p], recv_sem=recv.at[w, p],
                    device_id=(x, y, 1 - c), device_id_type=MESH)
                cp.start()
                started.append(cp)
        for w in range(n):
            half = bufs[w].shape[1] // 2
            for p, (px, py) in enumerate(chips):
                theirs = bufs[w].at[2 * px + py, pl.ds((1 - c) * half, half)]
                pltpu.make_async_remote_copy(
                    src_ref=theirs, dst_ref=theirs, send_sem=send.at[w, p], recv_sem=recv.at[w, p],
                    device_id=(x, y, 1 - c), device_id_type=MESH).wait_recv()
        for cp in started:
            cp.wait_send()

    return pl.pallas_call(
        body, name=name, in_specs=[ANY_SPEC] * n, out_specs=[ANY_SPEC] * n,
        out_shape=[jax.ShapeDtypeStruct(a.shape, a.dtype) for a in lands],
        input_output_aliases={i: i for i in range(n)},
        scratch_shapes=[pltpu.SemaphoreType.DMA((n, 3)), pltpu.SemaphoreType.DMA((n, 3))],
    )(*lands)


def _place_own(name, shard, land, me):
    r, c = shard.shape
    tm = min(r, 512)

    def body(me_ref, shard_ref, land_ref, out_ref):
        out_ref[...] = shard_ref[...]

    return pl.pallas_call(
        body, name=name,
        grid_spec=pltpu.PrefetchScalarGridSpec(
            num_scalar_prefetch=1, grid=(r // tm,),
            in_specs=[pl.BlockSpec((tm, c), lambda i, me_ref: (i, 0)), ANY_SPEC],
            out_specs=pl.BlockSpec((None, tm, c), lambda i, me_ref: (me_ref[0], i, 0))),
        out_shape=jax.ShapeDtypeStruct(land.shape, land.dtype), input_output_aliases={2: 0},
        compiler_params=_params(("arbitrary",)),
    )(me, shard, land)


def _sum_slabs(name, grads, land, me):
    _, r, c = grads.shape
    tm = min(r, 256)

    def body(me_ref, own_ref, land_ref, out_ref):
        acc = own_ref[...].astype(F32)
        for p in range(3):
            acc = acc + land_ref[p].astype(F32)
        out_ref[...] = acc

    return pl.pallas_call(
        body, name=name,
        grid_spec=pltpu.PrefetchScalarGridSpec(
            num_scalar_prefetch=1, grid=(r // tm,),
            in_specs=[pl.BlockSpec((None, tm, c), lambda i, me_ref: (me_ref[0], i, 0)),
                      pl.BlockSpec((3, tm, c), lambda i, me_ref: (0, i, 0))],
            out_specs=pl.BlockSpec((tm, c), lambda i, me_ref: (i, 0))),
        out_shape=jax.ShapeDtypeStruct((r, c), F32), compiler_params=_params(("parallel",)),
    )(me, grads, land)


def _allreduce_stats(stats, run_after):
    n = len(stats)

    def body(*refs):
        ins, outs = refs[:n], refs[n + 1:2 * n + 1]
        mine, gath = refs[2 * n + 1:3 * n + 1], refs[3 * n + 1:4 * n + 1]
        send, recv = refs[4 * n + 1:]
        x, y, c = lax.axis_index("x"), lax.axis_index("y"), lax.axis_index("c")
        me = 4 * x + 2 * y + c
        flips = [(bx, by, bc) for bx in (0, 1) for by in (0, 1) for bc in (0, 1)][1:]

        def peer(f):
            return (x + f[0] * (1 - 2 * x), y + f[1] * (1 - 2 * y), c + f[2] * (1 - 2 * c))

        copies = []
        for t in range(n):
            tot = ins[t][0]
            for b in range(1, ins[t].shape[0]):
                tot = tot + ins[t][b]
            mine[t][...] = tot
            gath[t][me] = tot
            for k, f in enumerate(flips):
                cp = pltpu.make_async_remote_copy(
                    src_ref=mine[t], dst_ref=gath[t].at[me], send_sem=send.at[t, k],
                    recv_sem=recv.at[t, k], device_id=peer(f), device_id_type=MESH)
                cp.start()
                copies.append(cp)
        for t in range(n):
            for k, f in enumerate(flips):
                px, py, pc = peer(f)
                pltpu.make_async_remote_copy(
                    src_ref=mine[t], dst_ref=gath[t].at[4 * px + 2 * py + pc], send_sem=send.at[t, k],
                    recv_sem=recv.at[t, k], device_id=(px, py, pc), device_id_type=MESH).wait_recv()
        for cp in copies:
            cp.wait_send()
        for t in range(n):
            tot = gath[t][0]
            for dev in range(1, 8):
                tot = tot + gath[t][dev]
            outs[t][...] = tot

    vm = pl.BlockSpec(memory_space=pltpu.VMEM)
    return pl.pallas_call(
        body, name="allreduce_stats", in_specs=[vm] * n + [ANY_SPEC], out_specs=[vm] * n,
        out_shape=[jax.ShapeDtypeStruct(s.shape[1:], F32) for s in stats],
        scratch_shapes=[pltpu.VMEM(s.shape[1:], F32) for s in stats]
        + [pltpu.VMEM((8, *s.shape[1:]), F32) for s in stats]
        + [pltpu.SemaphoreType.DMA((n, 7)), pltpu.SemaphoreType.DMA((n, 7))],
    )(*stats, run_after)


def _adamw(name, w, m, v, g_parts):
    r, c = w.shape
    tm = min(r, 256)
    n_g = len(g_parts)

    def body(*refs):
        w_ref, m_ref, v_ref = refs[:3]
        g_refs = refs[3:3 + n_g]
        g_out, d_out, m_out, v_out = refs[3 + n_g:]
        g = g_refs[0][...]
        for gr in g_refs[1:]:
            g = g + gr[...]
        m_new = ADAM_B1 * m_ref[...] + (1.0 - ADAM_B1) * g
        v_new = ADAM_B2 * v_ref[...] + (1.0 - ADAM_B2) * (g * g)
        m_hat = m_new / (1.0 - ADAM_B1 ** ADAM_STEP)
        v_hat = v_new / (1.0 - ADAM_B2 ** ADAM_STEP)
        g_out[...] = g
        d_out[...] = -ADAM_LR * (m_hat / (jnp.sqrt(v_hat) + ADAM_EPS) + ADAM_WD * w_ref[...])
        m_out[...] = m_new
        v_out[...] = v_new

    blk = pl.BlockSpec((tm, c), lambda i: (i, 0))
    out = jax.ShapeDtypeStruct((r, c), F32)
    return pl.pallas_call(
        body, name=name, grid=(r // tm,), in_specs=[blk] * (3 + n_g), out_specs=[blk] * 4,
        out_shape=[out] * 4, compiler_params=_params(("parallel",)),
    )(w, m, v, *g_parts)


def _rope_tables(positions):
    half = HEAD_DIM // 2
    inv_freq = ROPE_THETA ** (-jnp.arange(half, dtype=F32) / half)
    ang = positions.astype(F32)[0, :, None] * inv_freq
    cos, sin = jnp.cos(ang), jnp.sin(ang)
    return jnp.concatenate([cos, cos], axis=-1), jnp.concatenate([-sin, sin], axis=-1)


def kernel(x, positions, w_in, w_pool, pool_scale, w_branch_attn, w_branch_pool, w_out, ln_mix_g, ln_mix_b, w_ff1, w_ff2, ln_ff_g, ln_ff_b, loss_target, m_w_in, m_w_pool, m_pool_scale, m_w_branch_attn, m_w_branch_pool, m_w_out, m_ln_mix_g, m_ln_mix_b, m_w_ff1, m_w_ff2, m_ln_ff_g, m_ln_ff_b, v_w_in, v_w_pool, v_pool_scale, v_w_branch_attn, v_w_branch_pool, v_w_out, v_ln_mix_g, v_ln_mix_b, v_w_ff1, v_w_ff2, v_ln_ff_g, v_ln_ff_b):
    s, d = x.shape[1], x.shape[2]
    aw = d // 2
    ng = len(POOL_WINDOWS)
    pgw = aw // ng
    x2d, target = x[0], loss_target[0]
    xb = x2d.astype(BF16)
    cos_t, sin_t = _rope_tables(positions)

    big = {"w_in": w_in[0], "w_pool": w_pool[0].reshape(-1, pgw), "w_branch_attn": w_branch_attn[0],
           "w_branch_pool": w_branch_pool[0], "w_out": w_out[0], "w_ff1": w_ff1[0], "w_ff2": w_ff2[0]}
    names = list(big)
    me_chip = (2 * lax.axis_index("x") + lax.axis_index("y")).astype(jnp.int32).reshape(1)
    land_shapes = [(N_CHIPS, *big[k].shape) for k in names]
    gathering_in = _exchange_start("gather_start_in", GATHER_HALF,
                                   [_to_bf16("to_bf16_w_in", big["w_in"], positions)], land_shapes[:1])
    shards = [_to_bf16(f"to_bf16_{k}", big[k], gathering_in["token"]) for k in names[1:]]
    gathering = _exchange_start("gather_start", GATHER, shards, land_shapes[1:])

    def gathered(name, which, after):
        srcs, lands = _exchange_wait(f"gather_wait_{name}", GATHER, gathering, which, after)
        return [_place_own(f"place_own_{names[w + 1]}", srcs[i], lands[i], me_chip)
                for i, w in enumerate(which)]

    rows_sh = pgw // N_CHIPS
    dff = N_CHIPS * big["w_ff2"].shape[0]

    srcs, lands = _exchange_wait("gather_wait_in", GATHER_HALF, gathering_in, [0],
                                 [gathering["token"], xb, cos_t, sin_t])
    w_in_st = _place_own("place_own_w_in", srcs[0], _swap_halves("swap_halves_in", lands)[0], me_chip)
    h, hv4, hv16 = _in_proj(xb, w_in_st, cos_t, sin_t, aw)
    (wp_st,) = gathered("pool", [0], h)
    wp = wp_st.reshape(N_CHIPS, ng, rows_sh, pgw).transpose(1, 0, 2, 3).reshape(ng, pgw, pgw)
    qkv = {1: (h, h, h), 4: (hv4, hv4, hv4), 16: (hv16, hv16, hv16)}
    offs = {dil: (0, dil, 2 * dil) for dil in DILATIONS}
    o_parts, lse_parts = [], []
    for dil in DILATIONS:
        o_p, lse_p = _attn_fwd(f"attn_fwd_d{dil}", *qkv[dil], offs[dil], aw, dil * aw)
        o_parts.append(o_p)
        lse_parts.append(lse_p)
    o_attn, lse, lse4, lse16 = _attn_combine(o_parts[0], lse_parts[0], o_parts[1:], lse_parts[1:])
    p, pw, y = _pool_fwd(h, wp, pool_scale, aw)
    wba_st, wbp_st, w_out_st = gathered("mix", [1, 2, 3], y)
    w_out_full = w_out_st.reshape(d, d)
    ya, yp, merged = _branch_merge(o_attn, y, wba_st, wbp_st, h, aw, d)
    xhat1, rstd1, x1b = _mix_norm(merged, w_out_full, x2d, ln_mix_g, ln_mix_b)
    w1_st, w2_st = gathered("ff", [4, 5], x1b)
    w2_full = w2_st.reshape(dff, d)
    r, r_slope = _ff_up(x1b, w1_st)
    dz2, dz2b, st2 = _ff_down_loss(r, w2_full, xhat1, ln_mix_g, ln_mix_b, ln_ff_g, ln_ff_b, target)

    def scatter_start(name, grads):
        return _exchange_start(f"scatter_start_{name}", SCATTER, grads, [(3, *g.shape[1:]) for g in grads])

    da = _ff_down_bwd(dz2b, w2_full, r_slope)
    g_w2 = _wgrad("wgrad_ff2", r, dz2b, d).reshape(N_CHIPS, dff // N_CHIPS, d)
    g_w1 = _wgrad("wgrad_ff1", x1b, da, dff // N_CHIPS)
    sent_ff = scatter_start("ff", [g_w1, g_w2])
    dz1, dz1b, st1 = _ff_up_bwd(da, w1_st, dz2, xhat1, rstd1, ln_mix_g, [sent_ff["token"]])
    dya, dyp, dh_gates = _mix_bwd(dz1b, w_out_full, h, ya, yp, aw)
    g_wout = _wgrad("wgrad_out", merged, dz1b, d).reshape(N_CHIPS, d // N_CHIPS, d)
    g_wba = _wgrad("wgrad_branch_attn", o_attn, dya, d // N_CHIPS)
    g_wbp = _wgrad("wgrad_branch_pool", y, dyp, d // N_CHIPS)
    do, delta, do4, do16, delta4, delta16 = _attn_out_bwd(dya, wba_st, o_attn)
    dpw, stp = _pool_out_bwd(dyp, wbp_st, pw, pool_scale)
    dwp, du = _pool_bwd(dpw, p, wp)
    g_wp = dwp.reshape(ng, N_CHIPS, rows_sh, pgw).transpose(1, 0, 2, 3).reshape(
        N_CHIPS, ng * rows_sh, pgw).astype(BF16)
    sent_mix = scatter_start("mix", [g_wp, g_wba, g_wbp, g_wout])

    bwd_in = {1: (do, lse, delta), 4: (do4, lse4, delta4), 16: (do16, lse16, delta16)}
    dqkv = {}
    for dil in DILATIONS:
        args = (*qkv[dil], *bwd_in[dil], offs[dil], aw, dil * aw)
        dqkv[dil] = _attn_bwd(f"attn_bwd_d{dil}", *args)
    dh = _qkvu_grad(dqkv[1], dqkv[4], dqkv[16], du, cos_t, sin_t, dh_gates)
    g_win = _wgrad("wgrad_in", xb, dh, d)
    sent_in = scatter_start("in", [g_win])
    grad_x = _in_proj_bwd_x("in_proj_bwd", dh, w_in_st, dz1, ALPHA, [sent_mix["token"], sent_in["token"]])

    moments = {"w_in": (m_w_in, v_w_in), "w_pool": (m_w_pool, v_w_pool),
               "w_branch_attn": (m_w_branch_attn, v_w_branch_attn),
               "w_branch_pool": (m_w_branch_pool, v_w_branch_pool), "w_out": (m_w_out, v_w_out),
               "w_ff1": (m_w_ff1, v_w_ff1), "w_ff2": (m_w_ff2, v_w_ff2)}
    originals = {"w_in": w_in, "w_pool": w_pool, "w_branch_attn": w_branch_attn,
                 "w_branch_pool": w_branch_pool, "w_out": w_out, "w_ff1": w_ff1, "w_ff2": w_ff2}
    res = {}

    def summed(name, sent, keys, after):
        srcs, lands = _exchange_wait(f"scatter_wait_{name}", SCATTER, sent, list(range(len(keys))), after)
        parts = [_sum_slabs(f"sum_slabs_{k}", srcs[i], lands[i], me_chip) for i, k in enumerate(keys)]
        return _exchange_start(f"cores_start_{name}", SIBLING, parts, [a.shape for a in parts])

    def updated(name, swapping, keys, after):
        mine, other = _exchange_wait(f"cores_wait_{name}", SIBLING, swapping, list(range(len(keys))), after)
        for i, k in enumerate(keys):
            mk, vk = (a.reshape(big[k].shape) for a in moments[k])
            outs = _adamw(f"adamw_{k}", big[k], mk, vk, [mine[i], other[i]])
            res[k] = [o.reshape(originals[k].shape) for o in outs]

    groups = {"ff": ["w_ff1", "w_ff2"], "mix": ["w_pool", "w_branch_attn", "w_branch_pool", "w_out"],
              "in": ["w_in"]}
    swap_ff = summed("ff", sent_ff, groups["ff"], grad_x)
    swap_mix = summed("mix", sent_mix, groups["mix"], swap_ff["token"])
    swap_in = summed("in", sent_in, groups["in"], swap_mix["token"])
    updated("ff", swap_ff, groups["ff"], swap_in["token"])
    updated("mix", swap_mix, groups["mix"], res["w_ff2"][0])
    updated("in", swap_in, groups["in"], res["w_out"][0])
    tot2, tot1, totp = _allreduce_stats([st2, st1, stp], res["w_in"][0])

    def pad_d(a):
        return jnp.pad(a, ((0, 0), (0, d - a.shape[1])))

    small = ["ln_mix_g", "ln_mix_b", "ln_ff_g", "ln_ff_b", "pool_scale"]
    small_w = {"ln_mix_g": ln_mix_g, "ln_mix_b": ln_mix_b, "ln_ff_g": ln_ff_g, "ln_ff_b": ln_ff_b,
               "pool_scale": pool_scale}
    small_m = {"ln_mix_g": m_ln_mix_g, "ln_mix_b": m_ln_mix_b, "ln_ff_g": m_ln_ff_g,
               "ln_ff_b": m_ln_ff_b, "pool_scale": m_pool_scale}
    small_v = {"ln_mix_g": v_ln_mix_g, "ln_mix_b": v_ln_mix_b, "ln_ff_g": v_ln_ff_g,
               "ln_ff_b": v_ln_ff_b, "pool_scale": v_pool_scale}
    small_g = [tot1[0:1], tot1[1:2], tot2[0:1], tot2[1:2], pad_d(totp[0:1])]

    def pack(rows):
        return jnp.concatenate([pad_d(a) for a in rows] + [jnp.zeros((8 - len(rows), d), F32)], axis=0)

    outs = _adamw("adamw_small", pack([small_w[k] for k in small]), pack([small_m[k] for k in small]),
                  pack([small_v[k] for k in small]), [pack(small_g)])
    for i, k in enumerate(small):
        res[k] = [o[i:i + 1, :small_w[k].shape[1]] for o in outs]
    loss = tot2[2, 0]

    order = ["w_in", "w_pool", "pool_scale", "w_branch_attn", "w_branch_pool", "w_out", "ln_mix_g",
             "ln_mix_b", "w_ff1", "w_ff2", "ln_ff_g", "ln_ff_b"]
    result = [loss, grad_x[None]]
    for idx in range(4):
        result += [res[k][idx] for k in order]
    return tuple(result)
```

```python
import jax
import jax.numpy as jnp
from jax import lax
from jax.experimental import pallas as pl
from jax.experimental.pallas import tpu as pltpu

F32 = jnp.float32
BF16 = jnp.bfloat16
MESH = pl.DeviceIdType.MESH

HEAD_DIM = 128
SUB_BLOCK = 128
ATTN_STEP_ROWS = 512
ATTN_FWD_STEP_ROWS = 1024
DILATIONS = (1, 4, 16)
POOL_WINDOWS = (2, 4, 8, 16)
POOL_HALO = 16
ROPE_THETA = 10000.0
LN_EPS = 1e-5
ALPHA = 2.0 ** 0.25
ADAM_LR, ADAM_B1, ADAM_B2, ADAM_EPS, ADAM_WD, ADAM_STEP = 0.001, 0.9, 0.999, 1e-08, 0.01, 10
NEG = -1e30
N_CHIPS = 4
VMEM_LIMIT = 62 * 1024 * 1024
EPILOGUE_ROWS = 128

def _params(sem=None, vmem=VMEM_LIMIT):
    kw = {"vmem_limit_bytes": vmem}
    if sem is not None:
        kw["dimension_semantics"] = sem
    return pltpu.CompilerParams(**kw)


def _dot(a, b, contract):
    return lax.dot_general(a, b, (contract, ((), ())), preferred_element_type=F32)


ANY_SPEC = pl.BlockSpec(memory_space=pl.ANY)
HBM_SPEC = pl.BlockSpec(memory_space=pltpu.HBM)
SEM_SPEC = pl.BlockSpec(memory_space=pltpu.SEMAPHORE)
DATAFLOW = pltpu.SideEffectType.DATAFLOW_SIDE_EFFECTING


def _hbm(a):
    return pltpu.with_memory_space_constraint(a, pltpu.HBM)


NN = ((1,), (0,))
NT = ((1,), (1,))
TN = ((0,), (0,))


def _mm(name, grid, a, a_spec, b, b_spec, contract, extras, extra_specs, out_shape, out_specs,
        epilogue, acc_shape, acc_as_ref=False, run_after=(), scratch=(),
        semantics=("parallel", "parallel", "arbitrary")):
    nk = grid[2]
    n_ex = len(extras)
    n_in = 2 + n_ex + len(run_after)
    n_out = len(out_shape)
    n_scr = len(scratch)

    def body(*refs):
        a_ref, b_ref = refs[0], refs[1]
        ex = refs[2:2 + n_ex]
        outs = refs[n_in:n_in + n_out]
        scr = refs[n_in + n_out:n_in + n_out + n_scr]
        if nk == 1:
            epilogue(_dot(a_ref[...], b_ref[...], contract), ex, outs, *scr)
        else:
            acc = refs[n_in + n_out + n_scr]
            k = pl.program_id(2)

            @pl.when(k == 0)
            def _():
                acc[...] = jnp.zeros_like(acc)

            acc[...] += _dot(a_ref[...], b_ref[...], contract)

            @pl.when(k == nk - 1)
            def _():
                epilogue(acc if acc_as_ref else acc[...], ex, outs, *scr)

    acc_scratch = [pltpu.VMEM(acc_shape, F32)] if nk > 1 else []
    return pl.pallas_call(
        body, name=name, grid=grid,
        in_specs=[a_spec, b_spec, *extra_specs, *[ANY_SPEC] * len(run_after)],
        out_specs=out_specs, out_shape=out_shape, scratch_shapes=[*scratch, *acc_scratch],
        compiler_params=_params(semantics),
    )(a, b, *extras, *run_after)


def _stats_rows(rows, width):
    idx = lax.broadcasted_iota(jnp.int32, (8, width), 0)
    out = jnp.zeros((8, width), F32)
    for r, v in enumerate(rows):
        out = jnp.where(idx == r, jnp.broadcast_to(v, (8, width)), out)
    return out


def _layer_norm_fwd(z):
    mu = jnp.mean(z, axis=-1, keepdims=True)
    zc = z - mu
    var = jnp.mean(zc * zc, axis=-1, keepdims=True)
    rstd = lax.rsqrt(var + LN_EPS)
    return zc * rstd, rstd


def _layer_norm_bwd(dy, xhat, rstd, g):
    dxh = dy * g
    m1 = jnp.mean(dxh, axis=-1, keepdims=True)
    m2 = jnp.mean(dxh * xhat, axis=-1, keepdims=True)
    return rstd * (dxh - m1 - xhat * m2)


def _heads_scratch(rows, width):
    return pltpu.VMEM((width // HEAD_DIM, rows, HEAD_DIM), F32)


def _to_views(src_ref, view_refs, dtype, heads=None):
    nh, rows, _ = src_ref.shape
    width = nh * HEAD_DIM
    for dil, view_ref in zip(DILATIONS[1:], view_refs):
        for r in range(dil):
            for hh in (range(nh) if heads is None else heads):
                c0 = r * width + hh * HEAD_DIM
                view_ref[:, c0:c0 + HEAD_DIM] = (
                    src_ref[hh, pl.ds(r, rows // dil, stride=dil), :].astype(dtype))


def _from_views(view_refs, dst_refs):
    nh, rows, _ = dst_refs[0].shape
    width = nh * HEAD_DIM
    for dil, view_ref, dst_ref in zip(DILATIONS[1:], view_refs, dst_refs):
        for r in range(dil):
            for hh in range(nh):
                c0 = r * width + hh * HEAD_DIM
                dst_ref[hh, pl.ds(r, rows // dil, stride=dil), :] = (
                    view_ref[:, c0:c0 + HEAD_DIM].astype(F32))


def _view_shape(rows, width, dil, parts=1):
    return (rows // dil, parts * dil * width)


def _in_proj(xb, w_in_st, cos_t, sin_t, aw):
    s, d = xb.shape
    n_sh = w_in_st.shape[2]
    tm, tn = min(s, 1024), aw
    per = n_sh // tn
    grid = (s // tm, (N_CHIPS * n_sh) // tn, 1)

    def epilogue(acc, ex, outs, scr):
        cos_ref, sin_ref = ex
        h_ref, v4_ref, v16_ref = outs
        seg = pl.program_id(1)

        heads = [slice(hh * HEAD_DIM, (hh + 1) * HEAD_DIM) for hh in range(tn // HEAD_DIM)]

        @pl.when(seg < 2)
        def _():
            cos, sin = cos_ref[...], sin_ref[...]
            for hh, sl in enumerate(heads):
                t = acc[:, sl]
                scr[hh] = t * cos + pltpu.roll(t, HEAD_DIM // 2, 1) * sin

        @pl.when(seg == 2)
        def _():
            for hh, sl in enumerate(heads):
                scr[hh] = acc[:, sl]

        @pl.when(seg < 3)
        def _():
            for hh, sl in enumerate(heads):
                h_ref[:, sl] = scr[hh].astype(BF16)
            _to_views(scr, (v4_ref, v16_ref), BF16)

        @pl.when(seg == 3)
        def _():
            h_ref[...] = acc.astype(BF16)

        @pl.when(seg >= 4)
        def _():
            h_ref[...] = (0.5 * jnp.tanh(0.5 * acc) + 0.5).astype(BF16)

    def view_spec(dil):
        return pl.BlockSpec((tm // dil, dil * aw), lambda i, j, k: (i, jnp.minimum(j, 2)))

    return _mm(
        "in_proj", grid, xb, pl.BlockSpec((tm, d), lambda i, j, k: (i, 0)),
        w_in_st, pl.BlockSpec((None, d, tn), lambda i, j, k: (j // per, 0, j % per)), NN,
        [cos_t, sin_t], [pl.BlockSpec((tm, HEAD_DIM), lambda i, j, k: (i, 0))] * 2,
        [jax.ShapeDtypeStruct((s, N_CHIPS * n_sh), BF16)]
        + [jax.ShapeDtypeStruct(_view_shape(s, aw, dil, 3), BF16) for dil in DILATIONS[1:]],
        [pl.BlockSpec((tm, tn), lambda i, j, k: (i, j))] + [view_spec(dil) for dil in DILATIONS[1:]],
        epilogue, None, scratch=[_heads_scratch(tm, aw)],
        semantics=("parallel", "arbitrary", "arbitrary"))


def _pack_heads(cols):
    rows, rep = cols[0].shape[0], HEAD_DIM // len(cols)
    lane = lax.broadcasted_iota(jnp.int32, (rows, HEAD_DIM), 1)
    out = jnp.zeros((rows, HEAD_DIM), F32)
    for hh, col in enumerate(cols):
        out = jnp.where((lane >= hh * rep) & (lane < (hh + 1) * rep), col, out)
    return out


def _head_col(packed, hh, nh):
    lane = lax.broadcasted_iota(jnp.int32, packed.shape, 1)
    return jnp.sum(jnp.where(lane == hh * (HEAD_DIM // nh), packed, 0.0), axis=-1, keepdims=True)


def _band_masks(block_idx):
    qi = lax.broadcasted_iota(jnp.int32, (SUB_BLOCK, 2 * SUB_BLOCK), 0)
    kj = lax.broadcasted_iota(jnp.int32, (SUB_BLOCK, 2 * SUB_BLOCK), 1)
    first_key = jnp.where(block_idx > 0, 0, SUB_BLOCK)
    return (kj >= qi) & (kj <= qi + SUB_BLOCK) & (kj >= first_key)


def _attn_fwd(name, q, k, v, offs, cw, width):
    m = q.shape[0]
    nh = cw // HEAD_DIM
    rb = min(m, ATTN_FWD_STEP_ROWS)
    nsub = rb // SUB_BLOCK
    grid = (width // cw, m // rb)
    scale = HEAD_DIM ** -0.5

    def body(q_ref, k_ref, v_ref, o_ref, lse_ref, kprev, vprev):
        n = pl.program_id(1)

        @pl.when(n == 0)
        def _():
            kprev[...] = jnp.zeros_like(kprev)
            vprev[...] = jnp.zeros_like(vprev)

        heads = [slice(hh * HEAD_DIM, (hh + 1) * HEAD_DIM) for hh in range(nh)]
        valids = [_band_masks(n)] + [_band_masks(1)] * (nsub - 1)

        def rows(sub):
            return slice(sub * SUB_BLOCK, (sub + 1) * SUB_BLOCK)

        def cat(prev, ref, sub, sl):
            if sub == 0:
                return jnp.concatenate([prev[:, sl], ref[rows(0), sl]], axis=0)
            return ref[(sub - 1) * SUB_BLOCK:(sub + 1) * SUB_BLOCK, sl]

        scs = [[_dot(q_ref[rows(sub), sl], cat(kprev, k_ref, sub, sl), NT) for sl in heads]
               for sub in range(nsub)]
        for sub in range(nsub):
            lses = []
            for hh, sl in enumerate(heads):
                sc = jnp.where(valids[sub], scs[sub][hh] * scale, NEG)
                mx = jnp.max(sc, axis=-1, keepdims=True)
                p = jnp.exp(sc - mx)
                l = jnp.sum(p, axis=-1, keepdims=True)
                o = _dot(p.astype(BF16), cat(vprev, v_ref, sub, sl), NN) / l
                o_ref[rows(sub), sl] = o.astype(BF16)
                lses.append(mx + jnp.log(l))
            lse_ref[rows(sub), :] = _pack_heads(lses)
        kprev[...] = k_ref[rows(nsub - 1), :]
        vprev[...] = v_ref[rows(nsub - 1), :]

    def cur(off):
        return pl.BlockSpec((rb, cw), lambda j, n: (n, j + off))

    return pl.pallas_call(
        body, name=name, grid=grid,
        in_specs=[cur(offs[0]), cur(offs[1]), cur(offs[2])],
        out_specs=[cur(0), pl.BlockSpec((rb, HEAD_DIM), lambda j, n: (n, j))],
        out_shape=[jax.ShapeDtypeStruct((m, width), BF16),
                   jax.ShapeDtypeStruct((m, width // cw * HEAD_DIM), F32)],
        scratch_shapes=[pltpu.VMEM((SUB_BLOCK, cw), BF16)] * 2,
        compiler_params=_params(("parallel", "arbitrary")),
    )(q, k, v)


def _view_spec(tm, aw, dil):
    return pl.BlockSpec((tm // dil, dil * aw), lambda i: (i, 0))


def _attn_combine(o1, l1, o_views, l_views):
    s, aw = o1.shape
    nh = aw // HEAD_DIM
    tm = min(s, 512)

    def body(o1_ref, l1_ref, o4_ref, o16_ref, l4_ref, l16_ref, o_ref, lse_ref, lse4_ref, lse16_ref,
             so4, so16, sl4, sl16, stot):
        _from_views((o4_ref, o16_ref), (so4, so16))
        _from_views((l4_ref, l16_ref), (sl4, sl16))
        a, b, c = l1_ref[...], sl4[0], sl16[0]
        mx = jnp.maximum(jnp.maximum(a, b), c)
        ea, eb, ec = jnp.exp(a - mx), jnp.exp(b - mx), jnp.exp(c - mx)
        tot = ea + eb + ec
        inv = 1.0 / tot
        wa, wb, wc = ea * inv, eb * inv, ec * inv
        lse_tot = mx + jnp.log(tot)
        stot[0] = lse_tot
        lse_ref[...] = lse_tot
        _to_views(stot, (lse4_ref, lse16_ref), F32)
        for hh in range(nh):
            sl = slice(hh * HEAD_DIM, (hh + 1) * HEAD_DIM)
            o = (_head_col(wa, hh, nh) * o1_ref[:, sl].astype(F32) + _head_col(wb, hh, nh) * so4[hh]
                 + _head_col(wc, hh, nh) * so16[hh])
            o_ref[:, sl] = o.astype(BF16)

    row = pl.BlockSpec((tm, aw), lambda i: (i, 0))
    stat = pl.BlockSpec((tm, HEAD_DIM), lambda i: (i, 0))
    views = [_view_spec(tm, aw, dil) for dil in DILATIONS[1:]]
    stat_views = [_view_spec(tm, HEAD_DIM, dil) for dil in DILATIONS[1:]]
    return pl.pallas_call(
        body, name="attn_combine", grid=(s // tm,), in_specs=[row, stat, *views, *stat_views],
        out_specs=[row, stat, *stat_views],
        out_shape=[jax.ShapeDtypeStruct((s, aw), BF16), jax.ShapeDtypeStruct((s, HEAD_DIM), F32)]
        + [jax.ShapeDtypeStruct(_view_shape(s, HEAD_DIM, dil), F32) for dil in DILATIONS[1:]],
        scratch_shapes=[_heads_scratch(tm, aw)] * 2 + [_heads_scratch(tm, HEAD_DIM)] * 3,
        compiler_params=_params(("parallel",)),
    )(o1, l1, *o_views, *l_views)


def _pool_counts(tm, rows, pgw, row0):
    t = lax.broadcasted_iota(jnp.int32, (rows, len(POOL_WINDOWS) * pgw), 0) + row0
    col = lax.broadcasted_iota(jnp.int32, (rows, len(POOL_WINDOWS) * pgw), 1)
    w = jnp.full((rows, len(POOL_WINDOWS) * pgw), POOL_WINDOWS[0], jnp.int32)
    for g in range(1, len(POOL_WINDOWS)):
        w = jnp.where(col >= g * pgw, POOL_WINDOWS[g], w)
    return jnp.minimum(t + 1, w).astype(F32)


def _window_sums(xs, direction, pgw):
    rows = xs.shape[0]
    acc = xs
    out = None
    col = lax.broadcasted_iota(jnp.int32, xs.shape, 1)
    for g, w in enumerate(POOL_WINDOWS):
        sh = w // 2
        acc = acc + pltpu.roll(acc, sh if direction > 0 else rows - sh, 0)
        out = acc if out is None else jnp.where(col >= g * pgw, acc, out)
    return out


def _pool_fwd(h, wp, scale, aw):
    s = h.shape[0]
    pw_ = aw
    pgw = pw_ // len(POOL_WINDOWS)
    tm = min(s, 512)
    hb = tm // POOL_HALO

    def body(u_ref, halo_ref, wp_ref, sc_ref, p_ref, pw_ref, y_ref):
        i = pl.program_id(0)
        u = u_ref[...].astype(F32)
        halo = halo_ref[...].astype(F32) * jnp.where(i > 0, 1.0, 0.0)
        xs = jnp.concatenate([halo, u], axis=0)
        sums = _window_sums(xs, +1, pgw)[POOL_HALO:]
        p = (sums / _pool_counts(tm, tm, pgw, i * tm) - u).astype(BF16)
        p_ref[...] = p
        sc = sc_ref[...]
        for g in range(len(POOL_WINDOWS)):
            sl = slice(g * pgw, (g + 1) * pgw)
            pw = _dot(p[:, sl], wp_ref[g], NN)
            pw_ref[:, sl] = pw.astype(BF16)
            y_ref[:, sl] = (pw * sc[:, sl]).astype(BF16)

    out = jax.ShapeDtypeStruct((s, pw_), BF16)
    row = pl.BlockSpec((tm, pw_), lambda i: (i, 0))
    return pl.pallas_call(
        body, name="pool_fwd", grid=(s // tm,),
        in_specs=[pl.BlockSpec((tm, pw_), lambda i: (i, 3)),
                  pl.BlockSpec((POOL_HALO, pw_), lambda i: (jnp.maximum(i * hb - 1, 0), 3)),
                  pl.BlockSpec(wp.shape, lambda i: (0, 0, 0)),
                  pl.BlockSpec((1, pw_), lambda i: (0, 0))],
        out_specs=[row, row, row], out_shape=[out, out, out],
        compiler_params=_params(("parallel",)),
    )(h, h, wp, scale)


def _branch_merge(o_attn, y, wba_st, wbp_st, h, aw, d):
    s = o_attn.shape[0]
    tn = wba_st.shape[2]
    tm = min(s, 1024)
    ga0 = 4 * aw // tn
    gp0 = (4 * aw + d) // tn

    def body(o_ref, y_ref, wa_ref, wp_ref, sga_ref, sgp_ref, ya_ref, yp_ref, mg_ref):
        ya = _dot(o_ref[...], wa_ref[...], NN)
        yp = _dot(y_ref[...], wp_ref[...], NN)
        ya_ref[...] = ya.astype(BF16)
        yp_ref[...] = yp.astype(BF16)
        mg_ref[...] = (sga_ref[...].astype(F32) * ya + sgp_ref[...].astype(F32) * yp).astype(BF16)

    out = jax.ShapeDtypeStruct((s, d), BF16)
    blk = pl.BlockSpec((tm, tn), lambda i, j: (i, j))
    return pl.pallas_call(
        body, name="branch_merge", grid=(s // tm, N_CHIPS),
        in_specs=[pl.BlockSpec((tm, aw), lambda i, j: (i, 0)),
                  pl.BlockSpec((tm, aw), lambda i, j: (i, 0)),
                  pl.BlockSpec((None, aw, tn), lambda i, j: (j, 0, 0)),
                  pl.BlockSpec((None, aw, tn), lambda i, j: (j, 0, 0)),
                  pl.BlockSpec((tm, tn), lambda i, j: (i, j + ga0)),
                  pl.BlockSpec((tm, tn), lambda i, j: (i, j + gp0))],
        out_specs=[blk, blk, blk], out_shape=[out, out, out],
        compiler_params=_params(("parallel", "parallel")),
    )(o_attn, y, wba_st, wbp_st, h, h)


def _mix_norm(merged, w_out, x, g1, b1):
    s, d = x.shape
    tm = min(s, 256)

    def epilogue(acc, ex, outs):
        x_ref, g_ref, b_ref = ex
        xh_ref, rs_ref, xb_ref = outs
        xhat, rstd = _layer_norm_fwd(ALPHA * x_ref[...] + acc)
        xh_ref[...] = xhat
        rs_ref[...] = rstd
        xb_ref[...] = (xhat * g_ref[...] + b_ref[...]).astype(BF16)

    row = pl.BlockSpec((tm, d), lambda i, j, k: (i, 0))
    vec = pl.BlockSpec((1, d), lambda i, j, k: (0, 0))
    return _mm(
        "mix_norm", (s // tm, 1, 1), merged, row, w_out, pl.BlockSpec((d, d), lambda i, j, k: (0, 0)),
        NN, [x, g1, b1], [row, vec, vec],
        [jax.ShapeDtypeStruct((s, d), F32), jax.ShapeDtypeStruct((s, 1), F32),
         jax.ShapeDtypeStruct((s, d), BF16)],
        [row, pl.BlockSpec((tm, 1), lambda i, j, k: (i, 0)), row], epilogue, None)


def _ff_up(x1b, w1_st):
    s, d = x1b.shape
    n_sh = w1_st.shape[2]
    tm, tn = min(s, 1024), min(n_sh, 1024)
    per = n_sh // tn

    def epilogue(acc, ex, outs):
        r = jnp.maximum(acc, 0.0)
        outs[0][...] = (r * r).astype(BF16)
        outs[1][...] = (2.0 * r).astype(BF16)

    blk = pl.BlockSpec((tm, tn), lambda i, j, k: (i, j))
    out = jax.ShapeDtypeStruct((s, N_CHIPS * n_sh), BF16)
    return _mm(
        "ff_up", (s // tm, N_CHIPS * per, 1), x1b, pl.BlockSpec((tm, d), lambda i, j, k: (i, 0)),
        w1_st, pl.BlockSpec((None, d, tn), lambda i, j, k: (j // per, 0, j % per)), NN, [], [],
        [out, out], [blk, blk], epilogue, None)


def _ff_down_loss(r, w2, xhat1, g1, b1, g2, b2, target):
    s, d = xhat1.shape
    dff = r.shape[1]
    tm, tk = min(s, 512), min(dff, 2048)
    ch = min(tm, EPILOGUE_ROWS)

    def epilogue(acc_ref, ex, outs):
        xh1_ref, g1_ref, b1_ref, g2_ref, b2_ref, t_ref = ex
        dz_ref, dzb_ref, st_ref = outs
        g1v, b1v, g2v, b2v = g1_ref[...], b1_ref[...], g2_ref[...], b2_ref[...]
        dg = db = loss = None
        for c in range(tm // ch):
            rows = slice(c * ch, (c + 1) * ch)
            x1 = xh1_ref[rows, :] * g1v + b1v
            xhat2, rstd2 = _layer_norm_fwd(ALPHA * x1 + acc_ref[rows, :])
            err = xhat2 * g2v + b2v - t_ref[rows, :]
            dy = err * (1.0 / d)
            dz = _layer_norm_bwd(dy, xhat2, rstd2, g2v)
            dz_ref[rows, :] = dz
            dzb_ref[rows, :] = dz.astype(BF16)
            parts = (jnp.sum(dy * xhat2, axis=0, keepdims=True), jnp.sum(dy, axis=0, keepdims=True),
                     jnp.sum(jnp.sum(err * err, axis=-1, keepdims=True), axis=0, keepdims=True))
            dg, db, loss = parts if c == 0 else (dg + parts[0], db + parts[1], loss + parts[2])
        st_ref[...] = _stats_rows([dg, db, jnp.broadcast_to((0.5 / d) * loss, (1, d))], d)

    row = pl.BlockSpec((tm, d), lambda i, j, k: (i, 0))
    vec = pl.BlockSpec((1, d), lambda i, j, k: (0, 0))
    return _mm(
        "ff_down_loss", (s // tm, 1, dff // tk), r, pl.BlockSpec((tm, tk), lambda i, j, k: (i, k)),
        w2, pl.BlockSpec((tk, d), lambda i, j, k: (k, 0)), NN,
        [xhat1, g1, b1, g2, b2, target], [row, vec, vec, vec, vec, row],
        [jax.ShapeDtypeStruct((s, d), F32), jax.ShapeDtypeStruct((s, d), BF16),
         jax.ShapeDtypeStruct((s // tm, 8, d), F32)],
        [row, row, pl.BlockSpec((None, 8, d), lambda i, j, k: (i, 0, 0))], epilogue, (tm, d),
        acc_as_ref=True)


def _ff_down_bwd(dz2b, w2, r_slope):
    s, d = dz2b.shape
    dff = r_slope.shape[1]
    tm, tn = min(s, 1024), min(dff, 1024)

    def epilogue(acc, ex, outs):
        outs[0][...] = (acc * ex[0][...].astype(F32)).astype(BF16)

    blk = pl.BlockSpec((tm, tn), lambda i, j, k: (i, j))
    return _mm(
        "ff_down_bwd", (s // tm, dff // tn, 1), dz2b, pl.BlockSpec((tm, d), lambda i, j, k: (i, 0)),
        w2, pl.BlockSpec((tn, d), lambda i, j, k: (j, 0)), NT, [r_slope], [blk],
        [jax.ShapeDtypeStruct((s, dff), BF16)], [blk], epilogue, None)[0]


def _wgrad(name, a, g, n_sh):
    s, rows = a.shape
    cols = g.shape[1]
    tm, tn, tk = min(rows, 2048), min(cols, 1024), min(s, 2048)
    if tn >= n_sh:
        span = tn // n_sh
        out_spec = pl.BlockSpec((span, tm, n_sh), lambda i, j, k: (j, i, 0))

        def epilogue(acc_ref, ex, outs):
            for sh in range(span):
                outs[0][sh] = acc_ref[:, sh * n_sh:(sh + 1) * n_sh].astype(BF16)
    else:
        per = n_sh // tn
        out_spec = pl.BlockSpec((None, tm, tn), lambda i, j, k: (j // per, i, j % per))

        def epilogue(acc_ref, ex, outs):
            outs[0][...] = acc_ref[...].astype(BF16)

    return _mm(
        name, (rows // tm, cols // tn, s // tk), a, pl.BlockSpec((tk, tm), lambda i, j, k: (k, i)),
        g, pl.BlockSpec((tk, tn), lambda i, j, k: (k, j)), TN, [], [],
        [jax.ShapeDtypeStruct((cols // n_sh, rows, n_sh), BF16)], [out_spec], epilogue,
        (tm, tn), acc_as_ref=True)[0]


def _ff_up_bwd(da, w1_st, dz2, xhat1, rstd1, g1, run_after):
    s, d = dz2.shape
    n_sh = w1_st.shape[2]
    tm, tk = min(s, 512), min(n_sh, 2048)
    per = n_sh // tk
    ch = min(tm, EPILOGUE_ROWS)

    def epilogue(acc_ref, ex, outs):
        dz2_ref, xh_ref, rs_ref, g_ref = ex
        dz_ref, dzb_ref, st_ref = outs
        gv = g_ref[...]
        dg = db = None
        for c in range(tm // ch):
            rows = slice(c * ch, (c + 1) * ch)
            dx1 = ALPHA * dz2_ref[rows, :] + acc_ref[rows, :]
            xhat = xh_ref[rows, :]
            dz = _layer_norm_bwd(dx1, xhat, rs_ref[rows, :], gv)
            dz_ref[rows, :] = dz
            dzb_ref[rows, :] = dz.astype(BF16)
            parts = (jnp.sum(dx1 * xhat, axis=0, keepdims=True), jnp.sum(dx1, axis=0, keepdims=True))
            dg, db = parts if c == 0 else (dg + parts[0], db + parts[1])
        st_ref[...] = _stats_rows([dg, db], d)

    row = pl.BlockSpec((tm, d), lambda i, j, k: (i, 0))
    return _mm(
        "ff_up_bwd", (s // tm, 1, N_CHIPS * per), da, pl.BlockSpec((tm, tk), lambda i, j, k: (i, k)),
        w1_st, pl.BlockSpec((None, d, tk), lambda i, j, k: (k // per, 0, k % per)), NT,
        [dz2, xhat1, rstd1, g1],
        [row, row, pl.BlockSpec((tm, 1), lambda i, j, k: (i, 0)), pl.BlockSpec((1, d), lambda i, j, k: (0, 0))],
        [jax.ShapeDtypeStruct((s, d), F32), jax.ShapeDtypeStruct((s, d), BF16),
         jax.ShapeDtypeStruct((s // tm, 8, d), F32)],
        [row, row, pl.BlockSpec((None, 8, d), lambda i, j, k: (i, 0, 0))], epilogue, (tm, d),
        acc_as_ref=True, run_after=run_after)


def _mix_bwd(dz1b, w_out, h, ya, yp, aw):
    s, d = dz1b.shape
    tm = min(s, 256)
    gblk = 4 * aw // d

    def epilogue(acc, ex, outs):
        sga_ref, sgp_ref, ya_ref, yp_ref = ex
        dya_ref, dyp_ref, dg_ref = outs
        sga, sgp = sga_ref[...].astype(F32), sgp_ref[...].astype(F32)
        dya_ref[...] = (acc * sga).astype(BF16)
        dyp_ref[...] = (acc * sgp).astype(BF16)
        dg_ref[:, :d] = (acc * ya_ref[...].astype(F32) * (sga * (1.0 - sga))).astype(BF16)
        dg_ref[:, d:] = (acc * yp_ref[...].astype(F32) * (sgp * (1.0 - sgp))).astype(BF16)

    row = pl.BlockSpec((tm, d), lambda i, j, k: (i, 0))
    return _mm(
        "mix_bwd", (s // tm, 1, 1), dz1b, row, w_out, pl.BlockSpec((d, d), lambda i, j, k: (0, 0)), NT,
        [h, h, ya, yp],
        [pl.BlockSpec((tm, d), lambda i, j, k: (i, gblk)),
         pl.BlockSpec((tm, d), lambda i, j, k: (i, gblk + 1)), row, row],
        [jax.ShapeDtypeStruct((s, d), BF16), jax.ShapeDtypeStruct((s, d), BF16),
         jax.ShapeDtypeStruct((s, 4 * d), BF16)],
        [row, row, pl.BlockSpec((tm, 2 * d), lambda i, j, k: (i, 1))], epilogue, None)


def _branch_in_bwd(name, tm, dyb, wb_st, epilogue, extras, extra_specs, out_shape, out_specs,
                   scratch=()):
    s, d = dyb.shape
    aw = wb_st.shape[1]
    wb_t = wb_st.transpose(0, 2, 1).reshape(d, aw)
    return _mm(
        name, (s // tm, 1, 1), dyb, pl.BlockSpec((tm, d), lambda i, j, k: (i, 0)),
        wb_t, pl.BlockSpec((d, aw), lambda i, j, k: (0, 0)), NN,
        extras, extra_specs, out_shape, out_specs, epilogue, None, scratch=scratch)


def _attn_out_bwd(dya, wba_st, o_attn):
    s, aw = o_attn.shape
    tm = min(s, 512)

    def epilogue(acc_ref, ex, outs, sdo, sdl):
        do_ref, dl_ref, do4_ref, do16_ref, dl4_ref, dl16_ref = outs
        deltas = []
        for hh in range(aw // HEAD_DIM):
            sl = slice(hh * HEAD_DIM, (hh + 1) * HEAD_DIM)
            do = acc_ref[:, sl]
            deltas.append(jnp.sum(do * ex[0][:, sl].astype(F32), axis=-1, keepdims=True))
            sdo[hh] = do
            do_ref[:, sl] = do.astype(BF16)
        packed = _pack_heads(deltas)
        sdl[0] = packed
        dl_ref[...] = packed
        _to_views(sdo, (do4_ref, do16_ref), BF16)
        _to_views(sdl, (dl4_ref, dl16_ref), F32)

    def specs(width):
        return ([pl.BlockSpec((tm, width), lambda i, j, k: (i, 0))]
                + [pl.BlockSpec((tm // dil, dil * width), lambda i, j, k: (i, 0)) for dil in DILATIONS[1:]])

    def shapes(width, dtype):
        return ([jax.ShapeDtypeStruct((s, width), dtype)]
                + [jax.ShapeDtypeStruct(_view_shape(s, width, dil), dtype) for dil in DILATIONS[1:]])

    do_specs, dl_specs = specs(aw), specs(HEAD_DIM)
    do_shapes, dl_shapes = shapes(aw, BF16), shapes(HEAD_DIM, F32)
    return _branch_in_bwd(
        "attn_out_bwd", tm, dya, wba_st, epilogue, [o_attn], [do_specs[0]],
        [do_shapes[0], dl_shapes[0], *do_shapes[1:], *dl_shapes[1:]],
        [do_specs[0], dl_specs[0], *do_specs[1:], *dl_specs[1:]],
        scratch=[_heads_scratch(tm, aw), _heads_scratch(tm, HEAD_DIM)])


def _pool_out_bwd(dyp, wbp_st, pw, scale):
    s, pw_ = pw.shape
    tm = min(s, 1024)

    def epilogue(acc_ref, ex, outs):
        pw_ref, sc_ref = ex
        dpw_ref, st_ref = outs
        acc = acc_ref[...]
        dpw_ref[...] = (acc * sc_ref[...]).astype(BF16)
        st_ref[...] = _stats_rows([jnp.sum(acc * pw_ref[...].astype(F32), axis=0, keepdims=True)], pw_)

    row = pl.BlockSpec((tm, pw_), lambda i, j, k: (i, 0))
    return _branch_in_bwd(
        "pool_out_bwd", tm, dyp, wbp_st, epilogue, [pw, scale],
        [row, pl.BlockSpec((1, pw_), lambda i, j, k: (0, 0))],
        [jax.ShapeDtypeStruct((s, pw_), BF16), jax.ShapeDtypeStruct((s // tm, 8, pw_), F32)],
        [row, pl.BlockSpec((None, 8, pw_), lambda i, j, k: (i, 0, 0))])


def _pool_bwd(dpw, p, wp):
    s, pw_ = p.shape
    ng = len(POOL_WINDOWS)
    pgw = pw_ // ng
    tm = min(s, 512)
    hb = tm // POOL_HALO
    nblk = s // tm

    def body(dpw_ref, nxt_ref, p_ref, wp_ref, dwp_ref, du_ref):
        i = pl.program_id(0)
        nxt = (nxt_ref[...].astype(F32) * jnp.where(i < nblk - 1, 1.0, 0.0)).astype(BF16)
        dpw_all = jnp.concatenate([dpw_ref[...], nxt], axis=0)

        @pl.when(i == 0)
        def _():
            dwp_ref[...] = jnp.zeros_like(dwp_ref)

        dps = []
        for g in range(ng):
            sl = slice(g * pgw, (g + 1) * pgw)
            dwp_ref[g] += _dot(p_ref[:, sl], dpw_ref[:, sl], TN)
            dps.append(_dot(dpw_all[:, sl], wp_ref[g], NT))
        dp = jnp.concatenate(dps, axis=1)
        dpn = dp / _pool_counts(tm, tm + POOL_HALO, pgw, i * tm)
        du_ref[...] = (_window_sums(dpn, -1, pgw)[:tm] - dp[:tm]).astype(BF16)

    row = pl.BlockSpec((tm, pw_), lambda i: (i, 0))
    full = pl.BlockSpec((ng, pgw, pgw), lambda i: (0, 0, 0))
    return pl.pallas_call(
        body, name="pool_bwd", grid=(nblk,),
        in_specs=[row, pl.BlockSpec((POOL_HALO, pw_), lambda i: (jnp.minimum((i + 1) * hb, s // POOL_HALO - 1), 0)),
                  row, full],
        out_specs=[full, row],
        out_shape=[jax.ShapeDtypeStruct((ng, pgw, pgw), F32), jax.ShapeDtypeStruct((s, pw_), BF16)],
        compiler_params=_params(("arbitrary",)),
    )(dpw, dpw, p, wp)


def _attn_bwd(name, q, k, v, do, lse, delta, offs, cw, width):
    m = do.shape[0]
    nh = cw // HEAD_DIM
    rb = min(m, ATTN_STEP_ROWS)
    nsub = rb // SUB_BLOCK
    nstep = m // rb
    single = nstep == 1
    grid = (width // cw, nstep + (not single))
    scale = HEAD_DIM ** -0.5
    last = slice(rb - SUB_BLOCK, rb)

    def body(q_ref, k_ref, v_ref, do_ref, lse_ref, dl_ref, dq_ref, dk_ref, dv_ref,
             kprev, vprev, dk_carry, dv_carry):
        n = pl.program_id(1)

        @pl.when(n == 0)
        def _():
            for ref in (kprev, vprev, dk_carry, dv_carry):
                ref[...] = jnp.zeros_like(ref)

        qi = lax.broadcasted_iota(jnp.int32, (SUB_BLOCK, 2 * SUB_BLOCK), 0)
        kj = lax.broadcasted_iota(jnp.int32, (SUB_BLOCK, 2 * SUB_BLOCK), 1)
        band = (kj >= qi) & (kj <= qi + SUB_BLOCK)
        flush = jnp.where(n == nstep, 4 * SUB_BLOCK, 0)
        valids = [band & (kj >= jnp.where(n == 0, SUB_BLOCK, flush))] + [band & (kj >= flush)] * (nsub - 1)
        heads = [slice(hh * HEAD_DIM, (hh + 1) * HEAD_DIM) for hh in range(nh)]

        def rows(sub):
            return slice(sub * SUB_BLOCK, (sub + 1) * SUB_BLOCK)

        def cat(prev, ref, sub, sl):
            if sub == 0:
                return jnp.concatenate([prev[:, sl], ref[rows(0), sl]], axis=0)
            return ref[(sub - 1) * SUB_BLOCK:(sub + 1) * SUB_BLOCK, sl]

        kcats = [[cat(kprev, k_ref, sub, sl) for sl in heads] for sub in range(nsub)]
        scs = [[_dot(q_ref[rows(sub), sl], kcats[sub][hh], NT) for hh, sl in enumerate(heads)]
               for sub in range(nsub)]
        dps = [[_dot(do_ref[rows(sub), sl], cat(vprev, v_ref, sub, sl), NT) for sl in heads]
               for sub in range(nsub)]
        dqs = []
        stats = [(lse_ref[rows(sub), :], dl_ref[rows(sub), :]) for sub in range(nsub)]
        for hh, sl in enumerate(heads):
            dk2, dv2 = [], []
            for sub in range(nsub):
                lse_h, dl_h = _head_col(stats[sub][0], hh, nh), _head_col(stats[sub][1], hh, nh)
                valid = valids[sub]
                p = jnp.where(valid, jnp.exp(jnp.where(valid, scs[sub][hh] * scale, NEG) - lse_h), 0.0)
                ds = (p * (dps[sub][hh] - dl_h)).astype(BF16)
                dqs.append((sub, sl, (_dot(ds, kcats[sub][hh], NN) * scale).astype(BF16)))
                dk2.append(_dot(ds, q_ref[rows(sub), sl], TN) * scale)
                dv2.append(_dot(p.astype(BF16), do_ref[rows(sub), sl], TN))
            for out_ref, carry, new in ((dk_ref, dk_carry, dk2), (dv_ref, dv_carry, dv2)):
                if not single:
                    out_ref[last, sl] = (carry[last, sl] + new[0][:SUB_BLOCK]).astype(BF16)
                    if nsub > 1:
                        out_ref[:rb - SUB_BLOCK, sl] = carry[:rb - SUB_BLOCK, sl].astype(BF16)
                for sub in range(nsub):
                    val = new[sub][SUB_BLOCK:]
                    if sub + 1 < nsub:
                        val = val + new[sub + 1][:SUB_BLOCK]
                    if single:
                        out_ref[rows(sub), sl] = val.astype(BF16)
                    else:
                        carry[rows(sub), sl] = val

        @pl.when(n < nstep)
        def _():
            for sub, sl, val in dqs:
                dq_ref[rows(sub), sl] = val

        kprev[...] = k_ref[last, :]
        vprev[...] = v_ref[last, :]

    def cur(off):
        return pl.BlockSpec((rb, cw), lambda j, n: (jnp.minimum(n, nstep - 1), j + off))

    lagged = pl.BlockSpec((rb, cw), lambda j, n: (jnp.maximum(n - 1, 0), j))
    stat = pl.BlockSpec((rb, HEAD_DIM), lambda j, n: (jnp.minimum(n, nstep - 1), j))
    out = jax.ShapeDtypeStruct((m, width), BF16)
    return pl.pallas_call(
        body, name=name, grid=grid,
        in_specs=[cur(offs[0]), cur(offs[1]), cur(offs[2]), cur(0), stat, stat],
        out_specs=[cur(0), lagged, lagged], out_shape=[out, out, out],
        scratch_shapes=[pltpu.VMEM((SUB_BLOCK, cw), BF16)] * 2 + [pltpu.VMEM((rb, cw), F32)] * 2,
        compiler_params=_params(("parallel", "arbitrary")),
    )(q, k, v, do, lse, delta)


def _qkvu_grad(d1, d4, d16, du, cos_t, sin_t, dh):
    s, aw = du.shape
    tm = min(s, 512)

    def body(*refs):
        nat, v4, v16 = refs[0:3], refs[3:6], refs[6:9]
        cos_ref, sin_ref, du_ref, _, out_ref, s4, s16 = refs[9:]
        cos, sin = cos_ref[...], sin_ref[...]
        for part in range(3):
            _from_views((v4[part], v16[part]), (s4, s16))
            for hh in range(aw // HEAD_DIM):
                sl = slice(hh * HEAD_DIM, (hh + 1) * HEAD_DIM)
                t = nat[part][:, sl].astype(F32) + s4[hh] + s16[hh]
                if part < 2:
                    t = t * cos - pltpu.roll(t, HEAD_DIM // 2, 1) * sin
                out_ref[:, part * aw + hh * HEAD_DIM:part * aw + (hh + 1) * HEAD_DIM] = t.astype(BF16)
        out_ref[:, 3 * aw:] = du_ref[...]

    row = pl.BlockSpec((tm, aw), lambda i: (i, 0))
    tab = pl.BlockSpec((tm, HEAD_DIM), lambda i: (i, 0))
    return pl.pallas_call(
        body, name="qkvu_grad", grid=(s // tm,),
        in_specs=[row] * 3 + [_view_spec(tm, aw, 4)] * 3 + [_view_spec(tm, aw, 16)] * 3
        + [tab, tab, row, ANY_SPEC],
        out_specs=pl.BlockSpec((tm, 4 * aw), lambda i: (i, 0)),
        out_shape=jax.ShapeDtypeStruct(dh.shape, BF16), input_output_aliases={12: 0},
        scratch_shapes=[_heads_scratch(tm, aw)] * 2,
        compiler_params=_params(("parallel",)),
    )(*d1, *d4, *d16, cos_t, sin_t, du, dh)


def _in_proj_bwd_x(name, dh, w_in_st, base, scale_base, run_after=()):
    s, kdim = dh.shape
    d, n_sh = w_in_st.shape[1], w_in_st.shape[2]
    tm, tk = min(s, 512), min(n_sh, 2048)
    per = n_sh // tk

    ch = min(tm, 2 * EPILOGUE_ROWS)

    def epilogue(acc_ref, ex, outs):
        for c in range(tm // ch):
            rows = slice(c * ch, (c + 1) * ch)
            outs[0][rows, :] = scale_base * ex[0][rows, :] + acc_ref[rows, :]

    row = pl.BlockSpec((tm, d), lambda i, j, k: (i, 0))
    return _mm(
        name, (s // tm, 1, kdim // tk), dh, pl.BlockSpec((tm, tk), lambda i, j, k: (i, k)),
        w_in_st, pl.BlockSpec((None, d, tk), lambda i, j, k: (k // per, 0, k % per)), NT,
        [base], [row], [jax.ShapeDtypeStruct((s, d), F32)], [row], epilogue, (tm, d),
        acc_as_ref=True, run_after=run_after)[0]


def _chip_peers():
    x, y, c = lax.axis_index("x"), lax.axis_index("y"), lax.axis_index("c")
    return x, y, c, [(1 - x, y), (x, 1 - y), (1 - x, 1 - y)]


GATHER, GATHER_HALF, SCATTER, SIBLING = "gather", "gather_half", "scatter", "sibling"


def _exchange_peers(mode):
    x, y, c, chips = _chip_peers()
    if mode == SIBLING:
        return x, y, c, [(x, y, 1 - c)]
    return x, y, c, [(px, py, c) for px, py in chips]


def _core_half(ref_or_shape, c):
    rows = (ref_or_shape.shape[0]) // 2
    return pl.ds(c * rows, rows)


def _exchange_descriptor(mode, src, land, send, recv, p, peer, me, arriving):
    pid = 2 * peer[0] + peer[1]
    if mode == GATHER:
        src_ref, dst_ref = src, land.at[pid if arriving else me]
    elif mode == GATHER_HALF:
        rows = _core_half(src, peer[2])
        src_ref, dst_ref = src.at[rows], land.at[pid if arriving else me, rows]
    elif mode == SCATTER:
        src_ref, dst_ref = src.at[pid], land.at[p]
    else:
        src_ref, dst_ref = src, land
    return pltpu.make_async_remote_copy(
        src_ref=src_ref, dst_ref=dst_ref, send_sem=send.at[p], recv_sem=recv.at[p],
        device_id=peer, device_id_type=MESH)


def _exchange_start(name, mode, srcs, land_shapes):
    n = len(srcs)
    lands = [_hbm(lax.empty(shape, src.dtype)) for shape, src in zip(land_shapes, srcs)]

    def body(*refs):
        src_refs, land_refs = refs[:n], refs[n:2 * n]
        sends, recvs = refs[2 * n:3 * n], refs[3 * n:4 * n]
        token = refs[6 * n]
        x, y, c, peers = _exchange_peers(mode)
        me = 2 * x + y
        for w in range(n):
            for p, peer in enumerate(peers):
                _exchange_descriptor(mode, src_refs[w], land_refs[w], sends[w], recvs[w], p, peer,
                                     me, arriving=False).start()
        token[...] = jnp.zeros_like(token)

    sem = pltpu.SemaphoreType.DMA((3,))
    outs = pl.pallas_call(
        body, name=name, in_specs=[HBM_SPEC] * (2 * n),
        out_specs=[SEM_SPEC] * (2 * n) + [HBM_SPEC] * (2 * n) + [pl.BlockSpec(memory_space=pltpu.VMEM)],
        out_shape=[sem] * (2 * n) + [pltpu.HBM(a.shape, a.dtype) for a in (*srcs, *lands)]
        + [jax.ShapeDtypeStruct((8, 128), F32)],
        input_output_aliases={i: 2 * n + i for i in range(2 * n)},
        compiler_params=pltpu.CompilerParams(has_side_effects=DATAFLOW),
    )(*[_hbm(a) for a in srcs], *lands)
    return {"send": outs[:n], "recv": outs[n:2 * n], "src": outs[2 * n:3 * n],
            "land": outs[3 * n:4 * n], "token": outs[4 * n]}


def _exchange_wait(name, mode, started, which, after):
    m = len(which)

    def body(*refs):
        src_refs, land_refs = refs[:m], refs[m:2 * m]
        sends, recvs = refs[2 * m:3 * m], refs[3 * m:4 * m]
        x, y, c, peers = _exchange_peers(mode)
        me = 2 * x + y
        for w in range(m):
            for p, peer in enumerate(peers):
                _exchange_descriptor(mode, src_refs[w], land_refs[w], sends[w], recvs[w], p, peer,
                                     me, arriving=False).wait_send()
                _exchange_descriptor(mode, src_refs[w], land_refs[w], sends[w], recvs[w], p, peer,
                                     me, arriving=True).wait_recv()

    pick = lambda key: [started[key][w] for w in which]
    bufs = pick("src") + pick("land")
    after = list(after) if isinstance(after, (list, tuple)) else [after]
    outs = pl.pallas_call(
        body, name=name,
        in_specs=[HBM_SPEC] * (2 * m) + [SEM_SPEC] * (2 * m) + [ANY_SPEC] * len(after),
        out_specs=[HBM_SPEC] * (2 * m), out_shape=[pltpu.HBM(a.shape, a.dtype) for a in bufs],
        input_output_aliases={i: i for i in range(2 * m)},
        compiler_params=pltpu.CompilerParams(has_side_effects=DATAFLOW),
    )(*bufs, *pick("send"), *pick("recv"), *after)
    return outs[:m], outs[m:]


def _to_bf16(name, a, run_after):
    r, c = a.shape
    tm = min(r, 512)

    def body(a_ref, after_ref, out_ref):
        out_ref[...] = a_ref[...].astype(BF16)

    blk = pl.BlockSpec((tm, c), lambda i: (i, 0))
    return pl.pallas_call(
        body, name=name, grid=(r // tm,), in_specs=[blk, ANY_SPEC], out_specs=blk,
        out_shape=jax.ShapeDtypeStruct((r, c), BF16), compiler_params=_params(("parallel",)),
    )(a, run_after)


def _swap_halves(name, lands):
    n = len(lands)

    def body(*refs):
        bufs = refs[n:2 * n]
        send, recv = refs[2 * n:]
        x, y, c, chips = _chip_peers()
        started = []
        for w in range(n):
            half = bufs[w].shape[1] // 2
            for p, (px, py) in enumerate(chips):
                mine = bufs[w].at[2 * px + py, pl.ds(c * half, half)]
                cp = pltpu.make_async_remote_copy(
                    src_ref=mine, dst_ref=mine, send_sem=send.at[w, p], recv_sem=recv.at[w, p],
                    device_id=(x, y, 1 - c), device_id_type=MESH)
                cp.start()
                started.append(cp)
        for w in range(n):
            half = bufs[w].shape[1] // 2
            for p, (px, py) in enumerate(chips):
                theirs = bufs[w].at[2 * px + py, pl.ds((1 - c) * half, half)]
                pltpu.make_async_remote_copy(
                    src_ref=theirs, dst_ref=theirs, send_sem=send.at[w, p], recv_sem=recv.at[w, p],
                    device_id=(x, y, 1 - c), device_id_type=MESH).wait_recv()
        for cp in started:
            cp.wait_send()

    return pl.pallas_call(
        body, name=name, in_specs=[ANY_SPEC] * n, out_specs=[ANY_SPEC] * n,
        out_shape=[jax.ShapeDtypeStruct(a.shape, a.dtype) for a in lands],
        input_output_aliases={i: i for i in range(n)},
        scratch_shapes=[pltpu.SemaphoreType.DMA((n, 3)), pltpu.SemaphoreType.DMA((n, 3))],
    )(*lands)


def _place_own(name, shard, land, me):
    r, c = shard.shape
    tm = min(r, 512)

    def body(me_ref, shard_ref, land_ref, out_ref):
        out_ref[...] = shard_ref[...]

    return pl.pallas_call(
        body, name=name,
        grid_spec=pltpu.PrefetchScalarGridSpec(
            num_scalar_prefetch=1, grid=(r // tm,),
            in_specs=[pl.BlockSpec((tm, c), lambda i, me_ref: (i, 0)), ANY_SPEC],
            out_specs=pl.BlockSpec((None, tm, c), lambda i, me_ref: (me_ref[0], i, 0))),
        out_shape=jax.ShapeDtypeStruct(land.shape, land.dtype), input_output_aliases={2: 0},
        compiler_params=_params(("arbitrary",)),
    )(me, shard, land)


def _sum_slabs(name, grads, land, me):
    _, r, c = grads.shape
    tm = min(r, 256)

    def body(me_ref, own_ref, land_ref, out_ref):
        acc = own_ref[...].astype(F32)
        for p in range(3):
            acc = acc + land_ref[p].astype(F32)
        out_ref[...] = acc

    return pl.pallas_call(
        body, name=name,
        grid_spec=pltpu.PrefetchScalarGridSpec(
            num_scalar_prefetch=1, grid=(r // tm,),
            in_specs=[pl.BlockSpec((None, tm, c), lambda i, me_ref: (me_ref[0], i, 0)),
                      pl.BlockSpec((3, tm, c), lambda i, me_ref: (0, i, 0))],
            out_specs=pl.BlockSpec((tm, c), lambda i, me_ref: (i, 0))),
        out_shape=jax.ShapeDtypeStruct((r, c), F32), compiler_params=_params(("parallel",)),
    )(me, grads, land)


def _allreduce_stats(stats, run_after):
    n = len(stats)

    def body(*refs):
        ins, outs = refs[:n], refs[n + 1:2 * n + 1]
        mine, gath = refs[2 * n + 1:3 * n + 1], refs[3 * n + 1:4 * n + 1]
        send, recv = refs[4 * n + 1:]
        x, y, c = lax.axis_index("x"), lax.axis_index("y"), lax.axis_index("c")
        me = 4 * x + 2 * y + c
        flips = [(bx, by, bc) for bx in (0, 1) for by in (0, 1) for bc in (0, 1)][1:]

        def peer(f):
            return (x + f[0] * (1 - 2 * x), y + f[1] * (1 - 2 * y), c + f[2] * (1 - 2 * c))

        copies = []
        for t in range(n):
            tot = ins[t][0]
            for b in range(1, ins[t].shape[0]):
                tot = tot + ins[t][b]
            mine[t][...] = tot
            gath[t][me] = tot
            for k, f in enumerate(flips):
                cp = pltpu.make_async_remote_copy(
                    src_ref=mine[t], dst_ref=gath[t].at[me], send_sem=send.at[t, k],
                    recv_sem=recv.at[t, k], device_id=peer(f), device_id_type=MESH)
                cp.start()
                copies.append(cp)
        for t in range(n):
            for k, f in enumerate(flips):
                px, py, pc = peer(f)
                pltpu.make_async_remote_copy(
                    src_ref=mine[t], dst_ref=gath[t].at[4 * px + 2 * py + pc], send_sem=send.at[t, k],
                    recv_sem=recv.at[t, k], device_id=(px, py, pc), device_id_type=MESH).wait_recv()
        for cp in copies:
            cp.wait_send()
        for t in range(n):
            tot = gath[t][0]
            for dev in range(1, 8):
                tot = tot + gath[t][dev]
            outs[t][...] = tot

    vm = pl.BlockSpec(memory_space=pltpu.VMEM)
    return pl.pallas_call(
        body, name="allreduce_stats", in_specs=[vm] * n + [ANY_SPEC], out_specs=[vm] * n,
        out_shape=[jax.ShapeDtypeStruct(s.shape[1:], F32) for s in stats],
        scratch_shapes=[pltpu.VMEM(s.shape[1:], F32) for s in stats]
        + [pltpu.VMEM((8, *s.shape[1:]), F32) for s in stats]
        + [pltpu.SemaphoreType.DMA((n, 7)), pltpu.SemaphoreType.DMA((n, 7))],
    )(*stats, run_after)


def _adamw(name, w, m, v, g_parts):
    r, c = w.shape
    tm = min(r, 256)
    n_g = len(g_parts)

    def body(*refs):
        w_ref, m_ref, v_ref = refs[:3]
        g_refs = refs[3:3 + n_g]
        g_out, d_out, m_out, v_out = refs[3 + n_g:]
        g = g_refs[0][...]
        for gr in g_refs[1:]:
            g = g + gr[...]
        m_new = ADAM_B1 * m_ref[...] + (1.0 - ADAM_B1) * g
        v_new = ADAM_B2 * v_ref[...] + (1.0 - ADAM_B2) * (g * g)
        m_hat = m_new / (1.0 - ADAM_B1 ** ADAM_STEP)
        v_hat = v_new / (1.0 - ADAM_B2 ** ADAM_STEP)
        g_out[...] = g
        d_out[...] = -ADAM_LR * (m_hat / (jnp.sqrt(v_hat) + ADAM_EPS) + ADAM_WD * w_ref[...])
        m_out[...] = m_new
        v_out[...] = v_new

    blk = pl.BlockSpec((tm, c), lambda i: (i, 0))
    out = jax.ShapeDtypeStruct((r, c), F32)
    return pl.pallas_call(
        body, name=name, grid=(r // tm,), in_specs=[blk] * (3 + n_g), out_specs=[blk] * 4,
        out_shape=[out] * 4, compiler_params=_params(("parallel",)),
    )(w, m, v, *g_parts)


def _rope_tables(positions):
    half = HEAD_DIM // 2
    inv_freq = ROPE_THETA ** (-jnp.arange(half, dtype=F32) / half)
    ang = positions.astype(F32)[0, :, None] * inv_freq
    cos, sin = jnp.cos(ang), jnp.sin(ang)
    return jnp.concatenate([cos, cos], axis=-1), jnp.concatenate([-sin, sin], axis=-1)


def kernel(x, positions, w_in, w_pool, pool_scale, w_branch_attn, w_branch_pool, w_out, ln_mix_g, ln_mix_b, w_ff1, w_ff2, ln_ff_g, ln_ff_b, loss_target, m_w_in, m_w_pool, m_pool_scale, m_w_branch_attn, m_w_branch_pool, m_w_out, m_ln_mix_g, m_ln_mix_b, m_w_ff1, m_w_ff2, m_ln_ff_g, m_ln_ff_b, v_w_in, v_w_pool, v_pool_scale, v_w_branch_attn, v_w_branch_pool, v_w_out, v_ln_mix_g, v_ln_mix_b, v_w_ff1, v_w_ff2, v_ln_ff_g, v_ln_ff_b):
    s, d = x.shape[1], x.shape[2]
    aw = d // 2
    ng = len(POOL_WINDOWS)
    pgw = aw // ng
    x2d, target = x[0], loss_target[0]
    xb = x2d.astype(BF16)
    cos_t, sin_t = _rope_tables(positions)

    big = {"w_in": w_in[0], "w_pool": w_pool[0].reshape(-1, pgw), "w_branch_attn": w_branch_attn[0],
           "w_branch_pool": w_branch_pool[0], "w_out": w_out[0], "w_ff1": w_ff1[0], "w_ff2": w_ff2[0]}
    names = list(big)
    me_chip = (2 * lax.axis_index("x") + lax.axis_index("y")).astype(jnp.int32).reshape(1)
    land_shapes = [(N_CHIPS, *big[k].shape) for k in names]
    gathering_in = _exchange_start("gather_start_in", GATHER_HALF,
                                   [_to_bf16("to_bf16_w_in", big["w_in"], positions)], land_shapes[:1])
    shards = [_to_bf16(f"to_bf16_{k}", big[k], gathering_in["token"]) for k in names[1:]]
    gathering = _exchange_start("gather_start", GATHER, shards, land_shapes[1:])

    def gathered(name, which, after):
        srcs, lands = _exchange_wait(f"gather_wait_{name}", GATHER, gathering, which, after)
        return [_place_own(f"place_own_{names[w + 1]}", srcs[i], lands[i], me_chip)
                for i, w in enumerate(which)]

    rows_sh = pgw // N_CHIPS
    dff = N_CHIPS * big["w_ff2"].shape[0]

    srcs, lands = _exchange_wait("gather_wait_in", GATHER_HALF, gathering_in, [0],
                                 [gathering["token"], xb, cos_t, sin_t])
    w_in_st = _place_own("place_own_w_in", srcs[0], _swap_halves("swap_halves_in", lands)[0], me_chip)
    h, hv4, hv16 = _in_proj(xb, w_in_st, cos_t, sin_t, aw)
    (wp_st,) = gathered("pool", [0], h)
    wp = wp_st.reshape(N_CHIPS, ng, rows_sh, pgw).transpose(1, 0, 2, 3).reshape(ng, pgw, pgw)
    qkv = {1: (h, h, h), 4: (hv4, hv4, hv4), 16: (hv16, hv16, hv16)}
    offs = {dil: (0, dil, 2 * dil) for dil in DILATIONS}
    o_parts, lse_parts = [], []
    for dil in DILATIONS:
        o_p, lse_p = _attn_fwd(f"attn_fwd_d{dil}", *qkv[dil], offs[dil], aw, dil * aw)
        o_parts.append(o_p)
        lse_parts.append(lse_p)
    o_attn, lse, lse4, lse16 = _attn_combine(o_parts[0], lse_parts[0], o_parts[1:], lse_parts[1:])
    p, pw, y = _pool_fwd(h, wp, pool_scale, aw)
    wba_st, wbp_st, w_out_st = gathered("mix", [1, 2, 3], y)
    w_out_full = w_out_st.reshape(d, d)
    ya, yp, merged = _branch_merge(o_attn, y, wba_st, wbp_st, h, aw, d)
    xhat1, rstd1, x1b = _mix_norm(merged, w_out_full, x2d, ln_mix_g, ln_mix_b)
    w1_st, w2_st = gathered("ff", [4, 5], x1b)
    w2_full = w2_st.reshape(dff, d)
    r, r_slope = _ff_up(x1b, w1_st)
    dz2, dz2b, st2 = _ff_down_loss(r, w2_full, xhat1, ln_mix_g, ln_mix_b, ln_ff_g, ln_ff_b, target)

    def scatter_start(name, grads):
        return _exchange_start(f"scatter_start_{name}", SCATTER, grads, [(3, *g.shape[1:]) for g in grads])

    da = _ff_down_bwd(dz2b, w2_full, r_slope)
    g_w2 = _wgrad("wgrad_ff2", r, dz2b, d).reshape(N_CHIPS, dff // N_CHIPS, d)
    g_w1 = _wgrad("wgrad_ff1", x1b, da, dff // N_CHIPS)
    sent_ff = scatter_start("ff", [g_w1, g_w2])
    dz1, dz1b, st1 = _ff_up_bwd(da, w1_st, dz2, xhat1, rstd1, ln_mix_g, [sent_ff["token"]])
    dya, dyp, dh_gates = _mix_bwd(dz1b, w_out_full, h, ya, yp, aw)
    g_wout = _wgrad("wgrad_out", merged, dz1b, d).reshape(N_CHIPS, d // N_CHIPS, d)
    g_wba = _wgrad("wgrad_branch_attn", o_attn, dya, d // N_CHIPS)
    g_wbp = _wgrad("wgrad_branch_pool", y, dyp, d // N_CHIPS)
    do, delta, do4, do16, delta4, delta16 = _attn_out_bwd(dya, wba_st, o_attn)
    dpw, stp = _pool_out_bwd(dyp, wbp_st, pw, pool_scale)
    dwp, du = _pool_bwd(dpw, p, wp)
    g_wp = dwp.reshape(ng, N_CHIPS, rows_sh, pgw).transpose(1, 0, 2, 3).reshape(
        N_CHIPS, ng * rows_sh, pgw).astype(BF16)
    sent_mix = scatter_start("mix", [g_wp, g_wba, g_wbp, g_wout])

    bwd_in = {1: (do, lse, delta), 4: (do4, lse4, delta4), 16: (do16, lse16, delta16)}
    dqkv = {}
    for dil in DILATIONS:
        args = (*qkv[dil], *bwd_in[dil], offs[dil], aw, dil * aw)
        dqkv[dil] = _attn_bwd(f"attn_bwd_d{dil}", *args)
    dh = _qkvu_grad(dqkv[1], dqkv[4], dqkv[16], du, cos_t, sin_t, dh_gates)
    g_win = _wgrad("wgrad_in", xb, dh, d)
    sent_in = scatter_start("in", [g_win])
    grad_x = _in_proj_bwd_x("in_proj_bwd", dh, w_in_st, dz1, ALPHA, [sent_mix["token"], sent_in["token"]])

    moments = {"w_in": (m_w_in, v_w_in), "w_pool": (m_w_pool, v_w_pool),
               "w_branch_attn": (m_w_branch_attn, v_w_branch_attn),
               "w_branch_pool": (m_w_branch_pool, v_w_branch_pool), "w_out": (m_w_out, v_w_out),
               "w_ff1": (m_w_ff1, v_w_ff1), "w_ff2": (m_w_ff2, v_w_ff2)}
    originals = {"w_in": w_in, "w_pool": w_pool, "w_branch_attn": w_branch_attn,
                 "w_branch_pool": w_branch_pool, "w_out": w_out, "w_ff1": w_ff1, "w_ff2": w_ff2}
    res = {}

    def summed(name, sent, keys, after):
        srcs, lands = _exchange_wait(f"scatter_wait_{name}", SCATTER, sent, list(range(len(keys))), after)
        parts = [_sum_slabs(f"sum_slabs_{k}", srcs[i], lands[i], me_chip) for i, k in enumerate(keys)]
        return _exchange_start(f"cores_start_{name}", SIBLING, parts, [a.shape for a in parts])

    def updated(name, swapping, keys, after):
        mine, other = _exchange_wait(f"cores_wait_{name}", SIBLING, swapping, list(range(len(keys))), after)
        for i, k in enumerate(keys):
            mk, vk = (a.reshape(big[k].shape) for a in moments[k])
            outs = _adamw(f"adamw_{k}", big[k], mk, vk, [mine[i], other[i]])
            res[k] = [o.reshape(originals[k].shape) for o in outs]

    groups = {"ff": ["w_ff1", "w_ff2"], "mix": ["w_pool", "w_branch_attn", "w_branch_pool", "w_out"],
              "in": ["w_in"]}
    swap_mix = summed("mix", sent_mix, groups["mix"], grad_x)
    swap_ff = summed("ff", sent_ff, groups["ff"], swap_mix["token"])
    swap_in = summed("in", sent_in, groups["in"], swap_ff["token"])
    updated("mix", swap_mix, groups["mix"], swap_in["token"])
    updated("ff", swap_ff, groups["ff"], res["w_out"][0])
    updated("in", swap_in, groups["in"], res["w_ff2"][0])
    tot2, tot1, totp = _allreduce_stats([st2, st1, stp], res["w_in"][0])

    def pad_d(a):
        return jnp.pad(a, ((0, 0), (0, d - a.shape[1])))

    small = ["ln_mix_g", "ln_mix_b", "ln_ff_g", "ln_ff_b", "pool_scale"]
    small_w = {"ln_mix_g": ln_mix_g, "ln_mix_b": ln_mix_b, "ln_ff_g": ln_ff_g, "ln_ff_b": ln_ff_b,
               "pool_scale": pool_scale}
    small_m = {"ln_mix_g": m_ln_mix_g, "ln_mix_b": m_ln_mix_b, "ln_ff_g": m_ln_ff_g,
               "ln_ff_b": m_ln_ff_b, "pool_scale": m_pool_scale}
    small_v = {"ln_mix_g": v_ln_mix_g, "ln_mix_b": v_ln_mix_b, "ln_ff_g": v_ln_ff_g,
               "ln_ff_b": v_ln_ff_b, "pool_scale": v_pool_scale}
    small_g = [tot1[0:1], tot1[1:2], tot2[0:1], tot2[1:2], pad_d(totp[0:1])]

    def pack(rows):
        return jnp.concatenate([pad_d(a) for a in rows] + [jnp.zeros((8 - len(rows), d), F32)], axis=0)

    outs = _adamw("adamw_small", pack([small_w[k] for k in small]), pack([small_m[k] for k in small]),
                  pack([small_v[k] for k in small]), [pack(small_g)])
    for i, k in enumerate(small):
        res[k] = [o[i:i + 1, :small_w[k].shape[1]] for o in outs]
    loss = tot2[2, 0]

    order = ["w_in", "w_pool", "pool_scale", "w_branch_attn", "w_branch_pool", "w_out", "ln_mix_g",
             "ln_mix_b", "w_ff1", "w_ff2", "ln_ff_g", "ln_ff_b"]
    result = [loss, grad_x[None]]
    for idx in range(4):
        result += [res[k][idx] for k in order]
    return tuple(result)
```

```python
import jax
import jax.numpy as jnp
from jax import lax
from jax.experimental import pallas as pl
from jax.experimental.pallas import tpu as pltpu

F32 = jnp.float32
BF16 = jnp.bfloat16
MESH = pl.DeviceIdType.MESH

HEAD_DIM = 128
SUB_BLOCK = 128
ATTN_STEP_ROWS = 512
DILATIONS = (1, 4, 16)
POOL_WINDOWS = (2, 4, 8, 16)
POOL_HALO = 16
ROPE_THETA = 10000.0
LN_EPS = 1e-5
ALPHA = 2.0 ** 0.25
ADAM_LR, ADAM_B1, ADAM_B2, ADAM_EPS, ADAM_WD, ADAM_STEP = 0.001, 0.9, 0.999, 1e-08, 0.01, 10
NEG = -1e30
N_CHIPS = 4
VMEM_LIMIT = 62 * 1024 * 1024
EPILOGUE_ROWS = 128

def _params(sem=None, vmem=VMEM_LIMIT):
    kw = {"vmem_limit_bytes": vmem}
    if sem is not None:
        kw["dimension_semantics"] = sem
    return pltpu.CompilerParams(**kw)


def _dot(a, b, contract):
    return lax.dot_general(a, b, (contract, ((), ())), preferred_element_type=F32)


ANY_SPEC = pl.BlockSpec(memory_space=pl.ANY)
HBM_SPEC = pl.BlockSpec(memory_space=pltpu.HBM)
SEM_SPEC = pl.BlockSpec(memory_space=pltpu.SEMAPHORE)
DATAFLOW = pltpu.SideEffectType.DATAFLOW_SIDE_EFFECTING


def _hbm(a):
    return pltpu.with_memory_space_constraint(a, pltpu.HBM)


NN = ((1,), (0,))
NT = ((1,), (1,))
TN = ((0,), (0,))


def _mm(name, grid, a, a_spec, b, b_spec, contract, extras, extra_specs, out_shape, out_specs,
        epilogue, acc_shape, acc_as_ref=False, run_after=(), scratch=(),
        semantics=("parallel", "parallel", "arbitrary")):
    nk = grid[2]
    n_ex = len(extras)
    n_in = 2 + n_ex + len(run_after)
    n_out = len(out_shape)
    n_scr = len(scratch)

    def body(*refs):
        a_ref, b_ref = refs[0], refs[1]
        ex = refs[2:2 + n_ex]
        outs = refs[n_in:n_in + n_out]
        scr = refs[n_in + n_out:n_in + n_out + n_scr]
        if nk == 1:
            epilogue(_dot(a_ref[...], b_ref[...], contract), ex, outs, *scr)
        else:
            acc = refs[n_in + n_out + n_scr]
            k = pl.program_id(2)

            @pl.when(k == 0)
            def _():
                acc[...] = jnp.zeros_like(acc)

            acc[...] += _dot(a_ref[...], b_ref[...], contract)

            @pl.when(k == nk - 1)
            def _():
                epilogue(acc if acc_as_ref else acc[...], ex, outs, *scr)

    acc_scratch = [pltpu.VMEM(acc_shape, F32)] if nk > 1 else []
    return pl.pallas_call(
        body, name=name, grid=grid,
        in_specs=[a_spec, b_spec, *extra_specs, *[ANY_SPEC] * len(run_after)],
        out_specs=out_specs, out_shape=out_shape, scratch_shapes=[*scratch, *acc_scratch],
        compiler_params=_params(semantics),
    )(a, b, *extras, *run_after)


def _stats_rows(rows, width):
    idx = lax.broadcasted_iota(jnp.int32, (8, width), 0)
    out = jnp.zeros((8, width), F32)
    for r, v in enumerate(rows):
        out = jnp.where(idx == r, jnp.broadcast_to(v, (8, width)), out)
    return out


def _layer_norm_fwd(z):
    mu = jnp.mean(z, axis=-1, keepdims=True)
    zc = z - mu
    var = jnp.mean(zc * zc, axis=-1, keepdims=True)
    rstd = lax.rsqrt(var + LN_EPS)
    return zc * rstd, rstd


def _layer_norm_bwd(dy, xhat, rstd, g):
    dxh = dy * g
    m1 = jnp.mean(dxh, axis=-1, keepdims=True)
    m2 = jnp.mean(dxh * xhat, axis=-1, keepdims=True)
    return rstd * (dxh - m1 - xhat * m2)


def _heads_scratch(rows, width):
    return pltpu.VMEM((width // HEAD_DIM, rows, HEAD_DIM), F32)


def _to_views(src_ref, view_refs, dtype, heads=None):
    nh, rows, _ = src_ref.shape
    width = nh * HEAD_DIM
    for dil, view_ref in zip(DILATIONS[1:], view_refs):
        for r in range(dil):
            for hh in (range(nh) if heads is None else heads):
                c0 = r * width + hh * HEAD_DIM
                view_ref[:, c0:c0 + HEAD_DIM] = (
                    src_ref[hh, pl.ds(r, rows // dil, stride=dil), :].astype(dtype))


def _from_views(view_refs, dst_refs):
    nh, rows, _ = dst_refs[0].shape
    width = nh * HEAD_DIM
    for dil, view_ref, dst_ref in zip(DILATIONS[1:], view_refs, dst_refs):
        for r in range(dil):
            for hh in range(nh):
                c0 = r * width + hh * HEAD_DIM
                dst_ref[hh, pl.ds(r, rows // dil, stride=dil), :] = (
                    view_ref[:, c0:c0 + HEAD_DIM].astype(F32))


def _view_shape(rows, width, dil, parts=1):
    return (rows // dil, parts * dil * width)


def _in_proj(xb, w_in_st, cos_t, sin_t, aw):
    s, d = xb.shape
    n_sh = w_in_st.shape[2]
    tm, tn = min(s, 1024), aw
    per = n_sh // tn
    grid = (s // tm, (N_CHIPS * n_sh) // tn, 1)

    def epilogue(acc, ex, outs, scr):
        cos_ref, sin_ref = ex
        h_ref, v4_ref, v16_ref = outs
        seg = pl.program_id(1)

        heads = [slice(hh * HEAD_DIM, (hh + 1) * HEAD_DIM) for hh in range(tn // HEAD_DIM)]

        @pl.when(seg < 2)
        def _():
            cos, sin = cos_ref[...], sin_ref[...]
            for hh, sl in enumerate(heads):
                t = acc[:, sl]
                scr[hh] = t * cos + pltpu.roll(t, HEAD_DIM // 2, 1) * sin

        @pl.when(seg == 2)
        def _():
            for hh, sl in enumerate(heads):
                scr[hh] = acc[:, sl]

        @pl.when(seg < 3)
        def _():
            for hh, sl in enumerate(heads):
                h_ref[:, sl] = scr[hh].astype(BF16)
            _to_views(scr, (v4_ref, v16_ref), BF16)

        @pl.when(seg == 3)
        def _():
            h_ref[...] = acc.astype(BF16)

        @pl.when(seg >= 4)
        def _():
            h_ref[...] = (0.5 * jnp.tanh(0.5 * acc) + 0.5).astype(BF16)

    def view_spec(dil):
        return pl.BlockSpec((tm // dil, dil * aw), lambda i, j, k: (i, jnp.minimum(j, 2)))

    return _mm(
        "in_proj", grid, xb, pl.BlockSpec((tm, d), lambda i, j, k: (i, 0)),
        w_in_st, pl.BlockSpec((None, d, tn), lambda i, j, k: (j // per, 0, j % per)), NN,
        [cos_t, sin_t], [pl.BlockSpec((tm, HEAD_DIM), lambda i, j, k: (i, 0))] * 2,
        [jax.ShapeDtypeStruct((s, N_CHIPS * n_sh), BF16)]
        + [jax.ShapeDtypeStruct(_view_shape(s, aw, dil, 3), BF16) for dil in DILATIONS[1:]],
        [pl.BlockSpec((tm, tn), lambda i, j, k: (i, j))] + [view_spec(dil) for dil in DILATIONS[1:]],
        epilogue, None, scratch=[_heads_scratch(tm, aw)],
        semantics=("parallel", "arbitrary", "arbitrary"))


def _pack_heads(cols):
    rows, rep = cols[0].shape[0], HEAD_DIM // len(cols)
    lane = lax.broadcasted_iota(jnp.int32, (rows, HEAD_DIM), 1)
    out = jnp.zeros((rows, HEAD_DIM), F32)
    for hh, col in enumerate(cols):
        out = jnp.where((lane >= hh * rep) & (lane < (hh + 1) * rep), col, out)
    return out


def _head_col(packed, hh, nh):
    lane = lax.broadcasted_iota(jnp.int32, packed.shape, 1)
    return jnp.sum(jnp.where(lane == hh * (HEAD_DIM // nh), packed, 0.0), axis=-1, keepdims=True)


def _band_masks(block_idx):
    qi = lax.broadcasted_iota(jnp.int32, (SUB_BLOCK, 2 * SUB_BLOCK), 0)
    kj = lax.broadcasted_iota(jnp.int32, (SUB_BLOCK, 2 * SUB_BLOCK), 1)
    first_key = jnp.where(block_idx > 0, 0, SUB_BLOCK)
    return (kj >= qi) & (kj <= qi + SUB_BLOCK) & (kj >= first_key)


def _attn_fwd(name, q, k, v, offs, cw, width):
    m = q.shape[0]
    nh = cw // HEAD_DIM
    rb = min(m, ATTN_STEP_ROWS)
    nsub = rb // SUB_BLOCK
    grid = (width // cw, m // rb)
    scale = HEAD_DIM ** -0.5

    def body(q_ref, k_ref, v_ref, o_ref, lse_ref, kprev, vprev):
        n = pl.program_id(1)

        @pl.when(n == 0)
        def _():
            kprev[...] = jnp.zeros_like(kprev)
            vprev[...] = jnp.zeros_like(vprev)

        heads = [slice(hh * HEAD_DIM, (hh + 1) * HEAD_DIM) for hh in range(nh)]
        valids = [_band_masks(n)] + [_band_masks(1)] * (nsub - 1)

        def rows(sub):
            return slice(sub * SUB_BLOCK, (sub + 1) * SUB_BLOCK)

        def cat(prev, ref, sub, sl):
            if sub == 0:
                return jnp.concatenate([prev[:, sl], ref[rows(0), sl]], axis=0)
            return ref[(sub - 1) * SUB_BLOCK:(sub + 1) * SUB_BLOCK, sl]

        scs = [[_dot(q_ref[rows(sub), sl], cat(kprev, k_ref, sub, sl), NT) for sl in heads]
               for sub in range(nsub)]
        for sub in range(nsub):
            lses = []
            for hh, sl in enumerate(heads):
                sc = jnp.where(valids[sub], scs[sub][hh] * scale, NEG)
                mx = jnp.max(sc, axis=-1, keepdims=True)
                p = jnp.exp(sc - mx)
                l = jnp.sum(p, axis=-1, keepdims=True)
                o = _dot(p.astype(BF16), cat(vprev, v_ref, sub, sl), NN) / l
                o_ref[rows(sub), sl] = o.astype(BF16)
                lses.append(mx + jnp.log(l))
            lse_ref[rows(sub), :] = _pack_heads(lses)
        kprev[...] = k_ref[rows(nsub - 1), :]
        vprev[...] = v_ref[rows(nsub - 1), :]

    def cur(off):
        return pl.BlockSpec((rb, cw), lambda j, n: (n, j + off))

    return pl.pallas_call(
        body, name=name, grid=grid,
        in_specs=[cur(offs[0]), cur(offs[1]), cur(offs[2])],
        out_specs=[cur(0), pl.BlockSpec((rb, HEAD_DIM), lambda j, n: (n, j))],
        out_shape=[jax.ShapeDtypeStruct((m, width), BF16),
                   jax.ShapeDtypeStruct((m, width // cw * HEAD_DIM), F32)],
        scratch_shapes=[pltpu.VMEM((SUB_BLOCK, cw), BF16)] * 2,
        compiler_params=_params(("parallel", "arbitrary")),
    )(q, k, v)


def _view_spec(tm, aw, dil):
    return pl.BlockSpec((tm // dil, dil * aw), lambda i: (i, 0))


def _attn_combine(o1, l1, o_views, l_views):
    s, aw = o1.shape
    nh = aw // HEAD_DIM
    tm = min(s, 512)

    def body(o1_ref, l1_ref, o4_ref, o16_ref, l4_ref, l16_ref, o_ref, lse_ref, lse4_ref, lse16_ref,
             so4, so16, sl4, sl16, stot):
        _from_views((o4_ref, o16_ref), (so4, so16))
        _from_views((l4_ref, l16_ref), (sl4, sl16))
        a, b, c = l1_ref[...], sl4[0], sl16[0]
        mx = jnp.maximum(jnp.maximum(a, b), c)
        ea, eb, ec = jnp.exp(a - mx), jnp.exp(b - mx), jnp.exp(c - mx)
        tot = ea + eb + ec
        inv = 1.0 / tot
        wa, wb, wc = ea * inv, eb * inv, ec * inv
        lse_tot = mx + jnp.log(tot)
        stot[0] = lse_tot
        lse_ref[...] = lse_tot
        _to_views(stot, (lse4_ref, lse16_ref), F32)
        for hh in range(nh):
            sl = slice(hh * HEAD_DIM, (hh + 1) * HEAD_DIM)
            o = (_head_col(wa, hh, nh) * o1_ref[:, sl].astype(F32) + _head_col(wb, hh, nh) * so4[hh]
                 + _head_col(wc, hh, nh) * so16[hh])
            o_ref[:, sl] = o.astype(BF16)

    row = pl.BlockSpec((tm, aw), lambda i: (i, 0))
    stat = pl.BlockSpec((tm, HEAD_DIM), lambda i: (i, 0))
    views = [_view_spec(tm, aw, dil) for dil in DILATIONS[1:]]
    stat_views = [_view_spec(tm, HEAD_DIM, dil) for dil in DILATIONS[1:]]
    return pl.pallas_call(
        body, name="attn_combine", grid=(s // tm,), in_specs=[row, stat, *views, *stat_views],
        out_specs=[row, stat, *stat_views],
        out_shape=[jax.ShapeDtypeStruct((s, aw), BF16), jax.ShapeDtypeStruct((s, HEAD_DIM), F32)]
        + [jax.ShapeDtypeStruct(_view_shape(s, HEAD_DIM, dil), F32) for dil in DILATIONS[1:]],
        scratch_shapes=[_heads_scratch(tm, aw)] * 2 + [_heads_scratch(tm, HEAD_DIM)] * 3,
        compiler_params=_params(("parallel",)),
    )(o1, l1, *o_views, *l_views)


def _pool_counts(tm, rows, pgw, row0):
    t = lax.broadcasted_iota(jnp.int32, (rows, len(POOL_WINDOWS) * pgw), 0) + row0
    col = lax.broadcasted_iota(jnp.int32, (rows, len(POOL_WINDOWS) * pgw), 1)
    w = jnp.full((rows, len(POOL_WINDOWS) * pgw), POOL_WINDOWS[0], jnp.int32)
    for g in range(1, len(POOL_WINDOWS)):
        w = jnp.where(col >= g * pgw, POOL_WINDOWS[g], w)
    return jnp.minimum(t + 1, w).astype(F32)


def _window_sums(xs, direction, pgw):
    rows = xs.shape[0]
    acc = xs
    out = None
    col = lax.broadcasted_iota(jnp.int32, xs.shape, 1)
    for g, w in enumerate(POOL_WINDOWS):
        sh = w // 2
        acc = acc + pltpu.roll(acc, sh if direction > 0 else rows - sh, 0)
        out = acc if out is None else jnp.where(col >= g * pgw, acc, out)
    return out


def _pool_fwd(h, wp, scale, aw):
    s = h.shape[0]
    pw_ = aw
    pgw = pw_ // len(POOL_WINDOWS)
    tm = min(s, 512)
    hb = tm // POOL_HALO

    def body(u_ref, halo_ref, wp_ref, sc_ref, p_ref, pw_ref, y_ref):
        i = pl.program_id(0)
        u = u_ref[...].astype(F32)
        halo = halo_ref[...].astype(F32) * jnp.where(i > 0, 1.0, 0.0)
        xs = jnp.concatenate([halo, u], axis=0)
        sums = _window_sums(xs, +1, pgw)[POOL_HALO:]
        p = (sums / _pool_counts(tm, tm, pgw, i * tm) - u).astype(BF16)
        p_ref[...] = p
        sc = sc_ref[...]
        for g in range(len(POOL_WINDOWS)):
            sl = slice(g * pgw, (g + 1) * pgw)
            pw = _dot(p[:, sl], wp_ref[g], NN)
            pw_ref[:, sl] = pw.astype(BF16)
            y_ref[:, sl] = (pw * sc[:, sl]).astype(BF16)

    out = jax.ShapeDtypeStruct((s, pw_), BF16)
    row = pl.BlockSpec((tm, pw_), lambda i: (i, 0))
    return pl.pallas_call(
        body, name="pool_fwd", grid=(s // tm,),
        in_specs=[pl.BlockSpec((tm, pw_), lambda i: (i, 3)),
                  pl.BlockSpec((POOL_HALO, pw_), lambda i: (jnp.maximum(i * hb - 1, 0), 3)),
                  pl.BlockSpec(wp.shape, lambda i: (0, 0, 0)),
                  pl.BlockSpec((1, pw_), lambda i: (0, 0))],
        out_specs=[row, row, row], out_shape=[out, out, out],
        compiler_params=_params(("parallel",)),
    )(h, h, wp, scale)


def _branch_merge(o_attn, y, wba_st, wbp_st, h, aw, d):
    s = o_attn.shape[0]
    tn = wba_st.shape[2]
    tm = min(s, 1024)
    ga0 = 4 * aw // tn
    gp0 = (4 * aw + d) // tn

    def body(o_ref, y_ref, wa_ref, wp_ref, sga_ref, sgp_ref, ya_ref, yp_ref, mg_ref):
        ya = _dot(o_ref[...], wa_ref[...], NN)
        yp = _dot(y_ref[...], wp_ref[...], NN)
        ya_ref[...] = ya.astype(BF16)
        yp_ref[...] = yp.astype(BF16)
        mg_ref[...] = (sga_ref[...].astype(F32) * ya + sgp_ref[...].astype(F32) * yp).astype(BF16)

    out = jax.ShapeDtypeStruct((s, d), BF16)
    blk = pl.BlockSpec((tm, tn), lambda i, j: (i, j))
    return pl.pallas_call(
        body, name="branch_merge", grid=(s // tm, N_CHIPS),
        in_specs=[pl.BlockSpec((tm, aw), lambda i, j: (i, 0)),
                  pl.BlockSpec((tm, aw), lambda i, j: (i, 0)),
                  pl.BlockSpec((None, aw, tn), lambda i, j: (j, 0, 0)),
                  pl.BlockSpec((None, aw, tn), lambda i, j: (j, 0, 0)),
                  pl.BlockSpec((tm, tn), lambda i, j: (i, j + ga0)),
                  pl.BlockSpec((tm, tn), lambda i, j: (i, j + gp0))],
        out_specs=[blk, blk, blk], out_shape=[out, out, out],
        compiler_params=_params(("parallel", "parallel")),
    )(o_attn, y, wba_st, wbp_st, h, h)


def _mix_norm(merged, w_out, x, g1, b1):
    s, d = x.shape
    tm = min(s, 256)

    def epilogue(acc, ex, outs):
        x_ref, g_ref, b_ref = ex
        xh_ref, rs_ref, xb_ref = outs
        xhat, rstd = _layer_norm_fwd(ALPHA * x_ref[...] + acc)
        xh_ref[...] = xhat
        rs_ref[...] = rstd
        xb_ref[...] = (xhat * g_ref[...] + b_ref[...]).astype(BF16)

    row = pl.BlockSpec((tm, d), lambda i, j, k: (i, 0))
    vec = pl.BlockSpec((1, d), lambda i, j, k: (0, 0))
    return _mm(
        "mix_norm", (s // tm, 1, 1), merged, row, w_out, pl.BlockSpec((d, d), lambda i, j, k: (0, 0)),
        NN, [x, g1, b1], [row, vec, vec],
        [jax.ShapeDtypeStruct((s, d), F32), jax.ShapeDtypeStruct((s, 1), F32),
         jax.ShapeDtypeStruct((s, d), BF16)],
        [row, pl.BlockSpec((tm, 1), lambda i, j, k: (i, 0)), row], epilogue, None)


def _ff_up(x1b, w1_st):
    s, d = x1b.shape
    n_sh = w1_st.shape[2]
    tm, tn = min(s, 1024), min(n_sh, 1024)
    per = n_sh // tn

    def epilogue(acc, ex, outs):
        r = jnp.maximum(acc, 0.0)
        outs[0][...] = (r * r).astype(BF16)
        outs[1][...] = (2.0 * r).astype(BF16)

    blk = pl.BlockSpec((tm, tn), lambda i, j, k: (i, j))
    out = jax.ShapeDtypeStruct((s, N_CHIPS * n_sh), BF16)
    return _mm(
        "ff_up", (s // tm, N_CHIPS * per, 1), x1b, pl.BlockSpec((tm, d), lambda i, j, k: (i, 0)),
        w1_st, pl.BlockSpec((None, d, tn), lambda i, j, k: (j // per, 0, j % per)), NN, [], [],
        [out, out], [blk, blk], epilogue, None)


def _ff_down_loss(r, w2, xhat1, g1, b1, g2, b2, target):
    s, d = xhat1.shape
    dff = r.shape[1]
    tm, tk = min(s, 512), min(dff, 2048)
    ch = min(tm, EPILOGUE_ROWS)

    def epilogue(acc_ref, ex, outs):
        xh1_ref, g1_ref, b1_ref, g2_ref, b2_ref, t_ref = ex
        dz_ref, dzb_ref, st_ref = outs
        g1v, b1v, g2v, b2v = g1_ref[...], b1_ref[...], g2_ref[...], b2_ref[...]
        dg = db = loss = None
        for c in range(tm // ch):
            rows = slice(c * ch, (c + 1) * ch)
            x1 = xh1_ref[rows, :] * g1v + b1v
            xhat2, rstd2 = _layer_norm_fwd(ALPHA * x1 + acc_ref[rows, :])
            err = xhat2 * g2v + b2v - t_ref[rows, :]
            dy = err * (1.0 / d)
            dz = _layer_norm_bwd(dy, xhat2, rstd2, g2v)
            dz_ref[rows, :] = dz
            dzb_ref[rows, :] = dz.astype(BF16)
            parts = (jnp.sum(dy * xhat2, axis=0, keepdims=True), jnp.sum(dy, axis=0, keepdims=True),
                     jnp.sum(jnp.sum(err * err, axis=-1, keepdims=True), axis=0, keepdims=True))
            dg, db, loss = parts if c == 0 else (dg + parts[0], db + parts[1], loss + parts[2])
        st_ref[...] = _stats_rows([dg, db, jnp.broadcast_to((0.5 / d) * loss, (1, d))], d)

    row = pl.BlockSpec((tm, d), lambda i, j, k: (i, 0))
    vec = pl.BlockSpec((1, d), lambda i, j, k: (0, 0))
    return _mm(
        "ff_down_loss", (s // tm, 1, dff // tk), r, pl.BlockSpec((tm, tk), lambda i, j, k: (i, k)),
        w2, pl.BlockSpec((tk, d), lambda i, j, k: (k, 0)), NN,
        [xhat1, g1, b1, g2, b2, target], [row, vec, vec, vec, vec, row],
        [jax.ShapeDtypeStruct((s, d), F32), jax.ShapeDtypeStruct((s, d), BF16),
         jax.ShapeDtypeStruct((s // tm, 8, d), F32)],
        [row, row, pl.BlockSpec((None, 8, d), lambda i, j, k: (i, 0, 0))], epilogue, (tm, d),
        acc_as_ref=True)


def _ff_down_bwd(dz2b, w2, r_slope):
    s, d = dz2b.shape
    dff = r_slope.shape[1]
    tm, tn = min(s, 1024), min(dff, 1024)

    def epilogue(acc, ex, outs):
        outs[0][...] = (acc * ex[0][...].astype(F32)).astype(BF16)

    blk = pl.BlockSpec((tm, tn), lambda i, j, k: (i, j))
    return _mm(
        "ff_down_bwd", (s // tm, dff // tn, 1), dz2b, pl.BlockSpec((tm, d), lambda i, j, k: (i, 0)),
        w2, pl.BlockSpec((tn, d), lambda i, j, k: (j, 0)), NT, [r_slope], [blk],
        [jax.ShapeDtypeStruct((s, dff), BF16)], [blk], epilogue, None)[0]


def _wgrad(name, a, g, n_sh):
    s, rows = a.shape
    cols = g.shape[1]
    tm, tn, tk = min(rows, 2048), min(cols, 1024), min(s, 2048)
    if tn >= n_sh:
        span = tn // n_sh
        out_spec = pl.BlockSpec((span, tm, n_sh), lambda i, j, k: (j, i, 0))

        def epilogue(acc_ref, ex, outs):
            for sh in range(span):
                outs[0][sh] = acc_ref[:, sh * n_sh:(sh + 1) * n_sh].astype(BF16)
    else:
        per = n_sh // tn
        out_spec = pl.BlockSpec((None, tm, tn), lambda i, j, k: (j // per, i, j % per))

        def epilogue(acc_ref, ex, outs):
            outs[0][...] = acc_ref[...].astype(BF16)

    return _mm(
        name, (rows // tm, cols // tn, s // tk), a, pl.BlockSpec((tk, tm), lambda i, j, k: (k, i)),
        g, pl.BlockSpec((tk, tn), lambda i, j, k: (k, j)), TN, [], [],
        [jax.ShapeDtypeStruct((cols // n_sh, rows, n_sh), BF16)], [out_spec], epilogue,
        (tm, tn), acc_as_ref=True)[0]


def _ff_up_bwd(da, w1_st, dz2, xhat1, rstd1, g1, run_after):
    s, d = dz2.shape
    n_sh = w1_st.shape[2]
    tm, tk = min(s, 512), min(n_sh, 2048)
    per = n_sh // tk
    ch = min(tm, EPILOGUE_ROWS)

    def epilogue(acc_ref, ex, outs):
        dz2_ref, xh_ref, rs_ref, g_ref = ex
        dz_ref, dzb_ref, st_ref = outs
        gv = g_ref[...]
        dg = db = None
        for c in range(tm // ch):
            rows = slice(c * ch, (c + 1) * ch)
            dx1 = ALPHA * dz2_ref[rows, :] + acc_ref[rows, :]
            xhat = xh_ref[rows, :]
            dz = _layer_norm_bwd(dx1, xhat, rs_ref[rows, :], gv)
            dz_ref[rows, :] = dz
            dzb_ref[rows, :] = dz.astype(BF16)
            parts = (jnp.sum(dx1 * xhat, axis=0, keepdims=True), jnp.sum(dx1, axis=0, keepdims=True))
            dg, db = parts if c == 0 else (dg + parts[0], db + parts[1])
        st_ref[...] = _stats_rows([dg, db], d)

    row = pl.BlockSpec((tm, d), lambda i, j, k: (i, 0))
    return _mm(
        "ff_up_bwd", (s // tm, 1, N_CHIPS * per), da, pl.BlockSpec((tm, tk), lambda i, j, k: (i, k)),
        w1_st, pl.BlockSpec((None, d, tk), lambda i, j, k: (k // per, 0, k % per)), NT,
        [dz2, xhat1, rstd1, g1],
        [row, row, pl.BlockSpec((tm, 1), lambda i, j, k: (i, 0)), pl.BlockSpec((1, d), lambda i, j, k: (0, 0))],
        [jax.ShapeDtypeStruct((s, d), F32), jax.ShapeDtypeStruct((s, d), BF16),
         jax.ShapeDtypeStruct((s // tm, 8, d), F32)],
        [row, row, pl.BlockSpec((None, 8, d), lambda i, j, k: (i, 0, 0))], epilogue, (tm, d),
        acc_as_ref=True, run_after=run_after)


def _mix_bwd(dz1b, w_out, h, ya, yp, aw):
    s, d = dz1b.shape
    tm = min(s, 256)
    gblk = 4 * aw // d

    def epilogue(acc, ex, outs):
        sga_ref, sgp_ref, ya_ref, yp_ref = ex
        dya_ref, dyp_ref, dg_ref = outs
        sga, sgp = sga_ref[...].astype(F32), sgp_ref[...].astype(F32)
        dya_ref[...] = (acc * sga).astype(BF16)
        dyp_ref[...] = (acc * sgp).astype(BF16)
        dg_ref[:, :d] = (acc * ya_ref[...].astype(F32) * (sga * (1.0 - sga))).astype(BF16)
        dg_ref[:, d:] = (acc * yp_ref[...].astype(F32) * (sgp * (1.0 - sgp))).astype(BF16)

    row = pl.BlockSpec((tm, d), lambda i, j, k: (i, 0))
    return _mm(
        "mix_bwd", (s // tm, 1, 1), dz1b, row, w_out, pl.BlockSpec((d, d), lambda i, j, k: (0, 0)), NT,
        [h, h, ya, yp],
        [pl.BlockSpec((tm, d), lambda i, j, k: (i, gblk)),
         pl.BlockSpec((tm, d), lambda i, j, k: (i, gblk + 1)), row, row],
        [jax.ShapeDtypeStruct((s, d), BF16), jax.ShapeDtypeStruct((s, d), BF16),
         jax.ShapeDtypeStruct((s, 4 * d), BF16)],
        [row, row, pl.BlockSpec((tm, 2 * d), lambda i, j, k: (i, 1))], epilogue, None)


def _branch_in_bwd(name, tm, dyb, wb_st, epilogue, extras, extra_specs, out_shape, out_specs,
                   scratch=()):
    s, d = dyb.shape
    aw = wb_st.shape[1]
    wb_t = wb_st.transpose(0, 2, 1).reshape(d, aw)
    return _mm(
        name, (s // tm, 1, 1), dyb, pl.BlockSpec((tm, d), lambda i, j, k: (i, 0)),
        wb_t, pl.BlockSpec((d, aw), lambda i, j, k: (0, 0)), NN,
        extras, extra_specs, out_shape, out_specs, epilogue, None, scratch=scratch)


def _attn_out_bwd(dya, wba_st, o_attn):
    s, aw = o_attn.shape
    tm = min(s, 512)

    def epilogue(acc_ref, ex, outs, sdo, sdl):
        do_ref, dl_ref, do4_ref, do16_ref, dl4_ref, dl16_ref = outs
        deltas = []
        for hh in range(aw // HEAD_DIM):
            sl = slice(hh * HEAD_DIM, (hh + 1) * HEAD_DIM)
            do = acc_ref[:, sl]
            deltas.append(jnp.sum(do * ex[0][:, sl].astype(F32), axis=-1, keepdims=True))
            sdo[hh] = do
            do_ref[:, sl] = do.astype(BF16)
        packed = _pack_heads(deltas)
        sdl[0] = packed
        dl_ref[...] = packed
        _to_views(sdo, (do4_ref, do16_ref), BF16)
        _to_views(sdl, (dl4_ref, dl16_ref), F32)

    def specs(width):
        return ([pl.BlockSpec((tm, width), lambda i, j, k: (i, 0))]
                + [pl.BlockSpec((tm // dil, dil * width), lambda i, j, k: (i, 0)) for dil in DILATIONS[1:]])

    def shapes(width, dtype):
        return ([jax.ShapeDtypeStruct((s, width), dtype)]
                + [jax.ShapeDtypeStruct(_view_shape(s, width, dil), dtype) for dil in DILATIONS[1:]])

    do_specs, dl_specs = specs(aw), specs(HEAD_DIM)
    do_shapes, dl_shapes = shapes(aw, BF16), shapes(HEAD_DIM, F32)
    return _branch_in_bwd(
        "attn_out_bwd", tm, dya, wba_st, epilogue, [o_attn], [do_specs[0]],
        [do_shapes[0], dl_shapes[0], *do_shapes[1:], *dl_shapes[1:]],
        [do_specs[0], dl_specs[0], *do_specs[1:], *dl_specs[1:]],
        scratch=[_heads_scratch(tm, aw), _heads_scratch(tm, HEAD_DIM)])


def _pool_out_bwd(dyp, wbp_st, pw, scale):
    s, pw_ = pw.shape
    tm = min(s, 1024)

    def epilogue(acc_ref, ex, outs):
        pw_ref, sc_ref = ex
        dpw_ref, st_ref = outs
        acc = acc_ref[...]
        dpw_ref[...] = (acc * sc_ref[...]).astype(BF16)
        st_ref[...] = _stats_rows([jnp.sum(acc * pw_ref[...].astype(F32), axis=0, keepdims=True)], pw_)

    row = pl.BlockSpec((tm, pw_), lambda i, j, k: (i, 0))
    return _branch_in_bwd(
        "pool_out_bwd", tm, dyp, wbp_st, epilogue, [pw, scale],
        [row, pl.BlockSpec((1, pw_), lambda i, j, k: (0, 0))],
        [jax.ShapeDtypeStruct((s, pw_), BF16), jax.ShapeDtypeStruct((s // tm, 8, pw_), F32)],
        [row, pl.BlockSpec((None, 8, pw_), lambda i, j, k: (i, 0, 0))])


def _pool_bwd(dpw, p, wp):
    s, pw_ = p.shape
    ng = len(POOL_WINDOWS)
    pgw = pw_ // ng
    tm = min(s, 512)
    hb = tm // POOL_HALO
    nblk = s // tm

    def body(dpw_ref, nxt_ref, p_ref, wp_ref, dwp_ref, du_ref):
        i = pl.program_id(0)
        nxt = (nxt_ref[...].astype(F32) * jnp.where(i < nblk - 1, 1.0, 0.0)).astype(BF16)
        dpw_all = jnp.concatenate([dpw_ref[...], nxt], axis=0)

        @pl.when(i == 0)
        def _():
            dwp_ref[...] = jnp.zeros_like(dwp_ref)

        dps = []
        for g in range(ng):
            sl = slice(g * pgw, (g + 1) * pgw)
            dwp_ref[g] += _dot(p_ref[:, sl], dpw_ref[:, sl], TN)
            dps.append(_dot(dpw_all[:, sl], wp_ref[g], NT))
        dp = jnp.concatenate(dps, axis=1)
        dpn = dp / _pool_counts(tm, tm + POOL_HALO, pgw, i * tm)
        du_ref[...] = (_window_sums(dpn, -1, pgw)[:tm] - dp[:tm]).astype(BF16)

    row = pl.BlockSpec((tm, pw_), lambda i: (i, 0))
    full = pl.BlockSpec((ng, pgw, pgw), lambda i: (0, 0, 0))
    return pl.pallas_call(
        body, name="pool_bwd", grid=(nblk,),
        in_specs=[row, pl.BlockSpec((POOL_HALO, pw_), lambda i: (jnp.minimum((i + 1) * hb, s // POOL_HALO - 1), 0)),
                  row, full],
        out_specs=[full, row],
        out_shape=[jax.ShapeDtypeStruct((ng, pgw, pgw), F32), jax.ShapeDtypeStruct((s, pw_), BF16)],
        compiler_params=_params(("arbitrary",)),
    )(dpw, dpw, p, wp)


def _attn_bwd(name, q, k, v, do, lse, delta, offs, cw, width):
    m = do.shape[0]
    nh = cw // HEAD_DIM
    rb = min(m, ATTN_STEP_ROWS)
    nsub = rb // SUB_BLOCK
    nstep = m // rb
    single = nstep == 1
    grid = (width // cw, nstep + (not single))
    scale = HEAD_DIM ** -0.5
    last = slice(rb - SUB_BLOCK, rb)

    def body(q_ref, k_ref, v_ref, do_ref, lse_ref, dl_ref, dq_ref, dk_ref, dv_ref,
             kprev, vprev, dk_carry, dv_carry):
        n = pl.program_id(1)

        @pl.when(n == 0)
        def _():
            for ref in (kprev, vprev, dk_carry, dv_carry):
                ref[...] = jnp.zeros_like(ref)

        @pl.when(n < nstep)
        def _():
            qi = lax.broadcasted_iota(jnp.int32, (SUB_BLOCK, 2 * SUB_BLOCK), 0)
            kj = lax.broadcasted_iota(jnp.int32, (SUB_BLOCK, 2 * SUB_BLOCK), 1)
            band = (kj >= qi) & (kj <= qi + SUB_BLOCK)
            valids = [band & (kj >= jnp.where(n == 0, SUB_BLOCK, 0))] + [band] * (nsub - 1)
            heads = [slice(hh * HEAD_DIM, (hh + 1) * HEAD_DIM) for hh in range(nh)]

            def rows(sub):
                return slice(sub * SUB_BLOCK, (sub + 1) * SUB_BLOCK)

            def cat(prev, ref, sub, sl):
                if sub == 0:
                    return jnp.concatenate([prev[:, sl], ref[rows(0), sl]], axis=0)
                return ref[(sub - 1) * SUB_BLOCK:(sub + 1) * SUB_BLOCK, sl]

            kcats = [[cat(kprev, k_ref, sub, sl) for sl in heads] for sub in range(nsub)]
            scs = [[_dot(q_ref[rows(sub), sl], kcats[sub][hh], NT) for hh, sl in enumerate(heads)]
                   for sub in range(nsub)]
            dps = [[_dot(do_ref[rows(sub), sl], cat(vprev, v_ref, sub, sl), NT) for sl in heads]
                   for sub in range(nsub)]
            stats = [(lse_ref[rows(sub), :], dl_ref[rows(sub), :]) for sub in range(nsub)]
            for hh, sl in enumerate(heads):
                dk2, dv2 = [], []
                for sub in range(nsub):
                    lse_h, dl_h = _head_col(stats[sub][0], hh, nh), _head_col(stats[sub][1], hh, nh)
                    valid = valids[sub]
                    p = jnp.where(valid, jnp.exp(jnp.where(valid, scs[sub][hh] * scale, NEG) - lse_h), 0.0)
                    ds = (p * (dps[sub][hh] - dl_h)).astype(BF16)
                    dq_ref[rows(sub), sl] = (_dot(ds, kcats[sub][hh], NN) * scale).astype(BF16)
                    dk2.append(_dot(ds, q_ref[rows(sub), sl], TN) * scale)
                    dv2.append(_dot(p.astype(BF16), do_ref[rows(sub), sl], TN))
                for out_ref, carry, new in ((dk_ref, dk_carry, dk2), (dv_ref, dv_carry, dv2)):
                    if not single:
                        out_ref[last, sl] = (carry[last, sl] + new[0][:SUB_BLOCK]).astype(BF16)
                        if nsub > 1:
                            out_ref[:rb - SUB_BLOCK, sl] = carry[:rb - SUB_BLOCK, sl].astype(BF16)
                    for sub in range(nsub):
                        val = new[sub][SUB_BLOCK:]
                        if sub + 1 < nsub:
                            val = val + new[sub + 1][:SUB_BLOCK]
                        if single:
                            out_ref[rows(sub), sl] = val.astype(BF16)
                        else:
                            carry[rows(sub), sl] = val
            kprev[...] = k_ref[last, :]
            vprev[...] = v_ref[last, :]

        if not single:
            @pl.when(n == nstep)
            def _():
                dk_ref[...] = dk_carry[...].astype(BF16)
                dv_ref[...] = dv_carry[...].astype(BF16)

    def cur(off):
        return pl.BlockSpec((rb, cw), lambda j, n: (jnp.minimum(n, nstep - 1), j + off))

    lagged = pl.BlockSpec((rb, cw), lambda j, n: (jnp.maximum(n - 1, 0), j))
    stat = pl.BlockSpec((rb, HEAD_DIM), lambda j, n: (jnp.minimum(n, nstep - 1), j))
    out = jax.ShapeDtypeStruct((m, width), BF16)
    return pl.pallas_call(
        body, name=name, grid=grid,
        in_specs=[cur(offs[0]), cur(offs[1]), cur(offs[2]), cur(0), stat, stat],
        out_specs=[cur(0), lagged, lagged], out_shape=[out, out, out],
        scratch_shapes=[pltpu.VMEM((SUB_BLOCK, cw), BF16)] * 2 + [pltpu.VMEM((rb, cw), F32)] * 2,
        compiler_params=_params(("parallel", "arbitrary")),
    )(q, k, v, do, lse, delta)


def _qkvu_grad(d1, d4, d16, du, cos_t, sin_t, dh):
    s, aw = du.shape
    tm = min(s, 512)

    def body(*refs):
        nat, v4, v16 = refs[0:3], refs[3:6], refs[6:9]
        cos_ref, sin_ref, du_ref, _, out_ref, s4, s16 = refs[9:]
        cos, sin = cos_ref[...], sin_ref[...]
        for part in range(3):
            _from_views((v4[part], v16[part]), (s4, s16))
            for hh in range(aw // HEAD_DIM):
                sl = slice(hh * HEAD_DIM, (hh + 1) * HEAD_DIM)
                t = nat[part][:, sl].astype(F32) + s4[hh] + s16[hh]
                if part < 2:
                    t = t * cos - pltpu.roll(t, HEAD_DIM // 2, 1) * sin
                out_ref[:, part * aw + hh * HEAD_DIM:part * aw + (hh + 1) * HEAD_DIM] = t.astype(BF16)
        out_ref[:, 3 * aw:] = du_ref[...]

    row = pl.BlockSpec((tm, aw), lambda i: (i, 0))
    tab = pl.BlockSpec((tm, HEAD_DIM), lambda i: (i, 0))
    return pl.pallas_call(
        body, name="qkvu_grad", grid=(s // tm,),
        in_specs=[row] * 3 + [_view_spec(tm, aw, 4)] * 3 + [_view_spec(tm, aw, 16)] * 3
        + [tab, tab, row, ANY_SPEC],
        out_specs=pl.BlockSpec((tm, 4 * aw), lambda i: (i, 0)),
        out_shape=jax.ShapeDtypeStruct(dh.shape, BF16), input_output_aliases={12: 0},
        scratch_shapes=[_heads_scratch(tm, aw)] * 2,
        compiler_params=_params(("parallel",)),
    )(*d1, *d4, *d16, cos_t, sin_t, du, dh)


def _in_proj_bwd_x(name, dh, w_in_st, base, scale_base, run_after=()):
    s, kdim = dh.shape
    d, n_sh = w_in_st.shape[1], w_in_st.shape[2]
    tm, tk = min(s, 512), min(n_sh, 2048)
    per = n_sh // tk

    ch = min(tm, 2 * EPILOGUE_ROWS)

    def epilogue(acc_ref, ex, outs):
        for c in range(tm // ch):
            rows = slice(c * ch, (c + 1) * ch)
            outs[0][rows, :] = scale_base * ex[0][rows, :] + acc_ref[rows, :]

    row = pl.BlockSpec((tm, d), lambda i, j, k: (i, 0))
    return _mm(
        name, (s // tm, 1, kdim // tk), dh, pl.BlockSpec((tm, tk), lambda i, j, k: (i, k)),
        w_in_st, pl.BlockSpec((None, d, tk), lambda i, j, k: (k // per, 0, k % per)), NT,
        [base], [row], [jax.ShapeDtypeStruct((s, d), F32)], [row], epilogue, (tm, d),
        acc_as_ref=True, run_after=run_after)[0]


def _chip_peers():
    x, y, c = lax.axis_index("x"), lax.axis_index("y"), lax.axis_index("c")
    return x, y, c, [(1 - x, y), (x, 1 - y), (1 - x, 1 - y)]


GATHER, GATHER_HALF, SCATTER, SIBLING = "gather", "gather_half", "scatter", "sibling"


def _exchange_peers(mode):
    x, y, c, chips = _chip_peers()
    if mode == SIBLING:
        return x, y, c, [(x, y, 1 - c)]
    return x, y, c, [(px, py, c) for px, py in chips]


def _core_half(ref_or_shape, c):
    rows = (ref_or_shape.shape[0]) // 2
    return pl.ds(c * rows, rows)


def _exchange_descriptor(mode, src, land, send, recv, p, peer, me, arriving):
    pid = 2 * peer[0] + peer[1]
    if mode == GATHER:
        src_ref, dst_ref = src, land.at[pid if arriving else me]
    elif mode == GATHER_HALF:
        rows = _core_half(src, peer[2])
        src_ref, dst_ref = src.at[rows], land.at[pid if arriving else me, rows]
    elif mode == SCATTER:
        src_ref, dst_ref = src.at[pid], land.at[p]
    else:
        src_ref, dst_ref = src, land
    return pltpu.make_async_remote_copy(
        src_ref=src_ref, dst_ref=dst_ref, send_sem=send.at[p], recv_sem=recv.at[p],
        device_id=peer, device_id_type=MESH)


def _exchange_start(name, mode, srcs, land_shapes):
    n = len(srcs)
    lands = [_hbm(lax.empty(shape, src.dtype)) for shape, src in zip(land_shapes, srcs)]

    def body(*refs):
        src_refs, land_refs = refs[:n], refs[n:2 * n]
        sends, recvs = refs[2 * n:3 * n], refs[3 * n:4 * n]
        token = refs[6 * n]
        x, y, c, peers = _exchange_peers(mode)
        me = 2 * x + y
        for w in range(n):
            for p, peer in enumerate(peers):
                _exchange_descriptor(mode, src_refs[w], land_refs[w], sends[w], recvs[w], p, peer,
                                     me, arriving=False).start()
        token[...] = jnp.zeros_like(token)

    sem = pltpu.SemaphoreType.DMA((3,))
    outs = pl.pallas_call(
        body, name=name, in_specs=[HBM_SPEC] * (2 * n),
        out_specs=[SEM_SPEC] * (2 * n) + [HBM_SPEC] * (2 * n) + [pl.BlockSpec(memory_space=pltpu.VMEM)],
        out_shape=[sem] * (2 * n) + [pltpu.HBM(a.shape, a.dtype) for a in (*srcs, *lands)]
        + [jax.ShapeDtypeStruct((8, 128), F32)],
        input_output_aliases={i: 2 * n + i for i in range(2 * n)},
        compiler_params=pltpu.CompilerParams(has_side_effects=DATAFLOW),
    )(*[_hbm(a) for a in srcs], *lands)
    return {"send": outs[:n], "recv": outs[n:2 * n], "src": outs[2 * n:3 * n],
            "land": outs[3 * n:4 * n], "token": outs[4 * n]}


def _exchange_wait(name, mode, started, which, after):
    m = len(which)

    def body(*refs):
        src_refs, land_refs = refs[:m], refs[m:2 * m]
        sends, recvs = refs[2 * m:3 * m], refs[3 * m:4 * m]
        x, y, c, peers = _exchange_peers(mode)
        me = 2 * x + y
        for w in range(m):
            for p, peer in enumerate(peers):
                _exchange_descriptor(mode, src_refs[w], land_refs[w], sends[w], recvs[w], p, peer,
                                     me, arriving=False).wait_send()
                _exchange_descriptor(mode, src_refs[w], land_refs[w], sends[w], recvs[w], p, peer,
                                     me, arriving=True).wait_recv()

    pick = lambda key: [started[key][w] for w in which]
    bufs = pick("src") + pick("land")
    after = list(after) if isinstance(after, (list, tuple)) else [after]
    outs = pl.pallas_call(
        body, name=name,
        in_specs=[HBM_SPEC] * (2 * m) + [SEM_SPEC] * (2 * m) + [ANY_SPEC] * len(after),
        out_specs=[HBM_SPEC] * (2 * m), out_shape=[pltpu.HBM(a.shape, a.dtype) for a in bufs],
        input_output_aliases={i: i for i in range(2 * m)},
        compiler_params=pltpu.CompilerParams(has_side_effects=DATAFLOW),
    )(*bufs, *pick("send"), *pick("recv"), *after)
    return outs[:m], outs[m:]


def _to_bf16(name, a, run_after):
    r, c = a.shape
    tm = min(r, 512)

    def body(a_ref, after_ref, out_ref):
        out_ref[...] = a_ref[...].astype(BF16)

    blk = pl.BlockSpec((tm, c), lambda i: (i, 0))
    return pl.pallas_call(
        body, name=name, grid=(r // tm,), in_specs=[blk, ANY_SPEC], out_specs=blk,
        out_shape=jax.ShapeDtypeStruct((r, c), BF16), compiler_params=_params(("parallel",)),
    )(a, run_after)


def _swap_halves(name, lands):
    n = len(lands)

    def body(*refs):
        bufs = refs[n:2 * n]
        send, recv = refs[2 * n:]
        x, y, c, chips = _chip_peers()
        started = []
        for w in range(n):
            half = bufs[w].shape[1] // 2
            for p, (px, py) in enumerate(chips):
                mine = bufs[w].at[2 * px + py, pl.ds(c * half, half)]
                cp = pltpu.make_async_remote_copy(
                    src_ref=mine, dst_ref=mine, send_sem=send.at[w, p], recv_sem=recv.at[w, p],
                    device_id=(x, y, 1 - c), device_id_type=MESH)
                cp.start()
                started.append(cp)
        for w in range(n):
            half = bufs[w].shape[1] // 2
            for p, (px, py) in enumerate(chips):
                theirs = bufs[w].at[2 * px + py, pl.ds((1 - c) * half, half)]
                pltpu.make_async_remote_copy(
                    src_ref=theirs, dst_ref=theirs, send_sem=send.at[w, p], recv_sem=recv.at[w, p],
                    device_id=(x, y, 1 - c), device_id_type=MESH).wait_recv()
        for cp in started:
            cp.wait_send()

    return pl.pallas_call(
        body, name=name, in_specs=[ANY_SPEC] * n, out_specs=[ANY_SPEC] * n,
        out_shape=[jax.ShapeDtypeStruct(a.shape, a.dtype) for a in lands],
        input_output_aliases={i: i for i in range(n)},
        scratch_shapes=[pltpu.SemaphoreType.DMA((n, 3)), pltpu.SemaphoreType.DMA((n, 3))],
    )(*lands)


def _place_own(name, shard, land, me):
    r, c = shard.shape
    tm = min(r, 512)

    def body(me_ref, shard_ref, land_ref, out_ref):
        out_ref[...] = shard_ref[...]

    return pl.pallas_call(
        body, name=name,
        grid_spec=pltpu.PrefetchScalarGridSpec(
            num_scalar_prefetch=1, grid=(r // tm,),
            in_specs=[pl.BlockSpec((tm, c), lambda i, me_ref: (i, 0)), ANY_SPEC],
            out_specs=pl.BlockSpec((None, tm, c), lambda i, me_ref: (me_ref[0], i, 0))),
        out_shape=jax.ShapeDtypeStruct(land.shape, land.dtype), input_output_aliases={2: 0},
        compiler_params=_params(("arbitrary",)),
    )(me, shard, land)


def _sum_slabs(name, grads, land, me):
    _, r, c = grads.shape
    tm = min(r, 256)

    def body(me_ref, own_ref, land_ref, out_ref):
        acc = own_ref[...].astype(F32)
        for p in range(3):
            acc = acc + land_ref[p].astype(F32)
        out_ref[...] = acc

    return pl.pallas_call(
        body, name=name,
        grid_spec=pltpu.PrefetchScalarGridSpec(
            num_scalar_prefetch=1, grid=(r // tm,),
            in_specs=[pl.BlockSpec((None, tm, c), lambda i, me_ref: (me_ref[0], i, 0)),
                      pl.BlockSpec((3, tm, c), lambda i, me_ref: (0, i, 0))],
            out_specs=pl.BlockSpec((tm, c), lambda i, me_ref: (i, 0))),
        out_shape=jax.ShapeDtypeStruct((r, c), F32), compiler_params=_params(("parallel",)),
    )(me, grads, land)


def _allreduce_stats(stats, run_after):
    n = len(stats)

    def body(*refs):
        ins, outs = refs[:n], refs[n + 1:2 * n + 1]
        mine, gath = refs[2 * n + 1:3 * n + 1], refs[3 * n + 1:4 * n + 1]
        send, recv = refs[4 * n + 1:]
        x, y, c = lax.axis_index("x"), lax.axis_index("y"), lax.axis_index("c")
        me = 4 * x + 2 * y + c
        flips = [(bx, by, bc) for bx in (0, 1) for by in (0, 1) for bc in (0, 1)][1:]

        def peer(f):
            return (x + f[0] * (1 - 2 * x), y + f[1] * (1 - 2 * y), c + f[2] * (1 - 2 * c))

        copies = []
        for t in range(n):
            tot = ins[t][0]
            for b in range(1, ins[t].shape[0]):
                tot = tot + ins[t][b]
            mine[t][...] = tot
            gath[t][me] = tot
            for k, f in enumerate(flips):
                cp = pltpu.make_async_remote_copy(
                    src_ref=mine[t], dst_ref=gath[t].at[me], send_sem=send.at[t, k],
                    recv_sem=recv.at[t, k], device_id=peer(f), device_id_type=MESH)
                cp.start()
                copies.append(cp)
        for t in range(n):
            for k, f in enumerate(flips):
                px, py, pc = peer(f)
                pltpu.make_async_remote_copy(
                    src_ref=mine[t], dst_ref=gath[t].at[4 * px + 2 * py + pc], send_sem=send.at[t, k],
                    recv_sem=recv.at[t, k], device_id=(px, py, pc), device_id_type=MESH).wait_recv()
        for cp in copies:
            cp.wait_send()
        for t in range(n):
            tot = gath[t][0]
            for dev in range(1, 8):
                tot = tot + gath[t][dev]
            outs[t][...] = tot

    vm = pl.BlockSpec(memory_space=pltpu.VMEM)
    return pl.pallas_call(
        body, name="allreduce_stats", in_specs=[vm] * n + [ANY_SPEC], out_specs=[vm] * n,
        out_shape=[jax.ShapeDtypeStruct(s.shape[1:], F32) for s in stats],
        scratch_shapes=[pltpu.VMEM(s.shape[1:], F32) for s in stats]
        + [pltpu.VMEM((8, *s.shape[1:]), F32) for s in stats]
        + [pltpu.SemaphoreType.DMA((n, 7)), pltpu.SemaphoreType.DMA((n, 7))],
    )(*stats, run_after)


def _adamw(name, w, m, v, g_parts):
    r, c = w.shape
    tm = min(r, 256)
    n_g = len(g_parts)

    def body(*refs):
        w_ref, m_ref, v_ref = refs[:3]
        g_refs = refs[3:3 + n_g]
        g_out, d_out, m_out, v_out = refs[3 + n_g:]
        g = g_refs[0][...]
        for gr in g_refs[1:]:
            g = g + gr[...]
        m_new = ADAM_B1 * m_ref[...] + (1.0 - ADAM_B1) * g
        v_new = ADAM_B2 * v_ref[...] + (1.0 - ADAM_B2) * (g * g)
        m_hat = m_new / (1.0 - ADAM_B1 ** ADAM_STEP)
        v_hat = v_new / (1.0 - ADAM_B2 ** ADAM_STEP)
        g_out[...] = g
        d_out[...] = -ADAM_LR * (m_hat / (jnp.sqrt(v_hat) + ADAM_EPS) + ADAM_WD * w_ref[...])
        m_out[...] = m_new
        v_out[...] = v_new

    blk = pl.BlockSpec((tm, c), lambda i: (i, 0))
    out = jax.ShapeDtypeStruct((r, c), F32)
    return pl.pallas_call(
        body, name=name, grid=(r // tm,), in_specs=[blk] * (3 + n_g), out_specs=[blk] * 4,
        out_shape=[out] * 4, compiler_params=_params(("parallel",)),
    )(w, m, v, *g_parts)


def _rope_tables(positions):
    half = HEAD_DIM // 2
    inv_freq = ROPE_THETA ** (-jnp.arange(half, dtype=F32) / half)
    ang = positions.astype(F32)[0, :, None] * inv_freq
    cos, sin = jnp.cos(ang), jnp.sin(ang)
    return jnp.concatenate([cos, cos], axis=-1), jnp.concatenate([-sin, sin], axis=-1)


def kernel(x, positions, w_in, w_pool, pool_scale, w_branch_attn, w_branch_pool, w_out, ln_mix_g, ln_mix_b, w_ff1, w_ff2, ln_ff_g, ln_ff_b, loss_target, m_w_in, m_w_pool, m_pool_scale, m_w_branch_attn, m_w_branch_pool, m_w_out, m_ln_mix_g, m_ln_mix_b, m_w_ff1, m_w_ff2, m_ln_ff_g, m_ln_ff_b, v_w_in, v_w_pool, v_pool_scale, v_w_branch_attn, v_w_branch_pool, v_w_out, v_ln_mix_g, v_ln_mix_b, v_w_ff1, v_w_ff2, v_ln_ff_g, v_ln_ff_b):
    s, d = x.shape[1], x.shape[2]
    aw = d // 2
    ng = len(POOL_WINDOWS)
    pgw = aw // ng
    x2d, target = x[0], loss_target[0]
    xb = x2d.astype(BF16)
    cos_t, sin_t = _rope_tables(positions)

    big = {"w_in": w_in[0], "w_pool": w_pool[0].reshape(-1, pgw), "w_branch_attn": w_branch_attn[0],
           "w_branch_pool": w_branch_pool[0], "w_out": w_out[0], "w_ff1": w_ff1[0], "w_ff2": w_ff2[0]}
    names = list(big)
    me_chip = (2 * lax.axis_index("x") + lax.axis_index("y")).astype(jnp.int32).reshape(1)
    land_shapes = [(N_CHIPS, *big[k].shape) for k in names]
    gathering_in = _exchange_start("gather_start_in", GATHER_HALF,
                                   [_to_bf16("to_bf16_w_in", big["w_in"], positions)], land_shapes[:1])
    shards = [_to_bf16(f"to_bf16_{k}", big[k], gathering_in["token"]) for k in names[1:]]
    gathering = _exchange_start("gather_start", GATHER, shards, land_shapes[1:])

    def gathered(name, which, after):
        srcs, lands = _exchange_wait(f"gather_wait_{name}", GATHER, gathering, which, after)
        return [_place_own(f"place_own_{names[w + 1]}", srcs[i], lands[i], me_chip)
                for i, w in enumerate(which)]

    rows_sh = pgw // N_CHIPS
    dff = N_CHIPS * big["w_ff2"].shape[0]

    srcs, lands = _exchange_wait("gather_wait_in", GATHER_HALF, gathering_in, [0],
                                 [gathering["token"], xb, cos_t, sin_t])
    w_in_st = _place_own("place_own_w_in", srcs[0], _swap_halves("swap_halves_in", lands)[0], me_chip)
    h, hv4, hv16 = _in_proj(xb, w_in_st, cos_t, sin_t, aw)
    (wp_st,) = gathered("pool", [0], h)
    wp = wp_st.reshape(N_CHIPS, ng, rows_sh, pgw).transpose(1, 0, 2, 3).reshape(ng, pgw, pgw)
    qkv = {1: (h, h, h), 4: (hv4, hv4, hv4), 16: (hv16, hv16, hv16)}
    offs = {dil: (0, dil, 2 * dil) for dil in DILATIONS}
    o_parts, lse_parts = [], []
    for dil in DILATIONS:
        o_p, lse_p = _attn_fwd(f"attn_fwd_d{dil}", *qkv[dil], offs[dil], aw, dil * aw)
        o_parts.append(o_p)
        lse_parts.append(lse_p)
    o_attn, lse, lse4, lse16 = _attn_combine(o_parts[0], lse_parts[0], o_parts[1:], lse_parts[1:])
    p, pw, y = _pool_fwd(h, wp, pool_scale, aw)
    wba_st, wbp_st, w_out_st = gathered("mix", [1, 2, 3], y)
    w_out_full = w_out_st.reshape(d, d)
    ya, yp, merged = _branch_merge(o_attn, y, wba_st, wbp_st, h, aw, d)
    xhat1, rstd1, x1b = _mix_norm(merged, w_out_full, x2d, ln_mix_g, ln_mix_b)
    w1_st, w2_st = gathered("ff", [4, 5], x1b)
    w2_full = w2_st.reshape(dff, d)
    r, r_slope = _ff_up(x1b, w1_st)
    dz2, dz2b, st2 = _ff_down_loss(r, w2_full, xhat1, ln_mix_g, ln_mix_b, ln_ff_g, ln_ff_b, target)

    def scatter_start(name, grads):
        return _exchange_start(f"scatter_start_{name}", SCATTER, grads, [(3, *g.shape[1:]) for g in grads])

    da = _ff_down_bwd(dz2b, w2_full, r_slope)
    g_w2 = _wgrad("wgrad_ff2", r, dz2b, d).reshape(N_CHIPS, dff // N_CHIPS, d)
    g_w1 = _wgrad("wgrad_ff1", x1b, da, dff // N_CHIPS)
    sent_ff = scatter_start("ff", [g_w1, g_w2])
    dz1, dz1b, st1 = _ff_up_bwd(da, w1_st, dz2, xhat1, rstd1, ln_mix_g, [sent_ff["token"]])
    dya, dyp, dh_gates = _mix_bwd(dz1b, w_out_full, h, ya, yp, aw)
    g_wout = _wgrad("wgrad_out", merged, dz1b, d).reshape(N_CHIPS, d // N_CHIPS, d)
    g_wba = _wgrad("wgrad_branch_attn", o_attn, dya, d // N_CHIPS)
    g_wbp = _wgrad("wgrad_branch_pool", y, dyp, d // N_CHIPS)
    do, delta, do4, do16, delta4, delta16 = _attn_out_bwd(dya, wba_st, o_attn)
    dpw, stp = _pool_out_bwd(dyp, wbp_st, pw, pool_scale)
    dwp, du = _pool_bwd(dpw, p, wp)
    g_wp = dwp.reshape(ng, N_CHIPS, rows_sh, pgw).transpose(1, 0, 2, 3).reshape(
        N_CHIPS, ng * rows_sh, pgw).astype(BF16)
    sent_mix = scatter_start("mix", [g_wp, g_wba, g_wbp, g_wout])

    bwd_in = {1: (do, lse, delta), 4: (do4, lse4, delta4), 16: (do16, lse16, delta16)}
    dqkv = {}
    for dil in DILATIONS:
        args = (*qkv[dil], *bwd_in[dil], offs[dil], aw, dil * aw)
        dqkv[dil] = _attn_bwd(f"attn_bwd_d{dil}", *args)
    dh = _qkvu_grad(dqkv[1], dqkv[4], dqkv[16], du, cos_t, sin_t, dh_gates)
    g_win = _wgrad("wgrad_in", xb, dh, d)
    sent_in = scatter_start("in", [g_win])
    grad_x = _in_proj_bwd_x("in_proj_bwd", dh, w_in_st, dz1, ALPHA, [sent_mix["token"], sent_in["token"]])

    moments = {"w_in": (m_w_in, v_w_in), "w_pool": (m_w_pool, v_w_pool),
               "w_branch_attn": (m_w_branch_attn, v_w_branch_attn),
               "w_branch_pool": (m_w_branch_pool, v_w_branch_pool), "w_out": (m_w_out, v_w_out),
               "w_ff1": (m_w_ff1, v_w_ff1), "w_ff2": (m_w_ff2, v_w_ff2)}
    originals = {"w_in": w_in, "w_pool": w_pool, "w_branch_attn": w_branch_attn,
                 "w_branch_pool": w_branch_pool, "w_out": w_out, "w_ff1": w_ff1, "w_ff2": w_ff2}
    res = {}

    def summed(name, sent, keys, after):
        srcs, lands = _exchange_wait(f"scatter_wait_{name}", SCATTER, sent, list(range(len(keys))), after)
        parts = [_sum_slabs(f"sum_slabs_{k}", srcs[i], lands[i], me_chip) for i, k in enumerate(keys)]
        return _exchange_start(f"cores_start_{name}", SIBLING, parts, [a.shape for a in parts])

    def updated(name, swapping, keys, after):
        mine, other = _exchange_wait(f"cores_wait_{name}", SIBLING, swapping, list(range(len(keys))), after)
        for i, k in enumerate(keys):
            mk, vk = (a.reshape(big[k].shape) for a in moments[k])
            outs = _adamw(f"adamw_{k}", big[k], mk, vk, [mine[i], other[i]])
            res[k] = [o.reshape(originals[k].shape) for o in outs]

    groups = {"ff": ["w_ff1", "w_ff2"], "mix": ["w_pool", "w_branch_attn", "w_branch_pool", "w_out"],
              "in": ["w_in"]}
    swap_ff = summed("ff", sent_ff, groups["ff"], grad_x)
    swap_mix = summed("mix", sent_mix, groups["mix"], swap_ff["token"])
    swap_in = summed("in", sent_in, groups["in"], swap_mix["token"])
    updated("ff", swap_ff, groups["ff"], swap_in["token"])
    updated("mix", swap_mix, groups["mix"], res["w_ff2"][0])
    updated("in", swap_in, groups["in"], res["w_out"][0])
    tot2, tot1, totp = _allreduce_stats([st2, st1, stp], res["w_in"][0])

    def pad_d(a):
        return jnp.pad(a, ((0, 0), (0, d - a.shape[1])))

    small = ["ln_mix_g", "ln_mix_b", "ln_ff_g", "ln_ff_b", "pool_scale"]
    small_w = {"ln_mix_g": ln_mix_g, "ln_mix_b": ln_mix_b, "ln_ff_g": ln_ff_g, "ln_ff_b": ln_ff_b,
               "pool_scale": pool_scale}
    small_m = {"ln_mix_g": m_ln_mix_g, "ln_mix_b": m_ln_mix_b, "ln_ff_g": m_ln_ff_g,
               "ln_ff_b": m_ln_ff_b, "pool_scale": m_pool_scale}
    small_v = {"ln_mix_g": v_ln_mix_g, "ln_mix_b": v_ln_mix_b, "ln_ff_g": v_ln_ff_g,
               "ln_ff_b": v_ln_ff_b, "pool_scale": v_pool_scale}
    small_g = [tot1[0:1], tot1[1:2], tot2[0:1], tot2[1:2], pad_d(totp[0:1])]

    def pack(rows):
        return jnp.concatenate([pad_d(a) for a in rows] + [jnp.zeros((8 - len(rows), d), F32)], axis=0)

    outs = _adamw("adamw_small", pack([small_w[k] for k in small]), pack([small_m[k] for k in small]),
                  pack([small_v[k] for k in small]), [pack(small_g)])
    for i, k in enumerate(small):
        res[k] = [o[i:i + 1, :small_w[k].shape[1]] for o in outs]
    loss = tot2[2, 0]

    order = ["w_in", "w_pool", "pool_scale", "w_branch_attn", "w_branch_pool", "w_out", "ln_mix_g",
             "ln_mix_b", "w_ff1", "w_ff2", "ln_ff_g", "ln_ff_b"]
    result = [loss, grad_x[None]]
    for idx in range(4):
        result += [res[k][idx] for k in order]
    return tuple(result)
```

```python
import jax
import jax.numpy as jnp
from jax import lax
from jax.experimental import pallas as pl
from jax.experimental.pallas import tpu as pltpu

F32 = jnp.float32
BF16 = jnp.bfloat16
MESH = pl.DeviceIdType.MESH

HEAD_DIM = 128
SUB_BLOCK = 128
ATTN_STEP_ROWS = 512
DILATIONS = (1, 4, 16)
POOL_WINDOWS = (2, 4, 8, 16)
POOL_HALO = 16
ROPE_THETA = 10000.0
LN_EPS = 1e-5
ALPHA = 2.0 ** 0.25
ADAM_LR, ADAM_B1, ADAM_B2, ADAM_EPS, ADAM_WD, ADAM_STEP = 0.001, 0.9, 0.999, 1e-08, 0.01, 10
NEG = -1e30
N_CHIPS = 4
VMEM_LIMIT = 62 * 1024 * 1024
EPILOGUE_ROWS = 128

def _params(sem=None, vmem=VMEM_LIMIT):
    kw = {"vmem_limit_bytes": vmem}
    if sem is not None:
        kw["dimension_semantics"] = sem
    return pltpu.CompilerParams(**kw)


def _dot(a, b, contract):
    return lax.dot_general(a, b, (contract, ((), ())), preferred_element_type=F32)


ANY_SPEC = pl.BlockSpec(memory_space=pl.ANY)
HBM_SPEC = pl.BlockSpec(memory_space=pltpu.HBM)
SEM_SPEC = pl.BlockSpec(memory_space=pltpu.SEMAPHORE)
DATAFLOW = pltpu.SideEffectType.DATAFLOW_SIDE_EFFECTING


def _hbm(a):
    return pltpu.with_memory_space_constraint(a, pltpu.HBM)


NN = ((1,), (0,))
NT = ((1,), (1,))
TN = ((0,), (0,))


def _mm(name, grid, a, a_spec, b, b_spec, contract, extras, extra_specs, out_shape, out_specs,
        epilogue, acc_shape, acc_as_ref=False, run_after=(), scratch=(),
        semantics=("parallel", "parallel", "arbitrary")):
    nk = grid[2]
    n_ex = len(extras)
    n_in = 2 + n_ex + len(run_after)
    n_out = len(out_shape)
    n_scr = len(scratch)

    def body(*refs):
        a_ref, b_ref = refs[0], refs[1]
        ex = refs[2:2 + n_ex]
        outs = refs[n_in:n_in + n_out]
        scr = refs[n_in + n_out:n_in + n_out + n_scr]
        if nk == 1:
            epilogue(_dot(a_ref[...], b_ref[...], contract), ex, outs, *scr)
        else:
            acc = refs[n_in + n_out + n_scr]
            k = pl.program_id(2)

            @pl.when(k == 0)
            def _():
                acc[...] = jnp.zeros_like(acc)

            acc[...] += _dot(a_ref[...], b_ref[...], contract)

            @pl.when(k == nk - 1)
            def _():
                epilogue(acc if acc_as_ref else acc[...], ex, outs, *scr)

    acc_scratch = [pltpu.VMEM(acc_shape, F32)] if nk > 1 else []
    return pl.pallas_call(
        body, name=name, grid=grid,
        in_specs=[a_spec, b_spec, *extra_specs, *[ANY_SPEC] * len(run_after)],
        out_specs=out_specs, out_shape=out_shape, scratch_shapes=[*scratch, *acc_scratch],
        compiler_params=_params(semantics),
    )(a, b, *extras, *run_after)


def _stats_rows(rows, width):
    idx = lax.broadcasted_iota(jnp.int32, (8, width), 0)
    out = jnp.zeros((8, width), F32)
    for r, v in enumerate(rows):
        out = jnp.where(idx == r, jnp.broadcast_to(v, (8, width)), out)
    return out


def _layer_norm_fwd(z):
    mu = jnp.mean(z, axis=-1, keepdims=True)
    zc = z - mu
    var = jnp.mean(zc * zc, axis=-1, keepdims=True)
    rstd = lax.rsqrt(var + LN_EPS)
    return zc * rstd, rstd


def _layer_norm_bwd(dy, xhat, rstd, g):
    dxh = dy * g
    m1 = jnp.mean(dxh, axis=-1, keepdims=True)
    m2 = jnp.mean(dxh * xhat, axis=-1, keepdims=True)
    return rstd * (dxh - m1 - xhat * m2)


def _heads_scratch(rows, width):
    return pltpu.VMEM((width // HEAD_DIM, rows, HEAD_DIM), F32)


def _to_views(src_ref, view_refs, dtype, heads=None):
    nh, rows, _ = src_ref.shape
    width = nh * HEAD_DIM
    for dil, view_ref in zip(DILATIONS[1:], view_refs):
        for r in range(dil):
            for hh in (range(nh) if heads is None else heads):
                c0 = r * width + hh * HEAD_DIM
                view_ref[:, c0:c0 + HEAD_DIM] = (
                    src_ref[hh, pl.ds(r, rows // dil, stride=dil), :].astype(dtype))


def _from_views(view_refs, dst_refs):
    nh, rows, _ = dst_refs[0].shape
    width = nh * HEAD_DIM
    for dil, view_ref, dst_ref in zip(DILATIONS[1:], view_refs, dst_refs):
        for r in range(dil):
            for hh in range(nh):
                c0 = r * width + hh * HEAD_DIM
                dst_ref[hh, pl.ds(r, rows // dil, stride=dil), :] = (
                    view_ref[:, c0:c0 + HEAD_DIM].astype(F32))


def _view_shape(rows, width, dil, parts=1):
    return (rows // dil, parts * dil * width)


def _in_proj(xb, w_in_st, cos_t, sin_t, aw):
    s, d = xb.shape
    n_sh = w_in_st.shape[2]
    tm, tn = min(s, 1024), aw
    per = n_sh // tn
    grid = (s // tm, (N_CHIPS * n_sh) // tn, 1)

    def epilogue(acc, ex, outs, scr):
        cos_ref, sin_ref = ex
        h_ref, v4_ref, v16_ref = outs
        seg = pl.program_id(1)

        heads = [slice(hh * HEAD_DIM, (hh + 1) * HEAD_DIM) for hh in range(tn // HEAD_DIM)]

        @pl.when(seg < 2)
        def _():
            cos, sin = cos_ref[...], sin_ref[...]
            for hh, sl in enumerate(heads):
                t = acc[:, sl]
                scr[hh] = t * cos + pltpu.roll(t, HEAD_DIM // 2, 1) * sin

        @pl.when(seg == 2)
        def _():
            for hh, sl in enumerate(heads):
                scr[hh] = acc[:, sl]

        @pl.when(seg < 3)
        def _():
            for hh, sl in enumerate(heads):
                h_ref[:, sl] = scr[hh].astype(BF16)
            _to_views(scr, (v4_ref, v16_ref), BF16)

        @pl.when(seg == 3)
        def _():
            h_ref[...] = acc.astype(BF16)

        @pl.when(seg >= 4)
        def _():
            h_ref[...] = (0.5 * jnp.tanh(0.5 * acc) + 0.5).astype(BF16)

    def view_spec(dil):
        return pl.BlockSpec((tm // dil, dil * aw), lambda i, j, k: (i, jnp.minimum(j, 2)))

    return _mm(
        "in_proj", grid, xb, pl.BlockSpec((tm, d), lambda i, j, k: (i, 0)),
        w_in_st, pl.BlockSpec((None, d, tn), lambda i, j, k: (j // per, 0, j % per)), NN,
        [cos_t, sin_t], [pl.BlockSpec((tm, HEAD_DIM), lambda i, j, k: (i, 0))] * 2,
        [jax.ShapeDtypeStruct((s, N_CHIPS * n_sh), BF16)]
        + [jax.ShapeDtypeStruct(_view_shape(s, aw, dil, 3), BF16) for dil in DILATIONS[1:]],
        [pl.BlockSpec((tm, tn), lambda i, j, k: (i, j))] + [view_spec(dil) for dil in DILATIONS[1:]],
        epilogue, None, scratch=[_heads_scratch(tm, aw)],
        semantics=("parallel", "arbitrary", "arbitrary"))


def _pack_heads(cols):
    rows, rep = cols[0].shape[0], HEAD_DIM // len(cols)
    lane = lax.broadcasted_iota(jnp.int32, (rows, HEAD_DIM), 1)
    out = jnp.zeros((rows, HEAD_DIM), F32)
    for hh, col in enumerate(cols):
        out = jnp.where((lane >= hh * rep) & (lane < (hh + 1) * rep), col, out)
    return out


def _head_col(packed, hh, nh):
    lane = lax.broadcasted_iota(jnp.int32, packed.shape, 1)
    return jnp.sum(jnp.where(lane == hh * (HEAD_DIM // nh), packed, 0.0), axis=-1, keepdims=True)


def _band_masks(block_idx):
    qi = lax.broadcasted_iota(jnp.int32, (SUB_BLOCK, 2 * SUB_BLOCK), 0)
    kj = lax.broadcasted_iota(jnp.int32, (SUB_BLOCK, 2 * SUB_BLOCK), 1)
    first_key = jnp.where(block_idx > 0, 0, SUB_BLOCK)
    return (kj >= qi) & (kj <= qi + SUB_BLOCK) & (kj >= first_key)


def _attn_fwd(name, q, k, v, offs, cw, width):
    m = q.shape[0]
    nh = cw // HEAD_DIM
    rb = min(m, ATTN_STEP_ROWS)
    nsub = rb // SUB_BLOCK
    grid = (width // cw, m // rb)
    scale = HEAD_DIM ** -0.5

    def body(q_ref, k_ref, v_ref, o_ref, lse_ref, kprev, vprev):
        n = pl.program_id(1)

        @pl.when(n == 0)
        def _():
            kprev[...] = jnp.zeros_like(kprev)
            vprev[...] = jnp.zeros_like(vprev)

        heads = [slice(hh * HEAD_DIM, (hh + 1) * HEAD_DIM) for hh in range(nh)]
        valids = [_band_masks(n)] + [_band_masks(1)] * (nsub - 1)

        def rows(sub):
            return slice(sub * SUB_BLOCK, (sub + 1) * SUB_BLOCK)

        def cat(prev, ref, sub, sl):
            if sub == 0:
                return jnp.concatenate([prev[:, sl], ref[rows(0), sl]], axis=0)
            return ref[(sub - 1) * SUB_BLOCK:(sub + 1) * SUB_BLOCK, sl]

        scs = [[_dot(q_ref[rows(sub), sl], cat(kprev, k_ref, sub, sl), NT) for sl in heads]
               for sub in range(nsub)]
        for sub in range(nsub):
            lses = []
            for hh, sl in enumerate(heads):
                sc = jnp.where(valids[sub], scs[sub][hh] * scale, NEG)
                mx = jnp.max(sc, axis=-1, keepdims=True)
                p = jnp.exp(sc - mx)
                l = jnp.sum(p, axis=-1, keepdims=True)
                o = _dot(p.astype(BF16), cat(vprev, v_ref, sub, sl), NN) / l
                o_ref[rows(sub), sl] = o.astype(BF16)
                lses.append(mx + jnp.log(l))
            lse_ref[rows(sub), :] = _pack_heads(lses)
        kprev[...] = k_ref[rows(nsub - 1), :]
        vprev[...] = v_ref[rows(nsub - 1), :]

    def cur(off):
        return pl.BlockSpec((rb, cw), lambda j, n: (n, j + off))

    return pl.pallas_call(
        body, name=name, grid=grid,
        in_specs=[cur(offs[0]), cur(offs[1]), cur(offs[2])],
        out_specs=[cur(0), pl.BlockSpec((rb, HEAD_DIM), lambda j, n: (n, j))],
        out_shape=[jax.ShapeDtypeStruct((m, width), BF16),
                   jax.ShapeDtypeStruct((m, width // cw * HEAD_DIM), F32)],
        scratch_shapes=[pltpu.VMEM((SUB_BLOCK, cw), BF16)] * 2,
        compiler_params=_params(("parallel", "arbitrary")),
    )(q, k, v)


def _view_spec(tm, aw, dil):
    return pl.BlockSpec((tm // dil, dil * aw), lambda i: (i, 0))


def _attn_combine(o1, l1, o_views, l_views):
    s, aw = o1.shape
    nh = aw // HEAD_DIM
    tm = min(s, 512)

    def body(o1_ref, l1_ref, o4_ref, o16_ref, l4_ref, l16_ref, o_ref, lse_ref, lse4_ref, lse16_ref,
             so4, so16, sl4, sl16, stot):
        _from_views((o4_ref, o16_ref), (so4, so16))
        _from_views((l4_ref, l16_ref), (sl4, sl16))
        a, b, c = l1_ref[...], sl4[0], sl16[0]
        mx = jnp.maximum(jnp.maximum(a, b), c)
        ea, eb, ec = jnp.exp(a - mx), jnp.exp(b - mx), jnp.exp(c - mx)
        tot = ea + eb + ec
        inv = 1.0 / tot
        wa, wb, wc = ea * inv, eb * inv, ec * inv
        lse_tot = mx + jnp.log(tot)
        stot[0] = lse_tot
        lse_ref[...] = lse_tot
        _to_views(stot, (lse4_ref, lse16_ref), F32)
        for hh in range(nh):
            sl = slice(hh * HEAD_DIM, (hh + 1) * HEAD_DIM)
            o = (_head_col(wa, hh, nh) * o1_ref[:, sl].astype(F32) + _head_col(wb, hh, nh) * so4[hh]
                 + _head_col(wc, hh, nh) * so16[hh])
            o_ref[:, sl] = o.astype(BF16)

    row = pl.BlockSpec((tm, aw), lambda i: (i, 0))
    stat = pl.BlockSpec((tm, HEAD_DIM), lambda i: (i, 0))
    views = [_view_spec(tm, aw, dil) for dil in DILATIONS[1:]]
    stat_views = [_view_spec(tm, HEAD_DIM, dil) for dil in DILATIONS[1:]]
    return pl.pallas_call(
        body, name="attn_combine", grid=(s // tm,), in_specs=[row, stat, *views, *stat_views],
        out_specs=[row, stat, *stat_views],
        out_shape=[jax.ShapeDtypeStruct((s, aw), BF16), jax.ShapeDtypeStruct((s, HEAD_DIM), F32)]
        + [jax.ShapeDtypeStruct(_view_shape(s, HEAD_DIM, dil), F32) for dil in DILATIONS[1:]],
        scratch_shapes=[_heads_scratch(tm, aw)] * 2 + [_heads_scratch(tm, HEAD_DIM)] * 3,
        compiler_params=_params(("parallel",)),
    )(o1, l1, *o_views, *l_views)


def _pool_counts(tm, rows, pgw, row0):
    t = lax.broadcasted_iota(jnp.int32, (rows, len(POOL_WINDOWS) * pgw), 0) + row0
    col = lax.broadcasted_iota(jnp.int32, (rows, len(POOL_WINDOWS) * pgw), 1)
    w = jnp.full((rows, len(POOL_WINDOWS) * pgw), POOL_WINDOWS[0], jnp.int32)
    for g in range(1, len(POOL_WINDOWS)):
        w = jnp.where(col >= g * pgw, POOL_WINDOWS[g], w)
    return jnp.minimum(t + 1, w).astype(F32)


def _window_sums(xs, direction, pgw):
    rows = xs.shape[0]
    acc = xs
    out = None
    col = lax.broadcasted_iota(jnp.int32, xs.shape, 1)
    for g, w in enumerate(POOL_WINDOWS):
        sh = w // 2
        acc = acc + pltpu.roll(acc, sh if direction > 0 else rows - sh, 0)
        out = acc if out is None else jnp.where(col >= g * pgw, acc, out)
    return out


def _pool_fwd(h, wp, scale, aw):
    s = h.shape[0]
    pw_ = aw
    pgw = pw_ // len(POOL_WINDOWS)
    tm = min(s, 512)
    hb = tm // POOL_HALO

    def body(u_ref, halo_ref, wp_ref, sc_ref, p_ref, pw_ref, y_ref):
        i = pl.program_id(0)
        u = u_ref[...].astype(F32)
        halo = halo_ref[...].astype(F32) * jnp.where(i > 0, 1.0, 0.0)
        xs = jnp.concatenate([halo, u], axis=0)
        sums = _window_sums(xs, +1, pgw)[POOL_HALO:]
        p = (sums / _pool_counts(tm, tm, pgw, i * tm) - u).astype(BF16)
        p_ref[...] = p
        sc = sc_ref[...]
        for g in range(len(POOL_WINDOWS)):
            sl = slice(g * pgw, (g + 1) * pgw)
            pw = _dot(p[:, sl], wp_ref[g], NN)
            pw_ref[:, sl] = pw.astype(BF16)
            y_ref[:, sl] = (pw * sc[:, sl]).astype(BF16)

    out = jax.ShapeDtypeStruct((s, pw_), BF16)
    row = pl.BlockSpec((tm, pw_), lambda i: (i, 0))
    return pl.pallas_call(
        body, name="pool_fwd", grid=(s // tm,),
        in_specs=[pl.BlockSpec((tm, pw_), lambda i: (i, 3)),
                  pl.BlockSpec((POOL_HALO, pw_), lambda i: (jnp.maximum(i * hb - 1, 0), 3)),
                  pl.BlockSpec(wp.shape, lambda i: (0, 0, 0)),
                  pl.BlockSpec((1, pw_), lambda i: (0, 0))],
        out_specs=[row, row, row], out_shape=[out, out, out],
        compiler_params=_params(("parallel",)),
    )(h, h, wp, scale)


def _branch_merge(o_attn, y, wba_st, wbp_st, h, aw, d):
    s = o_attn.shape[0]
    tn = wba_st.shape[2]
    tm = min(s, 1024)
    ga0 = 4 * aw // tn
    gp0 = (4 * aw + d) // tn

    def body(o_ref, y_ref, wa_ref, wp_ref, sga_ref, sgp_ref, ya_ref, yp_ref, mg_ref):
        ya = _dot(o_ref[...], wa_ref[...], NN)
        yp = _dot(y_ref[...], wp_ref[...], NN)
        ya_ref[...] = ya.astype(BF16)
        yp_ref[...] = yp.astype(BF16)
        mg_ref[...] = (sga_ref[...].astype(F32) * ya + sgp_ref[...].astype(F32) * yp).astype(BF16)

    out = jax.ShapeDtypeStruct((s, d), BF16)
    blk = pl.BlockSpec((tm, tn), lambda i, j: (i, j))
    return pl.pallas_call(
        body, name="branch_merge", grid=(s // tm, N_CHIPS),
        in_specs=[pl.BlockSpec((tm, aw), lambda i, j: (i, 0)),
                  pl.BlockSpec((tm, aw), lambda i, j: (i, 0)),
                  pl.BlockSpec((None, aw, tn), lambda i, j: (j, 0, 0)),
                  pl.BlockSpec((None, aw, tn), lambda i, j: (j, 0, 0)),
                  pl.BlockSpec((tm, tn), lambda i, j: (i, j + ga0)),
                  pl.BlockSpec((tm, tn), lambda i, j: (i, j + gp0))],
        out_specs=[blk, blk, blk], out_shape=[out, out, out],
        compiler_params=_params(("parallel", "parallel")),
    )(o_attn, y, wba_st, wbp_st, h, h)


def _mix_norm(merged, w_out, x, g1, b1):
    s, d = x.shape
    tm = min(s, 512)

    def epilogue(acc, ex, outs):
        x_ref, g_ref, b_ref = ex
        xh_ref, rs_ref, xb_ref = outs
        xhat, rstd = _layer_norm_fwd(ALPHA * x_ref[...] + acc)
        xh_ref[...] = xhat
        rs_ref[...] = rstd
        xb_ref[...] = (xhat * g_ref[...] + b_ref[...]).astype(BF16)

    row = pl.BlockSpec((tm, d), lambda i, j, k: (i, 0))
    vec = pl.BlockSpec((1, d), lambda i, j, k: (0, 0))
    return _mm(
        "mix_norm", (s // tm, 1, 1), merged, row, w_out, pl.BlockSpec((d, d), lambda i, j, k: (0, 0)),
        NN, [x, g1, b1], [row, vec, vec],
        [jax.ShapeDtypeStruct((s, d), F32), jax.ShapeDtypeStruct((s, 1), F32),
         jax.ShapeDtypeStruct((s, d), BF16)],
        [row, pl.BlockSpec((tm, 1), lambda i, j, k: (i, 0)), row], epilogue, None)


def _ff_up(x1b, w1_st):
    s, d = x1b.shape
    n_sh = w1_st.shape[2]
    tm, tn = min(s, 1024), min(n_sh, 1024)
    per = n_sh // tn

    def epilogue(acc, ex, outs):
        r = jnp.maximum(acc, 0.0)
        outs[0][...] = (r * r).astype(BF16)
        outs[1][...] = (2.0 * r).astype(BF16)

    blk = pl.BlockSpec((tm, tn), lambda i, j, k: (i, j))
    out = jax.ShapeDtypeStruct((s, N_CHIPS * n_sh), BF16)
    return _mm(
        "ff_up", (s // tm, N_CHIPS * per, 1), x1b, pl.BlockSpec((tm, d), lambda i, j, k: (i, 0)),
        w1_st, pl.BlockSpec((None, d, tn), lambda i, j, k: (j // per, 0, j % per)), NN, [], [],
        [out, out], [blk, blk], epilogue, None)


def _ff_down_loss(r, w2, xhat1, g1, b1, g2, b2, target):
    s, d = xhat1.shape
    dff = r.shape[1]
    tm, tk = min(s, 512), min(dff, 2048)
    ch = min(tm, EPILOGUE_ROWS)

    def epilogue(acc_ref, ex, outs):
        xh1_ref, g1_ref, b1_ref, g2_ref, b2_ref, t_ref = ex
        dz_ref, dzb_ref, st_ref = outs
        g1v, b1v, g2v, b2v = g1_ref[...], b1_ref[...], g2_ref[...], b2_ref[...]
        dg = db = loss = None
        for c in range(tm // ch):
            rows = slice(c * ch, (c + 1) * ch)
            x1 = xh1_ref[rows, :] * g1v + b1v
            xhat2, rstd2 = _layer_norm_fwd(ALPHA * x1 + acc_ref[rows, :])
            err = xhat2 * g2v + b2v - t_ref[rows, :]
            dy = err * (1.0 / d)
            dz = _layer_norm_bwd(dy, xhat2, rstd2, g2v)
            dz_ref[rows, :] = dz
            dzb_ref[rows, :] = dz.astype(BF16)
            parts = (jnp.sum(dy * xhat2, axis=0, keepdims=True), jnp.sum(dy, axis=0, keepdims=True),
                     jnp.sum(jnp.sum(err * err, axis=-1, keepdims=True), axis=0, keepdims=True))
            dg, db, loss = parts if c == 0 else (dg + parts[0], db + parts[1], loss + parts[2])
        st_ref[...] = _stats_rows([dg, db, jnp.broadcast_to((0.5 / d) * loss, (1, d))], d)

    row = pl.BlockSpec((tm, d), lambda i, j, k: (i, 0))
    vec = pl.BlockSpec((1, d), lambda i, j, k: (0, 0))
    return _mm(
        "ff_down_loss", (s // tm, 1, dff // tk), r, pl.BlockSpec((tm, tk), lambda i, j, k: (i, k)),
        w2, pl.BlockSpec((tk, d), lambda i, j, k: (k, 0)), NN,
        [xhat1, g1, b1, g2, b2, target], [row, vec, vec, vec, vec, row],
        [jax.ShapeDtypeStruct((s, d), F32), jax.ShapeDtypeStruct((s, d), BF16),
         jax.ShapeDtypeStruct((s // tm, 8, d), F32)],
        [row, row, pl.BlockSpec((None, 8, d), lambda i, j, k: (i, 0, 0))], epilogue, (tm, d),
        acc_as_ref=True)


def _ff_down_bwd(dz2b, w2, r_slope):
    s, d = dz2b.shape
    dff = r_slope.shape[1]
    tm, tn = min(s, 1024), min(dff, 1024)

    def epilogue(acc, ex, outs):
        outs[0][...] = (acc * ex[0][...].astype(F32)).astype(BF16)

    blk = pl.BlockSpec((tm, tn), lambda i, j, k: (i, j))
    return _mm(
        "ff_down_bwd", (s // tm, dff // tn, 1), dz2b, pl.BlockSpec((tm, d), lambda i, j, k: (i, 0)),
        w2, pl.BlockSpec((tn, d), lambda i, j, k: (j, 0)), NT, [r_slope], [blk],
        [jax.ShapeDtypeStruct((s, dff), BF16)], [blk], epilogue, None)[0]


def _wgrad(name, a, g, n_sh):
    s, rows = a.shape
    cols = g.shape[1]
    tm, tn, tk = min(rows, 2048), min(cols, 1024), min(s, 2048)
    if tn >= n_sh:
        span = tn // n_sh
        out_spec = pl.BlockSpec((span, tm, n_sh), lambda i, j, k: (j, i, 0))

        def epilogue(acc_ref, ex, outs):
            for sh in range(span):
                outs[0][sh] = acc_ref[:, sh * n_sh:(sh + 1) * n_sh].astype(BF16)
    else:
        per = n_sh // tn
        out_spec = pl.BlockSpec((None, tm, tn), lambda i, j, k: (j // per, i, j % per))

        def epilogue(acc_ref, ex, outs):
            outs[0][...] = acc_ref[...].astype(BF16)

    return _mm(
        name, (rows // tm, cols // tn, s // tk), a, pl.BlockSpec((tk, tm), lambda i, j, k: (k, i)),
        g, pl.BlockSpec((tk, tn), lambda i, j, k: (k, j)), TN, [], [],
        [jax.ShapeDtypeStruct((cols // n_sh, rows, n_sh), BF16)], [out_spec], epilogue,
        (tm, tn), acc_as_ref=True)[0]


def _ff_up_bwd(da, w1_st, dz2, xhat1, rstd1, g1, run_after):
    s, d = dz2.shape
    n_sh = w1_st.shape[2]
    tm, tk = min(s, 512), min(n_sh, 2048)
    per = n_sh // tk
    ch = min(tm, EPILOGUE_ROWS)

    def epilogue(acc_ref, ex, outs):
        dz2_ref, xh_ref, rs_ref, g_ref = ex
        dz_ref, dzb_ref, st_ref = outs
        gv = g_ref[...]
        dg = db = None
        for c in range(tm // ch):
            rows = slice(c * ch, (c + 1) * ch)
            dx1 = ALPHA * dz2_ref[rows, :] + acc_ref[rows, :]
            xhat = xh_ref[rows, :]
            dz = _layer_norm_bwd(dx1, xhat, rs_ref[rows, :], gv)
            dz_ref[rows, :] = dz
            dzb_ref[rows, :] = dz.astype(BF16)
            parts = (jnp.sum(dx1 * xhat, axis=0, keepdims=True), jnp.sum(dx1, axis=0, keepdims=True))
            dg, db = parts if c == 0 else (dg + parts[0], db + parts[1])
        st_ref[...] = _stats_rows([dg, db], d)

    row = pl.BlockSpec((tm, d), lambda i, j, k: (i, 0))
    return _mm(
        "ff_up_bwd", (s // tm, 1, N_CHIPS * per), da, pl.BlockSpec((tm, tk), lambda i, j, k: (i, k)),
        w1_st, pl.BlockSpec((None, d, tk), lambda i, j, k: (k // per, 0, k % per)), NT,
        [dz2, xhat1, rstd1, g1],
        [row, row, pl.BlockSpec((tm, 1), lambda i, j, k: (i, 0)), pl.BlockSpec((1, d), lambda i, j, k: (0, 0))],
        [jax.ShapeDtypeStruct((s, d), F32), jax.ShapeDtypeStruct((s, d), BF16),
         jax.ShapeDtypeStruct((s // tm, 8, d), F32)],
        [row, row, pl.BlockSpec((None, 8, d), lambda i, j, k: (i, 0, 0))], epilogue, (tm, d),
        acc_as_ref=True, run_after=run_after)


def _mix_bwd(dz1b, w_out, h, ya, yp, aw):
    s, d = dz1b.shape
    tm = min(s, 512)
    gblk = 4 * aw // d

    def epilogue(acc, ex, outs):
        sga_ref, sgp_ref, ya_ref, yp_ref = ex
        dya_ref, dyp_ref, dg_ref = outs
        sga, sgp = sga_ref[...].astype(F32), sgp_ref[...].astype(F32)
        dya_ref[...] = (acc * sga).astype(BF16)
        dyp_ref[...] = (acc * sgp).astype(BF16)
        dg_ref[:, :d] = (acc * ya_ref[...].astype(F32) * (sga * (1.0 - sga))).astype(BF16)
        dg_ref[:, d:] = (acc * yp_ref[...].astype(F32) * (sgp * (1.0 - sgp))).astype(BF16)

    row = pl.BlockSpec((tm, d), lambda i, j, k: (i, 0))
    return _mm(
        "mix_bwd", (s // tm, 1, 1), dz1b, row, w_out, pl.BlockSpec((d, d), lambda i, j, k: (0, 0)), NT,
        [h, h, ya, yp],
        [pl.BlockSpec((tm, d), lambda i, j, k: (i, gblk)),
         pl.BlockSpec((tm, d), lambda i, j, k: (i, gblk + 1)), row, row],
        [jax.ShapeDtypeStruct((s, d), BF16), jax.ShapeDtypeStruct((s, d), BF16),
         jax.ShapeDtypeStruct((s, 4 * d), BF16)],
        [row, row, pl.BlockSpec((tm, 2 * d), lambda i, j, k: (i, 1))], epilogue, None)


def _branch_in_bwd(name, tm, dyb, wb_st, epilogue, extras, extra_specs, out_shape, out_specs,
                   scratch=()):
    s, d = dyb.shape
    aw = wb_st.shape[1]
    wb_t = wb_st.transpose(0, 2, 1).reshape(d, aw)
    return _mm(
        name, (s // tm, 1, 1), dyb, pl.BlockSpec((tm, d), lambda i, j, k: (i, 0)),
        wb_t, pl.BlockSpec((d, aw), lambda i, j, k: (0, 0)), NN,
        extras, extra_specs, out_shape, out_specs, epilogue, None, scratch=scratch)


def _attn_out_bwd(dya, wba_st, o_attn):
    s, aw = o_attn.shape
    tm = min(s, 512)

    def epilogue(acc_ref, ex, outs, sdo, sdl):
        do_ref, dl_ref, do4_ref, do16_ref, dl4_ref, dl16_ref = outs
        deltas = []
        for hh in range(aw // HEAD_DIM):
            sl = slice(hh * HEAD_DIM, (hh + 1) * HEAD_DIM)
            do = acc_ref[:, sl]
            deltas.append(jnp.sum(do * ex[0][:, sl].astype(F32), axis=-1, keepdims=True))
            sdo[hh] = do
            do_ref[:, sl] = do.astype(BF16)
        packed = _pack_heads(deltas)
        sdl[0] = packed
        dl_ref[...] = packed
        _to_views(sdo, (do4_ref, do16_ref), BF16)
        _to_views(sdl, (dl4_ref, dl16_ref), F32)

    def specs(width):
        return ([pl.BlockSpec((tm, width), lambda i, j, k: (i, 0))]
                + [pl.BlockSpec((tm // dil, dil * width), lambda i, j, k: (i, 0)) for dil in DILATIONS[1:]])

    def shapes(width, dtype):
        return ([jax.ShapeDtypeStruct((s, width), dtype)]
                + [jax.ShapeDtypeStruct(_view_shape(s, width, dil), dtype) for dil in DILATIONS[1:]])

    do_specs, dl_specs = specs(aw), specs(HEAD_DIM)
    do_shapes, dl_shapes = shapes(aw, BF16), shapes(HEAD_DIM, F32)
    return _branch_in_bwd(
        "attn_out_bwd", tm, dya, wba_st, epilogue, [o_attn], [do_specs[0]],
        [do_shapes[0], dl_shapes[0], *do_shapes[1:], *dl_shapes[1:]],
        [do_specs[0], dl_specs[0], *do_specs[1:], *dl_specs[1:]],
        scratch=[_heads_scratch(tm, aw), _heads_scratch(tm, HEAD_DIM)])


def _pool_out_bwd(dyp, wbp_st, pw, scale):
    s, pw_ = pw.shape
    tm = min(s, 1024)

    def epilogue(acc_ref, ex, outs):
        pw_ref, sc_ref = ex
        dpw_ref, st_ref = outs
        acc = acc_ref[...]
        dpw_ref[...] = (acc * sc_ref[...]).astype(BF16)
        st_ref[...] = _stats_rows([jnp.sum(acc * pw_ref[...].astype(F32), axis=0, keepdims=True)], pw_)

    row = pl.BlockSpec((tm, pw_), lambda i, j, k: (i, 0))
    return _branch_in_bwd(
        "pool_out_bwd", tm, dyp, wbp_st, epilogue, [pw, scale],
        [row, pl.BlockSpec((1, pw_), lambda i, j, k: (0, 0))],
        [jax.ShapeDtypeStruct((s, pw_), BF16), jax.ShapeDtypeStruct((s // tm, 8, pw_), F32)],
        [row, pl.BlockSpec((None, 8, pw_), lambda i, j, k: (i, 0, 0))])


def _pool_bwd(dpw, p, wp):
    s, pw_ = p.shape
    ng = len(POOL_WINDOWS)
    pgw = pw_ // ng
    tm = min(s, 512)
    hb = tm // POOL_HALO
    nblk = s // tm

    def body(dpw_ref, nxt_ref, p_ref, wp_ref, dwp_ref, du_ref):
        i = pl.program_id(0)
        nxt = (nxt_ref[...].astype(F32) * jnp.where(i < nblk - 1, 1.0, 0.0)).astype(BF16)
        dpw_all = jnp.concatenate([dpw_ref[...], nxt], axis=0)

        @pl.when(i == 0)
        def _():
            dwp_ref[...] = jnp.zeros_like(dwp_ref)

        dps = []
        for g in range(ng):
            sl = slice(g * pgw, (g + 1) * pgw)
            dwp_ref[g] += _dot(p_ref[:, sl], dpw_ref[:, sl], TN)
            dps.append(_dot(dpw_all[:, sl], wp_ref[g], NT))
        dp = jnp.concatenate(dps, axis=1)
        dpn = dp / _pool_counts(tm, tm + POOL_HALO, pgw, i * tm)
        du_ref[...] = (_window_sums(dpn, -1, pgw)[:tm] - dp[:tm]).astype(BF16)

    row = pl.BlockSpec((tm, pw_), lambda i: (i, 0))
    full = pl.BlockSpec((ng, pgw, pgw), lambda i: (0, 0, 0))
    return pl.pallas_call(
        body, name="pool_bwd", grid=(nblk,),
        in_specs=[row, pl.BlockSpec((POOL_HALO, pw_), lambda i: (jnp.minimum((i + 1) * hb, s // POOL_HALO - 1), 0)),
                  row, full],
        out_specs=[full, row],
        out_shape=[jax.ShapeDtypeStruct((ng, pgw, pgw), F32), jax.ShapeDtypeStruct((s, pw_), BF16)],
        compiler_params=_params(("arbitrary",)),
    )(dpw, dpw, p, wp)


def _attn_bwd(name, q, k, v, do, lse, delta, offs, cw, width):
    m = do.shape[0]
    nh = cw // HEAD_DIM
    rb = min(m, ATTN_STEP_ROWS)
    nsub = rb // SUB_BLOCK
    nstep = m // rb
    single = nstep == 1
    grid = (width // cw, nstep + (not single))
    scale = HEAD_DIM ** -0.5
    last = slice(rb - SUB_BLOCK, rb)

    def body(q_ref, k_ref, v_ref, do_ref, lse_ref, dl_ref, dq_ref, dk_ref, dv_ref,
             kprev, vprev, dk_carry, dv_carry):
        n = pl.program_id(1)

        @pl.when(n == 0)
        def _():
            for ref in (kprev, vprev, dk_carry, dv_carry):
                ref[...] = jnp.zeros_like(ref)

        @pl.when(n < nstep)
        def _():
            qi = lax.broadcasted_iota(jnp.int32, (SUB_BLOCK, 2 * SUB_BLOCK), 0)
            kj = lax.broadcasted_iota(jnp.int32, (SUB_BLOCK, 2 * SUB_BLOCK), 1)
            band = (kj >= qi) & (kj <= qi + SUB_BLOCK)
            valids = [band & (kj >= jnp.where(n == 0, SUB_BLOCK, 0))] + [band] * (nsub - 1)
            heads = [slice(hh * HEAD_DIM, (hh + 1) * HEAD_DIM) for hh in range(nh)]

            def rows(sub):
                return slice(sub * SUB_BLOCK, (sub + 1) * SUB_BLOCK)

            def cat(prev, ref, sub, sl):
                if sub == 0:
                    return jnp.concatenate([prev[:, sl], ref[rows(0), sl]], axis=0)
                return ref[(sub - 1) * SUB_BLOCK:(sub + 1) * SUB_BLOCK, sl]

            kcats = [[cat(kprev, k_ref, sub, sl) for sl in heads] for sub in range(nsub)]
            scs = [[_dot(q_ref[rows(sub), sl], kcats[sub][hh], NT) for hh, sl in enumerate(heads)]
                   for sub in range(nsub)]
            dps = [[_dot(do_ref[rows(sub), sl], cat(vprev, v_ref, sub, sl), NT) for sl in heads]
                   for sub in range(nsub)]
            stats = [(lse_ref[rows(sub), :], dl_ref[rows(sub), :]) for sub in range(nsub)]
            for hh, sl in enumerate(heads):
                dk2, dv2 = [], []
                for sub in range(nsub):
                    lse_h, dl_h = _head_col(stats[sub][0], hh, nh), _head_col(stats[sub][1], hh, nh)
                    valid = valids[sub]
                    p = jnp.where(valid, jnp.exp(jnp.where(valid, scs[sub][hh] * scale, NEG) - lse_h), 0.0)
                    ds = (p * (dps[sub][hh] - dl_h)).astype(BF16)
                    dq_ref[rows(sub), sl] = (_dot(ds, kcats[sub][hh], NN) * scale).astype(BF16)
                    dk2.append(_dot(ds, q_ref[rows(sub), sl], TN) * scale)
                    dv2.append(_dot(p.astype(BF16), do_ref[rows(sub), sl], TN))
                for out_ref, carry, new in ((dk_ref, dk_carry, dk2), (dv_ref, dv_carry, dv2)):
                    if not single:
                        out_ref[last, sl] = (carry[last, sl] + new[0][:SUB_BLOCK]).astype(BF16)
                        if nsub > 1:
                            out_ref[:rb - SUB_BLOCK, sl] = carry[:rb - SUB_BLOCK, sl].astype(BF16)
                    for sub in range(nsub):
                        val = new[sub][SUB_BLOCK:]
                        if sub + 1 < nsub:
                            val = val + new[sub + 1][:SUB_BLOCK]
                        if single:
                            out_ref[rows(sub), sl] = val.astype(BF16)
                        else:
                            carry[rows(sub), sl] = val
            kprev[...] = k_ref[last, :]
            vprev[...] = v_ref[last, :]

        if not single:
            @pl.when(n == nstep)
            def _():
                dk_ref[...] = dk_carry[...].astype(BF16)
                dv_ref[...] = dv_carry[...].astype(BF16)

    def cur(off):
        return pl.BlockSpec((rb, cw), lambda j, n: (jnp.minimum(n, nstep - 1), j + off))

    lagged = pl.BlockSpec((rb, cw), lambda j, n: (jnp.maximum(n - 1, 0), j))
    stat = pl.BlockSpec((rb, HEAD_DIM), lambda j, n: (jnp.minimum(n, nstep - 1), j))
    out = jax.ShapeDtypeStruct((m, width), BF16)
    return pl.pallas_call(
        body, name=name, grid=grid,
        in_specs=[cur(offs[0]), cur(offs[1]), cur(offs[2]), cur(0), stat, stat],
        out_specs=[cur(0), lagged, lagged], out_shape=[out, out, out],
        scratch_shapes=[pltpu.VMEM((SUB_BLOCK, cw), BF16)] * 2 + [pltpu.VMEM((rb, cw), F32)] * 2,
        compiler_params=_params(("parallel", "arbitrary")),
    )(q, k, v, do, lse, delta)


def _qkvu_grad(d1, d4, d16, du, cos_t, sin_t, dh):
    s, aw = du.shape
    tm = min(s, 512)

    def body(*refs):
        nat, v4, v16 = refs[0:3], refs[3:6], refs[6:9]
        cos_ref, sin_ref, du_ref, _, out_ref, s4, s16 = refs[9:]
        cos, sin = cos_ref[...], sin_ref[...]
        for part in range(3):
            _from_views((v4[part], v16[part]), (s4, s16))
            for hh in range(aw // HEAD_DIM):
                sl = slice(hh * HEAD_DIM, (hh + 1) * HEAD_DIM)
                t = nat[part][:, sl].astype(F32) + s4[hh] + s16[hh]
                if part < 2:
                    t = t * cos - pltpu.roll(t, HEAD_DIM // 2, 1) * sin
                out_ref[:, part * aw + hh * HEAD_DIM:part * aw + (hh + 1) * HEAD_DIM] = t.astype(BF16)
        out_ref[:, 3 * aw:] = du_ref[...]

    row = pl.BlockSpec((tm, aw), lambda i: (i, 0))
    tab = pl.BlockSpec((tm, HEAD_DIM), lambda i: (i, 0))
    return pl.pallas_call(
        body, name="qkvu_grad", grid=(s // tm,),
        in_specs=[row] * 3 + [_view_spec(tm, aw, 4)] * 3 + [_view_spec(tm, aw, 16)] * 3
        + [tab, tab, row, ANY_SPEC],
        out_specs=pl.BlockSpec((tm, 4 * aw), lambda i: (i, 0)),
        out_shape=jax.ShapeDtypeStruct(dh.shape, BF16), input_output_aliases={12: 0},
        scratch_shapes=[_heads_scratch(tm, aw)] * 2,
        compiler_params=_params(("parallel",)),
    )(*d1, *d4, *d16, cos_t, sin_t, du, dh)


def _in_proj_bwd_x(name, dh, w_in_st, base, scale_base, run_after=()):
    s, kdim = dh.shape
    d, n_sh = w_in_st.shape[1], w_in_st.shape[2]
    tm, tk = min(s, 512), min(n_sh, 2048)
    per = n_sh // tk

    ch = min(tm, 2 * EPILOGUE_ROWS)

    def epilogue(acc_ref, ex, outs):
        for c in range(tm // ch):
            rows = slice(c * ch, (c + 1) * ch)
            outs[0][rows, :] = scale_base * ex[0][rows, :] + acc_ref[rows, :]

    row = pl.BlockSpec((tm, d), lambda i, j, k: (i, 0))
    return _mm(
        name, (s // tm, 1, kdim // tk), dh, pl.BlockSpec((tm, tk), lambda i, j, k: (i, k)),
        w_in_st, pl.BlockSpec((None, d, tk), lambda i, j, k: (k // per, 0, k % per)), NT,
        [base], [row], [jax.ShapeDtypeStruct((s, d), F32)], [row], epilogue, (tm, d),
        acc_as_ref=True, run_after=run_after)[0]


def _chip_peers():
    x, y, c = lax.axis_index("x"), lax.axis_index("y"), lax.axis_index("c")
    return x, y, c, [(1 - x, y), (x, 1 - y), (1 - x, 1 - y)]


GATHER, GATHER_HALF, SCATTER, SIBLING = "gather", "gather_half", "scatter", "sibling"


def _exchange_peers(mode):
    x, y, c, chips = _chip_peers()
    if mode == SIBLING:
        return x, y, c, [(x, y, 1 - c)]
    return x, y, c, [(px, py, c) for px, py in chips]


def _core_half(ref_or_shape, c):
    rows = (ref_or_shape.shape[0]) // 2
    return pl.ds(c * rows, rows)


def _exchange_descriptor(mode, src, land, send, recv, p, peer, me, arriving):
    pid = 2 * peer[0] + peer[1]
    if mode == GATHER:
        src_ref, dst_ref = src, land.at[pid if arriving else me]
    elif mode == GATHER_HALF:
        rows = _core_half(src, peer[2])
        src_ref, dst_ref = src.at[rows], land.at[pid if arriving else me, rows]
    elif mode == SCATTER:
        src_ref, dst_ref = src.at[pid], land.at[p]
    else:
        src_ref, dst_ref = src, land
    return pltpu.make_async_remote_copy(
        src_ref=src_ref, dst_ref=dst_ref, send_sem=send.at[p], recv_sem=recv.at[p],
        device_id=peer, device_id_type=MESH)


def _exchange_start(name, mode, srcs, land_shapes):
    n = len(srcs)
    lands = [_hbm(lax.empty(shape, src.dtype)) for shape, src in zip(land_shapes, srcs)]

    def body(*refs):
        src_refs, land_refs = refs[:n], refs[n:2 * n]
        sends, recvs = refs[2 * n:3 * n], refs[3 * n:4 * n]
        token = refs[6 * n]
        x, y, c, peers = _exchange_peers(mode)
        me = 2 * x + y
        for w in range(n):
            for p, peer in enumerate(peers):
                _exchange_descriptor(mode, src_refs[w], land_refs[w], sends[w], recvs[w], p, peer,
                                     me, arriving=False).start()
        token[...] = jnp.zeros_like(token)

    sem = pltpu.SemaphoreType.DMA((3,))
    outs = pl.pallas_call(
        body, name=name, in_specs=[HBM_SPEC] * (2 * n),
        out_specs=[SEM_SPEC] * (2 * n) + [HBM_SPEC] * (2 * n) + [pl.BlockSpec(memory_space=pltpu.VMEM)],
        out_shape=[sem] * (2 * n) + [pltpu.HBM(a.shape, a.dtype) for a in (*srcs, *lands)]
        + [jax.ShapeDtypeStruct((8, 128), F32)],
        input_output_aliases={i: 2 * n + i for i in range(2 * n)},
        compiler_params=pltpu.CompilerParams(has_side_effects=DATAFLOW),
    )(*[_hbm(a) for a in srcs], *lands)
    return {"send": outs[:n], "recv": outs[n:2 * n], "src": outs[2 * n:3 * n],
            "land": outs[3 * n:4 * n], "token": outs[4 * n]}


def _exchange_wait(name, mode, started, which, after):
    m = len(which)

    def body(*refs):
        src_refs, land_refs = refs[:m], refs[m:2 * m]
        sends, recvs = refs[2 * m:3 * m], refs[3 * m:4 * m]
        x, y, c, peers = _exchange_peers(mode)
        me = 2 * x + y
        for w in range(m):
            for p, peer in enumerate(peers):
                _exchange_descriptor(mode, src_refs[w], land_refs[w], sends[w], recvs[w], p, peer,
                                     me, arriving=False).wait_send()
                _exchange_descriptor(mode, src_refs[w], land_refs[w], sends[w], recvs[w], p, peer,
                                     me, arriving=True).wait_recv()

    pick = lambda key: [started[key][w] for w in which]
    bufs = pick("src") + pick("land")
    after = list(after) if isinstance(after, (list, tuple)) else [after]
    outs = pl.pallas_call(
        body, name=name,
        in_specs=[HBM_SPEC] * (2 * m) + [SEM_SPEC] * (2 * m) + [ANY_SPEC] * len(after),
        out_specs=[HBM_SPEC] * (2 * m), out_shape=[pltpu.HBM(a.shape, a.dtype) for a in bufs],
        input_output_aliases={i: i for i in range(2 * m)},
        compiler_params=pltpu.CompilerParams(has_side_effects=DATAFLOW),
    )(*bufs, *pick("send"), *pick("recv"), *after)
    return outs[:m], outs[m:]


def _to_bf16(name, a, run_after):
    r, c = a.shape
    tm = min(r, 512)

    def body(a_ref, after_ref, out_ref):
        out_ref[...] = a_ref[...].astype(BF16)

    blk = pl.BlockSpec((tm, c), lambda i: (i, 0))
    return pl.pallas_call(
        body, name=name, grid=(r // tm,), in_specs=[blk, ANY_SPEC], out_specs=blk,
        out_shape=jax.ShapeDtypeStruct((r, c), BF16), compiler_params=_params(("parallel",)),
    )(a, run_after)


def _swap_halves(name, lands):
    n = len(lands)

    def body(*refs):
        bufs = refs[n:2 * n]
        send, recv = refs[2 * n:]
        x, y, c, chips = _chip_peers()
        started = []
        for w in range(n):
            half = bufs[w].shape[1] // 2
            for p, (px, py) in enumerate(chips):
                mine = bufs[w].at[2 * px + py, pl.ds(c * half, half)]
                cp = pltpu.make_async_remote_copy(
                    src_ref=mine, dst_ref=mine, send_sem=send.at[w, p], recv_sem=recv.at[w, p],
                    device_id=(x, y, 1 - c), device_id_type=MESH)
                cp.start()
                started.append(cp)
        for w in range(n):
            half = bufs[w].shape[1] // 2
            for p, (px, py) in enumerate(chips):
                theirs = bufs[w].at[2 * px + py, pl.ds((1 - c) * half, half)]
                pltpu.make_async_remote_copy(
                    src_ref=theirs, dst_ref=theirs, send_sem=send.at[w, p], recv_sem=recv.at[w, p],
                    device_id=(x, y, 1 - c), device_id_type=MESH).wait_recv()
        for cp in started:
            cp.wait_send()

    return pl.pallas_call(
        body, name=name, in_specs=[ANY_SPEC] * n, out_specs=[ANY_SPEC] * n,
        out_shape=[jax.ShapeDtypeStruct(a.shape, a.dtype) for a in lands],
        input_output_aliases={i: i for i in range(n)},
        scratch_shapes=[pltpu.SemaphoreType.DMA((n, 3)), pltpu.SemaphoreType.DMA((n, 3))],
    )(*lands)


def _place_own(name, shard, land, me):
    r, c = shard.shape
    tm = min(r, 512)

    def body(me_ref, shard_ref, land_ref, out_ref):
        out_ref[...] = shard_ref[...]

    return pl.pallas_call(
        body, name=name,
        grid_spec=pltpu.PrefetchScalarGridSpec(
            num_scalar_prefetch=1, grid=(r // tm,),
            in_specs=[pl.BlockSpec((tm, c), lambda i, me_ref: (i, 0)), ANY_SPEC],
            out_specs=pl.BlockSpec((None, tm, c), lambda i, me_ref: (me_ref[0], i, 0))),
        out_shape=jax.ShapeDtypeStruct(land.shape, land.dtype), input_output_aliases={2: 0},
        compiler_params=_params(("arbitrary",)),
    )(me, shard, land)


def _sum_slabs(name, grads, land, me):
    _, r, c = grads.shape
    tm = min(r, 256)

    def body(me_ref, own_ref, land_ref, out_ref):
        acc = own_ref[...].astype(F32)
        for p in range(3):
            acc = acc + land_ref[p].astype(F32)
        out_ref[...] = acc

    return pl.pallas_call(
        body, name=name,
        grid_spec=pltpu.PrefetchScalarGridSpec(
            num_scalar_prefetch=1, grid=(r // tm,),
            in_specs=[pl.BlockSpec((None, tm, c), lambda i, me_ref: (me_ref[0], i, 0)),
                      pl.BlockSpec((3, tm, c), lambda i, me_ref: (0, i, 0))],
            out_specs=pl.BlockSpec((tm, c), lambda i, me_ref: (i, 0))),
        out_shape=jax.ShapeDtypeStruct((r, c), F32), compiler_params=_params(("parallel",)),
    )(me, grads, land)


def _allreduce_stats(stats, run_after):
    n = len(stats)

    def body(*refs):
        ins, outs = refs[:n], refs[n + 1:2 * n + 1]
        mine, gath = refs[2 * n + 1:3 * n + 1], refs[3 * n + 1:4 * n + 1]
        send, recv = refs[4 * n + 1:]
        x, y, c = lax.axis_index("x"), lax.axis_index("y"), lax.axis_index("c")
        me = 4 * x + 2 * y + c
        flips = [(bx, by, bc) for bx in (0, 1) for by in (0, 1) for bc in (0, 1)][1:]

        def peer(f):
            return (x + f[0] * (1 - 2 * x), y + f[1] * (1 - 2 * y), c + f[2] * (1 - 2 * c))

        copies = []
        for t in range(n):
            tot = ins[t][0]
            for b in range(1, ins[t].shape[0]):
                tot = tot + ins[t][b]
            mine[t][...] = tot
            gath[t][me] = tot
            for k, f in enumerate(flips):
                cp = pltpu.make_async_remote_copy(
                    src_ref=mine[t], dst_ref=gath[t].at[me], send_sem=send.at[t, k],
                    recv_sem=recv.at[t, k], device_id=peer(f), device_id_type=MESH)
                cp.start()
                copies.append(cp)
        for t in range(n):
            for k, f in enumerate(flips):
                px, py, pc = peer(f)
                pltpu.make_async_remote_copy(
                    src_ref=mine[t], dst_ref=gath[t].at[4 * px + 2 * py + pc], send_sem=send.at[t, k],
                    recv_sem=recv.at[t, k], device_id=(px, py, pc), device_id_type=MESH).wait_recv()
        for cp in copies:
            cp.wait_send()
        for t in range(n):
            tot = gath[t][0]
            for dev in range(1, 8):
                tot = tot + gath[t][dev]
            outs[t][...] = tot

    vm = pl.BlockSpec(memory_space=pltpu.VMEM)
    return pl.pallas_call(
        body, name="allreduce_stats", in_specs=[vm] * n + [ANY_SPEC], out_specs=[vm] * n,
        out_shape=[jax.ShapeDtypeStruct(s.shape[1:], F32) for s in stats],
        scratch_shapes=[pltpu.VMEM(s.shape[1:], F32) for s in stats]
        + [pltpu.VMEM((8, *s.shape[1:]), F32) for s in stats]
        + [pltpu.SemaphoreType.DMA((n, 7)), pltpu.SemaphoreType.DMA((n, 7))],
    )(*stats, run_after)


def _adamw(name, w, m, v, g_parts):
    r, c = w.shape
    tm = min(r, 256)
    n_g = len(g_parts)

    def body(*refs):
        w_ref, m_ref, v_ref = refs[:3]
        g_refs = refs[3:3 + n_g]
        g_out, d_out, m_out, v_out = refs[3 + n_g:]
        g = g_refs[0][...]
        for gr in g_refs[1:]:
            g = g + gr[...]
        m_new = ADAM_B1 * m_ref[...] + (1.0 - ADAM_B1) * g
        v_new = ADAM_B2 * v_ref[...] + (1.0 - ADAM_B2) * (g * g)
        m_hat = m_new / (1.0 - ADAM_B1 ** ADAM_STEP)
        v_hat = v_new / (1.0 - ADAM_B2 ** ADAM_STEP)
        g_out[...] = g
        d_out[...] = -ADAM_LR * (m_hat / (jnp.sqrt(v_hat) + ADAM_EPS) + ADAM_WD * w_ref[...])
        m_out[...] = m_new
        v_out[...] = v_new

    blk = pl.BlockSpec((tm, c), lambda i: (i, 0))
    out = jax.ShapeDtypeStruct((r, c), F32)
    return pl.pallas_call(
        body, name=name, grid=(r // tm,), in_specs=[blk] * (3 + n_g), out_specs=[blk] * 4,
        out_shape=[out] * 4, compiler_params=_params(("parallel",)),
    )(w, m, v, *g_parts)


def _rope_tables(positions):
    half = HEAD_DIM // 2
    inv_freq = ROPE_THETA ** (-jnp.arange(half, dtype=F32) / half)
    ang = positions.astype(F32)[0, :, None] * inv_freq
    cos, sin = jnp.cos(ang), jnp.sin(ang)
    return jnp.concatenate([cos, cos], axis=-1), jnp.concatenate([-sin, sin], axis=-1)


def kernel(x, positions, w_in, w_pool, pool_scale, w_branch_attn, w_branch_pool, w_out, ln_mix_g, ln_mix_b, w_ff1, w_ff2, ln_ff_g, ln_ff_b, loss_target, m_w_in, m_w_pool, m_pool_scale, m_w_branch_attn, m_w_branch_pool, m_w_out, m_ln_mix_g, m_ln_mix_b, m_w_ff1, m_w_ff2, m_ln_ff_g, m_ln_ff_b, v_w_in, v_w_pool, v_pool_scale, v_w_branch_attn, v_w_branch_pool, v_w_out, v_ln_mix_g, v_ln_mix_b, v_w_ff1, v_w_ff2, v_ln_ff_g, v_ln_ff_b):
    s, d = x.shape[1], x.shape[2]
    aw = d // 2
    ng = len(POOL_WINDOWS)
    pgw = aw // ng
    x2d, target = x[0], loss_target[0]
    xb = x2d.astype(BF16)
    cos_t, sin_t = _rope_tables(positions)

    big = {"w_in": w_in[0], "w_pool": w_pool[0].reshape(-1, pgw), "w_branch_attn": w_branch_attn[0],
           "w_branch_pool": w_branch_pool[0], "w_out": w_out[0], "w_ff1": w_ff1[0], "w_ff2": w_ff2[0]}
    names = list(big)
    me_chip = (2 * lax.axis_index("x") + lax.axis_index("y")).astype(jnp.int32).reshape(1)
    land_shapes = [(N_CHIPS, *big[k].shape) for k in names]
    gathering_in = _exchange_start("gather_start_in", GATHER_HALF,
                                   [_to_bf16("to_bf16_w_in", big["w_in"], positions)], land_shapes[:1])
    shards = [_to_bf16(f"to_bf16_{k}", big[k], gathering_in["token"]) for k in names[1:]]
    gathering = _exchange_start("gather_start", GATHER, shards, land_shapes[1:])

    def gathered(name, which, after):
        srcs, lands = _exchange_wait(f"gather_wait_{name}", GATHER, gathering, which, after)
        return [_place_own(f"place_own_{names[w + 1]}", srcs[i], lands[i], me_chip)
                for i, w in enumerate(which)]

    rows_sh = pgw // N_CHIPS
    dff = N_CHIPS * big["w_ff2"].shape[0]

    srcs, lands = _exchange_wait("gather_wait_in", GATHER_HALF, gathering_in, [0],
                                 [gathering["token"], xb, cos_t, sin_t])
    w_in_st = _place_own("place_own_w_in", srcs[0], _swap_halves("swap_halves_in", lands)[0], me_chip)
    h, hv4, hv16 = _in_proj(xb, w_in_st, cos_t, sin_t, aw)
    (wp_st,) = gathered("pool", [0], h)
    wp = wp_st.reshape(N_CHIPS, ng, rows_sh, pgw).transpose(1, 0, 2, 3).reshape(ng, pgw, pgw)
    qkv = {1: (h, h, h), 4: (hv4, hv4, hv4), 16: (hv16, hv16, hv16)}
    offs = {dil: (0, dil, 2 * dil) for dil in DILATIONS}
    o_parts, lse_parts = [], []
    for dil in DILATIONS:
        o_p, lse_p = _attn_fwd(f"attn_fwd_d{dil}", *qkv[dil], offs[dil], aw, dil * aw)
        o_parts.append(o_p)
        lse_parts.append(lse_p)
    o_attn, lse, lse4, lse16 = _attn_combine(o_parts[0], lse_parts[0], o_parts[1:], lse_parts[1:])
    p, pw, y = _pool_fwd(h, wp, pool_scale, aw)
    wba_st, wbp_st, w_out_st = gathered("mix", [1, 2, 3], y)
    w_out_full = w_out_st.reshape(d, d)
    ya, yp, merged = _branch_merge(o_attn, y, wba_st, wbp_st, h, aw, d)
    xhat1, rstd1, x1b = _mix_norm(merged, w_out_full, x2d, ln_mix_g, ln_mix_b)
    w1_st, w2_st = gathered("ff", [4, 5], x1b)
    w2_full = w2_st.reshape(dff, d)
    r, r_slope = _ff_up(x1b, w1_st)
    dz2, dz2b, st2 = _ff_down_loss(r, w2_full, xhat1, ln_mix_g, ln_mix_b, ln_ff_g, ln_ff_b, target)

    def scatter_start(name, grads):
        return _exchange_start(f"scatter_start_{name}", SCATTER, grads, [(3, *g.shape[1:]) for g in grads])

    da = _ff_down_bwd(dz2b, w2_full, r_slope)
    g_w2 = _wgrad("wgrad_ff2", r, dz2b, d).reshape(N_CHIPS, dff // N_CHIPS, d)
    g_w1 = _wgrad("wgrad_ff1", x1b, da, dff // N_CHIPS)
    sent_ff = scatter_start("ff", [g_w1, g_w2])
    dz1, dz1b, st1 = _ff_up_bwd(da, w1_st, dz2, xhat1, rstd1, ln_mix_g, [sent_ff["token"]])
    dya, dyp, dh_gates = _mix_bwd(dz1b, w_out_full, h, ya, yp, aw)
    g_wout = _wgrad("wgrad_out", merged, dz1b, d).reshape(N_CHIPS, d // N_CHIPS, d)
    g_wba = _wgrad("wgrad_branch_attn", o_attn, dya, d // N_CHIPS)
    g_wbp = _wgrad("wgrad_branch_pool", y, dyp, d // N_CHIPS)
    do, delta, do4, do16, delta4, delta16 = _attn_out_bwd(dya, wba_st, o_attn)
    dpw, stp = _pool_out_bwd(dyp, wbp_st, pw, pool_scale)
    dwp, du = _pool_bwd(dpw, p, wp)
    g_wp = dwp.reshape(ng, N_CHIPS, rows_sh, pgw).transpose(1, 0, 2, 3).reshape(
        N_CHIPS, ng * rows_sh, pgw).astype(BF16)
    sent_mix = scatter_start("mix", [g_wp, g_wba, g_wbp, g_wout])

    bwd_in = {1: (do, lse, delta), 4: (do4, lse4, delta4), 16: (do16, lse16, delta16)}
    dqkv = {}
    for dil in DILATIONS:
        args = (*qkv[dil], *bwd_in[dil], offs[dil], aw, dil * aw)
        dqkv[dil] = _attn_bwd(f"attn_bwd_d{dil}", *args)
    dh = _qkvu_grad(dqkv[1], dqkv[4], dqkv[16], du, cos_t, sin_t, dh_gates)
    g_win = _wgrad("wgrad_in", xb, dh, d)
    sent_in = scatter_start("in", [g_win])
    grad_x = _in_proj_bwd_x("in_proj_bwd", dh, w_in_st, dz1, ALPHA, [sent_mix["token"], sent_in["token"]])

    moments = {"w_in": (m_w_in, v_w_in), "w_pool": (m_w_pool, v_w_pool),
               "w_branch_attn": (m_w_branch_attn, v_w_branch_attn),
               "w_branch_pool": (m_w_branch_pool, v_w_branch_pool), "w_out": (m_w_out, v_w_out),
               "w_ff1": (m_w_ff1, v_w_ff1), "w_ff2": (m_w_ff2, v_w_ff2)}
    originals = {"w_in": w_in, "w_pool": w_pool, "w_branch_attn": w_branch_attn,
                 "w_branch_pool": w_branch_pool, "w_out": w_out, "w_ff1": w_ff1, "w_ff2": w_ff2}
    res = {}

    def summed(name, sent, keys, after):
        srcs, lands = _exchange_wait(f"scatter_wait_{name}", SCATTER, sent, list(range(len(keys))), after)
        parts = [_sum_slabs(f"sum_slabs_{k}", srcs[i], lands[i], me_chip) for i, k in enumerate(keys)]
        return _exchange_start(f"cores_start_{name}", SIBLING, parts, [a.shape for a in parts])

    def updated(name, swapping, keys, after):
        mine, other = _exchange_wait(f"cores_wait_{name}", SIBLING, swapping, list(range(len(keys))), after)
        for i, k in enumerate(keys):
            mk, vk = (a.reshape(big[k].shape) for a in moments[k])
            outs = _adamw(f"adamw_{k}", big[k], mk, vk, [mine[i], other[i]])
            res[k] = [o.reshape(originals[k].shape) for o in outs]

    groups = {"ff": ["w_ff1", "w_ff2"], "mix": ["w_pool", "w_branch_attn", "w_branch_pool", "w_out"],
              "in": ["w_in"]}
    swap_ff = summed("ff", sent_ff, groups["ff"], grad_x)
    swap_mix = summed("mix", sent_mix, groups["mix"], swap_ff["token"])
    swap_in = summed("in", sent_in, groups["in"], swap_mix["token"])
    updated("ff", swap_ff, groups["ff"], swap_in["token"])
    updated("mix", swap_mix, groups["mix"], res["w_ff2"][0])
    updated("in", swap_in, groups["in"], res["w_out"][0])
    tot2, tot1, totp = _allreduce_stats([st2, st1, stp], res["w_in"][0])

    def pad_d(a):
        return jnp.pad(a, ((0, 0), (0, d - a.shape[1])))

    small = ["ln_mix_g", "ln_mix_b", "ln_ff_g", "ln_ff_b", "pool_scale"]
    small_w = {"ln_mix_g": ln_mix_g, "ln_mix_b": ln_mix_b, "ln_ff_g": ln_ff_g, "ln_ff_b": ln_ff_b,
               "pool_scale": pool_scale}
    small_m = {"ln_mix_g": m_ln_mix_g, "ln_mix_b": m_ln_mix_b, "ln_ff_g": m_ln_ff_g,
               "ln_ff_b": m_ln_ff_b, "pool_scale": m_pool_scale}
    small_v = {"ln_mix_g": v_ln_mix_g, "ln_mix_b": v_ln_mix_b, "ln_ff_g": v_ln_ff_g,
               "ln_ff_b": v_ln_ff_b, "pool_scale": v_pool_scale}
    small_g = [tot1[0:1], tot1[1:2], tot2[0:1], tot2[1:2], pad_d(totp[0:1])]

    def pack(rows):
        return jnp.concatenate([pad_d(a) for a in rows] + [jnp.zeros((8 - len(rows), d), F32)], axis=0)

    outs = _adamw("adamw_small", pack([small_w[k] for k in small]), pack([small_m[k] for k in small]),
                  pack([small_v[k] for k in small]), [pack(small_g)])
    for i, k in enumerate(small):
        res[k] = [o[i:i + 1, :small_w[k].shape[1]] for o in outs]
    loss = tot2[2, 0]

    order = ["w_in", "w_pool", "pool_scale", "w_branch_attn", "w_branch_pool", "w_out", "ln_mix_g",
             "ln_mix_b", "w_ff1", "w_ff2", "ln_ff_g", "ln_ff_b"]
    result = [loss, grad_x[None]]
    for idx in range(4):
        result += [res[k][idx] for k in order]
    return tuple(result)
```
